```python
import jax, jax.numpy as jnp
from jax import lax
import numpy as np

D_MODEL = 2048
BATCH = 8
SEQ = 4096
DEPTH = 1

HEAD_DIM = 128
N_FOX = 6
N_SB = 6
N_MEM = 4
N_MEM_TOK = 256
D_FF = 5632
CONV_W = 3
BLOCK_Q = 128
N_BRANCH = 3
EPS = 1e-6

FOX_W = N_FOX * HEAD_DIM
SB_W = N_SB * HEAD_DIM
MEM_W = N_MEM * HEAD_DIM
IN_COLS = 3 * FOX_W + N_FOX + 3 * SB_W + MEM_W + N_BRANCH * D_MODEL

kernel_name = 'fox_stickbreak_memory_gated_hybrid'


def rms_norm(t, g):
    tf = t.astype(jnp.float32)
    y = tf * lax.rsqrt(jnp.mean(tf * tf, axis=-1, keepdims=True) + EPS)
    return (y * g.astype(jnp.float32)).astype(t.dtype)


def split_heads(t, n):
    b, s, _ = t.shape
    return t.reshape(b, s, n, HEAD_DIM).transpose(0, 2, 1, 3)


def merge_heads(t):
    b, h, s, d = t.shape
    return t.transpose(0, 2, 1, 3).reshape(b, s, h * d)


def to_blocks(t):
    b, h, s = t.shape[:3]
    nb = s // BLOCK_Q
    t = t.reshape((b, h, nb, BLOCK_Q) + t.shape[3:])
    return jnp.moveaxis(t, 2, 0)


def from_blocks(t):
    nb, b, h, q, d = t.shape
    return jnp.moveaxis(t, 0, 2).reshape(b, h, nb * q, d)


def forgetting_attention(q, k, v, log_f):
    s_len = q.shape[2]
    scale = HEAD_DIM ** -0.5
    c = jnp.cumsum(log_f, axis=-1)
    key_pos = jnp.arange(s_len)

    def block(args):
        q_blk, c_blk, i = args
        q_pos = i * BLOCK_Q + jnp.arange(BLOCK_Q)
        s = jnp.einsum('bhqd,bhkd->bhqk', q_blk, k).astype(jnp.float32) * scale
        s = s + c_blk[..., :, None] - c[..., None, :]
        s = jnp.where(key_pos[None, :] <= q_pos[:, None], s, -jnp.inf)
        p = jax.nn.softmax(s, axis=-1)
        return jnp.einsum('bhqk,bhkd->bhqd', p.astype(v.dtype), v)

    nb = s_len // BLOCK_Q
    out = lax.map(block, (to_blocks(q), to_blocks(c), jnp.arange(nb)))
    return from_blocks(out)


def stick_breaking_attention(q, k, v):
    s_len = q.shape[2]
    scale = HEAD_DIM ** -0.5
    key_pos = jnp.arange(s_len)

    def block(args):
        q_blk, i = args
        q_pos = i * BLOCK_Q + jnp.arange(BLOCK_Q)
        z = jnp.einsum('bhqd,bhkd->bhqk', q_blk, k).astype(jnp.float32) * scale
        before = key_pos[None, :] < q_pos[:, None]
        log_beta = jax.nn.log_sigmoid(z)
        log_1m = jnp.where(before, log_beta - z, 0.0)
        log_remain = lax.cumsum(log_1m, axis=log_1m.ndim - 1, reverse=True) - log_1m
        a = jnp.where(before, jnp.exp(log_beta + log_remain), 0.0)
        return jnp.einsum('bhqk,bhkd->bhqd', a.astype(v.dtype), v)

    nb = s_len // BLOCK_Q
    out = lax.map(block, (to_blocks(q), jnp.arange(nb)))
    return from_blocks(out)


def memory_attention(q, k, v):
    s = jnp.einsum('bhqd,bhmd->bhqm', q, k).astype(jnp.float32) * (HEAD_DIM ** -0.5)
    p = jax.nn.softmax(s, axis=-1)
    return jnp.einsum('bhqm,bhmd->bhqd', p.astype(v.dtype), v)


def causal_depthwise_conv(u, w, b):
    c = u.shape[-1]
    y = lax.conv_general_dilated(u, w[:, None, :], window_strides=(1,),
                                 padding=[(CONV_W - 1, 0)],
                                 dimension_numbers=('NWC', 'WIO', 'NWC'),
                                 feature_group_count=c)
    return y + b


def _fwd_setup_inputs(seed: int = 0) -> dict:
    key = jax.random.key(seed)
    ks = jax.random.split(key, 24)
    f32 = jnp.float32
    nrm = lambda k, shape, fan_in: jax.random.normal(k, shape, f32) * (fan_in ** -0.5)
    gain = lambda k, shape: 1.0 + 0.05 * jax.random.normal(k, shape, f32)
    x = jax.random.normal(ks[0], (BATCH, SEQ, D_MODEL), f32)
    mem = jax.random.normal(ks[1], (BATCH, N_MEM_TOK, D_MODEL), f32)
    b_forget = (jnp.linspace(1.0, 6.0, N_FOX, dtype=f32)[None, :]
                + 0.1 * jax.random.normal(ks[4], (DEPTH, N_FOX), f32))
    return {
        'x': x,
        'mem': mem,
        'g_mix': gain(ks[2], (DEPTH, D_MODEL)),
        'w_in': nrm(ks[3], (DEPTH, D_MODEL, IN_COLS), D_MODEL),
        'b_forget': b_forget,
        'g_q_fox': gain(ks[5], (DEPTH, HEAD_DIM)),
        'g_k_fox': gain(ks[6], (DEPTH, HEAD_DIM)),
        'g_mem': gain(ks[7], (DEPTH, D_MODEL)),
        'w_mem_kv': nrm(ks[8], (DEPTH, D_MODEL, 2 * MEM_W), D_MODEL),
        'g_q_mem': gain(ks[9], (DEPTH, HEAD_DIM)),
        'g_k_mem': gain(ks[10], (DEPTH, HEAD_DIM)),
        'w_br_fox': nrm(ks[11], (DEPTH, FOX_W, D_MODEL), FOX_W),
        'w_br_sb': nrm(ks[12], (DEPTH, SB_W, D_MODEL), SB_W),
        'w_br_mem': nrm(ks[13], (DEPTH, MEM_W, D_MODEL), MEM_W),
        'b_gate': 0.01 * jax.random.normal(ks[14], (DEPTH, N_BRANCH, D_MODEL), f32),
        'w_out': nrm(ks[15], (DEPTH, D_MODEL, D_MODEL), D_MODEL),
        'g_ffn': gain(ks[16], (DEPTH, D_MODEL)),
        'w_up': nrm(ks[17], (DEPTH, D_MODEL, 2 * D_FF), D_MODEL),
        'conv_w': nrm(ks[18], (DEPTH, CONV_W, 2 * D_FF), CONV_W),
        'conv_b': 0.01 * jax.random.normal(ks[19], (DEPTH, 2 * D_FF), f32),
        'w_down': nrm(ks[20], (DEPTH, D_FF, D_MODEL), D_FF),
    }


def _fwd_reference(x, mem, g_mix, w_in, b_forget, g_q_fox, g_k_fox, g_mem, w_mem_kv,
              g_q_mem, g_k_mem, w_br_fox, w_br_sb, w_br_mem, b_gate, w_out,
              g_ffn, w_up, conv_w, conv_b, w_down):
    b, s, _ = x.shape
    cuts = np.cumsum([FOX_W, FOX_W, FOX_W, N_FOX, SB_W, SB_W, SB_W, MEM_W]).tolist()
    for l in range(DEPTH):
        h = rms_norm(x, g_mix[l])
        proj = h @ w_in[l]
        fq, fk, fv, f_logit, sq, sk, sv, mq, gates = jnp.split(proj, cuts, axis=-1)

        qa = rms_norm(split_heads(fq, N_FOX), g_q_fox[l])
        ka = rms_norm(split_heads(fk, N_FOX), g_k_fox[l])
        va = split_heads(fv, N_FOX)
        log_f = jax.nn.log_sigmoid(f_logit.astype(jnp.float32) + b_forget[l].astype(jnp.float32))
        o_fox = merge_heads(forgetting_attention(qa, ka, va, log_f.transpose(0, 2, 1)))

        o_sb = merge_heads(stick_breaking_attention(split_heads(sq, N_SB), split_heads(sk, N_SB),
                                                    split_heads(sv, N_SB)))

        mkv = rms_norm(mem, g_mem[l]) @ w_mem_kv[l]
        mk, mv = jnp.split(mkv, 2, axis=-1)
        qm = rms_norm(split_heads(mq, N_MEM), g_q_mem[l])
        km = rms_norm(split_heads(mk, N_MEM), g_k_mem[l])
        o_mem = merge_heads(memory_attention(qm, km, split_heads(mv, N_MEM)))

        g = jax.nn.sigmoid(gates.reshape(b, s, N_BRANCH, D_MODEL) + b_gate[l])
        merged = (g[:, :, 0] * (o_fox @ w_br_fox[l])
                  + g[:, :, 1] * (o_sb @ w_br_sb[l])
                  + g[:, :, 2] * (o_mem @ w_br_mem[l]))
        x = x + merged @ w_out[l]

        h2 = rms_norm(x, g_ffn[l])
        u = causal_depthwise_conv(h2 @ w_up[l], conv_w[l], conv_b[l])
        u_gate, u_val = jnp.split(u, 2, axis=-1)
        x = x + (jax.nn.silu(u_gate) * u_val) @ w_down[l]
    return x


import jax as _jax
import jax.numpy as _jnp

TWIN_FORMAT = 'train_step'
FWD_PARAMS = ['x', 'mem', 'g_mix', 'w_in', 'b_forget', 'g_q_fox', 'g_k_fox', 'g_mem', 'w_mem_kv', 'g_q_mem', 'g_k_mem', 'w_br_fox', 'w_br_sb', 'w_br_mem', 'b_gate', 'w_out', 'g_ffn', 'w_up', 'conv_w', 'conv_b', 'w_down']
TWIN_WEIGHTS = ['g_mix', 'w_in', 'b_forget', 'g_q_fox', 'g_k_fox', 'g_mem', 'w_mem_kv', 'g_q_mem', 'g_k_mem', 'w_br_fox', 'w_br_sb', 'w_br_mem', 'b_gate', 'w_out', 'g_ffn', 'w_up', 'conv_w', 'conv_b', 'w_down']
TWIN_DIFF_INPUT = 'x'
TWIN_INPUTS = ['x', 'mem', 'g_mix', 'w_in', 'b_forget', 'g_q_fox', 'g_k_fox', 'g_mem', 'w_mem_kv', 'g_q_mem', 'g_k_mem', 'w_br_fox', 'w_br_sb', 'w_br_mem', 'b_gate', 'w_out', 'g_ffn', 'w_up', 'conv_w', 'conv_b', 'w_down', 'loss_target', 'm_g_mix', 'm_w_in', 'm_b_forget', 'm_g_q_fox', 'm_g_k_fox', 'm_g_mem', 'm_w_mem_kv', 'm_g_q_mem', 'm_g_k_mem', 'm_w_br_fox', 'm_w_br_sb', 'm_w_br_mem', 'm_b_gate', 'm_w_out', 'm_g_ffn', 'm_w_up', 'm_conv_w', 'm_conv_b', 'm_w_down', 'v_g_mix', 'v_w_in', 'v_b_forget', 'v_g_q_fox', 'v_g_k_fox', 'v_g_mem', 'v_w_mem_kv', 'v_g_q_mem', 'v_g_k_mem', 'v_w_br_fox', 'v_w_br_sb', 'v_w_br_mem', 'v_b_gate', 'v_w_out', 'v_g_ffn', 'v_w_up', 'v_conv_w', 'v_conv_b', 'v_w_down']
TWIN_OUTPUTS = ['loss', 'grad_x', 'grad_g_mix', 'grad_w_in', 'grad_b_forget', 'grad_g_q_fox', 'grad_g_k_fox', 'grad_g_mem', 'grad_w_mem_kv', 'grad_g_q_mem', 'grad_g_k_mem', 'grad_w_br_fox', 'grad_w_br_sb', 'grad_w_br_mem', 'grad_b_gate', 'grad_w_out', 'grad_g_ffn', 'grad_w_up', 'grad_conv_w', 'grad_conv_b', 'grad_w_down', 'delta_g_mix', 'delta_w_in', 'delta_b_forget', 'delta_g_q_fox', 'delta_g_k_fox', 'delta_g_mem', 'delta_w_mem_kv', 'delta_g_q_mem', 'delta_g_k_mem', 'delta_w_br_fox', 'delta_w_br_sb', 'delta_w_br_mem', 'delta_b_gate', 'delta_w_out', 'delta_g_ffn', 'delta_w_up', 'delta_conv_w', 'delta_conv_b', 'delta_w_down', 'new_m_g_mix', 'new_m_w_in', 'new_m_b_forget', 'new_m_g_q_fox', 'new_m_g_k_fox', 'new_m_g_mem', 'new_m_w_mem_kv', 'new_m_g_q_mem', 'new_m_g_k_mem', 'new_m_w_br_fox', 'new_m_w_br_sb', 'new_m_w_br_mem', 'new_m_b_gate', 'new_m_w_out', 'new_m_g_ffn', 'new_m_w_up', 'new_m_conv_w', 'new_m_conv_b', 'new_m_w_down', 'new_v_g_mix', 'new_v_w_in', 'new_v_b_forget', 'new_v_g_q_fox', 'new_v_g_k_fox', 'new_v_g_mem', 'new_v_w_mem_kv', 'new_v_g_q_mem', 'new_v_g_k_mem', 'new_v_w_br_fox', 'new_v_w_br_sb', 'new_v_w_br_mem', 'new_v_b_gate', 'new_v_w_out', 'new_v_g_ffn', 'new_v_w_up', 'new_v_conv_w', 'new_v_conv_b', 'new_v_w_down']
TWIN_LEAF_KINDS = {'loss': 'loss', 'grad_x': 'grad_x', 'grad_g_mix': 'grad_w', 'grad_w_in': 'grad_w', 'grad_b_forget': 'grad_w', 'grad_g_q_fox': 'grad_w', 'grad_g_k_fox': 'grad_w', 'grad_g_mem': 'grad_w', 'grad_w_mem_kv': 'grad_w', 'grad_g_q_mem': 'grad_w', 'grad_g_k_mem': 'grad_w', 'grad_w_br_fox': 'grad_w', 'grad_w_br_sb': 'grad_w', 'grad_w_br_mem': 'grad_w', 'grad_b_gate': 'grad_w', 'grad_w_out': 'grad_w', 'grad_g_ffn': 'grad_w', 'grad_w_up': 'grad_w', 'grad_conv_w': 'grad_w', 'grad_conv_b': 'grad_w', 'grad_w_down': 'grad_w', 'delta_g_mix': 'delta_w', 'delta_w_in': 'delta_w', 'delta_b_forget': 'delta_w', 'delta_g_q_fox': 'delta_w', 'delta_g_k_fox': 'delta_w', 'delta_g_mem': 'delta_w', 'delta_w_mem_kv': 'delta_w', 'delta_g_q_mem': 'delta_w', 'delta_g_k_mem': 'delta_w', 'delta_w_br_fox': 'delta_w', 'delta_w_br_sb': 'delta_w', 'delta_w_br_mem': 'delta_w', 'delta_b_gate': 'delta_w', 'delta_w_out': 'delta_w', 'delta_g_ffn': 'delta_w', 'delta_w_up': 'delta_w', 'delta_conv_w': 'delta_w', 'delta_conv_b': 'delta_w', 'delta_w_down': 'delta_w', 'new_m_g_mix': 'new_m', 'new_m_w_in': 'new_m', 'new_m_b_forget': 'new_m', 'new_m_g_q_fox': 'new_m', 'new_m_g_k_fox': 'new_m', 'new_m_g_mem': 'new_m', 'new_m_w_mem_kv': 'new_m', 'new_m_g_q_mem': 'new_m', 'new_m_g_k_mem': 'new_m', 'new_m_w_br_fox': 'new_m', 'new_m_w_br_sb': 'new_m', 'new_m_w_br_mem': 'new_m', 'new_m_b_gate': 'new_m', 'new_m_w_out': 'new_m', 'new_m_g_ffn': 'new_m', 'new_m_w_up': 'new_m', 'new_m_conv_w': 'new_m', 'new_m_conv_b': 'new_m', 'new_m_w_down': 'new_m', 'new_v_g_mix': 'new_v', 'new_v_w_in': 'new_v', 'new_v_b_forget': 'new_v', 'new_v_g_q_fox': 'new_v', 'new_v_g_k_fox': 'new_v', 'new_v_g_mem': 'new_v', 'new_v_w_mem_kv': 'new_v', 'new_v_g_q_mem': 'new_v', 'new_v_g_k_mem': 'new_v', 'new_v_w_br_fox': 'new_v', 'new_v_w_br_sb': 'new_v', 'new_v_w_br_mem': 'new_v', 'new_v_b_gate': 'new_v', 'new_v_w_out': 'new_v', 'new_v_g_ffn': 'new_v', 'new_v_w_up': 'new_v', 'new_v_conv_w': 'new_v', 'new_v_conv_b': 'new_v', 'new_v_w_down': 'new_v'}


def _forward(args):
    return _fwd_reference(*[args[k] for k in FWD_PARAMS])


def _output_shape():
    def fwd():
        inp = _fwd_setup_inputs(0)
        return _fwd_reference(*[inp[k] for k in FWD_PARAMS])
    out = _jax.eval_shape(fwd)
    return out.shape, out.dtype

N_MICROBATCH = 1
ADAM_LR = 0.001
ADAM_B1 = 0.9
ADAM_B2 = 0.999
ADAM_EPS = 1e-08
ADAM_WD = 0.01
ADAM_STEP = 10
PER_EXAMPLE_BATCH_AXIS = {'x': 0, 'mem': 0, 'loss_target': 0}
SHARED_INPUTS = []
_WEIGHT_DTYPES = {'g_mix': _jnp.float32, 'w_in': _jnp.float32, 'b_forget': _jnp.float32, 'g_q_fox': _jnp.float32, 'g_k_fox': _jnp.float32, 'g_mem': _jnp.float32, 'w_mem_kv': _jnp.float32, 'g_q_mem': _jnp.float32, 'g_k_mem': _jnp.float32, 'w_br_fox': _jnp.float32, 'w_br_sb': _jnp.float32, 'w_br_mem': _jnp.float32, 'b_gate': _jnp.float32, 'w_out': _jnp.float32, 'g_ffn': _jnp.float32, 'w_up': _jnp.float32, 'conv_w': _jnp.float32, 'conv_b': _jnp.float32, 'w_down': _jnp.float32}
MOMENT_SCALE = {'g_mix': 3.337330e+00, 'w_in': 6.912246e-02, 'b_forget': 6.198041e+01, 'g_q_fox': 2.668355e+00, 'g_k_fox': 2.667548e+00, 'g_mem': 6.869514e-02, 'w_mem_kv': 6.614937e-02, 'g_q_mem': 7.227043e-01, 'g_k_mem': 7.292035e-01, 'w_br_fox': 5.059128e-02, 'w_br_sb': 1.144910e-01, 'w_br_mem': 4.540706e-02, 'b_gate': 4.057946e-01, 'w_out': 1.236731e-01, 'g_ffn': 1.238194e+01, 'w_up': 1.041141e-01, 'conv_w': 1.762810e+00, 'conv_b': 1.543308e+00, 'w_down': 1.355083e-01}


def _to_microbatches(a, axis):
    t = _jnp.moveaxis(a, axis, 0)
    t = t.reshape((N_MICROBATCH, t.shape[0] // N_MICROBATCH) + t.shape[1:])
    return _jnp.moveaxis(t, 1, axis + 1)


def setup_inputs(seed: int = 0) -> dict:
    inp = _fwd_setup_inputs(seed)
    key = _jax.random.fold_in(_jax.random.key(seed), 7919)
    shape, _ = _output_shape()
    out = dict(inp)
    out["loss_target"] = _jax.random.normal(_jax.random.fold_in(key, 0), shape, _jnp.float32)
    for i, name in enumerate(TWIN_WEIGHTS):
        w = inp[name].astype(_jnp.float32)
        if MOMENT_SCALE is None:
            s = _jnp.sqrt(_jnp.mean(_jnp.square(w)) + 1e-30)
        else:
            s = MOMENT_SCALE[name]
        km, kv = _jax.random.split(_jax.random.fold_in(key, i + 1))
        out[name] = w
        out["m_" + name] = s * _jax.random.normal(km, w.shape, _jnp.float32)
        out["v_" + name] = (s * s) * _jax.random.uniform(kv, w.shape, _jnp.float32, 0.5, 1.5)
    if N_MICROBATCH > 1:
        for name, axis in PER_EXAMPLE_BATCH_AXIS.items():
            out[name] = _to_microbatches(out[name], axis)
    return {'x': out['x'], 'mem': out['mem'], 'g_mix': out['g_mix'], 'w_in': out['w_in'], 'b_forget': out['b_forget'], 'g_q_fox': out['g_q_fox'], 'g_k_fox': out['g_k_fox'], 'g_mem': out['g_mem'], 'w_mem_kv': out['w_mem_kv'], 'g_q_mem': out['g_q_mem'], 'g_k_mem': out['g_k_mem'], 'w_br_fox': out['w_br_fox'], 'w_br_sb': out['w_br_sb'], 'w_br_mem': out['w_br_mem'], 'b_gate': out['b_gate'], 'w_out': out['w_out'], 'g_ffn': out['g_ffn'], 'w_up': out['w_up'], 'conv_w': out['conv_w'], 'conv_b': out['conv_b'], 'w_down': out['w_down'], 'loss_target': out['loss_target'], 'm_g_mix': out['m_g_mix'], 'm_w_in': out['m_w_in'], 'm_b_forget': out['m_b_forget'], 'm_g_q_fox': out['m_g_q_fox'], 'm_g_k_fox': out['m_g_k_fox'], 'm_g_mem': out['m_g_mem'], 'm_w_mem_kv': out['m_w_mem_kv'], 'm_g_q_mem': out['m_g_q_mem'], 'm_g_k_mem': out['m_g_k_mem'], 'm_w_br_fox': out['m_w_br_fox'], 'm_w_br_sb': out['m_w_br_sb'], 'm_w_br_mem': out['m_w_br_mem'], 'm_b_gate': out['m_b_gate'], 'm_w_out': out['m_w_out'], 'm_g_ffn': out['m_g_ffn'], 'm_w_up': out['m_w_up'], 'm_conv_w': out['m_conv_w'], 'm_conv_b': out['m_conv_b'], 'm_w_down': out['m_w_down'], 'v_g_mix': out['v_g_mix'], 'v_w_in': out['v_w_in'], 'v_b_forget': out['v_b_forget'], 'v_g_q_fox': out['v_g_q_fox'], 'v_g_k_fox': out['v_g_k_fox'], 'v_g_mem': out['v_g_mem'], 'v_w_mem_kv': out['v_w_mem_kv'], 'v_g_q_mem': out['v_g_q_mem'], 'v_g_k_mem': out['v_g_k_mem'], 'v_w_br_fox': out['v_w_br_fox'], 'v_w_br_sb': out['v_w_br_sb'], 'v_w_br_mem': out['v_w_br_mem'], 'v_b_gate': out['v_b_gate'], 'v_w_out': out['v_w_out'], 'v_g_ffn': out['v_g_ffn'], 'v_w_up': out['v_w_up'], 'v_conv_w': out['v_conv_w'], 'v_conv_b': out['v_conv_b'], 'v_w_down': out['v_w_down']}


def _loss(weights, diff, rest, loss_target):
    with _jax.named_scope("forward"):
        args = {**rest, TWIN_DIFF_INPUT: diff, **{k: w.astype(_WEIGHT_DTYPES[k]) for k, w in weights.items()}}
        y = _forward(args)
    with _jax.named_scope("loss_head"):
        err = _jnp.square(y.astype(_jnp.float32) - loss_target)
        return 0.5 * _jnp.sum(_jnp.mean(err, axis=-1)) if err.ndim else 0.5 * err


def _adamw(w, g, m, v):
    m = ADAM_B1 * m + (1.0 - ADAM_B1) * g
    v = ADAM_B2 * v + (1.0 - ADAM_B2) * _jnp.square(g)
    m_hat = m / (1.0 - ADAM_B1 ** ADAM_STEP)
    v_hat = v / (1.0 - ADAM_B2 ** ADAM_STEP)
    delta = -ADAM_LR * (m_hat / (_jnp.sqrt(v_hat) + ADAM_EPS) + ADAM_WD * w)
    return delta, m, v


def reference(x, mem, g_mix, w_in, b_forget, g_q_fox, g_k_fox, g_mem, w_mem_kv, g_q_mem, g_k_mem, w_br_fox, w_br_sb, w_br_mem, b_gate, w_out, g_ffn, w_up, conv_w, conv_b, w_down, loss_target, m_g_mix, m_w_in, m_b_forget, m_g_q_fox, m_g_k_fox, m_g_mem, m_w_mem_kv, m_g_q_mem, m_g_k_mem, m_w_br_fox, m_w_br_sb, m_w_br_mem, m_b_gate, m_w_out, m_g_ffn, m_w_up, m_conv_w, m_conv_b, m_w_down, v_g_mix, v_w_in, v_b_forget, v_g_q_fox, v_g_k_fox, v_g_mem, v_w_mem_kv, v_g_q_mem, v_g_k_mem, v_w_br_fox, v_w_br_sb, v_w_br_mem, v_b_gate, v_w_out, v_g_ffn, v_w_up, v_conv_w, v_conv_b, v_w_down):
    given = dict(x=x, mem=mem, g_mix=g_mix, w_in=w_in, b_forget=b_forget, g_q_fox=g_q_fox, g_k_fox=g_k_fox, g_mem=g_mem, w_mem_kv=w_mem_kv, g_q_mem=g_q_mem, g_k_mem=g_k_mem, w_br_fox=w_br_fox, w_br_sb=w_br_sb, w_br_mem=w_br_mem, b_gate=b_gate, w_out=w_out, g_ffn=g_ffn, w_up=w_up, conv_w=conv_w, conv_b=conv_b, w_down=w_down, loss_target=loss_target, m_g_mix=m_g_mix, m_w_in=m_w_in, m_b_forget=m_b_forget, m_g_q_fox=m_g_q_fox, m_g_k_fox=m_g_k_fox, m_g_mem=m_g_mem, m_w_mem_kv=m_w_mem_kv, m_g_q_mem=m_g_q_mem, m_g_k_mem=m_g_k_mem, m_w_br_fox=m_w_br_fox, m_w_br_sb=m_w_br_sb, m_w_br_mem=m_w_br_mem, m_b_gate=m_b_gate, m_w_out=m_w_out, m_g_ffn=m_g_ffn, m_w_up=m_w_up, m_conv_w=m_conv_w, m_conv_b=m_conv_b, m_w_down=m_w_down, v_g_mix=v_g_mix, v_w_in=v_w_in, v_b_forget=v_b_forget, v_g_q_fox=v_g_q_fox, v_g_k_fox=v_g_k_fox, v_g_mem=v_g_mem, v_w_mem_kv=v_w_mem_kv, v_g_q_mem=v_g_q_mem, v_g_k_mem=v_g_k_mem, v_w_br_fox=v_w_br_fox, v_w_br_sb=v_w_br_sb, v_w_br_mem=v_w_br_mem, v_b_gate=v_b_gate, v_w_out=v_w_out, v_g_ffn=v_g_ffn, v_w_up=v_w_up, v_conv_w=v_conv_w, v_conv_b=v_conv_b, v_w_down=v_w_down)
    weights = {n: given[n] for n in TWIN_WEIGHTS}
    shared = {n: given[n] for n in SHARED_INPUTS}
    per_example = {n: given[n] for n in ['x', 'mem']}
    grad_fn = _jax.value_and_grad(_loss, argnums=(0, 1))

    def one_microbatch(ex, loss_target):
        ex = dict(ex)
        diff = ex.pop(TWIN_DIFF_INPUT)
        return grad_fn(weights, diff, {**shared, **ex}, loss_target)

    if N_MICROBATCH == 1:
        loss, (grad_w, grad_x) = one_microbatch(per_example, given["loss_target"])
    else:
        def body(carry, xs):
            loss_sum, grad_sum = carry
            l_k, (gw_k, gx_k) = one_microbatch(xs[0], xs[1])
            with _jax.named_scope("update"):
                return (loss_sum + l_k, _jax.tree.map(_jnp.add, grad_sum, gw_k)), gx_k

        init = (_jnp.zeros((), _jnp.float32), _jax.tree.map(_jnp.zeros_like, weights))
        (loss, grad_w), grad_x = _jax.lax.scan(body, init, (per_example, given["loss_target"]))
    with _jax.named_scope("update"):
        delta_w, new_m, new_v = {}, {}, {}
        for n in TWIN_WEIGHTS:
            delta_w[n], new_m[n], new_v[n] = _adamw(weights[n], grad_w[n], given["m_" + n], given["v_" + n])
    return (loss, grad_x, *[grad_w[n] for n in TWIN_WEIGHTS], *[delta_w[n] for n in TWIN_WEIGHTS],
            *[new_m[n] for n in TWIN_WEIGHTS], *[new_v[n] for n in TWIN_WEIGHTS])
```

```python
import functools

import jax
import jax.numpy as jnp
from jax import lax
from jax.experimental import pallas as pl
from jax.experimental.pallas import tpu as pltpu

F32 = jnp.float32
BF16 = jnp.bfloat16

HEAD_DIM = 128
EPS = 1e-6
NEG_BIG = -1e30

ADAM_LR = 0.001
ADAM_B1 = 0.9
ADAM_B2 = 0.999
ADAM_EPS = 1e-08
ADAM_WD = 0.01
ADAM_STEP = 10

LANES = 128
BF16_SUBLANES = 16
VMEM_LIMIT_BYTES = 56 * 1024 * 1024
MM_TILE = 1024
ATT_TILE = 256
ROW_TILE = 256
COL_TILE = 512
ADAM_BLOCK_BYTES = 1 << 20

N_CHIPS = 4
N_DEV = 8
MESH = pl.DeviceIdType.MESH

IN_NAMES = ['x', 'mem', 'g_mix', 'w_in', 'b_forget', 'g_q_fox', 'g_k_fox', 'g_mem', 'w_mem_kv', 'g_q_mem', 'g_k_mem',
            'w_br_fox', 'w_br_sb', 'w_br_mem', 'b_gate', 'w_out', 'g_ffn', 'w_up', 'conv_w', 'conv_b', 'w_down']
WEIGHTS = IN_NAMES[2:]


def _tile(n, target):
    if n <= target:
        return n
    for t in range(target - target % LANES, LANES - 1, -LANES):
        if n % t == 0:
            return t
    return n


def _params(*sem):
    return pltpu.CompilerParams(dimension_semantics=sem, vmem_limit_bytes=VMEM_LIMIT_BYTES)


def _log_sigmoid(z):
    return jnp.minimum(z, 0.0) - jnp.log(1.0 + jnp.exp(-jnp.abs(z)))


def _split2(v):
    hi = v.astype(BF16)
    lo = (v - hi.astype(F32)).astype(BF16)
    return hi, lo


def _split3(v):
    hi = v.astype(BF16)
    r = v - hi.astype(F32)
    mid = r.astype(BF16)
    lo = (r - mid.astype(F32)).astype(BF16)
    return hi, mid, lo


def _dot(a, b):
    return lax.dot_general(a, b, (((1,), (0,)), ((), ())), preferred_element_type=F32)


def _dot_nt(a, b):
    return lax.dot_general(a, b, (((1,), (1,)), ((), ())), preferred_element_type=F32)


def _dot_tn(a, b):
    return lax.dot_general(a, b, (((0,), (0,)), ((), ())), preferred_element_type=F32)


def _mm(a, b, mode, out_dtype, name, residual=None):
    if mode == "nn":
        (m, k), (k2, n) = a.shape, b.shape
    elif mode == "nt":
        (m, k), (n, k2) = a.shape, b.shape
    else:
        (k, m), (k2, n) = a.shape, b.shape
    assert k == k2, (a.shape, b.shape, mode)
    tm, tn, tk = _tile(m, MM_TILE), _tile(n, MM_TILE), _tile(k, MM_TILE)
    nk = k // tk
    dot = {"nn": _dot, "nt": _dot_nt, "tn": _dot_tn}[mode]
    has_res = residual is not None

    def body(*refs):
        if has_res:
            a_ref, b_ref, r_ref, o_ref = refs[:4]
        else:
            a_ref, b_ref, o_ref = refs[:3]
            r_ref = None

        def finish(acc):
            if has_res:
                acc = acc + r_ref[...]
            o_ref[...] = acc.astype(o_ref.dtype)

        part = dot(a_ref[...], b_ref[...])
        if nk == 1:
            finish(part)
        else:
            acc_ref = refs[-1]
            kk = pl.program_id(2)

            @pl.when(kk == 0)
            def _():
                acc_ref[...] = part

            @pl.when(kk > 0)
            def _():
                acc_ref[...] += part

            @pl.when(kk == nk - 1)
            def _():
                finish(acc_ref[...])

    if mode == "tn":
        a_spec = pl.BlockSpec((tk, tm), lambda j, i, kk: (kk, i))
    else:
        a_spec = pl.BlockSpec((tm, tk), lambda j, i, kk: (i, kk))
    if mode == "nt":
        b_spec = pl.BlockSpec((tn, tk), lambda j, i, kk: (j, kk))
    else:
        b_spec = pl.BlockSpec((tk, tn), lambda j, i, kk: (kk, j))
    o_spec = pl.BlockSpec((tm, tn), lambda j, i, kk: (i, j))
    in_specs = [a_spec, b_spec] + ([o_spec] if has_res else [])
    args = (a, b) + ((residual,) if has_res else ())
    return pl.pallas_call(
        body,
        grid=(n // tn, m // tm, nk),
        in_specs=in_specs,
        out_specs=o_spec,
        out_shape=jax.ShapeDtypeStruct((m, n), out_dtype),
        scratch_shapes=[pltpu.VMEM((tm, tn), F32)] if nk > 1 else [],
        compiler_params=_params("parallel", "parallel", "arbitrary"),
        name=name,
    )(*args)


def _rms_fwd(x, g, name):
    s, d = x.shape
    tm = _tile(s, ROW_TILE)

    def body(x_ref, g_ref, h_ref, r_ref):
        xf = x_ref[...]
        r = lax.rsqrt(jnp.mean(xf * xf, axis=-1, keepdims=True) + EPS)
        h_ref[...] = ((xf * r) * g_ref[...]).astype(BF16)
        r_ref[...] = r

    return pl.pallas_call(
        body,
        grid=(s // tm,),
        in_specs=[pl.BlockSpec((tm, d), lambda i: (i, 0)), pl.BlockSpec((1, d), lambda i: (0, 0))],
        out_specs=[pl.BlockSpec((tm, d), lambda i: (i, 0)), pl.BlockSpec((tm, 1), lambda i: (i, 0))],
        out_shape=[jax.ShapeDtypeStruct((s, d), BF16), jax.ShapeDtypeStruct((s, 1), F32)],
        compiler_params=_params("parallel"),
        name=name,
    )(x, g)


def _rms_bwd(dh, x, rstd, g, res, name):
    s, d = x.shape
    tm = _tile(s, ROW_TILE)
    has_res = res is not None

    def body(*refs):
        if has_res:
            dh_ref, x_ref, r_ref, g_ref, res_ref, dx_ref, dxb_ref, dg_ref = refs
        else:
            dh_ref, x_ref, r_ref, g_ref, dx_ref, dxb_ref, dg_ref = refs
        dhf = dh_ref[...].astype(F32)
        xhat = x_ref[...] * r_ref[...]
        dy = dhf * g_ref[...]
        dx = r_ref[...] * (dy - xhat * jnp.mean(dy * xhat, axis=-1, keepdims=True))
        if has_res:
            dx = dx + res_ref[...]
        dx_ref[...] = dx
        dxb_ref[...] = dx.astype(BF16)
        part = jnp.sum(dhf * xhat, axis=0, keepdims=True)

        @pl.when(pl.program_id(0) == 0)
        def _():
            dg_ref[...] = part

        @pl.when(pl.program_id(0) > 0)
        def _():
            dg_ref[...] += part

    row = pl.BlockSpec((tm, d), lambda i: (i, 0))
    vec = pl.BlockSpec((1, d), lambda i: (0, 0))
    in_specs = [row, row, pl.BlockSpec((tm, 1), lambda i: (i, 0)), vec] + ([row] if has_res else [])
    args = (dh, x, rstd, g) + ((res,) if has_res else ())
    return pl.pallas_call(
        body,
        grid=(s // tm,),
        in_specs=in_specs,
        out_specs=[row, row, vec],
        out_shape=[jax.ShapeDtypeStruct((s, d), F32), jax.ShapeDtypeStruct((s, d), BF16),
                   jax.ShapeDtypeStruct((1, d), F32)],
        compiler_params=_params("arbitrary"),
        name=name,
    )(*args)


def _headnorm_fwd(src, col0, nheads, g, name):
    s = src.shape[0]
    tm = _tile(s, ROW_TILE)

    def body(x_ref, g_ref, o_ref):
        xf = x_ref[...].astype(F32)
        r = lax.rsqrt(jnp.mean(xf * xf, axis=-1, keepdims=True) + EPS)
        o_ref[...] = ((xf * r) * g_ref[...]).astype(BF16)

    return pl.pallas_call(
        body,
        grid=(s // tm, nheads),
        in_specs=[pl.BlockSpec((tm, HEAD_DIM), lambda i, h: (i, col0 + h)),
                  pl.BlockSpec((1, HEAD_DIM), lambda i, h: (0, 0))],
        out_specs=pl.BlockSpec((tm, HEAD_DIM), lambda i, h: (i, h)),
        out_shape=jax.ShapeDtypeStruct((s, nheads * HEAD_DIM), BF16),
        compiler_params=_params("parallel", "parallel"),
        name=name,
    )(src, g)


def _headnorm_bwd(dxn, src, col0, nheads, g, name):
    s = src.shape[0]
    tm = _tile(s, ROW_TILE)

    def body(d_ref, x_ref, g_ref, dx_ref, dg_ref):
        xf = x_ref[...].astype(F32)
        r = lax.rsqrt(jnp.mean(xf * xf, axis=-1, keepdims=True) + EPS)
        xhat = xf * r
        dn = d_ref[...].astype(F32)
        dy = dn * g_ref[...]
        dx_ref[...] = (r * (dy - xhat * jnp.mean(dy * xhat, axis=-1, keepdims=True))).astype(BF16)
        part = jnp.sum(dn * xhat, axis=0, keepdims=True)
        first = jnp.logical_and(pl.program_id(0) == 0, pl.program_id(1) == 0)

        @pl.when(first)
        def _():
            dg_ref[...] = part

        @pl.when(jnp.logical_not(first))
        def _():
            dg_ref[...] += part

    return pl.pallas_call(
        body,
        grid=(s // tm, nheads),
        in_specs=[pl.BlockSpec((tm, HEAD_DIM), lambda i, h: (i, h)),
                  pl.BlockSpec((tm, HEAD_DIM), lambda i, h: (i, col0 + h)),
                  pl.BlockSpec((1, HEAD_DIM), lambda i, h: (0, 0))],
        out_specs=[pl.BlockSpec((tm, HEAD_DIM), lambda i, h: (i, h)),
                   pl.BlockSpec((1, HEAD_DIM), lambda i, h: (0, 0))],
        out_shape=[jax.ShapeDtypeStruct((s, nheads * HEAD_DIM), BF16), jax.ShapeDtypeStruct((1, HEAD_DIM), F32)],
        compiler_params=_params("arbitrary", "arbitrary"),
        name=name,
    )(dxn, src, g)


def _tri(t, lower_inclusive):
    r = lax.broadcasted_iota(jnp.int32, (t, t), 0)
    c = lax.broadcasted_iota(jnp.int32, (t, t), 1)
    keep = (c <= r) if lower_inclusive else (c >= r)
    return jnp.where(keep, 1.0, 0.0).astype(BF16)


def _forget_fwd(f_logit, b_pad):
    s = f_logit.shape[0]
    t = _tile(s, ATT_TILE)

    def body(f_ref, b_ref, c_ref, carry):
        @pl.when(pl.program_id(0) == 0)
        def _():
            carry[...] = jnp.zeros_like(carry)

        lf = _log_sigmoid(f_ref[...] + b_ref[...])
        tri = _tri(t, True)
        acc = carry[...]
        for part in _split3(lf):
            acc = acc + _dot(tri, part)
        c_ref[...] = acc
        carry[...] += jnp.sum(lf, axis=0, keepdims=True)

    return pl.pallas_call(
        body,
        grid=(s // t,),
        in_specs=[pl.BlockSpec((t, LANES), lambda i: (i, 0)), pl.BlockSpec((1, LANES), lambda i: (0, 0))],
        out_specs=pl.BlockSpec((t, LANES), lambda i: (i, 0)),
        out_shape=jax.ShapeDtypeStruct((s, LANES), F32),
        scratch_shapes=[pltpu.VMEM((1, LANES), F32)],
        compiler_params=_params("arbitrary"),
        name="forget_fwd",
    )(f_logit, b_pad)


def _forget_bwd(dc, f_logit, b_pad):
    s = f_logit.shape[0]
    t = _tile(s, ATT_TILE)
    nb = s // t

    def body(dc_ref, f_ref, b_ref, df_ref, db_ref, carry):
        @pl.when(pl.program_id(0) == 0)
        def _():
            carry[...] = jnp.zeros_like(carry)
            db_ref[...] = jnp.zeros_like(db_ref)

        d = dc_ref[...]
        tri = _tri(t, False)
        acc = carry[...]
        for part in _split3(d):
            acc = acc + _dot(tri, part)
        z = f_ref[...] + b_ref[...]
        df = acc * jnp.exp(_log_sigmoid(-z))
        df_ref[...] = df
        db_ref[...] += jnp.sum(df, axis=0, keepdims=True)
        carry[...] += jnp.sum(d, axis=0, keepdims=True)

    rev = pl.BlockSpec((t, LANES), lambda i: (nb - 1 - i, 0))
    vec = pl.BlockSpec((1, LANES), lambda i: (0, 0))
    return pl.pallas_call(
        body,
        grid=(nb,),
        in_specs=[rev, rev, vec],
        out_specs=[rev, vec],
        out_shape=[jax.ShapeDtypeStruct((s, LANES), F32), jax.ShapeDtypeStruct((1, LANES), F32)],
        scratch_shapes=[pltpu.VMEM((1, LANES), F32)],
        compiler_params=_params("arbitrary"),
        name="forget_bwd",
    )(dc, f_logit, b_pad)


def _pos(t, qi, kj):
    row = qi * t + lax.broadcasted_iota(jnp.int32, (t, t), 0)
    col = kj * t + lax.broadcasted_iota(jnp.int32, (t, t), 1)
    return row, col


def _fox_fwd(qn, kn, proj, colv, c_col, c_row, nheads):
    s = qn.shape[0]
    t = _tile(s, ATT_TILE)
    scale = HEAD_DIM ** -0.5

    def body(q_ref, k_ref, v_ref, cc_ref, cr_ref, o_ref, lse_ref):
        qi = pl.program_id(1)
        q = q_ref[...]
        cc = cc_ref[0]

        def step(kj, carry):
            m, l, acc = carry
            off = pl.multiple_of(kj * t, t)
            k = k_ref[pl.ds(off, t), :]
            v = v_ref[pl.ds(off, t), :]
            sc = _dot_nt(q, k) * scale + (cc - cr_ref[0, :, pl.ds(off, t)])
            row, col = _pos(t, qi, kj)
            sc = jnp.where(col <= row, sc, NEG_BIG)
            m_new = jnp.maximum(m, jnp.max(sc, axis=-1, keepdims=True))
            p = jnp.exp(sc - m_new)
            alpha = jnp.exp(m - m_new)
            l = alpha * l + jnp.sum(p, axis=-1, keepdims=True)
            acc = alpha * acc + _dot(p.astype(BF16), v)
            return m_new, l, acc

        init = (jnp.full((t, 1), NEG_BIG, F32), jnp.zeros((t, 1), F32), jnp.zeros((t, HEAD_DIM), F32))
        m, l, acc = lax.fori_loop(0, qi + 1, step, init)
        o_ref[...] = (acc / l).astype(BF16)
        lse_ref[0] = m + jnp.log(l)

    return pl.pallas_call(
        body,
        grid=(nheads, s // t),
        in_specs=[pl.BlockSpec((t, HEAD_DIM), lambda h, i: (i, h)),
                  pl.BlockSpec((s, HEAD_DIM), lambda h, i: (0, h)),
                  pl.BlockSpec((s, HEAD_DIM), lambda h, i: (0, colv + h)),
                  pl.BlockSpec((1, t, 1), lambda h, i: (h, i, 0)),
                  pl.BlockSpec((1, 1, s), lambda h, i: (h, 0, 0))],
        out_specs=[pl.BlockSpec((t, HEAD_DIM), lambda h, i: (i, h)),
                   pl.BlockSpec((1, t, 1), lambda h, i: (h, i, 0))],
        out_shape=[jax.ShapeDtypeStruct((s, nheads * HEAD_DIM), BF16), jax.ShapeDtypeStruct((nheads, s, 1), F32)],
        compiler_params=_params("parallel", "parallel"),
        name="fox_fwd",
    )(qn, kn, proj, c_col, c_row)


def _fox_bwd(qn, kn, proj, colv, c_col, c_row, o, do, lse, nheads):
    s = qn.shape[0]
    t = _tile(s, ATT_TILE)
    scale = HEAD_DIM ** -0.5

    def body(q_ref, k_ref, v_ref, cc_ref, cr_ref, o_ref, do_ref, lse_ref, dq_ref, dk_ref, dv_ref, dcs_ref):
        qi = pl.program_id(1)

        @pl.when(qi == 0)
        def _():
            dk_ref[...] = jnp.zeros_like(dk_ref)
            dv_ref[...] = jnp.zeros_like(dv_ref)
            dcs_ref[...] = jnp.zeros_like(dcs_ref)

        q = q_ref[...]
        do_ = do_ref[...]
        cc = cc_ref[0]
        lse_ = lse_ref[0]
        delta = jnp.sum(o_ref[...].astype(F32) * do_.astype(F32), axis=-1, keepdims=True)

        def step(kj, dq):
            off = pl.multiple_of(kj * t, t)
            k = k_ref[pl.ds(off, t), :]
            v = v_ref[pl.ds(off, t), :]
            sc = _dot_nt(q, k) * scale + (cc - cr_ref[0, :, pl.ds(off, t)])
            row, col = _pos(t, qi, kj)
            p = jnp.where(col <= row, jnp.exp(sc - lse_), 0.0)
            dp = _dot_nt(do_, v)
            ds = p * (dp - delta)
            dsb = ds.astype(BF16)
            dv_ref[pl.ds(off, t), :] += _dot_tn(p.astype(BF16), do_)
            dk_ref[pl.ds(off, t), :] += _dot_tn(dsb, q) * scale
            dcs_ref[0, :, pl.ds(off, t)] += jnp.sum(ds, axis=0, keepdims=True)
            return dq + _dot(dsb, k) * scale

        dq_ref[...] = lax.fori_loop(0, qi + 1, step, jnp.zeros((t, HEAD_DIM), F32))

    tile = pl.BlockSpec((t, HEAD_DIM), lambda h, i: (i, h))
    full = pl.BlockSpec((s, HEAD_DIM), lambda h, i: (0, h))
    colspec = pl.BlockSpec((1, t, 1), lambda h, i: (h, i, 0))
    rowspec = pl.BlockSpec((1, 1, s), lambda h, i: (h, 0, 0))
    w = nheads * HEAD_DIM
    return pl.pallas_call(
        body,
        grid=(nheads, s // t),
        in_specs=[tile, full, pl.BlockSpec((s, HEAD_DIM), lambda h, i: (0, colv + h)), colspec, rowspec,
                  tile, tile, colspec],
        out_specs=[tile, full, full, rowspec],
        out_shape=[jax.ShapeDtypeStruct((s, w), F32), jax.ShapeDtypeStruct((s, w), F32),
                   jax.ShapeDtypeStruct((s, w), F32), jax.ShapeDtypeStruct((nheads, 1, s), F32)],
        compiler_params=_params("arbitrary", "arbitrary"),
        name="fox_bwd",
    )(qn, kn, proj, c_col, c_row, o, do, lse)


def _sb_tile(q, k, scale, t, qi, kj):
    z = _dot_nt(q, k) * scale
    lb = _log_sigmoid(z)
    row, col = _pos(t, qi, kj)
    valid = col < row
    lm = jnp.where(valid, lb - z, 0.0)
    r = lax.broadcasted_iota(jnp.int32, (t, t), 0)
    c = lax.broadcasted_iota(jnp.int32, (t, t), 1)
    later = jnp.where(r > c, 1.0, 0.0).astype(BF16)
    hi, lo = _split2(lm)
    suffix = _dot(hi, later) + _dot(lo, later)
    return lb, valid, lm, suffix


def _sb_fwd(proj, colq, colk, colv, nheads):
    s = proj.shape[0]
    t = _tile(s, ATT_TILE)
    scale = HEAD_DIM ** -0.5

    def body(q_ref, k_ref, v_ref, o_ref):
        qi = pl.program_id(1)
        q = q_ref[...]

        def step(i, carry):
            rc, acc = carry
            kj = qi - i
            off = pl.multiple_of(kj * t, t)
            k = k_ref[pl.ds(off, t), :]
            v = v_ref[pl.ds(off, t), :]
            lb, valid, lm, suffix = _sb_tile(q, k, scale, t, qi, kj)
            a = jnp.where(valid, jnp.exp(lb + suffix + rc), 0.0)
            acc = acc + _dot(a.astype(BF16), v)
            return rc + jnp.sum(lm, axis=-1, keepdims=True), acc

        _, acc = lax.fori_loop(0, qi + 1, step, (jnp.zeros((t, 1), F32), jnp.zeros((t, HEAD_DIM), F32)))
        o_ref[...] = acc.astype(BF16)

    return pl.pallas_call(
        body,
        grid=(nheads, s // t),
        in_specs=[pl.BlockSpec((t, HEAD_DIM), lambda h, i: (i, colq + h)),
                  pl.BlockSpec((s, HEAD_DIM), lambda h, i: (0, colk + h)),
                  pl.BlockSpec((s, HEAD_DIM), lambda h, i: (0, colv + h))],
        out_specs=pl.BlockSpec((t, HEAD_DIM), lambda h, i: (i, h)),
        out_shape=jax.ShapeDtypeStruct((s, nheads * HEAD_DIM), BF16),
        compiler_params=_params("parallel", "parallel"),
        name="sb_fwd",
    )(proj, proj, proj)


def _sb_bwd(proj, colq, colk, colv, do, nheads):
    s = proj.shape[0]
    t = _tile(s, ATT_TILE)
    scale = HEAD_DIM ** -0.5

    def body(q_ref, k_ref, v_ref, do_ref, dq_ref, dk_ref, dv_ref, g_s, beta_s):
        qi = pl.program_id(1)

        @pl.when(qi == 0)
        def _():
            dk_ref[...] = jnp.zeros_like(dk_ref)
            dv_ref[...] = jnp.zeros_like(dv_ref)

        q = q_ref[...]
        do_ = do_ref[...]

        def back(i, rc):
            kj = qi - i
            off = pl.multiple_of(kj * t, t)
            k = k_ref[pl.ds(off, t), :]
            v = v_ref[pl.ds(off, t), :]
            lb, valid, lm, suffix = _sb_tile(q, k, scale, t, qi, kj)
            a = jnp.where(valid, jnp.exp(lb + suffix + rc), 0.0)
            g_s[:, pl.ds(off, t)] = a * _dot_nt(do_, v)
            beta_s[:, pl.ds(off, t)] = jnp.exp(lb)
            dv_ref[pl.ds(off, t), :] += _dot_tn(a.astype(BF16), do_)
            return rc + jnp.sum(lm, axis=-1, keepdims=True)

        lax.fori_loop(0, qi + 1, back, jnp.zeros((t, 1), F32))

        def fwd(kj, carry):
            gc, dq = carry
            off = pl.multiple_of(kj * t, t)
            k = k_ref[pl.ds(off, t), :]
            g = g_s[:, pl.ds(off, t)]
            beta = beta_s[:, pl.ds(off, t)]
            r = lax.broadcasted_iota(jnp.int32, (t, t), 0)
            c = lax.broadcasted_iota(jnp.int32, (t, t), 1)
            earlier = jnp.where(r < c, 1.0, 0.0).astype(BF16)
            hi, lo = _split2(g)
            gsum = _dot(hi, earlier) + _dot(lo, earlier) + gc
            row, col = _pos(t, qi, kj)
            dz = jnp.where(col < row, g * (1.0 - beta) - gsum * beta, 0.0).astype(BF16)
            dk_ref[pl.ds(off, t), :] += _dot_tn(dz, q) * scale
            return gc + jnp.sum(g, axis=-1, keepdims=True), dq + _dot(dz, k) * scale

        _, dq = lax.fori_loop(0, qi + 1, fwd, (jnp.zeros((t, 1), F32), jnp.zeros((t, HEAD_DIM), F32)))
        dq_ref[...] = dq

    tile = pl.BlockSpec((t, HEAD_DIM), lambda h, i: (i, h))
    full = pl.BlockSpec((s, HEAD_DIM), lambda h, i: (0, h))
    w = nheads * HEAD_DIM
    return pl.pallas_call(
        body,
        grid=(nheads, s // t),
        in_specs=[pl.BlockSpec((t, HEAD_DIM), lambda h, i: (i, colq + h)),
                  pl.BlockSpec((s, HEAD_DIM), lambda h, i: (0, colk + h)),
                  pl.BlockSpec((s, HEAD_DIM), lambda h, i: (0, colv + h)),
                  tile],
        out_specs=[tile, full, full],
        out_shape=[jax.ShapeDtypeStruct((s, w), F32)] * 3,
        scratch_shapes=[pltpu.VMEM((t, s), F32), pltpu.VMEM((t, s), F32)],
        compiler_params=_params("arbitrary", "arbitrary"),
        name="sb_bwd",
    )(proj, proj, proj, do)


def _mem_fwd(qn, kn, mkv, nheads):
    s = qn.shape[0]
    mtok = kn.shape[0]
    t = _tile(s, ATT_TILE)
    scale = HEAD_DIM ** -0.5

    def body(q_ref, k_ref, v_ref, o_ref):
        sc = _dot_nt(q_ref[...], k_ref[...]) * scale
        p = jnp.exp(sc - jnp.max(sc, axis=-1, keepdims=True))
        p = p / jnp.sum(p, axis=-1, keepdims=True)
        o_ref[...] = _dot(p.astype(BF16), v_ref[...]).astype(BF16)

    return pl.pallas_call(
        body,
        grid=(nheads, s // t),
        in_specs=[pl.BlockSpec((t, HEAD_DIM), lambda h, i: (i, h)),
                  pl.BlockSpec((mtok, HEAD_DIM), lambda h, i: (0, h)),
                  pl.BlockSpec((mtok, HEAD_DIM), lambda h, i: (0, nheads + h))],
        out_specs=pl.BlockSpec((t, HEAD_DIM), lambda h, i: (i, h)),
        out_shape=jax.ShapeDtypeStruct((s, nheads * HEAD_DIM), BF16),
        compiler_params=_params("parallel", "parallel"),
        name="mem_fwd",
    )(qn, kn, mkv)


def _mem_bwd(qn, kn, mkv, do, nheads):
    s = qn.shape[0]
    mtok = kn.shape[0]
    t = _tile(s, ATT_TILE)
    scale = HEAD_DIM ** -0.5

    def body(q_ref, k_ref, v_ref, do_ref, dq_ref, dk_ref, dv_ref):
        @pl.when(pl.program_id(1) == 0)
        def _():
            dk_ref[...] = jnp.zeros_like(dk_ref)
            dv_ref[...] = jnp.zeros_like(dv_ref)

        q = q_ref[...]
        k = k_ref[...]
        do_ = do_ref[...]
        sc = _dot_nt(q, k) * scale
        p = jnp.exp(sc - jnp.max(sc, axis=-1, keepdims=True))
        p = p / jnp.sum(p, axis=-1, keepdims=True)
        dp = _dot_nt(do_, v_ref[...])
        ds = (p * (dp - jnp.sum(p * dp, axis=-1, keepdims=True))).astype(BF16)
        dq_ref[...] = _dot(ds, k) * scale
        dk_ref[...] += _dot_tn(ds, q) * scale
        dv_ref[...] += _dot_tn(p.astype(BF16), do_)

    tile = pl.BlockSpec((t, HEAD_DIM), lambda h, i: (i, h))
    kspec = pl.BlockSpec((mtok, HEAD_DIM), lambda h, i: (0, h))
    w = nheads * HEAD_DIM
    return pl.pallas_call(
        body,
        grid=(nheads, s // t),
        in_specs=[tile, kspec, pl.BlockSpec((mtok, HEAD_DIM), lambda h, i: (0, nheads + h)), tile],
        out_specs=[tile, kspec, kspec],
        out_shape=[jax.ShapeDtypeStruct((s, w), F32), jax.ShapeDtypeStruct((mtok, w), F32),
                   jax.ShapeDtypeStruct((mtok, w), F32)],
        compiler_params=_params("arbitrary", "arbitrary"),
        name="mem_bwd",
    )(qn, kn, mkv, do)


def _merge_fwd(p0, p1, p2, proj, colg, b_gate):
    s, d = p0.shape
    tm, tn = _tile(s, ROW_TILE), _tile(d, COL_TILE)
    g0 = colg * LANES // tn
    nj = d // tn

    def body(p0_ref, p1_ref, p2_ref, ga_ref, gb_ref, gc_ref, b_ref, o_ref):
        acc = jnp.zeros((tm, tn), F32)
        for b, (p_ref, g_ref) in enumerate(((p0_ref, ga_ref), (p1_ref, gb_ref), (p2_ref, gc_ref))):
            gate = jax.nn.sigmoid(g_ref[...].astype(F32) + b_ref[b:b + 1, :])
            acc = acc + gate * p_ref[...]
        o_ref[...] = acc.astype(BF16)

    blk = pl.BlockSpec((tm, tn), lambda i, j: (i, j))
    gates = [pl.BlockSpec((tm, tn), functools.partial(lambda i, j, b: (i, g0 + b * nj + j), b=b)) for b in range(3)]
    return pl.pallas_call(
        body,
        grid=(s // tm, nj),
        in_specs=[blk, blk, blk] + gates + [pl.BlockSpec((3, tn), lambda i, j: (0, j))],
        out_specs=blk,
        out_shape=jax.ShapeDtypeStruct((s, d), BF16),
        compiler_params=_params("parallel", "parallel"),
        name="merge_fwd",
    )(p0, p1, p2, proj, proj, proj, b_gate)


def _merge_bwd(dmerged, p0, p1, p2, proj, colg, b_gate):
    s, d = p0.shape
    tm, tn = _tile(s, ROW_TILE), _tile(d, COL_TILE)
    g0 = colg * LANES // tn
    nj = d // tn

    def body(dm_ref, p0_ref, p1_ref, p2_ref, ga_ref, gb_ref, gc_ref, b_ref,
             d0_ref, d1_ref, d2_ref, dga_ref, dgb_ref, dgc_ref, db_ref):
        dm = dm_ref[...].astype(F32)
        parts = []
        for b, (p_ref, g_ref, dp_ref, dg_ref) in enumerate(((p0_ref, ga_ref, d0_ref, dga_ref),
                                                            (p1_ref, gb_ref, d1_ref, dgb_ref),
                                                            (p2_ref, gc_ref, d2_ref, dgc_ref))):
            gate = jax.nn.sigmoid(g_ref[...].astype(F32) + b_ref[b:b + 1, :])
            dp_ref[...] = (dm * gate).astype(BF16)
            dgate = dm * p_ref[...] * gate * (1.0 - gate)
            dg_ref[...] = dgate.astype(BF16)
            parts.append(jnp.sum(dgate, axis=0, keepdims=True))
        part = jnp.concatenate(parts, axis=0)

        @pl.when(pl.program_id(1) == 0)
        def _():
            db_ref[...] = part

        @pl.when(pl.program_id(1) > 0)
        def _():
            db_ref[...] += part

    blk = pl.BlockSpec((tm, tn), lambda j, i: (i, j))
    gates = [pl.BlockSpec((tm, tn), functools.partial(lambda j, i, b: (i, g0 + b * nj + j), b=b)) for b in range(3)]
    bias = pl.BlockSpec((3, tn), lambda j, i: (0, j))
    return pl.pallas_call(
        body,
        grid=(nj, s // tm),
        in_specs=[blk, blk, blk, blk] + gates + [bias],
        out_specs=[blk] * 6 + [bias],
        out_shape=[jax.ShapeDtypeStruct((s, d), BF16)] * 6 + [jax.ShapeDtypeStruct((3, d), F32)],
        compiler_params=_params("parallel", "arbitrary"),
        name="merge_bwd",
    )(dmerged, p0, p1, p2, proj, proj, proj, b_gate)


def _shift_down(v, n):
    rows = lax.broadcasted_iota(jnp.int32, v.shape, 0)
    return jnp.where(rows >= n, pltpu.roll(v, n, 0), 0.0)


def _shift_up(v, n):
    s = v.shape[0]
    rows = lax.broadcasted_iota(jnp.int32, v.shape, 0)
    return jnp.where(rows < s - n, pltpu.roll(v, s - n, 0), 0.0)


def _conv(v, w_ref, b_ref):
    taps = w_ref.shape[0]
    out = v * w_ref[taps - 1:taps, :] + b_ref[...]
    for n in range(1, taps):
        out = out + _shift_down(v, n) * w_ref[taps - 1 - n:taps - n, :]
    return out


def _conv_act_fwd(up, conv_w, conv_b):
    s, f2 = up.shape
    f = f2 // 2
    tn = LANES
    nj = f // tn
    taps = conv_w.shape[0]

    def body(ug_ref, uv_ref, wg_ref, wv_ref, bg_ref, bv_ref, o_ref):
        cg = _conv(ug_ref[...].astype(F32), wg_ref, bg_ref)
        cv = _conv(uv_ref[...].astype(F32), wv_ref, bv_ref)
        o_ref[...] = (cg * jax.nn.sigmoid(cg) * cv).astype(BF16)

    return pl.pallas_call(
        body,
        grid=(nj,),
        in_specs=[pl.BlockSpec((s, tn), lambda j: (0, j)), pl.BlockSpec((s, tn), lambda j: (0, nj + j)),
                  pl.BlockSpec((taps, tn), lambda j: (0, j)), pl.BlockSpec((taps, tn), lambda j: (0, nj + j)),
                  pl.BlockSpec((1, tn), lambda j: (0, j)), pl.BlockSpec((1, tn), lambda j: (0, nj + j))],
        out_specs=pl.BlockSpec((s, tn), lambda j: (0, j)),
        out_shape=jax.ShapeDtypeStruct((s, f), BF16),
        compiler_params=_params("parallel"),
        name="conv_act_fwd",
    )(up, up, conv_w, conv_w, conv_b, conv_b)


def _conv_act_bwd(up, conv_w, conv_b, dact):
    s, f2 = up.shape
    f = f2 // 2
    tn = LANES
    nj = f // tn
    taps = conv_w.shape[0]

    def half(v, du, w_ref, dup_ref, dw_ref, db_ref):
        dup = du * w_ref[taps - 1:taps, :]
        rows = [None] * taps
        rows[taps - 1] = jnp.sum(du * v, axis=0, keepdims=True)
        for n in range(1, taps):
            dup = dup + _shift_up(du, n) * w_ref[taps - 1 - n:taps - n, :]
            rows[taps - 1 - n] = jnp.sum(du * _shift_down(v, n), axis=0, keepdims=True)
        dup_ref[...] = dup.astype(BF16)
        dw_ref[...] = jnp.concatenate(rows, axis=0)
        db_ref[...] = jnp.sum(du, axis=0, keepdims=True)

    def body(ug_ref, uv_ref, wg_ref, wv_ref, bg_ref, bv_ref, da_ref,
             dug_ref, duv_ref, dwg_ref, dwv_ref, dbg_ref, dbv_ref):
        ug = ug_ref[...].astype(F32)
        uv = uv_ref[...].astype(F32)
        cg = _conv(ug, wg_ref, bg_ref)
        cv = _conv(uv, wv_ref, bv_ref)
        da = da_ref[...].astype(F32)
        sg = jax.nn.sigmoid(cg)
        dcv = da * cg * sg
        dcg = da * cv * (sg + cg * sg * (1.0 - sg))
        half(ug, dcg, wg_ref, dug_ref, dwg_ref, dbg_ref)
        half(uv, dcv, wv_ref, duv_ref, dwv_ref, dbv_ref)

    lo = lambda rows: pl.BlockSpec((rows, tn), lambda j: (0, j))
    hi = lambda rows: pl.BlockSpec((rows, tn), lambda j: (0, nj + j))
    return pl.pallas_call(
        body,
        grid=(nj,),
        in_specs=[lo(s), hi(s), lo(taps), hi(taps), lo(1), hi(1), lo(s)],
        out_specs=[lo(s), lo(s), lo(taps), lo(taps), lo(1), lo(1)],
        out_shape=[jax.ShapeDtypeStruct((s, f), BF16)] * 2 + [jax.ShapeDtypeStruct((taps, f), F32)] * 2
        + [jax.ShapeDtypeStruct((1, f), F32)] * 2,
        compiler_params=_params("parallel"),
        name="conv_act_bwd",
    )(up, up, conv_w, conv_w, conv_b, conv_b, dact)


def _loss_grad(y, target):
    s, d = y.shape
    tm = _tile(s, ROW_TILE)

    def body(y_ref, t_ref, dy_ref, dyb_ref, l_ref):
        e = y_ref[...] - t_ref[...]
        dy = e * (1.0 / d)
        dy_ref[...] = dy
        dyb_ref[...] = dy.astype(BF16)
        tot = jnp.sum(jnp.sum(e * e, axis=-1, keepdims=True), axis=0, keepdims=True)
        l_ref[...] = jnp.broadcast_to(tot, (8, LANES))

    row = pl.BlockSpec((tm, d), lambda i: (i, 0))
    return pl.pallas_call(
        body,
        grid=(s // tm,),
        in_specs=[row, row],
        out_specs=[row, row, pl.BlockSpec((8, LANES), lambda i: (i, 0))],
        out_shape=[jax.ShapeDtypeStruct((s, d), F32), jax.ShapeDtypeStruct((s, d), BF16),
                   jax.ShapeDtypeStruct((s // tm * 8, LANES), F32)],
        compiler_params=_params("parallel"),
        name="loss_grad",
    )(y, target)


def _row_tile(rows, row_bytes, budget):
    if rows * row_bytes <= budget or rows % 8:
        return rows
    best = 8
    for t in range(8, rows, 8):
        if rows % t == 0 and t * row_bytes <= budget:
            best = t
    return best


def _adamw(w, g, m, v, name):
    r, c = w.shape
    tr = _row_tile(r, c * 4, ADAM_BLOCK_BYTES)

    def body(w_ref, g_ref, m_ref, v_ref, d_ref, mo_ref, vo_ref):
        gg = g_ref[...]
        m_new = ADAM_B1 * m_ref[...] + (1.0 - ADAM_B1) * gg
        v_new = ADAM_B2 * v_ref[...] + (1.0 - ADAM_B2) * (gg * gg)
        m_hat = m_new / (1.0 - ADAM_B1 ** ADAM_STEP)
        v_hat = v_new / (1.0 - ADAM_B2 ** ADAM_STEP)
        d_ref[...] = -ADAM_LR * (m_hat / (jnp.sqrt(v_hat) + ADAM_EPS) + ADAM_WD * w_ref[...])
        mo_ref[...] = m_new
        vo_ref[...] = v_new

    blk = pl.BlockSpec((tr, c), lambda i: (i, 0))
    return pl.pallas_call(
        body,
        grid=(r // tr,),
        in_specs=[blk] * 4,
        out_specs=[blk] * 3,
        out_shape=[jax.ShapeDtypeStruct((r, c), F32)] * 3,
        compiler_params=_params("parallel"),
        name=name,
    )(w, g, m, v)


def _add_sibling(g, r1, core, name):
    _, _, h, c = g.shape
    th = _row_tile(h, c * 2, ADAM_BLOCK_BYTES)

    def body(core_ref, g_ref, r_ref, o_ref):
        o_ref[...] = (g_ref[...].astype(F32) + r_ref[...].astype(F32)).astype(BF16)

    return pl.pallas_call(
        body,
        grid_spec=pltpu.PrefetchScalarGridSpec(
            num_scalar_prefetch=1,
            grid=(N_CHIPS, h // th),
            in_specs=[pl.BlockSpec((None, None, th, c), lambda j, i, core_ref: (j, core_ref[0], i, 0)),
                      pl.BlockSpec((None, th, c), lambda j, i, core_ref: (j, i, 0))],
            out_specs=pl.BlockSpec((None, th, c), lambda j, i, core_ref: (j, i, 0)),
        ),
        out_shape=jax.ShapeDtypeStruct((N_CHIPS, h, c), BF16),
        compiler_params=_params("parallel", "parallel"),
        name=name,
    )(core, g, r1)


def _add_chips(hsum, r2, chip, name):
    _, h, c = hsum.shape
    th = _row_tile(h, c * 4, ADAM_BLOCK_BYTES)

    def body(chip_ref, own_ref, r_ref, o_ref):
        acc = own_ref[...].astype(F32)
        for j in range(N_CHIPS - 1):
            acc = acc + r_ref[j].astype(F32)
        o_ref[...] = acc

    return pl.pallas_call(
        body,
        grid_spec=pltpu.PrefetchScalarGridSpec(
            num_scalar_prefetch=1,
            grid=(h // th,),
            in_specs=[pl.BlockSpec((None, th, c), lambda i, chip_ref: (chip_ref[0], i, 0)),
                      pl.BlockSpec((N_CHIPS - 1, th, c), lambda i, chip_ref: (0, i, 0))],
            out_specs=pl.BlockSpec((th, c), lambda i, chip_ref: (i, 0)),
        ),
        out_shape=jax.ShapeDtypeStruct((h, c), F32),
        compiler_params=_params("parallel"),
        name=name,
    )(chip, hsum, r2)


def _sum_devices(parts):
    _, r, c = parts.shape

    def body(p_ref, o_ref):
        acc = p_ref[0]
        for j in range(1, N_DEV):
            acc = acc + p_ref[j]
        o_ref[...] = acc

    return pl.pallas_call(
        body,
        out_shape=jax.ShapeDtypeStruct((r, c), F32),
        compiler_params=pltpu.CompilerParams(vmem_limit_bytes=VMEM_LIMIT_BYTES),
        name="sum_devices",
    )(parts)


ANY = pl.BlockSpec(memory_space=pl.ANY)


def _place():
    x, y, c = lax.axis_index("x"), lax.axis_index("y"), lax.axis_index("c")
    others = [(1 - x, y), (x, 1 - y), (1 - x, 1 - y)]
    return x, y, c, others


def _remote(src, dst, send_sem, recv_sem, to):
    return pltpu.make_async_remote_copy(src_ref=src, dst_ref=dst, send_sem=send_sem, recv_sem=recv_sem,
                                        device_id=to, device_id_type=MESH)


def _gather_weights(shards):
    n = len(shards)

    def body(*refs):
        ins, outs = refs[:n], refs[n:2 * n]
        send_sems, recv_sems, local_sems = refs[2 * n:]
        x, y, c, others = _place()
        me = 2 * x + y
        sibling = (x, y, 1 - c)
        local, sent = [], []
        for i in range(n):
            h = ins[i].shape[0] // 2
            mine = pl.ds(pl.multiple_of(c * h, BF16_SUBLANES), h)
            cp = pltpu.make_async_copy(ins[i], outs[i].at[me], local_sems.at[i])
            cp.start()
            local.append(cp)
            for j, (ox, oy) in enumerate(others):
                cp = _remote(ins[i].at[mine], outs[i].at[me, mine], send_sems.at[6 * i + j], recv_sems.at[6 * i + j],
                             (ox, oy, c))
                cp.start()
                sent.append(cp)
        for i in range(n):
            h = ins[i].shape[0] // 2
            mine = pl.ds(pl.multiple_of(c * h, BF16_SUBLANES), h)
            for j, (ox, oy) in enumerate(others):
                blk = outs[i].at[2 * ox + oy, mine]
                _remote(blk, blk, send_sems.at[6 * i + j], recv_sems.at[6 * i + j], (ox, oy, c)).wait_recv()
                cp = _remote(blk, blk, send_sems.at[6 * i + 3 + j], recv_sems.at[6 * i + 3 + j], sibling)
                cp.start()
                sent.append(cp)
        for i in range(n):
            h = ins[i].shape[0] // 2
            theirs = pl.ds(pl.multiple_of((1 - c) * h, BF16_SUBLANES), h)
            for j, (ox, oy) in enumerate(others):
                blk = outs[i].at[2 * ox + oy, theirs]
                _remote(blk, blk, send_sems.at[6 * i + 3 + j], recv_sems.at[6 * i + 3 + j], sibling).wait_recv()
        for cp in sent:
            cp.wait_send()
        for cp in local:
            cp.wait()

    return pl.pallas_call(
        body,
        in_specs=[ANY] * n,
        out_specs=[ANY] * n,
        out_shape=[jax.ShapeDtypeStruct((N_CHIPS,) + s.shape, s.dtype) for s in shards],
        scratch_shapes=[pltpu.SemaphoreType.DMA((6 * n,)), pltpu.SemaphoreType.DMA((6 * n,)),
                        pltpu.SemaphoreType.DMA((n,))],
        name="gather_weights",
    )(*shards)


def _swap_halves(grads):
    n = len(grads)

    def body(*refs):
        ins, outs = refs[:n], refs[n:2 * n]
        send_sems, recv_sems = refs[2 * n:]
        x, y, c, _ = _place()
        copies = [_remote(ins[i].at[:, 1 - c], outs[i], send_sems.at[i], recv_sems.at[i], (x, y, 1 - c))
                  for i in range(n)]
        for cp in copies:
            cp.start()
        for cp in copies:
            cp.wait()

    return pl.pallas_call(
        body,
        in_specs=[ANY] * n,
        out_specs=[ANY] * n,
        out_shape=[jax.ShapeDtypeStruct((g.shape[0],) + g.shape[2:], g.dtype) for g in grads],
        scratch_shapes=[pltpu.SemaphoreType.DMA((n,)), pltpu.SemaphoreType.DMA((n,))],
        name="swap_halves",
    )(*grads)


def _scatter_chips(sums):
    n = len(sums)
    k = N_CHIPS - 1

    def body(*refs):
        ins, outs = refs[:n], refs[n:2 * n]
        send_sems, recv_sems = refs[2 * n:]
        _, _, c, others = _place()
        copies = [_remote(ins[i].at[2 * ox + oy], outs[i].at[j], send_sems.at[k * i + j], recv_sems.at[k * i + j],
                          (ox, oy, c))
                  for i in range(n) for j, (ox, oy) in enumerate(others)]
        for cp in copies:
            cp.start()
        for cp in copies:
            cp.wait()

    return pl.pallas_call(
        body,
        in_specs=[ANY] * n,
        out_specs=[ANY] * n,
        out_shape=[jax.ShapeDtypeStruct((k,) + g.shape[1:], g.dtype) for g in sums],
        scratch_shapes=[pltpu.SemaphoreType.DMA((k * n,)), pltpu.SemaphoreType.DMA((k * n,))],
        name="scatter_chips",
    )(*sums)


def _join_halves(finals):
    n = len(finals)

    def body(*refs):
        ins, outs = refs[:n], refs[n:2 * n]
        send_sems, recv_sems, local_sems = refs[2 * n:]
        x, y, c, _ = _place()
        local = [pltpu.make_async_copy(ins[i], outs[i].at[c], local_sems.at[i]) for i in range(n)]
        sends = [_remote(ins[i], outs[i].at[c], send_sems.at[i], recv_sems.at[i], (x, y, 1 - c)) for i in range(n)]
        for cp in local + sends:
            cp.start()
        for i in range(n):
            sends[i].wait_send()
            _remote(ins[i], outs[i].at[1 - c], send_sems.at[i], recv_sems.at[i], (x, y, 1 - c)).wait_recv()
            local[i].wait()

    return pl.pallas_call(
        body,
        in_specs=[ANY] * n,
        out_specs=[ANY] * n,
        out_shape=[jax.ShapeDtypeStruct((2,) + f.shape, f.dtype) for f in finals],
        scratch_shapes=[pltpu.SemaphoreType.DMA((n,)), pltpu.SemaphoreType.DMA((n,)), pltpu.SemaphoreType.DMA((n,))],
        name="join_halves",
    )(*finals)


def _gather_small(vec):
    k = N_DEV - 1

    def body(v_ref, o_ref, send_sems, recv_sems, local_sem):
        x, y, c, _ = _place()
        me = 4 * x + 2 * y + c
        local = pltpu.make_async_copy(v_ref, o_ref.at[me], local_sem)
        local.start()
        peers = [(x ^ (r >> 2 & 1), y ^ (r >> 1 & 1), c ^ (r & 1)) for r in range(1, N_DEV)]
        sends = [_remote(v_ref, o_ref.at[me], send_sems.at[j], recv_sems.at[j], p) for j, p in enumerate(peers)]
        for cp in sends:
            cp.start()
        for j, (px, py, pc) in enumerate(peers):
            sends[j].wait_send()
            blk = o_ref.at[4 * px + 2 * py + pc]
            _remote(blk, blk, send_sems.at[j], recv_sems.at[j], (px, py, pc)).wait_recv()
        local.wait()

    return pl.pallas_call(
        body,
        in_specs=[ANY],
        out_specs=ANY,
        out_shape=jax.ShapeDtypeStruct((N_DEV,) + vec.shape, vec.dtype),
        scratch_shapes=[pltpu.SemaphoreType.DMA((k,)), pltpu.SemaphoreType.DMA((k,)), pltpu.SemaphoreType.DMA(())],
        name="gather_small",
    )(vec)


def _reduce_scatter(grads, core, chip):
    split = [g.reshape(N_CHIPS, 2, g.shape[1] // 2, g.shape[2]) for g in grads]
    theirs = _swap_halves(split)
    sums = [_add_sibling(g, r, core, "add_sibling_%d" % i) for i, (g, r) in enumerate(zip(split, theirs))]
    others = _scatter_chips(sums)
    finals = [_add_chips(s, r, chip, "add_chips_%d" % i) for i, (s, r) in enumerate(zip(sums, others))]
    joined = _join_halves(finals)
    return [j.reshape(2 * j.shape[1], j.shape[2]) for j in joined]


def _local_step(x, mem, target, w):
    d = x.shape[1]
    nf = w["w_br_fox"].shape[0] // HEAD_DIM
    nsb = w["w_br_sb"].shape[0] // HEAD_DIM
    nm = w["w_br_mem"].shape[0] // HEAD_DIM
    fq, fk, fv = 0, nf, 2 * nf
    sq, sk, sv = 3 * nf, 3 * nf + nsb, 3 * nf + 2 * nsb
    mq = 3 * nf + 3 * nsb
    gates = mq + nm

    h, rstd1 = _rms_fwd(x, w["g_mix"], "rms_mix_fwd")
    proj = _mm(h, w["w_in_main"], "nn", BF16, "proj_main")
    f_logit = _mm(h, w["w_in_f"], "nn", F32, "proj_forget")
    c_sum = _forget_fwd(f_logit, w["b_forget"])
    c_t = c_sum[:, :nf].T
    c_col, c_row = c_t[:, :, None], c_t[:, None, :]
    qn = _headnorm_fwd(proj, fq, nf, w["g_q_fox"], "fox_qnorm_fwd")
    kn = _headnorm_fwd(proj, fk, nf, w["g_k_fox"], "fox_knorm_fwd")
    o_fox, lse = _fox_fwd(qn, kn, proj, fv, c_col, c_row, nf)
    o_sb = _sb_fwd(proj, sq, sk, sv, nsb)
    memn, rstd_m = _rms_fwd(mem, w["g_mem"], "rms_mem_fwd")
    mkv = _mm(memn, w["w_mem_kv"], "nn", BF16, "mem_kv")
    kmn = _headnorm_fwd(mkv, 0, nm, w["g_k_mem"], "mem_knorm_fwd")
    qmn = _headnorm_fwd(proj, mq, nm, w["g_q_mem"], "mem_qnorm_fwd")
    o_mem = _mem_fwd(qmn, kmn, mkv, nm)
    p0 = _mm(o_fox, w["w_br_fox"], "nn", F32, "branch_fox")
    p1 = _mm(o_sb, w["w_br_sb"], "nn", F32, "branch_sb")
    p2 = _mm(o_mem, w["w_br_mem"], "nn", F32, "branch_mem")
    merged = _merge_fwd(p0, p1, p2, proj, gates, w["b_gate"])
    x1 = _mm(merged, w["w_out"], "nn", F32, "out_proj", residual=x)
    h2, rstd2 = _rms_fwd(x1, w["g_ffn"], "rms_ffn_fwd")
    up = _mm(h2, w["w_up"], "nn", BF16, "ffn_up")
    act = _conv_act_fwd(up, w["conv_w"], w["conv_b"])
    y = _mm(act, w["w_down"], "nn", F32, "ffn_down", residual=x1)
    dy, dyb, lparts = _loss_grad(y, target)
    loss = (0.5 / d) * jnp.sum(lparts[::8, 0])

    g = {}
    dact = _mm(dyb, w["w_down"], "nt", BF16, "ffn_down_dx")
    g["w_down"] = _mm(act, dyb, "tn", BF16, "ffn_down_dw")
    dug, duv, dwg, dwv, dbg, dbv = _conv_act_bwd(up, w["conv_w"], w["conv_b"], dact)
    dup = jnp.concatenate([dug, duv], axis=1)
    g["conv_w"] = jnp.concatenate([dwg, dwv], axis=1)
    g["conv_b"] = jnp.concatenate([dbg, dbv], axis=1)
    dh2 = _mm(dup, w["w_up"], "nt", BF16, "ffn_up_dx")
    g["w_up"] = _mm(h2, dup, "tn", BF16, "ffn_up_dw")
    dx1, dx1b, g["g_ffn"] = _rms_bwd(dh2, x1, rstd2, w["g_ffn"], dy, "rms_ffn_bwd")
    dmerged = _mm(dx1b, w["w_out"], "nt", BF16, "out_proj_dx")
    g["w_out"] = _mm(merged, dx1b, "tn", BF16, "out_proj_dw")
    dp0, dp1, dp2, dga, dgb, dgc, g["b_gate"] = _merge_bwd(dmerged, p0, p1, p2, proj, gates, w["b_gate"])
    do_fox = _mm(dp0, w["w_br_fox"], "nt", BF16, "branch_fox_dx")
    do_sb = _mm(dp1, w["w_br_sb"], "nt", BF16, "branch_sb_dx")
    do_mem = _mm(dp2, w["w_br_mem"], "nt", BF16, "branch_mem_dx")
    g["w_br_fox"] = _mm(o_fox, dp0, "tn", BF16, "branch_fox_dw")
    g["w_br_sb"] = _mm(o_sb, dp1, "tn", BF16, "branch_sb_dw")
    g["w_br_mem"] = _mm(o_mem, dp2, "tn", BF16, "branch_mem_dw")

    dqn, dkn, dfv, dcs = _fox_bwd(qn, kn, proj, fv, c_col, c_row, o_fox, do_fox, lse, nf)
    dfq, g["g_q_fox"] = _headnorm_bwd(dqn, proj, fq, nf, w["g_q_fox"], "fox_qnorm_bwd")
    dfk, g["g_k_fox"] = _headnorm_bwd(dkn, proj, fk, nf, w["g_k_fox"], "fox_knorm_bwd")
    dc = jnp.pad(-dcs[:, 0, :].T, ((0, 0), (0, LANES - nf)))
    df, g["b_forget"] = _forget_bwd(dc, f_logit, w["b_forget"])
    dsq, dsk, dsv = _sb_bwd(proj, sq, sk, sv, do_sb, nsb)
    dqmn, dkmn, dvm = _mem_bwd(qmn, kmn, mkv, do_mem, nm)
    dmq, g["g_q_mem"] = _headnorm_bwd(dqmn, proj, mq, nm, w["g_q_mem"], "mem_qnorm_bwd")
    dkm, g["g_k_mem"] = _headnorm_bwd(dkmn, mkv, 0, nm, w["g_k_mem"], "mem_knorm_bwd")
    dmkv = jnp.concatenate([dkm, dvm.astype(BF16)], axis=1)
    g["w_mem_kv"] = _mm(memn, dmkv, "tn", BF16, "mem_kv_dw")
    dmemn = _mm(dmkv, w["w_mem_kv"], "nt", BF16, "mem_kv_dx")
    _, _, g["g_mem"] = _rms_bwd(dmemn, mem, rstd_m, w["g_mem"], None, "rms_mem_bwd")

    dproj = jnp.concatenate([dfq, dfk, dfv.astype(BF16), dsq.astype(BF16), dsk.astype(BF16), dsv.astype(BF16),
                             dmq, dga, dgb, dgc], axis=1)
    dfb = df.astype(BF16)
    dh_main = _mm(dproj, w["w_in_main"], "nt", F32, "proj_main_dx")
    dh = _mm(dfb, w["w_in_f"], "nt", F32, "proj_forget_dx", residual=dh_main)
    g["w_in_main"] = _mm(h, dproj, "tn", BF16, "proj_main_dw")
    g["w_in_f"] = _mm(h, dfb, "tn", BF16, "proj_forget_dw")
    grad_x, _, g["g_mix"] = _rms_bwd(dh, x, rstd1, w["g_mix"], dx1, "rms_mix_bwd")
    return loss, grad_x, g


ROW_SHARDED = ("w_in_main", "w_in_f", "w_mem_kv", "w_out", "w_down")
COL_SHARDED = ("w_br_fox", "w_br_sb", "w_br_mem", "w_up")
BIG = ROW_SHARDED + COL_SHARDED
SMALL = ("g_mix", "b_forget", "g_q_fox", "g_k_fox", "g_mem", "g_q_mem", "g_k_mem", "b_gate", "g_ffn", "conv_w",
         "conv_b")
SMALL_SHARDED = ("b_gate", "conv_w")
PACK_ROWS = 8


def _pack(arrs):
    flat = jnp.concatenate([a.reshape(-1) for a in arrs])
    unit = PACK_ROWS * LANES
    flat = jnp.pad(flat, (0, -flat.shape[0] % unit))
    return flat.reshape(-1, LANES)


def _unpack(packed, shapes):
    flat = packed.reshape(-1)
    out, at = [], 0
    for s in shapes:
        n = 1
        for dim in s:
            n *= dim
        out.append(flat[at:at + n].reshape(s))
        at += n
    return out


def kernel(x, mem, g_mix, w_in, b_forget, g_q_fox, g_k_fox, g_mem, w_mem_kv, g_q_mem, g_k_mem, w_br_fox, w_br_sb, w_br_mem, b_gate, w_out, g_ffn, w_up, conv_w, conv_b, w_down, loss_target, m_g_mix, m_w_in, m_b_forget, m_g_q_fox, m_g_k_fox, m_g_mem, m_w_mem_kv, m_g_q_mem, m_g_k_mem, m_w_br_fox, m_w_br_sb, m_w_br_mem, m_b_gate, m_w_out, m_g_ffn, m_w_up, m_conv_w, m_conv_b, m_w_down, v_g_mix, v_w_in, v_b_forget, v_g_q_fox, v_g_k_fox, v_g_mem, v_w_mem_kv, v_g_q_mem, v_g_k_mem, v_w_br_fox, v_w_br_sb, v_w_br_mem, v_b_gate, v_w_out, v_g_ffn, v_w_up, v_conv_w, v_conv_b, v_w_down):
    given = dict(g_mix=g_mix, w_in=w_in, b_forget=b_forget, g_q_fox=g_q_fox, g_k_fox=g_k_fox, g_mem=g_mem,
                 w_mem_kv=w_mem_kv, g_q_mem=g_q_mem, g_k_mem=g_k_mem, w_br_fox=w_br_fox, w_br_sb=w_br_sb,
                 w_br_mem=w_br_mem, b_gate=b_gate, w_out=w_out, g_ffn=g_ffn, w_up=w_up, conv_w=conv_w, conv_b=conv_b,
                 w_down=w_down)
    m_in = dict(g_mix=m_g_mix, w_in=m_w_in, b_forget=m_b_forget, g_q_fox=m_g_q_fox, g_k_fox=m_g_k_fox, g_mem=m_g_mem,
                w_mem_kv=m_w_mem_kv, g_q_mem=m_g_q_mem, g_k_mem=m_g_k_mem, w_br_fox=m_w_br_fox, w_br_sb=m_w_br_sb,
                w_br_mem=m_w_br_mem, b_gate=m_b_gate, w_out=m_w_out, g_ffn=m_g_ffn, w_up=m_w_up, conv_w=m_conv_w,
                conv_b=m_conv_b, w_down=m_w_down)
    v_in = dict(g_mix=v_g_mix, w_in=v_w_in, b_forget=v_b_forget, g_q_fox=v_g_q_fox, g_k_fox=v_g_k_fox, g_mem=v_g_mem,
                w_mem_kv=v_w_mem_kv, g_q_mem=v_g_q_mem, g_k_mem=v_g_k_mem, w_br_fox=v_w_br_fox, w_br_sb=v_w_br_sb,
                w_br_mem=v_w_br_mem, b_gate=v_b_gate, w_out=v_w_out, g_ffn=v_g_ffn, w_up=v_w_up, conv_w=v_conv_w,
                conv_b=v_conv_b, w_down=v_w_down)
    layered = {k: a.ndim == 3 for k, a in given.items()}
    drop = lambda a: a[0] if a.ndim == 3 else a
    given = {k: drop(a) for k, a in given.items()}
    m_in = {k: drop(a) for k, a in m_in.items()}
    v_in = {k: drop(a) for k, a in v_in.items()}

    xi, yi, ci = lax.axis_index("x"), lax.axis_index("y"), lax.axis_index("c")
    chip = (2 * xi + yi).astype(jnp.int32)
    core_arr = ci.astype(jnp.int32).reshape(1)
    chip_arr = chip.reshape(1)

    nf = given["b_forget"].shape[1]
    cut = 3 * given["w_br_fox"].shape[0]

    shard = {
        "w_in_main": jnp.concatenate([given["w_in"][:, :cut], given["w_in"][:, cut + nf:]], axis=1).astype(BF16),
        "w_in_f": jnp.pad(given["w_in"][:, cut:cut + nf], ((0, 0), (0, LANES - nf))).astype(BF16),
    }
    for name in BIG[2:]:
        shard[name] = given[name].astype(BF16)
    full = dict(zip(BIG, _gather_weights([shard[name] for name in BIG])))
    w = {}
    for name in ROW_SHARDED:
        a = full[name]
        w[name] = a.reshape(N_CHIPS * a.shape[1], a.shape[2])
    for name in COL_SHARDED:
        a = full[name]
        w[name] = a.transpose(1, 0, 2).reshape(a.shape[1], N_CHIPS * a.shape[2])
    small_shapes = [given[name].shape for name in SMALL_SHARDED]
    small_parts = _gather_small(_pack([given[name] for name in SMALL_SHARDED]))[0::2]
    per_chip = [_unpack(small_parts[j], small_shapes) for j in range(N_CHIPS)]
    for k, name in enumerate(SMALL_SHARDED):
        w[name] = jnp.concatenate([per_chip[j][k] for j in range(N_CHIPS)], axis=1)
    for name in SMALL:
        if name not in SMALL_SHARDED:
            w[name] = given[name]
    w["b_forget"] = jnp.pad(given["b_forget"], ((0, 0), (0, LANES - nf)))

    loss, grad_x, g = _local_step(x[0], mem[0], loss_target[0], w)
    loss = lax.psum(loss, ("x", "y", "c"))

    by_shard = []
    for name in BIG:
        a = g[name]
        if name in ROW_SHARDED:
            by_shard.append(a.reshape(N_CHIPS, a.shape[0] // N_CHIPS, a.shape[1]))
        else:
            by_shard.append(a.reshape(a.shape[0], N_CHIPS, a.shape[1] // N_CHIPS).transpose(1, 0, 2))
    summed = dict(zip(BIG, _reduce_scatter(by_shard, core_arr, chip_arr)))
    grads = {name: summed[name] for name in BIG[2:]}
    grads["w_in"] = jnp.concatenate([summed["w_in_main"][:, :cut], summed["w_in_f"][:, :nf],
                                     summed["w_in_main"][:, cut:]], axis=1)

    g["b_forget"] = g["b_forget"][:, :nf]
    small_full_shapes = [g[name].shape for name in SMALL]
    small_sum = _unpack(_sum_devices(_gather_small(_pack([g[name] for name in SMALL]))), small_full_shapes)
    for name, a in zip(SMALL, small_sum):
        if name in SMALL_SHARDED:
            width = given[name].shape[1]
            a = lax.dynamic_slice_in_dim(a, chip * width, width, axis=1)
        grads[name] = a

    delta, new_m, new_v = {}, {}, {}
    for name in WEIGHTS:
        if name not in SMALL:
            delta[name], new_m[name], new_v[name] = _adamw(given[name], grads[name], m_in[name], v_in[name],
                                                           "adamw_" + name)
    shapes = [given[name].shape for name in SMALL]
    packed = [_pack([src[name] for name in SMALL]) for src in (given, grads, m_in, v_in)]
    for dst, res in zip((delta, new_m, new_v), _adamw(*packed, "adamw_small")):
        for name, a in zip(SMALL, _unpack(res, shapes)):
            dst[name] = a

    out = [loss, grad_x[None]]
    for src in (grads, delta, new_m, new_v):
        out.extend(src[name][None] if layered[name] else src[name] for name in WEIGHTS)
    return tuple(out)
```

```python
import functools

import jax
import jax.numpy as jnp
from jax import lax
from jax.experimental import pallas as pl
from jax.experimental.pallas import tpu as pltpu

F32 = jnp.float32
BF16 = jnp.bfloat16

HEAD_DIM = 128
EPS = 1e-6
NEG_BIG = -1e30

ADAM_LR = 0.001
ADAM_B1 = 0.9
ADAM_B2 = 0.999
ADAM_EPS = 1e-08
ADAM_WD = 0.01
ADAM_STEP = 10

LANES = 128
BF16_SUBLANES = 16
VMEM_LIMIT_BYTES = 56 * 1024 * 1024
MM_TILE = 1024
ATT_TILE = 256
ROW_TILE = 256
HEADNORM_ROWS = 512
COL_TILE = 512
ADAM_BLOCK_BYTES = 1 << 20

N_CHIPS = 4
N_DEV = 8
MESH = pl.DeviceIdType.MESH

IN_NAMES = ['x', 'mem', 'g_mix', 'w_in', 'b_forget', 'g_q_fox', 'g_k_fox', 'g_mem', 'w_mem_kv', 'g_q_mem', 'g_k_mem',
            'w_br_fox', 'w_br_sb', 'w_br_mem', 'b_gate', 'w_out', 'g_ffn', 'w_up', 'conv_w', 'conv_b', 'w_down']
WEIGHTS = IN_NAMES[2:]


def _tile(n, target):
    if n <= target:
        return n
    for t in range(target - target % LANES, LANES - 1, -LANES):
        if n % t == 0:
            return t
    return n


def _params(*sem):
    return pltpu.CompilerParams(dimension_semantics=sem, vmem_limit_bytes=VMEM_LIMIT_BYTES)


def _log_sigmoid(z):
    return jnp.minimum(z, 0.0) - jnp.log(1.0 + jnp.exp(-jnp.abs(z)))


def _split2(v):
    hi = v.astype(BF16)
    lo = (v - hi.astype(F32)).astype(BF16)
    return hi, lo


def _split3(v):
    hi = v.astype(BF16)
    r = v - hi.astype(F32)
    mid = r.astype(BF16)
    lo = (r - mid.astype(F32)).astype(BF16)
    return hi, mid, lo


def _dot(a, b):
    return lax.dot_general(a, b, (((1,), (0,)), ((), ())), preferred_element_type=F32)


def _dot_nt(a, b):
    return lax.dot_general(a, b, (((1,), (1,)), ((), ())), preferred_element_type=F32)


def _dot_tn(a, b):
    return lax.dot_general(a, b, (((0,), (0,)), ((), ())), preferred_element_type=F32)


def _mm(a, b, mode, out_dtype, name, residual=None):
    if mode == "nn":
        (m, k), (k2, n) = a.shape, b.shape
    elif mode == "nt":
        (m, k), (n, k2) = a.shape, b.shape
    else:
        (k, m), (k2, n) = a.shape, b.shape
    assert k == k2, (a.shape, b.shape, mode)
    tm, tn, tk = _tile(m, MM_TILE), _tile(n, MM_TILE), _tile(k, MM_TILE)
    nk = k // tk
    dot = {"nn": _dot, "nt": _dot_nt, "tn": _dot_tn}[mode]
    has_res = residual is not None

    def body(*refs):
        if has_res:
            a_ref, b_ref, r_ref, o_ref = refs[:4]
        else:
            a_ref, b_ref, o_ref = refs[:3]
            r_ref = None

        def finish(acc):
            if has_res:
                acc = acc + r_ref[...]
            o_ref[...] = acc.astype(o_ref.dtype)

        part = dot(a_ref[...], b_ref[...])
        if nk == 1:
            finish(part)
        else:
            acc_ref = refs[-1]
            kk = pl.program_id(2)

            @pl.when(kk == 0)
            def _():
                acc_ref[...] = part

            @pl.when(kk > 0)
            def _():
                acc_ref[...] += part

            @pl.when(kk == nk - 1)
            def _():
                finish(acc_ref[...])

    if mode == "tn":
        a_spec = pl.BlockSpec((tk, tm), lambda j, i, kk: (kk, i))
    else:
        a_spec = pl.BlockSpec((tm, tk), lambda j, i, kk: (i, kk))
    if mode == "nt":
        b_spec = pl.BlockSpec((tn, tk), lambda j, i, kk: (j, kk))
    else:
        b_spec = pl.BlockSpec((tk, tn), lambda j, i, kk: (kk, j))
    o_spec = pl.BlockSpec((tm, tn), lambda j, i, kk: (i, j))
    in_specs = [a_spec, b_spec] + ([o_spec] if has_res else [])
    args = (a, b) + ((residual,) if has_res else ())
    return pl.pallas_call(
        body,
        grid=(n // tn, m // tm, nk),
        in_specs=in_specs,
        out_specs=o_spec,
        out_shape=jax.ShapeDtypeStruct((m, n), out_dtype),
        scratch_shapes=[pltpu.VMEM((tm, tn), F32)] if nk > 1 else [],
        compiler_params=_params("parallel", "parallel", "arbitrary"),
        name=name,
    )(*args)


def _rms_fwd(x, g, name):
    s, d = x.shape
    tm = _tile(s, ROW_TILE)

    def body(x_ref, g_ref, h_ref, r_ref):
        xf = x_ref[...]
        r = lax.rsqrt(jnp.mean(xf * xf, axis=-1, keepdims=True) + EPS)
        h_ref[...] = ((xf * r) * g_ref[...]).astype(BF16)
        r_ref[...] = r

    return pl.pallas_call(
        body,
        grid=(s // tm,),
        in_specs=[pl.BlockSpec((tm, d), lambda i: (i, 0)), pl.BlockSpec((1, d), lambda i: (0, 0))],
        out_specs=[pl.BlockSpec((tm, d), lambda i: (i, 0)), pl.BlockSpec((tm, 1), lambda i: (i, 0))],
        out_shape=[jax.ShapeDtypeStruct((s, d), BF16), jax.ShapeDtypeStruct((s, 1), F32)],
        compiler_params=_params("parallel"),
        name=name,
    )(x, g)


def _rms_bwd(dh, x, rstd, g, res, name):
    s, d = x.shape
    tm = _tile(s, ROW_TILE)
    has_res = res is not None

    def body(*refs):
        if has_res:
            dh_ref, x_ref, r_ref, g_ref, res_ref, dx_ref, dxb_ref, dg_ref = refs
        else:
            dh_ref, x_ref, r_ref, g_ref, dx_ref, dxb_ref, dg_ref = refs
        dhf = dh_ref[...].astype(F32)
        xhat = x_ref[...] * r_ref[...]
        dy = dhf * g_ref[...]
        dx = r_ref[...] * (dy - xhat * jnp.mean(dy * xhat, axis=-1, keepdims=True))
        if has_res:
            dx = dx + res_ref[...]
        dx_ref[...] = dx
        dxb_ref[...] = dx.astype(BF16)
        part = jnp.sum(dhf * xhat, axis=0, keepdims=True)

        @pl.when(pl.program_id(0) == 0)
        def _():
            dg_ref[...] = part

        @pl.when(pl.program_id(0) > 0)
        def _():
            dg_ref[...] += part

    row = pl.BlockSpec((tm, d), lambda i: (i, 0))
    vec = pl.BlockSpec((1, d), lambda i: (0, 0))
    in_specs = [row, row, pl.BlockSpec((tm, 1), lambda i: (i, 0)), vec] + ([row] if has_res else [])
    args = (dh, x, rstd, g) + ((res,) if has_res else ())
    return pl.pallas_call(
        body,
        grid=(s // tm,),
        in_specs=in_specs,
        out_specs=[row, row, vec],
        out_shape=[jax.ShapeDtypeStruct((s, d), F32), jax.ShapeDtypeStruct((s, d), BF16),
                   jax.ShapeDtypeStruct((1, d), F32)],
        compiler_params=_params("arbitrary"),
        name=name,
    )(*args)


def _headnorm_fwd(src, col0, nheads, g, name):
    s = src.shape[0]
    tm = _tile(s, HEADNORM_ROWS)
    w = nheads * HEAD_DIM
    assert col0 % nheads == 0

    def body(x_ref, g_ref, o_ref):
        for hh in range(nheads):
            xf = _head(x_ref, hh).astype(F32)
            r = lax.rsqrt(jnp.mean(xf * xf, axis=-1, keepdims=True) + EPS)
            o_ref[:, hh * HEAD_DIM:(hh + 1) * HEAD_DIM] = ((xf * r) * g_ref[...]).astype(BF16)

    return pl.pallas_call(
        body,
        grid=(s // tm,),
        in_specs=[pl.BlockSpec((tm, w), lambda i: (i, col0 // nheads)),
                  pl.BlockSpec((1, HEAD_DIM), lambda i: (0, 0))],
        out_specs=pl.BlockSpec((tm, w), lambda i: (i, 0)),
        out_shape=jax.ShapeDtypeStruct((s, w), BF16),
        compiler_params=_params("parallel"),
        name=name,
    )(src, g)


def _headnorm_bwd(dxn, src, col0, nheads, g, name):
    s = src.shape[0]
    tm = _tile(s, HEADNORM_ROWS)
    w = nheads * HEAD_DIM
    assert col0 % nheads == 0

    def body(d_ref, x_ref, g_ref, dx_ref, dg_ref):
        part = jnp.zeros((1, HEAD_DIM), F32)
        for hh in range(nheads):
            xf = _head(x_ref, hh).astype(F32)
            r = lax.rsqrt(jnp.mean(xf * xf, axis=-1, keepdims=True) + EPS)
            xhat = xf * r
            dn = _head(d_ref, hh).astype(F32)
            dy = dn * g_ref[...]
            dx = r * (dy - xhat * jnp.mean(dy * xhat, axis=-1, keepdims=True))
            dx_ref[:, hh * HEAD_DIM:(hh + 1) * HEAD_DIM] = dx.astype(BF16)
            part = part + jnp.sum(dn * xhat, axis=0, keepdims=True)

        @pl.when(pl.program_id(0) == 0)
        def _():
            dg_ref[...] = part

        @pl.when(pl.program_id(0) > 0)
        def _():
            dg_ref[...] += part

    return pl.pallas_call(
        body,
        grid=(s // tm,),
        in_specs=[pl.BlockSpec((tm, w), lambda i: (i, 0)),
                  pl.BlockSpec((tm, w), lambda i: (i, col0 // nheads)),
                  pl.BlockSpec((1, HEAD_DIM), lambda i: (0, 0))],
        out_specs=[pl.BlockSpec((tm, w), lambda i: (i, 0)),
                   pl.BlockSpec((1, HEAD_DIM), lambda i: (0, 0))],
        out_shape=[jax.ShapeDtypeStruct((s, w), BF16), jax.ShapeDtypeStruct((1, HEAD_DIM), F32)],
        compiler_params=_params("arbitrary"),
        name=name,
    )(dxn, src, g)


def _tri(t, lower_inclusive):
    r = lax.broadcasted_iota(jnp.int32, (t, t), 0)
    c = lax.broadcasted_iota(jnp.int32, (t, t), 1)
    keep = (c <= r) if lower_inclusive else (c >= r)
    return jnp.where(keep, 1.0, 0.0).astype(BF16)


def _forget_fwd(f_logit, b_pad):
    s = f_logit.shape[0]
    t = _tile(s, ATT_TILE)

    def body(f_ref, b_ref, c_ref, carry):
        @pl.when(pl.program_id(0) == 0)
        def _():
            carry[...] = jnp.zeros_like(carry)

        lf = _log_sigmoid(f_ref[...] + b_ref[...])
        tri = _tri(t, True)
        acc = carry[...]
        for part in _split3(lf):
            acc = acc + _dot(tri, part)
        c_ref[...] = acc
        carry[...] += jnp.sum(lf, axis=0, keepdims=True)

    return pl.pallas_call(
        body,
        grid=(s // t,),
        in_specs=[pl.BlockSpec((t, LANES), lambda i: (i, 0)), pl.BlockSpec((1, LANES), lambda i: (0, 0))],
        out_specs=pl.BlockSpec((t, LANES), lambda i: (i, 0)),
        out_shape=jax.ShapeDtypeStruct((s, LANES), F32),
        scratch_shapes=[pltpu.VMEM((1, LANES), F32)],
        compiler_params=_params("arbitrary"),
        name="forget_fwd",
    )(f_logit, b_pad)


def _forget_bwd(dc, f_logit, b_pad):
    s = f_logit.shape[0]
    t = _tile(s, ATT_TILE)
    nb = s // t

    def body(dc_ref, f_ref, b_ref, df_ref, db_ref, carry):
        @pl.when(pl.program_id(0) == 0)
        def _():
            carry[...] = jnp.zeros_like(carry)
            db_ref[...] = jnp.zeros_like(db_ref)

        d = dc_ref[...]
        tri = _tri(t, False)
        acc = carry[...]
        for part in _split3(d):
            acc = acc + _dot(tri, part)
        z = f_ref[...] + b_ref[...]
        df = acc * jnp.exp(_log_sigmoid(-z))
        df_ref[...] = df
        db_ref[...] += jnp.sum(df, axis=0, keepdims=True)
        carry[...] += jnp.sum(d, axis=0, keepdims=True)

    rev = pl.BlockSpec((t, LANES), lambda i: (nb - 1 - i, 0))
    vec = pl.BlockSpec((1, LANES), lambda i: (0, 0))
    return pl.pallas_call(
        body,
        grid=(nb,),
        in_specs=[rev, rev, vec],
        out_specs=[rev, vec],
        out_shape=[jax.ShapeDtypeStruct((s, LANES), F32), jax.ShapeDtypeStruct((1, LANES), F32)],
        scratch_shapes=[pltpu.VMEM((1, LANES), F32)],
        compiler_params=_params("arbitrary"),
        name="forget_bwd",
    )(dc, f_logit, b_pad)


HEAD_GROUP = 2
GROUP_W = HEAD_GROUP * HEAD_DIM


def _head(ref, hh, rows=slice(None)):
    return ref[rows, hh * HEAD_DIM:(hh + 1) * HEAD_DIM]


def _tri_mask(t, strict):
    r = lax.broadcasted_iota(jnp.int32, (t, t), 0)
    c = lax.broadcasted_iota(jnp.int32, (t, t), 1)
    return (c < r) if strict else (c <= r)


def _fox_fwd(qn, kn, proj, colv, c_col, c_row, nheads):
    s = qn.shape[0]
    t = _tile(s, ATT_TILE)
    scale = HEAD_DIM ** -0.5
    hg = HEAD_GROUP
    assert nheads % hg == 0 and colv % hg == 0

    def body(q_ref, k_ref, v_ref, cc_ref, cr_ref, o_ref, of_ref, lse_ref):
        qi = pl.program_id(1)
        causal = _tri_mask(t, False)

        def tile(kj, carry, diagonal):
            off = pl.multiple_of(kj * t, t)
            out = []
            for hh in range(hg):
                m, l, acc = carry[hh]
                k = _head(k_ref, hh, pl.ds(off, t))
                v = _head(v_ref, hh, pl.ds(off, t))
                sc = _dot_nt(_head(q_ref, hh), k) * scale + (cc_ref[hh] - cr_ref[hh, :, pl.ds(off, t)])
                if diagonal:
                    sc = jnp.where(causal, sc, NEG_BIG)
                m_new = jnp.maximum(m, jnp.max(sc, axis=-1, keepdims=True))
                p = jnp.exp(sc - m_new)
                alpha = jnp.exp(m - m_new)
                l = alpha * l + jnp.sum(p, axis=-1, keepdims=True)
                acc = alpha * acc + _dot(p.astype(BF16), v)
                out.append((m_new, l, acc))
            return tuple(out)

        init = tuple((jnp.full((t, 1), NEG_BIG, F32), jnp.zeros((t, 1), F32), jnp.zeros((t, HEAD_DIM), F32))
                     for _ in range(hg))
        carry = lax.fori_loop(0, qi, lambda kj, c: tile(kj, c, False), init)
        carry = tile(qi, carry, True)
        for hh in range(hg):
            m, l, acc = carry[hh]
            o = acc / l
            of_ref[:, hh * HEAD_DIM:(hh + 1) * HEAD_DIM] = o
            o_ref[:, hh * HEAD_DIM:(hh + 1) * HEAD_DIM] = o.astype(BF16)
            lse_ref[hh] = m + jnp.log(l)

    tile_spec = pl.BlockSpec((t, GROUP_W), lambda h, i: (i, h))
    w = nheads * HEAD_DIM
    return pl.pallas_call(
        body,
        grid=(nheads // hg, s // t),
        in_specs=[tile_spec,
                  pl.BlockSpec((s, GROUP_W), lambda h, i: (0, h)),
                  pl.BlockSpec((s, GROUP_W), lambda h, i: (0, colv // hg + h)),
                  pl.BlockSpec((hg, t, 1), lambda h, i: (h, i, 0)),
                  pl.BlockSpec((hg, 1, s), lambda h, i: (h, 0, 0))],
        out_specs=[tile_spec, tile_spec, pl.BlockSpec((hg, t, 1), lambda h, i: (h, i, 0))],
        out_shape=[jax.ShapeDtypeStruct((s, w), BF16), jax.ShapeDtypeStruct((s, w), F32),
                   jax.ShapeDtypeStruct((nheads, s, 1), F32)],
        compiler_params=_params("parallel", "parallel"),
        name="fox_fwd",
    )(qn, kn, proj, c_col, c_row)


def _fox_bwd(qn, kn, proj, colv, c_col, c_row, o, do, lse, nheads):
    s = qn.shape[0]
    t = _tile(s, ATT_TILE)
    scale = HEAD_DIM ** -0.5
    hg = HEAD_GROUP
    assert nheads % hg == 0 and colv % hg == 0

    def body(q_ref, k_ref, v_ref, cc_ref, cr_ref, o_ref, do_ref, lse_ref,
             dq_ref, dk_ref, dv_ref, drs_ref, dcs_ref):
        qi = pl.program_id(1)

        @pl.when(qi == 0)
        def _():
            dk_ref[...] = jnp.zeros_like(dk_ref)
            dv_ref[...] = jnp.zeros_like(dv_ref)
            dcs_ref[...] = jnp.zeros_like(dcs_ref)

        causal = _tri_mask(t, False)
        delta = [jnp.sum(_head(o_ref, hh) * _head(do_ref, hh).astype(F32), axis=-1, keepdims=True)
                 for hh in range(hg)]

        def tile(kj, carry, diagonal):
            off = pl.multiple_of(kj * t, t)
            out = []
            for hh in range(hg):
                dq, rs = carry[hh]
                cols = slice(hh * HEAD_DIM, (hh + 1) * HEAD_DIM)
                q = _head(q_ref, hh)
                do_ = _head(do_ref, hh)
                k = _head(k_ref, hh, pl.ds(off, t))
                v = _head(v_ref, hh, pl.ds(off, t))
                sc = _dot_nt(q, k) * scale + (cc_ref[hh] - cr_ref[hh, :, pl.ds(off, t)])
                p = jnp.exp(sc - lse_ref[hh])
                if diagonal:
                    p = jnp.where(causal, p, 0.0)
                ds = p * (_dot_nt(do_, v) - delta[hh])
                dsb = ds.astype(BF16)
                dv_ref[pl.ds(off, t), cols] += _dot_tn(p.astype(BF16), do_)
                dk_ref[pl.ds(off, t), cols] += _dot_tn(dsb, q) * scale
                dcs_ref[hh, :, pl.ds(off, t)] += jnp.sum(ds, axis=0, keepdims=True)
                out.append((dq + _dot(dsb, k) * scale, rs + jnp.sum(ds, axis=-1, keepdims=True)))
            return tuple(out)

        init = tuple((jnp.zeros((t, HEAD_DIM), F32), jnp.zeros((t, 1), F32)) for _ in range(hg))
        carry = lax.fori_loop(0, qi, lambda kj, c: tile(kj, c, False), init)
        carry = tile(qi, carry, True)
        for hh in range(hg):
            dq_ref[:, hh * HEAD_DIM:(hh + 1) * HEAD_DIM] = carry[hh][0]
            drs_ref[hh] = carry[hh][1]

    tile_spec = pl.BlockSpec((t, GROUP_W), lambda h, i: (i, h))
    full = pl.BlockSpec((s, GROUP_W), lambda h, i: (0, h))
    colspec = pl.BlockSpec((hg, t, 1), lambda h, i: (h, i, 0))
    rowspec = pl.BlockSpec((hg, 1, s), lambda h, i: (h, 0, 0))
    w = nheads * HEAD_DIM
    return pl.pallas_call(
        body,
        grid=(nheads // hg, s // t),
        in_specs=[tile_spec, full, pl.BlockSpec((s, GROUP_W), lambda h, i: (0, colv // hg + h)), colspec, rowspec,
                  tile_spec, tile_spec, colspec],
        out_specs=[tile_spec, full, full, colspec, rowspec],
        out_shape=[jax.ShapeDtypeStruct((s, w), F32), jax.ShapeDtypeStruct((s, w), F32),
                   jax.ShapeDtypeStruct((s, w), F32), jax.ShapeDtypeStruct((nheads, s, 1), F32),
                   jax.ShapeDtypeStruct((nheads, 1, s), F32)],
        compiler_params=_params("arbitrary", "arbitrary"),
        name="fox_bwd",
    )(qn, kn, proj, c_col, c_row, o, do, lse)


def _sb_tile(q, k, scale, later, valid):
    z = _dot_nt(q, k) * scale
    lb = _log_sigmoid(z)
    lm = lb - z
    if valid is not None:
        lm = jnp.where(valid, lm, 0.0)
    hi, lo = _split2(lm)
    suffix = _dot(hi, later) + _dot(lo, later)
    return lb, lm, suffix


def _later(t):
    r = lax.broadcasted_iota(jnp.int32, (t, t), 0)
    c = lax.broadcasted_iota(jnp.int32, (t, t), 1)
    return jnp.where(r > c, 1.0, 0.0).astype(BF16)


def _sb_fwd(proj, colq, colk, colv, nheads):
    s = proj.shape[0]
    t = _tile(s, ATT_TILE)
    scale = HEAD_DIM ** -0.5
    hg = HEAD_GROUP
    assert nheads % hg == 0 and colq % hg == 0 and colk % hg == 0 and colv % hg == 0

    def body(q_ref, k_ref, v_ref, o_ref):
        qi = pl.program_id(1)
        later = _later(t)
        before = _tri_mask(t, True)

        def tile(kj, carry, diagonal):
            off = pl.multiple_of(kj * t, t)
            out = []
            for hh in range(hg):
                rc, acc = carry[hh]
                k = _head(k_ref, hh, pl.ds(off, t))
                v = _head(v_ref, hh, pl.ds(off, t))
                lb, lm, suffix = _sb_tile(_head(q_ref, hh), k, scale, later, before if diagonal else None)
                a = jnp.exp(lb + suffix + rc)
                if diagonal:
                    a = jnp.where(before, a, 0.0)
                out.append((rc + jnp.sum(lm, axis=-1, keepdims=True), acc + _dot(a.astype(BF16), v)))
            return tuple(out)

        init = tuple((jnp.zeros((t, 1), F32), jnp.zeros((t, HEAD_DIM), F32)) for _ in range(hg))
        carry = tile(qi, init, True)
        carry = lax.fori_loop(1, qi + 1, lambda i, c: tile(qi - i, c, False), carry)
        for hh in range(hg):
            o_ref[:, hh * HEAD_DIM:(hh + 1) * HEAD_DIM] = carry[hh][1].astype(BF16)

    return pl.pallas_call(
        body,
        grid=(nheads // hg, s // t),
        in_specs=[pl.BlockSpec((t, GROUP_W), lambda h, i: (i, colq // hg + h)),
                  pl.BlockSpec((s, GROUP_W), lambda h, i: (0, colk // hg + h)),
                  pl.BlockSpec((s, GROUP_W), lambda h, i: (0, colv // hg + h))],
        out_specs=pl.BlockSpec((t, GROUP_W), lambda h, i: (i, h)),
        out_shape=jax.ShapeDtypeStruct((s, nheads * HEAD_DIM), BF16),
        compiler_params=_params("parallel", "parallel"),
        name="sb_fwd",
    )(proj, proj, proj)


def _sb_bwd(proj, colq, colk, colv, do, nheads):
    s = proj.shape[0]
    t = _tile(s, ATT_TILE)
    scale = HEAD_DIM ** -0.5
    hg = HEAD_GROUP
    assert nheads % hg == 0 and colq % hg == 0 and colk % hg == 0 and colv % hg == 0

    def body(q_ref, k_ref, v_ref, do_ref, dq_ref, dk_ref, dv_ref, g_s, beta_s):
        qi = pl.program_id(1)

        @pl.when(qi == 0)
        def _():
            dk_ref[...] = jnp.zeros_like(dk_ref)
            dv_ref[...] = jnp.zeros_like(dv_ref)

        later = _later(t)
        before = _tri_mask(t, True)

        def back(kj, carry, diagonal):
            off = pl.multiple_of(kj * t, t)
            out = []
            for hh in range(hg):
                cols = slice(hh * HEAD_DIM, (hh + 1) * HEAD_DIM)
                do_ = _head(do_ref, hh)
                k = _head(k_ref, hh, pl.ds(off, t))
                v = _head(v_ref, hh, pl.ds(off, t))
                lb, lm, suffix = _sb_tile(_head(q_ref, hh), k, scale, later, before if diagonal else None)
                a = jnp.exp(lb + suffix + carry[hh])
                if diagonal:
                    a = jnp.where(before, a, 0.0)
                g_s[hh, :, pl.ds(off, t)] = a * _dot_nt(do_, v)
                beta_s[hh, :, pl.ds(off, t)] = jnp.exp(lb)
                dv_ref[pl.ds(off, t), cols] += _dot_tn(a.astype(BF16), do_)
                out.append(carry[hh] + jnp.sum(lm, axis=-1, keepdims=True))
            return tuple(out)

        rc = back(qi, tuple(jnp.zeros((t, 1), F32) for _ in range(hg)), True)
        lax.fori_loop(1, qi + 1, lambda i, c: back(qi - i, c, False), rc)

        earlier = jnp.where(_tri_mask(t, False), 0.0, 1.0).astype(BF16)

        def fwd(kj, carry, diagonal):
            off = pl.multiple_of(kj * t, t)
            out = []
            for hh in range(hg):
                gc, dq = carry[hh]
                cols = slice(hh * HEAD_DIM, (hh + 1) * HEAD_DIM)
                k = _head(k_ref, hh, pl.ds(off, t))
                g = g_s[hh, :, pl.ds(off, t)]
                beta = beta_s[hh, :, pl.ds(off, t)]
                hi, lo = _split2(g)
                gsum = _dot(hi, earlier) + _dot(lo, earlier) + gc
                dz = g * (1.0 - beta) - gsum * beta
                if diagonal:
                    dz = jnp.where(before, dz, 0.0)
                dz = dz.astype(BF16)
                dk_ref[pl.ds(off, t), cols] += _dot_tn(dz, _head(q_ref, hh)) * scale
                out.append((gc + jnp.sum(g, axis=-1, keepdims=True), dq + _dot(dz, k) * scale))
            return tuple(out)

        init = tuple((jnp.zeros((t, 1), F32), jnp.zeros((t, HEAD_DIM), F32)) for _ in range(hg))
        carry = lax.fori_loop(0, qi, lambda kj, c: fwd(kj, c, False), init)
        carry = fwd(qi, carry, True)
        for hh in range(hg):
            dq_ref[:, hh * HEAD_DIM:(hh + 1) * HEAD_DIM] = carry[hh][1]

    tile_spec = pl.BlockSpec((t, GROUP_W), lambda h, i: (i, h))
    full = pl.BlockSpec((s, GROUP_W), lambda h, i: (0, h))
    w = nheads * HEAD_DIM
    return pl.pallas_call(
        body,
        grid=(nheads // hg, s // t),
        in_specs=[pl.BlockSpec((t, GROUP_W), lambda h, i: (i, colq // hg + h)),
                  pl.BlockSpec((s, GROUP_W), lambda h, i: (0, colk // hg + h)),
                  pl.BlockSpec((s, GROUP_W), lambda h, i: (0, colv // hg + h)),
                  tile_spec],
        out_specs=[tile_spec, full, full],
        out_shape=[jax.ShapeDtypeStruct((s, w), F32)] * 3,
        scratch_shapes=[pltpu.VMEM((hg, t, s), F32), pltpu.VMEM((hg, t, s), F32)],
        compiler_params=_params("arbitrary", "arbitrary"),
        name="sb_bwd",
    )(proj, proj, proj, do)


def _mem_fwd(qn, kn, mkv, nheads):
    s = qn.shape[0]
    mtok = kn.shape[0]
    t = _tile(s, ATT_TILE)
    scale = HEAD_DIM ** -0.5

    def body(q_ref, k_ref, v_ref, o_ref):
        sc = _dot_nt(q_ref[...], k_ref[...]) * scale
        p = jnp.exp(sc - jnp.max(sc, axis=-1, keepdims=True))
        p = p / jnp.sum(p, axis=-1, keepdims=True)
        o_ref[...] = _dot(p.astype(BF16), v_ref[...]).astype(BF16)

    return pl.pallas_call(
        body,
        grid=(nheads, s // t),
        in_specs=[pl.BlockSpec((t, HEAD_DIM), lambda h, i: (i, h)),
                  pl.BlockSpec((mtok, HEAD_DIM), lambda h, i: (0, h)),
                  pl.BlockSpec((mtok, HEAD_DIM), lambda h, i: (0, nheads + h))],
        out_specs=pl.BlockSpec((t, HEAD_DIM), lambda h, i: (i, h)),
        out_shape=jax.ShapeDtypeStruct((s, nheads * HEAD_DIM), BF16),
        compiler_params=_params("parallel", "parallel"),
        name="mem_fwd",
    )(qn, kn, mkv)


def _mem_bwd(qn, kn, mkv, do, nheads):
    s = qn.shape[0]
    mtok = kn.shape[0]
    t = _tile(s, ATT_TILE)
    scale = HEAD_DIM ** -0.5

    def body(q_ref, k_ref, v_ref, do_ref, dq_ref, dk_ref, dv_ref):
        @pl.when(pl.program_id(1) == 0)
        def _():
            dk_ref[...] = jnp.zeros_like(dk_ref)
            dv_ref[...] = jnp.zeros_like(dv_ref)

        q = q_ref[...]
        k = k_ref[...]
        do_ = do_ref[...]
        sc = _dot_nt(q, k) * scale
        p = jnp.exp(sc - jnp.max(sc, axis=-1, keepdims=True))
        p = p / jnp.sum(p, axis=-1, keepdims=True)
        dp = _dot_nt(do_, v_ref[...])
        ds = (p * (dp - jnp.sum(p * dp, axis=-1, keepdims=True))).astype(BF16)
        dq_ref[...] = _dot(ds, k) * scale
        dk_ref[...] += _dot_tn(ds, q) * scale
        dv_ref[...] += _dot_tn(p.astype(BF16), do_)

    tile = pl.BlockSpec((t, HEAD_DIM), lambda h, i: (i, h))
    kspec = pl.BlockSpec((mtok, HEAD_DIM), lambda h, i: (0, h))
    w = nheads * HEAD_DIM
    return pl.pallas_call(
        body,
        grid=(nheads, s // t),
        in_specs=[tile, kspec, pl.BlockSpec((mtok, HEAD_DIM), lambda h, i: (0, nheads + h)), tile],
        out_specs=[tile, kspec, kspec],
        out_shape=[jax.ShapeDtypeStruct((s, w), F32), jax.ShapeDtypeStruct((mtok, w), F32),
                   jax.ShapeDtypeStruct((mtok, w), F32)],
        compiler_params=_params("arbitrary", "arbitrary"),
        name="mem_bwd",
    )(qn, kn, mkv, do)


def _merge_fwd(p0, p1, p2, proj, colg, b_gate):
    s, d = p0.shape
    tm, tn = _tile(s, ROW_TILE), _tile(d, COL_TILE)
    g0 = colg * LANES // tn
    nj = d // tn

    def body(p0_ref, p1_ref, p2_ref, ga_ref, gb_ref, gc_ref, b_ref, o_ref):
        acc = jnp.zeros((tm, tn), F32)
        for b, (p_ref, g_ref) in enumerate(((p0_ref, ga_ref), (p1_ref, gb_ref), (p2_ref, gc_ref))):
            gate = jax.nn.sigmoid(g_ref[...].astype(F32) + b_ref[b:b + 1, :])
            acc = acc + gate * p_ref[...]
        o_ref[...] = acc.astype(BF16)

    blk = pl.BlockSpec((tm, tn), lambda i, j: (i, j))
    gates = [pl.BlockSpec((tm, tn), functools.partial(lambda i, j, b: (i, g0 + b * nj + j), b=b)) for b in range(3)]
    return pl.pallas_call(
        body,
        grid=(s // tm, nj),
        in_specs=[blk, blk, blk] + gates + [pl.BlockSpec((3, tn), lambda i, j: (0, j))],
        out_specs=blk,
        out_shape=jax.ShapeDtypeStruct((s, d), BF16),
        compiler_params=_params("parallel", "parallel"),
        name="merge_fwd",
    )(p0, p1, p2, proj, proj, proj, b_gate)


def _merge_bwd(dmerged, p0, p1, p2, proj, colg, b_gate):
    s, d = p0.shape
    tm, tn = _tile(s, ROW_TILE), _tile(d, COL_TILE)
    g0 = colg * LANES // tn
    nj = d // tn

    def body(dm_ref, p0_ref, p1_ref, p2_ref, ga_ref, gb_ref, gc_ref, b_ref,
             d0_ref, d1_ref, d2_ref, dga_ref, dgb_ref, dgc_ref, db_ref):
        dm = dm_ref[...].astype(F32)
        parts = []
        for b, (p_ref, g_ref, dp_ref, dg_ref) in enumerate(((p0_ref, ga_ref, d0_ref, dga_ref),
                                                            (p1_ref, gb_ref, d1_ref, dgb_ref),
                                                            (p2_ref, gc_ref, d2_ref, dgc_ref))):
            gate = jax.nn.sigmoid(g_ref[...].astype(F32) + b_ref[b:b + 1, :])
            dp_ref[...] = (dm * gate).astype(BF16)
            dgate = dm * p_ref[...] * gate * (1.0 - gate)
            dg_ref[...] = dgate.astype(BF16)
            parts.append(jnp.sum(dgate, axis=0, keepdims=True))
        part = jnp.concatenate(parts, axis=0)

        @pl.when(pl.program_id(1) == 0)
        def _():
            db_ref[...] = part

        @pl.when(pl.program_id(1) > 0)
        def _():
            db_ref[...] += part

    blk = pl.BlockSpec((tm, tn), lambda j, i: (i, j))
    gates = [pl.BlockSpec((tm, tn), functools.partial(lambda j, i, b: (i, g0 + b * nj + j), b=b)) for b in range(3)]
    bias = pl.BlockSpec((3, tn), lambda j, i: (0, j))
    return pl.pallas_call(
        body,
        grid=(nj, s // tm),
        in_specs=[blk, blk, blk, blk] + gates + [bias],
        out_specs=[blk] * 6 + [bias],
        out_shape=[jax.ShapeDtypeStruct((s, d), BF16)] * 6 + [jax.ShapeDtypeStruct((3, d), F32)],
        compiler_params=_params("parallel", "arbitrary"),
        name="merge_bwd",
    )(dmerged, p0, p1, p2, proj, proj, proj, b_gate)


def _shift_down(v, n):
    rows = lax.broadcasted_iota(jnp.int32, v.shape, 0)
    return jnp.where(rows >= n, pltpu.roll(v, n, 0), 0.0)


def _shift_up(v, n):
    s = v.shape[0]
    rows = lax.broadcasted_iota(jnp.int32, v.shape, 0)
    return jnp.where(rows < s - n, pltpu.roll(v, s - n, 0), 0.0)


def _conv(v, w_ref, b_ref):
    taps = w_ref.shape[0]
    out = v * w_ref[taps - 1:taps, :] + b_ref[...]
    for n in range(1, taps):
        out = out + _shift_down(v, n) * w_ref[taps - 1 - n:taps - n, :]
    return out


def _conv_act_fwd(up, conv_w, conv_b):
    s, f2 = up.shape
    f = f2 // 2
    tn = LANES
    nj = f // tn
    taps = conv_w.shape[0]

    def body(ug_ref, uv_ref, wg_ref, wv_ref, bg_ref, bv_ref, o_ref):
        cg = _conv(ug_ref[...].astype(F32), wg_ref, bg_ref)
        cv = _conv(uv_ref[...].astype(F32), wv_ref, bv_ref)
        o_ref[...] = (cg * jax.nn.sigmoid(cg) * cv).astype(BF16)

    return pl.pallas_call(
        body,
        grid=(nj,),
        in_specs=[pl.BlockSpec((s, tn), lambda j: (0, j)), pl.BlockSpec((s, tn), lambda j: (0, nj + j)),
                  pl.BlockSpec((taps, tn), lambda j: (0, j)), pl.BlockSpec((taps, tn), lambda j: (0, nj + j)),
                  pl.BlockSpec((1, tn), lambda j: (0, j)), pl.BlockSpec((1, tn), lambda j: (0, nj + j))],
        out_specs=pl.BlockSpec((s, tn), lambda j: (0, j)),
        out_shape=jax.ShapeDtypeStruct((s, f), BF16),
        compiler_params=_params("parallel"),
        name="conv_act_fwd",
    )(up, up, conv_w, conv_w, conv_b, conv_b)


def _conv_act_bwd(up, conv_w, conv_b, dact):
    s, f2 = up.shape
    f = f2 // 2
    tn = LANES
    nj = f // tn
    taps = conv_w.shape[0]

    def half(v, du, w_ref, dup_ref, dw_ref, db_ref):
        dup = du * w_ref[taps - 1:taps, :]
        rows = [None] * taps
        rows[taps - 1] = jnp.sum(du * v, axis=0, keepdims=True)
        for n in range(1, taps):
            dup = dup + _shift_up(du, n) * w_ref[taps - 1 - n:taps - n, :]
            rows[taps - 1 - n] = jnp.sum(du * _shift_down(v, n), axis=0, keepdims=True)
        dup_ref[...] = dup.astype(BF16)
        dw_ref[...] = jnp.concatenate(rows, axis=0)
        db_ref[...] = jnp.sum(du, axis=0, keepdims=True)

    def body(ug_ref, uv_ref, wg_ref, wv_ref, bg_ref, bv_ref, da_ref,
             dug_ref, duv_ref, dwg_ref, dwv_ref, dbg_ref, dbv_ref):
        ug = ug_ref[...].astype(F32)
        uv = uv_ref[...].astype(F32)
        cg = _conv(ug, wg_ref, bg_ref)
        cv = _conv(uv, wv_ref, bv_ref)
        da = da_ref[...].astype(F32)
        sg = jax.nn.sigmoid(cg)
        dcv = da * cg * sg
        dcg = da * cv * (sg + cg * sg * (1.0 - sg))
        half(ug, dcg, wg_ref, dug_ref, dwg_ref, dbg_ref)
        half(uv, dcv, wv_ref, duv_ref, dwv_ref, dbv_ref)

    lo = lambda rows: pl.BlockSpec((rows, tn), lambda j: (0, j))
    hi = lambda rows: pl.BlockSpec((rows, tn), lambda j: (0, nj + j))
    return pl.pallas_call(
        body,
        grid=(nj,),
        in_specs=[lo(s), hi(s), lo(taps), hi(taps), lo(1), hi(1), lo(s)],
        out_specs=[lo(s), lo(s), lo(taps), lo(taps), lo(1), lo(1)],
        out_shape=[jax.ShapeDtypeStruct((s, f), BF16)] * 2 + [jax.ShapeDtypeStruct((taps, f), F32)] * 2
        + [jax.ShapeDtypeStruct((1, f), F32)] * 2,
        compiler_params=_params("parallel"),
        name="conv_act_bwd",
    )(up, up, conv_w, conv_w, conv_b, conv_b, dact)


def _loss_grad(y, target):
    s, d = y.shape
    tm = _tile(s, ROW_TILE)

    def body(y_ref, t_ref, dy_ref, dyb_ref, l_ref):
        e = y_ref[...] - t_ref[...]
        dy = e * (1.0 / d)
        dy_ref[...] = dy
        dyb_ref[...] = dy.astype(BF16)
        tot = jnp.sum(jnp.sum(e * e, axis=-1, keepdims=True), axis=0, keepdims=True)
        l_ref[...] = jnp.broadcast_to(tot, (8, LANES))

    row = pl.BlockSpec((tm, d), lambda i: (i, 0))
    return pl.pallas_call(
        body,
        grid=(s // tm,),
        in_specs=[row, row],
        out_specs=[row, row, pl.BlockSpec((8, LANES), lambda i: (i, 0))],
        out_shape=[jax.ShapeDtypeStruct((s, d), F32), jax.ShapeDtypeStruct((s, d), BF16),
                   jax.ShapeDtypeStruct((s // tm * 8, LANES), F32)],
        compiler_params=_params("parallel"),
        name="loss_grad",
    )(y, target)


def _row_tile(rows, row_bytes, budget):
    if rows * row_bytes <= budget or rows % 8:
        return rows
    best = 8
    for t in range(8, rows, 8):
        if rows % t == 0 and t * row_bytes <= budget:
            best = t
    return best


def _adamw(w, g, m, v, name):
    r, c = w.shape
    tr = _row_tile(r, c * 4, ADAM_BLOCK_BYTES)

    def body(w_ref, g_ref, m_ref, v_ref, d_ref, mo_ref, vo_ref):
        gg = g_ref[...]
        m_new = ADAM_B1 * m_ref[...] + (1.0 - ADAM_B1) * gg
        v_new = ADAM_B2 * v_ref[...] + (1.0 - ADAM_B2) * (gg * gg)
        m_hat = m_new / (1.0 - ADAM_B1 ** ADAM_STEP)
        v_hat = v_new / (1.0 - ADAM_B2 ** ADAM_STEP)
        d_ref[...] = -ADAM_LR * (m_hat / (jnp.sqrt(v_hat) + ADAM_EPS) + ADAM_WD * w_ref[...])
        mo_ref[...] = m_new
        vo_ref[...] = v_new

    blk = pl.BlockSpec((tr, c), lambda i: (i, 0))
    return pl.pallas_call(
        body,
        grid=(r // tr,),
        in_specs=[blk] * 4,
        out_specs=[blk] * 3,
        out_shape=[jax.ShapeDtypeStruct((r, c), F32)] * 3,
        compiler_params=_params("parallel"),
        name=name,
    )(w, g, m, v)


def _add_sibling(g, r1, core, name):
    _, _, h, c = g.shape
    th = _row_tile(h, c * 2, ADAM_BLOCK_BYTES)

    def body(core_ref, g_ref, r_ref, o_ref):
        o_ref[...] = (g_ref[...].astype(F32) + r_ref[...].astype(F32)).astype(BF16)

    return pl.pallas_call(
        body,
        grid_spec=pltpu.PrefetchScalarGridSpec(
            num_scalar_prefetch=1,
            grid=(N_CHIPS, h // th),
            in_specs=[pl.BlockSpec((None, None, th, c), lambda j, i, core_ref: (j, core_ref[0], i, 0)),
                      pl.BlockSpec((None, th, c), lambda j, i, core_ref: (j, i, 0))],
            out_specs=pl.BlockSpec((None, th, c), lambda j, i, core_ref: (j, i, 0)),
        ),
        out_shape=jax.ShapeDtypeStruct((N_CHIPS, h, c), BF16),
        compiler_params=_params("parallel", "parallel"),
        name=name,
    )(core, g, r1)


def _add_chips(hsum, r2, chip_core, name):
    _, h, c = hsum.shape
    th = _row_tile(h, c * 4, ADAM_BLOCK_BYTES)

    def body(sel_ref, own_ref, r_ref, o_ref):
        acc = own_ref[...].astype(F32)
        for j in range(N_CHIPS - 1):
            acc = acc + r_ref[j].astype(F32)
        o_ref[...] = acc

    return pl.pallas_call(
        body,
        grid_spec=pltpu.PrefetchScalarGridSpec(
            num_scalar_prefetch=1,
            grid=(h // th,),
            in_specs=[pl.BlockSpec((None, th, c), lambda i, sel_ref: (sel_ref[0], i, 0)),
                      pl.BlockSpec((N_CHIPS - 1, th, c), lambda i, sel_ref: (0, i, 0))],
            out_specs=pl.BlockSpec((None, th, c), lambda i, sel_ref: (sel_ref[1], i, 0)),
        ),
        out_shape=jax.ShapeDtypeStruct((2, h, c), F32),
        compiler_params=_params("parallel"),
        name=name,
    )(chip_core, hsum, r2)


def _sum_devices(parts):
    _, r, c = parts.shape

    def body(p_ref, o_ref):
        acc = p_ref[0]
        for j in range(1, N_DEV):
            acc = acc + p_ref[j]
        o_ref[...] = acc

    return pl.pallas_call(
        body,
        out_shape=jax.ShapeDtypeStruct((r, c), F32),
        compiler_params=pltpu.CompilerParams(vmem_limit_bytes=VMEM_LIMIT_BYTES),
        name="sum_devices",
    )(parts)


ANY = pl.BlockSpec(memory_space=pl.ANY)


def _place():
    x, y, c = lax.axis_index("x"), lax.axis_index("y"), lax.axis_index("c")
    others = [(1 - x, y), (x, 1 - y), (1 - x, 1 - y)]
    return x, y, c, others


def _remote(src, dst, send_sem, recv_sem, to):
    return pltpu.make_async_remote_copy(src_ref=src, dst_ref=dst, send_sem=send_sem, recv_sem=recv_sem,
                                        device_id=to, device_id_type=MESH)


def _gather_weights(shards):
    n = len(shards)

    def body(*refs):
        ins, outs = refs[:n], refs[n:2 * n]
        send_sems, recv_sems = refs[2 * n:]
        x, y, c, others = _place()
        me = 2 * x + y
        sibling = (x, y, 1 - c)
        sent = []
        for i in range(n):
            h = ins[i].shape[0] // 2
            mine = pl.ds(pl.multiple_of(c * h, BF16_SUBLANES), h)
            for j, (ox, oy) in enumerate(others):
                cp = _remote(ins[i].at[mine], outs[i].at[me, mine], send_sems.at[6 * i + j], recv_sems.at[6 * i + j],
                             (ox, oy, c))
                cp.start()
                sent.append(cp)
        for i in range(n):
            h = ins[i].shape[0] // 2
            mine = pl.ds(pl.multiple_of(c * h, BF16_SUBLANES), h)
            for j, (ox, oy) in enumerate(others):
                blk = outs[i].at[2 * ox + oy, mine]
                _remote(blk, blk, send_sems.at[6 * i + j], recv_sems.at[6 * i + j], (ox, oy, c)).wait_recv()
                cp = _remote(blk, blk, send_sems.at[6 * i + 3 + j], recv_sems.at[6 * i + 3 + j], sibling)
                cp.start()
                sent.append(cp)
        for i in range(n):
            h = ins[i].shape[0] // 2
            theirs = pl.ds(pl.multiple_of((1 - c) * h, BF16_SUBLANES), h)
            for j, (ox, oy) in enumerate(others):
                blk = outs[i].at[2 * ox + oy, theirs]
                _remote(blk, blk, send_sems.at[6 * i + 3 + j], recv_sems.at[6 * i + 3 + j], sibling).wait_recv()
        for cp in sent:
            cp.wait_send()

    return pl.pallas_call(
        body,
        in_specs=[ANY] * n,
        out_specs=[ANY] * n,
        out_shape=[jax.ShapeDtypeStruct((N_CHIPS,) + s.shape, s.dtype) for s in shards],
        scratch_shapes=[pltpu.SemaphoreType.DMA((6 * n,)), pltpu.SemaphoreType.DMA((6 * n,))],
        name="gather_weights",
    )(*shards)


def _swap_halves(grads):
    n = len(grads)

    def body(*refs):
        ins, outs = refs[:n], refs[n:2 * n]
        send_sems, recv_sems = refs[2 * n:]
        x, y, c, _ = _place()
        copies = [_remote(ins[i].at[:, 1 - c], outs[i], send_sems.at[i], recv_sems.at[i], (x, y, 1 - c))
                  for i in range(n)]
        for cp in copies:
            cp.start()
        for cp in copies:
            cp.wait()

    return pl.pallas_call(
        body,
        in_specs=[ANY] * n,
        out_specs=[ANY] * n,
        out_shape=[jax.ShapeDtypeStruct((g.shape[0],) + g.shape[2:], g.dtype) for g in grads],
        scratch_shapes=[pltpu.SemaphoreType.DMA((n,)), pltpu.SemaphoreType.DMA((n,))],
        name="swap_halves",
    )(*grads)


def _scatter_chips(sums):
    n = len(sums)
    k = N_CHIPS - 1

    def body(*refs):
        ins, outs = refs[:n], refs[n:2 * n]
        send_sems, recv_sems = refs[2 * n:]
        _, _, c, others = _place()
        copies = [_remote(ins[i].at[2 * ox + oy], outs[i].at[j], send_sems.at[k * i + j], recv_sems.at[k * i + j],
                          (ox, oy, c))
                  for i in range(n) for j, (ox, oy) in enumerate(others)]
        for cp in copies:
            cp.start()
        for cp in copies:
            cp.wait()

    return pl.pallas_call(
        body,
        in_specs=[ANY] * n,
        out_specs=[ANY] * n,
        out_shape=[jax.ShapeDtypeStruct((k,) + g.shape[1:], g.dtype) for g in sums],
        scratch_shapes=[pltpu.SemaphoreType.DMA((k * n,)), pltpu.SemaphoreType.DMA((k * n,))],
        name="scatter_chips",
    )(*sums)


def _join_halves(finals):
    n = len(finals)

    def body(*refs):
        outs = refs[n:2 * n]
        send_sems, recv_sems = refs[2 * n:]
        x, y, c, _ = _place()
        sends = [_remote(outs[i].at[c], outs[i].at[c], send_sems.at[i], recv_sems.at[i], (x, y, 1 - c))
                 for i in range(n)]
        for cp in sends:
            cp.start()
        for i in range(n):
            sends[i].wait_send()
            other = outs[i].at[1 - c]
            _remote(other, other, send_sems.at[i], recv_sems.at[i], (x, y, 1 - c)).wait_recv()

    return pl.pallas_call(
        body,
        in_specs=[ANY] * n,
        out_specs=[ANY] * n,
        out_shape=[jax.ShapeDtypeStruct(f.shape, f.dtype) for f in finals],
        input_output_aliases={i: i for i in range(n)},
        scratch_shapes=[pltpu.SemaphoreType.DMA((n,)), pltpu.SemaphoreType.DMA((n,))],
        name="join_halves",
    )(*finals)


def _gather_small(vec):
    k = N_DEV - 1

    def body(v_ref, o_ref, send_sems, recv_sems, local_sem):
        x, y, c, _ = _place()
        me = 4 * x + 2 * y + c
        local = pltpu.make_async_copy(v_ref, o_ref.at[me], local_sem)
        local.start()
        peers = [(x ^ (r >> 2 & 1), y ^ (r >> 1 & 1), c ^ (r & 1)) for r in range(1, N_DEV)]
        sends = [_remote(v_ref, o_ref.at[me], send_sems.at[j], recv_sems.at[j], p) for j, p in enumerate(peers)]
        for cp in sends:
            cp.start()
        for j, (px, py, pc) in enumerate(peers):
            sends[j].wait_send()
            blk = o_ref.at[4 * px + 2 * py + pc]
            _remote(blk, blk, send_sems.at[j], recv_sems.at[j], (px, py, pc)).wait_recv()
        local.wait()

    return pl.pallas_call(
        body,
        in_specs=[ANY],
        out_specs=ANY,
        out_shape=jax.ShapeDtypeStruct((N_DEV,) + vec.shape, vec.dtype),
        scratch_shapes=[pltpu.SemaphoreType.DMA((k,)), pltpu.SemaphoreType.DMA((k,)), pltpu.SemaphoreType.DMA(())],
        name="gather_small",
    )(vec)


def _reduce_scatter(grads, core, chip_core):
    split = [g.reshape(N_CHIPS, 2, g.shape[1] // 2, g.shape[2]) for g in grads]
    theirs = _swap_halves(split)
    sums = [_add_sibling(g, r, core, "add_sibling_%d" % i) for i, (g, r) in enumerate(zip(split, theirs))]
    others = _scatter_chips(sums)
    finals = [_add_chips(s, r, chip_core, "add_chips_%d" % i) for i, (s, r) in enumerate(zip(sums, others))]
    joined = _join_halves(finals)
    return [j.reshape(2 * j.shape[1], j.shape[2]) for j in joined]


def _local_step(x, mem, target, w):
    d = x.shape[1]
    nf = w["w_br_fox"].shape[0] // HEAD_DIM
    nsb = w["w_br_sb"].shape[0] // HEAD_DIM
    nm = w["w_br_mem"].shape[0] // HEAD_DIM
    fq, fk, fv = 0, nf, 2 * nf
    sq, sk, sv = 3 * nf, 3 * nf + nsb, 3 * nf + 2 * nsb
    mq = 3 * nf + 3 * nsb
    gates = mq + nm

    h, rstd1 = _rms_fwd(x, w["g_mix"], "rms_mix_fwd")
    proj = _mm(h, w["w_in_main"], "nn", BF16, "proj_main")
    f_logit = _mm(h, w["w_in_f"], "nn", F32, "proj_forget")
    c_sum = _forget_fwd(f_logit, w["b_forget"])
    c_t = c_sum[:, :nf].T
    c_col, c_row = c_t[:, :, None], c_t[:, None, :]
    qn = _headnorm_fwd(proj, fq, nf, w["g_q_fox"], "fox_qnorm_fwd")
    kn = _headnorm_fwd(proj, fk, nf, w["g_k_fox"], "fox_knorm_fwd")
    o_fox, o_fox32, lse = _fox_fwd(qn, kn, proj, fv, c_col, c_row, nf)
    o_sb = _sb_fwd(proj, sq, sk, sv, nsb)
    memn, rstd_m = _rms_fwd(mem, w["g_mem"], "rms_mem_fwd")
    mkv = _mm(memn, w["w_mem_kv"], "nn", BF16, "mem_kv")
    kmn = _headnorm_fwd(mkv, 0, nm, w["g_k_mem"], "mem_knorm_fwd")
    qmn = _headnorm_fwd(proj, mq, nm, w["g_q_mem"], "mem_qnorm_fwd")
    o_mem = _mem_fwd(qmn, kmn, mkv, nm)
    p0 = _mm(o_fox, w["w_br_fox"], "nn", F32, "branch_fox")
    p1 = _mm(o_sb, w["w_br_sb"], "nn", F32, "branch_sb")
    p2 = _mm(o_mem, w["w_br_mem"], "nn", F32, "branch_mem")
    merged = _merge_fwd(p0, p1, p2, proj, gates, w["b_gate"])
    x1 = _mm(merged, w["w_out"], "nn", F32, "out_proj", residual=x)
    h2, rstd2 = _rms_fwd(x1, w["g_ffn"], "rms_ffn_fwd")
    up = _mm(h2, w["w_up"], "nn", BF16, "ffn_up")
    act = _conv_act_fwd(up, w["conv_w"], w["conv_b"])
    y = _mm(act, w["w_down"], "nn", F32, "ffn_down", residual=x1)
    dy, dyb, lparts = _loss_grad(y, target)
    loss = (0.5 / d) * jnp.sum(lparts[::8, 0])

    g = {}
    dact = _mm(dyb, w["w_down"], "nt", BF16, "ffn_down_dx")
    g["w_down"] = _mm(act, dyb, "tn", BF16, "ffn_down_dw")
    dug, duv, dwg, dwv, dbg, dbv = _conv_act_bwd(up, w["conv_w"], w["conv_b"], dact)
    dup = jnp.concatenate([dug, duv], axis=1)
    g["conv_w"] = jnp.concatenate([dwg, dwv], axis=1)
    g["conv_b"] = jnp.concatenate([dbg, dbv], axis=1)
    dh2 = _mm(dup, w["w_up"], "nt", BF16, "ffn_up_dx")
    g["w_up"] = _mm(h2, dup, "tn", BF16, "ffn_up_dw")
    dx1, dx1b, g["g_ffn"] = _rms_bwd(dh2, x1, rstd2, w["g_ffn"], dy, "rms_ffn_bwd")
    dmerged = _mm(dx1b, w["w_out"], "nt", BF16, "out_proj_dx")
    g["w_out"] = _mm(merged, dx1b, "tn", BF16, "out_proj_dw")
    dp0, dp1, dp2, dga, dgb, dgc, g["b_gate"] = _merge_bwd(dmerged, p0, p1, p2, proj, gates, w["b_gate"])
    do_fox = _mm(dp0, w["w_br_fox"], "nt", BF16, "branch_fox_dx")
    do_sb = _mm(dp1, w["w_br_sb"], "nt", BF16, "branch_sb_dx")
    do_mem = _mm(dp2, w["w_br_mem"], "nt", BF16, "branch_mem_dx")
    g["w_br_fox"] = _mm(o_fox, dp0, "tn", BF16, "branch_fox_dw")
    g["w_br_sb"] = _mm(o_sb, dp1, "tn", BF16, "branch_sb_dw")
    g["w_br_mem"] = _mm(o_mem, dp2, "tn", BF16, "branch_mem_dw")

    dqn, dkn, dfv, drs, dcs = _fox_bwd(qn, kn, proj, fv, c_col, c_row, o_fox32, do_fox, lse, nf)
    dfq, g["g_q_fox"] = _headnorm_bwd(dqn, proj, fq, nf, w["g_q_fox"], "fox_qnorm_bwd")
    dfk, g["g_k_fox"] = _headnorm_bwd(dkn, proj, fk, nf, w["g_k_fox"], "fox_knorm_bwd")
    dc = jnp.pad((drs[:, :, 0] - dcs[:, 0, :]).T, ((0, 0), (0, LANES - nf)))
    df, g["b_forget"] = _forget_bwd(dc, f_logit, w["b_forget"])
    dsq, dsk, dsv = _sb_bwd(proj, sq, sk, sv, do_sb, nsb)
    dqmn, dkmn, dvm = _mem_bwd(qmn, kmn, mkv, do_mem, nm)
    dmq, g["g_q_mem"] = _headnorm_bwd(dqmn, proj, mq, nm, w["g_q_mem"], "mem_qnorm_bwd")
    dkm, g["g_k_mem"] = _headnorm_bwd(dkmn, mkv, 0, nm, w["g_k_mem"], "mem_knorm_bwd")
    dmkv = jnp.concatenate([dkm, dvm.astype(BF16)], axis=1)
    g["w_mem_kv"] = _mm(memn, dmkv, "tn", BF16, "mem_kv_dw")
    dmemn = _mm(dmkv, w["w_mem_kv"], "nt", BF16, "mem_kv_dx")
    _, _, g["g_mem"] = _rms_bwd(dmemn, mem, rstd_m, w["g_mem"], None, "rms_mem_bwd")

    dproj = jnp.concatenate([dfq, dfk, dfv.astype(BF16), dsq.astype(BF16), dsk.astype(BF16), dsv.astype(BF16),
                             dmq, dga, dgb, dgc], axis=1)
    dfb = df.astype(BF16)
    dh_main = _mm(dproj, w["w_in_main"], "nt", F32, "proj_main_dx")
    dh = _mm(dfb, w["w_in_f"], "nt", F32, "proj_forget_dx", residual=dh_main)
    g["w_in_main"] = _mm(h, dproj, "tn", BF16, "proj_main_dw")
    g["w_in_f"] = _mm(h, dfb, "tn", BF16, "proj_forget_dw")
    grad_x, _, g["g_mix"] = _rms_bwd(dh, x, rstd1, w["g_mix"], dx1, "rms_mix_bwd")
    return loss, grad_x, g


ROW_SHARDED = ("w_in_main", "w_in_f", "w_mem_kv", "w_out", "w_down")
COL_SHARDED = ("w_br_fox", "w_br_sb", "w_br_mem", "w_up")
BIG = ROW_SHARDED + COL_SHARDED
SMALL = ("g_mix", "b_forget", "g_q_fox", "g_k_fox", "g_mem", "g_q_mem", "g_k_mem", "b_gate", "g_ffn", "conv_w",
         "conv_b")
SMALL_SHARDED = ("b_gate", "conv_w")
PACK_ROWS = 8


def _pack(arrs):
    flat = jnp.concatenate([a.reshape(-1) for a in arrs])
    unit = PACK_ROWS * LANES
    flat = jnp.pad(flat, (0, -flat.shape[0] % unit))
    return flat.reshape(-1, LANES)


def _unpack(packed, shapes):
    flat = packed.reshape(-1)
    out, at = [], 0
    for s in shapes:
        n = 1
        for dim in s:
            n *= dim
        out.append(flat[at:at + n].reshape(s))
        at += n
    return out


def kernel(x, mem, g_mix, w_in, b_forget, g_q_fox, g_k_fox, g_mem, w_mem_kv, g_q_mem, g_k_mem, w_br_fox, w_br_sb, w_br_mem, b_gate, w_out, g_ffn, w_up, conv_w, conv_b, w_down, loss_target, m_g_mix, m_w_in, m_b_forget, m_g_q_fox, m_g_k_fox, m_g_mem, m_w_mem_kv, m_g_q_mem, m_g_k_mem, m_w_br_fox, m_w_br_sb, m_w_br_mem, m_b_gate, m_w_out, m_g_ffn, m_w_up, m_conv_w, m_conv_b, m_w_down, v_g_mix, v_w_in, v_b_forget, v_g_q_fox, v_g_k_fox, v_g_mem, v_w_mem_kv, v_g_q_mem, v_g_k_mem, v_w_br_fox, v_w_br_sb, v_w_br_mem, v_b_gate, v_w_out, v_g_ffn, v_w_up, v_conv_w, v_conv_b, v_w_down):
    given = dict(g_mix=g_mix, w_in=w_in, b_forget=b_forget, g_q_fox=g_q_fox, g_k_fox=g_k_fox, g_mem=g_mem,
                 w_mem_kv=w_mem_kv, g_q_mem=g_q_mem, g_k_mem=g_k_mem, w_br_fox=w_br_fox, w_br_sb=w_br_sb,
                 w_br_mem=w_br_mem, b_gate=b_gate, w_out=w_out, g_ffn=g_ffn, w_up=w_up, conv_w=conv_w, conv_b=conv_b,
                 w_down=w_down)
    m_in = dict(g_mix=m_g_mix, w_in=m_w_in, b_forget=m_b_forget, g_q_fox=m_g_q_fox, g_k_fox=m_g_k_fox, g_mem=m_g_mem,
                w_mem_kv=m_w_mem_kv, g_q_mem=m_g_q_mem, g_k_mem=m_g_k_mem, w_br_fox=m_w_br_fox, w_br_sb=m_w_br_sb,
                w_br_mem=m_w_br_mem, b_gate=m_b_gate, w_out=m_w_out, g_ffn=m_g_ffn, w_up=m_w_up, conv_w=m_conv_w,
                conv_b=m_conv_b, w_down=m_w_down)
    v_in = dict(g_mix=v_g_mix, w_in=v_w_in, b_forget=v_b_forget, g_q_fox=v_g_q_fox, g_k_fox=v_g_k_fox, g_mem=v_g_mem,
                w_mem_kv=v_w_mem_kv, g_q_mem=v_g_q_mem, g_k_mem=v_g_k_mem, w_br_fox=v_w_br_fox, w_br_sb=v_w_br_sb,
                w_br_mem=v_w_br_mem, b_gate=v_b_gate, w_out=v_w_out, g_ffn=v_g_ffn, w_up=v_w_up, conv_w=v_conv_w,
                conv_b=v_conv_b, w_down=v_w_down)
    layered = {k: a.ndim == 3 for k, a in given.items()}
    drop = lambda a: a[0] if a.ndim == 3 else a
    given = {k: drop(a) for k, a in given.items()}
    m_in = {k: drop(a) for k, a in m_in.items()}
    v_in = {k: drop(a) for k, a in v_in.items()}

    xi, yi, ci = lax.axis_index("x"), lax.axis_index("y"), lax.axis_index("c")
    chip = (2 * xi + yi).astype(jnp.int32)
    core_arr = ci.astype(jnp.int32).reshape(1)
    chip_core = jnp.stack([chip, ci.astype(jnp.int32)])

    nf = given["b_forget"].shape[1]
    cut = 3 * given["w_br_fox"].shape[0]

    shard = {
        "w_in_main": jnp.concatenate([given["w_in"][:, :cut], given["w_in"][:, cut + nf:]], axis=1).astype(BF16),
        "w_in_f": jnp.pad(given["w_in"][:, cut:cut + nf], ((0, 0), (0, LANES - nf))).astype(BF16),
    }
    for name in BIG[2:]:
        shard[name] = given[name].astype(BF16)
    full = {name: lax.dynamic_update_slice(a, shard[name][None], (chip, 0, 0))
            for name, a in zip(BIG, _gather_weights([shard[name] for name in BIG]))}
    w = {}
    for name in ROW_SHARDED:
        a = full[name]
        w[name] = a.reshape(N_CHIPS * a.shape[1], a.shape[2])
    for name in COL_SHARDED:
        a = full[name]
        w[name] = a.transpose(1, 0, 2).reshape(a.shape[1], N_CHIPS * a.shape[2])
    small_shapes = [given[name].shape for name in SMALL_SHARDED]
    small_parts = _gather_small(_pack([given[name] for name in SMALL_SHARDED]))[0::2]
    per_chip = [_unpack(small_parts[j], small_shapes) for j in range(N_CHIPS)]
    for k, name in enumerate(SMALL_SHARDED):
        w[name] = jnp.concatenate([per_chip[j][k] for j in range(N_CHIPS)], axis=1)
    for name in SMALL:
        if name not in SMALL_SHARDED:
            w[name] = given[name]
    w["b_forget"] = jnp.pad(given["b_forget"], ((0, 0), (0, LANES - nf)))

    loss, grad_x, g = _local_step(x[0], mem[0], loss_target[0], w)
    loss = lax.psum(loss, ("x", "y", "c"))

    by_shard = []
    for name in BIG:
        a = g[name]
        if name in ROW_SHARDED:
            by_shard.append(a.reshape(N_CHIPS, a.shape[0] // N_CHIPS, a.shape[1]))
        else:
            by_shard.append(a.reshape(a.shape[0], N_CHIPS, a.shape[1] // N_CHIPS).transpose(1, 0, 2))
    summed = dict(zip(BIG, _reduce_scatter(by_shard, core_arr, chip_core)))
    grads = {name: summed[name] for name in BIG[2:]}
    grads["w_in"] = jnp.concatenate([summed["w_in_main"][:, :cut], summed["w_in_f"][:, :nf],
                                     summed["w_in_main"][:, cut:]], axis=1)

    g["b_forget"] = g["b_forget"][:, :nf]
    small_full_shapes = [g[name].shape for name in SMALL]
    small_sum = _unpack(_sum_devices(_gather_small(_pack([g[name] for name in SMALL]))), small_full_shapes)
    for name, a in zip(SMALL, small_sum):
        if name in SMALL_SHARDED:
            width = given[name].shape[1]
            a = lax.dynamic_slice_in_dim(a, chip * width, width, axis=1)
        grads[name] = a

    delta, new_m, new_v = {}, {}, {}
    for name in WEIGHTS:
        if name not in SMALL:
            delta[name], new_m[name], new_v[name] = _adamw(given[name], grads[name], m_in[name], v_in[name],
                                                           "adamw_" + name)
    shapes = [given[name].shape for name in SMALL]
    packed = [_pack([src[name] for name in SMALL]) for src in (given, grads, m_in, v_in)]
    for dst, res in zip((delta, new_m, new_v), _adamw(*packed, "adamw_small")):
        for name, a in zip(SMALL, _unpack(res, shapes)):
            dst[name] = a

    out = [loss, grad_x[None]]
    for src in (grads, delta, new_m, new_v):
        out.extend(src[name][None] if layered[name] else src[name] for name in WEIGHTS)
    return tuple(out)
```

```python
import functools

import jax
import jax.numpy as jnp
from jax import lax
from jax.experimental import pallas as pl
from jax.experimental.pallas import tpu as pltpu

F32 = jnp.float32
BF16 = jnp.bfloat16

HEAD_DIM = 128
EPS = 1e-6
NEG_BIG = -1e30

ADAM_LR = 0.001
ADAM_B1 = 0.9
ADAM_B2 = 0.999
ADAM_EPS = 1e-08
ADAM_WD = 0.01
ADAM_STEP = 10

LANES = 128
BF16_SUBLANES = 16
VMEM_LIMIT_BYTES = 56 * 1024 * 1024
MM_TILE = 1024
ATT_TILE = 256
ROW_TILE = 256
HEADNORM_ROWS = 512
COL_TILE = 512
ADAM_BLOCK_BYTES = 1 << 20

N_CHIPS = 4
N_DEV = 8
MESH = pl.DeviceIdType.MESH

IN_NAMES = ['x', 'mem', 'g_mix', 'w_in', 'b_forget', 'g_q_fox', 'g_k_fox', 'g_mem', 'w_mem_kv', 'g_q_mem', 'g_k_mem',
            'w_br_fox', 'w_br_sb', 'w_br_mem', 'b_gate', 'w_out', 'g_ffn', 'w_up', 'conv_w', 'conv_b', 'w_down']
WEIGHTS = IN_NAMES[2:]


def _tile(n, target):
    if n <= target:
        return n
    for t in range(target - target % LANES, LANES - 1, -LANES):
        if n % t == 0:
            return t
    return n


def _params(*sem):
    return pltpu.CompilerParams(dimension_semantics=sem, vmem_limit_bytes=VMEM_LIMIT_BYTES)


def _log_sigmoid(z):
    return jnp.minimum(z, 0.0) - jnp.log(1.0 + jnp.exp(-jnp.abs(z)))


def _split2(v):
    hi = v.astype(BF16)
    lo = (v - hi.astype(F32)).astype(BF16)
    return hi, lo


def _split3(v):
    hi = v.astype(BF16)
    r = v - hi.astype(F32)
    mid = r.astype(BF16)
    lo = (r - mid.astype(F32)).astype(BF16)
    return hi, mid, lo


def _dot(a, b):
    return lax.dot_general(a, b, (((1,), (0,)), ((), ())), preferred_element_type=F32)


def _dot_nt(a, b):
    return lax.dot_general(a, b, (((1,), (1,)), ((), ())), preferred_element_type=F32)


def _dot_tn(a, b):
    return lax.dot_general(a, b, (((0,), (0,)), ((), ())), preferred_element_type=F32)


ANY = pl.BlockSpec(memory_space=pl.ANY)


def _place():
    x, y, c = lax.axis_index("x"), lax.axis_index("y"), lax.axis_index("c")
    others = [(1 - x, y), (x, 1 - y), (1 - x, 1 - y)]
    return x, y, c, others


def _remote(src, dst, send_sem, recv_sem, to):
    return pltpu.make_async_remote_copy(src_ref=src, dst_ref=dst, send_sem=send_sem, recv_sem=recv_sem,
                                        device_id=to, device_id_type=MESH)


class _Gather:
    def __init__(self, shards):
        self.inputs = list(shards)
        n = len(shards)
        self.out_shapes = [jax.ShapeDtypeStruct((N_CHIPS,) + s.shape, s.dtype) for s in shards]
        self.scratch = [pltpu.SemaphoreType.DMA((6 * n,)), pltpu.SemaphoreType.DMA((6 * n,))]

    def _over_ici(self, ins, outs, sems):
        send_sems, recv_sems = sems
        x, y, c, others = _place()
        me = 2 * x + y
        copies = []
        for i in range(len(ins)):
            h = ins[i].shape[0] // 2
            mine = pl.ds(pl.multiple_of(c * h, BF16_SUBLANES), h)
            for j, (ox, oy) in enumerate(others):
                copies.append(_remote(ins[i].at[mine], outs[i].at[me, mine], send_sems.at[6 * i + j],
                                      recv_sems.at[6 * i + j], (ox, oy, c)))
        return copies

    def start(self, ins, outs, sems):
        for cp in self._over_ici(ins, outs, sems):
            cp.start()

    def finish(self, ins, outs, sems):
        send_sems, recv_sems = sems
        x, y, c, others = _place()
        sibling = (x, y, 1 - c)
        passed = []
        for i in range(len(ins)):
            h = ins[i].shape[0] // 2
            mine = pl.ds(pl.multiple_of(c * h, BF16_SUBLANES), h)
            for j, (ox, oy) in enumerate(others):
                blk = outs[i].at[2 * ox + oy, mine]
                _remote(blk, blk, send_sems.at[6 * i + j], recv_sems.at[6 * i + j], (ox, oy, c)).wait_recv()
                cp = _remote(blk, blk, send_sems.at[6 * i + 3 + j], recv_sems.at[6 * i + 3 + j], sibling)
                cp.start()
                passed.append(cp)
        for i in range(len(ins)):
            h = ins[i].shape[0] // 2
            theirs = pl.ds(pl.multiple_of((1 - c) * h, BF16_SUBLANES), h)
            for j, (ox, oy) in enumerate(others):
                blk = outs[i].at[2 * ox + oy, theirs]
                _remote(blk, blk, send_sems.at[6 * i + 3 + j], recv_sems.at[6 * i + 3 + j], sibling).wait_recv()
        for cp in self._over_ici(ins, outs, sems) + passed:
            cp.wait_send()


class _Scatter:
    def __init__(self, sums):
        self.inputs = list(sums)
        k = N_CHIPS - 1
        self.out_shapes = [jax.ShapeDtypeStruct((k,) + g.shape[1:], g.dtype) for g in sums]
        self.scratch = [pltpu.SemaphoreType.DMA((k * len(sums),)), pltpu.SemaphoreType.DMA((k * len(sums),))]

    def _copies(self, ins, outs, sems):
        send_sems, recv_sems = sems
        _, _, c, others = _place()
        k = N_CHIPS - 1
        return [_remote(ins[i].at[2 * ox + oy], outs[i].at[j], send_sems.at[k * i + j], recv_sems.at[k * i + j],
                        (ox, oy, c))
                for i in range(len(ins)) for j, (ox, oy) in enumerate(others)]

    def start(self, ins, outs, sems):
        for cp in self._copies(ins, outs, sems):
            cp.start()

    def finish(self, ins, outs, sems):
        for cp in self._copies(ins, outs, sems):
            cp.wait()


def _exchange(carry, name):
    n = len(carry.inputs)

    def body(*refs):
        ins, outs, sems = refs[:n], refs[n:2 * n], refs[2 * n:]
        carry.start(ins, outs, sems)
        carry.finish(ins, outs, sems)

    return pl.pallas_call(
        body,
        in_specs=[ANY] * n,
        out_specs=[ANY] * n,
        out_shape=carry.out_shapes,
        scratch_shapes=carry.scratch,
        name=name,
    )(*carry.inputs)


def _call(body, *, grid, in_specs, out_specs, out_shape, scratch_shapes, semantics, name, args, carry=None):
    n_in, n_out, n_scr = len(in_specs), len(out_specs), len(scratch_shapes)
    if carry is None:
        res = pl.pallas_call(body, grid=grid, in_specs=in_specs, out_specs=out_specs, out_shape=out_shape,
                             scratch_shapes=scratch_shapes, compiler_params=_params(*semantics), name=name)(*args)
        return list(res), []
    nci, nco = len(carry.inputs), len(carry.out_shapes)
    a, b = n_in, n_in + nci
    c, d = b + n_out, b + n_out + nco
    e = d + n_scr

    def carried(*refs):
        ids = [pl.program_id(k) for k in range(len(grid))]
        first = functools.reduce(jnp.logical_and, [i == 0 for i in ids])
        last = functools.reduce(jnp.logical_and, [i == n - 1 for i, n in zip(ids, grid)])

        @pl.when(first)
        def _():
            carry.start(refs[a:b], refs[c:d], refs[e:])

        body(*refs[:a], *refs[b:c], *refs[d:e])

        @pl.when(last)
        def _():
            carry.finish(refs[a:b], refs[c:d], refs[e:])

    res = pl.pallas_call(
        carried,
        grid=grid,
        in_specs=list(in_specs) + [ANY] * nci,
        out_specs=list(out_specs) + [ANY] * nco,
        out_shape=list(out_shape) + carry.out_shapes,
        scratch_shapes=list(scratch_shapes) + carry.scratch,
        compiler_params=_params(*(["arbitrary"] * len(grid))),
        name=name,
    )(*args, *carry.inputs)
    return list(res[:n_out]), list(res[n_out:])


def _mm(a, b, mode, out_dtype, name, residual=None, carry=None):
    if mode == "nn":
        (m, k), (k2, n) = a.shape, b.shape
    elif mode == "nt":
        (m, k), (n, k2) = a.shape, b.shape
    else:
        (k, m), (k2, n) = a.shape, b.shape
    assert k == k2, (a.shape, b.shape, mode)
    tm, tn, tk = _tile(m, MM_TILE), _tile(n, MM_TILE), _tile(k, MM_TILE)
    nk = k // tk
    dot = {"nn": _dot, "nt": _dot_nt, "tn": _dot_tn}[mode]
    has_res = residual is not None

    def body(*refs):
        if has_res:
            a_ref, b_ref, r_ref, o_ref = refs[:4]
        else:
            a_ref, b_ref, o_ref = refs[:3]
            r_ref = None

        def finish(acc):
            if has_res:
                acc = acc + r_ref[...]
            o_ref[...] = acc.astype(o_ref.dtype)

        part = dot(a_ref[...], b_ref[...])
        if nk == 1:
            finish(part)
        else:
            acc_ref = refs[-1]
            kk = pl.program_id(2)

            @pl.when(kk == 0)
            def _():
                acc_ref[...] = part

            @pl.when(kk > 0)
            def _():
                acc_ref[...] += part

            @pl.when(kk == nk - 1)
            def _():
                finish(acc_ref[...])

    if mode == "tn":
        a_spec = pl.BlockSpec((tk, tm), lambda j, i, kk: (kk, i))
    else:
        a_spec = pl.BlockSpec((tm, tk), lambda j, i, kk: (i, kk))
    if mode == "nt":
        b_spec = pl.BlockSpec((tn, tk), lambda j, i, kk: (j, kk))
    else:
        b_spec = pl.BlockSpec((tk, tn), lambda j, i, kk: (kk, j))
    o_spec = pl.BlockSpec((tm, tn), lambda j, i, kk: (i, j))
    in_specs = [a_spec, b_spec] + ([o_spec] if has_res else [])
    args = (a, b) + ((residual,) if has_res else ())
    (out,), moved = _call(
        body,
        grid=(n // tn, m // tm, nk),
        in_specs=in_specs,
        out_specs=[o_spec],
        out_shape=[jax.ShapeDtypeStruct((m, n), out_dtype)],
        scratch_shapes=[pltpu.VMEM((tm, tn), F32)] if nk > 1 else [],
        semantics=("parallel", "parallel", "arbitrary"),
        name=name,
        args=args,
        carry=carry,
    )
    return out if carry is None else (out, moved)


def _rms_fwd(x, g, name):
    s, d = x.shape
    tm = _tile(s, ROW_TILE)

    def body(x_ref, g_ref, h_ref, r_ref):
        xf = x_ref[...]
        r = lax.rsqrt(jnp.mean(xf * xf, axis=-1, keepdims=True) + EPS)
        h_ref[...] = ((xf * r) * g_ref[...]).astype(BF16)
        r_ref[...] = r

    return pl.pallas_call(
        body,
        grid=(s // tm,),
        in_specs=[pl.BlockSpec((tm, d), lambda i: (i, 0)), pl.BlockSpec((1, d), lambda i: (0, 0))],
        out_specs=[pl.BlockSpec((tm, d), lambda i: (i, 0)), pl.BlockSpec((tm, 1), lambda i: (i, 0))],
        out_shape=[jax.ShapeDtypeStruct((s, d), BF16), jax.ShapeDtypeStruct((s, 1), F32)],
        compiler_params=_params("parallel"),
        name=name,
    )(x, g)


def _rms_bwd(dh, x, rstd, g, res, name):
    s, d = x.shape
    tm = _tile(s, ROW_TILE)
    has_res = res is not None

    def body(*refs):
        if has_res:
            dh_ref, x_ref, r_ref, g_ref, res_ref, dx_ref, dxb_ref, dg_ref = refs
        else:
            dh_ref, x_ref, r_ref, g_ref, dx_ref, dxb_ref, dg_ref = refs
        dhf = dh_ref[...].astype(F32)
        xhat = x_ref[...] * r_ref[...]
        dy = dhf * g_ref[...]
        dx = r_ref[...] * (dy - xhat * jnp.mean(dy * xhat, axis=-1, keepdims=True))
        if has_res:
            dx = dx + res_ref[...]
        dx_ref[...] = dx
        dxb_ref[...] = dx.astype(BF16)
        part = jnp.sum(dhf * xhat, axis=0, keepdims=True)

        @pl.when(pl.program_id(0) == 0)
        def _():
            dg_ref[...] = part

        @pl.when(pl.program_id(0) > 0)
        def _():
            dg_ref[...] += part

    row = pl.BlockSpec((tm, d), lambda i: (i, 0))
    vec = pl.BlockSpec((1, d), lambda i: (0, 0))
    in_specs = [row, row, pl.BlockSpec((tm, 1), lambda i: (i, 0)), vec] + ([row] if has_res else [])
    args = (dh, x, rstd, g) + ((res,) if has_res else ())
    return pl.pallas_call(
        body,
        grid=(s // tm,),
        in_specs=in_specs,
        out_specs=[row, row, vec],
        out_shape=[jax.ShapeDtypeStruct((s, d), F32), jax.ShapeDtypeStruct((s, d), BF16),
                   jax.ShapeDtypeStruct((1, d), F32)],
        compiler_params=_params("arbitrary"),
        name=name,
    )(*args)


def _headnorm_fwd(src, col0, nheads, g, name):
    s = src.shape[0]
    tm = _tile(s, HEADNORM_ROWS)
    w = nheads * HEAD_DIM
    assert col0 % nheads == 0

    def body(x_ref, g_ref, o_ref):
        for hh in range(nheads):
            xf = _head(x_ref, hh).astype(F32)
            r = lax.rsqrt(jnp.mean(xf * xf, axis=-1, keepdims=True) + EPS)
            o_ref[:, hh * HEAD_DIM:(hh + 1) * HEAD_DIM] = ((xf * r) * g_ref[...]).astype(BF16)

    return pl.pallas_call(
        body,
        grid=(s // tm,),
        in_specs=[pl.BlockSpec((tm, w), lambda i: (i, col0 // nheads)),
                  pl.BlockSpec((1, HEAD_DIM), lambda i: (0, 0))],
        out_specs=pl.BlockSpec((tm, w), lambda i: (i, 0)),
        out_shape=jax.ShapeDtypeStruct((s, w), BF16),
        compiler_params=_params("parallel"),
        name=name,
    )(src, g)


def _headnorm_bwd(dxn, src, col0, nheads, g, name):
    s = src.shape[0]
    tm = _tile(s, HEADNORM_ROWS)
    w = nheads * HEAD_DIM
    assert col0 % nheads == 0

    def body(d_ref, x_ref, g_ref, dx_ref, dg_ref):
        part = jnp.zeros((1, HEAD_DIM), F32)
        for hh in range(nheads):
            xf = _head(x_ref, hh).astype(F32)
            r = lax.rsqrt(jnp.mean(xf * xf, axis=-1, keepdims=True) + EPS)
            xhat = xf * r
            dn = _head(d_ref, hh).astype(F32)
            dy = dn * g_ref[...]
            dx = r * (dy - xhat * jnp.mean(dy * xhat, axis=-1, keepdims=True))
            dx_ref[:, hh * HEAD_DIM:(hh + 1) * HEAD_DIM] = dx.astype(BF16)
            part = part + jnp.sum(dn * xhat, axis=0, keepdims=True)

        @pl.when(pl.program_id(0) == 0)
        def _():
            dg_ref[...] = part

        @pl.when(pl.program_id(0) > 0)
        def _():
            dg_ref[...] += part

    return pl.pallas_call(
        body,
        grid=(s // tm,),
        in_specs=[pl.BlockSpec((tm, w), lambda i: (i, 0)),
                  pl.BlockSpec((tm, w), lambda i: (i, col0 // nheads)),
                  pl.BlockSpec((1, HEAD_DIM), lambda i: (0, 0))],
        out_specs=[pl.BlockSpec((tm, w), lambda i: (i, 0)),
                   pl.BlockSpec((1, HEAD_DIM), lambda i: (0, 0))],
        out_shape=[jax.ShapeDtypeStruct((s, w), BF16), jax.ShapeDtypeStruct((1, HEAD_DIM), F32)],
        compiler_params=_params("arbitrary"),
        name=name,
    )(dxn, src, g)


def _tri(t, lower_inclusive):
    r = lax.broadcasted_iota(jnp.int32, (t, t), 0)
    c = lax.broadcasted_iota(jnp.int32, (t, t), 1)
    keep = (c <= r) if lower_inclusive else (c >= r)
    return jnp.where(keep, 1.0, 0.0).astype(BF16)


def _forget_fwd(f_logit, b_pad):
    s = f_logit.shape[0]
    t = _tile(s, ATT_TILE)

    def body(f_ref, b_ref, c_ref, carry):
        @pl.when(pl.program_id(0) == 0)
        def _():
            carry[...] = jnp.zeros_like(carry)

        lf = _log_sigmoid(f_ref[...] + b_ref[...])
        tri = _tri(t, True)
        acc = carry[...]
        for part in _split3(lf):
            acc = acc + _dot(tri, part)
        c_ref[...] = acc
        carry[...] += jnp.sum(lf, axis=0, keepdims=True)

    return pl.pallas_call(
        body,
        grid=(s // t,),
        in_specs=[pl.BlockSpec((t, LANES), lambda i: (i, 0)), pl.BlockSpec((1, LANES), lambda i: (0, 0))],
        out_specs=pl.BlockSpec((t, LANES), lambda i: (i, 0)),
        out_shape=jax.ShapeDtypeStruct((s, LANES), F32),
        scratch_shapes=[pltpu.VMEM((1, LANES), F32)],
        compiler_params=_params("arbitrary"),
        name="forget_fwd",
    )(f_logit, b_pad)


def _forget_bwd(dc, f_logit, b_pad):
    s = f_logit.shape[0]
    t = _tile(s, ATT_TILE)
    nb = s // t

    def body(dc_ref, f_ref, b_ref, df_ref, db_ref, carry):
        @pl.when(pl.program_id(0) == 0)
        def _():
            carry[...] = jnp.zeros_like(carry)
            db_ref[...] = jnp.zeros_like(db_ref)

        d = dc_ref[...]
        tri = _tri(t, False)
        acc = carry[...]
        for part in _split3(d):
            acc = acc + _dot(tri, part)
        z = f_ref[...] + b_ref[...]
        df = acc * jnp.exp(_log_sigmoid(-z))
        df_ref[...] = df
        db_ref[...] += jnp.sum(df, axis=0, keepdims=True)
        carry[...] += jnp.sum(d, axis=0, keepdims=True)

    rev = pl.BlockSpec((t, LANES), lambda i: (nb - 1 - i, 0))
    vec = pl.BlockSpec((1, LANES), lambda i: (0, 0))
    return pl.pallas_call(
        body,
        grid=(nb,),
        in_specs=[rev, rev, vec],
        out_specs=[rev, vec],
        out_shape=[jax.ShapeDtypeStruct((s, LANES), F32), jax.ShapeDtypeStruct((1, LANES), F32)],
        scratch_shapes=[pltpu.VMEM((1, LANES), F32)],
        compiler_params=_params("arbitrary"),
        name="forget_bwd",
    )(dc, f_logit, b_pad)


HEAD_GROUP = 2
GROUP_W = HEAD_GROUP * HEAD_DIM


def _head(ref, hh, rows=slice(None)):
    return ref[rows, hh * HEAD_DIM:(hh + 1) * HEAD_DIM]


def _tri_mask(t, strict):
    r = lax.broadcasted_iota(jnp.int32, (t, t), 0)
    c = lax.broadcasted_iota(jnp.int32, (t, t), 1)
    return (c < r) if strict else (c <= r)


def _fox_fwd(qn, kn, proj, colv, c_col, c_row, nheads, carry=None):
    s = qn.shape[0]
    t = _tile(s, ATT_TILE)
    scale = HEAD_DIM ** -0.5
    hg = HEAD_GROUP
    assert nheads % hg == 0 and colv % hg == 0

    def body(q_ref, k_ref, v_ref, cc_ref, cr_ref, o_ref, of_ref, lse_ref):
        qi = pl.program_id(1)
        causal = _tri_mask(t, False)

        def tile(kj, carry, diagonal):
            off = pl.multiple_of(kj * t, t)
            out = []
            for hh in range(hg):
                m, l, acc = carry[hh]
                k = _head(k_ref, hh, pl.ds(off, t))
                v = _head(v_ref, hh, pl.ds(off, t))
                sc = _dot_nt(_head(q_ref, hh), k) * scale + (cc_ref[hh] - cr_ref[hh, :, pl.ds(off, t)])
                if diagonal:
                    sc = jnp.where(causal, sc, NEG_BIG)
                m_new = jnp.maximum(m, jnp.max(sc, axis=-1, keepdims=True))
                p = jnp.exp(sc - m_new)
                alpha = jnp.exp(m - m_new)
                l = alpha * l + jnp.sum(p, axis=-1, keepdims=True)
                acc = alpha * acc + _dot(p.astype(BF16), v)
                out.append((m_new, l, acc))
            return tuple(out)

        init = tuple((jnp.full((t, 1), NEG_BIG, F32), jnp.zeros((t, 1), F32), jnp.zeros((t, HEAD_DIM), F32))
                     for _ in range(hg))
        carry = lax.fori_loop(0, qi, lambda kj, c: tile(kj, c, False), init)
        carry = tile(qi, carry, True)
        for hh in range(hg):
            m, l, acc = carry[hh]
            o = acc / l
            of_ref[:, hh * HEAD_DIM:(hh + 1) * HEAD_DIM] = o
            o_ref[:, hh * HEAD_DIM:(hh + 1) * HEAD_DIM] = o.astype(BF16)
            lse_ref[hh] = m + jnp.log(l)

    tile_spec = pl.BlockSpec((t, GROUP_W), lambda h, i: (i, h))
    w = nheads * HEAD_DIM
    return _call(
        body,
        grid=(nheads // hg, s // t),
        in_specs=[tile_spec,
                  pl.BlockSpec((s, GROUP_W), lambda h, i: (0, h)),
                  pl.BlockSpec((s, GROUP_W), lambda h, i: (0, colv // hg + h)),
                  pl.BlockSpec((hg, t, 1), lambda h, i: (h, i, 0)),
                  pl.BlockSpec((hg, 1, s), lambda h, i: (h, 0, 0))],
        out_specs=[tile_spec, tile_spec, pl.BlockSpec((hg, t, 1), lambda h, i: (h, i, 0))],
        out_shape=[jax.ShapeDtypeStruct((s, w), BF16), jax.ShapeDtypeStruct((s, w), F32),
                   jax.ShapeDtypeStruct((nheads, s, 1), F32)],
        scratch_shapes=[],
        semantics=("parallel", "parallel"),
        name="fox_fwd",
        args=(qn, kn, proj, c_col, c_row),
        carry=carry,
    )


def _fox_bwd(qn, kn, proj, colv, c_col, c_row, o, do, lse, nheads, carry=None):
    s = qn.shape[0]
    t = _tile(s, ATT_TILE)
    scale = HEAD_DIM ** -0.5
    hg = HEAD_GROUP
    assert nheads % hg == 0 and colv % hg == 0

    def body(q_ref, k_ref, v_ref, cc_ref, cr_ref, o_ref, do_ref, lse_ref,
             dq_ref, dk_ref, dv_ref, drs_ref, dcs_ref):
        qi = pl.program_id(1)

        @pl.when(qi == 0)
        def _():
            dk_ref[...] = jnp.zeros_like(dk_ref)
            dv_ref[...] = jnp.zeros_like(dv_ref)
            dcs_ref[...] = jnp.zeros_like(dcs_ref)

        causal = _tri_mask(t, False)
        delta = [jnp.sum(_head(o_ref, hh) * _head(do_ref, hh).astype(F32), axis=-1, keepdims=True)
                 for hh in range(hg)]

        def tile(kj, carry, diagonal):
            off = pl.multiple_of(kj * t, t)
            out = []
            for hh in range(hg):
                dq, rs = carry[hh]
                cols = slice(hh * HEAD_DIM, (hh + 1) * HEAD_DIM)
                q = _head(q_ref, hh)
                do_ = _head(do_ref, hh)
                k = _head(k_ref, hh, pl.ds(off, t))
                v = _head(v_ref, hh, pl.ds(off, t))
                sc = _dot_nt(q, k) * scale + (cc_ref[hh] - cr_ref[hh, :, pl.ds(off, t)])
                p = jnp.exp(sc - lse_ref[hh])
                if diagonal:
                    p = jnp.where(causal, p, 0.0)
                ds = p * (_dot_nt(do_, v) - delta[hh])
                dsb = ds.astype(BF16)
                dv_ref[pl.ds(off, t), cols] += _dot_tn(p.astype(BF16), do_)
                dk_ref[pl.ds(off, t), cols] += _dot_tn(dsb, q) * scale
                dcs_ref[hh, :, pl.ds(off, t)] += jnp.sum(ds, axis=0, keepdims=True)
                out.append((dq + _dot(dsb, k) * scale, rs + jnp.sum(ds, axis=-1, keepdims=True)))
            return tuple(out)

        init = tuple((jnp.zeros((t, HEAD_DIM), F32), jnp.zeros((t, 1), F32)) for _ in range(hg))
        carry = lax.fori_loop(0, qi, lambda kj, c: tile(kj, c, False), init)
        carry = tile(qi, carry, True)
        for hh in range(hg):
            dq_ref[:, hh * HEAD_DIM:(hh + 1) * HEAD_DIM] = carry[hh][0]
            drs_ref[hh] = carry[hh][1]

    tile_spec = pl.BlockSpec((t, GROUP_W), lambda h, i: (i, h))
    full = pl.BlockSpec((s, GROUP_W), lambda h, i: (0, h))
    colspec = pl.BlockSpec((hg, t, 1), lambda h, i: (h, i, 0))
    rowspec = pl.BlockSpec((hg, 1, s), lambda h, i: (h, 0, 0))
    w = nheads * HEAD_DIM
    return _call(
        body,
        grid=(nheads // hg, s // t),
        in_specs=[tile_spec, full, pl.BlockSpec((s, GROUP_W), lambda h, i: (0, colv // hg + h)), colspec, rowspec,
                  tile_spec, tile_spec, colspec],
        out_specs=[tile_spec, full, full, colspec, rowspec],
        out_shape=[jax.ShapeDtypeStruct((s, w), F32), jax.ShapeDtypeStruct((s, w), F32),
                   jax.ShapeDtypeStruct((s, w), F32), jax.ShapeDtypeStruct((nheads, s, 1), F32),
                   jax.ShapeDtypeStruct((nheads, 1, s), F32)],
        scratch_shapes=[],
        semantics=("arbitrary", "arbitrary"),
        name="fox_bwd",
        args=(qn, kn, proj, c_col, c_row, o, do, lse),
        carry=carry,
    )


def _sb_tile(q, k, scale, later, valid):
    z = _dot_nt(q, k) * scale
    lb = _log_sigmoid(z)
    lm = lb - z
    if valid is not None:
        lm = jnp.where(valid, lm, 0.0)
    hi, lo = _split2(lm)
    suffix = _dot(hi, later) + _dot(lo, later)
    return lb, lm, suffix


def _later(t):
    r = lax.broadcasted_iota(jnp.int32, (t, t), 0)
    c = lax.broadcasted_iota(jnp.int32, (t, t), 1)
    return jnp.where(r > c, 1.0, 0.0).astype(BF16)


def _sb_fwd(proj, colq, colk, colv, nheads):
    s = proj.shape[0]
    t = _tile(s, ATT_TILE)
    scale = HEAD_DIM ** -0.5
    hg = HEAD_GROUP
    assert nheads % hg == 0 and colq % hg == 0 and colk % hg == 0 and colv % hg == 0

    def body(q_ref, k_ref, v_ref, o_ref):
        qi = pl.program_id(1)
        later = _later(t)
        before = _tri_mask(t, True)

        def tile(kj, carry, diagonal):
            off = pl.multiple_of(kj * t, t)
            out = []
            for hh in range(hg):
                rc, acc = carry[hh]
                k = _head(k_ref, hh, pl.ds(off, t))
                v = _head(v_ref, hh, pl.ds(off, t))
                lb, lm, suffix = _sb_tile(_head(q_ref, hh), k, scale, later, before if diagonal else None)
                a = jnp.exp(lb + suffix + rc)
                if diagonal:
                    a = jnp.where(before, a, 0.0)
                out.append((rc + jnp.sum(lm, axis=-1, keepdims=True), acc + _dot(a.astype(BF16), v)))
            return tuple(out)

        init = tuple((jnp.zeros((t, 1), F32), jnp.zeros((t, HEAD_DIM), F32)) for _ in range(hg))
        carry = tile(qi, init, True)
        carry = lax.fori_loop(1, qi + 1, lambda i, c: tile(qi - i, c, False), carry)
        for hh in range(hg):
            o_ref[:, hh * HEAD_DIM:(hh + 1) * HEAD_DIM] = carry[hh][1].astype(BF16)

    return pl.pallas_call(
        body,
        grid=(nheads // hg, s // t),
        in_specs=[pl.BlockSpec((t, GROUP_W), lambda h, i: (i, colq // hg + h)),
                  pl.BlockSpec((s, GROUP_W), lambda h, i: (0, colk // hg + h)),
                  pl.BlockSpec((s, GROUP_W), lambda h, i: (0, colv // hg + h))],
        out_specs=pl.BlockSpec((t, GROUP_W), lambda h, i: (i, h)),
        out_shape=jax.ShapeDtypeStruct((s, nheads * HEAD_DIM), BF16),
        compiler_params=_params("parallel", "parallel"),
        name="sb_fwd",
    )(proj, proj, proj)


def _sb_bwd(proj, colq, colk, colv, do, nheads, carry=None):
    s = proj.shape[0]
    t = _tile(s, ATT_TILE)
    scale = HEAD_DIM ** -0.5
    hg = HEAD_GROUP
    assert nheads % hg == 0 and colq % hg == 0 and colk % hg == 0 and colv % hg == 0

    def body(q_ref, k_ref, v_ref, do_ref, dq_ref, dk_ref, dv_ref, g_s, beta_s):
        qi = pl.program_id(1)

        @pl.when(qi == 0)
        def _():
            dk_ref[...] = jnp.zeros_like(dk_ref)
            dv_ref[...] = jnp.zeros_like(dv_ref)

        later = _later(t)
        before = _tri_mask(t, True)

        def back(kj, carry, diagonal):
            off = pl.multiple_of(kj * t, t)
            out = []
            for hh in range(hg):
                cols = slice(hh * HEAD_DIM, (hh + 1) * HEAD_DIM)
                do_ = _head(do_ref, hh)
                k = _head(k_ref, hh, pl.ds(off, t))
                v = _head(v_ref, hh, pl.ds(off, t))
                lb, lm, suffix = _sb_tile(_head(q_ref, hh), k, scale, later, before if diagonal else None)
                a = jnp.exp(lb + suffix + carry[hh])
                if diagonal:
                    a = jnp.where(before, a, 0.0)
                g_s[hh, :, pl.ds(off, t)] = a * _dot_nt(do_, v)
                beta_s[hh, :, pl.ds(off, t)] = jnp.exp(lb)
                dv_ref[pl.ds(off, t), cols] += _dot_tn(a.astype(BF16), do_)
                out.append(carry[hh] + jnp.sum(lm, axis=-1, keepdims=True))
            return tuple(out)

        rc = back(qi, tuple(jnp.zeros((t, 1), F32) for _ in range(hg)), True)
        lax.fori_loop(1, qi + 1, lambda i, c: back(qi - i, c, False), rc)

        earlier = jnp.where(_tri_mask(t, False), 0.0, 1.0).astype(BF16)

        def fwd(kj, carry, diagonal):
            off = pl.multiple_of(kj * t, t)
            out = []
            for hh in range(hg):
                gc, dq = carry[hh]
                cols = slice(hh * HEAD_DIM, (hh + 1) * HEAD_DIM)
                k = _head(k_ref, hh, pl.ds(off, t))
                g = g_s[hh, :, pl.ds(off, t)]
                beta = beta_s[hh, :, pl.ds(off, t)]
                hi, lo = _split2(g)
                gsum = _dot(hi, earlier) + _dot(lo, earlier) + gc
                dz = g * (1.0 - beta) - gsum * beta
                if diagonal:
                    dz = jnp.where(before, dz, 0.0)
                dz = dz.astype(BF16)
                dk_ref[pl.ds(off, t), cols] += _dot_tn(dz, _head(q_ref, hh)) * scale
                out.append((gc + jnp.sum(g, axis=-1, keepdims=True), dq + _dot(dz, k) * scale))
            return tuple(out)

        init = tuple((jnp.zeros((t, 1), F32), jnp.zeros((t, HEAD_DIM), F32)) for _ in range(hg))
        carry = lax.fori_loop(0, qi, lambda kj, c: fwd(kj, c, False), init)
        carry = fwd(qi, carry, True)
        for hh in range(hg):
            dq_ref[:, hh * HEAD_DIM:(hh + 1) * HEAD_DIM] = carry[hh][1]

    tile_spec = pl.BlockSpec((t, GROUP_W), lambda h, i: (i, h))
    full = pl.BlockSpec((s, GROUP_W), lambda h, i: (0, h))
    w = nheads * HEAD_DIM
    return _call(
        body,
        grid=(nheads // hg, s // t),
        in_specs=[pl.BlockSpec((t, GROUP_W), lambda h, i: (i, colq // hg + h)),
                  pl.BlockSpec((s, GROUP_W), lambda h, i: (0, colk // hg + h)),
                  pl.BlockSpec((s, GROUP_W), lambda h, i: (0, colv // hg + h)),
                  tile_spec],
        out_specs=[tile_spec, full, full],
        out_shape=[jax.ShapeDtypeStruct((s, w), F32)] * 3,
        scratch_shapes=[pltpu.VMEM((hg, t, s), F32), pltpu.VMEM((hg, t, s), F32)],
        semantics=("arbitrary", "arbitrary"),
        name="sb_bwd",
        args=(proj, proj, proj, do),
        carry=carry,
    )


def _mem_fwd(qn, kn, mkv, nheads):
    s = qn.shape[0]
    mtok = kn.shape[0]
    t = _tile(s, ATT_TILE)
    scale = HEAD_DIM ** -0.5

    def body(q_ref, k_ref, v_ref, o_ref):
        sc = _dot_nt(q_ref[...], k_ref[...]) * scale
        p = jnp.exp(sc - jnp.max(sc, axis=-1, keepdims=True))
        p = p / jnp.sum(p, axis=-1, keepdims=True)
        o_ref[...] = _dot(p.astype(BF16), v_ref[...]).astype(BF16)

    return pl.pallas_call(
        body,
        grid=(nheads, s // t),
        in_specs=[pl.BlockSpec((t, HEAD_DIM), lambda h, i: (i, h)),
                  pl.BlockSpec((mtok, HEAD_DIM), lambda h, i: (0, h)),
                  pl.BlockSpec((mtok, HEAD_DIM), lambda h, i: (0, nheads + h))],
        out_specs=pl.BlockSpec((t, HEAD_DIM), lambda h, i: (i, h)),
        out_shape=jax.ShapeDtypeStruct((s, nheads * HEAD_DIM), BF16),
        compiler_params=_params("parallel", "parallel"),
        name="mem_fwd",
    )(qn, kn, mkv)


def _mem_bwd(qn, kn, mkv, do, nheads):
    s = qn.shape[0]
    mtok = kn.shape[0]
    t = _tile(s, ATT_TILE)
    scale = HEAD_DIM ** -0.5

    def body(q_ref, k_ref, v_ref, do_ref, dq_ref, dk_ref, dv_ref):
        @pl.when(pl.program_id(1) == 0)
        def _():
            dk_ref[...] = jnp.zeros_like(dk_ref)
            dv_ref[...] = jnp.zeros_like(dv_ref)

        q = q_ref[...]
        k = k_ref[...]
        do_ = do_ref[...]
        sc = _dot_nt(q, k) * scale
        p = jnp.exp(sc - jnp.max(sc, axis=-1, keepdims=True))
        p = p / jnp.sum(p, axis=-1, keepdims=True)
        dp = _dot_nt(do_, v_ref[...])
        ds = (p * (dp - jnp.sum(p * dp, axis=-1, keepdims=True))).astype(BF16)
        dq_ref[...] = _dot(ds, k) * scale
        dk_ref[...] += _dot_tn(ds, q) * scale
        dv_ref[...] += _dot_tn(p.astype(BF16), do_)

    tile = pl.BlockSpec((t, HEAD_DIM), lambda h, i: (i, h))
    kspec = pl.BlockSpec((mtok, HEAD_DIM), lambda h, i: (0, h))
    w = nheads * HEAD_DIM
    return pl.pallas_call(
        body,
        grid=(nheads, s // t),
        in_specs=[tile, kspec, pl.BlockSpec((mtok, HEAD_DIM), lambda h, i: (0, nheads + h)), tile],
        out_specs=[tile, kspec, kspec],
        out_shape=[jax.ShapeDtypeStruct((s, w), F32), jax.ShapeDtypeStruct((mtok, w), F32),
                   jax.ShapeDtypeStruct((mtok, w), F32)],
        compiler_params=_params("arbitrary", "arbitrary"),
        name="mem_bwd",
    )(qn, kn, mkv, do)


def _merge_fwd(p0, p1, p2, proj, colg, b_gate):
    s, d = p0.shape
    tm, tn = _tile(s, ROW_TILE), _tile(d, COL_TILE)
    g0 = colg * LANES // tn
    nj = d // tn

    def body(p0_ref, p1_ref, p2_ref, ga_ref, gb_ref, gc_ref, b_ref, o_ref):
        acc = jnp.zeros((tm, tn), F32)
        for b, (p_ref, g_ref) in enumerate(((p0_ref, ga_ref), (p1_ref, gb_ref), (p2_ref, gc_ref))):
            gate = jax.nn.sigmoid(g_ref[...].astype(F32) + b_ref[b:b + 1, :])
            acc = acc + gate * p_ref[...]
        o_ref[...] = acc.astype(BF16)

    blk = pl.BlockSpec((tm, tn), lambda i, j: (i, j))
    gates = [pl.BlockSpec((tm, tn), functools.partial(lambda i, j, b: (i, g0 + b * nj + j), b=b)) for b in range(3)]
    return pl.pallas_call(
        body,
        grid=(s // tm, nj),
        in_specs=[blk, blk, blk] + gates + [pl.BlockSpec((3, tn), lambda i, j: (0, j))],
        out_specs=blk,
        out_shape=jax.ShapeDtypeStruct((s, d), BF16),
        compiler_params=_params("parallel", "parallel"),
        name="merge_fwd",
    )(p0, p1, p2, proj, proj, proj, b_gate)


def _merge_bwd(dmerged, p0, p1, p2, proj, colg, b_gate):
    s, d = p0.shape
    tm, tn = _tile(s, ROW_TILE), _tile(d, COL_TILE)
    g0 = colg * LANES // tn
    nj = d // tn

    def body(dm_ref, p0_ref, p1_ref, p2_ref, ga_ref, gb_ref, gc_ref, b_ref,
             d0_ref, d1_ref, d2_ref, dga_ref, dgb_ref, dgc_ref, db_ref):
        dm = dm_ref[...].astype(F32)
        parts = []
        for b, (p_ref, g_ref, dp_ref, dg_ref) in enumerate(((p0_ref, ga_ref, d0_ref, dga_ref),
                                                            (p1_ref, gb_ref, d1_ref, dgb_ref),
                                                            (p2_ref, gc_ref, d2_ref, dgc_ref))):
            gate = jax.nn.sigmoid(g_ref[...].astype(F32) + b_ref[b:b + 1, :])
            dp_ref[...] = (dm * gate).astype(BF16)
            dgate = dm * p_ref[...] * gate * (1.0 - gate)
            dg_ref[...] = dgate.astype(BF16)
            parts.append(jnp.sum(dgate, axis=0, keepdims=True))
        part = jnp.concatenate(parts, axis=0)

        @pl.when(pl.program_id(1) == 0)
        def _():
            db_ref[...] = part

        @pl.when(pl.program_id(1) > 0)
        def _():
            db_ref[...] += part

    blk = pl.BlockSpec((tm, tn), lambda j, i: (i, j))
    gates = [pl.BlockSpec((tm, tn), functools.partial(lambda j, i, b: (i, g0 + b * nj + j), b=b)) for b in range(3)]
    bias = pl.BlockSpec((3, tn), lambda j, i: (0, j))
    return pl.pallas_call(
        body,
        grid=(nj, s // tm),
        in_specs=[blk, blk, blk, blk] + gates + [bias],
        out_specs=[blk] * 6 + [bias],
        out_shape=[jax.ShapeDtypeStruct((s, d), BF16)] * 6 + [jax.ShapeDtypeStruct((3, d), F32)],
        compiler_params=_params("parallel", "arbitrary"),
        name="merge_bwd",
    )(dmerged, p0, p1, p2, proj, proj, proj, b_gate)


def _shift_down(v, n):
    rows = lax.broadcasted_iota(jnp.int32, v.shape, 0)
    return jnp.where(rows >= n, pltpu.roll(v, n, 0), 0.0)


def _shift_up(v, n):
    s = v.shape[0]
    rows = lax.broadcasted_iota(jnp.int32, v.shape, 0)
    return jnp.where(rows < s - n, pltpu.roll(v, s - n, 0), 0.0)


def _conv(v, w_ref, b_ref):
    taps = w_ref.shape[0]
    out = v * w_ref[taps - 1:taps, :] + b_ref[...]
    for n in range(1, taps):
        out = out + _shift_down(v, n) * w_ref[taps - 1 - n:taps - n, :]
    return out


def _conv_act_fwd(up, conv_w, conv_b):
    s, f2 = up.shape
    f = f2 // 2
    tn = LANES
    nj = f // tn
    taps = conv_w.shape[0]

    def body(ug_ref, uv_ref, wg_ref, wv_ref, bg_ref, bv_ref, o_ref):
        cg = _conv(ug_ref[...].astype(F32), wg_ref, bg_ref)
        cv = _conv(uv_ref[...].astype(F32), wv_ref, bv_ref)
        o_ref[...] = (cg * jax.nn.sigmoid(cg) * cv).astype(BF16)

    return pl.pallas_call(
        body,
        grid=(nj,),
        in_specs=[pl.BlockSpec((s, tn), lambda j: (0, j)), pl.BlockSpec((s, tn), lambda j: (0, nj + j)),
                  pl.BlockSpec((taps, tn), lambda j: (0, j)), pl.BlockSpec((taps, tn), lambda j: (0, nj + j)),
                  pl.BlockSpec((1, tn), lambda j: (0, j)), pl.BlockSpec((1, tn), lambda j: (0, nj + j))],
        out_specs=pl.BlockSpec((s, tn), lambda j: (0, j)),
        out_shape=jax.ShapeDtypeStruct((s, f), BF16),
        compiler_params=_params("parallel"),
        name="conv_act_fwd",
    )(up, up, conv_w, conv_w, conv_b, conv_b)


def _conv_act_bwd(up, conv_w, conv_b, dact):
    s, f2 = up.shape
    f = f2 // 2
    tn = LANES
    nj = f // tn
    taps = conv_w.shape[0]

    def half(v, du, w_ref, dup_ref, dw_ref, db_ref):
        dup = du * w_ref[taps - 1:taps, :]
        rows = [None] * taps
        rows[taps - 1] = jnp.sum(du * v, axis=0, keepdims=True)
        for n in range(1, taps):
            dup = dup + _shift_up(du, n) * w_ref[taps - 1 - n:taps - n, :]
            rows[taps - 1 - n] = jnp.sum(du * _shift_down(v, n), axis=0, keepdims=True)
        dup_ref[...] = dup.astype(BF16)
        dw_ref[...] = jnp.concatenate(rows, axis=0)
        db_ref[...] = jnp.sum(du, axis=0, keepdims=True)

    def body(ug_ref, uv_ref, wg_ref, wv_ref, bg_ref, bv_ref, da_ref,
             dug_ref, duv_ref, dwg_ref, dwv_ref, dbg_ref, dbv_ref):
        ug = ug_ref[...].astype(F32)
        uv = uv_ref[...].astype(F32)
        cg = _conv(ug, wg_ref, bg_ref)
        cv = _conv(uv, wv_ref, bv_ref)
        da = da_ref[...].astype(F32)
        sg = jax.nn.sigmoid(cg)
        dcv = da * cg * sg
        dcg = da * cv * (sg + cg * sg * (1.0 - sg))
        half(ug, dcg, wg_ref, dug_ref, dwg_ref, dbg_ref)
        half(uv, dcv, wv_ref, duv_ref, dwv_ref, dbv_ref)

    lo = lambda rows: pl.BlockSpec((rows, tn), lambda j: (0, j))
    hi = lambda rows: pl.BlockSpec((rows, tn), lambda j: (0, nj + j))
    return pl.pallas_call(
        body,
        grid=(nj,),
        in_specs=[lo(s), hi(s), lo(taps), hi(taps), lo(1), hi(1), lo(s)],
        out_specs=[lo(s), lo(s), lo(taps), lo(taps), lo(1), lo(1)],
        out_shape=[jax.ShapeDtypeStruct((s, f), BF16)] * 2 + [jax.ShapeDtypeStruct((taps, f), F32)] * 2
        + [jax.ShapeDtypeStruct((1, f), F32)] * 2,
        compiler_params=_params("parallel"),
        name="conv_act_bwd",
    )(up, up, conv_w, conv_w, conv_b, conv_b, dact)


def _loss_grad(y, target):
    s, d = y.shape
    tm = _tile(s, ROW_TILE)

    def body(y_ref, t_ref, dy_ref, dyb_ref, l_ref):
        e = y_ref[...] - t_ref[...]
        dy = e * (1.0 / d)
        dy_ref[...] = dy
        dyb_ref[...] = dy.astype(BF16)
        tot = jnp.sum(jnp.sum(e * e, axis=-1, keepdims=True), axis=0, keepdims=True)
        l_ref[...] = jnp.broadcast_to(tot, (8, LANES))

    row = pl.BlockSpec((tm, d), lambda i: (i, 0))
    return pl.pallas_call(
        body,
        grid=(s // tm,),
        in_specs=[row, row],
        out_specs=[row, row, pl.BlockSpec((8, LANES), lambda i: (i, 0))],
        out_shape=[jax.ShapeDtypeStruct((s, d), F32), jax.ShapeDtypeStruct((s, d), BF16),
                   jax.ShapeDtypeStruct((s // tm * 8, LANES), F32)],
        compiler_params=_params("parallel"),
        name="loss_grad",
    )(y, target)


def _row_tile(rows, row_bytes, budget):
    if rows * row_bytes <= budget or rows % 8:
        return rows
    best = 8
    for t in range(8, rows, 8):
        if rows % t == 0 and t * row_bytes <= budget:
            best = t
    return best


def _adamw(w, g, m, v, name):
    r, c = w.shape
    tr = _row_tile(r, c * 4, ADAM_BLOCK_BYTES)

    def body(w_ref, g_ref, m_ref, v_ref, d_ref, mo_ref, vo_ref):
        gg = g_ref[...]
        m_new = ADAM_B1 * m_ref[...] + (1.0 - ADAM_B1) * gg
        v_new = ADAM_B2 * v_ref[...] + (1.0 - ADAM_B2) * (gg * gg)
        m_hat = m_new / (1.0 - ADAM_B1 ** ADAM_STEP)
        v_hat = v_new / (1.0 - ADAM_B2 ** ADAM_STEP)
        d_ref[...] = -ADAM_LR * (m_hat / (jnp.sqrt(v_hat) + ADAM_EPS) + ADAM_WD * w_ref[...])
        mo_ref[...] = m_new
        vo_ref[...] = v_new

    blk = pl.BlockSpec((tr, c), lambda i: (i, 0))
    return pl.pallas_call(
        body,
        grid=(r // tr,),
        in_specs=[blk] * 4,
        out_specs=[blk] * 3,
        out_shape=[jax.ShapeDtypeStruct((r, c), F32)] * 3,
        compiler_params=_params("parallel"),
        name=name,
    )(w, g, m, v)


def _add_sibling(g, r1, core, name):
    _, _, h, c = g.shape
    th = _row_tile(h, c * 2, ADAM_BLOCK_BYTES)

    def body(core_ref, g_ref, r_ref, o_ref):
        o_ref[...] = (g_ref[...].astype(F32) + r_ref[...].astype(F32)).astype(BF16)

    return pl.pallas_call(
        body,
        grid_spec=pltpu.PrefetchScalarGridSpec(
            num_scalar_prefetch=1,
            grid=(N_CHIPS, h // th),
            in_specs=[pl.BlockSpec((None, None, th, c), lambda j, i, core_ref: (j, core_ref[0], i, 0)),
                      pl.BlockSpec((None, th, c), lambda j, i, core_ref: (j, i, 0))],
            out_specs=pl.BlockSpec((None, th, c), lambda j, i, core_ref: (j, i, 0)),
        ),
        out_shape=jax.ShapeDtypeStruct((N_CHIPS, h, c), BF16),
        compiler_params=_params("parallel", "parallel"),
        name=name,
    )(core, g, r1)


def _add_chips(hsum, r2, chip_core, name):
    _, h, c = hsum.shape
    th = _row_tile(h, c * 4, ADAM_BLOCK_BYTES)

    def body(sel_ref, own_ref, r_ref, o_ref):
        acc = own_ref[...].astype(F32)
        for j in range(N_CHIPS - 1):
            acc = acc + r_ref[j].astype(F32)
        o_ref[...] = acc

    return pl.pallas_call(
        body,
        grid_spec=pltpu.PrefetchScalarGridSpec(
            num_scalar_prefetch=1,
            grid=(h // th,),
            in_specs=[pl.BlockSpec((None, th, c), lambda i, sel_ref: (sel_ref[0], i, 0)),
                      pl.BlockSpec((N_CHIPS - 1, th, c), lambda i, sel_ref: (0, i, 0))],
            out_specs=pl.BlockSpec((None, th, c), lambda i, sel_ref: (sel_ref[1], i, 0)),
        ),
        out_shape=jax.ShapeDtypeStruct((2, h, c), F32),
        compiler_params=_params("parallel"),
        name=name,
    )(chip_core, hsum, r2)


def _sum_devices(parts):
    _, r, c = parts.shape

    def body(p_ref, o_ref):
        acc = p_ref[0]
        for j in range(1, N_DEV):
            acc = acc + p_ref[j]
        o_ref[...] = acc

    return pl.pallas_call(
        body,
        out_shape=jax.ShapeDtypeStruct((r, c), F32),
        compiler_params=pltpu.CompilerParams(vmem_limit_bytes=VMEM_LIMIT_BYTES),
        name="sum_devices",
    )(parts)


def _swap_halves(grads, name):
    n = len(grads)

    def body(*refs):
        ins, outs = refs[:n], refs[n:2 * n]
        send_sems, recv_sems = refs[2 * n:]
        x, y, c, _ = _place()
        copies = [_remote(ins[i].at[:, 1 - c], outs[i], send_sems.at[i], recv_sems.at[i], (x, y, 1 - c))
                  for i in range(n)]
        for cp in copies:
            cp.start()
        for cp in copies:
            cp.wait()

    return pl.pallas_call(
        body,
        in_specs=[ANY] * n,
        out_specs=[ANY] * n,
        out_shape=[jax.ShapeDtypeStruct((g.shape[0],) + g.shape[2:], g.dtype) for g in grads],
        scratch_shapes=[pltpu.SemaphoreType.DMA((n,)), pltpu.SemaphoreType.DMA((n,))],
        name=name,
    )(*grads)


def _join_halves(finals):
    n = len(finals)

    def body(*refs):
        outs = refs[n:2 * n]
        send_sems, recv_sems = refs[2 * n:]
        x, y, c, _ = _place()
        sends = [_remote(outs[i].at[c], outs[i].at[c], send_sems.at[i], recv_sems.at[i], (x, y, 1 - c))
                 for i in range(n)]
        for cp in sends:
            cp.start()
        for i in range(n):
            sends[i].wait_send()
            other = outs[i].at[1 - c]
            _remote(other, other, send_sems.at[i], recv_sems.at[i], (x, y, 1 - c)).wait_recv()

    return pl.pallas_call(
        body,
        in_specs=[ANY] * n,
        out_specs=[ANY] * n,
        out_shape=[jax.ShapeDtypeStruct(f.shape, f.dtype) for f in finals],
        input_output_aliases={i: i for i in range(n)},
        scratch_shapes=[pltpu.SemaphoreType.DMA((n,)), pltpu.SemaphoreType.DMA((n,))],
        name="join_halves",
    )(*finals)


def _gather_small(vec):
    k = N_DEV - 1

    def body(v_ref, o_ref, send_sems, recv_sems, local_sem):
        x, y, c, _ = _place()
        me = 4 * x + 2 * y + c
        local = pltpu.make_async_copy(v_ref, o_ref.at[me], local_sem)
        local.start()
        peers = [(x ^ (r >> 2 & 1), y ^ (r >> 1 & 1), c ^ (r & 1)) for r in range(1, N_DEV)]
        sends = [_remote(v_ref, o_ref.at[me], send_sems.at[j], recv_sems.at[j], p) for j, p in enumerate(peers)]
        for cp in sends:
            cp.start()
        for j, (px, py, pc) in enumerate(peers):
            sends[j].wait_send()
            blk = o_ref.at[4 * px + 2 * py + pc]
            _remote(blk, blk, send_sems.at[j], recv_sems.at[j], (px, py, pc)).wait_recv()
        local.wait()

    return pl.pallas_call(
        body,
        in_specs=[ANY],
        out_specs=ANY,
        out_shape=jax.ShapeDtypeStruct((N_DEV,) + vec.shape, vec.dtype),
        scratch_shapes=[pltpu.SemaphoreType.DMA((k,)), pltpu.SemaphoreType.DMA((k,)), pltpu.SemaphoreType.DMA(())],
        name="gather_small",
    )(vec)


ROW_SHARDED = ("w_in_main", "w_in_f", "w_mem_kv", "w_out", "w_down")
COL_SHARDED = ("w_br_fox", "w_br_sb", "w_br_mem", "w_up")
BIG = ROW_SHARDED + COL_SHARDED


def _whole(name, gathered, shard, chip):
    a = lax.dynamic_update_slice(gathered, shard[None], (chip, 0, 0))
    if name in ROW_SHARDED:
        return a.reshape(N_CHIPS * a.shape[1], a.shape[2])
    return a.transpose(1, 0, 2).reshape(a.shape[1], N_CHIPS * a.shape[2])


def _by_shard(name, grad):
    if name in ROW_SHARDED:
        a = grad.reshape(N_CHIPS, grad.shape[0] // N_CHIPS, grad.shape[1])
    else:
        a = grad.reshape(grad.shape[0], N_CHIPS, grad.shape[1] // N_CHIPS).transpose(1, 0, 2)
    return a.reshape(N_CHIPS, 2, a.shape[1] // 2, a.shape[2])


def _sibling_sums(names, g, core, tag):
    split = [_by_shard(name, g[name]) for name in names]
    theirs = _swap_halves(split, "swap_halves_" + tag)
    return [_add_sibling(a, r, core, "add_sibling_" + name) for name, a, r in zip(names, split, theirs)]


GATHER_FIRST = ("w_in_main", "w_in_f", "w_mem_kv")
GATHER_LATE = ("w_down", "w_out", "w_br_fox", "w_br_sb", "w_br_mem")
REDUCE_FFN = ("w_down", "w_up")
REDUCE_MIX = ("w_out", "w_br_fox", "w_br_sb", "w_br_mem")
REDUCE_IN = ("w_in_main", "w_in_f", "w_mem_kv")


def _local_step(x, mem, target, w, shard, chip, core, chip_core):
    d = x.shape[1]
    nf = shard["w_br_fox"].shape[0] // HEAD_DIM
    nsb = shard["w_br_sb"].shape[0] // HEAD_DIM
    nm = shard["w_br_mem"].shape[0] // HEAD_DIM
    w = dict(w)

    def take(names, gathered):
        for name, a in zip(names, gathered):
            w[name] = _whole(name, a, shard[name], chip)

    take(GATHER_FIRST, _exchange(_Gather([shard[name] for name in GATHER_FIRST]), "gather_first"))
    fq, fk, fv = 0, nf, 2 * nf
    sq, sk, sv = 3 * nf, 3 * nf + nsb, 3 * nf + 2 * nsb
    mq = 3 * nf + 3 * nsb
    gates = mq + nm

    h, rstd1 = _rms_fwd(x, w["g_mix"], "rms_mix_fwd")
    proj, moved = _mm(h, w["w_in_main"], "nn", BF16, "proj_main", carry=_Gather([shard["w_up"]]))
    take(("w_up",), moved)
    f_logit = _mm(h, w["w_in_f"], "nn", F32, "proj_forget")
    c_sum = _forget_fwd(f_logit, w["b_forget"])
    c_t = c_sum[:, :nf].T
    c_col, c_row = c_t[:, :, None], c_t[:, None, :]
    qn = _headnorm_fwd(proj, fq, nf, w["g_q_fox"], "fox_qnorm_fwd")
    kn = _headnorm_fwd(proj, fk, nf, w["g_k_fox"], "fox_knorm_fwd")
    (o_fox, o_fox32, lse), moved = _fox_fwd(qn, kn, proj, fv, c_col, c_row, nf,
                                            carry=_Gather([shard[name] for name in GATHER_LATE]))
    take(GATHER_LATE, moved)
    o_sb = _sb_fwd(proj, sq, sk, sv, nsb)
    memn, rstd_m = _rms_fwd(mem, w["g_mem"], "rms_mem_fwd")
    mkv = _mm(memn, w["w_mem_kv"], "nn", BF16, "mem_kv")
    kmn = _headnorm_fwd(mkv, 0, nm, w["g_k_mem"], "mem_knorm_fwd")
    qmn = _headnorm_fwd(proj, mq, nm, w["g_q_mem"], "mem_qnorm_fwd")
    o_mem = _mem_fwd(qmn, kmn, mkv, nm)
    p0 = _mm(o_fox, w["w_br_fox"], "nn", F32, "branch_fox")
    p1 = _mm(o_sb, w["w_br_sb"], "nn", F32, "branch_sb")
    p2 = _mm(o_mem, w["w_br_mem"], "nn", F32, "branch_mem")
    merged = _merge_fwd(p0, p1, p2, proj, gates, w["b_gate"])
    x1 = _mm(merged, w["w_out"], "nn", F32, "out_proj", residual=x)
    h2, rstd2 = _rms_fwd(x1, w["g_ffn"], "rms_ffn_fwd")
    up = _mm(h2, w["w_up"], "nn", BF16, "ffn_up")
    act = _conv_act_fwd(up, w["conv_w"], w["conv_b"])
    y = _mm(act, w["w_down"], "nn", F32, "ffn_down", residual=x1)
    dy, dyb, lparts = _loss_grad(y, target)
    loss = (0.5 / d) * jnp.sum(lparts[::8, 0])

    g = {}
    dact = _mm(dyb, w["w_down"], "nt", BF16, "ffn_down_dx")
    g["w_down"] = _mm(act, dyb, "tn", BF16, "ffn_down_dw")
    dug, duv, dwg, dwv, dbg, dbv = _conv_act_bwd(up, w["conv_w"], w["conv_b"], dact)
    dup = jnp.concatenate([dug, duv], axis=1)
    g["conv_w"] = jnp.concatenate([dwg, dwv], axis=1)
    g["conv_b"] = jnp.concatenate([dbg, dbv], axis=1)
    dh2 = _mm(dup, w["w_up"], "nt", BF16, "ffn_up_dx")
    g["w_up"] = _mm(h2, dup, "tn", BF16, "ffn_up_dw")
    sums_ffn = _sibling_sums(REDUCE_FFN, g, core, "ffn")
    dx1, dx1b, g["g_ffn"] = _rms_bwd(dh2, x1, rstd2, w["g_ffn"], dy, "rms_ffn_bwd")
    dmerged = _mm(dx1b, w["w_out"], "nt", BF16, "out_proj_dx")
    g["w_out"] = _mm(merged, dx1b, "tn", BF16, "out_proj_dw")
    dp0, dp1, dp2, dga, dgb, dgc, g["b_gate"] = _merge_bwd(dmerged, p0, p1, p2, proj, gates, w["b_gate"])
    do_fox = _mm(dp0, w["w_br_fox"], "nt", BF16, "branch_fox_dx")
    do_sb = _mm(dp1, w["w_br_sb"], "nt", BF16, "branch_sb_dx")
    do_mem = _mm(dp2, w["w_br_mem"], "nt", BF16, "branch_mem_dx")
    g["w_br_fox"] = _mm(o_fox, dp0, "tn", BF16, "branch_fox_dw")
    g["w_br_sb"] = _mm(o_sb, dp1, "tn", BF16, "branch_sb_dw")
    g["w_br_mem"] = _mm(o_mem, dp2, "tn", BF16, "branch_mem_dw")
    sums_mix = _sibling_sums(REDUCE_MIX, g, core, "mix")

    (dqn, dkn, dfv, drs, dcs), others_ffn = _fox_bwd(qn, kn, proj, fv, c_col, c_row, o_fox32, do_fox, lse, nf,
                                                     carry=_Scatter(sums_ffn))
    dfq, g["g_q_fox"] = _headnorm_bwd(dqn, proj, fq, nf, w["g_q_fox"], "fox_qnorm_bwd")
    dfk, g["g_k_fox"] = _headnorm_bwd(dkn, proj, fk, nf, w["g_k_fox"], "fox_knorm_bwd")
    dc = jnp.pad((drs[:, :, 0] - dcs[:, 0, :]).T, ((0, 0), (0, LANES - nf)))
    df, g["b_forget"] = _forget_bwd(dc, f_logit, w["b_forget"])
    (dsq, dsk, dsv), others_mix = _sb_bwd(proj, sq, sk, sv, do_sb, nsb, carry=_Scatter(sums_mix))
    dqmn, dkmn, dvm = _mem_bwd(qmn, kmn, mkv, do_mem, nm)
    dmq, g["g_q_mem"] = _headnorm_bwd(dqmn, proj, mq, nm, w["g_q_mem"], "mem_qnorm_bwd")
    dkm, g["g_k_mem"] = _headnorm_bwd(dkmn, mkv, 0, nm, w["g_k_mem"], "mem_knorm_bwd")
    dmkv = jnp.concatenate([dkm, dvm.astype(BF16)], axis=1)
    g["w_mem_kv"] = _mm(memn, dmkv, "tn", BF16, "mem_kv_dw")
    dmemn = _mm(dmkv, w["w_mem_kv"], "nt", BF16, "mem_kv_dx")
    _, _, g["g_mem"] = _rms_bwd(dmemn, mem, rstd_m, w["g_mem"], None, "rms_mem_bwd")

    dproj = jnp.concatenate([dfq, dfk, dfv.astype(BF16), dsq.astype(BF16), dsk.astype(BF16), dsv.astype(BF16),
                             dmq, dga, dgb, dgc], axis=1)
    dfb = df.astype(BF16)
    g["w_in_main"] = _mm(h, dproj, "tn", BF16, "proj_main_dw")
    g["w_in_f"] = _mm(h, dfb, "tn", BF16, "proj_forget_dw")
    sums_in = _sibling_sums(REDUCE_IN, g, core, "in")
    dh_main, others_in = _mm(dproj, w["w_in_main"], "nt", F32, "proj_main_dx", carry=_Scatter(sums_in))
    dh = _mm(dfb, w["w_in_f"], "nt", F32, "proj_forget_dx", residual=dh_main)
    grad_x, _, g["g_mix"] = _rms_bwd(dh, x, rstd1, w["g_mix"], dx1, "rms_mix_bwd")

    names = REDUCE_FFN + REDUCE_MIX + REDUCE_IN
    finals = [_add_chips(own, theirs, chip_core, "add_chips_" + name)
              for name, own, theirs in zip(names, sums_ffn + sums_mix + sums_in, others_ffn + others_mix + others_in)]
    summed = {name: a.reshape(2 * a.shape[1], a.shape[2]) for name, a in zip(names, _join_halves(finals))}
    return loss, grad_x, g, summed


SMALL = ("g_mix", "b_forget", "g_q_fox", "g_k_fox", "g_mem", "g_q_mem", "g_k_mem", "b_gate", "g_ffn", "conv_w",
         "conv_b")
SMALL_SHARDED = ("b_gate", "conv_w")
PACK_ROWS = 8


def _pack(arrs):
    flat = jnp.concatenate([a.reshape(-1) for a in arrs])
    unit = PACK_ROWS * LANES
    flat = jnp.pad(flat, (0, -flat.shape[0] % unit))
    return flat.reshape(-1, LANES)


def _unpack(packed, shapes):
    flat = packed.reshape(-1)
    out, at = [], 0
    for s in shapes:
        n = 1
        for dim in s:
            n *= dim
        out.append(flat[at:at + n].reshape(s))
        at += n
    return out


def kernel(x, mem, g_mix, w_in, b_forget, g_q_fox, g_k_fox, g_mem, w_mem_kv, g_q_mem, g_k_mem, w_br_fox, w_br_sb, w_br_mem, b_gate, w_out, g_ffn, w_up, conv_w, conv_b, w_down, loss_target, m_g_mix, m_w_in, m_b_forget, m_g_q_fox, m_g_k_fox, m_g_mem, m_w_mem_kv, m_g_q_mem, m_g_k_mem, m_w_br_fox, m_w_br_sb, m_w_br_mem, m_b_gate, m_w_out, m_g_ffn, m_w_up, m_conv_w, m_conv_b, m_w_down, v_g_mix, v_w_in, v_b_forget, v_g_q_fox, v_g_k_fox, v_g_mem, v_w_mem_kv, v_g_q_mem, v_g_k_mem, v_w_br_fox, v_w_br_sb, v_w_br_mem, v_b_gate, v_w_out, v_g_ffn, v_w_up, v_conv_w, v_conv_b, v_w_down):
    given = dict(g_mix=g_mix, w_in=w_in, b_forget=b_forget, g_q_fox=g_q_fox, g_k_fox=g_k_fox, g_mem=g_mem,
                 w_mem_kv=w_mem_kv, g_q_mem=g_q_mem, g_k_mem=g_k_mem, w_br_fox=w_br_fox, w_br_sb=w_br_sb,
                 w_br_mem=w_br_mem, b_gate=b_gate, w_out=w_out, g_ffn=g_ffn, w_up=w_up, conv_w=conv_w, conv_b=conv_b,
                 w_down=w_down)
    m_in = dict(g_mix=m_g_mix, w_in=m_w_in, b_forget=m_b_forget, g_q_fox=m_g_q_fox, g_k_fox=m_g_k_fox, g_mem=m_g_mem,
                w_mem_kv=m_w_mem_kv, g_q_mem=m_g_q_mem, g_k_mem=m_g_k_mem, w_br_fox=m_w_br_fox, w_br_sb=m_w_br_sb,
                w_br_mem=m_w_br_mem, b_gate=m_b_gate, w_out=m_w_out, g_ffn=m_g_ffn, w_up=m_w_up, conv_w=m_conv_w,
                conv_b=m_conv_b, w_down=m_w_down)
    v_in = dict(g_mix=v_g_mix, w_in=v_w_in, b_forget=v_b_forget, g_q_fox=v_g_q_fox, g_k_fox=v_g_k_fox, g_mem=v_g_mem,
                w_mem_kv=v_w_mem_kv, g_q_mem=v_g_q_mem, g_k_mem=v_g_k_mem, w_br_fox=v_w_br_fox, w_br_sb=v_w_br_sb,
                w_br_mem=v_w_br_mem, b_gate=v_b_gate, w_out=v_w_out, g_ffn=v_g_ffn, w_up=v_w_up, conv_w=v_conv_w,
                conv_b=v_conv_b, w_down=v_w_down)
    layered = {k: a.ndim == 3 for k, a in given.items()}
    drop = lambda a: a[0] if a.ndim == 3 else a
    given = {k: drop(a) for k, a in given.items()}
    m_in = {k: drop(a) for k, a in m_in.items()}
    v_in = {k: drop(a) for k, a in v_in.items()}

    xi, yi, ci = lax.axis_index("x"), lax.axis_index("y"), lax.axis_index("c")
    chip = (2 * xi + yi).astype(jnp.int32)
    core_arr = ci.astype(jnp.int32).reshape(1)
    chip_core = jnp.stack([chip, ci.astype(jnp.int32)])

    nf = given["b_forget"].shape[1]
    cut = 3 * given["w_br_fox"].shape[0]

    shard = {
        "w_in_main": jnp.concatenate([given["w_in"][:, :cut], given["w_in"][:, cut + nf:]], axis=1).astype(BF16),
        "w_in_f": jnp.pad(given["w_in"][:, cut:cut + nf], ((0, 0), (0, LANES - nf))).astype(BF16),
    }
    for name in BIG:
        if name not in shard:
            shard[name] = given[name].astype(BF16)
    w = {}
    small_shapes = [given[name].shape for name in SMALL_SHARDED]
    small_parts = _gather_small(_pack([given[name] for name in SMALL_SHARDED]))[0::2]
    per_chip = [_unpack(small_parts[j], small_shapes) for j in range(N_CHIPS)]
    for k, name in enumerate(SMALL_SHARDED):
        w[name] = jnp.concatenate([per_chip[j][k] for j in range(N_CHIPS)], axis=1)
    for name in SMALL:
        if name not in SMALL_SHARDED:
            w[name] = given[name]
    w["b_forget"] = jnp.pad(given["b_forget"], ((0, 0), (0, LANES - nf)))

    loss, grad_x, g, summed = _local_step(x[0], mem[0], loss_target[0], w, shard, chip, core_arr, chip_core)
    loss = lax.psum(loss, ("x", "y", "c"))
    grads = {name: summed[name] for name in BIG if name in given}
    grads["w_in"] = jnp.concatenate([summed["w_in_main"][:, :cut], summed["w_in_f"][:, :nf],
                                     summed["w_in_main"][:, cut:]], axis=1)

    g["b_forget"] = g["b_forget"][:, :nf]
    small_full_shapes = [g[name].shape for name in SMALL]
    small_sum = _unpack(_sum_devices(_gather_small(_pack([g[name] for name in SMALL]))), small_full_shapes)
    for name, a in zip(SMALL, small_sum):
        if name in SMALL_SHARDED:
            width = given[name].shape[1]
            a = lax.dynamic_slice_in_dim(a, chip * width, width, axis=1)
        grads[name] = a

    delta, new_m, new_v = {}, {}, {}
    for name in WEIGHTS:
        if name not in SMALL:
            delta[name], new_m[name], new_v[name] = _adamw(given[name], grads[name], m_in[name], v_in[name],
                                                           "adamw_" + name)
    shapes = [given[name].shape for name in SMALL]
    packed = [_pack([src[name] for name in SMALL]) for src in (given, grads, m_in, v_in)]
    for dst, res in zip((delta, new_m, new_v), _adamw(*packed, "adamw_small")):
        for name, a in zip(SMALL, _unpack(res, shapes)):
            dst[name] = a

    out = [loss, grad_x[None]]
    for src in (grads, delta, new_m, new_v):
        out.extend(src[name][None] if layered[name] else src[name] for name in WEIGHTS)
    return tuple(out)
```

```python
import functools

import jax
import jax.numpy as jnp
from jax import lax
from jax.experimental import pallas as pl
from jax.experimental.pallas import tpu as pltpu

F32 = jnp.float32
BF16 = jnp.bfloat16

HEAD_DIM = 128
EPS = 1e-6
NEG_BIG = -1e30

ADAM_LR = 0.001
ADAM_B1 = 0.9
ADAM_B2 = 0.999
ADAM_EPS = 1e-08
ADAM_WD = 0.01
ADAM_STEP = 10

LANES = 128
BF16_SUBLANES = 16
VMEM_LIMIT_BYTES = 56 * 1024 * 1024
MM_TILE = 1024
ATT_TILE = 256
ROW_TILE = 256
HEADNORM_ROWS = 512
COL_TILE = 512
ADAM_BLOCK_BYTES = 1 << 20

N_CHIPS = 4
N_DEV = 8
MESH = pl.DeviceIdType.MESH

IN_NAMES = ['x', 'mem', 'g_mix', 'w_in', 'b_forget', 'g_q_fox', 'g_k_fox', 'g_mem', 'w_mem_kv', 'g_q_mem', 'g_k_mem',
            'w_br_fox', 'w_br_sb', 'w_br_mem', 'b_gate', 'w_out', 'g_ffn', 'w_up', 'conv_w', 'conv_b', 'w_down']
WEIGHTS = IN_NAMES[2:]


def _tile(n, target):
    if n <= target:
        return n
    for t in range(target - target % LANES, LANES - 1, -LANES):
        if n % t == 0:
            return t
    return n


def _params(*sem):
    return pltpu.CompilerParams(dimension_semantics=sem, vmem_limit_bytes=VMEM_LIMIT_BYTES)


def _log_sigmoid(z):
    return jnp.minimum(z, 0.0) - jnp.log(1.0 + jnp.exp(-jnp.abs(z)))


def _split2(v):
    hi = v.astype(BF16)
    lo = (v - hi.astype(F32)).astype(BF16)
    return hi, lo


def _split3(v):
    hi = v.astype(BF16)
    r = v - hi.astype(F32)
    mid = r.astype(BF16)
    lo = (r - mid.astype(F32)).astype(BF16)
    return hi, mid, lo


def _dot(a, b):
    return lax.dot_general(a, b, (((1,), (0,)), ((), ())), preferred_element_type=F32)


def _dot_nt(a, b):
    return lax.dot_general(a, b, (((1,), (1,)), ((), ())), preferred_element_type=F32)


def _dot_tn(a, b):
    return lax.dot_general(a, b, (((0,), (0,)), ((), ())), preferred_element_type=F32)


ANY = pl.BlockSpec(memory_space=pl.ANY)


def _place():
    x, y, c = lax.axis_index("x"), lax.axis_index("y"), lax.axis_index("c")
    others = [(1 - x, y), (x, 1 - y), (1 - x, 1 - y)]
    return x, y, c, others


def _remote(src, dst, send_sem, recv_sem, to):
    return pltpu.make_async_remote_copy(src_ref=src, dst_ref=dst, send_sem=send_sem, recv_sem=recv_sem,
                                        device_id=to, device_id_type=MESH)


class _Gather:
    def __init__(self, shards):
        self.inputs = list(shards)
        n = len(shards)
        self.out_shapes = [jax.ShapeDtypeStruct((N_CHIPS,) + s.shape, s.dtype) for s in shards]
        self.scratch = [pltpu.SemaphoreType.DMA((6 * n,)), pltpu.SemaphoreType.DMA((6 * n,))]

    def _over_ici(self, ins, outs, sems):
        send_sems, recv_sems = sems
        x, y, c, others = _place()
        me = 2 * x + y
        copies = []
        for i in range(len(ins)):
            h = ins[i].shape[0] // 2
            mine = pl.ds(pl.multiple_of(c * h, BF16_SUBLANES), h)
            for j, (ox, oy) in enumerate(others):
                copies.append(_remote(ins[i].at[mine], outs[i].at[me, mine], send_sems.at[6 * i + j],
                                      recv_sems.at[6 * i + j], (ox, oy, c)))
        return copies

    def start(self, ins, outs, sems):
        for cp in self._over_ici(ins, outs, sems):
            cp.start()

    def finish(self, ins, outs, sems):
        send_sems, recv_sems = sems
        x, y, c, others = _place()
        sibling = (x, y, 1 - c)
        passed = []
        for i in range(len(ins)):
            h = ins[i].shape[0] // 2
            mine = pl.ds(pl.multiple_of(c * h, BF16_SUBLANES), h)
            for j, (ox, oy) in enumerate(others):
                blk = outs[i].at[2 * ox + oy, mine]
                _remote(blk, blk, send_sems.at[6 * i + j], recv_sems.at[6 * i + j], (ox, oy, c)).wait_recv()
                cp = _remote(blk, blk, send_sems.at[6 * i + 3 + j], recv_sems.at[6 * i + 3 + j], sibling)
                cp.start()
                passed.append(cp)
        for i in range(len(ins)):
            h = ins[i].shape[0] // 2
            theirs = pl.ds(pl.multiple_of((1 - c) * h, BF16_SUBLANES), h)
            for j, (ox, oy) in enumerate(others):
                blk = outs[i].at[2 * ox + oy, theirs]
                _remote(blk, blk, send_sems.at[6 * i + 3 + j], recv_sems.at[6 * i + 3 + j], sibling).wait_recv()
        for cp in self._over_ici(ins, outs, sems) + passed:
            cp.wait_send()


class _Scatter:
    def __init__(self, sums):
        self.inputs = list(sums)
        k = N_CHIPS - 1
        self.out_shapes = [jax.ShapeDtypeStruct((k,) + g.shape[1:], g.dtype) for g in sums]
        self.scratch = [pltpu.SemaphoreType.DMA((k * len(sums),)), pltpu.SemaphoreType.DMA((k * len(sums),))]

    def _copies(self, ins, outs, sems):
        send_sems, recv_sems = sems
        _, _, c, others = _place()
        k = N_CHIPS - 1
        return [_remote(ins[i].at[2 * ox + oy], outs[i].at[j], send_sems.at[k * i + j], recv_sems.at[k * i + j],
                        (ox, oy, c))
                for i in range(len(ins)) for j, (ox, oy) in enumerate(others)]

    def start(self, ins, outs, sems):
        for cp in self._copies(ins, outs, sems):
            cp.start()

    def finish(self, ins, outs, sems):
        for cp in self._copies(ins, outs, sems):
            cp.wait()


def _exchange(carry, name):
    n = len(carry.inputs)

    def body(*refs):
        ins, outs, sems = refs[:n], refs[n:2 * n], refs[2 * n:]
        carry.start(ins, outs, sems)
        carry.finish(ins, outs, sems)

    return pl.pallas_call(
        body,
        in_specs=[ANY] * n,
        out_specs=[ANY] * n,
        out_shape=carry.out_shapes,
        scratch_shapes=carry.scratch,
        name=name,
    )(*carry.inputs)


def _call(body, *, grid, in_specs, out_specs, out_shape, scratch_shapes, semantics, name, args, carry=None):
    n_in, n_out, n_scr = len(in_specs), len(out_specs), len(scratch_shapes)
    if carry is None:
        res = pl.pallas_call(body, grid=grid, in_specs=in_specs, out_specs=out_specs, out_shape=out_shape,
                             scratch_shapes=scratch_shapes, compiler_params=_params(*semantics), name=name)(*args)
        return list(res), []
    nci, nco = len(carry.inputs), len(carry.out_shapes)
    a, b = n_in, n_in + nci
    c, d = b + n_out, b + n_out + nco
    e = d + n_scr

    def carried(*refs):
        ids = [pl.program_id(k) for k in range(len(grid))]
        first = functools.reduce(jnp.logical_and, [i == 0 for i in ids])
        last = functools.reduce(jnp.logical_and, [i == n - 1 for i, n in zip(ids, grid)])

        @pl.when(first)
        def _():
            carry.start(refs[a:b], refs[c:d], refs[e:])

        body(*refs[:a], *refs[b:c], *refs[d:e])

        @pl.when(last)
        def _():
            carry.finish(refs[a:b], refs[c:d], refs[e:])

    res = pl.pallas_call(
        carried,
        grid=grid,
        in_specs=list(in_specs) + [ANY] * nci,
        out_specs=list(out_specs) + [ANY] * nco,
        out_shape=list(out_shape) + carry.out_shapes,
        scratch_shapes=list(scratch_shapes) + carry.scratch,
        compiler_params=_params(*(["arbitrary"] * len(grid))),
        name=name,
    )(*args, *carry.inputs)
    return list(res[:n_out]), list(res[n_out:])


def _mm(a, b, mode, out_dtype, name, residual=None, carry=None):
    if mode == "nn":
        (m, k), (k2, n) = a.shape, b.shape
    elif mode == "nt":
        (m, k), (n, k2) = a.shape, b.shape
    else:
        (k, m), (k2, n) = a.shape, b.shape
    assert k == k2, (a.shape, b.shape, mode)
    tm, tn, tk = _tile(m, MM_TILE), _tile(n, MM_TILE), _tile(k, MM_TILE)
    nk = k // tk
    dot = {"nn": _dot, "nt": _dot_nt, "tn": _dot_tn}[mode]
    has_res = residual is not None

    def body(*refs):
        if has_res:
            a_ref, b_ref, r_ref, o_ref = refs[:4]
        else:
            a_ref, b_ref, o_ref = refs[:3]
            r_ref = None

        def finish(acc):
            if has_res:
                acc = acc + r_ref[...]
            o_ref[...] = acc.astype(o_ref.dtype)

        part = dot(a_ref[...], b_ref[...])
        if nk == 1:
            finish(part)
        else:
            acc_ref = refs[-1]
            kk = pl.program_id(2)

            @pl.when(kk == 0)
            def _():
                acc_ref[...] = part

            @pl.when(kk > 0)
            def _():
                acc_ref[...] += part

            @pl.when(kk == nk - 1)
            def _():
                finish(acc_ref[...])

    if mode == "tn":
        a_spec = pl.BlockSpec((tk, tm), lambda j, i, kk: (kk, i))
    else:
        a_spec = pl.BlockSpec((tm, tk), lambda j, i, kk: (i, kk))
    if mode == "nt":
        b_spec = pl.BlockSpec((tn, tk), lambda j, i, kk: (j, kk))
    else:
        b_spec = pl.BlockSpec((tk, tn), lambda j, i, kk: (kk, j))
    o_spec = pl.BlockSpec((tm, tn), lambda j, i, kk: (i, j))
    in_specs = [a_spec, b_spec] + ([o_spec] if has_res else [])
    args = (a, b) + ((residual,) if has_res else ())
    (out,), moved = _call(
        body,
        grid=(n // tn, m // tm, nk),
        in_specs=in_specs,
        out_specs=[o_spec],
        out_shape=[jax.ShapeDtypeStruct((m, n), out_dtype)],
        scratch_shapes=[pltpu.VMEM((tm, tn), F32)] if nk > 1 else [],
        semantics=("parallel", "parallel", "arbitrary"),
        name=name,
        args=args,
        carry=carry,
    )
    return out if carry is None else (out, moved)


def _rms_fwd(x, g, name):
    s, d = x.shape
    tm = _tile(s, ROW_TILE)

    def body(x_ref, g_ref, h_ref, r_ref):
        xf = x_ref[...]
        r = lax.rsqrt(jnp.mean(xf * xf, axis=-1, keepdims=True) + EPS)
        h_ref[...] = ((xf * r) * g_ref[...]).astype(BF16)
        r_ref[...] = r

    return pl.pallas_call(
        body,
        grid=(s // tm,),
        in_specs=[pl.BlockSpec((tm, d), lambda i: (i, 0)), pl.BlockSpec((1, d), lambda i: (0, 0))],
        out_specs=[pl.BlockSpec((tm, d), lambda i: (i, 0)), pl.BlockSpec((tm, 1), lambda i: (i, 0))],
        out_shape=[jax.ShapeDtypeStruct((s, d), BF16), jax.ShapeDtypeStruct((s, 1), F32)],
        compiler_params=_params("parallel"),
        name=name,
    )(x, g)


def _rms_bwd(dh, x, rstd, g, res, name):
    s, d = x.shape
    tm = _tile(s, ROW_TILE)
    has_res = res is not None

    def body(*refs):
        if has_res:
            dh_ref, x_ref, r_ref, g_ref, res_ref, dx_ref, dxb_ref, dg_ref = refs
        else:
            dh_ref, x_ref, r_ref, g_ref, dx_ref, dxb_ref, dg_ref = refs
        dhf = dh_ref[...].astype(F32)
        xhat = x_ref[...] * r_ref[...]
        dy = dhf * g_ref[...]
        dx = r_ref[...] * (dy - xhat * jnp.mean(dy * xhat, axis=-1, keepdims=True))
        if has_res:
            dx = dx + res_ref[...]
        dx_ref[...] = dx
        dxb_ref[...] = dx.astype(BF16)
        part = jnp.sum(dhf * xhat, axis=0, keepdims=True)

        @pl.when(pl.program_id(0) == 0)
        def _():
            dg_ref[...] = part

        @pl.when(pl.program_id(0) > 0)
        def _():
            dg_ref[...] += part

    row = pl.BlockSpec((tm, d), lambda i: (i, 0))
    vec = pl.BlockSpec((1, d), lambda i: (0, 0))
    in_specs = [row, row, pl.BlockSpec((tm, 1), lambda i: (i, 0)), vec] + ([row] if has_res else [])
    args = (dh, x, rstd, g) + ((res,) if has_res else ())
    return pl.pallas_call(
        body,
        grid=(s // tm,),
        in_specs=in_specs,
        out_specs=[row, row, vec],
        out_shape=[jax.ShapeDtypeStruct((s, d), F32), jax.ShapeDtypeStruct((s, d), BF16),
                   jax.ShapeDtypeStruct((1, d), F32)],
        compiler_params=_params("arbitrary"),
        name=name,
    )(*args)


def _headnorm_fwd(src, col0, nheads, g, name):
    s = src.shape[0]
    tm = _tile(s, HEADNORM_ROWS)
    w = nheads * HEAD_DIM
    assert col0 % nheads == 0

    def body(x_ref, g_ref, o_ref):
        for hh in range(nheads):
            xf = _head(x_ref, hh).astype(F32)
            r = lax.rsqrt(jnp.mean(xf * xf, axis=-1, keepdims=True) + EPS)
            o_ref[:, hh * HEAD_DIM:(hh + 1) * HEAD_DIM] = ((xf * r) * g_ref[...]).astype(BF16)

    return pl.pallas_call(
        body,
        grid=(s // tm,),
        in_specs=[pl.BlockSpec((tm, w), lambda i: (i, col0 // nheads)),
                  pl.BlockSpec((1, HEAD_DIM), lambda i: (0, 0))],
        out_specs=pl.BlockSpec((tm, w), lambda i: (i, 0)),
        out_shape=jax.ShapeDtypeStruct((s, w), BF16),
        compiler_params=_params("parallel"),
        name=name,
    )(src, g)


def _headnorm_bwd(dxn, src, col0, nheads, g, name):
    s = src.shape[0]
    tm = _tile(s, HEADNORM_ROWS)
    w = nheads * HEAD_DIM
    assert col0 % nheads == 0

    def body(d_ref, x_ref, g_ref, dx_ref, dg_ref):
        part = jnp.zeros((1, HEAD_DIM), F32)
        for hh in range(nheads):
            xf = _head(x_ref, hh).astype(F32)
            r = lax.rsqrt(jnp.mean(xf * xf, axis=-1, keepdims=True) + EPS)
            xhat = xf * r
            dn = _head(d_ref, hh).astype(F32)
            dy = dn * g_ref[...]
            dx = r * (dy - xhat * jnp.mean(dy * xhat, axis=-1, keepdims=True))
            dx_ref[:, hh * HEAD_DIM:(hh + 1) * HEAD_DIM] = dx.astype(BF16)
            part = part + jnp.sum(dn * xhat, axis=0, keepdims=True)

        @pl.when(pl.program_id(0) == 0)
        def _():
            dg_ref[...] = part

        @pl.when(pl.program_id(0) > 0)
        def _():
            dg_ref[...] += part

    return pl.pallas_call(
        body,
        grid=(s // tm,),
        in_specs=[pl.BlockSpec((tm, w), lambda i: (i, 0)),
                  pl.BlockSpec((tm, w), lambda i: (i, col0 // nheads)),
                  pl.BlockSpec((1, HEAD_DIM), lambda i: (0, 0))],
        out_specs=[pl.BlockSpec((tm, w), lambda i: (i, 0)),
                   pl.BlockSpec((1, HEAD_DIM), lambda i: (0, 0))],
        out_shape=[jax.ShapeDtypeStruct((s, w), BF16), jax.ShapeDtypeStruct((1, HEAD_DIM), F32)],
        compiler_params=_params("arbitrary"),
        name=name,
    )(dxn, src, g)


def _tri(t, lower_inclusive):
    r = lax.broadcasted_iota(jnp.int32, (t, t), 0)
    c = lax.broadcasted_iota(jnp.int32, (t, t), 1)
    keep = (c <= r) if lower_inclusive else (c >= r)
    return jnp.where(keep, 1.0, 0.0).astype(BF16)


def _forget_fwd(f_logit, b_pad):
    s = f_logit.shape[0]
    t = _tile(s, ATT_TILE)

    def body(f_ref, b_ref, c_ref, carry):
        @pl.when(pl.program_id(0) == 0)
        def _():
            carry[...] = jnp.zeros_like(carry)

        lf = _log_sigmoid(f_ref[...] + b_ref[...])
        tri = _tri(t, True)
        acc = carry[...]
        for part in _split3(lf):
            acc = acc + _dot(tri, part)
        c_ref[...] = acc
        carry[...] += jnp.sum(lf, axis=0, keepdims=True)

    return pl.pallas_call(
        body,
        grid=(s // t,),
        in_specs=[pl.BlockSpec((t, LANES), lambda i: (i, 0)), pl.BlockSpec((1, LANES), lambda i: (0, 0))],
        out_specs=pl.BlockSpec((t, LANES), lambda i: (i, 0)),
        out_shape=jax.ShapeDtypeStruct((s, LANES), F32),
        scratch_shapes=[pltpu.VMEM((1, LANES), F32)],
        compiler_params=_params("arbitrary"),
        name="forget_fwd",
    )(f_logit, b_pad)


def _forget_bwd(dc, f_logit, b_pad):
    s = f_logit.shape[0]
    t = _tile(s, ATT_TILE)
    nb = s // t

    def body(dc_ref, f_ref, b_ref, df_ref, db_ref, carry):
        @pl.when(pl.program_id(0) == 0)
        def _():
            carry[...] = jnp.zeros_like(carry)
            db_ref[...] = jnp.zeros_like(db_ref)

        d = dc_ref[...]
        tri = _tri(t, False)
        acc = carry[...]
        for part in _split3(d):
            acc = acc + _dot(tri, part)
        z = f_ref[...] + b_ref[...]
        df = acc * jnp.exp(_log_sigmoid(-z))
        df_ref[...] = df
        db_ref[...] += jnp.sum(df, axis=0, keepdims=True)
        carry[...] += jnp.sum(d, axis=0, keepdims=True)

    rev = pl.BlockSpec((t, LANES), lambda i: (nb - 1 - i, 0))
    vec = pl.BlockSpec((1, LANES), lambda i: (0, 0))
    return pl.pallas_call(
        body,
        grid=(nb,),
        in_specs=[rev, rev, vec],
        out_specs=[rev, vec],
        out_shape=[jax.ShapeDtypeStruct((s, LANES), F32), jax.ShapeDtypeStruct((1, LANES), F32)],
        scratch_shapes=[pltpu.VMEM((1, LANES), F32)],
        compiler_params=_params("arbitrary"),
        name="forget_bwd",
    )(dc, f_logit, b_pad)


SB_FWD_GROUP = 3
FOX_GROUP = 3
SB_BWD_GROUP = 3


def _head(ref, hh, rows=slice(None)):
    return ref[rows, hh * HEAD_DIM:(hh + 1) * HEAD_DIM]


def _tri_mask(t, strict):
    r = lax.broadcasted_iota(jnp.int32, (t, t), 0)
    c = lax.broadcasted_iota(jnp.int32, (t, t), 1)
    return (c < r) if strict else (c <= r)


def _fox_fwd(qn, kn, proj, colv, c_col, c_row, nheads, carry=None):
    s = qn.shape[0]
    t = _tile(s, ATT_TILE)
    scale = HEAD_DIM ** -0.5
    hg = FOX_GROUP
    gw = hg * HEAD_DIM
    assert nheads % hg == 0 and colv % hg == 0

    def body(q_ref, k_ref, v_ref, cc_ref, cr_ref, o_ref, of_ref, lse_ref):
        qi = pl.program_id(1)
        causal = _tri_mask(t, False)

        def tile(kj, carry, diagonal):
            off = pl.multiple_of(kj * t, t)
            heads = range(hg)
            rows = pl.ds(off, t)
            qk = [_dot_nt(_head(q_ref, hh), _head(k_ref, hh, rows)) for hh in heads]
            sc = [qk[hh] * scale + (cc_ref[hh] - cr_ref[hh, :, rows]) for hh in heads]
            if diagonal:
                sc = [jnp.where(causal, sc[hh], NEG_BIG) for hh in heads]
            m_new = [jnp.maximum(carry[hh][0], jnp.max(sc[hh], axis=-1, keepdims=True)) for hh in heads]
            p = [jnp.exp(sc[hh] - m_new[hh]) for hh in heads]
            pv = [_dot(p[hh].astype(BF16), _head(v_ref, hh, rows)) for hh in heads]
            out = []
            for hh in heads:
                m, l, acc = carry[hh]
                alpha = jnp.exp(m - m_new[hh])
                out.append((m_new[hh], alpha * l + jnp.sum(p[hh], axis=-1, keepdims=True), alpha * acc + pv[hh]))
            return tuple(out)

        init = tuple((jnp.full((t, 1), NEG_BIG, F32), jnp.zeros((t, 1), F32), jnp.zeros((t, HEAD_DIM), F32))
                     for _ in range(hg))
        carry = lax.fori_loop(0, qi, lambda kj, c: tile(kj, c, False), init)
        carry = tile(qi, carry, True)
        for hh in range(hg):
            m, l, acc = carry[hh]
            o = acc / l
            of_ref[:, hh * HEAD_DIM:(hh + 1) * HEAD_DIM] = o
            o_ref[:, hh * HEAD_DIM:(hh + 1) * HEAD_DIM] = o.astype(BF16)
            lse_ref[hh] = m + jnp.log(l)

    tile_spec = pl.BlockSpec((t, gw), lambda h, i: (i, h))
    w = nheads * HEAD_DIM
    return _call(
        body,
        grid=(nheads // hg, s // t),
        in_specs=[tile_spec,
                  pl.BlockSpec((s, gw), lambda h, i: (0, h)),
                  pl.BlockSpec((s, gw), lambda h, i: (0, colv // hg + h)),
                  pl.BlockSpec((hg, t, 1), lambda h, i: (h, i, 0)),
                  pl.BlockSpec((hg, 1, s), lambda h, i: (h, 0, 0))],
        out_specs=[tile_spec, tile_spec, pl.BlockSpec((hg, t, 1), lambda h, i: (h, i, 0))],
        out_shape=[jax.ShapeDtypeStruct((s, w), BF16), jax.ShapeDtypeStruct((s, w), F32),
                   jax.ShapeDtypeStruct((nheads, s, 1), F32)],
        scratch_shapes=[],
        semantics=("parallel", "parallel"),
        name="fox_fwd",
        args=(qn, kn, proj, c_col, c_row),
        carry=carry,
    )


def _fox_bwd(qn, kn, proj, colv, c_col, c_row, o, do, lse, nheads, carry=None):
    s = qn.shape[0]
    t = _tile(s, ATT_TILE)
    scale = HEAD_DIM ** -0.5
    hg = FOX_GROUP
    gw = hg * HEAD_DIM
    assert nheads % hg == 0 and colv % hg == 0

    def body(q_ref, k_ref, v_ref, cc_ref, cr_ref, o_ref, do_ref, lse_ref,
             dq_ref, dk_ref, dv_ref, drs_ref, dcs_ref):
        qi = pl.program_id(1)

        @pl.when(qi == 0)
        def _():
            dk_ref[...] = jnp.zeros_like(dk_ref)
            dv_ref[...] = jnp.zeros_like(dv_ref)
            dcs_ref[...] = jnp.zeros_like(dcs_ref)

        causal = _tri_mask(t, False)
        delta = [jnp.sum(_head(o_ref, hh) * _head(do_ref, hh).astype(F32), axis=-1, keepdims=True)
                 for hh in range(hg)]

        def tile(kj, carry, diagonal):
            off = pl.multiple_of(kj * t, t)
            heads = range(hg)
            rows = pl.ds(off, t)
            qk = [_dot_nt(_head(q_ref, hh), _head(k_ref, hh, rows)) for hh in heads]
            dp = [_dot_nt(_head(do_ref, hh), _head(v_ref, hh, rows)) for hh in heads]
            p = [jnp.exp(qk[hh] * scale + (cc_ref[hh] - cr_ref[hh, :, rows]) - lse_ref[hh]) for hh in heads]
            if diagonal:
                p = [jnp.where(causal, p[hh], 0.0) for hh in heads]
            ds = [p[hh] * (dp[hh] - delta[hh]) for hh in heads]
            dsb = [ds[hh].astype(BF16) for hh in heads]
            dv = [_dot_tn(p[hh].astype(BF16), _head(do_ref, hh)) for hh in heads]
            dk = [_dot_tn(dsb[hh], _head(q_ref, hh)) * scale for hh in heads]
            dq = [_dot(dsb[hh], _head(k_ref, hh, rows)) * scale for hh in heads]
            for hh in heads:
                cols = slice(hh * HEAD_DIM, (hh + 1) * HEAD_DIM)
                dv_ref[rows, cols] += dv[hh]
                dk_ref[rows, cols] += dk[hh]
                dcs_ref[hh, :, rows] += jnp.sum(ds[hh], axis=0, keepdims=True)
            return tuple((carry[hh][0] + dq[hh], carry[hh][1] + jnp.sum(ds[hh], axis=-1, keepdims=True))
                         for hh in heads)

        init = tuple((jnp.zeros((t, HEAD_DIM), F32), jnp.zeros((t, 1), F32)) for _ in range(hg))
        carry = lax.fori_loop(0, qi, lambda kj, c: tile(kj, c, False), init)
        carry = tile(qi, carry, True)
        for hh in range(hg):
            dq_ref[:, hh * HEAD_DIM:(hh + 1) * HEAD_DIM] = carry[hh][0]
            drs_ref[hh] = carry[hh][1]

    tile_spec = pl.BlockSpec((t, gw), lambda h, i: (i, h))
    full = pl.BlockSpec((s, gw), lambda h, i: (0, h))
    colspec = pl.BlockSpec((hg, t, 1), lambda h, i: (h, i, 0))
    rowspec = pl.BlockSpec((hg, 1, s), lambda h, i: (h, 0, 0))
    w = nheads * HEAD_DIM
    return _call(
        body,
        grid=(nheads // hg, s // t),
        in_specs=[tile_spec, full, pl.BlockSpec((s, gw), lambda h, i: (0, colv // hg + h)), colspec, rowspec,
                  tile_spec, tile_spec, colspec],
        out_specs=[tile_spec, full, full, colspec, rowspec],
        out_shape=[jax.ShapeDtypeStruct((s, w), F32), jax.ShapeDtypeStruct((s, w), F32),
                   jax.ShapeDtypeStruct((s, w), F32), jax.ShapeDtypeStruct((nheads, s, 1), F32),
                   jax.ShapeDtypeStruct((nheads, 1, s), F32)],
        scratch_shapes=[],
        semantics=("arbitrary", "arbitrary"),
        name="fox_bwd",
        args=(qn, kn, proj, c_col, c_row, o, do, lse),
        carry=carry,
    )


def _sb_tile(q, k, scale, later, valid):
    z = _dot_nt(q, k) * scale
    lb = _log_sigmoid(z)
    lm = lb - z
    if valid is not None:
        lm = jnp.where(valid, lm, 0.0)
    suffix = _dot(jnp.concatenate(_split2(lm), axis=1), later)
    return lb, lm, suffix


def _later(t):
    r = lax.broadcasted_iota(jnp.int32, (2 * t, t), 0) % t
    c = lax.broadcasted_iota(jnp.int32, (2 * t, t), 1)
    return jnp.where(r > c, 1.0, 0.0).astype(BF16)


def _sb_fwd(proj, colq, colk, colv, nheads):
    s = proj.shape[0]
    t = _tile(s, ATT_TILE)
    scale = HEAD_DIM ** -0.5
    hg = SB_FWD_GROUP
    gw = hg * HEAD_DIM
    assert nheads % hg == 0 and colq % hg == 0 and colk % hg == 0 and colv % hg == 0

    def body(q_ref, k_ref, v_ref, o_ref):
        qi = pl.program_id(1)
        later = _later(t)
        before = _tri_mask(t, True)

        def tile(kj, carry, diagonal):
            off = pl.multiple_of(kj * t, t)
            heads = range(hg)
            z = [_dot_nt(_head(q_ref, hh), _head(k_ref, hh, pl.ds(off, t))) * scale for hh in heads]
            lb = [_log_sigmoid(z[hh]) for hh in heads]
            lm = [lb[hh] - z[hh] for hh in heads]
            if diagonal:
                lm = [jnp.where(before, lm[hh], 0.0) for hh in heads]
            parts = [jnp.concatenate(_split2(lm[hh]), axis=1) for hh in heads]
            suffix = [_dot(parts[hh], later) for hh in heads]
            a = [jnp.exp(lb[hh] + suffix[hh] + carry[hh][0]) for hh in heads]
            if diagonal:
                a = [jnp.where(before, a[hh], 0.0) for hh in heads]
            av = [_dot(a[hh].astype(BF16), _head(v_ref, hh, pl.ds(off, t))) for hh in heads]
            return tuple((carry[hh][0] + jnp.sum(lm[hh], axis=-1, keepdims=True), carry[hh][1] + av[hh])
                         for hh in heads)

        init = tuple((jnp.zeros((t, 1), F32), jnp.zeros((t, HEAD_DIM), F32)) for _ in range(hg))
        carry = tile(qi, init, True)
        carry = lax.fori_loop(1, qi + 1, lambda i, c: tile(qi - i, c, False), carry)
        for hh in range(hg):
            o_ref[:, hh * HEAD_DIM:(hh + 1) * HEAD_DIM] = carry[hh][1].astype(BF16)

    return pl.pallas_call(
        body,
        grid=(nheads // hg, s // t),
        in_specs=[pl.BlockSpec((t, gw), lambda h, i: (i, colq // hg + h)),
                  pl.BlockSpec((s, gw), lambda h, i: (0, colk // hg + h)),
                  pl.BlockSpec((s, gw), lambda h, i: (0, colv // hg + h))],
        out_specs=pl.BlockSpec((t, gw), lambda h, i: (i, h)),
        out_shape=jax.ShapeDtypeStruct((s, nheads * HEAD_DIM), BF16),
        compiler_params=_params("parallel", "parallel"),
        name="sb_fwd",
    )(proj, proj, proj)


def _sb_bwd(proj, colq, colk, colv, do, nheads, carry=None):
    s = proj.shape[0]
    t = _tile(s, ATT_TILE)
    scale = HEAD_DIM ** -0.5
    hg = SB_BWD_GROUP
    gw = hg * HEAD_DIM
    assert nheads % hg == 0 and colq % hg == 0 and colk % hg == 0 and colv % hg == 0

    def body(q_ref, k_ref, v_ref, do_ref, dq_ref, dk_ref, dv_ref, g_s, beta_s):
        qi = pl.program_id(1)

        @pl.when(qi == 0)
        def _():
            dk_ref[...] = jnp.zeros_like(dk_ref)
            dv_ref[...] = jnp.zeros_like(dv_ref)

        later = _later(t)
        before = _tri_mask(t, True)

        def back(kj, carry, diagonal):
            off = pl.multiple_of(kj * t, t)
            heads = range(hg)
            rows = pl.ds(off, t)
            z = [_dot_nt(_head(q_ref, hh), _head(k_ref, hh, rows)) * scale for hh in heads]
            da = [_dot_nt(_head(do_ref, hh), _head(v_ref, hh, rows)) for hh in heads]
            lb = [_log_sigmoid(z[hh]) for hh in heads]
            lm = [lb[hh] - z[hh] for hh in heads]
            if diagonal:
                lm = [jnp.where(before, lm[hh], 0.0) for hh in heads]
            parts = [jnp.concatenate(_split2(lm[hh]), axis=1) for hh in heads]
            suffix = [_dot(parts[hh], later) for hh in heads]
            a = [jnp.exp(lb[hh] + suffix[hh] + carry[hh]) for hh in heads]
            if diagonal:
                a = [jnp.where(before, a[hh], 0.0) for hh in heads]
            dv = [_dot_tn(a[hh].astype(BF16), _head(do_ref, hh)) for hh in heads]
            for hh in heads:
                g_s[hh, :, rows] = a[hh] * da[hh]
                beta_s[hh, :, rows] = jnp.exp(lb[hh]).astype(BF16)
            for hh in heads:
                dv_ref[rows, hh * HEAD_DIM:(hh + 1) * HEAD_DIM] += dv[hh]
            return tuple(carry[hh] + jnp.sum(lm[hh], axis=-1, keepdims=True) for hh in heads)

        rc = back(qi, tuple(jnp.zeros((t, 1), F32) for _ in range(hg)), True)
        lax.fori_loop(1, qi + 1, lambda i, c: back(qi - i, c, False), rc)

        earlier = jnp.where(lax.broadcasted_iota(jnp.int32, (2 * t, t), 0) % t
                            < lax.broadcasted_iota(jnp.int32, (2 * t, t), 1), 1.0, 0.0).astype(BF16)

        def fwd(kj, carry, diagonal):
            off = pl.multiple_of(kj * t, t)
            heads = range(hg)
            rows = pl.ds(off, t)
            g = [g_s[hh, :, rows] for hh in heads]
            parts = [jnp.concatenate(_split2(g[hh]), axis=1) for hh in heads]
            gsum = [_dot(parts[hh], earlier) + carry[hh][0] for hh in heads]
            dz = []
            for hh in heads:
                beta = beta_s[hh, :, rows].astype(F32)
                d = g[hh] * (1.0 - beta) - gsum[hh] * beta
                if diagonal:
                    d = jnp.where(before, d, 0.0)
                dz.append(d.astype(BF16))
            dk = [_dot_tn(dz[hh], _head(q_ref, hh)) * scale for hh in heads]
            dq = [_dot(dz[hh], _head(k_ref, hh, rows)) * scale for hh in heads]
            for hh in heads:
                dk_ref[rows, hh * HEAD_DIM:(hh + 1) * HEAD_DIM] += dk[hh]
            return tuple((carry[hh][0] + jnp.sum(g[hh], axis=-1, keepdims=True), carry[hh][1] + dq[hh])
                         for hh in heads)

        init = tuple((jnp.zeros((t, 1), F32), jnp.zeros((t, HEAD_DIM), F32)) for _ in range(hg))
        carry = lax.fori_loop(0, qi, lambda kj, c: fwd(kj, c, False), init)
        carry = fwd(qi, carry, True)
        for hh in range(hg):
            dq_ref[:, hh * HEAD_DIM:(hh + 1) * HEAD_DIM] = carry[hh][1]

    once = pl.Buffered(buffer_count=1)
    tile_spec = pl.BlockSpec((t, gw), lambda h, i: (i, h))
    full = pl.BlockSpec((s, gw), lambda h, i: (0, h), pipeline_mode=once)
    w = nheads * HEAD_DIM
    return _call(
        body,
        grid=(nheads // hg, s // t),
        in_specs=[pl.BlockSpec((t, gw), lambda h, i: (i, colq // hg + h)),
                  pl.BlockSpec((s, gw), lambda h, i: (0, colk // hg + h), pipeline_mode=once),
                  pl.BlockSpec((s, gw), lambda h, i: (0, colv // hg + h), pipeline_mode=once),
                  tile_spec],
        out_specs=[tile_spec, full, full],
        out_shape=[jax.ShapeDtypeStruct((s, w), F32)] * 3,
        scratch_shapes=[pltpu.VMEM((hg, t, s), F32), pltpu.VMEM((hg, t, s), BF16)],
        semantics=("arbitrary", "arbitrary"),
        name="sb_bwd",
        args=(proj, proj, proj, do),
        carry=carry,
    )


def _mem_fwd(qn, kn, mkv, nheads):
    s = qn.shape[0]
    mtok = kn.shape[0]
    t = _tile(s, ATT_TILE)
    scale = HEAD_DIM ** -0.5

    def body(q_ref, k_ref, v_ref, o_ref):
        sc = _dot_nt(q_ref[...], k_ref[...]) * scale
        p = jnp.exp(sc - jnp.max(sc, axis=-1, keepdims=True))
        p = p / jnp.sum(p, axis=-1, keepdims=True)
        o_ref[...] = _dot(p.astype(BF16), v_ref[...]).astype(BF16)

    return pl.pallas_call(
        body,
        grid=(nheads, s // t),
        in_specs=[pl.BlockSpec((t, HEAD_DIM), lambda h, i: (i, h)),
                  pl.BlockSpec((mtok, HEAD_DIM), lambda h, i: (0, h)),
                  pl.BlockSpec((mtok, HEAD_DIM), lambda h, i: (0, nheads + h))],
        out_specs=pl.BlockSpec((t, HEAD_DIM), lambda h, i: (i, h)),
        out_shape=jax.ShapeDtypeStruct((s, nheads * HEAD_DIM), BF16),
        compiler_params=_params("parallel", "parallel"),
        name="mem_fwd",
    )(qn, kn, mkv)


def _mem_bwd(qn, kn, mkv, do, nheads):
    s = qn.shape[0]
    mtok = kn.shape[0]
    t = _tile(s, ATT_TILE)
    scale = HEAD_DIM ** -0.5

    def body(q_ref, k_ref, v_ref, do_ref, dq_ref, dk_ref, dv_ref):
        @pl.when(pl.program_id(1) == 0)
        def _():
            dk_ref[...] = jnp.zeros_like(dk_ref)
            dv_ref[...] = jnp.zeros_like(dv_ref)

        q = q_ref[...]
        k = k_ref[...]
        do_ = do_ref[...]
        sc = _dot_nt(q, k) * scale
        p = jnp.exp(sc - jnp.max(sc, axis=-1, keepdims=True))
        p = p / jnp.sum(p, axis=-1, keepdims=True)
        dp = _dot_nt(do_, v_ref[...])
        ds = (p * (dp - jnp.sum(p * dp, axis=-1, keepdims=True))).astype(BF16)
        dq_ref[...] = _dot(ds, k) * scale
        dk_ref[...] += _dot_tn(ds, q) * scale
        dv_ref[...] += _dot_tn(p.astype(BF16), do_)

    tile = pl.BlockSpec((t, HEAD_DIM), lambda h, i: (i, h))
    kspec = pl.BlockSpec((mtok, HEAD_DIM), lambda h, i: (0, h))
    w = nheads * HEAD_DIM
    return pl.pallas_call(
        body,
        grid=(nheads, s // t),
        in_specs=[tile, kspec, pl.BlockSpec((mtok, HEAD_DIM), lambda h, i: (0, nheads + h)), tile],
        out_specs=[tile, kspec, kspec],
        out_shape=[jax.ShapeDtypeStruct((s, w), F32), jax.ShapeDtypeStruct((mtok, w), F32),
                   jax.ShapeDtypeStruct((mtok, w), F32)],
        compiler_params=_params("arbitrary", "arbitrary"),
        name="mem_bwd",
    )(qn, kn, mkv, do)


def _merge_fwd(p0, p1, p2, proj, colg, b_gate):
    s, d = p0.shape
    tm, tn = _tile(s, ROW_TILE), _tile(d, COL_TILE)
    g0 = colg * LANES // tn
    nj = d // tn

    def body(p0_ref, p1_ref, p2_ref, ga_ref, gb_ref, gc_ref, b_ref, o_ref):
        acc = jnp.zeros((tm, tn), F32)
        for b, (p_ref, g_ref) in enumerate(((p0_ref, ga_ref), (p1_ref, gb_ref), (p2_ref, gc_ref))):
            gate = jax.nn.sigmoid(g_ref[...].astype(F32) + b_ref[b:b + 1, :])
            acc = acc + gate * p_ref[...]
        o_ref[...] = acc.astype(BF16)

    blk = pl.BlockSpec((tm, tn), lambda i, j: (i, j))
    gates = [pl.BlockSpec((tm, tn), functools.partial(lambda i, j, b: (i, g0 + b * nj + j), b=b)) for b in range(3)]
    return pl.pallas_call(
        body,
        grid=(s // tm, nj),
        in_specs=[blk, blk, blk] + gates + [pl.BlockSpec((3, tn), lambda i, j: (0, j))],
        out_specs=blk,
        out_shape=jax.ShapeDtypeStruct((s, d), BF16),
        compiler_params=_params("parallel", "parallel"),
        name="merge_fwd",
    )(p0, p1, p2, proj, proj, proj, b_gate)


def _merge_bwd(dmerged, p0, p1, p2, proj, colg, b_gate):
    s, d = p0.shape
    tm, tn = _tile(s, ROW_TILE), _tile(d, COL_TILE)
    g0 = colg * LANES // tn
    nj = d // tn

    def body(dm_ref, p0_ref, p1_ref, p2_ref, ga_ref, gb_ref, gc_ref, b_ref,
             d0_ref, d1_ref, d2_ref, dga_ref, dgb_ref, dgc_ref, db_ref):
        dm = dm_ref[...].astype(F32)
        parts = []
        for b, (p_ref, g_ref, dp_ref, dg_ref) in enumerate(((p0_ref, ga_ref, d0_ref, dga_ref),
                                                            (p1_ref, gb_ref, d1_ref, dgb_ref),
                                                            (p2_ref, gc_ref, d2_ref, dgc_ref))):
            gate = jax.nn.sigmoid(g_ref[...].astype(F32) + b_ref[b:b + 1, :])
            dp_ref[...] = (dm * gate).astype(BF16)
            dgate = dm * p_ref[...] * gate * (1.0 - gate)
            dg_ref[...] = dgate.astype(BF16)
            parts.append(jnp.sum(dgate, axis=0, keepdims=True))
        part = jnp.concatenate(parts, axis=0)

        @pl.when(pl.program_id(1) == 0)
        def _():
            db_ref[...] = part

        @pl.when(pl.program_id(1) > 0)
        def _():
            db_ref[...] += part

    blk = pl.BlockSpec((tm, tn), lambda j, i: (i, j))
    gates = [pl.BlockSpec((tm, tn), functools.partial(lambda j, i, b: (i, g0 + b * nj + j), b=b)) for b in range(3)]
    bias = pl.BlockSpec((3, tn), lambda j, i: (0, j))
    return pl.pallas_call(
        body,
        grid=(nj, s // tm),
        in_specs=[blk, blk, blk, blk] + gates + [bias],
        out_specs=[blk] * 6 + [bias],
        out_shape=[jax.ShapeDtypeStruct((s, d), BF16)] * 6 + [jax.ShapeDtypeStruct((3, d), F32)],
        compiler_params=_params("parallel", "arbitrary"),
        name="merge_bwd",
    )(dmerged, p0, p1, p2, proj, proj, proj, b_gate)


def _shift_down(v, n):
    rows = lax.broadcasted_iota(jnp.int32, v.shape, 0)
    return jnp.where(rows >= n, pltpu.roll(v, n, 0), 0.0)


def _shift_up(v, n):
    s = v.shape[0]
    rows = lax.broadcasted_iota(jnp.int32, v.shape, 0)
    return jnp.where(rows < s - n, pltpu.roll(v, s - n, 0), 0.0)


def _conv(v, w_ref, b_ref):
    taps = w_ref.shape[0]
    out = v * w_ref[taps - 1:taps, :] + b_ref[...]
    for n in range(1, taps):
        out = out + _shift_down(v, n) * w_ref[taps - 1 - n:taps - n, :]
    return out


def _conv_act_fwd(up, conv_w, conv_b):
    s, f2 = up.shape
    f = f2 // 2
    tn = LANES
    nj = f // tn
    taps = conv_w.shape[0]

    def body(ug_ref, uv_ref, wg_ref, wv_ref, bg_ref, bv_ref, o_ref):
        cg = _conv(ug_ref[...].astype(F32), wg_ref, bg_ref)
        cv = _conv(uv_ref[...].astype(F32), wv_ref, bv_ref)
        o_ref[...] = (cg * jax.nn.sigmoid(cg) * cv).astype(BF16)

    return pl.pallas_call(
        body,
        grid=(nj,),
        in_specs=[pl.BlockSpec((s, tn), lambda j: (0, j)), pl.BlockSpec((s, tn), lambda j: (0, nj + j)),
                  pl.BlockSpec((taps, tn), lambda j: (0, j)), pl.BlockSpec((taps, tn), lambda j: (0, nj + j)),
                  pl.BlockSpec((1, tn), lambda j: (0, j)), pl.BlockSpec((1, tn), lambda j: (0, nj + j))],
        out_specs=pl.BlockSpec((s, tn), lambda j: (0, j)),
        out_shape=jax.ShapeDtypeStruct((s, f), BF16),
        compiler_params=_params("parallel"),
        name="conv_act_fwd",
    )(up, up, conv_w, conv_w, conv_b, conv_b)


def _conv_act_bwd(up, conv_w, conv_b, dact):
    s, f2 = up.shape
    f = f2 // 2
    tn = LANES
    nj = f // tn
    taps = conv_w.shape[0]

    def half(v, du, w_ref, dup_ref, dw_ref, db_ref):
        dup = du * w_ref[taps - 1:taps, :]
        rows = [None] * taps
        rows[taps - 1] = jnp.sum(du * v, axis=0, keepdims=True)
        for n in range(1, taps):
            dup = dup + _shift_up(du, n) * w_ref[taps - 1 - n:taps - n, :]
            rows[taps - 1 - n] = jnp.sum(du * _shift_down(v, n), axis=0, keepdims=True)
        dup_ref[...] = dup.astype(BF16)
        dw_ref[...] = jnp.concatenate(rows, axis=0)
        db_ref[...] = jnp.sum(du, axis=0, keepdims=True)

    def body(ug_ref, uv_ref, wg_ref, wv_ref, bg_ref, bv_ref, da_ref,
             dug_ref, duv_ref, dwg_ref, dwv_ref, dbg_ref, dbv_ref):
        ug = ug_ref[...].astype(F32)
        uv = uv_ref[...].astype(F32)
        cg = _conv(ug, wg_ref, bg_ref)
        cv = _conv(uv, wv_ref, bv_ref)
        da = da_ref[...].astype(F32)
        sg = jax.nn.sigmoid(cg)
        dcv = da * cg * sg
        dcg = da * cv * (sg + cg * sg * (1.0 - sg))
        half(ug, dcg, wg_ref, dug_ref, dwg_ref, dbg_ref)
        half(uv, dcv, wv_ref, duv_ref, dwv_ref, dbv_ref)

    lo = lambda rows: pl.BlockSpec((rows, tn), lambda j: (0, j))
    hi = lambda rows: pl.BlockSpec((rows, tn), lambda j: (0, nj + j))
    return pl.pallas_call(
        body,
        grid=(nj,),
        in_specs=[lo(s), hi(s), lo(taps), hi(taps), lo(1), hi(1), lo(s)],
        out_specs=[lo(s), lo(s), lo(taps), lo(taps), lo(1), lo(1)],
        out_shape=[jax.ShapeDtypeStruct((s, f), BF16)] * 2 + [jax.ShapeDtypeStruct((taps, f), F32)] * 2
        + [jax.ShapeDtypeStruct((1, f), F32)] * 2,
        compiler_params=_params("parallel"),
        name="conv_act_bwd",
    )(up, up, conv_w, conv_w, conv_b, conv_b, dact)


def _loss_grad(y, target):
    s, d = y.shape
    tm = _tile(s, ROW_TILE)

    def body(y_ref, t_ref, dy_ref, dyb_ref, l_ref):
        e = y_ref[...] - t_ref[...]
        dy = e * (1.0 / d)
        dy_ref[...] = dy
        dyb_ref[...] = dy.astype(BF16)
        tot = jnp.sum(jnp.sum(e * e, axis=-1, keepdims=True), axis=0, keepdims=True)
        l_ref[...] = jnp.broadcast_to(tot, (8, LANES))

    row = pl.BlockSpec((tm, d), lambda i: (i, 0))
    return pl.pallas_call(
        body,
        grid=(s // tm,),
        in_specs=[row, row],
        out_specs=[row, row, pl.BlockSpec((8, LANES), lambda i: (i, 0))],
        out_shape=[jax.ShapeDtypeStruct((s, d), F32), jax.ShapeDtypeStruct((s, d), BF16),
                   jax.ShapeDtypeStruct((s // tm * 8, LANES), F32)],
        compiler_params=_params("parallel"),
        name="loss_grad",
    )(y, target)


def _row_tile(rows, row_bytes, budget):
    if rows * row_bytes <= budget or rows % 8:
        return rows
    best = 8
    for t in range(8, rows, 8):
        if rows % t == 0 and t * row_bytes <= budget:
            best = t
    return best


def _adamw(w, g, m, v, name):
    r, c = w.shape
    tr = _row_tile(r, c * 4, ADAM_BLOCK_BYTES)

    def body(w_ref, g_ref, m_ref, v_ref, d_ref, mo_ref, vo_ref):
        gg = g_ref[...]
        m_new = ADAM_B1 * m_ref[...] + (1.0 - ADAM_B1) * gg
        v_new = ADAM_B2 * v_ref[...] + (1.0 - ADAM_B2) * (gg * gg)
        m_hat = m_new / (1.0 - ADAM_B1 ** ADAM_STEP)
        v_hat = v_new / (1.0 - ADAM_B2 ** ADAM_STEP)
        d_ref[...] = -ADAM_LR * (m_hat / (jnp.sqrt(v_hat) + ADAM_EPS) + ADAM_WD * w_ref[...])
        mo_ref[...] = m_new
        vo_ref[...] = v_new

    blk = pl.BlockSpec((tr, c), lambda i: (i, 0))
    return pl.pallas_call(
        body,
        grid=(r // tr,),
        in_specs=[blk] * 4,
        out_specs=[blk] * 3,
        out_shape=[jax.ShapeDtypeStruct((r, c), F32)] * 3,
        compiler_params=_params("parallel"),
        name=name,
    )(w, g, m, v)


def _add_sibling(g, r1, core, name):
    _, _, h, c = g.shape
    th = _row_tile(h, c * 2, ADAM_BLOCK_BYTES)

    def body(core_ref, g_ref, r_ref, o_ref):
        o_ref[...] = (g_ref[...].astype(F32) + r_ref[...].astype(F32)).astype(BF16)

    return pl.pallas_call(
        body,
        grid_spec=pltpu.PrefetchScalarGridSpec(
            num_scalar_prefetch=1,
            grid=(N_CHIPS, h // th),
            in_specs=[pl.BlockSpec((None, None, th, c), lambda j, i, core_ref: (j, core_ref[0], i, 0)),
                      pl.BlockSpec((None, th, c), lambda j, i, core_ref: (j, i, 0))],
            out_specs=pl.BlockSpec((None, th, c), lambda j, i, core_ref: (j, i, 0)),
        ),
        out_shape=jax.ShapeDtypeStruct((N_CHIPS, h, c), BF16),
        compiler_params=_params("parallel", "parallel"),
        name=name,
    )(core, g, r1)


def _add_chips(hsum, r2, chip_core, name):
    _, h, c = hsum.shape
    th = _row_tile(h, c * 4, ADAM_BLOCK_BYTES)

    def body(sel_ref, own_ref, r_ref, o_ref):
        acc = own_ref[...].astype(F32)
        for j in range(N_CHIPS - 1):
            acc = acc + r_ref[j].astype(F32)
        o_ref[...] = acc

    return pl.pallas_call(
        body,
        grid_spec=pltpu.PrefetchScalarGridSpec(
            num_scalar_prefetch=1,
            grid=(h // th,),
            in_specs=[pl.BlockSpec((None, th, c), lambda i, sel_ref: (sel_ref[0], i, 0)),
                      pl.BlockSpec((N_CHIPS - 1, th, c), lambda i, sel_ref: (0, i, 0))],
            out_specs=pl.BlockSpec((None, th, c), lambda i, sel_ref: (sel_ref[1], i, 0)),
        ),
        out_shape=jax.ShapeDtypeStruct((2, h, c), F32),
        compiler_params=_params("parallel"),
        name=name,
    )(chip_core, hsum, r2)


def _sum_devices(parts):
    _, r, c = parts.shape

    def body(p_ref, o_ref):
        acc = p_ref[0]
        for j in range(1, N_DEV):
            acc = acc + p_ref[j]
        o_ref[...] = acc

    return pl.pallas_call(
        body,
        out_shape=jax.ShapeDtypeStruct((r, c), F32),
        compiler_params=pltpu.CompilerParams(vmem_limit_bytes=VMEM_LIMIT_BYTES),
        name="sum_devices",
    )(parts)


def _swap_halves(grads, name):
    n = len(grads)

    def body(*refs):
        ins, outs = refs[:n], refs[n:2 * n]
        send_sems, recv_sems = refs[2 * n:]
        x, y, c, _ = _place()
        copies = [_remote(ins[i].at[:, 1 - c], outs[i], send_sems.at[i], recv_sems.at[i], (x, y, 1 - c))
                  for i in range(n)]
        for cp in copies:
            cp.start()
        for cp in copies:
            cp.wait()

    return pl.pallas_call(
        body,
        in_specs=[ANY] * n,
        out_specs=[ANY] * n,
        out_shape=[jax.ShapeDtypeStruct((g.shape[0],) + g.shape[2:], g.dtype) for g in grads],
        scratch_shapes=[pltpu.SemaphoreType.DMA((n,)), pltpu.SemaphoreType.DMA((n,))],
        name=name,
    )(*grads)


def _join_halves(finals):
    n = len(finals)

    def body(*refs):
        outs = refs[n:2 * n]
        send_sems, recv_sems = refs[2 * n:]
        x, y, c, _ = _place()
        sends = [_remote(outs[i].at[c], outs[i].at[c], send_sems.at[i], recv_sems.at[i], (x, y, 1 - c))
                 for i in range(n)]
        for cp in sends:
            cp.start()
        for i in range(n):
            sends[i].wait_send()
            other = outs[i].at[1 - c]
            _remote(other, other, send_sems.at[i], recv_sems.at[i], (x, y, 1 - c)).wait_recv()

    return pl.pallas_call(
        body,
        in_specs=[ANY] * n,
        out_specs=[ANY] * n,
        out_shape=[jax.ShapeDtypeStruct(f.shape, f.dtype) for f in finals],
        input_output_aliases={i: i for i in range(n)},
        scratch_shapes=[pltpu.SemaphoreType.DMA((n,)), pltpu.SemaphoreType.DMA((n,))],
        name="join_halves",
    )(*finals)


def _gather_small(vec):
    k = N_DEV - 1

    def body(v_ref, o_ref, send_sems, recv_sems, local_sem):
        x, y, c, _ = _place()
        me = 4 * x + 2 * y + c
        local = pltpu.make_async_copy(v_ref, o_ref.at[me], local_sem)
        local.start()
        peers = [(x ^ (r >> 2 & 1), y ^ (r >> 1 & 1), c ^ (r & 1)) for r in range(1, N_DEV)]
        sends = [_remote(v_ref, o_ref.at[me], send_sems.at[j], recv_sems.at[j], p) for j, p in enumerate(peers)]
        for cp in sends:
            cp.start()
        for j, (px, py, pc) in enumerate(peers):
            sends[j].wait_send()
            blk = o_ref.at[4 * px + 2 * py + pc]
            _remote(blk, blk, send_sems.at[j], recv_sems.at[j], (px, py, pc)).wait_recv()
        local.wait()

    return pl.pallas_call(
        body,
        in_specs=[ANY],
        out_specs=ANY,
        out_shape=jax.ShapeDtypeStruct((N_DEV,) + vec.shape, vec.dtype),
        scratch_shapes=[pltpu.SemaphoreType.DMA((k,)), pltpu.SemaphoreType.DMA((k,)), pltpu.SemaphoreType.DMA(())],
        name="gather_small",
    )(vec)


ROW_SHARDED = ("w_in_main", "w_in_f", "w_mem_kv", "w_out", "w_down")
COL_SHARDED = ("w_br_fox", "w_br_sb", "w_br_mem", "w_up")
BIG = ROW_SHARDED + COL_SHARDED


def _whole(name, gathered, shard, chip):
    a = lax.dynamic_update_slice(gathered, shard[None], (chip, 0, 0))
    if name in ROW_SHARDED:
        return a.reshape(N_CHIPS * a.shape[1], a.shape[2])
    return a.transpose(1, 0, 2).reshape(a.shape[1], N_CHIPS * a.shape[2])


def _by_shard(name, grad):
    if name in ROW_SHARDED:
        a = grad.reshape(N_CHIPS, grad.shape[0] // N_CHIPS, grad.shape[1])
    else:
        a = grad.reshape(grad.shape[0], N_CHIPS, grad.shape[1] // N_CHIPS).transpose(1, 0, 2)
    return a.reshape(N_CHIPS, 2, a.shape[1] // 2, a.shape[2])


def _sibling_sums(names, g, core, tag):
    split = [_by_shard(name, g[name]) for name in names]
    theirs = _swap_halves(split, "swap_halves_" + tag)
    return [_add_sibling(a, r, core, "add_sibling_" + name) for name, a, r in zip(names, split, theirs)]


GATHER_FIRST = ("w_in_main", "w_in_f", "w_mem_kv")
GATHER_LATE = ("w_down", "w_out", "w_br_fox", "w_br_sb", "w_br_mem")
REDUCE_FFN = ("w_down", "w_up")
REDUCE_MIX = ("w_out", "w_br_fox", "w_br_sb", "w_br_mem")
REDUCE_IN = ("w_in_main", "w_in_f", "w_mem_kv")


def _local_step(x, mem, target, w, shard, chip, core, chip_core):
    d = x.shape[1]
    nf = shard["w_br_fox"].shape[0] // HEAD_DIM
    nsb = shard["w_br_sb"].shape[0] // HEAD_DIM
    nm = shard["w_br_mem"].shape[0] // HEAD_DIM
    w = dict(w)

    def take(names, gathered):
        for name, a in zip(names, gathered):
            w[name] = _whole(name, a, shard[name], chip)

    take(GATHER_FIRST, _exchange(_Gather([shard[name] for name in GATHER_FIRST]), "gather_first"))
    fq, fk, fv = 0, nf, 2 * nf
    sq, sk, sv = 3 * nf, 3 * nf + nsb, 3 * nf + 2 * nsb
    mq = 3 * nf + 3 * nsb
    gates = mq + nm

    h, rstd1 = _rms_fwd(x, w["g_mix"], "rms_mix_fwd")
    proj, moved = _mm(h, w["w_in_main"], "nn", BF16, "proj_main", carry=_Gather([shard["w_up"]]))
    take(("w_up",), moved)
    f_logit = _mm(h, w["w_in_f"], "nn", F32, "proj_forget")
    c_sum = _forget_fwd(f_logit, w["b_forget"])
    c_t = c_sum[:, :nf].T
    c_col, c_row = c_t[:, :, None], c_t[:, None, :]
    qn = _headnorm_fwd(proj, fq, nf, w["g_q_fox"], "fox_qnorm_fwd")
    kn = _headnorm_fwd(proj, fk, nf, w["g_k_fox"], "fox_knorm_fwd")
    (o_fox, o_fox32, lse), moved = _fox_fwd(qn, kn, proj, fv, c_col, c_row, nf,
                                            carry=_Gather([shard[name] for name in GATHER_LATE]))
    take(GATHER_LATE, moved)
    o_sb = _sb_fwd(proj, sq, sk, sv, nsb)
    memn, rstd_m = _rms_fwd(mem, w["g_mem"], "rms_mem_fwd")
    mkv = _mm(memn, w["w_mem_kv"], "nn", BF16, "mem_kv")
    kmn = _headnorm_fwd(mkv, 0, nm, w["g_k_mem"], "mem_knorm_fwd")
    qmn = _headnorm_fwd(proj, mq, nm, w["g_q_mem"], "mem_qnorm_fwd")
    o_mem = _mem_fwd(qmn, kmn, mkv, nm)
    p0 = _mm(o_fox, w["w_br_fox"], "nn", F32, "branch_fox")
    p1 = _mm(o_sb, w["w_br_sb"], "nn", F32, "branch_sb")
    p2 = _mm(o_mem, w["w_br_mem"], "nn", F32, "branch_mem")
    merged = _merge_fwd(p0, p1, p2, proj, gates, w["b_gate"])
    x1 = _mm(merged, w["w_out"], "nn", F32, "out_proj", residual=x)
    h2, rstd2 = _rms_fwd(x1, w["g_ffn"], "rms_ffn_fwd")
    up = _mm(h2, w["w_up"], "nn", BF16, "ffn_up")
    act = _conv_act_fwd(up, w["conv_w"], w["conv_b"])
    y = _mm(act, w["w_down"], "nn", F32, "ffn_down", residual=x1)
    dy, dyb, lparts = _loss_grad(y, target)
    loss = (0.5 / d) * jnp.sum(lparts[::8, 0])

    g = {}
    dact = _mm(dyb, w["w_down"], "nt", BF16, "ffn_down_dx")
    g["w_down"] = _mm(act, dyb, "tn", BF16, "ffn_down_dw")
    dug, duv, dwg, dwv, dbg, dbv = _conv_act_bwd(up, w["conv_w"], w["conv_b"], dact)
    dup = jnp.concatenate([dug, duv], axis=1)
    g["conv_w"] = jnp.concatenate([dwg, dwv], axis=1)
    g["conv_b"] = jnp.concatenate([dbg, dbv], axis=1)
    dh2 = _mm(dup, w["w_up"], "nt", BF16, "ffn_up_dx")
    g["w_up"] = _mm(h2, dup, "tn", BF16, "ffn_up_dw")
    sums_ffn = _sibling_sums(REDUCE_FFN, g, core, "ffn")
    dx1, dx1b, g["g_ffn"] = _rms_bwd(dh2, x1, rstd2, w["g_ffn"], dy, "rms_ffn_bwd")
    dmerged = _mm(dx1b, w["w_out"], "nt", BF16, "out_proj_dx")
    g["w_out"] = _mm(merged, dx1b, "tn", BF16, "out_proj_dw")
    dp0, dp1, dp2, dga, dgb, dgc, g["b_gate"] = _merge_bwd(dmerged, p0, p1, p2, proj, gates, w["b_gate"])
    do_fox = _mm(dp0, w["w_br_fox"], "nt", BF16, "branch_fox_dx")
    do_sb = _mm(dp1, w["w_br_sb"], "nt", BF16, "branch_sb_dx")
    do_mem = _mm(dp2, w["w_br_mem"], "nt", BF16, "branch_mem_dx")
    g["w_br_fox"] = _mm(o_fox, dp0, "tn", BF16, "branch_fox_dw")
    g["w_br_sb"] = _mm(o_sb, dp1, "tn", BF16, "branch_sb_dw")
    g["w_br_mem"] = _mm(o_mem, dp2, "tn", BF16, "branch_mem_dw")
    sums_mix = _sibling_sums(REDUCE_MIX, g, core, "mix")

    (dqn, dkn, dfv, drs, dcs), others_ffn = _fox_bwd(qn, kn, proj, fv, c_col, c_row, o_fox32, do_fox, lse, nf,
                                                     carry=_Scatter(sums_ffn))
    dfq, g["g_q_fox"] = _headnorm_bwd(dqn, proj, fq, nf, w["g_q_fox"], "fox_qnorm_bwd")
    dfk, g["g_k_fox"] = _headnorm_bwd(dkn, proj, fk, nf, w["g_k_fox"], "fox_knorm_bwd")
    dc = jnp.pad((drs[:, :, 0] - dcs[:, 0, :]).T, ((0, 0), (0, LANES - nf)))
    df, g["b_forget"] = _forget_bwd(dc, f_logit, w["b_forget"])
    (dsq, dsk, dsv), others_mix = _sb_bwd(proj, sq, sk, sv, do_sb, nsb, carry=_Scatter(sums_mix))
    dqmn, dkmn, dvm = _mem_bwd(qmn, kmn, mkv, do_mem, nm)
    dmq, g["g_q_mem"] = _headnorm_bwd(dqmn, proj, mq, nm, w["g_q_mem"], "mem_qnorm_bwd")
    dkm, g["g_k_mem"] = _headnorm_bwd(dkmn, mkv, 0, nm, w["g_k_mem"], "mem_knorm_bwd")
    dmkv = jnp.concatenate([dkm, dvm.astype(BF16)], axis=1)
    g["w_mem_kv"] = _mm(memn, dmkv, "tn", BF16, "mem_kv_dw")
    dmemn = _mm(dmkv, w["w_mem_kv"], "nt", BF16, "mem_kv_dx")
    _, _, g["g_mem"] = _rms_bwd(dmemn, mem, rstd_m, w["g_mem"], None, "rms_mem_bwd")

    dproj = jnp.concatenate([dfq, dfk, dfv.astype(BF16), dsq.astype(BF16), dsk.astype(BF16), dsv.astype(BF16),
                             dmq, dga, dgb, dgc], axis=1)
    dfb = df.astype(BF16)
    g["w_in_main"] = _mm(h, dproj, "tn", BF16, "proj_main_dw")
    g["w_in_f"] = _mm(h, dfb, "tn", BF16, "proj_forget_dw")
    sums_in = _sibling_sums(REDUCE_IN, g, core, "in")
    dh_main, others_in = _mm(dproj, w["w_in_main"], "nt", F32, "proj_main_dx", carry=_Scatter(sums_in))
    dh = _mm(dfb, w["w_in_f"], "nt", F32, "proj_forget_dx", residual=dh_main)
    grad_x, _, g["g_mix"] = _rms_bwd(dh, x, rstd1, w["g_mix"], dx1, "rms_mix_bwd")

    names = REDUCE_FFN + REDUCE_MIX + REDUCE_IN
    finals = [_add_chips(own, theirs, chip_core, "add_chips_" + name)
              for name, own, theirs in zip(names, sums_ffn + sums_mix + sums_in, others_ffn + others_mix + others_in)]
    summed = {name: a.reshape(2 * a.shape[1], a.shape[2]) for name, a in zip(names, _join_halves(finals))}
    return loss, grad_x, g, summed


SMALL = ("g_mix", "b_forget", "g_q_fox", "g_k_fox", "g_mem", "g_q_mem", "g_k_mem", "b_gate", "g_ffn", "conv_w",
         "conv_b")
SMALL_SHARDED = ("b_gate", "conv_w")
PACK_ROWS = 8


def _pack(arrs):
    flat = jnp.concatenate([a.reshape(-1) for a in arrs])
    unit = PACK_ROWS * LANES
    flat = jnp.pad(flat, (0, -flat.shape[0] % unit))
    return flat.reshape(-1, LANES)


def _unpack(packed, shapes):
    flat = packed.reshape(-1)
    out, at = [], 0
    for s in shapes:
        n = 1
        for dim in s:
            n *= dim
        out.append(flat[at:at + n].reshape(s))
        at += n
    return out


def kernel(x, mem, g_mix, w_in, b_forget, g_q_fox, g_k_fox, g_mem, w_mem_kv, g_q_mem, g_k_mem, w_br_fox, w_br_sb, w_br_mem, b_gate, w_out, g_ffn, w_up, conv_w, conv_b, w_down, loss_target, m_g_mix, m_w_in, m_b_forget, m_g_q_fox, m_g_k_fox, m_g_mem, m_w_mem_kv, m_g_q_mem, m_g_k_mem, m_w_br_fox, m_w_br_sb, m_w_br_mem, m_b_gate, m_w_out, m_g_ffn, m_w_up, m_conv_w, m_conv_b, m_w_down, v_g_mix, v_w_in, v_b_forget, v_g_q_fox, v_g_k_fox, v_g_mem, v_w_mem_kv, v_g_q_mem, v_g_k_mem, v_w_br_fox, v_w_br_sb, v_w_br_mem, v_b_gate, v_w_out, v_g_ffn, v_w_up, v_conv_w, v_conv_b, v_w_down):
    given = dict(g_mix=g_mix, w_in=w_in, b_forget=b_forget, g_q_fox=g_q_fox, g_k_fox=g_k_fox, g_mem=g_mem,
                 w_mem_kv=w_mem_kv, g_q_mem=g_q_mem, g_k_mem=g_k_mem, w_br_fox=w_br_fox, w_br_sb=w_br_sb,
                 w_br_mem=w_br_mem, b_gate=b_gate, w_out=w_out, g_ffn=g_ffn, w_up=w_up, conv_w=conv_w, conv_b=conv_b,
                 w_down=w_down)
    m_in = dict(g_mix=m_g_mix, w_in=m_w_in, b_forget=m_b_forget, g_q_fox=m_g_q_fox, g_k_fox=m_g_k_fox, g_mem=m_g_mem,
                w_mem_kv=m_w_mem_kv, g_q_mem=m_g_q_mem, g_k_mem=m_g_k_mem, w_br_fox=m_w_br_fox, w_br_sb=m_w_br_sb,
                w_br_mem=m_w_br_mem, b_gate=m_b_gate, w_out=m_w_out, g_ffn=m_g_ffn, w_up=m_w_up, conv_w=m_conv_w,
                conv_b=m_conv_b, w_down=m_w_down)
    v_in = dict(g_mix=v_g_mix, w_in=v_w_in, b_forget=v_b_forget, g_q_fox=v_g_q_fox, g_k_fox=v_g_k_fox, g_mem=v_g_mem,
                w_mem_kv=v_w_mem_kv, g_q_mem=v_g_q_mem, g_k_mem=v_g_k_mem, w_br_fox=v_w_br_fox, w_br_sb=v_w_br_sb,
                w_br_mem=v_w_br_mem, b_gate=v_b_gate, w_out=v_w_out, g_ffn=v_g_ffn, w_up=v_w_up, conv_w=v_conv_w,
                conv_b=v_conv_b, w_down=v_w_down)
    layered = {k: a.ndim == 3 for k, a in given.items()}
    drop = lambda a: a[0] if a.ndim == 3 else a
    given = {k: drop(a) for k, a in given.items()}
    m_in = {k: drop(a) for k, a in m_in.items()}
    v_in = {k: drop(a) for k, a in v_in.items()}

    xi, yi, ci = lax.axis_index("x"), lax.axis_index("y"), lax.axis_index("c")
    chip = (2 * xi + yi).astype(jnp.int32)
    core_arr = ci.astype(jnp.int32).reshape(1)
    chip_core = jnp.stack([chip, ci.astype(jnp.int32)])

    nf = given["b_forget"].shape[1]
    cut = 3 * given["w_br_fox"].shape[0]

    shard = {
        "w_in_main": jnp.concatenate([given["w_in"][:, :cut], given["w_in"][:, cut + nf:]], axis=1).astype(BF16),
        "w_in_f": jnp.pad(given["w_in"][:, cut:cut + nf], ((0, 0), (0, LANES - nf))).astype(BF16),
    }
    for name in BIG:
        if name not in shard:
            shard[name] = given[name].astype(BF16)
    w = {}
    small_shapes = [given[name].shape for name in SMALL_SHARDED]
    small_parts = _gather_small(_pack([given[name] for name in SMALL_SHARDED]))[0::2]
    per_chip = [_unpack(small_parts[j], small_shapes) for j in range(N_CHIPS)]
    for k, name in enumerate(SMALL_SHARDED):
        w[name] = jnp.concatenate([per_chip[j][k] for j in range(N_CHIPS)], axis=1)
    for name in SMALL:
        if name not in SMALL_SHARDED:
            w[name] = given[name]
    w["b_forget"] = jnp.pad(given["b_forget"], ((0, 0), (0, LANES - nf)))

    loss, grad_x, g, summed = _local_step(x[0], mem[0], loss_target[0], w, shard, chip, core_arr, chip_core)
    loss = lax.psum(loss, ("x", "y", "c"))
    grads = {name: summed[name] for name in BIG if name in given}
    grads["w_in"] = jnp.concatenate([summed["w_in_main"][:, :cut], summed["w_in_f"][:, :nf],
                                     summed["w_in_main"][:, cut:]], axis=1)

    g["b_forget"] = g["b_forget"][:, :nf]
    small_full_shapes = [g[name].shape for name in SMALL]
    small_sum = _unpack(_sum_devices(_gather_small(_pack([g[name] for name in SMALL]))), small_full_shapes)
    for name, a in zip(SMALL, small_sum):
        if name in SMALL_SHARDED:
            width = given[name].shape[1]
            a = lax.dynamic_slice_in_dim(a, chip * width, width, axis=1)
        grads[name] = a

    delta, new_m, new_v = {}, {}, {}
    for name in WEIGHTS:
        if name not in SMALL:
            delta[name], new_m[name], new_v[name] = _adamw(given[name], grads[name], m_in[name], v_in[name],
                                                           "adamw_" + name)
    shapes = [given[name].shape for name in SMALL]
    packed = [_pack([src[name] for name in SMALL]) for src in (given, grads, m_in, v_in)]
    for dst, res in zip((delta, new_m, new_v), _adamw(*packed, "adamw_small")):
        for name, a in zip(SMALL, _unpack(res, shapes)):
            dst[name] = a

    out = [loss, grad_x[None]]
    for src in (grads, delta, new_m, new_v):
        out.extend(src[name][None] if layered[name] else src[name] for name in WEIGHTS)
    return tuple(out)
```

```python
import functools

import jax
import jax.numpy as jnp
from jax import lax
from jax.experimental import pallas as pl
from jax.experimental.pallas import tpu as pltpu

F32 = jnp.float32
BF16 = jnp.bfloat16

HEAD_DIM = 128
EPS = 1e-6
NEG_BIG = -1e30

ADAM_LR = 0.001
ADAM_B1 = 0.9
ADAM_B2 = 0.999
ADAM_EPS = 1e-08
ADAM_WD = 0.01
ADAM_STEP = 10

LANES = 128
BF16_SUBLANES = 16
VMEM_LIMIT_BYTES = 56 * 1024 * 1024
MM_TILE = 1024
MM_TILE_K = 2048
ATT_TILE = 256
ROW_TILE = 256
HEADNORM_ROWS = 512
COL_TILE = 512
ADAM_BLOCK_BYTES = 1 << 20

N_CHIPS = 4
N_DEV = 8
MESH = pl.DeviceIdType.MESH

IN_NAMES = ['x', 'mem', 'g_mix', 'w_in', 'b_forget', 'g_q_fox', 'g_k_fox', 'g_mem', 'w_mem_kv', 'g_q_mem', 'g_k_mem',
            'w_br_fox', 'w_br_sb', 'w_br_mem', 'b_gate', 'w_out', 'g_ffn', 'w_up', 'conv_w', 'conv_b', 'w_down']
WEIGHTS = IN_NAMES[2:]


def _tile(n, target):
    if n <= target:
        return n
    for t in range(target - target % LANES, LANES - 1, -LANES):
        if n % t == 0:
            return t
    return n


def _params(*sem):
    return pltpu.CompilerParams(dimension_semantics=sem, vmem_limit_bytes=VMEM_LIMIT_BYTES)


def _log_sigmoid(z):
    return jnp.minimum(z, 0.0) - jnp.log(1.0 + jnp.exp(-jnp.abs(z)))


def _split2(v):
    hi = v.astype(BF16)
    lo = (v - hi.astype(F32)).astype(BF16)
    return hi, lo


def _split3(v):
    hi = v.astype(BF16)
    r = v - hi.astype(F32)
    mid = r.astype(BF16)
    lo = (r - mid.astype(F32)).astype(BF16)
    return hi, mid, lo


def _dot(a, b):
    return lax.dot_general(a, b, (((1,), (0,)), ((), ())), preferred_element_type=F32)


def _dot_nt(a, b):
    return lax.dot_general(a, b, (((1,), (1,)), ((), ())), preferred_element_type=F32)


def _dot_tn(a, b):
    return lax.dot_general(a, b, (((0,), (0,)), ((), ())), preferred_element_type=F32)


ANY = pl.BlockSpec(memory_space=pl.ANY)


def _place():
    x, y, c = lax.axis_index("x"), lax.axis_index("y"), lax.axis_index("c")
    others = [(1 - x, y), (x, 1 - y), (1 - x, 1 - y)]
    return x, y, c, others


def _remote(src, dst, send_sem, recv_sem, to):
    return pltpu.make_async_remote_copy(src_ref=src, dst_ref=dst, send_sem=send_sem, recv_sem=recv_sem,
                                        device_id=to, device_id_type=MESH)


class _Gather:
    def __init__(self, shards):
        self.inputs = list(shards)
        n = len(shards)
        self.out_shapes = [jax.ShapeDtypeStruct((N_CHIPS,) + s.shape, s.dtype) for s in shards]
        self.scratch = [pltpu.SemaphoreType.DMA((6 * n,)), pltpu.SemaphoreType.DMA((6 * n,))]

    def _over_ici(self, ins, outs, sems):
        send_sems, recv_sems = sems
        x, y, c, others = _place()
        me = 2 * x + y
        copies = []
        for i in range(len(ins)):
            h = ins[i].shape[0] // 2
            mine = pl.ds(pl.multiple_of(c * h, BF16_SUBLANES), h)
            for j, (ox, oy) in enumerate(others):
                copies.append(_remote(ins[i].at[mine], outs[i].at[me, mine], send_sems.at[6 * i + j],
                                      recv_sems.at[6 * i + j], (ox, oy, c)))
        return copies

    def start(self, ins, outs, sems):
        for cp in self._over_ici(ins, outs, sems):
            cp.start()

    def finish(self, ins, outs, sems):
        send_sems, recv_sems = sems
        x, y, c, others = _place()
        sibling = (x, y, 1 - c)
        passed = []
        for i in range(len(ins)):
            h = ins[i].shape[0] // 2
            mine = pl.ds(pl.multiple_of(c * h, BF16_SUBLANES), h)
            for j, (ox, oy) in enumerate(others):
                blk = outs[i].at[2 * ox + oy, mine]
                _remote(blk, blk, send_sems.at[6 * i + j], recv_sems.at[6 * i + j], (ox, oy, c)).wait_recv()
                cp = _remote(blk, blk, send_sems.at[6 * i + 3 + j], recv_sems.at[6 * i + 3 + j], sibling)
                cp.start()
                passed.append(cp)
        for i in range(len(ins)):
            h = ins[i].shape[0] // 2
            theirs = pl.ds(pl.multiple_of((1 - c) * h, BF16_SUBLANES), h)
            for j, (ox, oy) in enumerate(others):
                blk = outs[i].at[2 * ox + oy, theirs]
                _remote(blk, blk, send_sems.at[6 * i + 3 + j], recv_sems.at[6 * i + 3 + j], sibling).wait_recv()
        for cp in self._over_ici(ins, outs, sems) + passed:
            cp.wait_send()


class _Scatter:
    def __init__(self, sums):
        self.inputs = list(sums)
        k = N_CHIPS - 1
        self.out_shapes = [jax.ShapeDtypeStruct((k,) + g.shape[1:], g.dtype) for g in sums]
        self.scratch = [pltpu.SemaphoreType.DMA((k * len(sums),)), pltpu.SemaphoreType.DMA((k * len(sums),))]

    def _copies(self, ins, outs, sems):
        send_sems, recv_sems = sems
        _, _, c, others = _place()
        k = N_CHIPS - 1
        return [_remote(ins[i].at[2 * ox + oy], outs[i].at[j], send_sems.at[k * i + j], recv_sems.at[k * i + j],
                        (ox, oy, c))
                for i in range(len(ins)) for j, (ox, oy) in enumerate(others)]

    def start(self, ins, outs, sems):
        for cp in self._copies(ins, outs, sems):
            cp.start()

    def finish(self, ins, outs, sems):
        for cp in self._copies(ins, outs, sems):
            cp.wait()


def _exchange(carry, name):
    n = len(carry.inputs)

    def body(*refs):
        ins, outs, sems = refs[:n], refs[n:2 * n], refs[2 * n:]
        carry.start(ins, outs, sems)
        carry.finish(ins, outs, sems)

    return pl.pallas_call(
        body,
        in_specs=[ANY] * n,
        out_specs=[ANY] * n,
        out_shape=carry.out_shapes,
        scratch_shapes=carry.scratch,
        name=name,
    )(*carry.inputs)


def _call(body, *, grid, in_specs, out_specs, out_shape, scratch_shapes, semantics, name, args, carry=None):
    n_in, n_out, n_scr = len(in_specs), len(out_specs), len(scratch_shapes)
    if carry is None:
        res = pl.pallas_call(body, grid=grid, in_specs=in_specs, out_specs=out_specs, out_shape=out_shape,
                             scratch_shapes=scratch_shapes, compiler_params=_params(*semantics), name=name)(*args)
        return list(res), []
    nci, nco = len(carry.inputs), len(carry.out_shapes)
    a, b = n_in, n_in + nci
    c, d = b + n_out, b + n_out + nco
    e = d + n_scr

    def carried(*refs):
        ids = [pl.program_id(k) for k in range(len(grid))]
        first = functools.reduce(jnp.logical_and, [i == 0 for i in ids])
        last = functools.reduce(jnp.logical_and, [i == n - 1 for i, n in zip(ids, grid)])

        @pl.when(first)
        def _():
            carry.start(refs[a:b], refs[c:d], refs[e:])

        body(*refs[:a], *refs[b:c], *refs[d:e])

        @pl.when(last)
        def _():
            carry.finish(refs[a:b], refs[c:d], refs[e:])

    res = pl.pallas_call(
        carried,
        grid=grid,
        in_specs=list(in_specs) + [ANY] * nci,
        out_specs=list(out_specs) + [ANY] * nco,
        out_shape=list(out_shape) + carry.out_shapes,
        scratch_shapes=list(scratch_shapes) + carry.scratch,
        compiler_params=_params(*(["arbitrary"] * len(grid))),
        name=name,
    )(*args, *carry.inputs)
    return list(res[:n_out]), list(res[n_out:])


def _mm(a, b, mode, out_dtype, name, residual=None, carry=None):
    if mode == "nn":
        (m, k), (k2, n) = a.shape, b.shape
    elif mode == "nt":
        (m, k), (n, k2) = a.shape, b.shape
    else:
        (k, m), (k2, n) = a.shape, b.shape
    assert k == k2, (a.shape, b.shape, mode)
    tm, tn, tk = _tile(m, MM_TILE), _tile(n, MM_TILE), _tile(k, MM_TILE_K)
    nk = k // tk
    dot = {"nn": _dot, "nt": _dot_nt, "tn": _dot_tn}[mode]
    has_res = residual is not None

    def body(*refs):
        if has_res:
            a_ref, b_ref, r_ref, o_ref = refs[:4]
        else:
            a_ref, b_ref, o_ref = refs[:3]
            r_ref = None

        def finish(acc):
            if has_res:
                acc = acc + r_ref[...]
            o_ref[...] = acc.astype(o_ref.dtype)

        part = dot(a_ref[...], b_ref[...])
        if nk == 1:
            finish(part)
        else:
            acc_ref = refs[-1]
            kk = pl.program_id(2)

            @pl.when(kk == 0)
            def _():
                acc_ref[...] = part

            @pl.when(kk > 0)
            def _():
                acc_ref[...] += part

            @pl.when(kk == nk - 1)
            def _():
                finish(acc_ref[...])

    if mode == "tn":
        a_spec = pl.BlockSpec((tk, tm), lambda j, i, kk: (kk, i))
    else:
        a_spec = pl.BlockSpec((tm, tk), lambda j, i, kk: (i, kk))
    if mode == "nt":
        b_spec = pl.BlockSpec((tn, tk), lambda j, i, kk: (j, kk))
    else:
        b_spec = pl.BlockSpec((tk, tn), lambda j, i, kk: (kk, j))
    o_spec = pl.BlockSpec((tm, tn), lambda j, i, kk: (i, j))
    in_specs = [a_spec, b_spec] + ([o_spec] if has_res else [])
    args = (a, b) + ((residual,) if has_res else ())
    (out,), moved = _call(
        body,
        grid=(n // tn, m // tm, nk),
        in_specs=in_specs,
        out_specs=[o_spec],
        out_shape=[jax.ShapeDtypeStruct((m, n), out_dtype)],
        scratch_shapes=[pltpu.VMEM((tm, tn), F32)] if nk > 1 else [],
        semantics=("parallel", "parallel", "arbitrary"),
        name=name,
        args=args,
        carry=carry,
    )
    return out if carry is None else (out, moved)


def _rms_fwd(x, g, name):
    s, d = x.shape
    tm = _tile(s, ROW_TILE)

    def body(x_ref, g_ref, h_ref, r_ref):
        xf = x_ref[...]
        r = lax.rsqrt(jnp.mean(xf * xf, axis=-1, keepdims=True) + EPS)
        h_ref[...] = ((xf * r) * g_ref[...]).astype(BF16)
        r_ref[...] = r

    return pl.pallas_call(
        body,
        grid=(s // tm,),
        in_specs=[pl.BlockSpec((tm, d), lambda i: (i, 0)), pl.BlockSpec((1, d), lambda i: (0, 0))],
        out_specs=[pl.BlockSpec((tm, d), lambda i: (i, 0)), pl.BlockSpec((tm, 1), lambda i: (i, 0))],
        out_shape=[jax.ShapeDtypeStruct((s, d), BF16), jax.ShapeDtypeStruct((s, 1), F32)],
        compiler_params=_params("parallel"),
        name=name,
    )(x, g)


def _rms_bwd(dh, x, rstd, g, res, name):
    s, d = x.shape
    tm = _tile(s, ROW_TILE)
    has_res = res is not None

    def body(*refs):
        if has_res:
            dh_ref, x_ref, r_ref, g_ref, res_ref, dx_ref, dxb_ref, dg_ref = refs
        else:
            dh_ref, x_ref, r_ref, g_ref, dx_ref, dxb_ref, dg_ref = refs
        dhf = dh_ref[...].astype(F32)
        xhat = x_ref[...] * r_ref[...]
        dy = dhf * g_ref[...]
        dx = r_ref[...] * (dy - xhat * jnp.mean(dy * xhat, axis=-1, keepdims=True))
        if has_res:
            dx = dx + res_ref[...]
        dx_ref[...] = dx
        dxb_ref[...] = dx.astype(BF16)
        part = jnp.sum(dhf * xhat, axis=0, keepdims=True)

        @pl.when(pl.program_id(0) == 0)
        def _():
            dg_ref[...] = part

        @pl.when(pl.program_id(0) > 0)
        def _():
            dg_ref[...] += part

    row = pl.BlockSpec((tm, d), lambda i: (i, 0))
    vec = pl.BlockSpec((1, d), lambda i: (0, 0))
    in_specs = [row, row, pl.BlockSpec((tm, 1), lambda i: (i, 0)), vec] + ([row] if has_res else [])
    args = (dh, x, rstd, g) + ((res,) if has_res else ())
    return pl.pallas_call(
        body,
        grid=(s // tm,),
        in_specs=in_specs,
        out_specs=[row, row, vec],
        out_shape=[jax.ShapeDtypeStruct((s, d), F32), jax.ShapeDtypeStruct((s, d), BF16),
                   jax.ShapeDtypeStruct((1, d), F32)],
        compiler_params=_params("arbitrary"),
        name=name,
    )(*args)


def _headnorm_fwd(src, col0, nheads, g, name):
    s = src.shape[0]
    tm = _tile(s, HEADNORM_ROWS)
    w = nheads * HEAD_DIM
    assert col0 % nheads == 0

    def body(x_ref, g_ref, o_ref):
        for hh in range(nheads):
            xf = _head(x_ref, hh).astype(F32)
            r = lax.rsqrt(jnp.mean(xf * xf, axis=-1, keepdims=True) + EPS)
            o_ref[:, hh * HEAD_DIM:(hh + 1) * HEAD_DIM] = ((xf * r) * g_ref[...]).astype(BF16)

    return pl.pallas_call(
        body,
        grid=(s // tm,),
        in_specs=[pl.BlockSpec((tm, w), lambda i: (i, col0 // nheads)),
                  pl.BlockSpec((1, HEAD_DIM), lambda i: (0, 0))],
        out_specs=pl.BlockSpec((tm, w), lambda i: (i, 0)),
        out_shape=jax.ShapeDtypeStruct((s, w), BF16),
        compiler_params=_params("parallel"),
        name=name,
    )(src, g)


def _headnorm_bwd(dxn, src, col0, nheads, g, name):
    s = src.shape[0]
    tm = _tile(s, HEADNORM_ROWS)
    w = nheads * HEAD_DIM
    assert col0 % nheads == 0

    def body(d_ref, x_ref, g_ref, dx_ref, dg_ref):
        part = jnp.zeros((1, HEAD_DIM), F32)
        for hh in range(nheads):
            xf = _head(x_ref, hh).astype(F32)
            r = lax.rsqrt(jnp.mean(xf * xf, axis=-1, keepdims=True) + EPS)
            xhat = xf * r
            dn = _head(d_ref, hh).astype(F32)
            dy = dn * g_ref[...]
            dx = r * (dy - xhat * jnp.mean(dy * xhat, axis=-1, keepdims=True))
            dx_ref[:, hh * HEAD_DIM:(hh + 1) * HEAD_DIM] = dx.astype(BF16)
            part = part + jnp.sum(dn * xhat, axis=0, keepdims=True)

        @pl.when(pl.program_id(0) == 0)
        def _():
            dg_ref[...] = part

        @pl.when(pl.program_id(0) > 0)
        def _():
            dg_ref[...] += part

    return pl.pallas_call(
        body,
        grid=(s // tm,),
        in_specs=[pl.BlockSpec((tm, w), lambda i: (i, 0)),
                  pl.BlockSpec((tm, w), lambda i: (i, col0 // nheads)),
                  pl.BlockSpec((1, HEAD_DIM), lambda i: (0, 0))],
        out_specs=[pl.BlockSpec((tm, w), lambda i: (i, 0)),
                   pl.BlockSpec((1, HEAD_DIM), lambda i: (0, 0))],
        out_shape=[jax.ShapeDtypeStruct((s, w), BF16), jax.ShapeDtypeStruct((1, HEAD_DIM), F32)],
        compiler_params=_params("arbitrary"),
        name=name,
    )(dxn, src, g)


def _tri(t, lower_inclusive):
    r = lax.broadcasted_iota(jnp.int32, (t, t), 0)
    c = lax.broadcasted_iota(jnp.int32, (t, t), 1)
    keep = (c <= r) if lower_inclusive else (c >= r)
    return jnp.where(keep, 1.0, 0.0).astype(BF16)


def _forget_fwd(f_logit, b_pad):
    s = f_logit.shape[0]
    t = _tile(s, ATT_TILE)

    def body(f_ref, b_ref, c_ref, carry):
        @pl.when(pl.program_id(0) == 0)
        def _():
            carry[...] = jnp.zeros_like(carry)

        lf = _log_sigmoid(f_ref[...] + b_ref[...])
        tri = _tri(t, True)
        acc = carry[...]
        for part in _split3(lf):
            acc = acc + _dot(tri, part)
        c_ref[...] = acc
        carry[...] += jnp.sum(lf, axis=0, keepdims=True)

    return pl.pallas_call(
        body,
        grid=(s // t,),
        in_specs=[pl.BlockSpec((t, LANES), lambda i: (i, 0)), pl.BlockSpec((1, LANES), lambda i: (0, 0))],
        out_specs=pl.BlockSpec((t, LANES), lambda i: (i, 0)),
        out_shape=jax.ShapeDtypeStruct((s, LANES), F32),
        scratch_shapes=[pltpu.VMEM((1, LANES), F32)],
        compiler_params=_params("arbitrary"),
        name="forget_fwd",
    )(f_logit, b_pad)


def _forget_bwd(dc, f_logit, b_pad):
    s = f_logit.shape[0]
    t = _tile(s, ATT_TILE)
    nb = s // t

    def body(dc_ref, f_ref, b_ref, df_ref, db_ref, carry):
        @pl.when(pl.program_id(0) == 0)
        def _():
            carry[...] = jnp.zeros_like(carry)
            db_ref[...] = jnp.zeros_like(db_ref)

        d = dc_ref[...]
        tri = _tri(t, False)
        acc = carry[...]
        for part in _split3(d):
            acc = acc + _dot(tri, part)
        z = f_ref[...] + b_ref[...]
        df = acc * jnp.exp(_log_sigmoid(-z))
        df_ref[...] = df
        db_ref[...] += jnp.sum(df, axis=0, keepdims=True)
        carry[...] += jnp.sum(d, axis=0, keepdims=True)

    rev = pl.BlockSpec((t, LANES), lambda i: (nb - 1 - i, 0))
    vec = pl.BlockSpec((1, LANES), lambda i: (0, 0))
    return pl.pallas_call(
        body,
        grid=(nb,),
        in_specs=[rev, rev, vec],
        out_specs=[rev, vec],
        out_shape=[jax.ShapeDtypeStruct((s, LANES), F32), jax.ShapeDtypeStruct((1, LANES), F32)],
        scratch_shapes=[pltpu.VMEM((1, LANES), F32)],
        compiler_params=_params("arbitrary"),
        name="forget_bwd",
    )(dc, f_logit, b_pad)


SB_FWD_GROUP = 3
FOX_GROUP = 3
SB_BWD_GROUP = 3


def _head(ref, hh, rows=slice(None)):
    return ref[rows, hh * HEAD_DIM:(hh + 1) * HEAD_DIM]


def _tri_mask(t, strict):
    r = lax.broadcasted_iota(jnp.int32, (t, t), 0)
    c = lax.broadcasted_iota(jnp.int32, (t, t), 1)
    return (c < r) if strict else (c <= r)


def _fox_fwd(qn, kn, proj, colv, c_col, c_row, nheads, carry=None):
    s = qn.shape[0]
    t = _tile(s, ATT_TILE)
    scale = HEAD_DIM ** -0.5
    hg = FOX_GROUP
    gw = hg * HEAD_DIM
    assert nheads % hg == 0 and colv % hg == 0

    def body(q_ref, k_ref, v_ref, cc_ref, cr_ref, o_ref, of_ref, lse_ref):
        qi = pl.program_id(1)
        causal = _tri_mask(t, False)

        def tile(kj, carry, diagonal):
            off = pl.multiple_of(kj * t, t)
            heads = range(hg)
            rows = pl.ds(off, t)
            qk = [_dot_nt(_head(q_ref, hh), _head(k_ref, hh, rows)) for hh in heads]
            sc = [qk[hh] * scale + (cc_ref[hh] - cr_ref[hh, :, rows]) for hh in heads]
            if diagonal:
                sc = [jnp.where(causal, sc[hh], NEG_BIG) for hh in heads]
            m_new = [jnp.maximum(carry[hh][0], jnp.max(sc[hh], axis=-1, keepdims=True)) for hh in heads]
            p = [jnp.exp(sc[hh] - m_new[hh]) for hh in heads]
            pv = [_dot(p[hh].astype(BF16), _head(v_ref, hh, rows)) for hh in heads]
            out = []
            for hh in heads:
                m, l, acc = carry[hh]
                alpha = jnp.exp(m - m_new[hh])
                out.append((m_new[hh], alpha * l + jnp.sum(p[hh], axis=-1, keepdims=True), alpha * acc + pv[hh]))
            return tuple(out)

        init = tuple((jnp.full((t, 1), NEG_BIG, F32), jnp.zeros((t, 1), F32), jnp.zeros((t, HEAD_DIM), F32))
                     for _ in range(hg))
        carry = lax.fori_loop(0, qi, lambda kj, c: tile(kj, c, False), init)
        carry = tile(qi, carry, True)
        for hh in range(hg):
            m, l, acc = carry[hh]
            o = acc / l
            of_ref[:, hh * HEAD_DIM:(hh + 1) * HEAD_DIM] = o
            o_ref[:, hh * HEAD_DIM:(hh + 1) * HEAD_DIM] = o.astype(BF16)
            lse_ref[hh] = m + jnp.log(l)

    tile_spec = pl.BlockSpec((t, gw), lambda h, i: (i, h))
    w = nheads * HEAD_DIM
    return _call(
        body,
        grid=(nheads // hg, s // t),
        in_specs=[tile_spec,
                  pl.BlockSpec((s, gw), lambda h, i: (0, h)),
                  pl.BlockSpec((s, gw), lambda h, i: (0, colv // hg + h)),
                  pl.BlockSpec((hg, t, 1), lambda h, i: (h, i, 0)),
                  pl.BlockSpec((hg, 1, s), lambda h, i: (h, 0, 0))],
        out_specs=[tile_spec, tile_spec, pl.BlockSpec((hg, t, 1), lambda h, i: (h, i, 0))],
        out_shape=[jax.ShapeDtypeStruct((s, w), BF16), jax.ShapeDtypeStruct((s, w), F32),
                   jax.ShapeDtypeStruct((nheads, s, 1), F32)],
        scratch_shapes=[],
        semantics=("parallel", "parallel"),
        name="fox_fwd",
        args=(qn, kn, proj, c_col, c_row),
        carry=carry,
    )


def _fox_bwd(qn, kn, proj, colv, c_col, c_row, o, do, lse, nheads, carry=None):
    s = qn.shape[0]
    t = _tile(s, ATT_TILE)
    scale = HEAD_DIM ** -0.5
    hg = FOX_GROUP
    gw = hg * HEAD_DIM
    assert nheads % hg == 0 and colv % hg == 0

    def body(q_ref, k_ref, v_ref, cc_ref, cr_ref, o_ref, do_ref, lse_ref,
             dq_ref, dk_ref, dv_ref, drs_ref, dcs_ref):
        qi = pl.program_id(1)

        @pl.when(qi == 0)
        def _():
            dk_ref[...] = jnp.zeros_like(dk_ref)
            dv_ref[...] = jnp.zeros_like(dv_ref)
            dcs_ref[...] = jnp.zeros_like(dcs_ref)

        causal = _tri_mask(t, False)
        delta = [jnp.sum(_head(o_ref, hh) * _head(do_ref, hh).astype(F32), axis=-1, keepdims=True)
                 for hh in range(hg)]

        def tile(kj, carry, diagonal):
            off = pl.multiple_of(kj * t, t)
            heads = range(hg)
            rows = pl.ds(off, t)
            qk = [_dot_nt(_head(q_ref, hh), _head(k_ref, hh, rows)) for hh in heads]
            dp = [_dot_nt(_head(do_ref, hh), _head(v_ref, hh, rows)) for hh in heads]
            p = [jnp.exp(qk[hh] * scale + (cc_ref[hh] - cr_ref[hh, :, rows]) - lse_ref[hh]) for hh in heads]
            if diagonal:
                p = [jnp.where(causal, p[hh], 0.0) for hh in heads]
            ds = [p[hh] * (dp[hh] - delta[hh]) for hh in heads]
            dsb = [ds[hh].astype(BF16) for hh in heads]
            dv = [_dot_tn(p[hh].astype(BF16), _head(do_ref, hh)) for hh in heads]
            dk = [_dot_tn(dsb[hh], _head(q_ref, hh)) * scale for hh in heads]
            dq = [_dot(dsb[hh], _head(k_ref, hh, rows)) * scale for hh in heads]
            for hh in heads:
                cols = slice(hh * HEAD_DIM, (hh + 1) * HEAD_DIM)
                dv_ref[rows, cols] += dv[hh]
                dk_ref[rows, cols] += dk[hh]
                dcs_ref[hh, :, rows] += jnp.sum(ds[hh], axis=0, keepdims=True)
            return tuple((carry[hh][0] + dq[hh], carry[hh][1] + jnp.sum(ds[hh], axis=-1, keepdims=True))
                         for hh in heads)

        init = tuple((jnp.zeros((t, HEAD_DIM), F32), jnp.zeros((t, 1), F32)) for _ in range(hg))
        carry = lax.fori_loop(0, qi, lambda kj, c: tile(kj, c, False), init)
        carry = tile(qi, carry, True)
        for hh in range(hg):
            dq_ref[:, hh * HEAD_DIM:(hh + 1) * HEAD_DIM] = carry[hh][0]
            drs_ref[hh] = carry[hh][1]

    tile_spec = pl.BlockSpec((t, gw), lambda h, i: (i, h))
    full = pl.BlockSpec((s, gw), lambda h, i: (0, h))
    colspec = pl.BlockSpec((hg, t, 1), lambda h, i: (h, i, 0))
    rowspec = pl.BlockSpec((hg, 1, s), lambda h, i: (h, 0, 0))
    w = nheads * HEAD_DIM
    return _call(
        body,
        grid=(nheads // hg, s // t),
        in_specs=[tile_spec, full, pl.BlockSpec((s, gw), lambda h, i: (0, colv // hg + h)), colspec, rowspec,
                  tile_spec, tile_spec, colspec],
        out_specs=[tile_spec, full, full, colspec, rowspec],
        out_shape=[jax.ShapeDtypeStruct((s, w), F32), jax.ShapeDtypeStruct((s, w), F32),
                   jax.ShapeDtypeStruct((s, w), F32), jax.ShapeDtypeStruct((nheads, s, 1), F32),
                   jax.ShapeDtypeStruct((nheads, 1, s), F32)],
        scratch_shapes=[],
        semantics=("arbitrary", "arbitrary"),
        name="fox_bwd",
        args=(qn, kn, proj, c_col, c_row, o, do, lse),
        carry=carry,
    )


def _sb_tile(q, k, scale, later, valid):
    z = _dot_nt(q, k) * scale
    lb = _log_sigmoid(z)
    lm = lb - z
    if valid is not None:
        lm = jnp.where(valid, lm, 0.0)
    suffix = _dot(jnp.concatenate(_split2(lm), axis=1), later)
    return lb, lm, suffix


def _later(t):
    r = lax.broadcasted_iota(jnp.int32, (2 * t, t), 0) % t
    c = lax.broadcasted_iota(jnp.int32, (2 * t, t), 1)
    return jnp.where(r > c, 1.0, 0.0).astype(BF16)


def _sb_fwd(proj, colq, colk, colv, nheads):
    s = proj.shape[0]
    t = _tile(s, ATT_TILE)
    scale = HEAD_DIM ** -0.5
    hg = SB_FWD_GROUP
    gw = hg * HEAD_DIM
    assert nheads % hg == 0 and colq % hg == 0 and colk % hg == 0 and colv % hg == 0

    def body(q_ref, k_ref, v_ref, o_ref):
        qi = pl.program_id(1)
        later = _later(t)
        before = _tri_mask(t, True)

        def tile(kj, carry, diagonal):
            off = pl.multiple_of(kj * t, t)
            heads = range(hg)
            z = [_dot_nt(_head(q_ref, hh), _head(k_ref, hh, pl.ds(off, t))) * scale for hh in heads]
            lb = [_log_sigmoid(z[hh]) for hh in heads]
            lm = [lb[hh] - z[hh] for hh in heads]
            if diagonal:
                lm = [jnp.where(before, lm[hh], 0.0) for hh in heads]
            parts = [jnp.concatenate(_split2(lm[hh]), axis=1) for hh in heads]
            suffix = [_dot(parts[hh], later) for hh in heads]
            a = [jnp.exp(lb[hh] + suffix[hh] + carry[hh][0]) for hh in heads]
            if diagonal:
                a = [jnp.where(before, a[hh], 0.0) for hh in heads]
            av = [_dot(a[hh].astype(BF16), _head(v_ref, hh, pl.ds(off, t))) for hh in heads]
            return tuple((carry[hh][0] + jnp.sum(lm[hh], axis=-1, keepdims=True), carry[hh][1] + av[hh])
                         for hh in heads)

        init = tuple((jnp.zeros((t, 1), F32), jnp.zeros((t, HEAD_DIM), F32)) for _ in range(hg))
        carry = tile(qi, init, True)
        carry = lax.fori_loop(1, qi + 1, lambda i, c: tile(qi - i, c, False), carry)
        for hh in range(hg):
            o_ref[:, hh * HEAD_DIM:(hh + 1) * HEAD_DIM] = carry[hh][1].astype(BF16)

    return pl.pallas_call(
        body,
        grid=(nheads // hg, s // t),
        in_specs=[pl.BlockSpec((t, gw), lambda h, i: (i, colq // hg + h)),
                  pl.BlockSpec((s, gw), lambda h, i: (0, colk // hg + h)),
                  pl.BlockSpec((s, gw), lambda h, i: (0, colv // hg + h))],
        out_specs=pl.BlockSpec((t, gw), lambda h, i: (i, h)),
        out_shape=jax.ShapeDtypeStruct((s, nheads * HEAD_DIM), BF16),
        compiler_params=_params("parallel", "parallel"),
        name="sb_fwd",
    )(proj, proj, proj)


def _sb_bwd(proj, colq, colk, colv, do, nheads, carry=None):
    s = proj.shape[0]
    t = _tile(s, ATT_TILE)
    scale = HEAD_DIM ** -0.5
    hg = SB_BWD_GROUP
    gw = hg * HEAD_DIM
    assert nheads % hg == 0 and colq % hg == 0 and colk % hg == 0 and colv % hg == 0

    def body(q_ref, k_ref, v_ref, do_ref, dq_ref, dk_ref, dv_ref, g_s, beta_s):
        qi = pl.program_id(1)

        @pl.when(qi == 0)
        def _():
            dk_ref[...] = jnp.zeros_like(dk_ref)
            dv_ref[...] = jnp.zeros_like(dv_ref)

        later = _later(t)
        before = _tri_mask(t, True)

        def back(kj, carry, diagonal):
            off = pl.multiple_of(kj * t, t)
            heads = range(hg)
            rows = pl.ds(off, t)
            z = [_dot_nt(_head(q_ref, hh), _head(k_ref, hh, rows)) * scale for hh in heads]
            da = [_dot_nt(_head(do_ref, hh), _head(v_ref, hh, rows)) for hh in heads]
            lb = [_log_sigmoid(z[hh]) for hh in heads]
            lm = [lb[hh] - z[hh] for hh in heads]
            if diagonal:
                lm = [jnp.where(before, lm[hh], 0.0) for hh in heads]
            parts = [jnp.concatenate(_split2(lm[hh]), axis=1) for hh in heads]
            suffix = [_dot(parts[hh], later) for hh in heads]
            a = [jnp.exp(lb[hh] + suffix[hh] + carry[hh]) for hh in heads]
            if diagonal:
                a = [jnp.where(before, a[hh], 0.0) for hh in heads]
            dv = [_dot_tn(a[hh].astype(BF16), _head(do_ref, hh)) for hh in heads]
            for hh in heads:
                g_s[hh, :, rows] = a[hh] * da[hh]
                beta_s[hh, :, rows] = jnp.exp(lb[hh]).astype(BF16)
            for hh in heads:
                dv_ref[rows, hh * HEAD_DIM:(hh + 1) * HEAD_DIM] += dv[hh]
            return tuple(carry[hh] + jnp.sum(lm[hh], axis=-1, keepdims=True) for hh in heads)

        rc = back(qi, tuple(jnp.zeros((t, 1), F32) for _ in range(hg)), True)
        lax.fori_loop(1, qi + 1, lambda i, c: back(qi - i, c, False), rc)

        earlier = jnp.where(lax.broadcasted_iota(jnp.int32, (2 * t, t), 0) % t
                            < lax.broadcasted_iota(jnp.int32, (2 * t, t), 1), 1.0, 0.0).astype(BF16)

        def fwd(kj, carry, diagonal):
            off = pl.multiple_of(kj * t, t)
            heads = range(hg)
            rows = pl.ds(off, t)
            g = [g_s[hh, :, rows] for hh in heads]
            parts = [jnp.concatenate(_split2(g[hh]), axis=1) for hh in heads]
            gsum = [_dot(parts[hh], earlier) + carry[hh][0] for hh in heads]
            dz = []
            for hh in heads:
                beta = beta_s[hh, :, rows].astype(F32)
                d = g[hh] * (1.0 - beta) - gsum[hh] * beta
                if diagonal:
                    d = jnp.where(before, d, 0.0)
                dz.append(d.astype(BF16))
            dk = [_dot_tn(dz[hh], _head(q_ref, hh)) * scale for hh in heads]
            dq = [_dot(dz[hh], _head(k_ref, hh, rows)) * scale for hh in heads]
            for hh in heads:
                dk_ref[rows, hh * HEAD_DIM:(hh + 1) * HEAD_DIM] += dk[hh]
            return tuple((carry[hh][0] + jnp.sum(g[hh], axis=-1, keepdims=True), carry[hh][1] + dq[hh])
                         for hh in heads)

        init = tuple((jnp.zeros((t, 1), F32), jnp.zeros((t, HEAD_DIM), F32)) for _ in range(hg))
        carry = lax.fori_loop(0, qi, lambda kj, c: fwd(kj, c, False), init)
        carry = fwd(qi, carry, True)
        for hh in range(hg):
            dq_ref[:, hh * HEAD_DIM:(hh + 1) * HEAD_DIM] = carry[hh][1]

    once = pl.Buffered(buffer_count=1)
    tile_spec = pl.BlockSpec((t, gw), lambda h, i: (i, h))
    full = pl.BlockSpec((s, gw), lambda h, i: (0, h), pipeline_mode=once)
    w = nheads * HEAD_DIM
    return _call(
        body,
        grid=(nheads // hg, s // t),
        in_specs=[pl.BlockSpec((t, gw), lambda h, i: (i, colq // hg + h)),
                  pl.BlockSpec((s, gw), lambda h, i: (0, colk // hg + h), pipeline_mode=once),
                  pl.BlockSpec((s, gw), lambda h, i: (0, colv // hg + h), pipeline_mode=once),
                  tile_spec],
        out_specs=[tile_spec, full, full],
        out_shape=[jax.ShapeDtypeStruct((s, w), F32)] * 3,
        scratch_shapes=[pltpu.VMEM((hg, t, s), F32), pltpu.VMEM((hg, t, s), BF16)],
        semantics=("arbitrary", "arbitrary"),
        name="sb_bwd",
        args=(proj, proj, proj, do),
        carry=carry,
    )


def _mem_fwd(qn, kn, mkv, nheads):
    s = qn.shape[0]
    mtok = kn.shape[0]
    t = _tile(s, ATT_TILE)
    scale = HEAD_DIM ** -0.5

    def body(q_ref, k_ref, v_ref, o_ref):
        sc = _dot_nt(q_ref[...], k_ref[...]) * scale
        p = jnp.exp(sc - jnp.max(sc, axis=-1, keepdims=True))
        p = p / jnp.sum(p, axis=-1, keepdims=True)
        o_ref[...] = _dot(p.astype(BF16), v_ref[...]).astype(BF16)

    return pl.pallas_call(
        body,
        grid=(nheads, s // t),
        in_specs=[pl.BlockSpec((t, HEAD_DIM), lambda h, i: (i, h)),
                  pl.BlockSpec((mtok, HEAD_DIM), lambda h, i: (0, h)),
                  pl.BlockSpec((mtok, HEAD_DIM), lambda h, i: (0, nheads + h))],
        out_specs=pl.BlockSpec((t, HEAD_DIM), lambda h, i: (i, h)),
        out_shape=jax.ShapeDtypeStruct((s, nheads * HEAD_DIM), BF16),
        compiler_params=_params("parallel", "parallel"),
        name="mem_fwd",
    )(qn, kn, mkv)


def _mem_bwd(qn, kn, mkv, do, nheads):
    s = qn.shape[0]
    mtok = kn.shape[0]
    t = _tile(s, ATT_TILE)
    scale = HEAD_DIM ** -0.5

    def body(q_ref, k_ref, v_ref, do_ref, dq_ref, dk_ref, dv_ref):
        @pl.when(pl.program_id(1) == 0)
        def _():
            dk_ref[...] = jnp.zeros_like(dk_ref)
            dv_ref[...] = jnp.zeros_like(dv_ref)

        q = q_ref[...]
        k = k_ref[...]
        do_ = do_ref[...]
        sc = _dot_nt(q, k) * scale
        p = jnp.exp(sc - jnp.max(sc, axis=-1, keepdims=True))
        p = p / jnp.sum(p, axis=-1, keepdims=True)
        dp = _dot_nt(do_, v_ref[...])
        ds = (p * (dp - jnp.sum(p * dp, axis=-1, keepdims=True))).astype(BF16)
        dq_ref[...] = _dot(ds, k) * scale
        dk_ref[...] += _dot_tn(ds, q) * scale
        dv_ref[...] += _dot_tn(p.astype(BF16), do_)

    tile = pl.BlockSpec((t, HEAD_DIM), lambda h, i: (i, h))
    kspec = pl.BlockSpec((mtok, HEAD_DIM), lambda h, i: (0, h))
    w = nheads * HEAD_DIM
    return pl.pallas_call(
        body,
        grid=(nheads, s // t),
        in_specs=[tile, kspec, pl.BlockSpec((mtok, HEAD_DIM), lambda h, i: (0, nheads + h)), tile],
        out_specs=[tile, kspec, kspec],
        out_shape=[jax.ShapeDtypeStruct((s, w), F32), jax.ShapeDtypeStruct((mtok, w), F32),
                   jax.ShapeDtypeStruct((mtok, w), F32)],
        compiler_params=_params("arbitrary", "arbitrary"),
        name="mem_bwd",
    )(qn, kn, mkv, do)


def _merge_fwd(p0, p1, p2, proj, colg, b_gate):
    s, d = p0.shape
    tm, tn = _tile(s, ROW_TILE), _tile(d, COL_TILE)
    g0 = colg * LANES // tn
    nj = d // tn

    def body(p0_ref, p1_ref, p2_ref, ga_ref, gb_ref, gc_ref, b_ref, o_ref):
        acc = jnp.zeros((tm, tn), F32)
        for b, (p_ref, g_ref) in enumerate(((p0_ref, ga_ref), (p1_ref, gb_ref), (p2_ref, gc_ref))):
            gate = jax.nn.sigmoid(g_ref[...].astype(F32) + b_ref[b:b + 1, :])
            acc = acc + gate * p_ref[...]
        o_ref[...] = acc.astype(BF16)

    blk = pl.BlockSpec((tm, tn), lambda i, j: (i, j))
    gates = [pl.BlockSpec((tm, tn), functools.partial(lambda i, j, b: (i, g0 + b * nj + j), b=b)) for b in range(3)]
    return pl.pallas_call(
        body,
        grid=(s // tm, nj),
        in_specs=[blk, blk, blk] + gates + [pl.BlockSpec((3, tn), lambda i, j: (0, j))],
        out_specs=blk,
        out_shape=jax.ShapeDtypeStruct((s, d), BF16),
        compiler_params=_params("parallel", "parallel"),
        name="merge_fwd",
    )(p0, p1, p2, proj, proj, proj, b_gate)


def _merge_bwd(dmerged, p0, p1, p2, proj, colg, b_gate):
    s, d = p0.shape
    tm, tn = _tile(s, ROW_TILE), _tile(d, COL_TILE)
    g0 = colg * LANES // tn
    nj = d // tn

    def body(dm_ref, p0_ref, p1_ref, p2_ref, ga_ref, gb_ref, gc_ref, b_ref,
             d0_ref, d1_ref, d2_ref, dga_ref, dgb_ref, dgc_ref, db_ref):
        dm = dm_ref[...].astype(F32)
        parts = []
        for b, (p_ref, g_ref, dp_ref, dg_ref) in enumerate(((p0_ref, ga_ref, d0_ref, dga_ref),
                                                            (p1_ref, gb_ref, d1_ref, dgb_ref),
                                                            (p2_ref, gc_ref, d2_ref, dgc_ref))):
            gate = jax.nn.sigmoid(g_ref[...].astype(F32) + b_ref[b:b + 1, :])
            dp_ref[...] = (dm * gate).astype(BF16)
            dgate = dm * p_ref[...] * gate * (1.0 - gate)
            dg_ref[...] = dgate.astype(BF16)
            parts.append(jnp.sum(dgate, axis=0, keepdims=True))
        part = jnp.concatenate(parts, axis=0)

        @pl.when(pl.program_id(1) == 0)
        def _():
            db_ref[...] = part

        @pl.when(pl.program_id(1) > 0)
        def _():
            db_ref[...] += part

    blk = pl.BlockSpec((tm, tn), lambda j, i: (i, j))
    gates = [pl.BlockSpec((tm, tn), functools.partial(lambda j, i, b: (i, g0 + b * nj + j), b=b)) for b in range(3)]
    bias = pl.BlockSpec((3, tn), lambda j, i: (0, j))
    return pl.pallas_call(
        body,
        grid=(nj, s // tm),
        in_specs=[blk, blk, blk, blk] + gates + [bias],
        out_specs=[blk] * 6 + [bias],
        out_shape=[jax.ShapeDtypeStruct((s, d), BF16)] * 6 + [jax.ShapeDtypeStruct((3, d), F32)],
        compiler_params=_params("parallel", "arbitrary"),
        name="merge_bwd",
    )(dmerged, p0, p1, p2, proj, proj, proj, b_gate)


def _shift_down(v, n):
    rows = lax.broadcasted_iota(jnp.int32, v.shape, 0)
    return jnp.where(rows >= n, pltpu.roll(v, n, 0), 0.0)


def _shift_up(v, n):
    s = v.shape[0]
    rows = lax.broadcasted_iota(jnp.int32, v.shape, 0)
    return jnp.where(rows < s - n, pltpu.roll(v, s - n, 0), 0.0)


def _conv(v, w_ref, b_ref):
    taps = w_ref.shape[0]
    out = v * w_ref[taps - 1:taps, :] + b_ref[...]
    for n in range(1, taps):
        out = out + _shift_down(v, n) * w_ref[taps - 1 - n:taps - n, :]
    return out


def _conv_act_fwd(up, conv_w, conv_b):
    s, f2 = up.shape
    f = f2 // 2
    tn = LANES
    nj = f // tn
    taps = conv_w.shape[0]

    def body(ug_ref, uv_ref, wg_ref, wv_ref, bg_ref, bv_ref, o_ref):
        cg = _conv(ug_ref[...].astype(F32), wg_ref, bg_ref)
        cv = _conv(uv_ref[...].astype(F32), wv_ref, bv_ref)
        o_ref[...] = (cg * jax.nn.sigmoid(cg) * cv).astype(BF16)

    return pl.pallas_call(
        body,
        grid=(nj,),
        in_specs=[pl.BlockSpec((s, tn), lambda j: (0, j)), pl.BlockSpec((s, tn), lambda j: (0, nj + j)),
                  pl.BlockSpec((taps, tn), lambda j: (0, j)), pl.BlockSpec((taps, tn), lambda j: (0, nj + j)),
                  pl.BlockSpec((1, tn), lambda j: (0, j)), pl.BlockSpec((1, tn), lambda j: (0, nj + j))],
        out_specs=pl.BlockSpec((s, tn), lambda j: (0, j)),
        out_shape=jax.ShapeDtypeStruct((s, f), BF16),
        compiler_params=_params("parallel"),
        name="conv_act_fwd",
    )(up, up, conv_w, conv_w, conv_b, conv_b)


def _conv_act_bwd(up, conv_w, conv_b, dact):
    s, f2 = up.shape
    f = f2 // 2
    tn = LANES
    nj = f // tn
    taps = conv_w.shape[0]

    def half(v, du, w_ref, dup_ref, dw_ref, db_ref):
        dup = du * w_ref[taps - 1:taps, :]
        rows = [None] * taps
        rows[taps - 1] = jnp.sum(du * v, axis=0, keepdims=True)
        for n in range(1, taps):
            dup = dup + _shift_up(du, n) * w_ref[taps - 1 - n:taps - n, :]
            rows[taps - 1 - n] = jnp.sum(du * _shift_down(v, n), axis=0, keepdims=True)
        dup_ref[...] = dup.astype(BF16)
        dw_ref[...] = jnp.concatenate(rows, axis=0)
        db_ref[...] = jnp.sum(du, axis=0, keepdims=True)

    def body(ug_ref, uv_ref, wg_ref, wv_ref, bg_ref, bv_ref, da_ref,
             dug_ref, duv_ref, dwg_ref, dwv_ref, dbg_ref, dbv_ref):
        ug = ug_ref[...].astype(F32)
        uv = uv_ref[...].astype(F32)
        cg = _conv(ug, wg_ref, bg_ref)
        cv = _conv(uv, wv_ref, bv_ref)
        da = da_ref[...].astype(F32)
        sg = jax.nn.sigmoid(cg)
        dcv = da * cg * sg
        dcg = da * cv * (sg + cg * sg * (1.0 - sg))
        half(ug, dcg, wg_ref, dug_ref, dwg_ref, dbg_ref)
        half(uv, dcv, wv_ref, duv_ref, dwv_ref, dbv_ref)

    lo = lambda rows: pl.BlockSpec((rows, tn), lambda j: (0, j))
    hi = lambda rows: pl.BlockSpec((rows, tn), lambda j: (0, nj + j))
    return pl.pallas_call(
        body,
        grid=(nj,),
        in_specs=[lo(s), hi(s), lo(taps), hi(taps), lo(1), hi(1), lo(s)],
        out_specs=[lo(s), lo(s), lo(taps), lo(taps), lo(1), lo(1)],
        out_shape=[jax.ShapeDtypeStruct((s, f), BF16)] * 2 + [jax.ShapeDtypeStruct((taps, f), F32)] * 2
        + [jax.ShapeDtypeStruct((1, f), F32)] * 2,
        compiler_params=_params("parallel"),
        name="conv_act_bwd",
    )(up, up, conv_w, conv_w, conv_b, conv_b, dact)


def _loss_grad(y, target):
    s, d = y.shape
    tm = _tile(s, ROW_TILE)

    def body(y_ref, t_ref, dy_ref, dyb_ref, l_ref):
        e = y_ref[...] - t_ref[...]
        dy = e * (1.0 / d)
        dy_ref[...] = dy
        dyb_ref[...] = dy.astype(BF16)
        tot = jnp.sum(jnp.sum(e * e, axis=-1, keepdims=True), axis=0, keepdims=True)
        l_ref[...] = jnp.broadcast_to(tot, (8, LANES))

    row = pl.BlockSpec((tm, d), lambda i: (i, 0))
    return pl.pallas_call(
        body,
        grid=(s // tm,),
        in_specs=[row, row],
        out_specs=[row, row, pl.BlockSpec((8, LANES), lambda i: (i, 0))],
        out_shape=[jax.ShapeDtypeStruct((s, d), F32), jax.ShapeDtypeStruct((s, d), BF16),
                   jax.ShapeDtypeStruct((s // tm * 8, LANES), F32)],
        compiler_params=_params("parallel"),
        name="loss_grad",
    )(y, target)


def _row_tile(rows, row_bytes, budget):
    if rows * row_bytes <= budget or rows % 8:
        return rows
    best = 8
    for t in range(8, rows, 8):
        if rows % t == 0 and t * row_bytes <= budget:
            best = t
    return best


def _adamw(w, g, m, v, name):
    r, c = w.shape
    tr = _row_tile(r, c * 4, ADAM_BLOCK_BYTES)

    def body(w_ref, g_ref, m_ref, v_ref, d_ref, mo_ref, vo_ref):
        gg = g_ref[...]
        m_new = ADAM_B1 * m_ref[...] + (1.0 - ADAM_B1) * gg
        v_new = ADAM_B2 * v_ref[...] + (1.0 - ADAM_B2) * (gg * gg)
        m_hat = m_new / (1.0 - ADAM_B1 ** ADAM_STEP)
        v_hat = v_new / (1.0 - ADAM_B2 ** ADAM_STEP)
        d_ref[...] = -ADAM_LR * (m_hat / (jnp.sqrt(v_hat) + ADAM_EPS) + ADAM_WD * w_ref[...])
        mo_ref[...] = m_new
        vo_ref[...] = v_new

    blk = pl.BlockSpec((tr, c), lambda i: (i, 0))
    return pl.pallas_call(
        body,
        grid=(r // tr,),
        in_specs=[blk] * 4,
        out_specs=[blk] * 3,
        out_shape=[jax.ShapeDtypeStruct((r, c), F32)] * 3,
        compiler_params=_params("parallel"),
        name=name,
    )(w, g, m, v)


def _add_sibling(g, r1, core, name):
    _, _, h, c = g.shape
    th = _row_tile(h, c * 2, ADAM_BLOCK_BYTES)

    def body(core_ref, g_ref, r_ref, o_ref):
        o_ref[...] = (g_ref[...].astype(F32) + r_ref[...].astype(F32)).astype(BF16)

    return pl.pallas_call(
        body,
        grid_spec=pltpu.PrefetchScalarGridSpec(
            num_scalar_prefetch=1,
            grid=(N_CHIPS, h // th),
            in_specs=[pl.BlockSpec((None, None, th, c), lambda j, i, core_ref: (j, core_ref[0], i, 0)),
                      pl.BlockSpec((None, th, c), lambda j, i, core_ref: (j, i, 0))],
            out_specs=pl.BlockSpec((None, th, c), lambda j, i, core_ref: (j, i, 0)),
        ),
        out_shape=jax.ShapeDtypeStruct((N_CHIPS, h, c), BF16),
        compiler_params=_params("parallel", "parallel"),
        name=name,
    )(core, g, r1)


def _add_chips(hsum, r2, chip_core, name):
    _, h, c = hsum.shape
    th = _row_tile(h, c * 4, ADAM_BLOCK_BYTES)

    def body(sel_ref, own_ref, r_ref, o_ref):
        acc = own_ref[...].astype(F32)
        for j in range(N_CHIPS - 1):
            acc = acc + r_ref[j].astype(F32)
        o_ref[...] = acc

    return pl.pallas_call(
        body,
        grid_spec=pltpu.PrefetchScalarGridSpec(
            num_scalar_prefetch=1,
            grid=(h // th,),
            in_specs=[pl.BlockSpec((None, th, c), lambda i, sel_ref: (sel_ref[0], i, 0)),
                      pl.BlockSpec((N_CHIPS - 1, th, c), lambda i, sel_ref: (0, i, 0))],
            out_specs=pl.BlockSpec((None, th, c), lambda i, sel_ref: (sel_ref[1], i, 0)),
        ),
        out_shape=jax.ShapeDtypeStruct((2, h, c), F32),
        compiler_params=_params("parallel"),
        name=name,
    )(chip_core, hsum, r2)


def _sum_devices(parts):
    _, r, c = parts.shape

    def body(p_ref, o_ref):
        acc = p_ref[0]
        for j in range(1, N_DEV):
            acc = acc + p_ref[j]
        o_ref[...] = acc

    return pl.pallas_call(
        body,
        out_shape=jax.ShapeDtypeStruct((r, c), F32),
        compiler_params=pltpu.CompilerParams(vmem_limit_bytes=VMEM_LIMIT_BYTES),
        name="sum_devices",
    )(parts)


def _swap_halves(grads, name):
    n = len(grads)

    def body(*refs):
        ins, outs = refs[:n], refs[n:2 * n]
        send_sems, recv_sems = refs[2 * n:]
        x, y, c, _ = _place()
        copies = [_remote(ins[i].at[:, 1 - c], outs[i], send_sems.at[i], recv_sems.at[i], (x, y, 1 - c))
                  for i in range(n)]
        for cp in copies:
            cp.start()
        for cp in copies:
            cp.wait()

    return pl.pallas_call(
        body,
        in_specs=[ANY] * n,
        out_specs=[ANY] * n,
        out_shape=[jax.ShapeDtypeStruct((g.shape[0],) + g.shape[2:], g.dtype) for g in grads],
        scratch_shapes=[pltpu.SemaphoreType.DMA((n,)), pltpu.SemaphoreType.DMA((n,))],
        name=name,
    )(*grads)


def _join_halves(finals):
    n = len(finals)

    def body(*refs):
        outs = refs[n:2 * n]
        send_sems, recv_sems = refs[2 * n:]
        x, y, c, _ = _place()
        sends = [_remote(outs[i].at[c], outs[i].at[c], send_sems.at[i], recv_sems.at[i], (x, y, 1 - c))
                 for i in range(n)]
        for cp in sends:
            cp.start()
        for i in range(n):
            sends[i].wait_send()
            other = outs[i].at[1 - c]
            _remote(other, other, send_sems.at[i], recv_sems.at[i], (x, y, 1 - c)).wait_recv()

    return pl.pallas_call(
        body,
        in_specs=[ANY] * n,
        out_specs=[ANY] * n,
        out_shape=[jax.ShapeDtypeStruct(f.shape, f.dtype) for f in finals],
        input_output_aliases={i: i for i in range(n)},
        scratch_shapes=[pltpu.SemaphoreType.DMA((n,)), pltpu.SemaphoreType.DMA((n,))],
        name="join_halves",
    )(*finals)


def _gather_small(vec):
    k = N_DEV - 1

    def body(v_ref, o_ref, send_sems, recv_sems, local_sem):
        x, y, c, _ = _place()
        me = 4 * x + 2 * y + c
        local = pltpu.make_async_copy(v_ref, o_ref.at[me], local_sem)
        local.start()
        peers = [(x ^ (r >> 2 & 1), y ^ (r >> 1 & 1), c ^ (r & 1)) for r in range(1, N_DEV)]
        sends = [_remote(v_ref, o_ref.at[me], send_sems.at[j], recv_sems.at[j], p) for j, p in enumerate(peers)]
        for cp in sends:
            cp.start()
        for j, (px, py, pc) in enumerate(peers):
            sends[j].wait_send()
            blk = o_ref.at[4 * px + 2 * py + pc]
            _remote(blk, blk, send_sems.at[j], recv_sems.at[j], (px, py, pc)).wait_recv()
        local.wait()

    return pl.pallas_call(
        body,
        in_specs=[ANY],
        out_specs=ANY,
        out_shape=jax.ShapeDtypeStruct((N_DEV,) + vec.shape, vec.dtype),
        scratch_shapes=[pltpu.SemaphoreType.DMA((k,)), pltpu.SemaphoreType.DMA((k,)), pltpu.SemaphoreType.DMA(())],
        name="gather_small",
    )(vec)


ROW_SHARDED = ("w_in_main", "w_in_f", "w_mem_kv", "w_out", "w_down")
COL_SHARDED = ("w_br_fox", "w_br_sb", "w_br_mem", "w_up")
BIG = ROW_SHARDED + COL_SHARDED


def _whole(name, gathered, shard, chip):
    a = lax.dynamic_update_slice(gathered, shard[None], (chip, 0, 0))
    if name in ROW_SHARDED:
        return a.reshape(N_CHIPS * a.shape[1], a.shape[2])
    return a.transpose(1, 0, 2).reshape(a.shape[1], N_CHIPS * a.shape[2])


def _by_shard(name, grad):
    if name in ROW_SHARDED:
        a = grad.reshape(N_CHIPS, grad.shape[0] // N_CHIPS, grad.shape[1])
    else:
        a = grad.reshape(grad.shape[0], N_CHIPS, grad.shape[1] // N_CHIPS).transpose(1, 0, 2)
    return a.reshape(N_CHIPS, 2, a.shape[1] // 2, a.shape[2])


def _sibling_sums(names, g, core, tag):
    split = [_by_shard(name, g[name]) for name in names]
    theirs = _swap_halves(split, "swap_halves_" + tag)
    return [_add_sibling(a, r, core, "add_sibling_" + name) for name, a, r in zip(names, split, theirs)]


GATHER_FIRST = ("w_in_main", "w_in_f", "w_mem_kv")
GATHER_LATE = ("w_down", "w_out", "w_br_fox", "w_br_sb", "w_br_mem")
REDUCE_FFN = ("w_down", "w_up")
REDUCE_MIX = ("w_out", "w_br_fox", "w_br_sb", "w_br_mem")
REDUCE_IN = ("w_in_main", "w_in_f", "w_mem_kv")


def _local_step(x, mem, target, w, shard, chip, core, chip_core):
    d = x.shape[1]
    nf = shard["w_br_fox"].shape[0] // HEAD_DIM
    nsb = shard["w_br_sb"].shape[0] // HEAD_DIM
    nm = shard["w_br_mem"].shape[0] // HEAD_DIM
    w = dict(w)

    def take(names, gathered):
        for name, a in zip(names, gathered):
            w[name] = _whole(name, a, shard[name], chip)

    take(GATHER_FIRST, _exchange(_Gather([shard[name] for name in GATHER_FIRST]), "gather_first"))
    fq, fk, fv = 0, nf, 2 * nf
    sq, sk, sv = 3 * nf, 3 * nf + nsb, 3 * nf + 2 * nsb
    mq = 3 * nf + 3 * nsb
    gates = mq + nm

    h, rstd1 = _rms_fwd(x, w["g_mix"], "rms_mix_fwd")
    proj, moved = _mm(h, w["w_in_main"], "nn", BF16, "proj_main", carry=_Gather([shard["w_up"]]))
    take(("w_up",), moved)
    f_logit = _mm(h, w["w_in_f"], "nn", F32, "proj_forget")
    c_sum = _forget_fwd(f_logit, w["b_forget"])
    c_t = c_sum[:, :nf].T
    c_col, c_row = c_t[:, :, None], c_t[:, None, :]
    qn = _headnorm_fwd(proj, fq, nf, w["g_q_fox"], "fox_qnorm_fwd")
    kn = _headnorm_fwd(proj, fk, nf, w["g_k_fox"], "fox_knorm_fwd")
    (o_fox, o_fox32, lse), moved = _fox_fwd(qn, kn, proj, fv, c_col, c_row, nf,
                                            carry=_Gather([shard[name] for name in GATHER_LATE]))
    take(GATHER_LATE, moved)
    o_sb = _sb_fwd(proj, sq, sk, sv, nsb)
    memn, rstd_m = _rms_fwd(mem, w["g_mem"], "rms_mem_fwd")
    mkv = _mm(memn, w["w_mem_kv"], "nn", BF16, "mem_kv")
    kmn = _headnorm_fwd(mkv, 0, nm, w["g_k_mem"], "mem_knorm_fwd")
    qmn = _headnorm_fwd(proj, mq, nm, w["g_q_mem"], "mem_qnorm_fwd")
    o_mem = _mem_fwd(qmn, kmn, mkv, nm)
    p0 = _mm(o_fox, w["w_br_fox"], "nn", F32, "branch_fox")
    p1 = _mm(o_sb, w["w_br_sb"], "nn", F32, "branch_sb")
    p2 = _mm(o_mem, w["w_br_mem"], "nn", F32, "branch_mem")
    merged = _merge_fwd(p0, p1, p2, proj, gates, w["b_gate"])
    x1 = _mm(merged, w["w_out"], "nn", F32, "out_proj", residual=x)
    h2, rstd2 = _rms_fwd(x1, w["g_ffn"], "rms_ffn_fwd")
    up = _mm(h2, w["w_up"], "nn", BF16, "ffn_up")
    act = _conv_act_fwd(up, w["conv_w"], w["conv_b"])
    y = _mm(act, w["w_down"], "nn", F32, "ffn_down", residual=x1)
    dy, dyb, lparts = _loss_grad(y, target)
    loss = (0.5 / d) * jnp.sum(lparts[::8, 0])

    g = {}
    dact = _mm(dyb, w["w_down"], "nt", BF16, "ffn_down_dx")
    g["w_down"] = _mm(act, dyb, "tn", BF16, "ffn_down_dw")
    dug, duv, dwg, dwv, dbg, dbv = _conv_act_bwd(up, w["conv_w"], w["conv_b"], dact)
    dup = jnp.concatenate([dug, duv], axis=1)
    g["conv_w"] = jnp.concatenate([dwg, dwv], axis=1)
    g["conv_b"] = jnp.concatenate([dbg, dbv], axis=1)
    dh2 = _mm(dup, w["w_up"], "nt", BF16, "ffn_up_dx")
    g["w_up"] = _mm(h2, dup, "tn", BF16, "ffn_up_dw")
    sums_ffn = _sibling_sums(REDUCE_FFN, g, core, "ffn")
    dx1, dx1b, g["g_ffn"] = _rms_bwd(dh2, x1, rstd2, w["g_ffn"], dy, "rms_ffn_bwd")
    dmerged = _mm(dx1b, w["w_out"], "nt", BF16, "out_proj_dx")
    g["w_out"] = _mm(merged, dx1b, "tn", BF16, "out_proj_dw")
    dp0, dp1, dp2, dga, dgb, dgc, g["b_gate"] = _merge_bwd(dmerged, p0, p1, p2, proj, gates, w["b_gate"])
    do_fox = _mm(dp0, w["w_br_fox"], "nt", BF16, "branch_fox_dx")
    do_sb = _mm(dp1, w["w_br_sb"], "nt", BF16, "branch_sb_dx")
    do_mem = _mm(dp2, w["w_br_mem"], "nt", BF16, "branch_mem_dx")
    g["w_br_fox"] = _mm(o_fox, dp0, "tn", BF16, "branch_fox_dw")
    g["w_br_sb"] = _mm(o_sb, dp1, "tn", BF16, "branch_sb_dw")
    g["w_br_mem"] = _mm(o_mem, dp2, "tn", BF16, "branch_mem_dw")
    sums_mix = _sibling_sums(REDUCE_MIX, g, core, "mix")

    (dqn, dkn, dfv, drs, dcs), others_ffn = _fox_bwd(qn, kn, proj, fv, c_col, c_row, o_fox32, do_fox, lse, nf,
                                                     carry=_Scatter(sums_ffn))
    dfq, g["g_q_fox"] = _headnorm_bwd(dqn, proj, fq, nf, w["g_q_fox"], "fox_qnorm_bwd")
    dfk, g["g_k_fox"] = _headnorm_bwd(dkn, proj, fk, nf, w["g_k_fox"], "fox_knorm_bwd")
    dc = jnp.pad((drs[:, :, 0] - dcs[:, 0, :]).T, ((0, 0), (0, LANES - nf)))
    df, g["b_forget"] = _forget_bwd(dc, f_logit, w["b_forget"])
    (dsq, dsk, dsv), others_mix = _sb_bwd(proj, sq, sk, sv, do_sb, nsb, carry=_Scatter(sums_mix))
    dqmn, dkmn, dvm = _mem_bwd(qmn, kmn, mkv, do_mem, nm)
    dmq, g["g_q_mem"] = _headnorm_bwd(dqmn, proj, mq, nm, w["g_q_mem"], "mem_qnorm_bwd")
    dkm, g["g_k_mem"] = _headnorm_bwd(dkmn, mkv, 0, nm, w["g_k_mem"], "mem_knorm_bwd")
    dmkv = jnp.concatenate([dkm, dvm.astype(BF16)], axis=1)
    g["w_mem_kv"] = _mm(memn, dmkv, "tn", BF16, "mem_kv_dw")
    dmemn = _mm(dmkv, w["w_mem_kv"], "nt", BF16, "mem_kv_dx")
    _, _, g["g_mem"] = _rms_bwd(dmemn, mem, rstd_m, w["g_mem"], None, "rms_mem_bwd")

    dproj = jnp.concatenate([dfq, dfk, dfv.astype(BF16), dsq.astype(BF16), dsk.astype(BF16), dsv.astype(BF16),
                             dmq, dga, dgb, dgc], axis=1)
    dfb = df.astype(BF16)
    g["w_in_main"] = _mm(h, dproj, "tn", BF16, "proj_main_dw")
    g["w_in_f"] = _mm(h, dfb, "tn", BF16, "proj_forget_dw")
    sums_in = _sibling_sums(REDUCE_IN, g, core, "in")
    dh_main, others_in = _mm(dproj, w["w_in_main"], "nt", F32, "proj_main_dx", carry=_Scatter(sums_in))
    dh = _mm(dfb, w["w_in_f"], "nt", F32, "proj_forget_dx", residual=dh_main)
    grad_x, _, g["g_mix"] = _rms_bwd(dh, x, rstd1, w["g_mix"], dx1, "rms_mix_bwd")

    names = REDUCE_FFN + REDUCE_MIX + REDUCE_IN
    finals = [_add_chips(own, theirs, chip_core, "add_chips_" + name)
              for name, own, theirs in zip(names, sums_ffn + sums_mix + sums_in, others_ffn + others_mix + others_in)]
    summed = {name: a.reshape(2 * a.shape[1], a.shape[2]) for name, a in zip(names, _join_halves(finals))}
    return loss, grad_x, g, summed


SMALL = ("g_mix", "b_forget", "g_q_fox", "g_k_fox", "g_mem", "g_q_mem", "g_k_mem", "b_gate", "g_ffn", "conv_w",
         "conv_b")
SMALL_SHARDED = ("b_gate", "conv_w")
PACK_ROWS = 8


def _pack(arrs):
    flat = jnp.concatenate([a.reshape(-1) for a in arrs])
    unit = PACK_ROWS * LANES
    flat = jnp.pad(flat, (0, -flat.shape[0] % unit))
    return flat.reshape(-1, LANES)


def _unpack(packed, shapes):
    flat = packed.reshape(-1)
    out, at = [], 0
    for s in shapes:
        n = 1
        for dim in s:
            n *= dim
        out.append(flat[at:at + n].reshape(s))
        at += n
    return out


def kernel(x, mem, g_mix, w_in, b_forget, g_q_fox, g_k_fox, g_mem, w_mem_kv, g_q_mem, g_k_mem, w_br_fox, w_br_sb, w_br_mem, b_gate, w_out, g_ffn, w_up, conv_w, conv_b, w_down, loss_target, m_g_mix, m_w_in, m_b_forget, m_g_q_fox, m_g_k_fox, m_g_mem, m_w_mem_kv, m_g_q_mem, m_g_k_mem, m_w_br_fox, m_w_br_sb, m_w_br_mem, m_b_gate, m_w_out, m_g_ffn, m_w_up, m_conv_w, m_conv_b, m_w_down, v_g_mix, v_w_in, v_b_forget, v_g_q_fox, v_g_k_fox, v_g_mem, v_w_mem_kv, v_g_q_mem, v_g_k_mem, v_w_br_fox, v_w_br_sb, v_w_br_mem, v_b_gate, v_w_out, v_g_ffn, v_w_up, v_conv_w, v_conv_b, v_w_down):
    given = dict(g_mix=g_mix, w_in=w_in, b_forget=b_forget, g_q_fox=g_q_fox, g_k_fox=g_k_fox, g_mem=g_mem,
                 w_mem_kv=w_mem_kv, g_q_mem=g_q_mem, g_k_mem=g_k_mem, w_br_fox=w_br_fox, w_br_sb=w_br_sb,
                 w_br_mem=w_br_mem, b_gate=b_gate, w_out=w_out, g_ffn=g_ffn, w_up=w_up, conv_w=conv_w, conv_b=conv_b,
                 w_down=w_down)
    m_in = dict(g_mix=m_g_mix, w_in=m_w_in, b_forget=m_b_forget, g_q_fox=m_g_q_fox, g_k_fox=m_g_k_fox, g_mem=m_g_mem,
                w_mem_kv=m_w_mem_kv, g_q_mem=m_g_q_mem, g_k_mem=m_g_k_mem, w_br_fox=m_w_br_fox, w_br_sb=m_w_br_sb,
                w_br_mem=m_w_br_mem, b_gate=m_b_gate, w_out=m_w_out, g_ffn=m_g_ffn, w_up=m_w_up, conv_w=m_conv_w,
                conv_b=m_conv_b, w_down=m_w_down)
    v_in = dict(g_mix=v_g_mix, w_in=v_w_in, b_forget=v_b_forget, g_q_fox=v_g_q_fox, g_k_fox=v_g_k_fox, g_mem=v_g_mem,
                w_mem_kv=v_w_mem_kv, g_q_mem=v_g_q_mem, g_k_mem=v_g_k_mem, w_br_fox=v_w_br_fox, w_br_sb=v_w_br_sb,
                w_br_mem=v_w_br_mem, b_gate=v_b_gate, w_out=v_w_out, g_ffn=v_g_ffn, w_up=v_w_up, conv_w=v_conv_w,
                conv_b=v_conv_b, w_down=v_w_down)
    layered = {k: a.ndim == 3 for k, a in given.items()}
    drop = lambda a: a[0] if a.ndim == 3 else a
    given = {k: drop(a) for k, a in given.items()}
    m_in = {k: drop(a) for k, a in m_in.items()}
    v_in = {k: drop(a) for k, a in v_in.items()}

    xi, yi, ci = lax.axis_index("x"), lax.axis_index("y"), lax.axis_index("c")
    chip = (2 * xi + yi).astype(jnp.int32)
    core_arr = ci.astype(jnp.int32).reshape(1)
    chip_core = jnp.stack([chip, ci.astype(jnp.int32)])

    nf = given["b_forget"].shape[1]
    cut = 3 * given["w_br_fox"].shape[0]

    shard = {
        "w_in_main": jnp.concatenate([given["w_in"][:, :cut], given["w_in"][:, cut + nf:]], axis=1).astype(BF16),
        "w_in_f": jnp.pad(given["w_in"][:, cut:cut + nf], ((0, 0), (0, LANES - nf))).astype(BF16),
    }
    for name in BIG:
        if name not in shard:
            shard[name] = given[name].astype(BF16)
    w = {}
    small_shapes = [given[name].shape for name in SMALL_SHARDED]
    small_parts = _gather_small(_pack([given[name] for name in SMALL_SHARDED]))[0::2]
    per_chip = [_unpack(small_parts[j], small_shapes) for j in range(N_CHIPS)]
    for k, name in enumerate(SMALL_SHARDED):
        w[name] = jnp.concatenate([per_chip[j][k] for j in range(N_CHIPS)], axis=1)
    for name in SMALL:
        if name not in SMALL_SHARDED:
            w[name] = given[name]
    w["b_forget"] = jnp.pad(given["b_forget"], ((0, 0), (0, LANES - nf)))

    loss, grad_x, g, summed = _local_step(x[0], mem[0], loss_target[0], w, shard, chip, core_arr, chip_core)
    loss = lax.psum(loss, ("x", "y", "c"))
    grads = {name: summed[name] for name in BIG if name in given}
    grads["w_in"] = jnp.concatenate([summed["w_in_main"][:, :cut], summed["w_in_f"][:, :nf],
                                     summed["w_in_main"][:, cut:]], axis=1)

    g["b_forget"] = g["b_forget"][:, :nf]
    small_full_shapes = [g[name].shape for name in SMALL]
    small_sum = _unpack(_sum_devices(_gather_small(_pack([g[name] for name in SMALL]))), small_full_shapes)
    for name, a in zip(SMALL, small_sum):
        if name in SMALL_SHARDED:
            width = given[name].shape[1]
            a = lax.dynamic_slice_in_dim(a, chip * width, width, axis=1)
        grads[name] = a

    delta, new_m, new_v = {}, {}, {}
    for name in WEIGHTS:
        if name not in SMALL:
            delta[name], new_m[name], new_v[name] = _adamw(given[name], grads[name], m_in[name], v_in[name],
                                                           "adamw_" + name)
    shapes = [given[name].shape for name in SMALL]
    packed = [_pack([src[name] for name in SMALL]) for src in (given, grads, m_in, v_in)]
    for dst, res in zip((delta, new_m, new_v), _adamw(*packed, "adamw_small")):
        for name, a in zip(SMALL, _unpack(res, shapes)):
            dst[name] = a

    out = [loss, grad_x[None]]
    for src in (grads, delta, new_m, new_v):
        out.extend(src[name][None] if layered[name] else src[name] for name in WEIGHTS)
    return tuple(out)
```

```python
import functools

import jax
import jax.numpy as jnp
from jax import lax
from jax.experimental import pallas as pl
from jax.experimental.pallas import tpu as pltpu

F32 = jnp.float32
BF16 = jnp.bfloat16

HEAD_DIM = 128
EPS = 1e-6
NEG_BIG = -1e30

ADAM_LR = 0.001
ADAM_B1 = 0.9
ADAM_B2 = 0.999
ADAM_EPS = 1e-08
ADAM_WD = 0.01
ADAM_STEP = 10

LANES = 128
BF16_SUBLANES = 16
VMEM_LIMIT_BYTES = 56 * 1024 * 1024
MM_TILE = 1024
MM_TILE_K = 2048
ATT_TILE = 256
ROW_TILE = 256
HEADNORM_ROWS = 512
COL_TILE = 512
ADAM_BLOCK_BYTES = 1 << 20

N_CHIPS = 4
N_DEV = 8
MESH = pl.DeviceIdType.MESH

IN_NAMES = ['x', 'mem', 'g_mix', 'w_in', 'b_forget', 'g_q_fox', 'g_k_fox', 'g_mem', 'w_mem_kv', 'g_q_mem', 'g_k_mem',
            'w_br_fox', 'w_br_sb', 'w_br_mem', 'b_gate', 'w_out', 'g_ffn', 'w_up', 'conv_w', 'conv_b', 'w_down']
WEIGHTS = IN_NAMES[2:]


def _tile(n, target):
    if n <= target:
        return n
    for t in range(target - target % LANES, LANES - 1, -LANES):
        if n % t == 0:
            return t
    return n


def _params(*sem):
    return pltpu.CompilerParams(dimension_semantics=sem, vmem_limit_bytes=VMEM_LIMIT_BYTES)


def _log_sigmoid(z):
    return jnp.minimum(z, 0.0) - jnp.log(1.0 + jnp.exp(-jnp.abs(z)))


def _split2(v):
    hi = v.astype(BF16)
    lo = (v - hi.astype(F32)).astype(BF16)
    return hi, lo


def _split3(v):
    hi = v.astype(BF16)
    r = v - hi.astype(F32)
    mid = r.astype(BF16)
    lo = (r - mid.astype(F32)).astype(BF16)
    return hi, mid, lo


def _dot(a, b):
    return lax.dot_general(a, b, (((1,), (0,)), ((), ())), preferred_element_type=F32)


def _dot_nt(a, b):
    return lax.dot_general(a, b, (((1,), (1,)), ((), ())), preferred_element_type=F32)


def _dot_tn(a, b):
    return lax.dot_general(a, b, (((0,), (0,)), ((), ())), preferred_element_type=F32)


ANY = pl.BlockSpec(memory_space=pl.ANY)


def _place():
    x, y, c = lax.axis_index("x"), lax.axis_index("y"), lax.axis_index("c")
    others = [(1 - x, y), (x, 1 - y), (1 - x, 1 - y)]
    return x, y, c, others


def _remote(src, dst, send_sem, recv_sem, to):
    return pltpu.make_async_remote_copy(src_ref=src, dst_ref=dst, send_sem=send_sem, recv_sem=recv_sem,
                                        device_id=to, device_id_type=MESH)


class _Gather:
    def __init__(self, shards):
        self.inputs = list(shards)
        n = len(shards)
        self.out_shapes = [jax.ShapeDtypeStruct((N_CHIPS,) + s.shape, s.dtype) for s in shards]
        self.scratch = [pltpu.SemaphoreType.DMA((6 * n,)), pltpu.SemaphoreType.DMA((6 * n,))]

    def _over_ici(self, ins, outs, sems):
        send_sems, recv_sems = sems
        x, y, c, others = _place()
        me = 2 * x + y
        copies = []
        for i in range(len(ins)):
            h = ins[i].shape[0] // 2
            mine = pl.ds(pl.multiple_of(c * h, BF16_SUBLANES), h)
            for j, (ox, oy) in enumerate(others):
                copies.append(_remote(ins[i].at[mine], outs[i].at[me, mine], send_sems.at[6 * i + j],
                                      recv_sems.at[6 * i + j], (ox, oy, c)))
        return copies

    def start(self, ins, outs, sems):
        for cp in self._over_ici(ins, outs, sems):
            cp.start()

    def finish(self, ins, outs, sems):
        send_sems, recv_sems = sems
        x, y, c, others = _place()
        sibling = (x, y, 1 - c)
        passed = []
        for i in range(len(ins)):
            h = ins[i].shape[0] // 2
            mine = pl.ds(pl.multiple_of(c * h, BF16_SUBLANES), h)
            for j, (ox, oy) in enumerate(others):
                blk = outs[i].at[2 * ox + oy, mine]
                _remote(blk, blk, send_sems.at[6 * i + j], recv_sems.at[6 * i + j], (ox, oy, c)).wait_recv()
                cp = _remote(blk, blk, send_sems.at[6 * i + 3 + j], recv_sems.at[6 * i + 3 + j], sibling)
                cp.start()
                passed.append(cp)
        for i in range(len(ins)):
            h = ins[i].shape[0] // 2
            theirs = pl.ds(pl.multiple_of((1 - c) * h, BF16_SUBLANES), h)
            for j, (ox, oy) in enumerate(others):
                blk = outs[i].at[2 * ox + oy, theirs]
                _remote(blk, blk, send_sems.at[6 * i + 3 + j], recv_sems.at[6 * i + 3 + j], sibling).wait_recv()
        for cp in self._over_ici(ins, outs, sems) + passed:
            cp.wait_send()


class _Scatter:
    def __init__(self, sums):
        self.inputs = list(sums)
        k = N_CHIPS - 1
        self.out_shapes = [jax.ShapeDtypeStruct((k,) + g.shape[1:], g.dtype) for g in sums]
        self.scratch = [pltpu.SemaphoreType.DMA((k * len(sums),)), pltpu.SemaphoreType.DMA((k * len(sums),))]

    def _copies(self, ins, outs, sems):
        send_sems, recv_sems = sems
        _, _, c, others = _place()
        k = N_CHIPS - 1
        return [_remote(ins[i].at[2 * ox + oy], outs[i].at[j], send_sems.at[k * i + j], recv_sems.at[k * i + j],
                        (ox, oy, c))
                for i in range(len(ins)) for j, (ox, oy) in enumerate(others)]

    def start(self, ins, outs, sems):
        for cp in self._copies(ins, outs, sems):
            cp.start()

    def finish(self, ins, outs, sems):
        for cp in self._copies(ins, outs, sems):
            cp.wait()


def _exchange(carry, name):
    n = len(carry.inputs)

    def body(*refs):
        ins, outs, sems = refs[:n], refs[n:2 * n], refs[2 * n:]
        carry.start(ins, outs, sems)
        carry.finish(ins, outs, sems)

    return pl.pallas_call(
        body,
        in_specs=[ANY] * n,
        out_specs=[ANY] * n,
        out_shape=carry.out_shapes,
        scratch_shapes=carry.scratch,
        name=name,
    )(*carry.inputs)


def _call(body, *, grid, in_specs, out_specs, out_shape, scratch_shapes, semantics, name, args, carry=None):
    n_in, n_out, n_scr = len(in_specs), len(out_specs), len(scratch_shapes)
    if carry is None:
        res = pl.pallas_call(body, grid=grid, in_specs=in_specs, out_specs=out_specs, out_shape=out_shape,
                             scratch_shapes=scratch_shapes, compiler_params=_params(*semantics), name=name)(*args)
        return list(res), []
    nci, nco = len(carry.inputs), len(carry.out_shapes)
    a, b = n_in, n_in + nci
    c, d = b + n_out, b + n_out + nco
    e = d + n_scr

    def carried(*refs):
        ids = [pl.program_id(k) for k in range(len(grid))]
        first = functools.reduce(jnp.logical_and, [i == 0 for i in ids])
        last = functools.reduce(jnp.logical_and, [i == n - 1 for i, n in zip(ids, grid)])

        @pl.when(first)
        def _():
            carry.start(refs[a:b], refs[c:d], refs[e:])

        body(*refs[:a], *refs[b:c], *refs[d:e])

        @pl.when(last)
        def _():
            carry.finish(refs[a:b], refs[c:d], refs[e:])

    res = pl.pallas_call(
        carried,
        grid=grid,
        in_specs=list(in_specs) + [ANY] * nci,
        out_specs=list(out_specs) + [ANY] * nco,
        out_shape=list(out_shape) + carry.out_shapes,
        scratch_shapes=list(scratch_shapes) + carry.scratch,
        compiler_params=_params(*(["arbitrary"] * len(grid))),
        name=name,
    )(*args, *carry.inputs)
    return list(res[:n_out]), list(res[n_out:])


def _mm(a, b, mode, out_dtype, name, residual=None, carry=None):
    if mode == "nn":
        (m, k), (k2, n) = a.shape, b.shape
    elif mode == "nt":
        (m, k), (n, k2) = a.shape, b.shape
    else:
        (k, m), (k2, n) = a.shape, b.shape
    assert k == k2, (a.shape, b.shape, mode)
    tm, tn, tk = _tile(m, MM_TILE), _tile(n, MM_TILE), _tile(k, MM_TILE_K)
    nk = k // tk
    dot = {"nn": _dot, "nt": _dot_nt, "tn": _dot_tn}[mode]
    has_res = residual is not None

    def body(*refs):
        if has_res:
            a_ref, b_ref, r_ref, o_ref = refs[:4]
        else:
            a_ref, b_ref, o_ref = refs[:3]
            r_ref = None

        def finish(acc):
            if has_res:
                acc = acc + r_ref[...]
            o_ref[...] = acc.astype(o_ref.dtype)

        part = dot(a_ref[...], b_ref[...])
        if nk == 1:
            finish(part)
        else:
            acc_ref = refs[-1]
            kk = pl.program_id(2)

            @pl.when(kk == 0)
            def _():
                acc_ref[...] = part

            @pl.when(kk > 0)
            def _():
                acc_ref[...] += part

            @pl.when(kk == nk - 1)
            def _():
                finish(acc_ref[...])

    if mode == "tn":
        a_spec = pl.BlockSpec((tk, tm), lambda j, i, kk: (kk, i))
    else:
        a_spec = pl.BlockSpec((tm, tk), lambda j, i, kk: (i, kk))
    if mode == "nt":
        b_spec = pl.BlockSpec((tn, tk), lambda j, i, kk: (j, kk))
    else:
        b_spec = pl.BlockSpec((tk, tn), lambda j, i, kk: (kk, j))
    o_spec = pl.BlockSpec((tm, tn), lambda j, i, kk: (i, j))
    in_specs = [a_spec, b_spec] + ([o_spec] if has_res else [])
    args = (a, b) + ((residual,) if has_res else ())
    (out,), moved = _call(
        body,
        grid=(n // tn, m // tm, nk),
        in_specs=in_specs,
        out_specs=[o_spec],
        out_shape=[jax.ShapeDtypeStruct((m, n), out_dtype)],
        scratch_shapes=[pltpu.VMEM((tm, tn), F32)] if nk > 1 else [],
        semantics=("parallel", "parallel", "arbitrary"),
        name=name,
        args=args,
        carry=carry,
    )
    return out if carry is None else (out, moved)


def _rms_fwd(x, g, name):
    s, d = x.shape
    tm = _tile(s, ROW_TILE)

    def body(x_ref, g_ref, h_ref, r_ref):
        xf = x_ref[...]
        r = lax.rsqrt(jnp.mean(xf * xf, axis=-1, keepdims=True) + EPS)
        h_ref[...] = ((xf * r) * g_ref[...]).astype(BF16)
        r_ref[...] = r

    return pl.pallas_call(
        body,
        grid=(s // tm,),
        in_specs=[pl.BlockSpec((tm, d), lambda i: (i, 0)), pl.BlockSpec((1, d), lambda i: (0, 0))],
        out_specs=[pl.BlockSpec((tm, d), lambda i: (i, 0)), pl.BlockSpec((tm, 1), lambda i: (i, 0))],
        out_shape=[jax.ShapeDtypeStruct((s, d), BF16), jax.ShapeDtypeStruct((s, 1), F32)],
        compiler_params=_params("parallel"),
        name=name,
    )(x, g)


def _rms_bwd(dh, x, rstd, g, res, name):
    s, d = x.shape
    tm = _tile(s, ROW_TILE)
    has_res = res is not None

    def body(*refs):
        if has_res:
            dh_ref, x_ref, r_ref, g_ref, res_ref, dx_ref, dxb_ref, dg_ref = refs
        else:
            dh_ref, x_ref, r_ref, g_ref, dx_ref, dxb_ref, dg_ref = refs
        dhf = dh_ref[...].astype(F32)
        xhat = x_ref[...] * r_ref[...]
        dy = dhf * g_ref[...]
        dx = r_ref[...] * (dy - xhat * jnp.mean(dy * xhat, axis=-1, keepdims=True))
        if has_res:
            dx = dx + res_ref[...]
        dx_ref[...] = dx
        dxb_ref[...] = dx.astype(BF16)
        part = jnp.sum(dhf * xhat, axis=0, keepdims=True)

        @pl.when(pl.program_id(0) == 0)
        def _():
            dg_ref[...] = part

        @pl.when(pl.program_id(0) > 0)
        def _():
            dg_ref[...] += part

    row = pl.BlockSpec((tm, d), lambda i: (i, 0))
    vec = pl.BlockSpec((1, d), lambda i: (0, 0))
    in_specs = [row, row, pl.BlockSpec((tm, 1), lambda i: (i, 0)), vec] + ([row] if has_res else [])
    args = (dh, x, rstd, g) + ((res,) if has_res else ())
    return pl.pallas_call(
        body,
        grid=(s // tm,),
        in_specs=in_specs,
        out_specs=[row, row, vec],
        out_shape=[jax.ShapeDtypeStruct((s, d), F32), jax.ShapeDtypeStruct((s, d), BF16),
                   jax.ShapeDtypeStruct((1, d), F32)],
        compiler_params=_params("arbitrary"),
        name=name,
    )(*args)


def _headnorm_fwd(src, col0, nheads, g, name):
    s = src.shape[0]
    tm = _tile(s, HEADNORM_ROWS)
    w = nheads * HEAD_DIM
    assert col0 % nheads == 0

    def body(x_ref, g_ref, o_ref):
        for hh in range(nheads):
            xf = _head(x_ref, hh).astype(F32)
            r = lax.rsqrt(jnp.mean(xf * xf, axis=-1, keepdims=True) + EPS)
            o_ref[:, hh * HEAD_DIM:(hh + 1) * HEAD_DIM] = ((xf * r) * g_ref[...]).astype(BF16)

    return pl.pallas_call(
        body,
        grid=(s // tm,),
        in_specs=[pl.BlockSpec((tm, w), lambda i: (i, col0 // nheads)),
                  pl.BlockSpec((1, HEAD_DIM), lambda i: (0, 0))],
        out_specs=pl.BlockSpec((tm, w), lambda i: (i, 0)),
        out_shape=jax.ShapeDtypeStruct((s, w), BF16),
        compiler_params=_params("parallel"),
        name=name,
    )(src, g)


def _headnorm_bwd(dxn, src, col0, nheads, g, name):
    s = src.shape[0]
    tm = _tile(s, HEADNORM_ROWS)
    w = nheads * HEAD_DIM
    assert col0 % nheads == 0

    def body(d_ref, x_ref, g_ref, dx_ref, dg_ref):
        part = jnp.zeros((1, HEAD_DIM), F32)
        for hh in range(nheads):
            xf = _head(x_ref, hh).astype(F32)
            r = lax.rsqrt(jnp.mean(xf * xf, axis=-1, keepdims=True) + EPS)
            xhat = xf * r
            dn = _head(d_ref, hh).astype(F32)
            dy = dn * g_ref[...]
            dx = r * (dy - xhat * jnp.mean(dy * xhat, axis=-1, keepdims=True))
            dx_ref[:, hh * HEAD_DIM:(hh + 1) * HEAD_DIM] = dx.astype(BF16)
            part = part + jnp.sum(dn * xhat, axis=0, keepdims=True)

        @pl.when(pl.program_id(0) == 0)
        def _():
            dg_ref[...] = part

        @pl.when(pl.program_id(0) > 0)
        def _():
            dg_ref[...] += part

    return pl.pallas_call(
        body,
        grid=(s // tm,),
        in_specs=[pl.BlockSpec((tm, w), lambda i: (i, 0)),
                  pl.BlockSpec((tm, w), lambda i: (i, col0 // nheads)),
                  pl.BlockSpec((1, HEAD_DIM), lambda i: (0, 0))],
        out_specs=[pl.BlockSpec((tm, w), lambda i: (i, 0)),
                   pl.BlockSpec((1, HEAD_DIM), lambda i: (0, 0))],
        out_shape=[jax.ShapeDtypeStruct((s, w), BF16), jax.ShapeDtypeStruct((1, HEAD_DIM), F32)],
        compiler_params=_params("arbitrary"),
        name=name,
    )(dxn, src, g)


def _tri(t, lower_inclusive):
    r = lax.broadcasted_iota(jnp.int32, (t, t), 0)
    c = lax.broadcasted_iota(jnp.int32, (t, t), 1)
    keep = (c <= r) if lower_inclusive else (c >= r)
    return jnp.where(keep, 1.0, 0.0).astype(BF16)


def _forget_fwd(f_logit, b_pad):
    s = f_logit.shape[0]
    t = _tile(s, ATT_TILE)

    def body(f_ref, b_ref, c_ref, carry):
        @pl.when(pl.program_id(0) == 0)
        def _():
            carry[...] = jnp.zeros_like(carry)

        lf = _log_sigmoid(f_ref[...] + b_ref[...])
        tri = _tri(t, True)
        acc = carry[...]
        for part in _split3(lf):
            acc = acc + _dot(tri, part)
        c_ref[...] = acc
        carry[...] += jnp.sum(lf, axis=0, keepdims=True)

    return pl.pallas_call(
        body,
        grid=(s // t,),
        in_specs=[pl.BlockSpec((t, LANES), lambda i: (i, 0)), pl.BlockSpec((1, LANES), lambda i: (0, 0))],
        out_specs=pl.BlockSpec((t, LANES), lambda i: (i, 0)),
        out_shape=jax.ShapeDtypeStruct((s, LANES), F32),
        scratch_shapes=[pltpu.VMEM((1, LANES), F32)],
        compiler_params=_params("arbitrary"),
        name="forget_fwd",
    )(f_logit, b_pad)


def _forget_bwd(dc, f_logit, b_pad):
    s = f_logit.shape[0]
    t = _tile(s, ATT_TILE)
    nb = s // t

    def body(dc_ref, f_ref, b_ref, df_ref, db_ref, carry):
        @pl.when(pl.program_id(0) == 0)
        def _():
            carry[...] = jnp.zeros_like(carry)
            db_ref[...] = jnp.zeros_like(db_ref)

        d = dc_ref[...]
        tri = _tri(t, False)
        acc = carry[...]
        for part in _split3(d):
            acc = acc + _dot(tri, part)
        z = f_ref[...] + b_ref[...]
        df = acc * jnp.exp(_log_sigmoid(-z))
        df_ref[...] = df
        db_ref[...] += jnp.sum(df, axis=0, keepdims=True)
        carry[...] += jnp.sum(d, axis=0, keepdims=True)

    rev = pl.BlockSpec((t, LANES), lambda i: (nb - 1 - i, 0))
    vec = pl.BlockSpec((1, LANES), lambda i: (0, 0))
    return pl.pallas_call(
        body,
        grid=(nb,),
        in_specs=[rev, rev, vec],
        out_specs=[rev, vec],
        out_shape=[jax.ShapeDtypeStruct((s, LANES), F32), jax.ShapeDtypeStruct((1, LANES), F32)],
        scratch_shapes=[pltpu.VMEM((1, LANES), F32)],
        compiler_params=_params("arbitrary"),
        name="forget_bwd",
    )(dc, f_logit, b_pad)


SB_FWD_GROUP = 3
FOX_GROUP = 3
SB_BWD_GROUP = 3


def _head(ref, hh, rows=slice(None)):
    return ref[rows, hh * HEAD_DIM:(hh + 1) * HEAD_DIM]


def _tri_mask(t, strict):
    r = lax.broadcasted_iota(jnp.int32, (t, t), 0)
    c = lax.broadcasted_iota(jnp.int32, (t, t), 1)
    return (c < r) if strict else (c <= r)


def _fox_fwd(qn, kn, proj, colv, c_col, c_row, nheads, carry=None):
    s = qn.shape[0]
    t = _tile(s, ATT_TILE)
    scale = HEAD_DIM ** -0.5
    hg = FOX_GROUP
    gw = hg * HEAD_DIM
    assert nheads % hg == 0 and colv % hg == 0

    def body(q_ref, k_ref, v_ref, cc_ref, cr_ref, o_ref, of_ref, lse_ref):
        qi = pl.program_id(1)
        causal = _tri_mask(t, False)

        def tile(kj, carry, diagonal):
            off = pl.multiple_of(kj * t, t)
            heads = range(hg)
            rows = pl.ds(off, t)
            qk = [_dot_nt(_head(q_ref, hh), _head(k_ref, hh, rows)) for hh in heads]
            sc = [qk[hh] * scale + (cc_ref[hh] - cr_ref[hh, :, rows]) for hh in heads]
            if diagonal:
                sc = [jnp.where(causal, sc[hh], NEG_BIG) for hh in heads]
            m_new = [jnp.maximum(carry[hh][0], jnp.max(sc[hh], axis=-1, keepdims=True)) for hh in heads]
            p = [jnp.exp(sc[hh] - m_new[hh]) for hh in heads]
            pv = [_dot(p[hh].astype(BF16), _head(v_ref, hh, rows)) for hh in heads]
            out = []
            for hh in heads:
                m, l, acc = carry[hh]
                alpha = jnp.exp(m - m_new[hh])
                out.append((m_new[hh], alpha * l + jnp.sum(p[hh], axis=-1, keepdims=True), alpha * acc + pv[hh]))
            return tuple(out)

        init = tuple((jnp.full((t, 1), NEG_BIG, F32), jnp.zeros((t, 1), F32), jnp.zeros((t, HEAD_DIM), F32))
                     for _ in range(hg))
        carry = lax.fori_loop(0, qi, lambda kj, c: tile(kj, c, False), init)
        carry = tile(qi, carry, True)
        for hh in range(hg):
            m, l, acc = carry[hh]
            o = acc / l
            of_ref[:, hh * HEAD_DIM:(hh + 1) * HEAD_DIM] = o
            o_ref[:, hh * HEAD_DIM:(hh + 1) * HEAD_DIM] = o.astype(BF16)
            lse_ref[hh] = m + jnp.log(l)

    tile_spec = pl.BlockSpec((t, gw), lambda h, i: (i, h))
    w = nheads * HEAD_DIM
    return _call(
        body,
        grid=(nheads // hg, s // t),
        in_specs=[tile_spec,
                  pl.BlockSpec((s, gw), lambda h, i: (0, h)),
                  pl.BlockSpec((s, gw), lambda h, i: (0, colv // hg + h)),
                  pl.BlockSpec((hg, t, 1), lambda h, i: (h, i, 0)),
                  pl.BlockSpec((hg, 1, s), lambda h, i: (h, 0, 0))],
        out_specs=[tile_spec, tile_spec, pl.BlockSpec((hg, t, 1), lambda h, i: (h, i, 0))],
        out_shape=[jax.ShapeDtypeStruct((s, w), BF16), jax.ShapeDtypeStruct((s, w), F32),
                   jax.ShapeDtypeStruct((nheads, s, 1), F32)],
        scratch_shapes=[],
        semantics=("parallel", "parallel"),
        name="fox_fwd",
        args=(qn, kn, proj, c_col, c_row),
        carry=carry,
    )


def _fox_bwd(qn, kn, proj, colv, c_col, c_row, o, do, lse, nheads, carry=None):
    s = qn.shape[0]
    t = _tile(s, ATT_TILE)
    scale = HEAD_DIM ** -0.5
    hg = FOX_GROUP
    gw = hg * HEAD_DIM
    assert nheads % hg == 0 and colv % hg == 0

    def body(q_ref, k_ref, v_ref, cc_ref, cr_ref, o_ref, do_ref, lse_ref,
             dq_ref, dk_ref, dv_ref, drs_ref, dcs_ref):
        qi = pl.program_id(1)

        @pl.when(qi == 0)
        def _():
            dk_ref[...] = jnp.zeros_like(dk_ref)
            dv_ref[...] = jnp.zeros_like(dv_ref)
            dcs_ref[...] = jnp.zeros_like(dcs_ref)

        causal = _tri_mask(t, False)
        delta = [jnp.sum(_head(o_ref, hh) * _head(do_ref, hh).astype(F32), axis=-1, keepdims=True)
                 for hh in range(hg)]

        def tile(kj, carry, diagonal):
            off = pl.multiple_of(kj * t, t)
            heads = range(hg)
            rows = pl.ds(off, t)
            qk = [_dot_nt(_head(q_ref, hh), _head(k_ref, hh, rows)) for hh in heads]
            dp = [_dot_nt(_head(do_ref, hh), _head(v_ref, hh, rows)) for hh in heads]
            p = [jnp.exp(qk[hh] * scale + (cc_ref[hh] - cr_ref[hh, :, rows]) - lse_ref[hh]) for hh in heads]
            if diagonal:
                p = [jnp.where(causal, p[hh], 0.0) for hh in heads]
            ds = [p[hh] * (dp[hh] - delta[hh]) for hh in heads]
            dsb = [ds[hh].astype(BF16) for hh in heads]
            dv = [_dot_tn(p[hh].astype(BF16), _head(do_ref, hh)) for hh in heads]
            dk = [_dot_tn(dsb[hh], _head(q_ref, hh)) * scale for hh in heads]
            dq = [_dot(dsb[hh], _head(k_ref, hh, rows)) * scale for hh in heads]
            for hh in heads:
                cols = slice(hh * HEAD_DIM, (hh + 1) * HEAD_DIM)
                dv_ref[rows, cols] += dv[hh]
                dk_ref[rows, cols] += dk[hh]
                dcs_ref[hh, :, rows] += jnp.sum(ds[hh], axis=0, keepdims=True)
            return tuple((carry[hh][0] + dq[hh], carry[hh][1] + jnp.sum(ds[hh], axis=-1, keepdims=True))
                         for hh in heads)

        init = tuple((jnp.zeros((t, HEAD_DIM), F32), jnp.zeros((t, 1), F32)) for _ in range(hg))
        carry = lax.fori_loop(0, qi, lambda kj, c: tile(kj, c, False), init)
        carry = tile(qi, carry, True)
        for hh in range(hg):
            dq_ref[:, hh * HEAD_DIM:(hh + 1) * HEAD_DIM] = carry[hh][0]
            drs_ref[hh] = carry[hh][1]

    tile_spec = pl.BlockSpec((t, gw), lambda h, i: (i, h))
    full = pl.BlockSpec((s, gw), lambda h, i: (0, h))
    colspec = pl.BlockSpec((hg, t, 1), lambda h, i: (h, i, 0))
    rowspec = pl.BlockSpec((hg, 1, s), lambda h, i: (h, 0, 0))
    w = nheads * HEAD_DIM
    return _call(
        body,
        grid=(nheads // hg, s // t),
        in_specs=[tile_spec, full, pl.BlockSpec((s, gw), lambda h, i: (0, colv // hg + h)), colspec, rowspec,
                  tile_spec, tile_spec, colspec],
        out_specs=[tile_spec, full, full, colspec, rowspec],
        out_shape=[jax.ShapeDtypeStruct((s, w), F32), jax.ShapeDtypeStruct((s, w), F32),
                   jax.ShapeDtypeStruct((s, w), F32), jax.ShapeDtypeStruct((nheads, s, 1), F32),
                   jax.ShapeDtypeStruct((nheads, 1, s), F32)],
        scratch_shapes=[],
        semantics=("arbitrary", "arbitrary"),
        name="fox_bwd",
        args=(qn, kn, proj, c_col, c_row, o, do, lse),
        carry=carry,
    )


def _sb_tile(q, k, scale, later, valid):
    z = _dot_nt(q, k) * scale
    lb = _log_sigmoid(z)
    lm = lb - z
    if valid is not None:
        lm = jnp.where(valid, lm, 0.0)
    suffix = _dot(jnp.concatenate(_split2(lm), axis=1), later)
    return lb, lm, suffix


def _later(t):
    r = lax.broadcasted_iota(jnp.int32, (2 * t, t), 0) % t
    c = lax.broadcasted_iota(jnp.int32, (2 * t, t), 1)
    return jnp.where(r > c, 1.0, 0.0).astype(BF16)


def _sb_fwd(proj, colq, colk, colv, nheads, carry=None):
    s = proj.shape[0]
    t = _tile(s, ATT_TILE)
    scale = HEAD_DIM ** -0.5
    hg = SB_FWD_GROUP
    gw = hg * HEAD_DIM
    assert nheads % hg == 0 and colq % hg == 0 and colk % hg == 0 and colv % hg == 0

    def body(q_ref, k_ref, v_ref, o_ref):
        qi = pl.program_id(1)
        later = _later(t)
        before = _tri_mask(t, True)

        def tile(kj, carry, diagonal):
            off = pl.multiple_of(kj * t, t)
            heads = range(hg)
            z = [_dot_nt(_head(q_ref, hh), _head(k_ref, hh, pl.ds(off, t))) * scale for hh in heads]
            lb = [_log_sigmoid(z[hh]) for hh in heads]
            lm = [lb[hh] - z[hh] for hh in heads]
            if diagonal:
                lm = [jnp.where(before, lm[hh], 0.0) for hh in heads]
            parts = [jnp.concatenate(_split2(lm[hh]), axis=1) for hh in heads]
            suffix = [_dot(parts[hh], later) for hh in heads]
            a = [jnp.exp(lb[hh] + suffix[hh] + carry[hh][0]) for hh in heads]
            if diagonal:
                a = [jnp.where(before, a[hh], 0.0) for hh in heads]
            av = [_dot(a[hh].astype(BF16), _head(v_ref, hh, pl.ds(off, t))) for hh in heads]
            return tuple((carry[hh][0] + jnp.sum(lm[hh], axis=-1, keepdims=True), carry[hh][1] + av[hh])
                         for hh in heads)

        init = tuple((jnp.zeros((t, 1), F32), jnp.zeros((t, HEAD_DIM), F32)) for _ in range(hg))
        carry = tile(qi, init, True)
        carry = lax.fori_loop(1, qi + 1, lambda i, c: tile(qi - i, c, False), carry)
        for hh in range(hg):
            o_ref[:, hh * HEAD_DIM:(hh + 1) * HEAD_DIM] = carry[hh][1].astype(BF16)

    return _call(
        body,
        grid=(nheads // hg, s // t),
        in_specs=[pl.BlockSpec((t, gw), lambda h, i: (i, colq // hg + h)),
                  pl.BlockSpec((s, gw), lambda h, i: (0, colk // hg + h)),
                  pl.BlockSpec((s, gw), lambda h, i: (0, colv // hg + h))],
        out_specs=[pl.BlockSpec((t, gw), lambda h, i: (i, h))],
        out_shape=[jax.ShapeDtypeStruct((s, nheads * HEAD_DIM), BF16)],
        scratch_shapes=[],
        semantics=("parallel", "parallel"),
        name="sb_fwd",
        args=(proj, proj, proj),
        carry=carry,
    )


def _sb_bwd(proj, colq, colk, colv, do, nheads, carry=None):
    s = proj.shape[0]
    t = _tile(s, ATT_TILE)
    scale = HEAD_DIM ** -0.5
    hg = SB_BWD_GROUP
    gw = hg * HEAD_DIM
    assert nheads % hg == 0 and colq % hg == 0 and colk % hg == 0 and colv % hg == 0

    def body(q_ref, k_ref, v_ref, do_ref, dq_ref, dk_ref, dv_ref, g_s, beta_s):
        qi = pl.program_id(1)

        @pl.when(qi == 0)
        def _():
            dk_ref[...] = jnp.zeros_like(dk_ref)
            dv_ref[...] = jnp.zeros_like(dv_ref)

        later = _later(t)
        before = _tri_mask(t, True)

        def back(kj, carry, diagonal):
            off = pl.multiple_of(kj * t, t)
            heads = range(hg)
            rows = pl.ds(off, t)
            z = [_dot_nt(_head(q_ref, hh), _head(k_ref, hh, rows)) * scale for hh in heads]
            da = [_dot_nt(_head(do_ref, hh), _head(v_ref, hh, rows)) for hh in heads]
            lb = [_log_sigmoid(z[hh]) for hh in heads]
            lm = [lb[hh] - z[hh] for hh in heads]
            if diagonal:
                lm = [jnp.where(before, lm[hh], 0.0) for hh in heads]
            parts = [jnp.concatenate(_split2(lm[hh]), axis=1) for hh in heads]
            suffix = [_dot(parts[hh], later) for hh in heads]
            a = [jnp.exp(lb[hh] + suffix[hh] + carry[hh]) for hh in heads]
            if diagonal:
                a = [jnp.where(before, a[hh], 0.0) for hh in heads]
            dv = [_dot_tn(a[hh].astype(BF16), _head(do_ref, hh)) for hh in heads]
            for hh in heads:
                g_s[hh, :, rows] = a[hh] * da[hh]
                beta_s[hh, :, rows] = jnp.exp(lb[hh]).astype(BF16)
            for hh in heads:
                dv_ref[rows, hh * HEAD_DIM:(hh + 1) * HEAD_DIM] += dv[hh]
            return tuple(carry[hh] + jnp.sum(lm[hh], axis=-1, keepdims=True) for hh in heads)

        rc = back(qi, tuple(jnp.zeros((t, 1), F32) for _ in range(hg)), True)
        lax.fori_loop(1, qi + 1, lambda i, c: back(qi - i, c, False), rc)

        earlier = jnp.where(lax.broadcasted_iota(jnp.int32, (2 * t, t), 0) % t
                            < lax.broadcasted_iota(jnp.int32, (2 * t, t), 1), 1.0, 0.0).astype(BF16)

        def fwd(kj, carry, diagonal):
            off = pl.multiple_of(kj * t, t)
            heads = range(hg)
            rows = pl.ds(off, t)
            g = [g_s[hh, :, rows] for hh in heads]
            parts = [jnp.concatenate(_split2(g[hh]), axis=1) for hh in heads]
            gsum = [_dot(parts[hh], earlier) + carry[hh][0] for hh in heads]
            dz = []
            for hh in heads:
                beta = beta_s[hh, :, rows].astype(F32)
                d = g[hh] * (1.0 - beta) - gsum[hh] * beta
                if diagonal:
                    d = jnp.where(before, d, 0.0)
                dz.append(d.astype(BF16))
            dk = [_dot_tn(dz[hh], _head(q_ref, hh)) * scale for hh in heads]
            dq = [_dot(dz[hh], _head(k_ref, hh, rows)) * scale for hh in heads]
            for hh in heads:
                dk_ref[rows, hh * HEAD_DIM:(hh + 1) * HEAD_DIM] += dk[hh]
            return tuple((carry[hh][0] + jnp.sum(g[hh], axis=-1, keepdims=True), carry[hh][1] + dq[hh])
                         for hh in heads)

        init = tuple((jnp.zeros((t, 1), F32), jnp.zeros((t, HEAD_DIM), F32)) for _ in range(hg))
        carry = lax.fori_loop(0, qi, lambda kj, c: fwd(kj, c, False), init)
        carry = fwd(qi, carry, True)
        for hh in range(hg):
            dq_ref[:, hh * HEAD_DIM:(hh + 1) * HEAD_DIM] = carry[hh][1]

    once = pl.Buffered(buffer_count=1)
    tile_spec = pl.BlockSpec((t, gw), lambda h, i: (i, h))
    full = pl.BlockSpec((s, gw), lambda h, i: (0, h), pipeline_mode=once)
    w = nheads * HEAD_DIM
    return _call(
        body,
        grid=(nheads // hg, s // t),
        in_specs=[pl.BlockSpec((t, gw), lambda h, i: (i, colq // hg + h)),
                  pl.BlockSpec((s, gw), lambda h, i: (0, colk // hg + h), pipeline_mode=once),
                  pl.BlockSpec((s, gw), lambda h, i: (0, colv // hg + h), pipeline_mode=once),
                  tile_spec],
        out_specs=[tile_spec, full, full],
        out_shape=[jax.ShapeDtypeStruct((s, w), F32)] * 3,
        scratch_shapes=[pltpu.VMEM((hg, t, s), F32), pltpu.VMEM((hg, t, s), BF16)],
        semantics=("arbitrary", "arbitrary"),
        name="sb_bwd",
        args=(proj, proj, proj, do),
        carry=carry,
    )


def _mem_fwd(qn, kn, mkv, nheads):
    s = qn.shape[0]
    mtok = kn.shape[0]
    t = _tile(s, ATT_TILE)
    scale = HEAD_DIM ** -0.5

    def body(q_ref, k_ref, v_ref, o_ref):
        sc = _dot_nt(q_ref[...], k_ref[...]) * scale
        p = jnp.exp(sc - jnp.max(sc, axis=-1, keepdims=True))
        p = p / jnp.sum(p, axis=-1, keepdims=True)
        o_ref[...] = _dot(p.astype(BF16), v_ref[...]).astype(BF16)

    return pl.pallas_call(
        body,
        grid=(nheads, s // t),
        in_specs=[pl.BlockSpec((t, HEAD_DIM), lambda h, i: (i, h)),
                  pl.BlockSpec((mtok, HEAD_DIM), lambda h, i: (0, h)),
                  pl.BlockSpec((mtok, HEAD_DIM), lambda h, i: (0, nheads + h))],
        out_specs=pl.BlockSpec((t, HEAD_DIM), lambda h, i: (i, h)),
        out_shape=jax.ShapeDtypeStruct((s, nheads * HEAD_DIM), BF16),
        compiler_params=_params("parallel", "parallel"),
        name="mem_fwd",
    )(qn, kn, mkv)


def _mem_bwd(qn, kn, mkv, do, nheads):
    s = qn.shape[0]
    mtok = kn.shape[0]
    t = _tile(s, ATT_TILE)
    scale = HEAD_DIM ** -0.5

    def body(q_ref, k_ref, v_ref, do_ref, dq_ref, dk_ref, dv_ref):
        @pl.when(pl.program_id(1) == 0)
        def _():
            dk_ref[...] = jnp.zeros_like(dk_ref)
            dv_ref[...] = jnp.zeros_like(dv_ref)

        q = q_ref[...]
        k = k_ref[...]
        do_ = do_ref[...]
        sc = _dot_nt(q, k) * scale
        p = jnp.exp(sc - jnp.max(sc, axis=-1, keepdims=True))
        p = p / jnp.sum(p, axis=-1, keepdims=True)
        dp = _dot_nt(do_, v_ref[...])
        ds = (p * (dp - jnp.sum(p * dp, axis=-1, keepdims=True))).astype(BF16)
        dq_ref[...] = _dot(ds, k) * scale
        dk_ref[...] += _dot_tn(ds, q) * scale
        dv_ref[...] += _dot_tn(p.astype(BF16), do_)

    tile = pl.BlockSpec((t, HEAD_DIM), lambda h, i: (i, h))
    kspec = pl.BlockSpec((mtok, HEAD_DIM), lambda h, i: (0, h))
    w = nheads * HEAD_DIM
    return pl.pallas_call(
        body,
        grid=(nheads, s // t),
        in_specs=[tile, kspec, pl.BlockSpec((mtok, HEAD_DIM), lambda h, i: (0, nheads + h)), tile],
        out_specs=[tile, kspec, kspec],
        out_shape=[jax.ShapeDtypeStruct((s, w), F32), jax.ShapeDtypeStruct((mtok, w), F32),
                   jax.ShapeDtypeStruct((mtok, w), F32)],
        compiler_params=_params("arbitrary", "arbitrary"),
        name="mem_bwd",
    )(qn, kn, mkv, do)


def _merge_fwd(p0, p1, p2, gates, b_gate):
    s, d = p0.shape
    tm, tn = _tile(s, ROW_TILE), _tile(d, COL_TILE)
    nj = d // tn

    def body(p0_ref, p1_ref, p2_ref, ga_ref, gb_ref, gc_ref, b_ref, o_ref):
        acc = jnp.zeros((tm, tn), F32)
        for b, (p_ref, g_ref) in enumerate(((p0_ref, ga_ref), (p1_ref, gb_ref), (p2_ref, gc_ref))):
            gate = jax.nn.sigmoid(g_ref[...].astype(F32) + b_ref[b:b + 1, :])
            acc = acc + gate * p_ref[...]
        o_ref[...] = acc.astype(BF16)

    blk = pl.BlockSpec((tm, tn), lambda i, j: (i, j))
    return pl.pallas_call(
        body,
        grid=(s // tm, nj),
        in_specs=[blk] * 6 + [pl.BlockSpec((3, tn), lambda i, j: (0, j))],
        out_specs=blk,
        out_shape=jax.ShapeDtypeStruct((s, d), BF16),
        compiler_params=_params("parallel", "parallel"),
        name="merge_fwd",
    )(p0, p1, p2, *gates, b_gate)


def _merge_bwd(dmerged, p0, p1, p2, gates, b_gate):
    s, d = p0.shape
    tm, tn = _tile(s, ROW_TILE), _tile(d, COL_TILE)
    nj = d // tn

    def body(dm_ref, p0_ref, p1_ref, p2_ref, ga_ref, gb_ref, gc_ref, b_ref,
             d0_ref, d1_ref, d2_ref, dga_ref, dgb_ref, dgc_ref, db_ref):
        dm = dm_ref[...].astype(F32)
        parts = []
        for b, (p_ref, g_ref, dp_ref, dg_ref) in enumerate(((p0_ref, ga_ref, d0_ref, dga_ref),
                                                            (p1_ref, gb_ref, d1_ref, dgb_ref),
                                                            (p2_ref, gc_ref, d2_ref, dgc_ref))):
            gate = jax.nn.sigmoid(g_ref[...].astype(F32) + b_ref[b:b + 1, :])
            dp_ref[...] = (dm * gate).astype(BF16)
            dgate = dm * p_ref[...] * gate * (1.0 - gate)
            dg_ref[...] = dgate.astype(BF16)
            parts.append(jnp.sum(dgate, axis=0, keepdims=True))
        part = jnp.concatenate(parts, axis=0)

        @pl.when(pl.program_id(1) == 0)
        def _():
            db_ref[...] = part

        @pl.when(pl.program_id(1) > 0)
        def _():
            db_ref[...] += part

    blk = pl.BlockSpec((tm, tn), lambda j, i: (i, j))
    bias = pl.BlockSpec((3, tn), lambda j, i: (0, j))
    return pl.pallas_call(
        body,
        grid=(nj, s // tm),
        in_specs=[blk] * 7 + [bias],
        out_specs=[blk] * 6 + [bias],
        out_shape=[jax.ShapeDtypeStruct((s, d), BF16)] * 6 + [jax.ShapeDtypeStruct((3, d), F32)],
        compiler_params=_params("parallel", "arbitrary"),
        name="merge_bwd",
    )(dmerged, p0, p1, p2, *gates, b_gate)


def _shift_down(v, n):
    rows = lax.broadcasted_iota(jnp.int32, v.shape, 0)
    return jnp.where(rows >= n, pltpu.roll(v, n, 0), 0.0)


def _shift_up(v, n):
    s = v.shape[0]
    rows = lax.broadcasted_iota(jnp.int32, v.shape, 0)
    return jnp.where(rows < s - n, pltpu.roll(v, s - n, 0), 0.0)


def _conv(v, w_ref, b_ref):
    taps = w_ref.shape[0]
    out = v * w_ref[taps - 1:taps, :] + b_ref[...]
    for n in range(1, taps):
        out = out + _shift_down(v, n) * w_ref[taps - 1 - n:taps - n, :]
    return out


def _conv_act_fwd(up, conv_w, conv_b):
    s, f2 = up.shape
    f = f2 // 2
    tn = LANES
    nj = f // tn
    taps = conv_w.shape[0]

    def body(ug_ref, uv_ref, wg_ref, wv_ref, bg_ref, bv_ref, o_ref):
        cg = _conv(ug_ref[...].astype(F32), wg_ref, bg_ref)
        cv = _conv(uv_ref[...].astype(F32), wv_ref, bv_ref)
        o_ref[...] = (cg * jax.nn.sigmoid(cg) * cv).astype(BF16)

    return pl.pallas_call(
        body,
        grid=(nj,),
        in_specs=[pl.BlockSpec((s, tn), lambda j: (0, j)), pl.BlockSpec((s, tn), lambda j: (0, nj + j)),
                  pl.BlockSpec((taps, tn), lambda j: (0, j)), pl.BlockSpec((taps, tn), lambda j: (0, nj + j)),
                  pl.BlockSpec((1, tn), lambda j: (0, j)), pl.BlockSpec((1, tn), lambda j: (0, nj + j))],
        out_specs=pl.BlockSpec((s, tn), lambda j: (0, j)),
        out_shape=jax.ShapeDtypeStruct((s, f), BF16),
        compiler_params=_params("parallel"),
        name="conv_act_fwd",
    )(up, up, conv_w, conv_w, conv_b, conv_b)


def _conv_act_bwd(up, conv_w, conv_b, dact):
    s, f2 = up.shape
    f = f2 // 2
    tn = LANES
    nj = f // tn
    taps = conv_w.shape[0]

    def half(v, du, w_ref, dup_ref, dw_ref, db_ref):
        dup = du * w_ref[taps - 1:taps, :]
        rows = [None] * taps
        rows[taps - 1] = jnp.sum(du * v, axis=0, keepdims=True)
        for n in range(1, taps):
            dup = dup + _shift_up(du, n) * w_ref[taps - 1 - n:taps - n, :]
            rows[taps - 1 - n] = jnp.sum(du * _shift_down(v, n), axis=0, keepdims=True)
        dup_ref[...] = dup.astype(BF16)
        dw_ref[...] = jnp.concatenate(rows, axis=0)
        db_ref[...] = jnp.sum(du, axis=0, keepdims=True)

    def body(ug_ref, uv_ref, wg_ref, wv_ref, bg_ref, bv_ref, da_ref,
             dug_ref, duv_ref, dwg_ref, dwv_ref, dbg_ref, dbv_ref):
        ug = ug_ref[...].astype(F32)
        uv = uv_ref[...].astype(F32)
        cg = _conv(ug, wg_ref, bg_ref)
        cv = _conv(uv, wv_ref, bv_ref)
        da = da_ref[...].astype(F32)
        sg = jax.nn.sigmoid(cg)
        dcv = da * cg * sg
        dcg = da * cv * (sg + cg * sg * (1.0 - sg))
        half(ug, dcg, wg_ref, dug_ref, dwg_ref, dbg_ref)
        half(uv, dcv, wv_ref, duv_ref, dwv_ref, dbv_ref)

    lo = lambda rows: pl.BlockSpec((rows, tn), lambda j: (0, j))
    hi = lambda rows: pl.BlockSpec((rows, tn), lambda j: (0, nj + j))
    return pl.pallas_call(
        body,
        grid=(nj,),
        in_specs=[lo(s), hi(s), lo(taps), hi(taps), lo(1), hi(1), lo(s)],
        out_specs=[lo(s), lo(s), lo(taps), lo(taps), lo(1), lo(1)],
        out_shape=[jax.ShapeDtypeStruct((s, f), BF16)] * 2 + [jax.ShapeDtypeStruct((taps, f), F32)] * 2
        + [jax.ShapeDtypeStruct((1, f), F32)] * 2,
        compiler_params=_params("parallel"),
        name="conv_act_bwd",
    )(up, up, conv_w, conv_w, conv_b, conv_b, dact)


def _loss_grad(y, target):
    s, d = y.shape
    tm = _tile(s, ROW_TILE)

    def body(y_ref, t_ref, dy_ref, dyb_ref, l_ref):
        e = y_ref[...] - t_ref[...]
        dy = e * (1.0 / d)
        dy_ref[...] = dy
        dyb_ref[...] = dy.astype(BF16)
        tot = jnp.sum(jnp.sum(e * e, axis=-1, keepdims=True), axis=0, keepdims=True)
        l_ref[...] = jnp.broadcast_to(tot, (8, LANES))

    row = pl.BlockSpec((tm, d), lambda i: (i, 0))
    return pl.pallas_call(
        body,
        grid=(s // tm,),
        in_specs=[row, row],
        out_specs=[row, row, pl.BlockSpec((8, LANES), lambda i: (i, 0))],
        out_shape=[jax.ShapeDtypeStruct((s, d), F32), jax.ShapeDtypeStruct((s, d), BF16),
                   jax.ShapeDtypeStruct((s // tm * 8, LANES), F32)],
        compiler_params=_params("parallel"),
        name="loss_grad",
    )(y, target)


def _row_tile(rows, row_bytes, budget):
    if rows * row_bytes <= budget or rows % 8:
        return rows
    best = 8
    for t in range(8, rows, 8):
        if rows % t == 0 and t * row_bytes <= budget:
            best = t
    return best


def _adamw(w, g, m, v, name):
    r, c = w.shape
    tr = _row_tile(r, c * 4, ADAM_BLOCK_BYTES)

    def body(w_ref, g_ref, m_ref, v_ref, d_ref, mo_ref, vo_ref):
        gg = g_ref[...]
        m_new = ADAM_B1 * m_ref[...] + (1.0 - ADAM_B1) * gg
        v_new = ADAM_B2 * v_ref[...] + (1.0 - ADAM_B2) * (gg * gg)
        m_hat = m_new / (1.0 - ADAM_B1 ** ADAM_STEP)
        v_hat = v_new / (1.0 - ADAM_B2 ** ADAM_STEP)
        d_ref[...] = -ADAM_LR * (m_hat / (jnp.sqrt(v_hat) + ADAM_EPS) + ADAM_WD * w_ref[...])
        mo_ref[...] = m_new
        vo_ref[...] = v_new

    blk = pl.BlockSpec((tr, c), lambda i: (i, 0))
    return pl.pallas_call(
        body,
        grid=(r // tr,),
        in_specs=[blk] * 4,
        out_specs=[blk] * 3,
        out_shape=[jax.ShapeDtypeStruct((r, c), F32)] * 3,
        compiler_params=_params("parallel"),
        name=name,
    )(w, g, m, v)


def _add_sibling(g, r1, core, name):
    _, _, h, c = g.shape
    th = _row_tile(h, c * 2, ADAM_BLOCK_BYTES)

    def body(core_ref, g_ref, r_ref, o_ref):
        o_ref[...] = (g_ref[...].astype(F32) + r_ref[...].astype(F32)).astype(BF16)

    return pl.pallas_call(
        body,
        grid_spec=pltpu.PrefetchScalarGridSpec(
            num_scalar_prefetch=1,
            grid=(N_CHIPS, h // th),
            in_specs=[pl.BlockSpec((None, None, th, c), lambda j, i, core_ref: (j, core_ref[0], i, 0)),
                      pl.BlockSpec((None, th, c), lambda j, i, core_ref: (j, i, 0))],
            out_specs=pl.BlockSpec((None, th, c), lambda j, i, core_ref: (j, i, 0)),
        ),
        out_shape=jax.ShapeDtypeStruct((N_CHIPS, h, c), BF16),
        compiler_params=_params("parallel", "parallel"),
        name=name,
    )(core, g, r1)


def _add_chips(hsum, r2, chip_core, name):
    _, h, c = hsum.shape
    th = _row_tile(h, c * 4, ADAM_BLOCK_BYTES)

    def body(sel_ref, own_ref, r_ref, o_ref):
        acc = own_ref[...].astype(F32)
        for j in range(N_CHIPS - 1):
            acc = acc + r_ref[j].astype(F32)
        o_ref[...] = acc

    return pl.pallas_call(
        body,
        grid_spec=pltpu.PrefetchScalarGridSpec(
            num_scalar_prefetch=1,
            grid=(h // th,),
            in_specs=[pl.BlockSpec((None, th, c), lambda i, sel_ref: (sel_ref[0], i, 0)),
                      pl.BlockSpec((N_CHIPS - 1, th, c), lambda i, sel_ref: (0, i, 0))],
            out_specs=pl.BlockSpec((None, th, c), lambda i, sel_ref: (sel_ref[1], i, 0)),
        ),
        out_shape=jax.ShapeDtypeStruct((2, h, c), F32),
        compiler_params=_params("parallel"),
        name=name,
    )(chip_core, hsum, r2)


def _sum_devices(parts):
    _, r, c = parts.shape

    def body(p_ref, o_ref):
        acc = p_ref[0]
        for j in range(1, N_DEV):
            acc = acc + p_ref[j]
        o_ref[...] = acc

    return pl.pallas_call(
        body,
        out_shape=jax.ShapeDtypeStruct((r, c), F32),
        compiler_params=pltpu.CompilerParams(vmem_limit_bytes=VMEM_LIMIT_BYTES),
        name="sum_devices",
    )(parts)


def _swap_halves(grads, name):
    n = len(grads)

    def body(*refs):
        ins, outs = refs[:n], refs[n:2 * n]
        send_sems, recv_sems = refs[2 * n:]
        x, y, c, _ = _place()
        copies = [_remote(ins[i].at[:, 1 - c], outs[i], send_sems.at[i], recv_sems.at[i], (x, y, 1 - c))
                  for i in range(n)]
        for cp in copies:
            cp.start()
        for cp in copies:
            cp.wait()

    return pl.pallas_call(
        body,
        in_specs=[ANY] * n,
        out_specs=[ANY] * n,
        out_shape=[jax.ShapeDtypeStruct((g.shape[0],) + g.shape[2:], g.dtype) for g in grads],
        scratch_shapes=[pltpu.SemaphoreType.DMA((n,)), pltpu.SemaphoreType.DMA((n,))],
        name=name,
    )(*grads)


def _join_halves(finals):
    n = len(finals)

    def body(*refs):
        outs = refs[n:2 * n]
        send_sems, recv_sems = refs[2 * n:]
        x, y, c, _ = _place()
        sends = [_remote(outs[i].at[c], outs[i].at[c], send_sems.at[i], recv_sems.at[i], (x, y, 1 - c))
                 for i in range(n)]
        for cp in sends:
            cp.start()
        for i in range(n):
            sends[i].wait_send()
            other = outs[i].at[1 - c]
            _remote(other, other, send_sems.at[i], recv_sems.at[i], (x, y, 1 - c)).wait_recv()

    return pl.pallas_call(
        body,
        in_specs=[ANY] * n,
        out_specs=[ANY] * n,
        out_shape=[jax.ShapeDtypeStruct(f.shape, f.dtype) for f in finals],
        input_output_aliases={i: i for i in range(n)},
        scratch_shapes=[pltpu.SemaphoreType.DMA((n,)), pltpu.SemaphoreType.DMA((n,))],
        name="join_halves",
    )(*finals)


def _gather_small(vec):
    k = N_DEV - 1

    def body(v_ref, o_ref, send_sems, recv_sems, local_sem):
        x, y, c, _ = _place()
        me = 4 * x + 2 * y + c
        local = pltpu.make_async_copy(v_ref, o_ref.at[me], local_sem)
        local.start()
        peers = [(x ^ (r >> 2 & 1), y ^ (r >> 1 & 1), c ^ (r & 1)) for r in range(1, N_DEV)]
        sends = [_remote(v_ref, o_ref.at[me], send_sems.at[j], recv_sems.at[j], p) for j, p in enumerate(peers)]
        for cp in sends:
            cp.start()
        for j, (px, py, pc) in enumerate(peers):
            sends[j].wait_send()
            blk = o_ref.at[4 * px + 2 * py + pc]
            _remote(blk, blk, send_sems.at[j], recv_sems.at[j], (px, py, pc)).wait_recv()
        local.wait()

    return pl.pallas_call(
        body,
        in_specs=[ANY],
        out_specs=ANY,
        out_shape=jax.ShapeDtypeStruct((N_DEV,) + vec.shape, vec.dtype),
        scratch_shapes=[pltpu.SemaphoreType.DMA((k,)), pltpu.SemaphoreType.DMA((k,)), pltpu.SemaphoreType.DMA(())],
        name="gather_small",
    )(vec)


W_IN_GATES = ("w_in_g0", "w_in_g1", "w_in_g2")
ROW_SHARDED = ("w_in_a", "w_in_f") + W_IN_GATES + ("w_mem_kv", "w_out", "w_down")
COL_SHARDED = ("w_br_fox", "w_br_sb", "w_br_mem", "w_up")
BIG = ROW_SHARDED + COL_SHARDED


def _whole(name, gathered, shard, chip):
    a = lax.dynamic_update_slice(gathered, shard[None], (chip, 0, 0))
    if name in ROW_SHARDED:
        return a.reshape(N_CHIPS * a.shape[1], a.shape[2])
    return a.transpose(1, 0, 2).reshape(a.shape[1], N_CHIPS * a.shape[2])


def _by_shard(name, grad):
    if name in ROW_SHARDED:
        a = grad.reshape(N_CHIPS, grad.shape[0] // N_CHIPS, grad.shape[1])
    else:
        a = grad.reshape(grad.shape[0], N_CHIPS, grad.shape[1] // N_CHIPS).transpose(1, 0, 2)
    return a.reshape(N_CHIPS, 2, a.shape[1] // 2, a.shape[2])


def _sibling_sums(names, g, core, tag):
    split = [_by_shard(name, g[name]) for name in names]
    theirs = _swap_halves(split, "swap_halves_" + tag)
    return [_add_sibling(a, r, core, "add_sibling_" + name) for name, a, r in zip(names, split, theirs)]


GATHER_FIRST = ("w_in_a", "w_in_f", "w_mem_kv")
GATHER_MIX = ("w_down", "w_out", "w_br_fox", "w_br_sb", "w_br_mem")
REDUCE_FFN = ("w_down", "w_up")
REDUCE_MIX = ("w_out", "w_br_fox", "w_br_sb", "w_br_mem") + W_IN_GATES
REDUCE_IN = ("w_in_a", "w_in_f", "w_mem_kv")


def _local_step(x, mem, target, w, shard, chip, core, chip_core):
    d = x.shape[1]
    nf = shard["w_br_fox"].shape[0] // HEAD_DIM
    nsb = shard["w_br_sb"].shape[0] // HEAD_DIM
    nm = shard["w_br_mem"].shape[0] // HEAD_DIM
    w = dict(w)

    def take(names, gathered):
        for name, a in zip(names, gathered):
            w[name] = _whole(name, a, shard[name], chip)

    take(GATHER_FIRST, _exchange(_Gather([shard[name] for name in GATHER_FIRST]), "gather_first"))
    fq, fk, fv = 0, nf, 2 * nf
    sq, sk, sv = 3 * nf, 3 * nf + nsb, 3 * nf + 2 * nsb
    mq = 3 * nf + 3 * nsb

    h, rstd1 = _rms_fwd(x, w["g_mix"], "rms_mix_fwd")
    proj, moved = _mm(h, w["w_in_a"], "nn", BF16, "proj_att", carry=_Gather([shard[name] for name in W_IN_GATES[:2]]))
    take(W_IN_GATES[:2], moved)
    f_logit = _mm(h, w["w_in_f"], "nn", F32, "proj_forget")
    gate0, moved = _mm(h, w["w_in_g0"], "nn", BF16, "proj_gate0", carry=_Gather([shard[W_IN_GATES[2]]]))
    take(W_IN_GATES[2:], moved)
    gates = (gate0, _mm(h, w["w_in_g1"], "nn", BF16, "proj_gate1"), _mm(h, w["w_in_g2"], "nn", BF16, "proj_gate2"))
    c_sum = _forget_fwd(f_logit, w["b_forget"])
    c_t = c_sum[:, :nf].T
    c_col, c_row = c_t[:, :, None], c_t[:, None, :]
    qn = _headnorm_fwd(proj, fq, nf, w["g_q_fox"], "fox_qnorm_fwd")
    kn = _headnorm_fwd(proj, fk, nf, w["g_k_fox"], "fox_knorm_fwd")
    (o_fox, o_fox32, lse), moved = _fox_fwd(qn, kn, proj, fv, c_col, c_row, nf,
                                            carry=_Gather([shard[name] for name in GATHER_MIX]))
    take(GATHER_MIX, moved)
    (o_sb,), moved = _sb_fwd(proj, sq, sk, sv, nsb, carry=_Gather([shard["w_up"]]))
    take(("w_up",), moved)
    memn, rstd_m = _rms_fwd(mem, w["g_mem"], "rms_mem_fwd")
    mkv = _mm(memn, w["w_mem_kv"], "nn", BF16, "mem_kv")
    kmn = _headnorm_fwd(mkv, 0, nm, w["g_k_mem"], "mem_knorm_fwd")
    qmn = _headnorm_fwd(proj, mq, nm, w["g_q_mem"], "mem_qnorm_fwd")
    o_mem = _mem_fwd(qmn, kmn, mkv, nm)
    p0 = _mm(o_fox, w["w_br_fox"], "nn", F32, "branch_fox")
    p1 = _mm(o_sb, w["w_br_sb"], "nn", F32, "branch_sb")
    p2 = _mm(o_mem, w["w_br_mem"], "nn", F32, "branch_mem")
    merged = _merge_fwd(p0, p1, p2, gates, w["b_gate"])
    x1 = _mm(merged, w["w_out"], "nn", F32, "out_proj", residual=x)
    h2, rstd2 = _rms_fwd(x1, w["g_ffn"], "rms_ffn_fwd")
    up = _mm(h2, w["w_up"], "nn", BF16, "ffn_up")
    act = _conv_act_fwd(up, w["conv_w"], w["conv_b"])
    y = _mm(act, w["w_down"], "nn", F32, "ffn_down", residual=x1)
    dy, dyb, lparts = _loss_grad(y, target)
    loss = (0.5 / d) * jnp.sum(lparts[::8, 0])

    g = {}
    dact = _mm(dyb, w["w_down"], "nt", BF16, "ffn_down_dx")
    g["w_down"] = _mm(act, dyb, "tn", BF16, "ffn_down_dw")
    dug, duv, dwg, dwv, dbg, dbv = _conv_act_bwd(up, w["conv_w"], w["conv_b"], dact)
    dup = jnp.concatenate([dug, duv], axis=1)
    g["conv_w"] = jnp.concatenate([dwg, dwv], axis=1)
    g["conv_b"] = jnp.concatenate([dbg, dbv], axis=1)
    dh2 = _mm(dup, w["w_up"], "nt", BF16, "ffn_up_dx")
    g["w_up"] = _mm(h2, dup, "tn", BF16, "ffn_up_dw")
    sums_ffn = _sibling_sums(REDUCE_FFN, g, core, "ffn")
    dx1, dx1b, g["g_ffn"] = _rms_bwd(dh2, x1, rstd2, w["g_ffn"], dy, "rms_ffn_bwd")
    dmerged = _mm(dx1b, w["w_out"], "nt", BF16, "out_proj_dx")
    g["w_out"] = _mm(merged, dx1b, "tn", BF16, "out_proj_dw")
    dp0, dp1, dp2, dga, dgb, dgc, g["b_gate"] = _merge_bwd(dmerged, p0, p1, p2, gates, w["b_gate"])
    dgates = (dga, dgb, dgc)
    for name, dgate in zip(W_IN_GATES, dgates):
        g[name] = _mm(h, dgate, "tn", BF16, name + "_dw")
    do_fox = _mm(dp0, w["w_br_fox"], "nt", BF16, "branch_fox_dx")
    do_sb = _mm(dp1, w["w_br_sb"], "nt", BF16, "branch_sb_dx")
    do_mem = _mm(dp2, w["w_br_mem"], "nt", BF16, "branch_mem_dx")
    g["w_br_fox"] = _mm(o_fox, dp0, "tn", BF16, "branch_fox_dw")
    g["w_br_sb"] = _mm(o_sb, dp1, "tn", BF16, "branch_sb_dw")
    g["w_br_mem"] = _mm(o_mem, dp2, "tn", BF16, "branch_mem_dw")
    sums_mix = _sibling_sums(REDUCE_MIX, g, core, "mix")

    (dqn, dkn, dfv, drs, dcs), others_ffn = _fox_bwd(qn, kn, proj, fv, c_col, c_row, o_fox32, do_fox, lse, nf,
                                                     carry=_Scatter(sums_ffn))
    dfq, g["g_q_fox"] = _headnorm_bwd(dqn, proj, fq, nf, w["g_q_fox"], "fox_qnorm_bwd")
    dfk, g["g_k_fox"] = _headnorm_bwd(dkn, proj, fk, nf, w["g_k_fox"], "fox_knorm_bwd")
    dc = jnp.pad((drs[:, :, 0] - dcs[:, 0, :]).T, ((0, 0), (0, LANES - nf)))
    df, g["b_forget"] = _forget_bwd(dc, f_logit, w["b_forget"])
    (dsq, dsk, dsv), others_mix = _sb_bwd(proj, sq, sk, sv, do_sb, nsb, carry=_Scatter(sums_mix))
    dqmn, dkmn, dvm = _mem_bwd(qmn, kmn, mkv, do_mem, nm)
    dmq, g["g_q_mem"] = _headnorm_bwd(dqmn, proj, mq, nm, w["g_q_mem"], "mem_qnorm_bwd")
    dkm, g["g_k_mem"] = _headnorm_bwd(dkmn, mkv, 0, nm, w["g_k_mem"], "mem_knorm_bwd")
    dmkv = jnp.concatenate([dkm, dvm.astype(BF16)], axis=1)
    g["w_mem_kv"] = _mm(memn, dmkv, "tn", BF16, "mem_kv_dw")
    dmemn = _mm(dmkv, w["w_mem_kv"], "nt", BF16, "mem_kv_dx")
    _, _, g["g_mem"] = _rms_bwd(dmemn, mem, rstd_m, w["g_mem"], None, "rms_mem_bwd")

    dproj = jnp.concatenate([dfq, dfk, dfv.astype(BF16), dsq.astype(BF16), dsk.astype(BF16), dsv.astype(BF16), dmq],
                            axis=1)
    dfb = df.astype(BF16)
    g["w_in_a"] = _mm(h, dproj, "tn", BF16, "proj_att_dw")
    g["w_in_f"] = _mm(h, dfb, "tn", BF16, "proj_forget_dw")
    sums_in = _sibling_sums(REDUCE_IN, g, core, "in")
    dh, others_in = _mm(dproj, w["w_in_a"], "nt", F32, "proj_att_dx", carry=_Scatter(sums_in))
    dh = _mm(dfb, w["w_in_f"], "nt", F32, "proj_forget_dx", residual=dh)
    for name, dgate in zip(W_IN_GATES, dgates):
        dh = _mm(dgate, w[name], "nt", F32, name + "_dx", residual=dh)
    grad_x, _, g["g_mix"] = _rms_bwd(dh, x, rstd1, w["g_mix"], dx1, "rms_mix_bwd")

    names = REDUCE_FFN + REDUCE_MIX + REDUCE_IN
    finals = [_add_chips(own, theirs, chip_core, "add_chips_" + name)
              for name, own, theirs in zip(names, sums_ffn + sums_mix + sums_in, others_ffn + others_mix + others_in)]
    summed = {name: a.reshape(2 * a.shape[1], a.shape[2]) for name, a in zip(names, _join_halves(finals))}
    return loss, grad_x, g, summed


SMALL = ("g_mix", "b_forget", "g_q_fox", "g_k_fox", "g_mem", "g_q_mem", "g_k_mem", "b_gate", "g_ffn", "conv_w",
         "conv_b")
SMALL_SHARDED = ("b_gate", "conv_w")
PACK_ROWS = 8


def _pack(arrs):
    flat = jnp.concatenate([a.reshape(-1) for a in arrs])
    unit = PACK_ROWS * LANES
    flat = jnp.pad(flat, (0, -flat.shape[0] % unit))
    return flat.reshape(-1, LANES)


def _unpack(packed, shapes):
    flat = packed.reshape(-1)
    out, at = [], 0
    for s in shapes:
        n = 1
        for dim in s:
            n *= dim
        out.append(flat[at:at + n].reshape(s))
        at += n
    return out


def kernel(x, mem, g_mix, w_in, b_forget, g_q_fox, g_k_fox, g_mem, w_mem_kv, g_q_mem, g_k_mem, w_br_fox, w_br_sb, w_br_mem, b_gate, w_out, g_ffn, w_up, conv_w, conv_b, w_down, loss_target, m_g_mix, m_w_in, m_b_forget, m_g_q_fox, m_g_k_fox, m_g_mem, m_w_mem_kv, m_g_q_mem, m_g_k_mem, m_w_br_fox, m_w_br_sb, m_w_br_mem, m_b_gate, m_w_out, m_g_ffn, m_w_up, m_conv_w, m_conv_b, m_w_down, v_g_mix, v_w_in, v_b_forget, v_g_q_fox, v_g_k_fox, v_g_mem, v_w_mem_kv, v_g_q_mem, v_g_k_mem, v_w_br_fox, v_w_br_sb, v_w_br_mem, v_b_gate, v_w_out, v_g_ffn, v_w_up, v_conv_w, v_conv_b, v_w_down):
    given = dict(g_mix=g_mix, w_in=w_in, b_forget=b_forget, g_q_fox=g_q_fox, g_k_fox=g_k_fox, g_mem=g_mem,
                 w_mem_kv=w_mem_kv, g_q_mem=g_q_mem, g_k_mem=g_k_mem, w_br_fox=w_br_fox, w_br_sb=w_br_sb,
                 w_br_mem=w_br_mem, b_gate=b_gate, w_out=w_out, g_ffn=g_ffn, w_up=w_up, conv_w=conv_w, conv_b=conv_b,
                 w_down=w_down)
    m_in = dict(g_mix=m_g_mix, w_in=m_w_in, b_forget=m_b_forget, g_q_fox=m_g_q_fox, g_k_fox=m_g_k_fox, g_mem=m_g_mem,
                w_mem_kv=m_w_mem_kv, g_q_mem=m_g_q_mem, g_k_mem=m_g_k_mem, w_br_fox=m_w_br_fox, w_br_sb=m_w_br_sb,
                w_br_mem=m_w_br_mem, b_gate=m_b_gate, w_out=m_w_out, g_ffn=m_g_ffn, w_up=m_w_up, conv_w=m_conv_w,
                conv_b=m_conv_b, w_down=m_w_down)
    v_in = dict(g_mix=v_g_mix, w_in=v_w_in, b_forget=v_b_forget, g_q_fox=v_g_q_fox, g_k_fox=v_g_k_fox, g_mem=v_g_mem,
                w_mem_kv=v_w_mem_kv, g_q_mem=v_g_q_mem, g_k_mem=v_g_k_mem, w_br_fox=v_w_br_fox, w_br_sb=v_w_br_sb,
                w_br_mem=v_w_br_mem, b_gate=v_b_gate, w_out=v_w_out, g_ffn=v_g_ffn, w_up=v_w_up, conv_w=v_conv_w,
                conv_b=v_conv_b, w_down=v_w_down)
    layered = {k: a.ndim == 3 for k, a in given.items()}
    drop = lambda a: a[0] if a.ndim == 3 else a
    given = {k: drop(a) for k, a in given.items()}
    m_in = {k: drop(a) for k, a in m_in.items()}
    v_in = {k: drop(a) for k, a in v_in.items()}

    xi, yi, ci = lax.axis_index("x"), lax.axis_index("y"), lax.axis_index("c")
    chip = (2 * xi + yi).astype(jnp.int32)
    core_arr = ci.astype(jnp.int32).reshape(1)
    chip_core = jnp.stack([chip, ci.astype(jnp.int32)])

    nf = given["b_forget"].shape[1]
    cut = 3 * given["w_br_fox"].shape[0]

    d_model = given["w_out"].shape[1]
    gate0 = given["w_in"].shape[1] - len(W_IN_GATES) * d_model
    shard = {
        "w_in_a": jnp.concatenate([given["w_in"][:, :cut], given["w_in"][:, cut + nf:gate0]], axis=1).astype(BF16),
        "w_in_f": jnp.pad(given["w_in"][:, cut:cut + nf], ((0, 0), (0, LANES - nf))).astype(BF16),
    }
    for b, name in enumerate(W_IN_GATES):
        shard[name] = given["w_in"][:, gate0 + b * d_model:gate0 + (b + 1) * d_model].astype(BF16)
    for name in BIG:
        if name not in shard:
            shard[name] = given[name].astype(BF16)
    w = {}
    small_shapes = [given[name].shape for name in SMALL_SHARDED]
    small_parts = _gather_small(_pack([given[name] for name in SMALL_SHARDED]))[0::2]
    per_chip = [_unpack(small_parts[j], small_shapes) for j in range(N_CHIPS)]
    for k, name in enumerate(SMALL_SHARDED):
        w[name] = jnp.concatenate([per_chip[j][k] for j in range(N_CHIPS)], axis=1)
    for name in SMALL:
        if name not in SMALL_SHARDED:
            w[name] = given[name]
    w["b_forget"] = jnp.pad(given["b_forget"], ((0, 0), (0, LANES - nf)))

    loss, grad_x, g, summed = _local_step(x[0], mem[0], loss_target[0], w, shard, chip, core_arr, chip_core)
    loss = lax.psum(loss, ("x", "y", "c"))
    grads = {name: summed[name] for name in BIG if name in given}
    grads["w_in"] = jnp.concatenate([summed["w_in_a"][:, :cut], summed["w_in_f"][:, :nf], summed["w_in_a"][:, cut:]]
                                    + [summed[name] for name in W_IN_GATES], axis=1)

    g["b_forget"] = g["b_forget"][:, :nf]
    small_full_shapes = [g[name].shape for name in SMALL]
    small_sum = _unpack(_sum_devices(_gather_small(_pack([g[name] for name in SMALL]))), small_full_shapes)
    for name, a in zip(SMALL, small_sum):
        if name in SMALL_SHARDED:
            width = given[name].shape[1]
            a = lax.dynamic_slice_in_dim(a, chip * width, width, axis=1)
        grads[name] = a

    delta, new_m, new_v = {}, {}, {}
    for name in WEIGHTS:
        if name not in SMALL:
            delta[name], new_m[name], new_v[name] = _adamw(given[name], grads[name], m_in[name], v_in[name],
                                                           "adamw_" + name)
    shapes = [given[name].shape for name in SMALL]
    packed = [_pack([src[name] for name in SMALL]) for src in (given, grads, m_in, v_in)]
    for dst, res in zip((delta, new_m, new_v), _adamw(*packed, "adamw_small")):
        for name, a in zip(SMALL, _unpack(res, shapes)):
            dst[name] = a

    out = [loss, grad_x[None]]
    for src in (grads, delta, new_m, new_v):
        out.extend(src[name][None] if layered[name] else src[name] for name in WEIGHTS)
    return tuple(out)
```

```python
import functools

import jax
import jax.numpy as jnp
from jax import lax
from jax.experimental import pallas as pl
from jax.experimental.pallas import tpu as pltpu

F32 = jnp.float32
BF16 = jnp.bfloat16

HEAD_DIM = 128
EPS = 1e-6
NEG_BIG = -1e30

ADAM_LR = 0.001
ADAM_B1 = 0.9
ADAM_B2 = 0.999
ADAM_EPS = 1e-08
ADAM_WD = 0.01
ADAM_STEP = 10

LANES = 128
BF16_SUBLANES = 16
VMEM_LIMIT_BYTES = 56 * 1024 * 1024
MM_TILE = 1024
MM_TILE_K = 2048
ATT_TILE = 256
ROW_TILE = 256
HEADNORM_ROWS = 512
COL_TILE = 512
ADAM_BLOCK_BYTES = 1 << 20

N_CHIPS = 4
N_DEV = 8
MESH = pl.DeviceIdType.MESH

IN_NAMES = ['x', 'mem', 'g_mix', 'w_in', 'b_forget', 'g_q_fox', 'g_k_fox', 'g_mem', 'w_mem_kv', 'g_q_mem', 'g_k_mem',
            'w_br_fox', 'w_br_sb', 'w_br_mem', 'b_gate', 'w_out', 'g_ffn', 'w_up', 'conv_w', 'conv_b', 'w_down']
WEIGHTS = IN_NAMES[2:]


def _tile(n, target):
    if n <= target:
        return n
    for t in range(target - target % LANES, LANES - 1, -LANES):
        if n % t == 0:
            return t
    return n


def _params(*sem):
    return pltpu.CompilerParams(dimension_semantics=sem, vmem_limit_bytes=VMEM_LIMIT_BYTES)


def _log_sigmoid(z):
    return jnp.minimum(z, 0.0) - jnp.log(1.0 + jnp.exp(-jnp.abs(z)))


def _split2(v):
    hi = v.astype(BF16)
    lo = (v - hi.astype(F32)).astype(BF16)
    return hi, lo


def _split3(v):
    hi = v.astype(BF16)
    r = v - hi.astype(F32)
    mid = r.astype(BF16)
    lo = (r - mid.astype(F32)).astype(BF16)
    return hi, mid, lo


def _dot(a, b):
    return lax.dot_general(a, b, (((1,), (0,)), ((), ())), preferred_element_type=F32)


def _dot_nt(a, b):
    return lax.dot_general(a, b, (((1,), (1,)), ((), ())), preferred_element_type=F32)


def _dot_tn(a, b):
    return lax.dot_general(a, b, (((0,), (0,)), ((), ())), preferred_element_type=F32)


ANY = pl.BlockSpec(memory_space=pl.ANY)


def _place():
    x, y, c = lax.axis_index("x"), lax.axis_index("y"), lax.axis_index("c")
    others = [(1 - x, y), (x, 1 - y), (1 - x, 1 - y)]
    return x, y, c, others


def _remote(src, dst, send_sem, recv_sem, to):
    return pltpu.make_async_remote_copy(src_ref=src, dst_ref=dst, send_sem=send_sem, recv_sem=recv_sem,
                                        device_id=to, device_id_type=MESH)


class _Gather:
    PER_SHARD = 7

    def __init__(self, shards):
        self.inputs = list(shards)
        n = len(shards) * self.PER_SHARD
        self.out_shapes = [jax.ShapeDtypeStruct((N_CHIPS,) + s.shape, s.dtype) for s in shards]
        self.scratch = [pltpu.SemaphoreType.DMA((n,)), pltpu.SemaphoreType.DMA((n,))]

    def _first(self, ins, outs, sems):
        send_sems, recv_sems = sems
        x, y, c, others = _place()
        me = 2 * x + y
        k = self.PER_SHARD
        copies = []
        for i in range(len(ins)):
            h = ins[i].shape[0] // 2
            mine = pl.ds(pl.multiple_of(c * h, BF16_SUBLANES), h)
            for j, (ox, oy) in enumerate(others):
                copies.append(_remote(ins[i].at[mine], outs[i].at[me, mine], send_sems.at[k * i + j],
                                      recv_sems.at[k * i + j], (ox, oy, c)))
            copies.append(_remote(ins[i], outs[i].at[me], send_sems.at[k * i + 6], recv_sems.at[k * i + 6],
                                  (x, y, 1 - c)))
        return copies

    def start(self, ins, outs, sems):
        for cp in self._first(ins, outs, sems):
            cp.start()

    def finish(self, ins, outs, sems):
        send_sems, recv_sems = sems
        x, y, c, others = _place()
        me = 2 * x + y
        sibling = (x, y, 1 - c)
        k = self.PER_SHARD
        passed = []
        for i in range(len(ins)):
            h = ins[i].shape[0] // 2
            mine = pl.ds(pl.multiple_of(c * h, BF16_SUBLANES), h)
            for j, (ox, oy) in enumerate(others):
                blk = outs[i].at[2 * ox + oy, mine]
                _remote(blk, blk, send_sems.at[k * i + j], recv_sems.at[k * i + j], (ox, oy, c)).wait_recv()
                cp = _remote(blk, blk, send_sems.at[k * i + 3 + j], recv_sems.at[k * i + 3 + j], sibling)
                cp.start()
                passed.append(cp)
        for i in range(len(ins)):
            h = ins[i].shape[0] // 2
            theirs = pl.ds(pl.multiple_of((1 - c) * h, BF16_SUBLANES), h)
            for j, (ox, oy) in enumerate(others):
                blk = outs[i].at[2 * ox + oy, theirs]
                _remote(blk, blk, send_sems.at[k * i + 3 + j], recv_sems.at[k * i + 3 + j], sibling).wait_recv()
            own = outs[i].at[me]
            _remote(own, own, send_sems.at[k * i + 6], recv_sems.at[k * i + 6], sibling).wait_recv()
        for cp in self._first(ins, outs, sems) + passed:
            cp.wait_send()


class _Scatter:
    def __init__(self, sums):
        self.inputs = list(sums)
        k = N_CHIPS - 1
        self.out_shapes = [jax.ShapeDtypeStruct((k,) + g.shape[1:], g.dtype) for g in sums]
        self.scratch = [pltpu.SemaphoreType.DMA((k * len(sums),)), pltpu.SemaphoreType.DMA((k * len(sums),))]

    def _copies(self, ins, outs, sems):
        send_sems, recv_sems = sems
        _, _, c, others = _place()
        k = N_CHIPS - 1
        return [_remote(ins[i].at[2 * ox + oy], outs[i].at[j], send_sems.at[k * i + j], recv_sems.at[k * i + j],
                        (ox, oy, c))
                for i in range(len(ins)) for j, (ox, oy) in enumerate(others)]

    def start(self, ins, outs, sems):
        for cp in self._copies(ins, outs, sems):
            cp.start()

    def finish(self, ins, outs, sems):
        for cp in self._copies(ins, outs, sems):
            cp.wait()


def _exchange(carry, name):
    n = len(carry.inputs)

    def body(*refs):
        ins, outs, sems = refs[:n], refs[n:2 * n], refs[2 * n:]
        carry.start(ins, outs, sems)
        carry.finish(ins, outs, sems)

    return pl.pallas_call(
        body,
        in_specs=[ANY] * n,
        out_specs=[ANY] * n,
        out_shape=carry.out_shapes,
        scratch_shapes=carry.scratch,
        name=name,
    )(*carry.inputs)


def _call(body, *, grid, in_specs, out_specs, out_shape, scratch_shapes, semantics, name, args, carry=None):
    n_in, n_out, n_scr = len(in_specs), len(out_specs), len(scratch_shapes)
    if carry is None:
        res = pl.pallas_call(body, grid=grid, in_specs=in_specs, out_specs=out_specs, out_shape=out_shape,
                             scratch_shapes=scratch_shapes, compiler_params=_params(*semantics), name=name)(*args)
        return list(res), []
    nci, nco = len(carry.inputs), len(carry.out_shapes)
    a, b = n_in, n_in + nci
    c, d = b + n_out, b + n_out + nco
    e = d + n_scr

    def carried(*refs):
        ids = [pl.program_id(k) for k in range(len(grid))]
        first = functools.reduce(jnp.logical_and, [i == 0 for i in ids])
        last = functools.reduce(jnp.logical_and, [i == n - 1 for i, n in zip(ids, grid)])

        @pl.when(first)
        def _():
            carry.start(refs[a:b], refs[c:d], refs[e:])

        body(*refs[:a], *refs[b:c], *refs[d:e])

        @pl.when(last)
        def _():
            carry.finish(refs[a:b], refs[c:d], refs[e:])

    res = pl.pallas_call(
        carried,
        grid=grid,
        in_specs=list(in_specs) + [ANY] * nci,
        out_specs=list(out_specs) + [ANY] * nco,
        out_shape=list(out_shape) + carry.out_shapes,
        scratch_shapes=list(scratch_shapes) + carry.scratch,
        compiler_params=_params(*(["arbitrary"] * len(grid))),
        name=name,
    )(*args, *carry.inputs)
    return list(res[:n_out]), list(res[n_out:])


def _mm(a, b, mode, out_dtype, name, residual=None, carry=None, loss_target=None):
    if mode == "nn":
        (m, k), (k2, n) = a.shape, b.shape
    elif mode == "nt":
        (m, k), (n, k2) = a.shape, b.shape
    else:
        (k, m), (k2, n) = a.shape, b.shape
    assert k == k2, (a.shape, b.shape, mode)
    tm, tn, tk = _tile(m, MM_TILE), _tile(n, MM_TILE), _tile(k, MM_TILE_K)
    nk = k // tk
    dot = {"nn": _dot, "nt": _dot_nt, "tn": _dot_tn}[mode]
    has_res = residual is not None
    has_loss = loss_target is not None
    n_in = 2 + has_res + has_loss

    def body(*refs):
        a_ref, b_ref = refs[:2]
        r_ref = refs[2] if has_res else None
        t_ref = refs[n_in - 1] if has_loss else None
        o_ref = refs[n_in]

        def finish(acc):
            if has_res:
                acc = acc + r_ref[...]
            if has_loss:
                err = acc - t_ref[...]
                dy = err * (1.0 / n)
                o_ref[...] = dy
                refs[n_in + 1][...] = dy.astype(BF16)
                tot = jnp.sum(jnp.sum(err * err, axis=-1, keepdims=True), axis=0, keepdims=True)
                refs[n_in + 2][...] = jnp.broadcast_to(tot, (8, LANES))
            else:
                o_ref[...] = acc.astype(o_ref.dtype)

        part = dot(a_ref[...], b_ref[...])
        if nk == 1:
            finish(part)
        else:
            acc_ref = refs[-1]
            kk = pl.program_id(2)

            @pl.when(kk == 0)
            def _():
                acc_ref[...] = part

            @pl.when(kk > 0)
            def _():
                acc_ref[...] += part

            @pl.when(kk == nk - 1)
            def _():
                finish(acc_ref[...])

    if mode == "tn":
        a_spec = pl.BlockSpec((tk, tm), lambda j, i, kk: (kk, i))
    else:
        a_spec = pl.BlockSpec((tm, tk), lambda j, i, kk: (i, kk))
    if mode == "nt":
        b_spec = pl.BlockSpec((tn, tk), lambda j, i, kk: (j, kk))
    else:
        b_spec = pl.BlockSpec((tk, tn), lambda j, i, kk: (kk, j))
    o_spec = pl.BlockSpec((tm, tn), lambda j, i, kk: (i, j))
    in_specs = [a_spec, b_spec] + [o_spec] * (has_res + has_loss)
    args = (a, b) + ((residual,) if has_res else ()) + ((loss_target,) if has_loss else ())
    out_specs, out_shape = [o_spec], [jax.ShapeDtypeStruct((m, n), out_dtype)]
    if has_loss:
        out_specs += [o_spec, pl.BlockSpec((8, LANES), lambda j, i, kk: (i, j))]
        out_shape += [jax.ShapeDtypeStruct((m, n), BF16), jax.ShapeDtypeStruct((m // tm * 8, n // tn * LANES), F32)]
    outs, moved = _call(
        body,
        grid=(n // tn, m // tm, nk),
        in_specs=in_specs,
        out_specs=out_specs,
        out_shape=out_shape,
        scratch_shapes=[pltpu.VMEM((tm, tn), F32)] if nk > 1 else [],
        semantics=("parallel", "parallel", "arbitrary"),
        name=name,
        args=args,
        carry=carry,
    )
    out = outs if has_loss else outs[0]
    return out if carry is None else (out, moved)


def _rms_fwd(x, g, name):
    s, d = x.shape
    tm = _tile(s, ROW_TILE)

    def body(x_ref, g_ref, h_ref, r_ref):
        xf = x_ref[...]
        r = lax.rsqrt(jnp.mean(xf * xf, axis=-1, keepdims=True) + EPS)
        h_ref[...] = ((xf * r) * g_ref[...]).astype(BF16)
        r_ref[...] = r

    return pl.pallas_call(
        body,
        grid=(s // tm,),
        in_specs=[pl.BlockSpec((tm, d), lambda i: (i, 0)), pl.BlockSpec((1, d), lambda i: (0, 0))],
        out_specs=[pl.BlockSpec((tm, d), lambda i: (i, 0)), pl.BlockSpec((tm, 1), lambda i: (i, 0))],
        out_shape=[jax.ShapeDtypeStruct((s, d), BF16), jax.ShapeDtypeStruct((s, 1), F32)],
        compiler_params=_params("parallel"),
        name=name,
    )(x, g)


def _rms_bwd(dh, x, rstd, g, res, name):
    s, d = x.shape
    tm = _tile(s, ROW_TILE)
    has_res = res is not None

    def body(*refs):
        if has_res:
            dh_ref, x_ref, r_ref, g_ref, res_ref, dx_ref, dxb_ref, dg_ref = refs
        else:
            dh_ref, x_ref, r_ref, g_ref, dx_ref, dxb_ref, dg_ref = refs
        dhf = dh_ref[...].astype(F32)
        xhat = x_ref[...] * r_ref[...]
        dy = dhf * g_ref[...]
        dx = r_ref[...] * (dy - xhat * jnp.mean(dy * xhat, axis=-1, keepdims=True))
        if has_res:
            dx = dx + res_ref[...]
        dx_ref[...] = dx
        dxb_ref[...] = dx.astype(BF16)
        part = jnp.sum(dhf * xhat, axis=0, keepdims=True)

        @pl.when(pl.program_id(0) == 0)
        def _():
            dg_ref[...] = part

        @pl.when(pl.program_id(0) > 0)
        def _():
            dg_ref[...] += part

    row = pl.BlockSpec((tm, d), lambda i: (i, 0))
    vec = pl.BlockSpec((1, d), lambda i: (0, 0))
    in_specs = [row, row, pl.BlockSpec((tm, 1), lambda i: (i, 0)), vec] + ([row] if has_res else [])
    args = (dh, x, rstd, g) + ((res,) if has_res else ())
    return pl.pallas_call(
        body,
        grid=(s // tm,),
        in_specs=in_specs,
        out_specs=[row, row, vec],
        out_shape=[jax.ShapeDtypeStruct((s, d), F32), jax.ShapeDtypeStruct((s, d), BF16),
                   jax.ShapeDtypeStruct((1, d), F32)],
        compiler_params=_params("arbitrary"),
        name=name,
    )(*args)


def _headnorm_fwd(src, col0, nheads, g, name):
    s = src.shape[0]
    tm = _tile(s, HEADNORM_ROWS)
    w = nheads * HEAD_DIM
    assert col0 % nheads == 0

    def body(x_ref, g_ref, o_ref):
        for hh in range(nheads):
            xf = _head(x_ref, hh).astype(F32)
            r = lax.rsqrt(jnp.mean(xf * xf, axis=-1, keepdims=True) + EPS)
            o_ref[:, hh * HEAD_DIM:(hh + 1) * HEAD_DIM] = ((xf * r) * g_ref[...]).astype(BF16)

    return pl.pallas_call(
        body,
        grid=(s // tm,),
        in_specs=[pl.BlockSpec((tm, w), lambda i: (i, col0 // nheads)),
                  pl.BlockSpec((1, HEAD_DIM), lambda i: (0, 0))],
        out_specs=pl.BlockSpec((tm, w), lambda i: (i, 0)),
        out_shape=jax.ShapeDtypeStruct((s, w), BF16),
        compiler_params=_params("parallel"),
        name=name,
    )(src, g)


def _headnorm_bwd(dxn, src, col0, nheads, g, name):
    s = src.shape[0]
    tm = _tile(s, HEADNORM_ROWS)
    w = nheads * HEAD_DIM
    assert col0 % nheads == 0

    def body(d_ref, x_ref, g_ref, dx_ref, dg_ref):
        part = jnp.zeros((1, HEAD_DIM), F32)
        for hh in range(nheads):
            xf = _head(x_ref, hh).astype(F32)
            r = lax.rsqrt(jnp.mean(xf * xf, axis=-1, keepdims=True) + EPS)
            xhat = xf * r
            dn = _head(d_ref, hh).astype(F32)
            dy = dn * g_ref[...]
            dx = r * (dy - xhat * jnp.mean(dy * xhat, axis=-1, keepdims=True))
            dx_ref[:, hh * HEAD_DIM:(hh + 1) * HEAD_DIM] = dx.astype(BF16)
            part = part + jnp.sum(dn * xhat, axis=0, keepdims=True)

        @pl.when(pl.program_id(0) == 0)
        def _():
            dg_ref[...] = part

        @pl.when(pl.program_id(0) > 0)
        def _():
            dg_ref[...] += part

    return pl.pallas_call(
        body,
        grid=(s // tm,),
        in_specs=[pl.BlockSpec((tm, w), lambda i: (i, 0)),
                  pl.BlockSpec((tm, w), lambda i: (i, col0 // nheads)),
                  pl.BlockSpec((1, HEAD_DIM), lambda i: (0, 0))],
        out_specs=[pl.BlockSpec((tm, w), lambda i: (i, 0)),
                   pl.BlockSpec((1, HEAD_DIM), lambda i: (0, 0))],
        out_shape=[jax.ShapeDtypeStruct((s, w), BF16), jax.ShapeDtypeStruct((1, HEAD_DIM), F32)],
        compiler_params=_params("arbitrary"),
        name=name,
    )(dxn, src, g)


def _tri(t, lower_inclusive):
    r = lax.broadcasted_iota(jnp.int32, (t, t), 0)
    c = lax.broadcasted_iota(jnp.int32, (t, t), 1)
    keep = (c <= r) if lower_inclusive else (c >= r)
    return jnp.where(keep, 1.0, 0.0).astype(BF16)


def _forget_fwd(f_logit, b_pad):
    s = f_logit.shape[0]
    t = _tile(s, ATT_TILE)

    def body(f_ref, b_ref, c_ref, carry):
        @pl.when(pl.program_id(0) == 0)
        def _():
            carry[...] = jnp.zeros_like(carry)

        lf = _log_sigmoid(f_ref[...] + b_ref[...])
        tri = _tri(t, True)
        acc = carry[...]
        for part in _split3(lf):
            acc = acc + _dot(tri, part)
        c_ref[...] = acc
        carry[...] += jnp.sum(lf, axis=0, keepdims=True)

    return pl.pallas_call(
        body,
        grid=(s // t,),
        in_specs=[pl.BlockSpec((t, LANES), lambda i: (i, 0)), pl.BlockSpec((1, LANES), lambda i: (0, 0))],
        out_specs=pl.BlockSpec((t, LANES), lambda i: (i, 0)),
        out_shape=jax.ShapeDtypeStruct((s, LANES), F32),
        scratch_shapes=[pltpu.VMEM((1, LANES), F32)],
        compiler_params=_params("arbitrary"),
        name="forget_fwd",
    )(f_logit, b_pad)


def _forget_bwd(dc, f_logit, b_pad):
    s = f_logit.shape[0]
    t = _tile(s, ATT_TILE)
    nb = s // t

    def body(dc_ref, f_ref, b_ref, df_ref, db_ref, carry):
        @pl.when(pl.program_id(0) == 0)
        def _():
            carry[...] = jnp.zeros_like(carry)
            db_ref[...] = jnp.zeros_like(db_ref)

        d = dc_ref[...]
        tri = _tri(t, False)
        acc = carry[...]
        for part in _split3(d):
            acc = acc + _dot(tri, part)
        z = f_ref[...] + b_ref[...]
        df = acc * jnp.exp(_log_sigmoid(-z))
        df_ref[...] = df
        db_ref[...] += jnp.sum(df, axis=0, keepdims=True)
        carry[...] += jnp.sum(d, axis=0, keepdims=True)

    rev = pl.BlockSpec((t, LANES), lambda i: (nb - 1 - i, 0))
    vec = pl.BlockSpec((1, LANES), lambda i: (0, 0))
    return pl.pallas_call(
        body,
        grid=(nb,),
        in_specs=[rev, rev, vec],
        out_specs=[rev, vec],
        out_shape=[jax.ShapeDtypeStruct((s, LANES), F32), jax.ShapeDtypeStruct((1, LANES), F32)],
        scratch_shapes=[pltpu.VMEM((1, LANES), F32)],
        compiler_params=_params("arbitrary"),
        name="forget_bwd",
    )(dc, f_logit, b_pad)


SB_FWD_GROUP = 6
FOX_GROUP = 6
SB_BWD_GROUP = 3


def _head(ref, hh, rows=slice(None)):
    return ref[rows, hh * HEAD_DIM:(hh + 1) * HEAD_DIM]


def _tri_mask(t, strict):
    r = lax.broadcasted_iota(jnp.int32, (t, t), 0)
    c = lax.broadcasted_iota(jnp.int32, (t, t), 1)
    return (c < r) if strict else (c <= r)


def _fox_fwd(qn, kn, proj, colv, c_col, c_row, nheads, carry=None):
    s = qn.shape[0]
    t = _tile(s, ATT_TILE)
    scale = HEAD_DIM ** -0.5
    hg = FOX_GROUP
    gw = hg * HEAD_DIM
    assert nheads % hg == 0 and colv % hg == 0

    def body(q_ref, k_ref, v_ref, cc_ref, cr_ref, o_ref, of_ref, lse_ref):
        qi = pl.program_id(1)
        causal = _tri_mask(t, False)

        def tile(kj, carry, diagonal):
            off = pl.multiple_of(kj * t, t)
            heads = range(hg)
            rows = pl.ds(off, t)
            qk = [_dot_nt(_head(q_ref, hh), _head(k_ref, hh, rows)) for hh in heads]
            sc = [qk[hh] * scale + (cc_ref[hh] - cr_ref[hh, :, rows]) for hh in heads]
            if diagonal:
                sc = [jnp.where(causal, sc[hh], NEG_BIG) for hh in heads]
            m_new = [jnp.maximum(carry[hh][0], jnp.max(sc[hh], axis=-1, keepdims=True)) for hh in heads]
            p = [jnp.exp(sc[hh] - m_new[hh]) for hh in heads]
            pv = [_dot(p[hh].astype(BF16), _head(v_ref, hh, rows)) for hh in heads]
            out = []
            for hh in heads:
                m, l, acc = carry[hh]
                alpha = jnp.exp(m - m_new[hh])
                out.append((m_new[hh], alpha * l + jnp.sum(p[hh], axis=-1, keepdims=True), alpha * acc + pv[hh]))
            return tuple(out)

        init = tuple((jnp.full((t, 1), NEG_BIG, F32), jnp.zeros((t, 1), F32), jnp.zeros((t, HEAD_DIM), F32))
                     for _ in range(hg))
        carry = lax.fori_loop(0, qi, lambda kj, c: tile(kj, c, False), init)
        carry = tile(qi, carry, True)
        for hh in range(hg):
            m, l, acc = carry[hh]
            o = acc / l
            of_ref[:, hh * HEAD_DIM:(hh + 1) * HEAD_DIM] = o
            o_ref[:, hh * HEAD_DIM:(hh + 1) * HEAD_DIM] = o.astype(BF16)
            lse_ref[hh] = m + jnp.log(l)

    tile_spec = pl.BlockSpec((t, gw), lambda h, i: (i, h))
    w = nheads * HEAD_DIM
    return _call(
        body,
        grid=(nheads // hg, s // t),
        in_specs=[tile_spec,
                  pl.BlockSpec((s, gw), lambda h, i: (0, h), pipeline_mode=pl.Buffered(buffer_count=1)),
                  pl.BlockSpec((s, gw), lambda h, i: (0, colv // hg + h), pipeline_mode=pl.Buffered(buffer_count=1)),
                  pl.BlockSpec((hg, t, 1), lambda h, i: (h, i, 0)),
                  pl.BlockSpec((hg, 1, s), lambda h, i: (h, 0, 0))],
        out_specs=[tile_spec, tile_spec, pl.BlockSpec((hg, t, 1), lambda h, i: (h, i, 0))],
        out_shape=[jax.ShapeDtypeStruct((s, w), BF16), jax.ShapeDtypeStruct((s, w), F32),
                   jax.ShapeDtypeStruct((nheads, s, 1), F32)],
        scratch_shapes=[],
        semantics=("parallel", "parallel"),
        name="fox_fwd",
        args=(qn, kn, proj, c_col, c_row),
        carry=carry,
    )


def _fox_bwd(qn, kn, proj, colv, c_col, c_row, o, do, lse, nheads, carry=None):
    s = qn.shape[0]
    t = _tile(s, ATT_TILE)
    scale = HEAD_DIM ** -0.5
    hg = FOX_GROUP
    gw = hg * HEAD_DIM
    assert nheads % hg == 0 and colv % hg == 0

    def body(q_ref, k_ref, v_ref, cc_ref, cr_ref, o_ref, do_ref, lse_ref,
             dq_ref, dk_ref, dv_ref, drs_ref, dcs_ref):
        qi = pl.program_id(1)

        @pl.when(qi == 0)
        def _():
            dk_ref[...] = jnp.zeros_like(dk_ref)
            dv_ref[...] = jnp.zeros_like(dv_ref)
            dcs_ref[...] = jnp.zeros_like(dcs_ref)

        causal = _tri_mask(t, False)
        delta = [jnp.sum(_head(o_ref, hh) * _head(do_ref, hh).astype(F32), axis=-1, keepdims=True)
                 for hh in range(hg)]

        def tile(kj, carry, diagonal):
            off = pl.multiple_of(kj * t, t)
            heads = range(hg)
            rows = pl.ds(off, t)
            qk = [_dot_nt(_head(q_ref, hh), _head(k_ref, hh, rows)) for hh in heads]
            dp = [_dot_nt(_head(do_ref, hh), _head(v_ref, hh, rows)) for hh in heads]
            p = [jnp.exp(qk[hh] * scale + (cc_ref[hh] - cr_ref[hh, :, rows]) - lse_ref[hh]) for hh in heads]
            if diagonal:
                p = [jnp.where(causal, p[hh], 0.0) for hh in heads]
            ds = [p[hh] * (dp[hh] - delta[hh]) for hh in heads]
            dsb = [ds[hh].astype(BF16) for hh in heads]
            dv = [_dot_tn(p[hh].astype(BF16), _head(do_ref, hh)) for hh in heads]
            dk = [_dot_tn(dsb[hh], _head(q_ref, hh)) * scale for hh in heads]
            dq = [_dot(dsb[hh], _head(k_ref, hh, rows)) * scale for hh in heads]
            for hh in heads:
                cols = slice(hh * HEAD_DIM, (hh + 1) * HEAD_DIM)
                dv_ref[rows, cols] += dv[hh]
                dk_ref[rows, cols] += dk[hh]
                dcs_ref[hh, :, rows] += jnp.sum(ds[hh], axis=0, keepdims=True)
            return tuple((carry[hh][0] + dq[hh], carry[hh][1] + jnp.sum(ds[hh], axis=-1, keepdims=True))
                         for hh in heads)

        init = tuple((jnp.zeros((t, HEAD_DIM), F32), jnp.zeros((t, 1), F32)) for _ in range(hg))
        carry = lax.fori_loop(0, qi, lambda kj, c: tile(kj, c, False), init)
        carry = tile(qi, carry, True)
        for hh in range(hg):
            dq_ref[:, hh * HEAD_DIM:(hh + 1) * HEAD_DIM] = carry[hh][0]
            drs_ref[hh] = carry[hh][1]

    tile_spec = pl.BlockSpec((t, gw), lambda h, i: (i, h))
    once = pl.Buffered(buffer_count=1)
    full = pl.BlockSpec((s, gw), lambda h, i: (0, h), pipeline_mode=once)
    colspec = pl.BlockSpec((hg, t, 1), lambda h, i: (h, i, 0))
    rowspec = pl.BlockSpec((hg, 1, s), lambda h, i: (h, 0, 0))
    w = nheads * HEAD_DIM
    return _call(
        body,
        grid=(nheads // hg, s // t),
        in_specs=[tile_spec, full, pl.BlockSpec((s, gw), lambda h, i: (0, colv // hg + h), pipeline_mode=once),
                  colspec, rowspec,
                  tile_spec, tile_spec, colspec],
        out_specs=[tile_spec, full, full, colspec, rowspec],
        out_shape=[jax.ShapeDtypeStruct((s, w), F32), jax.ShapeDtypeStruct((s, w), F32),
                   jax.ShapeDtypeStruct((s, w), F32), jax.ShapeDtypeStruct((nheads, s, 1), F32),
                   jax.ShapeDtypeStruct((nheads, 1, s), F32)],
        scratch_shapes=[],
        semantics=("arbitrary", "arbitrary"),
        name="fox_bwd",
        args=(qn, kn, proj, c_col, c_row, o, do, lse),
        carry=carry,
    )


def _sb_tile(q, k, scale, later, valid):
    z = _dot_nt(q, k) * scale
    lb = _log_sigmoid(z)
    lm = lb - z
    if valid is not None:
        lm = jnp.where(valid, lm, 0.0)
    suffix = _dot(jnp.concatenate(_split2(lm), axis=1), later)
    return lb, lm, suffix


def _later(t):
    r = lax.broadcasted_iota(jnp.int32, (2 * t, t), 0) % t
    c = lax.broadcasted_iota(jnp.int32, (2 * t, t), 1)
    return jnp.where(r > c, 1.0, 0.0).astype(BF16)


def _sb_fwd(proj, colq, colk, colv, nheads, carry=None):
    s = proj.shape[0]
    t = _tile(s, ATT_TILE)
    scale = HEAD_DIM ** -0.5
    hg = SB_FWD_GROUP
    gw = hg * HEAD_DIM
    assert nheads % hg == 0 and colq % hg == 0 and colk % hg == 0 and colv % hg == 0

    def body(q_ref, k_ref, v_ref, o_ref):
        qi = pl.program_id(1)
        later = _later(t)
        before = _tri_mask(t, True)

        def tile(kj, carry, diagonal):
            off = pl.multiple_of(kj * t, t)
            heads = range(hg)
            z = [_dot_nt(_head(q_ref, hh), _head(k_ref, hh, pl.ds(off, t))) * scale for hh in heads]
            lb = [_log_sigmoid(z[hh]) for hh in heads]
            lm = [lb[hh] - z[hh] for hh in heads]
            if diagonal:
                lm = [jnp.where(before, lm[hh], 0.0) for hh in heads]
            parts = [jnp.concatenate(_split2(lm[hh]), axis=1) for hh in heads]
            suffix = [_dot(parts[hh], later) for hh in heads]
            a = [jnp.exp(lb[hh] + suffix[hh] + carry[hh][0]) for hh in heads]
            if diagonal:
                a = [jnp.where(before, a[hh], 0.0) for hh in heads]
            av = [_dot(a[hh].astype(BF16), _head(v_ref, hh, pl.ds(off, t))) for hh in heads]
            return tuple((carry[hh][0] + jnp.sum(lm[hh], axis=-1, keepdims=True), carry[hh][1] + av[hh])
                         for hh in heads)

        init = tuple((jnp.zeros((t, 1), F32), jnp.zeros((t, HEAD_DIM), F32)) for _ in range(hg))
        carry = tile(qi, init, True)
        carry = lax.fori_loop(1, qi + 1, lambda i, c: tile(qi - i, c, False), carry)
        for hh in range(hg):
            o_ref[:, hh * HEAD_DIM:(hh + 1) * HEAD_DIM] = carry[hh][1].astype(BF16)

    return _call(
        body,
        grid=(nheads // hg, s // t),
        in_specs=[pl.BlockSpec((t, gw), lambda h, i: (i, colq // hg + h)),
                  pl.BlockSpec((s, gw), lambda h, i: (0, colk // hg + h), pipeline_mode=pl.Buffered(buffer_count=1)),
                  pl.BlockSpec((s, gw), lambda h, i: (0, colv // hg + h), pipeline_mode=pl.Buffered(buffer_count=1))],
        out_specs=[pl.BlockSpec((t, gw), lambda h, i: (i, h))],
        out_shape=[jax.ShapeDtypeStruct((s, nheads * HEAD_DIM), BF16)],
        scratch_shapes=[],
        semantics=("parallel", "parallel"),
        name="sb_fwd",
        args=(proj, proj, proj),
        carry=carry,
    )


def _sb_bwd(proj, colq, colk, colv, do, nheads, carry=None):
    s = proj.shape[0]
    t = _tile(s, ATT_TILE)
    scale = HEAD_DIM ** -0.5
    hg = SB_BWD_GROUP
    gw = hg * HEAD_DIM
    assert nheads % hg == 0 and colq % hg == 0 and colk % hg == 0 and colv % hg == 0

    def body(q_ref, k_ref, v_ref, do_ref, dq_ref, dk_ref, dv_ref, g_s, beta_s):
        qi = pl.program_id(1)

        @pl.when(qi == 0)
        def _():
            dk_ref[...] = jnp.zeros_like(dk_ref)
            dv_ref[...] = jnp.zeros_like(dv_ref)

        later = _later(t)
        before = _tri_mask(t, True)

        def back(kj, carry, diagonal):
            off = pl.multiple_of(kj * t, t)
            heads = range(hg)
            rows = pl.ds(off, t)
            z = [_dot_nt(_head(q_ref, hh), _head(k_ref, hh, rows)) * scale for hh in heads]
            da = [_dot_nt(_head(do_ref, hh), _head(v_ref, hh, rows)) for hh in heads]
            lb = [_log_sigmoid(z[hh]) for hh in heads]
            lm = [lb[hh] - z[hh] for hh in heads]
            if diagonal:
                lm = [jnp.where(before, lm[hh], 0.0) for hh in heads]
            parts = [jnp.concatenate(_split2(lm[hh]), axis=1) for hh in heads]
            suffix = [_dot(parts[hh], later) for hh in heads]
            a = [jnp.exp(lb[hh] + suffix[hh] + carry[hh]) for hh in heads]
            if diagonal:
                a = [jnp.where(before, a[hh], 0.0) for hh in heads]
            dv = [_dot_tn(a[hh].astype(BF16), _head(do_ref, hh)) for hh in heads]
            for hh in heads:
                g_s[hh, :, rows] = a[hh] * da[hh]
                beta_s[hh, :, rows] = jnp.exp(lb[hh]).astype(BF16)
            for hh in heads:
                dv_ref[rows, hh * HEAD_DIM:(hh + 1) * HEAD_DIM] += dv[hh]
            return tuple(carry[hh] + jnp.sum(lm[hh], axis=-1, keepdims=True) for hh in heads)

        rc = back(qi, tuple(jnp.zeros((t, 1), F32) for _ in range(hg)), True)
        lax.fori_loop(1, qi + 1, lambda i, c: back(qi - i, c, False), rc)

        earlier = jnp.where(lax.broadcasted_iota(jnp.int32, (2 * t, t), 0) % t
                            < lax.broadcasted_iota(jnp.int32, (2 * t, t), 1), 1.0, 0.0).astype(BF16)

        def fwd(kj, carry, diagonal):
            off = pl.multiple_of(kj * t, t)
            heads = range(hg)
            rows = pl.ds(off, t)
            g = [g_s[hh, :, rows] for hh in heads]
            parts = [jnp.concatenate(_split2(g[hh]), axis=1) for hh in heads]
            gsum = [_dot(parts[hh], earlier) + carry[hh][0] for hh in heads]
            dz = []
            for hh in heads:
                beta = beta_s[hh, :, rows].astype(F32)
                d = g[hh] * (1.0 - beta) - gsum[hh] * beta
                if diagonal:
                    d = jnp.where(before, d, 0.0)
                dz.append(d.astype(BF16))
            dk = [_dot_tn(dz[hh], _head(q_ref, hh)) * scale for hh in heads]
            dq = [_dot(dz[hh], _head(k_ref, hh, rows)) * scale for hh in heads]
            for hh in heads:
                dk_ref[rows, hh * HEAD_DIM:(hh + 1) * HEAD_DIM] += dk[hh]
            return tuple((carry[hh][0] + jnp.sum(g[hh], axis=-1, keepdims=True), carry[hh][1] + dq[hh])
                         for hh in heads)

        init = tuple((jnp.zeros((t, 1), F32), jnp.zeros((t, HEAD_DIM), F32)) for _ in range(hg))
        carry = lax.fori_loop(0, qi, lambda kj, c: fwd(kj, c, False), init)
        carry = fwd(qi, carry, True)
        for hh in range(hg):
            dq_ref[:, hh * HEAD_DIM:(hh + 1) * HEAD_DIM] = carry[hh][1]

    once = pl.Buffered(buffer_count=1)
    tile_spec = pl.BlockSpec((t, gw), lambda h, i: (i, h))
    full = pl.BlockSpec((s, gw), lambda h, i: (0, h), pipeline_mode=once)
    w = nheads * HEAD_DIM
    return _call(
        body,
        grid=(nheads // hg, s // t),
        in_specs=[pl.BlockSpec((t, gw), lambda h, i: (i, colq // hg + h)),
                  pl.BlockSpec((s, gw), lambda h, i: (0, colk // hg + h), pipeline_mode=once),
                  pl.BlockSpec((s, gw), lambda h, i: (0, colv // hg + h), pipeline_mode=once),
                  tile_spec],
        out_specs=[tile_spec, full, full],
        out_shape=[jax.ShapeDtypeStruct((s, w), F32)] * 3,
        scratch_shapes=[pltpu.VMEM((hg, t, s), F32), pltpu.VMEM((hg, t, s), BF16)],
        semantics=("arbitrary", "arbitrary"),
        name="sb_bwd",
        args=(proj, proj, proj, do),
        carry=carry,
    )


def _mem_fwd(qn, kn, mkv, nheads):
    s = qn.shape[0]
    mtok = kn.shape[0]
    t = _tile(s, ATT_TILE)
    scale = HEAD_DIM ** -0.5

    def body(q_ref, k_ref, v_ref, o_ref):
        sc = _dot_nt(q_ref[...], k_ref[...]) * scale
        p = jnp.exp(sc - jnp.max(sc, axis=-1, keepdims=True))
        p = p / jnp.sum(p, axis=-1, keepdims=True)
        o_ref[...] = _dot(p.astype(BF16), v_ref[...]).astype(BF16)

    return pl.pallas_call(
        body,
        grid=(nheads, s // t),
        in_specs=[pl.BlockSpec((t, HEAD_DIM), lambda h, i: (i, h)),
                  pl.BlockSpec((mtok, HEAD_DIM), lambda h, i: (0, h)),
                  pl.BlockSpec((mtok, HEAD_DIM), lambda h, i: (0, nheads + h))],
        out_specs=pl.BlockSpec((t, HEAD_DIM), lambda h, i: (i, h)),
        out_shape=jax.ShapeDtypeStruct((s, nheads * HEAD_DIM), BF16),
        compiler_params=_params("parallel", "parallel"),
        name="mem_fwd",
    )(qn, kn, mkv)


def _mem_bwd(qn, kn, mkv, do, nheads):
    s = qn.shape[0]
    mtok = kn.shape[0]
    t = _tile(s, ATT_TILE)
    scale = HEAD_DIM ** -0.5

    def body(q_ref, k_ref, v_ref, do_ref, dq_ref, dk_ref, dv_ref):
        @pl.when(pl.program_id(1) == 0)
        def _():
            dk_ref[...] = jnp.zeros_like(dk_ref)
            dv_ref[...] = jnp.zeros_like(dv_ref)

        q = q_ref[...]
        k = k_ref[...]
        do_ = do_ref[...]
        sc = _dot_nt(q, k) * scale
        p = jnp.exp(sc - jnp.max(sc, axis=-1, keepdims=True))
        p = p / jnp.sum(p, axis=-1, keepdims=True)
        dp = _dot_nt(do_, v_ref[...])
        ds = (p * (dp - jnp.sum(p * dp, axis=-1, keepdims=True))).astype(BF16)
        dq_ref[...] = _dot(ds, k) * scale
        dk_ref[...] += _dot_tn(ds, q) * scale
        dv_ref[...] += _dot_tn(p.astype(BF16), do_)

    tile = pl.BlockSpec((t, HEAD_DIM), lambda h, i: (i, h))
    kspec = pl.BlockSpec((mtok, HEAD_DIM), lambda h, i: (0, h))
    w = nheads * HEAD_DIM
    return pl.pallas_call(
        body,
        grid=(nheads, s // t),
        in_specs=[tile, kspec, pl.BlockSpec((mtok, HEAD_DIM), lambda h, i: (0, nheads + h)), tile],
        out_specs=[tile, kspec, kspec],
        out_shape=[jax.ShapeDtypeStruct((s, w), F32), jax.ShapeDtypeStruct((mtok, w), F32),
                   jax.ShapeDtypeStruct((mtok, w), F32)],
        compiler_params=_params("arbitrary", "arbitrary"),
        name="mem_bwd",
    )(qn, kn, mkv, do)


def _merge_fwd(p0, p1, p2, gates, b_gate):
    s, d = p0.shape
    tm, tn = _tile(s, ROW_TILE), _tile(d, COL_TILE)
    nj = d // tn

    def body(p0_ref, p1_ref, p2_ref, ga_ref, gb_ref, gc_ref, b_ref, o_ref):
        acc = jnp.zeros((tm, tn), F32)
        for b, (p_ref, g_ref) in enumerate(((p0_ref, ga_ref), (p1_ref, gb_ref), (p2_ref, gc_ref))):
            gate = jax.nn.sigmoid(g_ref[...].astype(F32) + b_ref[b:b + 1, :])
            acc = acc + gate * p_ref[...]
        o_ref[...] = acc.astype(BF16)

    blk = pl.BlockSpec((tm, tn), lambda i, j: (i, j))
    return pl.pallas_call(
        body,
        grid=(s // tm, nj),
        in_specs=[blk] * 6 + [pl.BlockSpec((3, tn), lambda i, j: (0, j))],
        out_specs=blk,
        out_shape=jax.ShapeDtypeStruct((s, d), BF16),
        compiler_params=_params("parallel", "parallel"),
        name="merge_fwd",
    )(p0, p1, p2, *gates, b_gate)


def _merge_bwd(dmerged, p0, p1, p2, gates, b_gate):
    s, d = p0.shape
    tm, tn = _tile(s, ROW_TILE), _tile(d, COL_TILE)
    nj = d // tn

    def body(dm_ref, p0_ref, p1_ref, p2_ref, ga_ref, gb_ref, gc_ref, b_ref,
             d0_ref, d1_ref, d2_ref, dga_ref, dgb_ref, dgc_ref, db_ref):
        dm = dm_ref[...].astype(F32)
        parts = []
        for b, (p_ref, g_ref, dp_ref, dg_ref) in enumerate(((p0_ref, ga_ref, d0_ref, dga_ref),
                                                            (p1_ref, gb_ref, d1_ref, dgb_ref),
                                                            (p2_ref, gc_ref, d2_ref, dgc_ref))):
            gate = jax.nn.sigmoid(g_ref[...].astype(F32) + b_ref[b:b + 1, :])
            dp_ref[...] = (dm * gate).astype(BF16)
            dgate = dm * p_ref[...] * gate * (1.0 - gate)
            dg_ref[...] = dgate.astype(BF16)
            parts.append(jnp.sum(dgate, axis=0, keepdims=True))
        part = jnp.concatenate(parts, axis=0)

        @pl.when(pl.program_id(1) == 0)
        def _():
            db_ref[...] = part

        @pl.when(pl.program_id(1) > 0)
        def _():
            db_ref[...] += part

    blk = pl.BlockSpec((tm, tn), lambda j, i: (i, j))
    bias = pl.BlockSpec((3, tn), lambda j, i: (0, j))
    return pl.pallas_call(
        body,
        grid=(nj, s // tm),
        in_specs=[blk] * 7 + [bias],
        out_specs=[blk] * 6 + [bias],
        out_shape=[jax.ShapeDtypeStruct((s, d), BF16)] * 6 + [jax.ShapeDtypeStruct((3, d), F32)],
        compiler_params=_params("parallel", "arbitrary"),
        name="merge_bwd",
    )(dmerged, p0, p1, p2, *gates, b_gate)


def _shift_down(v, n):
    rows = lax.broadcasted_iota(jnp.int32, v.shape, 0)
    return jnp.where(rows >= n, pltpu.roll(v, n, 0), 0.0)


def _shift_up(v, n):
    s = v.shape[0]
    rows = lax.broadcasted_iota(jnp.int32, v.shape, 0)
    return jnp.where(rows < s - n, pltpu.roll(v, s - n, 0), 0.0)


def _conv(v, w_ref, b_ref):
    taps = w_ref.shape[0]
    out = v * w_ref[taps - 1:taps, :] + b_ref[...]
    for n in range(1, taps):
        out = out + _shift_down(v, n) * w_ref[taps - 1 - n:taps - n, :]
    return out


def _conv_act_fwd(up, conv_w, conv_b):
    s, f2 = up.shape
    f = f2 // 2
    tn = LANES
    nj = f // tn
    taps = conv_w.shape[0]

    def body(ug_ref, uv_ref, wg_ref, wv_ref, bg_ref, bv_ref, o_ref):
        cg = _conv(ug_ref[...].astype(F32), wg_ref, bg_ref)
        cv = _conv(uv_ref[...].astype(F32), wv_ref, bv_ref)
        o_ref[...] = (cg * jax.nn.sigmoid(cg) * cv).astype(BF16)

    return pl.pallas_call(
        body,
        grid=(nj,),
        in_specs=[pl.BlockSpec((s, tn), lambda j: (0, j)), pl.BlockSpec((s, tn), lambda j: (0, nj + j)),
                  pl.BlockSpec((taps, tn), lambda j: (0, j)), pl.BlockSpec((taps, tn), lambda j: (0, nj + j)),
                  pl.BlockSpec((1, tn), lambda j: (0, j)), pl.BlockSpec((1, tn), lambda j: (0, nj + j))],
        out_specs=pl.BlockSpec((s, tn), lambda j: (0, j)),
        out_shape=jax.ShapeDtypeStruct((s, f), BF16),
        compiler_params=_params("parallel"),
        name="conv_act_fwd",
    )(up, up, conv_w, conv_w, conv_b, conv_b)


def _conv_act_bwd(up, conv_w, conv_b, dact):
    s, f2 = up.shape
    f = f2 // 2
    tn = LANES
    nj = f // tn
    taps = conv_w.shape[0]

    def half(v, du, w_ref, dup_ref, dw_ref, db_ref):
        dup = du * w_ref[taps - 1:taps, :]
        rows = [None] * taps
        rows[taps - 1] = jnp.sum(du * v, axis=0, keepdims=True)
        for n in range(1, taps):
            later = _shift_up(du, n)
            dup = dup + later * w_ref[taps - 1 - n:taps - n, :]
            rows[taps - 1 - n] = jnp.sum(later * v, axis=0, keepdims=True)
        dup_ref[...] = dup.astype(BF16)
        dw_ref[...] = jnp.concatenate(rows, axis=0)
        db_ref[...] = jnp.sum(du, axis=0, keepdims=True)

    def body(ug_ref, uv_ref, wg_ref, wv_ref, bg_ref, bv_ref, da_ref,
             dug_ref, duv_ref, dwg_ref, dwv_ref, dbg_ref, dbv_ref):
        ug = ug_ref[...].astype(F32)
        uv = uv_ref[...].astype(F32)
        cg = _conv(ug, wg_ref, bg_ref)
        cv = _conv(uv, wv_ref, bv_ref)
        da = da_ref[...].astype(F32)
        sg = jax.nn.sigmoid(cg)
        dcv = da * cg * sg
        dcg = da * cv * (sg + cg * sg * (1.0 - sg))
        half(ug, dcg, wg_ref, dug_ref, dwg_ref, dbg_ref)
        half(uv, dcv, wv_ref, duv_ref, dwv_ref, dbv_ref)

    lo = lambda rows: pl.BlockSpec((rows, tn), lambda j: (0, j))
    hi = lambda rows: pl.BlockSpec((rows, tn), lambda j: (0, nj + j))
    return pl.pallas_call(
        body,
        grid=(nj,),
        in_specs=[lo(s), hi(s), lo(taps), hi(taps), lo(1), hi(1), lo(s)],
        out_specs=[lo(s), lo(s), lo(taps), lo(taps), lo(1), lo(1)],
        out_shape=[jax.ShapeDtypeStruct((s, f), BF16)] * 2 + [jax.ShapeDtypeStruct((taps, f), F32)] * 2
        + [jax.ShapeDtypeStruct((1, f), F32)] * 2,
        compiler_params=_params("parallel"),
        name="conv_act_bwd",
    )(up, up, conv_w, conv_w, conv_b, conv_b, dact)


def _row_tile(rows, row_bytes, budget):
    if rows * row_bytes <= budget or rows % 8:
        return rows
    best = 8
    for t in range(8, rows, 8):
        if rows % t == 0 and t * row_bytes <= budget:
            best = t
    return best


def _adamw(w, g, m, v, name):
    r, c = w.shape
    tr = _row_tile(r, c * 4, ADAM_BLOCK_BYTES)

    def body(w_ref, g_ref, m_ref, v_ref, d_ref, mo_ref, vo_ref):
        gg = g_ref[...]
        m_new = ADAM_B1 * m_ref[...] + (1.0 - ADAM_B1) * gg
        v_new = ADAM_B2 * v_ref[...] + (1.0 - ADAM_B2) * (gg * gg)
        m_hat = m_new / (1.0 - ADAM_B1 ** ADAM_STEP)
        v_hat = v_new / (1.0 - ADAM_B2 ** ADAM_STEP)
        d_ref[...] = -ADAM_LR * (m_hat / (jnp.sqrt(v_hat) + ADAM_EPS) + ADAM_WD * w_ref[...])
        mo_ref[...] = m_new
        vo_ref[...] = v_new

    blk = pl.BlockSpec((tr, c), lambda i: (i, 0))
    return pl.pallas_call(
        body,
        grid=(r // tr,),
        in_specs=[blk] * 4,
        out_specs=[blk] * 3,
        out_shape=[jax.ShapeDtypeStruct((r, c), F32)] * 3,
        compiler_params=_params("parallel"),
        name=name,
    )(w, g, m, v)


def _add_sibling(g, r1, core, name):
    _, _, h, c = g.shape
    th = _row_tile(h, c * 2, ADAM_BLOCK_BYTES)

    def body(core_ref, g_ref, r_ref, o_ref):
        o_ref[...] = (g_ref[...].astype(F32) + r_ref[...].astype(F32)).astype(BF16)

    return pl.pallas_call(
        body,
        grid_spec=pltpu.PrefetchScalarGridSpec(
            num_scalar_prefetch=1,
            grid=(N_CHIPS, h // th),
            in_specs=[pl.BlockSpec((None, None, th, c), lambda j, i, core_ref: (j, core_ref[0], i, 0)),
                      pl.BlockSpec((None, th, c), lambda j, i, core_ref: (j, i, 0))],
            out_specs=pl.BlockSpec((None, th, c), lambda j, i, core_ref: (j, i, 0)),
        ),
        out_shape=jax.ShapeDtypeStruct((N_CHIPS, h, c), BF16),
        compiler_params=_params("parallel", "parallel"),
        name=name,
    )(core, g, r1)


def _add_chips(hsum, r2, chip_core, name):
    _, h, c = hsum.shape
    th = _row_tile(h, c * 4, ADAM_BLOCK_BYTES)

    def body(sel_ref, own_ref, r_ref, o_ref):
        acc = own_ref[...].astype(F32)
        for j in range(N_CHIPS - 1):
            acc = acc + r_ref[j].astype(F32)
        o_ref[...] = acc

    return pl.pallas_call(
        body,
        grid_spec=pltpu.PrefetchScalarGridSpec(
            num_scalar_prefetch=1,
            grid=(h // th,),
            in_specs=[pl.BlockSpec((None, th, c), lambda i, sel_ref: (sel_ref[0], i, 0)),
                      pl.BlockSpec((N_CHIPS - 1, th, c), lambda i, sel_ref: (0, i, 0))],
            out_specs=pl.BlockSpec((None, th, c), lambda i, sel_ref: (sel_ref[1], i, 0)),
        ),
        out_shape=jax.ShapeDtypeStruct((2, h, c), F32),
        compiler_params=_params("parallel"),
        name=name,
    )(chip_core, hsum, r2)


def _sum_devices(parts):
    _, r, c = parts.shape

    def body(p_ref, o_ref):
        acc = p_ref[0]
        for j in range(1, N_DEV):
            acc = acc + p_ref[j]
        o_ref[...] = acc

    return pl.pallas_call(
        body,
        out_shape=jax.ShapeDtypeStruct((r, c), F32),
        compiler_params=pltpu.CompilerParams(vmem_limit_bytes=VMEM_LIMIT_BYTES),
        name="sum_devices",
    )(parts)


def _swap_halves(grads, name):
    n = len(grads)

    def body(*refs):
        ins, outs = refs[:n], refs[n:2 * n]
        send_sems, recv_sems = refs[2 * n:]
        x, y, c, _ = _place()
        copies = [_remote(ins[i].at[:, 1 - c], outs[i], send_sems.at[i], recv_sems.at[i], (x, y, 1 - c))
                  for i in range(n)]
        for cp in copies:
            cp.start()
        for cp in copies:
            cp.wait()

    return pl.pallas_call(
        body,
        in_specs=[ANY] * n,
        out_specs=[ANY] * n,
        out_shape=[jax.ShapeDtypeStruct((g.shape[0],) + g.shape[2:], g.dtype) for g in grads],
        scratch_shapes=[pltpu.SemaphoreType.DMA((n,)), pltpu.SemaphoreType.DMA((n,))],
        name=name,
    )(*grads)


def _join_halves(finals):
    n = len(finals)

    def body(*refs):
        outs = refs[n:2 * n]
        send_sems, recv_sems = refs[2 * n:]
        x, y, c, _ = _place()
        sends = [_remote(outs[i].at[c], outs[i].at[c], send_sems.at[i], recv_sems.at[i], (x, y, 1 - c))
                 for i in range(n)]
        for cp in sends:
            cp.start()
        for i in range(n):
            sends[i].wait_send()
            other = outs[i].at[1 - c]
            _remote(other, other, send_sems.at[i], recv_sems.at[i], (x, y, 1 - c)).wait_recv()

    return pl.pallas_call(
        body,
        in_specs=[ANY] * n,
        out_specs=[ANY] * n,
        out_shape=[jax.ShapeDtypeStruct(f.shape, f.dtype) for f in finals],
        input_output_aliases={i: i for i in range(n)},
        scratch_shapes=[pltpu.SemaphoreType.DMA((n,)), pltpu.SemaphoreType.DMA((n,))],
        name="join_halves",
    )(*finals)


def _gather_small(vec):
    k = N_DEV - 1

    def body(v_ref, o_ref, send_sems, recv_sems, local_sem):
        x, y, c, _ = _place()
        me = 4 * x + 2 * y + c
        local = pltpu.make_async_copy(v_ref, o_ref.at[me], local_sem)
        local.start()
        peers = [(x ^ (r >> 2 & 1), y ^ (r >> 1 & 1), c ^ (r & 1)) for r in range(1, N_DEV)]
        sends = [_remote(v_ref, o_ref.at[me], send_sems.at[j], recv_sems.at[j], p) for j, p in enumerate(peers)]
        for cp in sends:
            cp.start()
        for j, (px, py, pc) in enumerate(peers):
            sends[j].wait_send()
            blk = o_ref.at[4 * px + 2 * py + pc]
            _remote(blk, blk, send_sems.at[j], recv_sems.at[j], (px, py, pc)).wait_recv()
        local.wait()

    return pl.pallas_call(
        body,
        in_specs=[ANY],
        out_specs=ANY,
        out_shape=jax.ShapeDtypeStruct((N_DEV,) + vec.shape, vec.dtype),
        scratch_shapes=[pltpu.SemaphoreType.DMA((k,)), pltpu.SemaphoreType.DMA((k,)), pltpu.SemaphoreType.DMA(())],
        name="gather_small",
    )(vec)


W_IN_GATES = ("w_in_g0", "w_in_g1", "w_in_g2")
ROW_SHARDED = ("w_in_a", "w_in_f") + W_IN_GATES + ("w_mem_kv", "w_out", "w_down")
COL_SHARDED = ("w_br_fox", "w_br_sb", "w_br_mem", "w_up")
BIG = ROW_SHARDED + COL_SHARDED


def _whole(name, a):
    if name in ROW_SHARDED:
        return a.reshape(N_CHIPS * a.shape[1], a.shape[2])
    return a.transpose(1, 0, 2).reshape(a.shape[1], N_CHIPS * a.shape[2])


def _by_shard(name, grad):
    if name in ROW_SHARDED:
        a = grad.reshape(N_CHIPS, grad.shape[0] // N_CHIPS, grad.shape[1])
    else:
        a = grad.reshape(grad.shape[0], N_CHIPS, grad.shape[1] // N_CHIPS).transpose(1, 0, 2)
    return a.reshape(N_CHIPS, 2, a.shape[1] // 2, a.shape[2])


def _sibling_sums(names, g, core, tag):
    split = [_by_shard(name, g[name]) for name in names]
    theirs = _swap_halves(split, "swap_halves_" + tag)
    return [_add_sibling(a, r, core, "add_sibling_" + name) for name, a, r in zip(names, split, theirs)]


GATHER_FIRST = ("w_in_a", "w_in_f", "w_mem_kv")
GATHER_MIX = ("w_down", "w_out", "w_br_fox", "w_br_sb", "w_br_mem")
REDUCE_FFN = ("w_down", "w_up")
REDUCE_MIX = ("w_out", "w_br_fox", "w_br_sb", "w_br_mem") + W_IN_GATES
REDUCE_IN = ("w_in_a", "w_in_f", "w_mem_kv")


def _local_step(x, mem, target, w, shard, core, chip_core):
    d = x.shape[1]
    nf = shard["w_br_fox"].shape[0] // HEAD_DIM
    nsb = shard["w_br_sb"].shape[0] // HEAD_DIM
    nm = shard["w_br_mem"].shape[0] // HEAD_DIM
    w = dict(w)

    def take(names, gathered):
        for name, a in zip(names, gathered):
            w[name] = _whole(name, a)

    take(GATHER_FIRST, _exchange(_Gather([shard[name] for name in GATHER_FIRST]), "gather_first"))
    fq, fk, fv = 0, nf, 2 * nf
    sq, sk, sv = 3 * nf, 3 * nf + nsb, 3 * nf + 2 * nsb
    mq = 3 * nf + 3 * nsb

    h, rstd1 = _rms_fwd(x, w["g_mix"], "rms_mix_fwd")
    proj, moved = _mm(h, w["w_in_a"], "nn", BF16, "proj_att", carry=_Gather([shard[name] for name in W_IN_GATES[:2]]))
    take(W_IN_GATES[:2], moved)
    f_logit = _mm(h, w["w_in_f"], "nn", F32, "proj_forget")
    gate0, moved = _mm(h, w["w_in_g0"], "nn", BF16, "proj_gate0", carry=_Gather([shard[W_IN_GATES[2]]]))
    take(W_IN_GATES[2:], moved)
    gates = (gate0, _mm(h, w["w_in_g1"], "nn", BF16, "proj_gate1"), _mm(h, w["w_in_g2"], "nn", BF16, "proj_gate2"))
    c_sum = _forget_fwd(f_logit, w["b_forget"])
    c_t = c_sum[:, :nf].T
    c_col, c_row = c_t[:, :, None], c_t[:, None, :]
    qn = _headnorm_fwd(proj, fq, nf, w["g_q_fox"], "fox_qnorm_fwd")
    kn = _headnorm_fwd(proj, fk, nf, w["g_k_fox"], "fox_knorm_fwd")
    (o_fox, o_fox32, lse), moved = _fox_fwd(qn, kn, proj, fv, c_col, c_row, nf,
                                            carry=_Gather([shard[name] for name in GATHER_MIX]))
    take(GATHER_MIX, moved)
    (o_sb,), moved = _sb_fwd(proj, sq, sk, sv, nsb, carry=_Gather([shard["w_up"]]))
    take(("w_up",), moved)
    memn, rstd_m = _rms_fwd(mem, w["g_mem"], "rms_mem_fwd")
    mkv = _mm(memn, w["w_mem_kv"], "nn", BF16, "mem_kv")
    kmn = _headnorm_fwd(mkv, 0, nm, w["g_k_mem"], "mem_knorm_fwd")
    qmn = _headnorm_fwd(proj, mq, nm, w["g_q_mem"], "mem_qnorm_fwd")
    o_mem = _mem_fwd(qmn, kmn, mkv, nm)
    p0 = _mm(o_fox, w["w_br_fox"], "nn", F32, "branch_fox")
    p1 = _mm(o_sb, w["w_br_sb"], "nn", F32, "branch_sb")
    p2 = _mm(o_mem, w["w_br_mem"], "nn", F32, "branch_mem")
    merged = _merge_fwd(p0, p1, p2, gates, w["b_gate"])
    x1 = _mm(merged, w["w_out"], "nn", F32, "out_proj", residual=x)
    h2, rstd2 = _rms_fwd(x1, w["g_ffn"], "rms_ffn_fwd")
    up = _mm(h2, w["w_up"], "nn", BF16, "ffn_up")
    act = _conv_act_fwd(up, w["conv_w"], w["conv_b"])
    dy, dyb, lparts = _mm(act, w["w_down"], "nn", F32, "ffn_down_loss", residual=x1, loss_target=target)
    loss = (0.5 / d) * jnp.sum(lparts[::8, ::LANES])

    g = {}
    dact = _mm(dyb, w["w_down"], "nt", BF16, "ffn_down_dx")
    g["w_down"] = _mm(act, dyb, "tn", BF16, "ffn_down_dw")
    dug, duv, dwg, dwv, dbg, dbv = _conv_act_bwd(up, w["conv_w"], w["conv_b"], dact)
    dup = jnp.concatenate([dug, duv], axis=1)
    g["conv_w"] = jnp.concatenate([dwg, dwv], axis=1)
    g["conv_b"] = jnp.concatenate([dbg, dbv], axis=1)
    dh2 = _mm(dup, w["w_up"], "nt", BF16, "ffn_up_dx")
    g["w_up"] = _mm(h2, dup, "tn", BF16, "ffn_up_dw")
    sums_ffn = _sibling_sums(REDUCE_FFN, g, core, "ffn")
    dx1, dx1b, g["g_ffn"] = _rms_bwd(dh2, x1, rstd2, w["g_ffn"], dy, "rms_ffn_bwd")
    dmerged = _mm(dx1b, w["w_out"], "nt", BF16, "out_proj_dx")
    g["w_out"] = _mm(merged, dx1b, "tn", BF16, "out_proj_dw")
    dp0, dp1, dp2, dga, dgb, dgc, g["b_gate"] = _merge_bwd(dmerged, p0, p1, p2, gates, w["b_gate"])
    dgates = (dga, dgb, dgc)
    for name, dgate in zip(W_IN_GATES, dgates):
        g[name] = _mm(h, dgate, "tn", BF16, name + "_dw")
    do_fox = _mm(dp0, w["w_br_fox"], "nt", BF16, "branch_fox_dx")
    do_sb = _mm(dp1, w["w_br_sb"], "nt", BF16, "branch_sb_dx")
    do_mem = _mm(dp2, w["w_br_mem"], "nt", BF16, "branch_mem_dx")
    g["w_br_fox"] = _mm(o_fox, dp0, "tn", BF16, "branch_fox_dw")
    g["w_br_sb"] = _mm(o_sb, dp1, "tn", BF16, "branch_sb_dw")
    g["w_br_mem"] = _mm(o_mem, dp2, "tn", BF16, "branch_mem_dw")
    sums_mix = _sibling_sums(REDUCE_MIX, g, core, "mix")

    (dqn, dkn, dfv, drs, dcs), others_ffn = _fox_bwd(qn, kn, proj, fv, c_col, c_row, o_fox32, do_fox, lse, nf,
                                                     carry=_Scatter(sums_ffn))
    dfq, g["g_q_fox"] = _headnorm_bwd(dqn, proj, fq, nf, w["g_q_fox"], "fox_qnorm_bwd")
    dfk, g["g_k_fox"] = _headnorm_bwd(dkn, proj, fk, nf, w["g_k_fox"], "fox_knorm_bwd")
    dc = jnp.pad((drs[:, :, 0] - dcs[:, 0, :]).T, ((0, 0), (0, LANES - nf)))
    df, g["b_forget"] = _forget_bwd(dc, f_logit, w["b_forget"])
    (dsq, dsk, dsv), others_mix = _sb_bwd(proj, sq, sk, sv, do_sb, nsb, carry=_Scatter(sums_mix))
    dqmn, dkmn, dvm = _mem_bwd(qmn, kmn, mkv, do_mem, nm)
    dmq, g["g_q_mem"] = _headnorm_bwd(dqmn, proj, mq, nm, w["g_q_mem"], "mem_qnorm_bwd")
    dkm, g["g_k_mem"] = _headnorm_bwd(dkmn, mkv, 0, nm, w["g_k_mem"], "mem_knorm_bwd")
    dmkv = jnp.concatenate([dkm, dvm.astype(BF16)], axis=1)
    g["w_mem_kv"] = _mm(memn, dmkv, "tn", BF16, "mem_kv_dw")
    dmemn = _mm(dmkv, w["w_mem_kv"], "nt", BF16, "mem_kv_dx")
    _, _, g["g_mem"] = _rms_bwd(dmemn, mem, rstd_m, w["g_mem"], None, "rms_mem_bwd")

    dproj = jnp.concatenate([dfq, dfk, dfv.astype(BF16), dsq.astype(BF16), dsk.astype(BF16), dsv.astype(BF16), dmq],
                            axis=1)
    dfb = df.astype(BF16)
    g["w_in_a"] = _mm(h, dproj, "tn", BF16, "proj_att_dw")
    g["w_in_f"] = _mm(h, dfb, "tn", BF16, "proj_forget_dw")
    sums_in = _sibling_sums(REDUCE_IN, g, core, "in")
    dh, others_in = _mm(dproj, w["w_in_a"], "nt", F32, "proj_att_dx", carry=_Scatter(sums_in))
    dh = _mm(dfb, w["w_in_f"], "nt", F32, "proj_forget_dx", residual=dh)
    for name, dgate in zip(W_IN_GATES, dgates):
        dh = _mm(dgate, w[name], "nt", F32, name + "_dx", residual=dh)
    grad_x, _, g["g_mix"] = _rms_bwd(dh, x, rstd1, w["g_mix"], dx1, "rms_mix_bwd")

    names = REDUCE_FFN + REDUCE_MIX + REDUCE_IN
    finals = [_add_chips(own, theirs, chip_core, "add_chips_" + name)
              for name, own, theirs in zip(names, sums_ffn + sums_mix + sums_in, others_ffn + others_mix + others_in)]
    summed = {name: a.reshape(2 * a.shape[1], a.shape[2]) for name, a in zip(names, _join_halves(finals))}
    return loss, grad_x, g, summed


SMALL = ("g_mix", "b_forget", "g_q_fox", "g_k_fox", "g_mem", "g_q_mem", "g_k_mem", "b_gate", "g_ffn", "conv_w",
         "conv_b")
SMALL_SHARDED = ("b_gate", "conv_w")
PACK_ROWS = 8


def _pack(arrs):
    flat = jnp.concatenate([a.reshape(-1) for a in arrs])
    unit = PACK_ROWS * LANES
    flat = jnp.pad(flat, (0, -flat.shape[0] % unit))
    return flat.reshape(-1, LANES)


def _unpack(packed, shapes):
    flat = packed.reshape(-1)
    out, at = [], 0
    for s in shapes:
        n = 1
        for dim in s:
            n *= dim
        out.append(flat[at:at + n].reshape(s))
        at += n
    return out


def kernel(x, mem, g_mix, w_in, b_forget, g_q_fox, g_k_fox, g_mem, w_mem_kv, g_q_mem, g_k_mem, w_br_fox, w_br_sb, w_br_mem, b_gate, w_out, g_ffn, w_up, conv_w, conv_b, w_down, loss_target, m_g_mix, m_w_in, m_b_forget, m_g_q_fox, m_g_k_fox, m_g_mem, m_w_mem_kv, m_g_q_mem, m_g_k_mem, m_w_br_fox, m_w_br_sb, m_w_br_mem, m_b_gate, m_w_out, m_g_ffn, m_w_up, m_conv_w, m_conv_b, m_w_down, v_g_mix, v_w_in, v_b_forget, v_g_q_fox, v_g_k_fox, v_g_mem, v_w_mem_kv, v_g_q_mem, v_g_k_mem, v_w_br_fox, v_w_br_sb, v_w_br_mem, v_b_gate, v_w_out, v_g_ffn, v_w_up, v_conv_w, v_conv_b, v_w_down):
    given = dict(g_mix=g_mix, w_in=w_in, b_forget=b_forget, g_q_fox=g_q_fox, g_k_fox=g_k_fox, g_mem=g_mem,
                 w_mem_kv=w_mem_kv, g_q_mem=g_q_mem, g_k_mem=g_k_mem, w_br_fox=w_br_fox, w_br_sb=w_br_sb,
                 w_br_mem=w_br_mem, b_gate=b_gate, w_out=w_out, g_ffn=g_ffn, w_up=w_up, conv_w=conv_w, conv_b=conv_b,
                 w_down=w_down)
    m_in = dict(g_mix=m_g_mix, w_in=m_w_in, b_forget=m_b_forget, g_q_fox=m_g_q_fox, g_k_fox=m_g_k_fox, g_mem=m_g_mem,
                w_mem_kv=m_w_mem_kv, g_q_mem=m_g_q_mem, g_k_mem=m_g_k_mem, w_br_fox=m_w_br_fox, w_br_sb=m_w_br_sb,
                w_br_mem=m_w_br_mem, b_gate=m_b_gate, w_out=m_w_out, g_ffn=m_g_ffn, w_up=m_w_up, conv_w=m_conv_w,
                conv_b=m_conv_b, w_down=m_w_down)
    v_in = dict(g_mix=v_g_mix, w_in=v_w_in, b_forget=v_b_forget, g_q_fox=v_g_q_fox, g_k_fox=v_g_k_fox, g_mem=v_g_mem,
                w_mem_kv=v_w_mem_kv, g_q_mem=v_g_q_mem, g_k_mem=v_g_k_mem, w_br_fox=v_w_br_fox, w_br_sb=v_w_br_sb,
                w_br_mem=v_w_br_mem, b_gate=v_b_gate, w_out=v_w_out, g_ffn=v_g_ffn, w_up=v_w_up, conv_w=v_conv_w,
                conv_b=v_conv_b, w_down=v_w_down)
    layered = {k: a.ndim == 3 for k, a in given.items()}
    drop = lambda a: a[0] if a.ndim == 3 else a
    given = {k: drop(a) for k, a in given.items()}
    m_in = {k: drop(a) for k, a in m_in.items()}
    v_in = {k: drop(a) for k, a in v_in.items()}

    xi, yi, ci = lax.axis_index("x"), lax.axis_index("y"), lax.axis_index("c")
    chip = (2 * xi + yi).astype(jnp.int32)
    core_arr = ci.astype(jnp.int32).reshape(1)
    chip_core = jnp.stack([chip, ci.astype(jnp.int32)])

    nf = given["b_forget"].shape[1]
    cut = 3 * given["w_br_fox"].shape[0]

    d_model = given["w_out"].shape[1]
    gate0 = given["w_in"].shape[1] - len(W_IN_GATES) * d_model
    shard = {
        "w_in_a": jnp.concatenate([given["w_in"][:, :cut], given["w_in"][:, cut + nf:gate0]], axis=1).astype(BF16),
        "w_in_f": jnp.pad(given["w_in"][:, cut:cut + nf], ((0, 0), (0, LANES - nf))).astype(BF16),
    }
    for b, name in enumerate(W_IN_GATES):
        shard[name] = given["w_in"][:, gate0 + b * d_model:gate0 + (b + 1) * d_model].astype(BF16)
    for name in BIG:
        if name not in shard:
            shard[name] = given[name].astype(BF16)
    w = {}
    small_shapes = [given[name].shape for name in SMALL_SHARDED]
    small_parts = _gather_small(_pack([given[name] for name in SMALL_SHARDED]))[0::2]
    per_chip = [_unpack(small_parts[j], small_shapes) for j in range(N_CHIPS)]
    for k, name in enumerate(SMALL_SHARDED):
        w[name] = jnp.concatenate([per_chip[j][k] for j in range(N_CHIPS)], axis=1)
    for name in SMALL:
        if name not in SMALL_SHARDED:
            w[name] = given[name]
    w["b_forget"] = jnp.pad(given["b_forget"], ((0, 0), (0, LANES - nf)))

    loss, grad_x, g, summed = _local_step(x[0], mem[0], loss_target[0], w, shard, core_arr, chip_core)
    loss = lax.psum(loss, ("x", "y", "c"))
    grads = {name: summed[name] for name in BIG if name in given}
    grads["w_in"] = jnp.concatenate([summed["w_in_a"][:, :cut], summed["w_in_f"][:, :nf], summed["w_in_a"][:, cut:]]
                                    + [summed[name] for name in W_IN_GATES], axis=1)

    g["b_forget"] = g["b_forget"][:, :nf]
    small_full_shapes = [g[name].shape for name in SMALL]
    small_sum = _unpack(_sum_devices(_gather_small(_pack([g[name] for name in SMALL]))), small_full_shapes)
    for name, a in zip(SMALL, small_sum):
        if name in SMALL_SHARDED:
            width = given[name].shape[1]
            a = lax.dynamic_slice_in_dim(a, chip * width, width, axis=1)
        grads[name] = a

    delta, new_m, new_v = {}, {}, {}
    for name in WEIGHTS:
        if name not in SMALL:
            delta[name], new_m[name], new_v[name] = _adamw(given[name], grads[name], m_in[name], v_in[name],
                                                           "adamw_" + name)
    shapes = [given[name].shape for name in SMALL]
    packed = [_pack([src[name] for name in SMALL]) for src in (given, grads, m_in, v_in)]
    for dst, res in zip((delta, new_m, new_v), _adamw(*packed, "adamw_small")):
        for name, a in zip(SMALL, _unpack(res, shapes)):
            dst[name] = a

    out = [loss, grad_x[None]]
    for src in (grads, delta, new_m, new_v):
        out.extend(src[name][None] if layered[name] else src[name] for name in WEIGHTS)
    return tuple(out)
```

```python
import functools

import jax
import jax.numpy as jnp
from jax import lax
from jax.experimental import pallas as pl
from jax.experimental.pallas import tpu as pltpu

F32 = jnp.float32
BF16 = jnp.bfloat16

HEAD_DIM = 128
EPS = 1e-6
NEG_BIG = -1e30

ADAM_LR = 0.001
ADAM_B1 = 0.9
ADAM_B2 = 0.999
ADAM_EPS = 1e-08
ADAM_WD = 0.01
ADAM_STEP = 10

LANES = 128
BF16_SUBLANES = 16
VMEM_LIMIT_BYTES = 56 * 1024 * 1024
MM_TILE = 1024
MM_TILE_K = 2048
ATT_TILE = 256
ROW_TILE = 256
HEADNORM_ROWS = 512
COL_TILE = 512
ADAM_BLOCK_BYTES = 2 << 20

N_CHIPS = 4
N_DEV = 8
MESH = pl.DeviceIdType.MESH

IN_NAMES = ['x', 'mem', 'g_mix', 'w_in', 'b_forget', 'g_q_fox', 'g_k_fox', 'g_mem', 'w_mem_kv', 'g_q_mem', 'g_k_mem',
            'w_br_fox', 'w_br_sb', 'w_br_mem', 'b_gate', 'w_out', 'g_ffn', 'w_up', 'conv_w', 'conv_b', 'w_down']
WEIGHTS = IN_NAMES[2:]


def _tile(n, target):
    if n <= target:
        return n
    for t in range(target - target % LANES, LANES - 1, -LANES):
        if n % t == 0:
            return t
    return n


def _params(*sem):
    return pltpu.CompilerParams(dimension_semantics=sem, vmem_limit_bytes=VMEM_LIMIT_BYTES)


def _log_sigmoid(z):
    return jnp.minimum(z, 0.0) - jnp.log(1.0 + jnp.exp(-jnp.abs(z)))


def _split2(v):
    hi = v.astype(BF16)
    lo = (v - hi.astype(F32)).astype(BF16)
    return hi, lo


def _split3(v):
    hi = v.astype(BF16)
    r = v - hi.astype(F32)
    mid = r.astype(BF16)
    lo = (r - mid.astype(F32)).astype(BF16)
    return hi, mid, lo


def _dot(a, b):
    return lax.dot_general(a, b, (((1,), (0,)), ((), ())), preferred_element_type=F32)


def _dot_nt(a, b):
    return lax.dot_general(a, b, (((1,), (1,)), ((), ())), preferred_element_type=F32)


def _dot_tn(a, b):
    return lax.dot_general(a, b, (((0,), (0,)), ((), ())), preferred_element_type=F32)


ANY = pl.BlockSpec(memory_space=pl.ANY)


def _place():
    x, y, c = lax.axis_index("x"), lax.axis_index("y"), lax.axis_index("c")
    others = [(1 - x, y), (x, 1 - y), (1 - x, 1 - y)]
    return x, y, c, others


def _remote(src, dst, send_sem, recv_sem, to):
    return pltpu.make_async_remote_copy(src_ref=src, dst_ref=dst, send_sem=send_sem, recv_sem=recv_sem,
                                        device_id=to, device_id_type=MESH)


class _Gather:
    PER_SHARD = 7

    def __init__(self, shards):
        self.inputs = list(shards)
        n = len(shards) * self.PER_SHARD
        self.out_shapes = [jax.ShapeDtypeStruct((N_CHIPS,) + s.shape, s.dtype) for s in shards]
        self.scratch = [pltpu.SemaphoreType.DMA((n,)), pltpu.SemaphoreType.DMA((n,))]

    def _first(self, ins, outs, sems):
        send_sems, recv_sems = sems
        x, y, c, others = _place()
        me = 2 * x + y
        k = self.PER_SHARD
        copies = []
        for i in range(len(ins)):
            h = ins[i].shape[0] // 2
            mine = pl.ds(pl.multiple_of(c * h, BF16_SUBLANES), h)
            for j, (ox, oy) in enumerate(others):
                copies.append(_remote(ins[i].at[mine], outs[i].at[me, mine], send_sems.at[k * i + j],
                                      recv_sems.at[k * i + j], (ox, oy, c)))
            copies.append(_remote(ins[i], outs[i].at[me], send_sems.at[k * i + 6], recv_sems.at[k * i + 6],
                                  (x, y, 1 - c)))
        return copies

    def start(self, ins, outs, sems):
        for cp in self._first(ins, outs, sems):
            cp.start()

    def finish(self, ins, outs, sems):
        send_sems, recv_sems = sems
        x, y, c, others = _place()
        me = 2 * x + y
        sibling = (x, y, 1 - c)
        k = self.PER_SHARD
        passed = []
        for i in range(len(ins)):
            h = ins[i].shape[0] // 2
            mine = pl.ds(pl.multiple_of(c * h, BF16_SUBLANES), h)
            for j, (ox, oy) in enumerate(others):
                blk = outs[i].at[2 * ox + oy, mine]
                _remote(blk, blk, send_sems.at[k * i + j], recv_sems.at[k * i + j], (ox, oy, c)).wait_recv()
                cp = _remote(blk, blk, send_sems.at[k * i + 3 + j], recv_sems.at[k * i + 3 + j], sibling)
                cp.start()
                passed.append(cp)
        for i in range(len(ins)):
            h = ins[i].shape[0] // 2
            theirs = pl.ds(pl.multiple_of((1 - c) * h, BF16_SUBLANES), h)
            for j, (ox, oy) in enumerate(others):
                blk = outs[i].at[2 * ox + oy, theirs]
                _remote(blk, blk, send_sems.at[k * i + 3 + j], recv_sems.at[k * i + 3 + j], sibling).wait_recv()
            own = outs[i].at[me]
            _remote(own, own, send_sems.at[k * i + 6], recv_sems.at[k * i + 6], sibling).wait_recv()
        for cp in self._first(ins, outs, sems) + passed:
            cp.wait_send()


class _Scatter:
    def __init__(self, sums):
        self.inputs = list(sums)
        k = N_CHIPS - 1
        self.out_shapes = [jax.ShapeDtypeStruct((k,) + g.shape[1:], g.dtype) for g in sums]
        self.scratch = [pltpu.SemaphoreType.DMA((k * len(sums),)), pltpu.SemaphoreType.DMA((k * len(sums),))]

    def _copies(self, ins, outs, sems):
        send_sems, recv_sems = sems
        _, _, c, others = _place()
        k = N_CHIPS - 1
        return [_remote(ins[i].at[2 * ox + oy], outs[i].at[j], send_sems.at[k * i + j], recv_sems.at[k * i + j],
                        (ox, oy, c))
                for i in range(len(ins)) for j, (ox, oy) in enumerate(others)]

    def start(self, ins, outs, sems):
        for cp in self._copies(ins, outs, sems):
            cp.start()

    def finish(self, ins, outs, sems):
        for cp in self._copies(ins, outs, sems):
            cp.wait()


def _exchange(carry, name):
    n = len(carry.inputs)

    def body(*refs):
        ins, outs, sems = refs[:n], refs[n:2 * n], refs[2 * n:]
        carry.start(ins, outs, sems)
        carry.finish(ins, outs, sems)

    return pl.pallas_call(
        body,
        in_specs=[ANY] * n,
        out_specs=[ANY] * n,
        out_shape=carry.out_shapes,
        scratch_shapes=carry.scratch,
        name=name,
    )(*carry.inputs)


def _call(body, *, grid, in_specs, out_specs, out_shape, scratch_shapes, semantics, name, args, carry=None):
    n_in, n_out, n_scr = len(in_specs), len(out_specs), len(scratch_shapes)
    if carry is None:
        res = pl.pallas_call(body, grid=grid, in_specs=in_specs, out_specs=out_specs, out_shape=out_shape,
                             scratch_shapes=scratch_shapes, compiler_params=_params(*semantics), name=name)(*args)
        return list(res), []
    nci, nco = len(carry.inputs), len(carry.out_shapes)
    a, b = n_in, n_in + nci
    c, d = b + n_out, b + n_out + nco
    e = d + n_scr

    def carried(*refs):
        ids = [pl.program_id(k) for k in range(len(grid))]
        first = functools.reduce(jnp.logical_and, [i == 0 for i in ids])
        last = functools.reduce(jnp.logical_and, [i == n - 1 for i, n in zip(ids, grid)])

        @pl.when(first)
        def _():
            carry.start(refs[a:b], refs[c:d], refs[e:])

        body(*refs[:a], *refs[b:c], *refs[d:e])

        @pl.when(last)
        def _():
            carry.finish(refs[a:b], refs[c:d], refs[e:])

    res = pl.pallas_call(
        carried,
        grid=grid,
        in_specs=list(in_specs) + [ANY] * nci,
        out_specs=list(out_specs) + [ANY] * nco,
        out_shape=list(out_shape) + carry.out_shapes,
        scratch_shapes=list(scratch_shapes) + carry.scratch,
        compiler_params=_params(*(["arbitrary"] * len(grid))),
        name=name,
    )(*args, *carry.inputs)
    return list(res[:n_out]), list(res[n_out:])


def _mm(a, b, mode, out_dtype, name, residual=None, carry=None, loss_target=None):
    if mode == "nn":
        (m, k), (k2, n) = a.shape, b.shape
    elif mode == "nt":
        (m, k), (n, k2) = a.shape, b.shape
    else:
        (k, m), (k2, n) = a.shape, b.shape
    assert k == k2, (a.shape, b.shape, mode)
    tm, tn, tk = _tile(m, MM_TILE), _tile(n, MM_TILE), _tile(k, MM_TILE_K)
    nk = k // tk
    dot = {"nn": _dot, "nt": _dot_nt, "tn": _dot_tn}[mode]
    has_res = residual is not None
    has_loss = loss_target is not None
    n_in = 2 + has_res + has_loss

    def body(*refs):
        a_ref, b_ref = refs[:2]
        r_ref = refs[2] if has_res else None
        t_ref = refs[n_in - 1] if has_loss else None
        o_ref = refs[n_in]

        def finish(acc):
            if has_res:
                acc = acc + r_ref[...]
            if has_loss:
                err = acc - t_ref[...]
                dy = err * (1.0 / n)
                o_ref[...] = dy
                refs[n_in + 1][...] = dy.astype(BF16)
                tot = jnp.sum(jnp.sum(err * err, axis=-1, keepdims=True), axis=0, keepdims=True)
                refs[n_in + 2][...] = jnp.broadcast_to(tot, (8, LANES))
            else:
                o_ref[...] = acc.astype(o_ref.dtype)

        part = dot(a_ref[...], b_ref[...])
        if nk == 1:
            finish(part)
        else:
            acc_ref = refs[-1]
            kk = pl.program_id(2)

            @pl.when(kk == 0)
            def _():
                acc_ref[...] = part

            @pl.when(kk > 0)
            def _():
                acc_ref[...] += part

            @pl.when(kk == nk - 1)
            def _():
                finish(acc_ref[...])

    if mode == "tn":
        a_spec = pl.BlockSpec((tk, tm), lambda j, i, kk: (kk, i))
    else:
        a_spec = pl.BlockSpec((tm, tk), lambda j, i, kk: (i, kk))
    if mode == "nt":
        b_spec = pl.BlockSpec((tn, tk), lambda j, i, kk: (j, kk))
    else:
        b_spec = pl.BlockSpec((tk, tn), lambda j, i, kk: (kk, j))
    o_spec = pl.BlockSpec((tm, tn), lambda j, i, kk: (i, j))
    in_specs = [a_spec, b_spec] + [o_spec] * (has_res + has_loss)
    args = (a, b) + ((residual,) if has_res else ()) + ((loss_target,) if has_loss else ())
    out_specs, out_shape = [o_spec], [jax.ShapeDtypeStruct((m, n), out_dtype)]
    if has_loss:
        out_specs += [o_spec, pl.BlockSpec((8, LANES), lambda j, i, kk: (i, j))]
        out_shape += [jax.ShapeDtypeStruct((m, n), BF16), jax.ShapeDtypeStruct((m // tm * 8, n // tn * LANES), F32)]
    outs, moved = _call(
        body,
        grid=(n // tn, m // tm, nk),
        in_specs=in_specs,
        out_specs=out_specs,
        out_shape=out_shape,
        scratch_shapes=[pltpu.VMEM((tm, tn), F32)] if nk > 1 else [],
        semantics=("parallel", "parallel", "arbitrary"),
        name=name,
        args=args,
        carry=carry,
    )
    out = outs if has_loss else outs[0]
    return out if carry is None else (out, moved)


def _rms_fwd(x, g, name):
    s, d = x.shape
    tm = _tile(s, ROW_TILE)

    def body(x_ref, g_ref, h_ref, r_ref):
        xf = x_ref[...]
        r = lax.rsqrt(jnp.mean(xf * xf, axis=-1, keepdims=True) + EPS)
        h_ref[...] = ((xf * r) * g_ref[...]).astype(BF16)
        r_ref[...] = r

    return pl.pallas_call(
        body,
        grid=(s // tm,),
        in_specs=[pl.BlockSpec((tm, d), lambda i: (i, 0)), pl.BlockSpec((1, d), lambda i: (0, 0))],
        out_specs=[pl.BlockSpec((tm, d), lambda i: (i, 0)), pl.BlockSpec((tm, 1), lambda i: (i, 0))],
        out_shape=[jax.ShapeDtypeStruct((s, d), BF16), jax.ShapeDtypeStruct((s, 1), F32)],
        compiler_params=_params("parallel"),
        name=name,
    )(x, g)


def _rms_bwd(dh, x, rstd, g, res, name):
    s, d = x.shape
    tm = _tile(s, ROW_TILE)
    has_res = res is not None

    def body(*refs):
        if has_res:
            dh_ref, x_ref, r_ref, g_ref, res_ref, dx_ref, dxb_ref, dg_ref = refs
        else:
            dh_ref, x_ref, r_ref, g_ref, dx_ref, dxb_ref, dg_ref = refs
        dhf = dh_ref[...].astype(F32)
        xhat = x_ref[...] * r_ref[...]
        dy = dhf * g_ref[...]
        dx = r_ref[...] * (dy - xhat * jnp.mean(dy * xhat, axis=-1, keepdims=True))
        if has_res:
            dx = dx + res_ref[...]
        dx_ref[...] = dx
        dxb_ref[...] = dx.astype(BF16)
        part = jnp.sum(dhf * xhat, axis=0, keepdims=True)

        @pl.when(pl.program_id(0) == 0)
        def _():
            dg_ref[...] = part

        @pl.when(pl.program_id(0) > 0)
        def _():
            dg_ref[...] += part

    row = pl.BlockSpec((tm, d), lambda i: (i, 0))
    vec = pl.BlockSpec((1, d), lambda i: (0, 0))
    in_specs = [row, row, pl.BlockSpec((tm, 1), lambda i: (i, 0)), vec] + ([row] if has_res else [])
    args = (dh, x, rstd, g) + ((res,) if has_res else ())
    return pl.pallas_call(
        body,
        grid=(s // tm,),
        in_specs=in_specs,
        out_specs=[row, row, vec],
        out_shape=[jax.ShapeDtypeStruct((s, d), F32), jax.ShapeDtypeStruct((s, d), BF16),
                   jax.ShapeDtypeStruct((1, d), F32)],
        compiler_params=_params("arbitrary"),
        name=name,
    )(*args)


def _headnorm_fwd(src, col0, nheads, g, name):
    s = src.shape[0]
    tm = _tile(s, HEADNORM_ROWS)
    w = nheads * HEAD_DIM
    assert col0 % nheads == 0

    def body(x_ref, g_ref, o_ref):
        for hh in range(nheads):
            xf = _head(x_ref, hh).astype(F32)
            r = lax.rsqrt(jnp.mean(xf * xf, axis=-1, keepdims=True) + EPS)
            o_ref[:, hh * HEAD_DIM:(hh + 1) * HEAD_DIM] = ((xf * r) * g_ref[...]).astype(BF16)

    return pl.pallas_call(
        body,
        grid=(s // tm,),
        in_specs=[pl.BlockSpec((tm, w), lambda i: (i, col0 // nheads)),
                  pl.BlockSpec((1, HEAD_DIM), lambda i: (0, 0))],
        out_specs=pl.BlockSpec((tm, w), lambda i: (i, 0)),
        out_shape=jax.ShapeDtypeStruct((s, w), BF16),
        compiler_params=_params("parallel"),
        name=name,
    )(src, g)


def _headnorm_bwd(dxn, src, col0, nheads, g, name):
    s = src.shape[0]
    tm = _tile(s, HEADNORM_ROWS)
    w = nheads * HEAD_DIM
    assert col0 % nheads == 0

    def body(d_ref, x_ref, g_ref, dx_ref, dg_ref):
        part = jnp.zeros((1, HEAD_DIM), F32)
        for hh in range(nheads):
            xf = _head(x_ref, hh).astype(F32)
            r = lax.rsqrt(jnp.mean(xf * xf, axis=-1, keepdims=True) + EPS)
            xhat = xf * r
            dn = _head(d_ref, hh).astype(F32)
            dy = dn * g_ref[...]
            dx = r * (dy - xhat * jnp.mean(dy * xhat, axis=-1, keepdims=True))
            dx_ref[:, hh * HEAD_DIM:(hh + 1) * HEAD_DIM] = dx.astype(BF16)
            part = part + jnp.sum(dn * xhat, axis=0, keepdims=True)

        @pl.when(pl.program_id(0) == 0)
        def _():
            dg_ref[...] = part

        @pl.when(pl.program_id(0) > 0)
        def _():
            dg_ref[...] += part

    return pl.pallas_call(
        body,
        grid=(s // tm,),
        in_specs=[pl.BlockSpec((tm, w), lambda i: (i, 0)),
                  pl.BlockSpec((tm, w), lambda i: (i, col0 // nheads)),
                  pl.BlockSpec((1, HEAD_DIM), lambda i: (0, 0))],
        out_specs=[pl.BlockSpec((tm, w), lambda i: (i, 0)),
                   pl.BlockSpec((1, HEAD_DIM), lambda i: (0, 0))],
        out_shape=[jax.ShapeDtypeStruct((s, w), BF16), jax.ShapeDtypeStruct((1, HEAD_DIM), F32)],
        compiler_params=_params("arbitrary"),
        name=name,
    )(dxn, src, g)


def _tri(t, lower_inclusive):
    r = lax.broadcasted_iota(jnp.int32, (t, t), 0)
    c = lax.broadcasted_iota(jnp.int32, (t, t), 1)
    keep = (c <= r) if lower_inclusive else (c >= r)
    return jnp.where(keep, 1.0, 0.0).astype(BF16)


def _forget_fwd(f_logit, b_pad):
    s = f_logit.shape[0]
    t = _tile(s, ATT_TILE)

    def body(f_ref, b_ref, c_ref, carry):
        @pl.when(pl.program_id(0) == 0)
        def _():
            carry[...] = jnp.zeros_like(carry)

        lf = _log_sigmoid(f_ref[...] + b_ref[...])
        tri = _tri(t, True)
        acc = carry[...]
        for part in _split3(lf):
            acc = acc + _dot(tri, part)
        c_ref[...] = acc
        carry[...] += jnp.sum(lf, axis=0, keepdims=True)

    return pl.pallas_call(
        body,
        grid=(s // t,),
        in_specs=[pl.BlockSpec((t, LANES), lambda i: (i, 0)), pl.BlockSpec((1, LANES), lambda i: (0, 0))],
        out_specs=pl.BlockSpec((t, LANES), lambda i: (i, 0)),
        out_shape=jax.ShapeDtypeStruct((s, LANES), F32),
        scratch_shapes=[pltpu.VMEM((1, LANES), F32)],
        compiler_params=_params("arbitrary"),
        name="forget_fwd",
    )(f_logit, b_pad)


def _forget_bwd(dc, f_logit, b_pad):
    s = f_logit.shape[0]
    t = _tile(s, ATT_TILE)
    nb = s // t

    def body(dc_ref, f_ref, b_ref, df_ref, db_ref, carry):
        @pl.when(pl.program_id(0) == 0)
        def _():
            carry[...] = jnp.zeros_like(carry)
            db_ref[...] = jnp.zeros_like(db_ref)

        d = dc_ref[...]
        tri = _tri(t, False)
        acc = carry[...]
        for part in _split3(d):
            acc = acc + _dot(tri, part)
        z = f_ref[...] + b_ref[...]
        df = acc * jnp.exp(_log_sigmoid(-z))
        df_ref[...] = df
        db_ref[...] += jnp.sum(df, axis=0, keepdims=True)
        carry[...] += jnp.sum(d, axis=0, keepdims=True)

    rev = pl.BlockSpec((t, LANES), lambda i: (nb - 1 - i, 0))
    vec = pl.BlockSpec((1, LANES), lambda i: (0, 0))
    return pl.pallas_call(
        body,
        grid=(nb,),
        in_specs=[rev, rev, vec],
        out_specs=[rev, vec],
        out_shape=[jax.ShapeDtypeStruct((s, LANES), F32), jax.ShapeDtypeStruct((1, LANES), F32)],
        scratch_shapes=[pltpu.VMEM((1, LANES), F32)],
        compiler_params=_params("arbitrary"),
        name="forget_bwd",
    )(dc, f_logit, b_pad)


SB_FWD_GROUP = 6
FOX_GROUP = 6
SB_BWD_GROUP = 3


def _head(ref, hh, rows=slice(None)):
    return ref[rows, hh * HEAD_DIM:(hh + 1) * HEAD_DIM]


def _tri_mask(t, strict):
    r = lax.broadcasted_iota(jnp.int32, (t, t), 0)
    c = lax.broadcasted_iota(jnp.int32, (t, t), 1)
    return (c < r) if strict else (c <= r)


def _fox_fwd(qn, kn, proj, colv, c_col, c_row, nheads, carry=None):
    s = qn.shape[0]
    t = _tile(s, ATT_TILE)
    scale = HEAD_DIM ** -0.5
    hg = FOX_GROUP
    gw = hg * HEAD_DIM
    assert nheads % hg == 0 and colv % hg == 0

    def body(q_ref, k_ref, v_ref, cc_ref, cr_ref, o_ref, of_ref, lse_ref):
        qi = pl.program_id(1)
        causal = _tri_mask(t, False)

        def tile(kj, carry, diagonal):
            off = pl.multiple_of(kj * t, t)
            heads = range(hg)
            rows = pl.ds(off, t)
            qk = [_dot_nt(_head(q_ref, hh), _head(k_ref, hh, rows)) for hh in heads]
            sc = [qk[hh] * scale + (cc_ref[hh] - cr_ref[hh, :, rows]) for hh in heads]
            if diagonal:
                sc = [jnp.where(causal, sc[hh], NEG_BIG) for hh in heads]
            m_new = [jnp.maximum(carry[hh][0], jnp.max(sc[hh], axis=-1, keepdims=True)) for hh in heads]
            p = [jnp.exp(sc[hh] - m_new[hh]) for hh in heads]
            pv = [_dot(p[hh].astype(BF16), _head(v_ref, hh, rows)) for hh in heads]
            out = []
            for hh in heads:
                m, l, acc = carry[hh]
                alpha = jnp.exp(m - m_new[hh])
                out.append((m_new[hh], alpha * l + jnp.sum(p[hh], axis=-1, keepdims=True), alpha * acc + pv[hh]))
            return tuple(out)

        init = tuple((jnp.full((t, 1), NEG_BIG, F32), jnp.zeros((t, 1), F32), jnp.zeros((t, HEAD_DIM), F32))
                     for _ in range(hg))
        carry = lax.fori_loop(0, qi, lambda kj, c: tile(kj, c, False), init)
        carry = tile(qi, carry, True)
        for hh in range(hg):
            m, l, acc = carry[hh]
            o = acc / l
            of_ref[:, hh * HEAD_DIM:(hh + 1) * HEAD_DIM] = o
            o_ref[:, hh * HEAD_DIM:(hh + 1) * HEAD_DIM] = o.astype(BF16)
            lse_ref[hh] = m + jnp.log(l)

    tile_spec = pl.BlockSpec((t, gw), lambda h, i: (i, h))
    w = nheads * HEAD_DIM
    return _call(
        body,
        grid=(nheads // hg, s // t),
        in_specs=[tile_spec,
                  pl.BlockSpec((s, gw), lambda h, i: (0, h), pipeline_mode=pl.Buffered(buffer_count=1)),
                  pl.BlockSpec((s, gw), lambda h, i: (0, colv // hg + h), pipeline_mode=pl.Buffered(buffer_count=1)),
                  pl.BlockSpec((hg, t, 1), lambda h, i: (h, i, 0)),
                  pl.BlockSpec((hg, 1, s), lambda h, i: (h, 0, 0))],
        out_specs=[tile_spec, tile_spec, pl.BlockSpec((hg, t, 1), lambda h, i: (h, i, 0))],
        out_shape=[jax.ShapeDtypeStruct((s, w), BF16), jax.ShapeDtypeStruct((s, w), F32),
                   jax.ShapeDtypeStruct((nheads, s, 1), F32)],
        scratch_shapes=[],
        semantics=("parallel", "parallel"),
        name="fox_fwd",
        args=(qn, kn, proj, c_col, c_row),
        carry=carry,
    )


def _fox_bwd(qn, kn, proj, colv, c_col, c_row, o, do, lse, nheads, carry=None):
    s = qn.shape[0]
    t = _tile(s, ATT_TILE)
    scale = HEAD_DIM ** -0.5
    hg = FOX_GROUP
    gw = hg * HEAD_DIM
    assert nheads % hg == 0 and colv % hg == 0

    def body(q_ref, k_ref, v_ref, cc_ref, cr_ref, o_ref, do_ref, lse_ref,
             dq_ref, dk_ref, dv_ref, drs_ref, dcs_ref):
        qi = pl.program_id(1)

        @pl.when(qi == 0)
        def _():
            dk_ref[...] = jnp.zeros_like(dk_ref)
            dv_ref[...] = jnp.zeros_like(dv_ref)
            dcs_ref[...] = jnp.zeros_like(dcs_ref)

        causal = _tri_mask(t, False)
        delta = [jnp.sum(_head(o_ref, hh) * _head(do_ref, hh).astype(F32), axis=-1, keepdims=True)
                 for hh in range(hg)]

        def tile(kj, carry, diagonal):
            off = pl.multiple_of(kj * t, t)
            heads = range(hg)
            rows = pl.ds(off, t)
            qk = [_dot_nt(_head(q_ref, hh), _head(k_ref, hh, rows)) for hh in heads]
            dp = [_dot_nt(_head(do_ref, hh), _head(v_ref, hh, rows)) for hh in heads]
            p = [jnp.exp(qk[hh] * scale + (cc_ref[hh] - cr_ref[hh, :, rows]) - lse_ref[hh]) for hh in heads]
            if diagonal:
                p = [jnp.where(causal, p[hh], 0.0) for hh in heads]
            ds = [p[hh] * (dp[hh] - delta[hh]) for hh in heads]
            dsb = [ds[hh].astype(BF16) for hh in heads]
            dv = [_dot_tn(p[hh].astype(BF16), _head(do_ref, hh)) for hh in heads]
            dk = [_dot_tn(dsb[hh], _head(q_ref, hh)) * scale for hh in heads]
            dq = [_dot(dsb[hh], _head(k_ref, hh, rows)) * scale for hh in heads]
            for hh in heads:
                cols = slice(hh * HEAD_DIM, (hh + 1) * HEAD_DIM)
                dv_ref[rows, cols] += dv[hh]
                dk_ref[rows, cols] += dk[hh]
                dcs_ref[hh, :, rows] += jnp.sum(ds[hh], axis=0, keepdims=True)
            return tuple((carry[hh][0] + dq[hh], carry[hh][1] + jnp.sum(ds[hh], axis=-1, keepdims=True))
                         for hh in heads)

        init = tuple((jnp.zeros((t, HEAD_DIM), F32), jnp.zeros((t, 1), F32)) for _ in range(hg))
        carry = lax.fori_loop(0, qi, lambda kj, c: tile(kj, c, False), init)
        carry = tile(qi, carry, True)
        for hh in range(hg):
            dq_ref[:, hh * HEAD_DIM:(hh + 1) * HEAD_DIM] = carry[hh][0]
            drs_ref[hh] = carry[hh][1]

    tile_spec = pl.BlockSpec((t, gw), lambda h, i: (i, h))
    once = pl.Buffered(buffer_count=1)
    full = pl.BlockSpec((s, gw), lambda h, i: (0, h), pipeline_mode=once)
    colspec = pl.BlockSpec((hg, t, 1), lambda h, i: (h, i, 0))
    rowspec = pl.BlockSpec((hg, 1, s), lambda h, i: (h, 0, 0))
    w = nheads * HEAD_DIM
    return _call(
        body,
        grid=(nheads // hg, s // t),
        in_specs=[tile_spec, full, pl.BlockSpec((s, gw), lambda h, i: (0, colv // hg + h), pipeline_mode=once),
                  colspec, rowspec,
                  tile_spec, tile_spec, colspec],
        out_specs=[tile_spec, full, full, colspec, rowspec],
        out_shape=[jax.ShapeDtypeStruct((s, w), F32), jax.ShapeDtypeStruct((s, w), F32),
                   jax.ShapeDtypeStruct((s, w), F32), jax.ShapeDtypeStruct((nheads, s, 1), F32),
                   jax.ShapeDtypeStruct((nheads, 1, s), F32)],
        scratch_shapes=[],
        semantics=("arbitrary", "arbitrary"),
        name="fox_bwd",
        args=(qn, kn, proj, c_col, c_row, o, do, lse),
        carry=carry,
    )


def _sb_tile(q, k, scale, later, valid):
    z = _dot_nt(q, k) * scale
    lb = _log_sigmoid(z)
    lm = lb - z
    if valid is not None:
        lm = jnp.where(valid, lm, 0.0)
    suffix = _dot(jnp.concatenate(_split2(lm), axis=1), later)
    return lb, lm, suffix


def _later(t):
    r = lax.broadcasted_iota(jnp.int32, (2 * t, t), 0) % t
    c = lax.broadcasted_iota(jnp.int32, (2 * t, t), 1)
    return jnp.where(r > c, 1.0, 0.0).astype(BF16)


def _sb_fwd(proj, colq, colk, colv, nheads, carry=None):
    s = proj.shape[0]
    t = _tile(s, ATT_TILE)
    scale = HEAD_DIM ** -0.5
    hg = SB_FWD_GROUP
    gw = hg * HEAD_DIM
    assert nheads % hg == 0 and colq % hg == 0 and colk % hg == 0 and colv % hg == 0

    def body(q_ref, k_ref, v_ref, o_ref):
        qi = pl.program_id(1)
        later = _later(t)
        before = _tri_mask(t, True)

        def tile(kj, carry, diagonal):
            off = pl.multiple_of(kj * t, t)
            heads = range(hg)
            z = [_dot_nt(_head(q_ref, hh), _head(k_ref, hh, pl.ds(off, t))) * scale for hh in heads]
            lb = [_log_sigmoid(z[hh]) for hh in heads]
            lm = [lb[hh] - z[hh] for hh in heads]
            if diagonal:
                lm = [jnp.where(before, lm[hh], 0.0) for hh in heads]
            parts = [jnp.concatenate(_split2(lm[hh]), axis=1) for hh in heads]
            suffix = [_dot(parts[hh], later) for hh in heads]
            a = [jnp.exp(lb[hh] + suffix[hh] + carry[hh][0]) for hh in heads]
            if diagonal:
                a = [jnp.where(before, a[hh], 0.0) for hh in heads]
            av = [_dot(a[hh].astype(BF16), _head(v_ref, hh, pl.ds(off, t))) for hh in heads]
            return tuple((carry[hh][0] + jnp.sum(lm[hh], axis=-1, keepdims=True), carry[hh][1] + av[hh])
                         for hh in heads)

        init = tuple((jnp.zeros((t, 1), F32), jnp.zeros((t, HEAD_DIM), F32)) for _ in range(hg))
        carry = tile(qi, init, True)
        carry = lax.fori_loop(1, qi + 1, lambda i, c: tile(qi - i, c, False), carry)
        for hh in range(hg):
            o_ref[:, hh * HEAD_DIM:(hh + 1) * HEAD_DIM] = carry[hh][1].astype(BF16)

    return _call(
        body,
        grid=(nheads // hg, s // t),
        in_specs=[pl.BlockSpec((t, gw), lambda h, i: (i, colq // hg + h)),
                  pl.BlockSpec((s, gw), lambda h, i: (0, colk // hg + h), pipeline_mode=pl.Buffered(buffer_count=1)),
                  pl.BlockSpec((s, gw), lambda h, i: (0, colv // hg + h), pipeline_mode=pl.Buffered(buffer_count=1))],
        out_specs=[pl.BlockSpec((t, gw), lambda h, i: (i, h))],
        out_shape=[jax.ShapeDtypeStruct((s, nheads * HEAD_DIM), BF16)],
        scratch_shapes=[],
        semantics=("parallel", "parallel"),
        name="sb_fwd",
        args=(proj, proj, proj),
        carry=carry,
    )


def _sb_bwd(proj, colq, colk, colv, do, nheads, carry=None):
    s = proj.shape[0]
    t = _tile(s, ATT_TILE)
    scale = HEAD_DIM ** -0.5
    hg = SB_BWD_GROUP
    gw = hg * HEAD_DIM
    assert nheads % hg == 0 and colq % hg == 0 and colk % hg == 0 and colv % hg == 0

    def body(q_ref, k_ref, v_ref, do_ref, dq_ref, dk_ref, dv_ref, g_s, beta_s):
        qi = pl.program_id(1)

        @pl.when(qi == 0)
        def _():
            dk_ref[...] = jnp.zeros_like(dk_ref)
            dv_ref[...] = jnp.zeros_like(dv_ref)

        later = _later(t)
        before = _tri_mask(t, True)

        def back(kj, carry, diagonal):
            off = pl.multiple_of(kj * t, t)
            heads = range(hg)
            rows = pl.ds(off, t)
            z = [_dot_nt(_head(q_ref, hh), _head(k_ref, hh, rows)) * scale for hh in heads]
            da = [_dot_nt(_head(do_ref, hh), _head(v_ref, hh, rows)) for hh in heads]
            lb = [_log_sigmoid(z[hh]) for hh in heads]
            lm = [lb[hh] - z[hh] for hh in heads]
            if diagonal:
                lm = [jnp.where(before, lm[hh], 0.0) for hh in heads]
            parts = [jnp.concatenate(_split2(lm[hh]), axis=1) for hh in heads]
            suffix = [_dot(parts[hh], later) for hh in heads]
            a = [jnp.exp(lb[hh] + suffix[hh] + carry[hh]) for hh in heads]
            if diagonal:
                a = [jnp.where(before, a[hh], 0.0) for hh in heads]
            dv = [_dot_tn(a[hh].astype(BF16), _head(do_ref, hh)) for hh in heads]
            for hh in heads:
                g_s[hh, :, rows] = a[hh] * da[hh]
                beta_s[hh, :, rows] = jnp.exp(lb[hh]).astype(BF16)
            for hh in heads:
                dv_ref[rows, hh * HEAD_DIM:(hh + 1) * HEAD_DIM] += dv[hh]
            return tuple(carry[hh] + jnp.sum(lm[hh], axis=-1, keepdims=True) for hh in heads)

        rc = back(qi, tuple(jnp.zeros((t, 1), F32) for _ in range(hg)), True)
        lax.fori_loop(1, qi + 1, lambda i, c: back(qi - i, c, False), rc)

        earlier = jnp.where(lax.broadcasted_iota(jnp.int32, (2 * t, t), 0) % t
                            < lax.broadcasted_iota(jnp.int32, (2 * t, t), 1), 1.0, 0.0).astype(BF16)

        def fwd(kj, carry, diagonal):
            off = pl.multiple_of(kj * t, t)
            heads = range(hg)
            rows = pl.ds(off, t)
            g = [g_s[hh, :, rows] for hh in heads]
            parts = [jnp.concatenate(_split2(g[hh]), axis=1) for hh in heads]
            gsum = [_dot(parts[hh], earlier) + carry[hh][0] for hh in heads]
            dz = []
            for hh in heads:
                beta = beta_s[hh, :, rows].astype(F32)
                d = g[hh] * (1.0 - beta) - gsum[hh] * beta
                if diagonal:
                    d = jnp.where(before, d, 0.0)
                dz.append(d.astype(BF16))
            dk = [_dot_tn(dz[hh], _head(q_ref, hh)) * scale for hh in heads]
            dq = [_dot(dz[hh], _head(k_ref, hh, rows)) * scale for hh in heads]
            for hh in heads:
                dk_ref[rows, hh * HEAD_DIM:(hh + 1) * HEAD_DIM] += dk[hh]
            return tuple((carry[hh][0] + jnp.sum(g[hh], axis=-1, keepdims=True), carry[hh][1] + dq[hh])
                         for hh in heads)

        init = tuple((jnp.zeros((t, 1), F32), jnp.zeros((t, HEAD_DIM), F32)) for _ in range(hg))
        carry = lax.fori_loop(0, qi, lambda kj, c: fwd(kj, c, False), init)
        carry = fwd(qi, carry, True)
        for hh in range(hg):
            dq_ref[:, hh * HEAD_DIM:(hh + 1) * HEAD_DIM] = carry[hh][1]

    once = pl.Buffered(buffer_count=1)
    tile_spec = pl.BlockSpec((t, gw), lambda h, i: (i, h))
    full = pl.BlockSpec((s, gw), lambda h, i: (0, h), pipeline_mode=once)
    w = nheads * HEAD_DIM
    return _call(
        body,
        grid=(nheads // hg, s // t),
        in_specs=[pl.BlockSpec((t, gw), lambda h, i: (i, colq // hg + h)),
                  pl.BlockSpec((s, gw), lambda h, i: (0, colk // hg + h), pipeline_mode=once),
                  pl.BlockSpec((s, gw), lambda h, i: (0, colv // hg + h), pipeline_mode=once),
                  tile_spec],
        out_specs=[tile_spec, full, full],
        out_shape=[jax.ShapeDtypeStruct((s, w), F32)] * 3,
        scratch_shapes=[pltpu.VMEM((hg, t, s), F32), pltpu.VMEM((hg, t, s), BF16)],
        semantics=("arbitrary", "arbitrary"),
        name="sb_bwd",
        args=(proj, proj, proj, do),
        carry=carry,
    )


def _mem_fwd(qn, kn, mkv, nheads):
    s = qn.shape[0]
    mtok = kn.shape[0]
    t = _tile(s, ATT_TILE)
    scale = HEAD_DIM ** -0.5

    def body(q_ref, k_ref, v_ref, o_ref):
        sc = _dot_nt(q_ref[...], k_ref[...]) * scale
        p = jnp.exp(sc - jnp.max(sc, axis=-1, keepdims=True))
        p = p / jnp.sum(p, axis=-1, keepdims=True)
        o_ref[...] = _dot(p.astype(BF16), v_ref[...]).astype(BF16)

    return pl.pallas_call(
        body,
        grid=(nheads, s // t),
        in_specs=[pl.BlockSpec((t, HEAD_DIM), lambda h, i: (i, h)),
                  pl.BlockSpec((mtok, HEAD_DIM), lambda h, i: (0, h)),
                  pl.BlockSpec((mtok, HEAD_DIM), lambda h, i: (0, nheads + h))],
        out_specs=pl.BlockSpec((t, HEAD_DIM), lambda h, i: (i, h)),
        out_shape=jax.ShapeDtypeStruct((s, nheads * HEAD_DIM), BF16),
        compiler_params=_params("parallel", "parallel"),
        name="mem_fwd",
    )(qn, kn, mkv)


def _mem_bwd(qn, kn, mkv, do, nheads):
    s = qn.shape[0]
    mtok = kn.shape[0]
    t = _tile(s, ATT_TILE)
    scale = HEAD_DIM ** -0.5

    def body(q_ref, k_ref, v_ref, do_ref, dq_ref, dk_ref, dv_ref):
        @pl.when(pl.program_id(1) == 0)
        def _():
            dk_ref[...] = jnp.zeros_like(dk_ref)
            dv_ref[...] = jnp.zeros_like(dv_ref)

        q = q_ref[...]
        k = k_ref[...]
        do_ = do_ref[...]
        sc = _dot_nt(q, k) * scale
        p = jnp.exp(sc - jnp.max(sc, axis=-1, keepdims=True))
        p = p / jnp.sum(p, axis=-1, keepdims=True)
        dp = _dot_nt(do_, v_ref[...])
        ds = (p * (dp - jnp.sum(p * dp, axis=-1, keepdims=True))).astype(BF16)
        dq_ref[...] = _dot(ds, k) * scale
        dk_ref[...] += _dot_tn(ds, q) * scale
        dv_ref[...] += _dot_tn(p.astype(BF16), do_)

    tile = pl.BlockSpec((t, HEAD_DIM), lambda h, i: (i, h))
    kspec = pl.BlockSpec((mtok, HEAD_DIM), lambda h, i: (0, h))
    w = nheads * HEAD_DIM
    return pl.pallas_call(
        body,
        grid=(nheads, s // t),
        in_specs=[tile, kspec, pl.BlockSpec((mtok, HEAD_DIM), lambda h, i: (0, nheads + h)), tile],
        out_specs=[tile, kspec, kspec],
        out_shape=[jax.ShapeDtypeStruct((s, w), F32), jax.ShapeDtypeStruct((mtok, w), F32),
                   jax.ShapeDtypeStruct((mtok, w), F32)],
        compiler_params=_params("arbitrary", "arbitrary"),
        name="mem_bwd",
    )(qn, kn, mkv, do)


def _merge_fwd(p0, p1, p2, gates, b_gate):
    s, d = p0.shape
    tm, tn = _tile(s, ROW_TILE), _tile(d, COL_TILE)
    nj = d // tn

    def body(p0_ref, p1_ref, p2_ref, ga_ref, gb_ref, gc_ref, b_ref, o_ref):
        acc = jnp.zeros((tm, tn), F32)
        for b, (p_ref, g_ref) in enumerate(((p0_ref, ga_ref), (p1_ref, gb_ref), (p2_ref, gc_ref))):
            gate = jax.nn.sigmoid(g_ref[...].astype(F32) + b_ref[b:b + 1, :])
            acc = acc + gate * p_ref[...]
        o_ref[...] = acc.astype(BF16)

    blk = pl.BlockSpec((tm, tn), lambda i, j: (i, j))
    return pl.pallas_call(
        body,
        grid=(s // tm, nj),
        in_specs=[blk] * 6 + [pl.BlockSpec((3, tn), lambda i, j: (0, j))],
        out_specs=blk,
        out_shape=jax.ShapeDtypeStruct((s, d), BF16),
        compiler_params=_params("parallel", "parallel"),
        name="merge_fwd",
    )(p0, p1, p2, *gates, b_gate)


def _merge_bwd(dmerged, p0, p1, p2, gates, b_gate):
    s, d = p0.shape
    tm, tn = _tile(s, ROW_TILE), _tile(d, COL_TILE)
    nj = d // tn

    def body(dm_ref, p0_ref, p1_ref, p2_ref, ga_ref, gb_ref, gc_ref, b_ref,
             d0_ref, d1_ref, d2_ref, dga_ref, dgb_ref, dgc_ref, db_ref):
        dm = dm_ref[...].astype(F32)
        parts = []
        for b, (p_ref, g_ref, dp_ref, dg_ref) in enumerate(((p0_ref, ga_ref, d0_ref, dga_ref),
                                                            (p1_ref, gb_ref, d1_ref, dgb_ref),
                                                            (p2_ref, gc_ref, d2_ref, dgc_ref))):
            gate = jax.nn.sigmoid(g_ref[...].astype(F32) + b_ref[b:b + 1, :])
            dp_ref[...] = (dm * gate).astype(BF16)
            dgate = dm * p_ref[...] * gate * (1.0 - gate)
            dg_ref[...] = dgate.astype(BF16)
            parts.append(jnp.sum(dgate, axis=0, keepdims=True))
        part = jnp.concatenate(parts, axis=0)

        @pl.when(pl.program_id(1) == 0)
        def _():
            db_ref[...] = part

        @pl.when(pl.program_id(1) > 0)
        def _():
            db_ref[...] += part

    blk = pl.BlockSpec((tm, tn), lambda j, i: (i, j))
    bias = pl.BlockSpec((3, tn), lambda j, i: (0, j))
    return pl.pallas_call(
        body,
        grid=(nj, s // tm),
        in_specs=[blk] * 7 + [bias],
        out_specs=[blk] * 6 + [bias],
        out_shape=[jax.ShapeDtypeStruct((s, d), BF16)] * 6 + [jax.ShapeDtypeStruct((3, d), F32)],
        compiler_params=_params("parallel", "arbitrary"),
        name="merge_bwd",
    )(dmerged, p0, p1, p2, *gates, b_gate)


def _shift_down(v, n):
    rows = lax.broadcasted_iota(jnp.int32, v.shape, 0)
    return jnp.where(rows >= n, pltpu.roll(v, n, 0), 0.0)


def _shift_up(v, n):
    s = v.shape[0]
    rows = lax.broadcasted_iota(jnp.int32, v.shape, 0)
    return jnp.where(rows < s - n, pltpu.roll(v, s - n, 0), 0.0)


def _conv(v, w_ref, b_ref):
    taps = w_ref.shape[0]
    out = v * w_ref[taps - 1:taps, :] + b_ref[...]
    for n in range(1, taps):
        out = out + _shift_down(v, n) * w_ref[taps - 1 - n:taps - n, :]
    return out


def _conv_act_fwd(up, conv_w, conv_b):
    s, f2 = up.shape
    f = f2 // 2
    tn = LANES
    nj = f // tn
    taps = conv_w.shape[0]

    def body(ug_ref, uv_ref, wg_ref, wv_ref, bg_ref, bv_ref, o_ref):
        cg = _conv(ug_ref[...].astype(F32), wg_ref, bg_ref)
        cv = _conv(uv_ref[...].astype(F32), wv_ref, bv_ref)
        o_ref[...] = (cg * jax.nn.sigmoid(cg) * cv).astype(BF16)

    return pl.pallas_call(
        body,
        grid=(nj,),
        in_specs=[pl.BlockSpec((s, tn), lambda j: (0, j)), pl.BlockSpec((s, tn), lambda j: (0, nj + j)),
                  pl.BlockSpec((taps, tn), lambda j: (0, j)), pl.BlockSpec((taps, tn), lambda j: (0, nj + j)),
                  pl.BlockSpec((1, tn), lambda j: (0, j)), pl.BlockSpec((1, tn), lambda j: (0, nj + j))],
        out_specs=pl.BlockSpec((s, tn), lambda j: (0, j)),
        out_shape=jax.ShapeDtypeStruct((s, f), BF16),
        compiler_params=_params("parallel"),
        name="conv_act_fwd",
    )(up, up, conv_w, conv_w, conv_b, conv_b)


def _conv_act_bwd(up, conv_w, conv_b, dact):
    s, f2 = up.shape
    f = f2 // 2
    tn = LANES
    nj = f // tn
    taps = conv_w.shape[0]

    def half(v, du, w_ref, dup_ref, dw_ref, db_ref):
        dup = du * w_ref[taps - 1:taps, :]
        rows = [None] * taps
        rows[taps - 1] = jnp.sum(du * v, axis=0, keepdims=True)
        for n in range(1, taps):
            later = _shift_up(du, n)
            dup = dup + later * w_ref[taps - 1 - n:taps - n, :]
            rows[taps - 1 - n] = jnp.sum(later * v, axis=0, keepdims=True)
        dup_ref[...] = dup.astype(BF16)
        dw_ref[...] = jnp.concatenate(rows, axis=0)
        db_ref[...] = jnp.sum(du, axis=0, keepdims=True)

    def body(ug_ref, uv_ref, wg_ref, wv_ref, bg_ref, bv_ref, da_ref,
             dug_ref, duv_ref, dwg_ref, dwv_ref, dbg_ref, dbv_ref):
        ug = ug_ref[...].astype(F32)
        uv = uv_ref[...].astype(F32)
        cg = _conv(ug, wg_ref, bg_ref)
        cv = _conv(uv, wv_ref, bv_ref)
        da = da_ref[...].astype(F32)
        sg = jax.nn.sigmoid(cg)
        dcv = da * cg * sg
        dcg = da * cv * (sg + cg * sg * (1.0 - sg))
        half(ug, dcg, wg_ref, dug_ref, dwg_ref, dbg_ref)
        half(uv, dcv, wv_ref, duv_ref, dwv_ref, dbv_ref)

    lo = lambda rows: pl.BlockSpec((rows, tn), lambda j: (0, j))
    hi = lambda rows: pl.BlockSpec((rows, tn), lambda j: (0, nj + j))
    return pl.pallas_call(
        body,
        grid=(nj,),
        in_specs=[lo(s), hi(s), lo(taps), hi(taps), lo(1), hi(1), lo(s)],
        out_specs=[lo(s), lo(s), lo(taps), lo(taps), lo(1), lo(1)],
        out_shape=[jax.ShapeDtypeStruct((s, f), BF16)] * 2 + [jax.ShapeDtypeStruct((taps, f), F32)] * 2
        + [jax.ShapeDtypeStruct((1, f), F32)] * 2,
        compiler_params=_params("parallel"),
        name="conv_act_bwd",
    )(up, up, conv_w, conv_w, conv_b, conv_b, dact)


def _row_tile(rows, row_bytes, budget):
    if rows * row_bytes <= budget or rows % 8:
        return rows
    best = 8
    for t in range(8, rows, 8):
        if rows % t == 0 and t * row_bytes <= budget:
            best = t
    return best


def _adamw(w, g, m, v, name):
    r, c = w.shape
    tr = _row_tile(r, c * 4, ADAM_BLOCK_BYTES)

    def body(w_ref, g_ref, m_ref, v_ref, d_ref, mo_ref, vo_ref):
        gg = g_ref[...]
        m_new = ADAM_B1 * m_ref[...] + (1.0 - ADAM_B1) * gg
        v_new = ADAM_B2 * v_ref[...] + (1.0 - ADAM_B2) * (gg * gg)
        m_hat = m_new / (1.0 - ADAM_B1 ** ADAM_STEP)
        v_hat = v_new / (1.0 - ADAM_B2 ** ADAM_STEP)
        d_ref[...] = -ADAM_LR * (m_hat / (jnp.sqrt(v_hat) + ADAM_EPS) + ADAM_WD * w_ref[...])
        mo_ref[...] = m_new
        vo_ref[...] = v_new

    blk = pl.BlockSpec((tr, c), lambda i: (i, 0))
    return pl.pallas_call(
        body,
        grid=(r // tr,),
        in_specs=[blk] * 4,
        out_specs=[blk] * 3,
        out_shape=[jax.ShapeDtypeStruct((r, c), F32)] * 3,
        compiler_params=_params("parallel"),
        name=name,
    )(w, g, m, v)


def _add_sibling(g, r1, core, name):
    _, _, h, c = g.shape
    th = _row_tile(h, c * 2, ADAM_BLOCK_BYTES)

    def body(core_ref, g_ref, r_ref, o_ref):
        o_ref[...] = (g_ref[...].astype(F32) + r_ref[...].astype(F32)).astype(BF16)

    return pl.pallas_call(
        body,
        grid_spec=pltpu.PrefetchScalarGridSpec(
            num_scalar_prefetch=1,
            grid=(N_CHIPS, h // th),
            in_specs=[pl.BlockSpec((None, None, th, c), lambda j, i, core_ref: (j, core_ref[0], i, 0)),
                      pl.BlockSpec((None, th, c), lambda j, i, core_ref: (j, i, 0))],
            out_specs=pl.BlockSpec((None, th, c), lambda j, i, core_ref: (j, i, 0)),
        ),
        out_shape=jax.ShapeDtypeStruct((N_CHIPS, h, c), BF16),
        compiler_params=_params("parallel", "parallel"),
        name=name,
    )(core, g, r1)


def _add_chips(hsum, r2, chip_core, name):
    _, h, c = hsum.shape
    th = _row_tile(h, c * 4, ADAM_BLOCK_BYTES)

    def body(sel_ref, own_ref, r_ref, o_ref):
        acc = own_ref[...].astype(F32)
        for j in range(N_CHIPS - 1):
            acc = acc + r_ref[j].astype(F32)
        o_ref[...] = acc

    return pl.pallas_call(
        body,
        grid_spec=pltpu.PrefetchScalarGridSpec(
            num_scalar_prefetch=1,
            grid=(h // th,),
            in_specs=[pl.BlockSpec((None, th, c), lambda i, sel_ref: (sel_ref[0], i, 0)),
                      pl.BlockSpec((N_CHIPS - 1, th, c), lambda i, sel_ref: (0, i, 0))],
            out_specs=pl.BlockSpec((None, th, c), lambda i, sel_ref: (sel_ref[1], i, 0)),
        ),
        out_shape=jax.ShapeDtypeStruct((2, h, c), F32),
        compiler_params=_params("parallel"),
        name=name,
    )(chip_core, hsum, r2)


def _sum_devices(parts):
    _, r, c = parts.shape

    def body(p_ref, o_ref):
        acc = p_ref[0]
        for j in range(1, N_DEV):
            acc = acc + p_ref[j]
        o_ref[...] = acc

    return pl.pallas_call(
        body,
        out_shape=jax.ShapeDtypeStruct((r, c), F32),
        compiler_params=pltpu.CompilerParams(vmem_limit_bytes=VMEM_LIMIT_BYTES),
        name="sum_devices",
    )(parts)


def _swap_halves(grads, name):
    n = len(grads)

    def body(*refs):
        ins, outs = refs[:n], refs[n:2 * n]
        send_sems, recv_sems = refs[2 * n:]
        x, y, c, _ = _place()
        copies = [_remote(ins[i].at[:, 1 - c], outs[i], send_sems.at[i], recv_sems.at[i], (x, y, 1 - c))
                  for i in range(n)]
        for cp in copies:
            cp.start()
        for cp in copies:
            cp.wait()

    return pl.pallas_call(
        body,
        in_specs=[ANY] * n,
        out_specs=[ANY] * n,
        out_shape=[jax.ShapeDtypeStruct((g.shape[0],) + g.shape[2:], g.dtype) for g in grads],
        scratch_shapes=[pltpu.SemaphoreType.DMA((n,)), pltpu.SemaphoreType.DMA((n,))],
        name=name,
    )(*grads)


def _join_halves(finals):
    n = len(finals)

    def body(*refs):
        outs = refs[n:2 * n]
        send_sems, recv_sems = refs[2 * n:]
        x, y, c, _ = _place()
        sends = [_remote(outs[i].at[c], outs[i].at[c], send_sems.at[i], recv_sems.at[i], (x, y, 1 - c))
                 for i in range(n)]
        for cp in sends:
            cp.start()
        for i in range(n):
            sends[i].wait_send()
            other = outs[i].at[1 - c]
            _remote(other, other, send_sems.at[i], recv_sems.at[i], (x, y, 1 - c)).wait_recv()

    return pl.pallas_call(
        body,
        in_specs=[ANY] * n,
        out_specs=[ANY] * n,
        out_shape=[jax.ShapeDtypeStruct(f.shape, f.dtype) for f in finals],
        input_output_aliases={i: i for i in range(n)},
        scratch_shapes=[pltpu.SemaphoreType.DMA((n,)), pltpu.SemaphoreType.DMA((n,))],
        name="join_halves",
    )(*finals)


def _gather_small(vec):
    k = N_DEV - 1

    def body(v_ref, o_ref, send_sems, recv_sems, local_sem):
        x, y, c, _ = _place()
        me = 4 * x + 2 * y + c
        local = pltpu.make_async_copy(v_ref, o_ref.at[me], local_sem)
        local.start()
        peers = [(x ^ (r >> 2 & 1), y ^ (r >> 1 & 1), c ^ (r & 1)) for r in range(1, N_DEV)]
        sends = [_remote(v_ref, o_ref.at[me], send_sems.at[j], recv_sems.at[j], p) for j, p in enumerate(peers)]
        for cp in sends:
            cp.start()
        for j, (px, py, pc) in enumerate(peers):
            sends[j].wait_send()
            blk = o_ref.at[4 * px + 2 * py + pc]
            _remote(blk, blk, send_sems.at[j], recv_sems.at[j], (px, py, pc)).wait_recv()
        local.wait()

    return pl.pallas_call(
        body,
        in_specs=[ANY],
        out_specs=ANY,
        out_shape=jax.ShapeDtypeStruct((N_DEV,) + vec.shape, vec.dtype),
        scratch_shapes=[pltpu.SemaphoreType.DMA((k,)), pltpu.SemaphoreType.DMA((k,)), pltpu.SemaphoreType.DMA(())],
        name="gather_small",
    )(vec)


W_IN_GATES = ("w_in_g0", "w_in_g1", "w_in_g2")
ROW_SHARDED = ("w_in_a", "w_in_f") + W_IN_GATES + ("w_mem_kv", "w_out", "w_down")
COL_SHARDED = ("w_br_fox", "w_br_sb", "w_br_mem", "w_up")
BIG = ROW_SHARDED + COL_SHARDED


def _whole(name, a):
    if name in ROW_SHARDED:
        return a.reshape(N_CHIPS * a.shape[1], a.shape[2])
    return a.transpose(1, 0, 2).reshape(a.shape[1], N_CHIPS * a.shape[2])


def _by_shard(name, grad):
    if name in ROW_SHARDED:
        a = grad.reshape(N_CHIPS, grad.shape[0] // N_CHIPS, grad.shape[1])
    else:
        a = grad.reshape(grad.shape[0], N_CHIPS, grad.shape[1] // N_CHIPS).transpose(1, 0, 2)
    return a.reshape(N_CHIPS, 2, a.shape[1] // 2, a.shape[2])


def _sibling_sums(names, g, core, tag):
    split = [_by_shard(name, g[name]) for name in names]
    theirs = _swap_halves(split, "swap_halves_" + tag)
    return [_add_sibling(a, r, core, "add_sibling_" + name) for name, a, r in zip(names, split, theirs)]


GATHER_FIRST = ("w_in_a", "w_in_f")
GATHER_EARLY = W_IN_GATES[:2] + ("w_mem_kv",)
GATHER_MIX = ("w_out", "w_br_fox", "w_br_sb", "w_br_mem")
REDUCE_FFN = ("w_down", "w_up")
REDUCE_MIX = ("w_out", "w_br_fox", "w_br_sb", "w_br_mem") + W_IN_GATES
REDUCE_IN = ("w_in_a", "w_in_f", "w_mem_kv")


def _local_step(x, mem, target, w, shard, core, chip_core):
    d = x.shape[1]
    nf = shard["w_br_fox"].shape[0] // HEAD_DIM
    nsb = shard["w_br_sb"].shape[0] // HEAD_DIM
    nm = shard["w_br_mem"].shape[0] // HEAD_DIM
    w = dict(w)

    def take(names, gathered):
        for name, a in zip(names, gathered):
            w[name] = _whole(name, a)

    take(GATHER_FIRST, _exchange(_Gather([shard[name] for name in GATHER_FIRST]), "gather_first"))
    fq, fk, fv = 0, nf, 2 * nf
    sq, sk, sv = 3 * nf, 3 * nf + nsb, 3 * nf + 2 * nsb
    mq = 3 * nf + 3 * nsb

    h, rstd1 = _rms_fwd(x, w["g_mix"], "rms_mix_fwd")
    proj, moved = _mm(h, w["w_in_a"], "nn", BF16, "proj_att", carry=_Gather([shard[name] for name in GATHER_EARLY]))
    take(GATHER_EARLY, moved)
    f_logit = _mm(h, w["w_in_f"], "nn", F32, "proj_forget")
    gate0, moved = _mm(h, w["w_in_g0"], "nn", BF16, "proj_gate0", carry=_Gather([shard[W_IN_GATES[2]]]))
    take(W_IN_GATES[2:], moved)
    gates = (gate0, _mm(h, w["w_in_g1"], "nn", BF16, "proj_gate1"), _mm(h, w["w_in_g2"], "nn", BF16, "proj_gate2"))
    c_sum = _forget_fwd(f_logit, w["b_forget"])
    c_t = c_sum[:, :nf].T
    c_col, c_row = c_t[:, :, None], c_t[:, None, :]
    qn = _headnorm_fwd(proj, fq, nf, w["g_q_fox"], "fox_qnorm_fwd")
    kn = _headnorm_fwd(proj, fk, nf, w["g_k_fox"], "fox_knorm_fwd")
    (o_fox, o_fox32, lse), moved = _fox_fwd(qn, kn, proj, fv, c_col, c_row, nf,
                                            carry=_Gather([shard[name] for name in GATHER_MIX]))
    take(GATHER_MIX, moved)
    (o_sb,), moved = _sb_fwd(proj, sq, sk, sv, nsb, carry=_Gather([shard["w_up"]]))
    take(("w_up",), moved)
    memn, rstd_m = _rms_fwd(mem, w["g_mem"], "rms_mem_fwd")
    mkv = _mm(memn, w["w_mem_kv"], "nn", BF16, "mem_kv")
    kmn = _headnorm_fwd(mkv, 0, nm, w["g_k_mem"], "mem_knorm_fwd")
    qmn = _headnorm_fwd(proj, mq, nm, w["g_q_mem"], "mem_qnorm_fwd")
    o_mem = _mem_fwd(qmn, kmn, mkv, nm)
    p0 = _mm(o_fox, w["w_br_fox"], "nn", F32, "branch_fox")
    p1 = _mm(o_sb, w["w_br_sb"], "nn", F32, "branch_sb")
    p2 = _mm(o_mem, w["w_br_mem"], "nn", F32, "branch_mem")
    merged = _merge_fwd(p0, p1, p2, gates, w["b_gate"])
    x1 = _mm(merged, w["w_out"], "nn", F32, "out_proj", residual=x)
    h2, rstd2 = _rms_fwd(x1, w["g_ffn"], "rms_ffn_fwd")
    up, moved = _mm(h2, w["w_up"], "nn", BF16, "ffn_up", carry=_Gather([shard["w_down"]]))
    take(("w_down",), moved)
    act =_conv_act_fwd(up, w["conv_w"], w["conv_b"])
    dy, dyb, lparts = _mm(act, w["w_down"], "nn", F32, "ffn_down_loss", residual=x1, loss_target=target)
    loss = (0.5 / d) * jnp.sum(lparts[::8, ::LANES])

    g = {}
    dact = _mm(dyb, w["w_down"], "nt", BF16, "ffn_down_dx")
    g["w_down"] = _mm(act, dyb, "tn", BF16, "ffn_down_dw")
    dug, duv, dwg, dwv, dbg, dbv = _conv_act_bwd(up, w["conv_w"], w["conv_b"], dact)
    dup = jnp.concatenate([dug, duv], axis=1)
    g["conv_w"] = jnp.concatenate([dwg, dwv], axis=1)
    g["conv_b"] = jnp.concatenate([dbg, dbv], axis=1)
    dh2 = _mm(dup, w["w_up"], "nt", BF16, "ffn_up_dx")
    g["w_up"] = _mm(h2, dup, "tn", BF16, "ffn_up_dw")
    sums_ffn = _sibling_sums(REDUCE_FFN, g, core, "ffn")
    dx1, dx1b, g["g_ffn"] = _rms_bwd(dh2, x1, rstd2, w["g_ffn"], dy, "rms_ffn_bwd")
    dmerged = _mm(dx1b, w["w_out"], "nt", BF16, "out_proj_dx")
    g["w_out"] = _mm(merged, dx1b, "tn", BF16, "out_proj_dw")
    dp0, dp1, dp2, dga, dgb, dgc, g["b_gate"] = _merge_bwd(dmerged, p0, p1, p2, gates, w["b_gate"])
    dgates = (dga, dgb, dgc)
    for name, dgate in zip(W_IN_GATES, dgates):
        g[name] = _mm(h, dgate, "tn", BF16, name + "_dw")
    do_fox = _mm(dp0, w["w_br_fox"], "nt", BF16, "branch_fox_dx")
    do_sb = _mm(dp1, w["w_br_sb"], "nt", BF16, "branch_sb_dx")
    do_mem = _mm(dp2, w["w_br_mem"], "nt", BF16, "branch_mem_dx")
    g["w_br_fox"] = _mm(o_fox, dp0, "tn", BF16, "branch_fox_dw")
    g["w_br_sb"] = _mm(o_sb, dp1, "tn", BF16, "branch_sb_dw")
    g["w_br_mem"] = _mm(o_mem, dp2, "tn", BF16, "branch_mem_dw")
    sums_mix = _sibling_sums(REDUCE_MIX, g, core, "mix")

    (dqn, dkn, dfv, drs, dcs), others_up = _fox_bwd(qn, kn, proj, fv, c_col, c_row, o_fox32, do_fox, lse, nf,
                                                    carry=_Scatter(sums_ffn[1:]))
    dfq, g["g_q_fox"] = _headnorm_bwd(dqn, proj, fq, nf, w["g_q_fox"], "fox_qnorm_bwd")
    dfk, g["g_k_fox"] = _headnorm_bwd(dkn, proj, fk, nf, w["g_k_fox"], "fox_knorm_bwd")
    dc = jnp.pad((drs[:, :, 0] - dcs[:, 0, :]).T, ((0, 0), (0, LANES - nf)))
    df, g["b_forget"] = _forget_bwd(dc, f_logit, w["b_forget"])
    (dsq, dsk, dsv), others_mix = _sb_bwd(proj, sq, sk, sv, do_sb, nsb, carry=_Scatter(sums_ffn[:1] + sums_mix))
    others_ffn = others_mix[:1] + others_up
    others_mix = others_mix[1:]
    dqmn, dkmn, dvm = _mem_bwd(qmn, kmn, mkv, do_mem, nm)
    dmq, g["g_q_mem"] = _headnorm_bwd(dqmn, proj, mq, nm, w["g_q_mem"], "mem_qnorm_bwd")
    dkm, g["g_k_mem"] = _headnorm_bwd(dkmn, mkv, 0, nm, w["g_k_mem"], "mem_knorm_bwd")
    dmkv = jnp.concatenate([dkm, dvm.astype(BF16)], axis=1)
    g["w_mem_kv"] = _mm(memn, dmkv, "tn", BF16, "mem_kv_dw")
    dmemn = _mm(dmkv, w["w_mem_kv"], "nt", BF16, "mem_kv_dx")
    _, _, g["g_mem"] = _rms_bwd(dmemn, mem, rstd_m, w["g_mem"], None, "rms_mem_bwd")

    dproj = jnp.concatenate([dfq, dfk, dfv.astype(BF16), dsq.astype(BF16), dsk.astype(BF16), dsv.astype(BF16), dmq],
                            axis=1)
    dfb = df.astype(BF16)
    g["w_in_a"] = _mm(h, dproj, "tn", BF16, "proj_att_dw")
    g["w_in_f"] = _mm(h, dfb, "tn", BF16, "proj_forget_dw")
    sums_in = _sibling_sums(REDUCE_IN, g, core, "in")
    dh, others_in = _mm(dproj, w["w_in_a"], "nt", F32, "proj_att_dx", carry=_Scatter(sums_in))
    dh = _mm(dfb, w["w_in_f"], "nt", F32, "proj_forget_dx", residual=dh)
    for name, dgate in zip(W_IN_GATES, dgates):
        dh = _mm(dgate, w[name], "nt", F32, name + "_dx", residual=dh)
    grad_x, _, g["g_mix"] = _rms_bwd(dh, x, rstd1, w["g_mix"], dx1, "rms_mix_bwd")

    names = REDUCE_FFN + REDUCE_MIX + REDUCE_IN
    finals = [_add_chips(own, theirs, chip_core, "add_chips_" + name)
              for name, own, theirs in zip(names, sums_ffn + sums_mix + sums_in, others_ffn + others_mix + others_in)]
    summed = {name: a.reshape(2 * a.shape[1], a.shape[2]) for name, a in zip(names, _join_halves(finals))}
    return loss, grad_x, g, summed


SMALL = ("g_mix", "b_forget", "g_q_fox", "g_k_fox", "g_mem", "g_q_mem", "g_k_mem", "b_gate", "g_ffn", "conv_w",
         "conv_b")
SMALL_SHARDED = ("b_gate", "conv_w")
PACK_ROWS = 8


def _pack(arrs):
    flat = jnp.concatenate([a.reshape(-1) for a in arrs])
    unit = PACK_ROWS * LANES
    flat = jnp.pad(flat, (0, -flat.shape[0] % unit))
    return flat.reshape(-1, LANES)


def _unpack(packed, shapes):
    flat = packed.reshape(-1)
    out, at = [], 0
    for s in shapes:
        n = 1
        for dim in s:
            n *= dim
        out.append(flat[at:at + n].reshape(s))
        at += n
    return out


def kernel(x, mem, g_mix, w_in, b_forget, g_q_fox, g_k_fox, g_mem, w_mem_kv, g_q_mem, g_k_mem, w_br_fox, w_br_sb, w_br_mem, b_gate, w_out, g_ffn, w_up, conv_w, conv_b, w_down, loss_target, m_g_mix, m_w_in, m_b_forget, m_g_q_fox, m_g_k_fox, m_g_mem, m_w_mem_kv, m_g_q_mem, m_g_k_mem, m_w_br_fox, m_w_br_sb, m_w_br_mem, m_b_gate, m_w_out, m_g_ffn, m_w_up, m_conv_w, m_conv_b, m_w_down, v_g_mix, v_w_in, v_b_forget, v_g_q_fox, v_g_k_fox, v_g_mem, v_w_mem_kv, v_g_q_mem, v_g_k_mem, v_w_br_fox, v_w_br_sb, v_w_br_mem, v_b_gate, v_w_out, v_g_ffn, v_w_up, v_conv_w, v_conv_b, v_w_down):
    given = dict(g_mix=g_mix, w_in=w_in, b_forget=b_forget, g_q_fox=g_q_fox, g_k_fox=g_k_fox, g_mem=g_mem,
                 w_mem_kv=w_mem_kv, g_q_mem=g_q_mem, g_k_mem=g_k_mem, w_br_fox=w_br_fox, w_br_sb=w_br_sb,
                 w_br_mem=w_br_mem, b_gate=b_gate, w_out=w_out, g_ffn=g_ffn, w_up=w_up, conv_w=conv_w, conv_b=conv_b,
                 w_down=w_down)
    m_in = dict(g_mix=m_g_mix, w_in=m_w_in, b_forget=m_b_forget, g_q_fox=m_g_q_fox, g_k_fox=m_g_k_fox, g_mem=m_g_mem,
                w_mem_kv=m_w_mem_kv, g_q_mem=m_g_q_mem, g_k_mem=m_g_k_mem, w_br_fox=m_w_br_fox, w_br_sb=m_w_br_sb,
                w_br_mem=m_w_br_mem, b_gate=m_b_gate, w_out=m_w_out, g_ffn=m_g_ffn, w_up=m_w_up, conv_w=m_conv_w,
                conv_b=m_conv_b, w_down=m_w_down)
    v_in = dict(g_mix=v_g_mix, w_in=v_w_in, b_forget=v_b_forget, g_q_fox=v_g_q_fox, g_k_fox=v_g_k_fox, g_mem=v_g_mem,
                w_mem_kv=v_w_mem_kv, g_q_mem=v_g_q_mem, g_k_mem=v_g_k_mem, w_br_fox=v_w_br_fox, w_br_sb=v_w_br_sb,
                w_br_mem=v_w_br_mem, b_gate=v_b_gate, w_out=v_w_out, g_ffn=v_g_ffn, w_up=v_w_up, conv_w=v_conv_w,
                conv_b=v_conv_b, w_down=v_w_down)
    layered = {k: a.ndim == 3 for k, a in given.items()}
    drop = lambda a: a[0] if a.ndim == 3 else a
    given = {k: drop(a) for k, a in given.items()}
    m_in = {k: drop(a) for k, a in m_in.items()}
    v_in = {k: drop(a) for k, a in v_in.items()}

    xi, yi, ci = lax.axis_index("x"), lax.axis_index("y"), lax.axis_index("c")
    chip = (2 * xi + yi).astype(jnp.int32)
    core_arr = ci.astype(jnp.int32).reshape(1)
    chip_core = jnp.stack([chip, ci.astype(jnp.int32)])

    nf = given["b_forget"].shape[1]
    cut = 3 * given["w_br_fox"].shape[0]

    d_model = given["w_out"].shape[1]
    gate0 = given["w_in"].shape[1] - len(W_IN_GATES) * d_model
    shard = {
        "w_in_a": jnp.concatenate([given["w_in"][:, :cut], given["w_in"][:, cut + nf:gate0]], axis=1).astype(BF16),
        "w_in_f": jnp.pad(given["w_in"][:, cut:cut + nf], ((0, 0), (0, LANES - nf))).astype(BF16),
    }
    for b, name in enumerate(W_IN_GATES):
        shard[name] = given["w_in"][:, gate0 + b * d_model:gate0 + (b + 1) * d_model].astype(BF16)
    for name in BIG:
        if name not in shard:
            shard[name] = given[name].astype(BF16)
    w = {}
    small_shapes = [given[name].shape for name in SMALL_SHARDED]
    small_parts = _gather_small(_pack([given[name] for name in SMALL_SHARDED]))[0::2]
    per_chip = [_unpack(small_parts[j], small_shapes) for j in range(N_CHIPS)]
    for k, name in enumerate(SMALL_SHARDED):
        w[name] = jnp.concatenate([per_chip[j][k] for j in range(N_CHIPS)], axis=1)
    for name in SMALL:
        if name not in SMALL_SHARDED:
            w[name] = given[name]
    w["b_forget"] = jnp.pad(given["b_forget"], ((0, 0), (0, LANES - nf)))

    loss, grad_x, g, summed = _local_step(x[0], mem[0], loss_target[0], w, shard, core_arr, chip_core)
    loss = lax.psum(loss, ("x", "y", "c"))
    grads = {name: summed[name] for name in BIG if name in given}
    grads["w_in"] = jnp.concatenate([summed["w_in_a"][:, :cut], summed["w_in_f"][:, :nf], summed["w_in_a"][:, cut:]]
                                    + [summed[name] for name in W_IN_GATES], axis=1)

    g["b_forget"] = g["b_forget"][:, :nf]
    small_full_shapes = [g[name].shape for name in SMALL]
    small_sum = _unpack(_sum_devices(_gather_small(_pack([g[name] for name in SMALL]))), small_full_shapes)
    for name, a in zip(SMALL, small_sum):
        if name in SMALL_SHARDED:
            width = given[name].shape[1]
            a = lax.dynamic_slice_in_dim(a, chip * width, width, axis=1)
        grads[name] = a

    delta, new_m, new_v = {}, {}, {}
    for name in WEIGHTS:
        if name not in SMALL:
            delta[name], new_m[name], new_v[name] = _adamw(given[name], grads[name], m_in[name], v_in[name],
                                                           "adamw_" + name)
    shapes = [given[name].shape for name in SMALL]
    packed = [_pack([src[name] for name in SMALL]) for src in (given, grads, m_in, v_in)]
    for dst, res in zip((delta, new_m, new_v), _adamw(*packed, "adamw_small")):
        for name, a in zip(SMALL, _unpack(res, shapes)):
            dst[name] = a

    out = [loss, grad_x[None]]
    for src in (grads, delta, new_m, new_v):
        out.extend(src[name][None] if layered[name] else src[name] for name in WEIGHTS)
    return tuple(out)
```

```python
import functools

import jax
import jax.numpy as jnp
from jax import lax
from jax.experimental import pallas as pl
from jax.experimental.pallas import tpu as pltpu

F32 = jnp.float32
BF16 = jnp.bfloat16

HEAD_DIM = 128
EPS = 1e-6
NEG_BIG = -1e30

ADAM_LR = 0.001
ADAM_B1 = 0.9
ADAM_B2 = 0.999
ADAM_EPS = 1e-08
ADAM_WD = 0.01
ADAM_STEP = 10

LANES = 128
BF16_SUBLANES = 16
VMEM_LIMIT_BYTES = 56 * 1024 * 1024
MM_TILE = 1024
MM_TILE_K = {"nn": 2048, "nt": 2048, "tn": 4096}
ATT_TILE = 256
ROW_TILE = 256
HEADNORM_ROWS = 512
COL_TILE = 512
ADAM_BLOCK_BYTES = 2 << 20

N_CHIPS = 4
N_DEV = 8
MESH = pl.DeviceIdType.MESH

IN_NAMES = ['x', 'mem', 'g_mix', 'w_in', 'b_forget', 'g_q_fox', 'g_k_fox', 'g_mem', 'w_mem_kv', 'g_q_mem', 'g_k_mem',
            'w_br_fox', 'w_br_sb', 'w_br_mem', 'b_gate', 'w_out', 'g_ffn', 'w_up', 'conv_w', 'conv_b', 'w_down']
WEIGHTS = IN_NAMES[2:]


def _tile(n, target):
    if n <= target:
        return n
    for t in range(target - target % LANES, LANES - 1, -LANES):
        if n % t == 0:
            return t
    return n


def _params(*sem):
    return pltpu.CompilerParams(dimension_semantics=sem, vmem_limit_bytes=VMEM_LIMIT_BYTES)


def _log_sigmoid(z):
    return jnp.minimum(z, 0.0) - jnp.log(1.0 + jnp.exp(-jnp.abs(z)))


def _split2(v):
    hi = v.astype(BF16)
    lo = (v - hi.astype(F32)).astype(BF16)
    return hi, lo


def _split3(v):
    hi = v.astype(BF16)
    r = v - hi.astype(F32)
    mid = r.astype(BF16)
    lo = (r - mid.astype(F32)).astype(BF16)
    return hi, mid, lo


def _dot(a, b):
    return lax.dot_general(a, b, (((1,), (0,)), ((), ())), preferred_element_type=F32)


def _dot_nt(a, b):
    return lax.dot_general(a, b, (((1,), (1,)), ((), ())), preferred_element_type=F32)


def _dot_tn(a, b):
    return lax.dot_general(a, b, (((0,), (0,)), ((), ())), preferred_element_type=F32)


ANY = pl.BlockSpec(memory_space=pl.ANY)


def _place():
    x, y, c = lax.axis_index("x"), lax.axis_index("y"), lax.axis_index("c")
    others = [(1 - x, y), (x, 1 - y), (1 - x, 1 - y)]
    return x, y, c, others


def _remote(src, dst, send_sem, recv_sem, to):
    return pltpu.make_async_remote_copy(src_ref=src, dst_ref=dst, send_sem=send_sem, recv_sem=recv_sem,
                                        device_id=to, device_id_type=MESH)


class _Gather:
    PER_SHARD = 7

    def __init__(self, shards):
        self.inputs = list(shards)
        n = len(shards) * self.PER_SHARD
        self.out_shapes = [jax.ShapeDtypeStruct((N_CHIPS,) + s.shape, s.dtype) for s in shards]
        self.scratch = [pltpu.SemaphoreType.DMA((n,)), pltpu.SemaphoreType.DMA((n,))]

    def _first(self, ins, outs, sems):
        send_sems, recv_sems = sems
        x, y, c, others = _place()
        me = 2 * x + y
        k = self.PER_SHARD
        copies = []
        for i in range(len(ins)):
            h = ins[i].shape[0] // 2
            mine = pl.ds(pl.multiple_of(c * h, BF16_SUBLANES), h)
            for j, (ox, oy) in enumerate(others):
                copies.append(_remote(ins[i].at[mine], outs[i].at[me, mine], send_sems.at[k * i + j],
                                      recv_sems.at[k * i + j], (ox, oy, c)))
            copies.append(_remote(ins[i], outs[i].at[me], send_sems.at[k * i + 6], recv_sems.at[k * i + 6],
                                  (x, y, 1 - c)))
        return copies

    def start(self, ins, outs, sems):
        for cp in self._first(ins, outs, sems):
            cp.start()

    def finish(self, ins, outs, sems):
        send_sems, recv_sems = sems
        x, y, c, others = _place()
        me = 2 * x + y
        sibling = (x, y, 1 - c)
        k = self.PER_SHARD
        passed = []
        for i in range(len(ins)):
            h = ins[i].shape[0] // 2
            mine = pl.ds(pl.multiple_of(c * h, BF16_SUBLANES), h)
            for j, (ox, oy) in enumerate(others):
                blk = outs[i].at[2 * ox + oy, mine]
                _remote(blk, blk, send_sems.at[k * i + j], recv_sems.at[k * i + j], (ox, oy, c)).wait_recv()
                cp = _remote(blk, blk, send_sems.at[k * i + 3 + j], recv_sems.at[k * i + 3 + j], sibling)
                cp.start()
                passed.append(cp)
        for i in range(len(ins)):
            h = ins[i].shape[0] // 2
            theirs = pl.ds(pl.multiple_of((1 - c) * h, BF16_SUBLANES), h)
            for j, (ox, oy) in enumerate(others):
                blk = outs[i].at[2 * ox + oy, theirs]
                _remote(blk, blk, send_sems.at[k * i + 3 + j], recv_sems.at[k * i + 3 + j], sibling).wait_recv()
            own = outs[i].at[me]
            _remote(own, own, send_sems.at[k * i + 6], recv_sems.at[k * i + 6], sibling).wait_recv()
        for cp in self._first(ins, outs, sems) + passed:
            cp.wait_send()


class _Scatter:
    def __init__(self, sums):
        self.inputs = list(sums)
        k = N_CHIPS - 1
        self.out_shapes = [jax.ShapeDtypeStruct((k,) + g.shape[1:], g.dtype) for g in sums]
        self.scratch = [pltpu.SemaphoreType.DMA((k * len(sums),)), pltpu.SemaphoreType.DMA((k * len(sums),))]

    def _copies(self, ins, outs, sems):
        send_sems, recv_sems = sems
        _, _, c, others = _place()
        k = N_CHIPS - 1
        return [_remote(ins[i].at[2 * ox + oy], outs[i].at[j], send_sems.at[k * i + j], recv_sems.at[k * i + j],
                        (ox, oy, c))
                for i in range(len(ins)) for j, (ox, oy) in enumerate(others)]

    def start(self, ins, outs, sems):
        for cp in self._copies(ins, outs, sems):
            cp.start()

    def finish(self, ins, outs, sems):
        for cp in self._copies(ins, outs, sems):
            cp.wait()


class _Swap:
    def __init__(self, grads):
        self.inputs = list(grads)
        n = len(grads)
        self.out_shapes = [jax.ShapeDtypeStruct((g.shape[0],) + g.shape[2:], g.dtype) for g in grads]
        self.scratch = [pltpu.SemaphoreType.DMA((n,)), pltpu.SemaphoreType.DMA((n,))]

    def _copies(self, ins, outs, sems):
        send_sems, recv_sems = sems
        x, y, c, _ = _place()
        return [_remote(ins[i].at[:, 1 - c], outs[i], send_sems.at[i], recv_sems.at[i], (x, y, 1 - c))
                for i in range(len(ins))]

    def start(self, ins, outs, sems):
        for cp in self._copies(ins, outs, sems):
            cp.start()

    def finish(self, ins, outs, sems):
        for cp in self._copies(ins, outs, sems):
            cp.wait()


class _Both:
    def __init__(self, first, second):
        self.parts = (first, second)
        self.inputs = first.inputs + second.inputs
        self.out_shapes = first.out_shapes + second.out_shapes
        self.scratch = first.scratch + second.scratch

    def _each(self, ins, outs, sems):
        first = self.parts[0]
        a, b, c = len(first.inputs), len(first.out_shapes), len(first.scratch)
        return ((first, ins[:a], outs[:b], sems[:c]), (self.parts[1], ins[a:], outs[b:], sems[c:]))

    def start(self, ins, outs, sems):
        for part, i, o, s in self._each(ins, outs, sems):
            part.start(i, o, s)

    def finish(self, ins, outs, sems):
        for part, i, o, s in self._each(ins, outs, sems):
            part.finish(i, o, s)


def _exchange(carry, name):
    n = len(carry.inputs)

    def body(*refs):
        ins, outs, sems = refs[:n], refs[n:2 * n], refs[2 * n:]
        carry.start(ins, outs, sems)
        carry.finish(ins, outs, sems)

    return pl.pallas_call(
        body,
        in_specs=[ANY] * n,
        out_specs=[ANY] * n,
        out_shape=carry.out_shapes,
        scratch_shapes=carry.scratch,
        name=name,
    )(*carry.inputs)


def _call(body, *, grid, in_specs, out_specs, out_shape, scratch_shapes, semantics, name, args, carry=None):
    n_in, n_out, n_scr = len(in_specs), len(out_specs), len(scratch_shapes)
    if carry is None:
        res = pl.pallas_call(body, grid=grid, in_specs=in_specs, out_specs=out_specs, out_shape=out_shape,
                             scratch_shapes=scratch_shapes, compiler_params=_params(*semantics), name=name)(*args)
        return list(res), []
    nci, nco = len(carry.inputs), len(carry.out_shapes)
    a, b = n_in, n_in + nci
    c, d = b + n_out, b + n_out + nco
    e = d + n_scr

    def carried(*refs):
        ids = [pl.program_id(k) for k in range(len(grid))]
        first = functools.reduce(jnp.logical_and, [i == 0 for i in ids])
        last = functools.reduce(jnp.logical_and, [i == n - 1 for i, n in zip(ids, grid)])

        @pl.when(first)
        def _():
            carry.start(refs[a:b], refs[c:d], refs[e:])

        body(*refs[:a], *refs[b:c], *refs[d:e])

        @pl.when(last)
        def _():
            carry.finish(refs[a:b], refs[c:d], refs[e:])

    res = pl.pallas_call(
        carried,
        grid=grid,
        in_specs=list(in_specs) + [ANY] * nci,
        out_specs=list(out_specs) + [ANY] * nco,
        out_shape=list(out_shape) + carry.out_shapes,
        scratch_shapes=list(scratch_shapes) + carry.scratch,
        compiler_params=_params(*(["arbitrary"] * len(grid))),
        name=name,
    )(*args, *carry.inputs)
    return list(res[:n_out]), list(res[n_out:])


def _mm(a, b, mode, out_dtype, name, residual=None, carry=None, loss_target=None):
    if mode == "nn":
        (m, k), (k2, n) = a.shape, b.shape
    elif mode == "nt":
        (m, k), (n, k2) = a.shape, b.shape
    else:
        (k, m), (k2, n) = a.shape, b.shape
    assert k == k2, (a.shape, b.shape, mode)
    has_res = residual is not None
    has_loss = loss_target is not None
    n_in = 2 + has_res + has_loss
    tm, tn, tk = _tile(m, MM_TILE), _tile(n, MM_TILE), _tile(k, MM_TILE_K[mode])
    nk = k // tk
    dot = {"nn": _dot, "nt": _dot_nt, "tn": _dot_tn}[mode]

    def body(*refs):
        a_ref, b_ref = refs[:2]
        r_ref = refs[2] if has_res else None
        t_ref = refs[n_in - 1] if has_loss else None
        o_ref = refs[n_in]

        def finish(acc):
            if has_res:
                acc = acc + r_ref[...]
            if has_loss:
                err = acc - t_ref[...]
                dy = err * (1.0 / n)
                o_ref[...] = dy
                refs[n_in + 1][...] = dy.astype(BF16)
                tot = jnp.sum(jnp.sum(err * err, axis=-1, keepdims=True), axis=0, keepdims=True)
                refs[n_in + 2][...] = jnp.broadcast_to(tot, (8, LANES))
            else:
                o_ref[...] = acc.astype(o_ref.dtype)

        part = dot(a_ref[...], b_ref[...])
        if nk == 1:
            finish(part)
        else:
            acc_ref = refs[-1]
            kk = pl.program_id(2)

            @pl.when(kk == 0)
            def _():
                acc_ref[...] = part

            @pl.when(kk > 0)
            def _():
                acc_ref[...] += part

            @pl.when(kk == nk - 1)
            def _():
                finish(acc_ref[...])

    if mode == "tn":
        a_spec = pl.BlockSpec((tk, tm), lambda j, i, kk: (kk, i))
    else:
        a_spec = pl.BlockSpec((tm, tk), lambda j, i, kk: (i, kk))
    if mode == "nt":
        b_spec = pl.BlockSpec((tn, tk), lambda j, i, kk: (j, kk))
    else:
        b_spec = pl.BlockSpec((tk, tn), lambda j, i, kk: (kk, j))
    o_spec = pl.BlockSpec((tm, tn), lambda j, i, kk: (i, j))
    in_specs = [a_spec, b_spec] + [o_spec] * (has_res + has_loss)
    args = (a, b) + ((residual,) if has_res else ()) + ((loss_target,) if has_loss else ())
    out_specs, out_shape = [o_spec], [jax.ShapeDtypeStruct((m, n), out_dtype)]
    if has_loss:
        out_specs += [o_spec, pl.BlockSpec((8, LANES), lambda j, i, kk: (i, j))]
        out_shape += [jax.ShapeDtypeStruct((m, n), BF16), jax.ShapeDtypeStruct((m // tm * 8, n // tn * LANES), F32)]
    outs, moved = _call(
        body,
        grid=(n // tn, m // tm, nk),
        in_specs=in_specs,
        out_specs=out_specs,
        out_shape=out_shape,
        scratch_shapes=[pltpu.VMEM((tm, tn), F32)] if nk > 1 else [],
        semantics=("parallel", "parallel", "arbitrary"),
        name=name,
        args=args,
        carry=carry,
    )
    out = outs if has_loss else outs[0]
    return out if carry is None else (out, moved)


def _rms_fwd(x, g, name):
    s, d = x.shape
    tm = _tile(s, ROW_TILE)

    def body(x_ref, g_ref, h_ref, r_ref):
        xf = x_ref[...]
        r = lax.rsqrt(jnp.mean(xf * xf, axis=-1, keepdims=True) + EPS)
        h_ref[...] = ((xf * r) * g_ref[...]).astype(BF16)
        r_ref[...] = r

    return pl.pallas_call(
        body,
        grid=(s // tm,),
        in_specs=[pl.BlockSpec((tm, d), lambda i: (i, 0)), pl.BlockSpec((1, d), lambda i: (0, 0))],
        out_specs=[pl.BlockSpec((tm, d), lambda i: (i, 0)), pl.BlockSpec((tm, 1), lambda i: (i, 0))],
        out_shape=[jax.ShapeDtypeStruct((s, d), BF16), jax.ShapeDtypeStruct((s, 1), F32)],
        compiler_params=_params("parallel"),
        name=name,
    )(x, g)


def _rms_bwd(dh, x, rstd, g, res, name):
    s, d = x.shape
    tm = _tile(s, ROW_TILE)
    has_res = res is not None

    def body(*refs):
        if has_res:
            dh_ref, x_ref, r_ref, g_ref, res_ref, dx_ref, dxb_ref, dg_ref = refs
        else:
            dh_ref, x_ref, r_ref, g_ref, dx_ref, dxb_ref, dg_ref = refs
        dhf = dh_ref[...].astype(F32)
        xhat = x_ref[...] * r_ref[...]
        dy = dhf * g_ref[...]
        dx = r_ref[...] * (dy - xhat * jnp.mean(dy * xhat, axis=-1, keepdims=True))
        if has_res:
            dx = dx + res_ref[...]
        dx_ref[...] = dx
        dxb_ref[...] = dx.astype(BF16)
        part = jnp.sum(dhf * xhat, axis=0, keepdims=True)

        @pl.when(pl.program_id(0) == 0)
        def _():
            dg_ref[...] = part

        @pl.when(pl.program_id(0) > 0)
        def _():
            dg_ref[...] += part

    row = pl.BlockSpec((tm, d), lambda i: (i, 0))
    vec = pl.BlockSpec((1, d), lambda i: (0, 0))
    in_specs = [row, row, pl.BlockSpec((tm, 1), lambda i: (i, 0)), vec] + ([row] if has_res else [])
    args = (dh, x, rstd, g) + ((res,) if has_res else ())
    return pl.pallas_call(
        body,
        grid=(s // tm,),
        in_specs=in_specs,
        out_specs=[row, row, vec],
        out_shape=[jax.ShapeDtypeStruct((s, d), F32), jax.ShapeDtypeStruct((s, d), BF16),
                   jax.ShapeDtypeStruct((1, d), F32)],
        compiler_params=_params("arbitrary"),
        name=name,
    )(*args)


def _headnorm_fwd(src, col0, nheads, g, name):
    s = src.shape[0]
    tm = _tile(s, HEADNORM_ROWS)
    w = nheads * HEAD_DIM
    assert col0 % nheads == 0

    def body(x_ref, g_ref, o_ref):
        for hh in range(nheads):
            xf = _head(x_ref, hh).astype(F32)
            r = lax.rsqrt(jnp.mean(xf * xf, axis=-1, keepdims=True) + EPS)
            o_ref[:, hh * HEAD_DIM:(hh + 1) * HEAD_DIM] = ((xf * r) * g_ref[...]).astype(BF16)

    return pl.pallas_call(
        body,
        grid=(s // tm,),
        in_specs=[pl.BlockSpec((tm, w), lambda i: (i, col0 // nheads)),
                  pl.BlockSpec((1, HEAD_DIM), lambda i: (0, 0))],
        out_specs=pl.BlockSpec((tm, w), lambda i: (i, 0)),
        out_shape=jax.ShapeDtypeStruct((s, w), BF16),
        compiler_params=_params("parallel"),
        name=name,
    )(src, g)


def _headnorm_bwd(dxn, src, col0, nheads, g, name):
    s = src.shape[0]
    tm = _tile(s, HEADNORM_ROWS)
    w = nheads * HEAD_DIM
    assert col0 % nheads == 0

    def body(d_ref, x_ref, g_ref, dx_ref, dg_ref):
        part = jnp.zeros((1, HEAD_DIM), F32)
        for hh in range(nheads):
            xf = _head(x_ref, hh).astype(F32)
            r = lax.rsqrt(jnp.mean(xf * xf, axis=-1, keepdims=True) + EPS)
            xhat = xf * r
            dn = _head(d_ref, hh).astype(F32)
            dy = dn * g_ref[...]
            dx = r * (dy - xhat * jnp.mean(dy * xhat, axis=-1, keepdims=True))
            dx_ref[:, hh * HEAD_DIM:(hh + 1) * HEAD_DIM] = dx.astype(BF16)
            part = part + jnp.sum(dn * xhat, axis=0, keepdims=True)

        @pl.when(pl.program_id(0) == 0)
        def _():
            dg_ref[...] = part

        @pl.when(pl.program_id(0) > 0)
        def _():
            dg_ref[...] += part

    return pl.pallas_call(
        body,
        grid=(s // tm,),
        in_specs=[pl.BlockSpec((tm, w), lambda i: (i, 0)),
                  pl.BlockSpec((tm, w), lambda i: (i, col0 // nheads)),
                  pl.BlockSpec((1, HEAD_DIM), lambda i: (0, 0))],
        out_specs=[pl.BlockSpec((tm, w), lambda i: (i, 0)),
                   pl.BlockSpec((1, HEAD_DIM), lambda i: (0, 0))],
        out_shape=[jax.ShapeDtypeStruct((s, w), BF16), jax.ShapeDtypeStruct((1, HEAD_DIM), F32)],
        compiler_params=_params("arbitrary"),
        name=name,
    )(dxn, src, g)


def _tri(t, lower_inclusive):
    r = lax.broadcasted_iota(jnp.int32, (t, t), 0)
    c = lax.broadcasted_iota(jnp.int32, (t, t), 1)
    keep = (c <= r) if lower_inclusive else (c >= r)
    return jnp.where(keep, 1.0, 0.0).astype(BF16)


def _forget_fwd(f_logit, b_pad):
    s = f_logit.shape[0]
    t = _tile(s, ATT_TILE)

    def body(f_ref, b_ref, c_ref, carry):
        @pl.when(pl.program_id(0) == 0)
        def _():
            carry[...] = jnp.zeros_like(carry)

        lf = _log_sigmoid(f_ref[...] + b_ref[...])
        tri = _tri(t, True)
        acc = carry[...]
        for part in _split3(lf):
            acc = acc + _dot(tri, part)
        c_ref[...] = acc
        carry[...] += jnp.sum(lf, axis=0, keepdims=True)

    return pl.pallas_call(
        body,
        grid=(s // t,),
        in_specs=[pl.BlockSpec((t, LANES), lambda i: (i, 0)), pl.BlockSpec((1, LANES), lambda i: (0, 0))],
        out_specs=pl.BlockSpec((t, LANES), lambda i: (i, 0)),
        out_shape=jax.ShapeDtypeStruct((s, LANES), F32),
        scratch_shapes=[pltpu.VMEM((1, LANES), F32)],
        compiler_params=_params("arbitrary"),
        name="forget_fwd",
    )(f_logit, b_pad)


def _forget_bwd(dc, f_logit, b_pad):
    s = f_logit.shape[0]
    t = _tile(s, ATT_TILE)
    nb = s // t

    def body(dc_ref, f_ref, b_ref, df_ref, db_ref, carry):
        @pl.when(pl.program_id(0) == 0)
        def _():
            carry[...] = jnp.zeros_like(carry)
            db_ref[...] = jnp.zeros_like(db_ref)

        d = dc_ref[...]
        tri = _tri(t, False)
        acc = carry[...]
        for part in _split3(d):
            acc = acc + _dot(tri, part)
        z = f_ref[...] + b_ref[...]
        df = acc * jnp.exp(_log_sigmoid(-z))
        df_ref[...] = df
        db_ref[...] += jnp.sum(df, axis=0, keepdims=True)
        carry[...] += jnp.sum(d, axis=0, keepdims=True)

    rev = pl.BlockSpec((t, LANES), lambda i: (nb - 1 - i, 0))
    vec = pl.BlockSpec((1, LANES), lambda i: (0, 0))
    return pl.pallas_call(
        body,
        grid=(nb,),
        in_specs=[rev, rev, vec],
        out_specs=[rev, vec],
        out_shape=[jax.ShapeDtypeStruct((s, LANES), F32), jax.ShapeDtypeStruct((1, LANES), F32)],
        scratch_shapes=[pltpu.VMEM((1, LANES), F32)],
        compiler_params=_params("arbitrary"),
        name="forget_bwd",
    )(dc, f_logit, b_pad)


SB_FWD_GROUP = 6
FOX_GROUP = 6
SB_BWD_GROUP = 3


def _head(ref, hh, rows=slice(None)):
    return ref[rows, hh * HEAD_DIM:(hh + 1) * HEAD_DIM]


def _tri_mask(t, strict):
    r = lax.broadcasted_iota(jnp.int32, (t, t), 0)
    c = lax.broadcasted_iota(jnp.int32, (t, t), 1)
    return (c < r) if strict else (c <= r)


def _fox_fwd(qn, kn, proj, colv, c_col, c_row, nheads, carry=None):
    s = qn.shape[0]
    t = _tile(s, ATT_TILE)
    scale = HEAD_DIM ** -0.5
    hg = FOX_GROUP
    gw = hg * HEAD_DIM
    assert nheads % hg == 0 and colv % hg == 0

    def body(q_ref, k_ref, v_ref, cc_ref, cr_ref, o_ref, of_ref, lse_ref):
        qi = pl.program_id(1)
        causal = _tri_mask(t, False)

        def tile(kj, carry, diagonal):
            off = pl.multiple_of(kj * t, t)
            heads = range(hg)
            rows = pl.ds(off, t)
            qk = [_dot_nt(_head(q_ref, hh), _head(k_ref, hh, rows)) for hh in heads]
            sc = [qk[hh] * scale + (cc_ref[hh] - cr_ref[hh, :, rows]) for hh in heads]
            if diagonal:
                sc = [jnp.where(causal, sc[hh], NEG_BIG) for hh in heads]
            m_new = [jnp.maximum(carry[hh][0], jnp.max(sc[hh], axis=-1, keepdims=True)) for hh in heads]
            p = [jnp.exp(sc[hh] - m_new[hh]) for hh in heads]
            pv = [_dot(p[hh].astype(BF16), _head(v_ref, hh, rows)) for hh in heads]
            out = []
            for hh in heads:
                m, l, acc = carry[hh]
                alpha = jnp.exp(m - m_new[hh])
                out.append((m_new[hh], alpha * l + jnp.sum(p[hh], axis=-1, keepdims=True), alpha * acc + pv[hh]))
            return tuple(out)

        init = tuple((jnp.full((t, 1), NEG_BIG, F32), jnp.zeros((t, 1), F32), jnp.zeros((t, HEAD_DIM), F32))
                     for _ in range(hg))
        carry = lax.fori_loop(0, qi, lambda kj, c: tile(kj, c, False), init)
        carry = tile(qi, carry, True)
        for hh in range(hg):
            m, l, acc = carry[hh]
            o = acc / l
            of_ref[:, hh * HEAD_DIM:(hh + 1) * HEAD_DIM] = o
            o_ref[:, hh * HEAD_DIM:(hh + 1) * HEAD_DIM] = o.astype(BF16)
            lse_ref[hh] = m + jnp.log(l)

    tile_spec = pl.BlockSpec((t, gw), lambda h, i: (i, h))
    w = nheads * HEAD_DIM
    return _call(
        body,
        grid=(nheads // hg, s // t),
        in_specs=[tile_spec,
                  pl.BlockSpec((s, gw), lambda h, i: (0, h), pipeline_mode=pl.Buffered(buffer_count=1)),
                  pl.BlockSpec((s, gw), lambda h, i: (0, colv // hg + h), pipeline_mode=pl.Buffered(buffer_count=1)),
                  pl.BlockSpec((hg, t, 1), lambda h, i: (h, i, 0)),
                  pl.BlockSpec((hg, 1, s), lambda h, i: (h, 0, 0))],
        out_specs=[tile_spec, tile_spec, pl.BlockSpec((hg, t, 1), lambda h, i: (h, i, 0))],
        out_shape=[jax.ShapeDtypeStruct((s, w), BF16), jax.ShapeDtypeStruct((s, w), F32),
                   jax.ShapeDtypeStruct((nheads, s, 1), F32)],
        scratch_shapes=[],
        semantics=("parallel", "parallel"),
        name="fox_fwd",
        args=(qn, kn, proj, c_col, c_row),
        carry=carry,
    )


def _fox_bwd(qn, kn, proj, colv, c_col, c_row, o, do, lse, nheads, carry=None):
    s = qn.shape[0]
    t = _tile(s, ATT_TILE)
    scale = HEAD_DIM ** -0.5
    hg = FOX_GROUP
    gw = hg * HEAD_DIM
    assert nheads % hg == 0 and colv % hg == 0

    def body(q_ref, k_ref, v_ref, cc_ref, cr_ref, o_ref, do_ref, lse_ref,
             dq_ref, dk_ref, dv_ref, drs_ref, dcs_ref):
        qi = pl.program_id(1)

        @pl.when(qi == 0)
        def _():
            dk_ref[...] = jnp.zeros_like(dk_ref)
            dv_ref[...] = jnp.zeros_like(dv_ref)
            dcs_ref[...] = jnp.zeros_like(dcs_ref)

        causal = _tri_mask(t, False)
        delta = [jnp.sum(_head(o_ref, hh) * _head(do_ref, hh).astype(F32), axis=-1, keepdims=True)
                 for hh in range(hg)]

        def tile(kj, carry, diagonal):
            off = pl.multiple_of(kj * t, t)
            heads = range(hg)
            rows = pl.ds(off, t)
            qk = [_dot_nt(_head(q_ref, hh), _head(k_ref, hh, rows)) for hh in heads]
            dp = [_dot_nt(_head(do_ref, hh), _head(v_ref, hh, rows)) for hh in heads]
            p = [jnp.exp(qk[hh] * scale + (cc_ref[hh] - cr_ref[hh, :, rows]) - lse_ref[hh]) for hh in heads]
            if diagonal:
                p = [jnp.where(causal, p[hh], 0.0) for hh in heads]
            ds = [p[hh] * (dp[hh] - delta[hh]) for hh in heads]
            dsb = [ds[hh].astype(BF16) for hh in heads]
            dv = [_dot_tn(p[hh].astype(BF16), _head(do_ref, hh)) for hh in heads]
            dk = [_dot_tn(dsb[hh], _head(q_ref, hh)) * scale for hh in heads]
            dq = [_dot(dsb[hh], _head(k_ref, hh, rows)) * scale for hh in heads]
            for hh in heads:
                cols = slice(hh * HEAD_DIM, (hh + 1) * HEAD_DIM)
                dv_ref[rows, cols] += dv[hh]
                dk_ref[rows, cols] += dk[hh]
                dcs_ref[hh, :, rows] += jnp.sum(ds[hh], axis=0, keepdims=True)
            return tuple((carry[hh][0] + dq[hh], carry[hh][1] + jnp.sum(ds[hh], axis=-1, keepdims=True))
                         for hh in heads)

        init = tuple((jnp.zeros((t, HEAD_DIM), F32), jnp.zeros((t, 1), F32)) for _ in range(hg))
        carry = lax.fori_loop(0, qi, lambda kj, c: tile(kj, c, False), init)
        carry = tile(qi, carry, True)
        for hh in range(hg):
            dq_ref[:, hh * HEAD_DIM:(hh + 1) * HEAD_DIM] = carry[hh][0]
            drs_ref[hh] = carry[hh][1]

    tile_spec = pl.BlockSpec((t, gw), lambda h, i: (i, h))
    once = pl.Buffered(buffer_count=1)
    full = pl.BlockSpec((s, gw), lambda h, i: (0, h), pipeline_mode=once)
    colspec = pl.BlockSpec((hg, t, 1), lambda h, i: (h, i, 0))
    rowspec = pl.BlockSpec((hg, 1, s), lambda h, i: (h, 0, 0))
    w = nheads * HEAD_DIM
    return _call(
        body,
        grid=(nheads // hg, s // t),
        in_specs=[tile_spec, full, pl.BlockSpec((s, gw), lambda h, i: (0, colv // hg + h), pipeline_mode=once),
                  colspec, rowspec,
                  tile_spec, tile_spec, colspec],
        out_specs=[tile_spec, full, full, colspec, rowspec],
        out_shape=[jax.ShapeDtypeStruct((s, w), F32), jax.ShapeDtypeStruct((s, w), F32),
                   jax.ShapeDtypeStruct((s, w), F32), jax.ShapeDtypeStruct((nheads, s, 1), F32),
                   jax.ShapeDtypeStruct((nheads, 1, s), F32)],
        scratch_shapes=[],
        semantics=("arbitrary", "arbitrary"),
        name="fox_bwd",
        args=(qn, kn, proj, c_col, c_row, o, do, lse),
        carry=carry,
    )


def _sb_tile(q, k, scale, later, valid):
    z = _dot_nt(q, k) * scale
    lb = _log_sigmoid(z)
    lm = lb - z
    if valid is not None:
        lm = jnp.where(valid, lm, 0.0)
    suffix = _dot(jnp.concatenate(_split2(lm), axis=1), later)
    return lb, lm, suffix


def _later(t):
    r = lax.broadcasted_iota(jnp.int32, (2 * t, t), 0) % t
    c = lax.broadcasted_iota(jnp.int32, (2 * t, t), 1)
    return jnp.where(r > c, 1.0, 0.0).astype(BF16)


def _sb_fwd(proj, colq, colk, colv, nheads, carry=None):
    s = proj.shape[0]
    t = _tile(s, ATT_TILE)
    scale = HEAD_DIM ** -0.5
    hg = SB_FWD_GROUP
    gw = hg * HEAD_DIM
    assert nheads % hg == 0 and colq % hg == 0 and colk % hg == 0 and colv % hg == 0

    def body(q_ref, k_ref, v_ref, o_ref):
        qi = pl.program_id(1)
        later = _later(t)
        before = _tri_mask(t, True)

        def tile(kj, carry, diagonal):
            off = pl.multiple_of(kj * t, t)
            heads = range(hg)
            z = [_dot_nt(_head(q_ref, hh), _head(k_ref, hh, pl.ds(off, t))) * scale for hh in heads]
            lb = [_log_sigmoid(z[hh]) for hh in heads]
            lm = [lb[hh] - z[hh] for hh in heads]
            if diagonal:
                lm = [jnp.where(before, lm[hh], 0.0) for hh in heads]
            parts = [jnp.concatenate(_split2(lm[hh]), axis=1) for hh in heads]
            suffix = [_dot(parts[hh], later) for hh in heads]
            a = [jnp.exp(lb[hh] + suffix[hh] + carry[hh][0]) for hh in heads]
            if diagonal:
                a = [jnp.where(before, a[hh], 0.0) for hh in heads]
            av = [_dot(a[hh].astype(BF16), _head(v_ref, hh, pl.ds(off, t))) for hh in heads]
            return tuple((carry[hh][0] + jnp.sum(lm[hh], axis=-1, keepdims=True), carry[hh][1] + av[hh])
                         for hh in heads)

        init = tuple((jnp.zeros((t, 1), F32), jnp.zeros((t, HEAD_DIM), F32)) for _ in range(hg))
        carry = tile(qi, init, True)
        carry = lax.fori_loop(1, qi + 1, lambda i, c: tile(qi - i, c, False), carry)
        for hh in range(hg):
            o_ref[:, hh * HEAD_DIM:(hh + 1) * HEAD_DIM] = carry[hh][1].astype(BF16)

    return _call(
        body,
        grid=(nheads // hg, s // t),
        in_specs=[pl.BlockSpec((t, gw), lambda h, i: (i, colq // hg + h)),
                  pl.BlockSpec((s, gw), lambda h, i: (0, colk // hg + h), pipeline_mode=pl.Buffered(buffer_count=1)),
                  pl.BlockSpec((s, gw), lambda h, i: (0, colv // hg + h), pipeline_mode=pl.Buffered(buffer_count=1))],
        out_specs=[pl.BlockSpec((t, gw), lambda h, i: (i, h))],
        out_shape=[jax.ShapeDtypeStruct((s, nheads * HEAD_DIM), BF16)],
        scratch_shapes=[],
        semantics=("parallel", "parallel"),
        name="sb_fwd",
        args=(proj, proj, proj),
        carry=carry,
    )


def _sb_bwd(proj, colq, colk, colv, do, nheads, carry=None):
    s = proj.shape[0]
    t = _tile(s, ATT_TILE)
    scale = HEAD_DIM ** -0.5
    hg = SB_BWD_GROUP
    gw = hg * HEAD_DIM
    assert nheads % hg == 0 and colq % hg == 0 and colk % hg == 0 and colv % hg == 0

    def body(q_ref, k_ref, v_ref, do_ref, dq_ref, dk_ref, dv_ref, g_s, beta_s):
        qi = pl.program_id(1)

        @pl.when(qi == 0)
        def _():
            dk_ref[...] = jnp.zeros_like(dk_ref)
            dv_ref[...] = jnp.zeros_like(dv_ref)

        later = _later(t)
        before = _tri_mask(t, True)

        def back(kj, carry, diagonal):
            off = pl.multiple_of(kj * t, t)
            heads = range(hg)
            rows = pl.ds(off, t)
            z = [_dot_nt(_head(q_ref, hh), _head(k_ref, hh, rows)) * scale for hh in heads]
            da = [_dot_nt(_head(do_ref, hh), _head(v_ref, hh, rows)) for hh in heads]
            lb = [_log_sigmoid(z[hh]) for hh in heads]
            lm = [lb[hh] - z[hh] for hh in heads]
            if diagonal:
                lm = [jnp.where(before, lm[hh], 0.0) for hh in heads]
            parts = [jnp.concatenate(_split2(lm[hh]), axis=1) for hh in heads]
            suffix = [_dot(parts[hh], later) for hh in heads]
            a = [jnp.exp(lb[hh] + suffix[hh] + carry[hh]) for hh in heads]
            if diagonal:
                a = [jnp.where(before, a[hh], 0.0) for hh in heads]
            dv = [_dot_tn(a[hh].astype(BF16), _head(do_ref, hh)) for hh in heads]
            for hh in heads:
                g_s[hh, :, rows] = a[hh] * da[hh]
                beta_s[hh, :, rows] = jnp.exp(lb[hh]).astype(BF16)
            for hh in heads:
                dv_ref[rows, hh * HEAD_DIM:(hh + 1) * HEAD_DIM] += dv[hh]
            return tuple(carry[hh] + jnp.sum(lm[hh], axis=-1, keepdims=True) for hh in heads)

        rc = back(qi, tuple(jnp.zeros((t, 1), F32) for _ in range(hg)), True)
        lax.fori_loop(1, qi + 1, lambda i, c: back(qi - i, c, False), rc)

        earlier = jnp.where(lax.broadcasted_iota(jnp.int32, (2 * t, t), 0) % t
                            < lax.broadcasted_iota(jnp.int32, (2 * t, t), 1), 1.0, 0.0).astype(BF16)

        def fwd(kj, carry, diagonal):
            off = pl.multiple_of(kj * t, t)
            heads = range(hg)
            rows = pl.ds(off, t)
            g = [g_s[hh, :, rows] for hh in heads]
            parts = [jnp.concatenate(_split2(g[hh]), axis=1) for hh in heads]
            gsum = [_dot(parts[hh], earlier) + carry[hh][0] for hh in heads]
            dz = []
            for hh in heads:
                beta = beta_s[hh, :, rows].astype(F32)
                d = g[hh] * (1.0 - beta) - gsum[hh] * beta
                if diagonal:
                    d = jnp.where(before, d, 0.0)
                dz.append(d.astype(BF16))
            dk = [_dot_tn(dz[hh], _head(q_ref, hh)) * scale for hh in heads]
            dq = [_dot(dz[hh], _head(k_ref, hh, rows)) * scale for hh in heads]
            for hh in heads:
                dk_ref[rows, hh * HEAD_DIM:(hh + 1) * HEAD_DIM] += dk[hh]
            return tuple((carry[hh][0] + jnp.sum(g[hh], axis=-1, keepdims=True), carry[hh][1] + dq[hh])
                         for hh in heads)

        init = tuple((jnp.zeros((t, 1), F32), jnp.zeros((t, HEAD_DIM), F32)) for _ in range(hg))
        carry = lax.fori_loop(0, qi, lambda kj, c: fwd(kj, c, False), init)
        carry = fwd(qi, carry, True)
        for hh in range(hg):
            dq_ref[:, hh * HEAD_DIM:(hh + 1) * HEAD_DIM] = carry[hh][1]

    once = pl.Buffered(buffer_count=1)
    tile_spec = pl.BlockSpec((t, gw), lambda h, i: (i, h))
    full = pl.BlockSpec((s, gw), lambda h, i: (0, h), pipeline_mode=once)
    w = nheads * HEAD_DIM
    return _call(
        body,
        grid=(nheads // hg, s // t),
        in_specs=[pl.BlockSpec((t, gw), lambda h, i: (i, colq // hg + h)),
                  pl.BlockSpec((s, gw), lambda h, i: (0, colk // hg + h), pipeline_mode=once),
                  pl.BlockSpec((s, gw), lambda h, i: (0, colv // hg + h), pipeline_mode=once),
                  tile_spec],
        out_specs=[tile_spec, full, full],
        out_shape=[jax.ShapeDtypeStruct((s, w), F32)] * 3,
        scratch_shapes=[pltpu.VMEM((hg, t, s), F32), pltpu.VMEM((hg, t, s), BF16)],
        semantics=("arbitrary", "arbitrary"),
        name="sb_bwd",
        args=(proj, proj, proj, do),
        carry=carry,
    )


def _mem_fwd(qn, kn, mkv, nheads):
    s = qn.shape[0]
    mtok = kn.shape[0]
    t = _tile(s, ATT_TILE)
    scale = HEAD_DIM ** -0.5

    def body(q_ref, k_ref, v_ref, o_ref):
        sc = _dot_nt(q_ref[...], k_ref[...]) * scale
        p = jnp.exp(sc - jnp.max(sc, axis=-1, keepdims=True))
        p = p / jnp.sum(p, axis=-1, keepdims=True)
        o_ref[...] = _dot(p.astype(BF16), v_ref[...]).astype(BF16)

    return pl.pallas_call(
        body,
        grid=(nheads, s // t),
        in_specs=[pl.BlockSpec((t, HEAD_DIM), lambda h, i: (i, h)),
                  pl.BlockSpec((mtok, HEAD_DIM), lambda h, i: (0, h)),
                  pl.BlockSpec((mtok, HEAD_DIM), lambda h, i: (0, nheads + h))],
        out_specs=pl.BlockSpec((t, HEAD_DIM), lambda h, i: (i, h)),
        out_shape=jax.ShapeDtypeStruct((s, nheads * HEAD_DIM), BF16),
        compiler_params=_params("parallel", "parallel"),
        name="mem_fwd",
    )(qn, kn, mkv)


def _mem_bwd(qn, kn, mkv, do, nheads):
    s = qn.shape[0]
    mtok = kn.shape[0]
    t = _tile(s, ATT_TILE)
    scale = HEAD_DIM ** -0.5

    def body(q_ref, k_ref, v_ref, do_ref, dq_ref, dk_ref, dv_ref):
        @pl.when(pl.program_id(1) == 0)
        def _():
            dk_ref[...] = jnp.zeros_like(dk_ref)
            dv_ref[...] = jnp.zeros_like(dv_ref)

        q = q_ref[...]
        k = k_ref[...]
        do_ = do_ref[...]
        sc = _dot_nt(q, k) * scale
        p = jnp.exp(sc - jnp.max(sc, axis=-1, keepdims=True))
        p = p / jnp.sum(p, axis=-1, keepdims=True)
        dp = _dot_nt(do_, v_ref[...])
        ds = (p * (dp - jnp.sum(p * dp, axis=-1, keepdims=True))).astype(BF16)
        dq_ref[...] = _dot(ds, k) * scale
        dk_ref[...] += _dot_tn(ds, q) * scale
        dv_ref[...] += _dot_tn(p.astype(BF16), do_)

    tile = pl.BlockSpec((t, HEAD_DIM), lambda h, i: (i, h))
    kspec = pl.BlockSpec((mtok, HEAD_DIM), lambda h, i: (0, h))
    w = nheads * HEAD_DIM
    return pl.pallas_call(
        body,
        grid=(nheads, s // t),
        in_specs=[tile, kspec, pl.BlockSpec((mtok, HEAD_DIM), lambda h, i: (0, nheads + h)), tile],
        out_specs=[tile, kspec, kspec],
        out_shape=[jax.ShapeDtypeStruct((s, w), F32), jax.ShapeDtypeStruct((mtok, w), F32),
                   jax.ShapeDtypeStruct((mtok, w), F32)],
        compiler_params=_params("arbitrary", "arbitrary"),
        name="mem_bwd",
    )(qn, kn, mkv, do)


def _merge_fwd(p0, p1, p2, gates, b_gate):
    s, d = p0.shape
    tm, tn = _tile(s, ROW_TILE), _tile(d, COL_TILE)
    nj = d // tn

    def body(p0_ref, p1_ref, p2_ref, ga_ref, gb_ref, gc_ref, b_ref, o_ref):
        acc = jnp.zeros((tm, tn), F32)
        for b, (p_ref, g_ref) in enumerate(((p0_ref, ga_ref), (p1_ref, gb_ref), (p2_ref, gc_ref))):
            gate = jax.nn.sigmoid(g_ref[...].astype(F32) + b_ref[b:b + 1, :])
            acc = acc + gate * p_ref[...]
        o_ref[...] = acc.astype(BF16)

    blk = pl.BlockSpec((tm, tn), lambda i, j: (i, j))
    return pl.pallas_call(
        body,
        grid=(s // tm, nj),
        in_specs=[blk] * 6 + [pl.BlockSpec((3, tn), lambda i, j: (0, j))],
        out_specs=blk,
        out_shape=jax.ShapeDtypeStruct((s, d), BF16),
        compiler_params=_params("parallel", "parallel"),
        name="merge_fwd",
    )(p0, p1, p2, *gates, b_gate)


def _merge_bwd(dmerged, p0, p1, p2, gates, b_gate):
    s, d = p0.shape
    tm, tn = _tile(s, ROW_TILE), _tile(d, COL_TILE)
    nj = d // tn

    def body(dm_ref, p0_ref, p1_ref, p2_ref, ga_ref, gb_ref, gc_ref, b_ref,
             d0_ref, d1_ref, d2_ref, dga_ref, dgb_ref, dgc_ref, db_ref):
        dm = dm_ref[...].astype(F32)
        parts = []
        for b, (p_ref, g_ref, dp_ref, dg_ref) in enumerate(((p0_ref, ga_ref, d0_ref, dga_ref),
                                                            (p1_ref, gb_ref, d1_ref, dgb_ref),
                                                            (p2_ref, gc_ref, d2_ref, dgc_ref))):
            gate = jax.nn.sigmoid(g_ref[...].astype(F32) + b_ref[b:b + 1, :])
            dp_ref[...] = (dm * gate).astype(BF16)
            dgate = dm * p_ref[...] * gate * (1.0 - gate)
            dg_ref[...] = dgate.astype(BF16)
            parts.append(jnp.sum(dgate, axis=0, keepdims=True))
        part = jnp.concatenate(parts, axis=0)

        @pl.when(pl.program_id(1) == 0)
        def _():
            db_ref[...] = part

        @pl.when(pl.program_id(1) > 0)
        def _():
            db_ref[...] += part

    blk = pl.BlockSpec((tm, tn), lambda j, i: (i, j))
    bias = pl.BlockSpec((3, tn), lambda j, i: (0, j))
    return pl.pallas_call(
        body,
        grid=(nj, s // tm),
        in_specs=[blk] * 7 + [bias],
        out_specs=[blk] * 6 + [bias],
        out_shape=[jax.ShapeDtypeStruct((s, d), BF16)] * 6 + [jax.ShapeDtypeStruct((3, d), F32)],
        compiler_params=_params("parallel", "arbitrary"),
        name="merge_bwd",
    )(dmerged, p0, p1, p2, *gates, b_gate)


def _shift_down(v, n):
    rows = lax.broadcasted_iota(jnp.int32, v.shape, 0)
    return jnp.where(rows >= n, pltpu.roll(v, n, 0), 0.0)


def _shift_up(v, n):
    s = v.shape[0]
    rows = lax.broadcasted_iota(jnp.int32, v.shape, 0)
    return jnp.where(rows < s - n, pltpu.roll(v, s - n, 0), 0.0)


def _conv(v, w_ref, b_ref):
    taps = w_ref.shape[0]
    out = v * w_ref[taps - 1:taps, :] + b_ref[...]
    for n in range(1, taps):
        out = out + _shift_down(v, n) * w_ref[taps - 1 - n:taps - n, :]
    return out


def _conv_act_fwd(up, conv_w, conv_b):
    s, f2 = up.shape
    f = f2 // 2
    tn = LANES
    nj = f // tn
    taps = conv_w.shape[0]

    def body(ug_ref, uv_ref, wg_ref, wv_ref, bg_ref, bv_ref, o_ref):
        cg = _conv(ug_ref[...].astype(F32), wg_ref, bg_ref)
        cv = _conv(uv_ref[...].astype(F32), wv_ref, bv_ref)
        o_ref[...] = (cg * jax.nn.sigmoid(cg) * cv).astype(BF16)

    return pl.pallas_call(
        body,
        grid=(nj,),
        in_specs=[pl.BlockSpec((s, tn), lambda j: (0, j)), pl.BlockSpec((s, tn), lambda j: (0, nj + j)),
                  pl.BlockSpec((taps, tn), lambda j: (0, j)), pl.BlockSpec((taps, tn), lambda j: (0, nj + j)),
                  pl.BlockSpec((1, tn), lambda j: (0, j)), pl.BlockSpec((1, tn), lambda j: (0, nj + j))],
        out_specs=pl.BlockSpec((s, tn), lambda j: (0, j)),
        out_shape=jax.ShapeDtypeStruct((s, f), BF16),
        compiler_params=_params("parallel"),
        name="conv_act_fwd",
    )(up, up, conv_w, conv_w, conv_b, conv_b)


def _conv_act_bwd(up, conv_w, conv_b, dact):
    s, f2 = up.shape
    f = f2 // 2
    tn = LANES
    nj = f // tn
    taps = conv_w.shape[0]

    def half(v, du, w_ref, dup_ref, dw_ref, db_ref):
        dup = du * w_ref[taps - 1:taps, :]
        rows = [None] * taps
        rows[taps - 1] = jnp.sum(du * v, axis=0, keepdims=True)
        for n in range(1, taps):
            later = _shift_up(du, n)
            dup = dup + later * w_ref[taps - 1 - n:taps - n, :]
            rows[taps - 1 - n] = jnp.sum(later * v, axis=0, keepdims=True)
        dup_ref[...] = dup.astype(BF16)
        dw_ref[...] = jnp.concatenate(rows, axis=0)
        db_ref[...] = jnp.sum(du, axis=0, keepdims=True)

    def body(ug_ref, uv_ref, wg_ref, wv_ref, bg_ref, bv_ref, da_ref,
             dug_ref, duv_ref, dwg_ref, dwv_ref, dbg_ref, dbv_ref):
        ug = ug_ref[...].astype(F32)
        uv = uv_ref[...].astype(F32)
        cg = _conv(ug, wg_ref, bg_ref)
        cv = _conv(uv, wv_ref, bv_ref)
        da = da_ref[...].astype(F32)
        sg = jax.nn.sigmoid(cg)
        dcv = da * cg * sg
        dcg = da * cv * (sg + cg * sg * (1.0 - sg))
        half(ug, dcg, wg_ref, dug_ref, dwg_ref, dbg_ref)
        half(uv, dcv, wv_ref, duv_ref, dwv_ref, dbv_ref)

    lo = lambda rows: pl.BlockSpec((rows, tn), lambda j: (0, j))
    hi = lambda rows: pl.BlockSpec((rows, tn), lambda j: (0, nj + j))
    return pl.pallas_call(
        body,
        grid=(nj,),
        in_specs=[lo(s), hi(s), lo(taps), hi(taps), lo(1), hi(1), lo(s)],
        out_specs=[lo(s), lo(s), lo(taps), lo(taps), lo(1), lo(1)],
        out_shape=[jax.ShapeDtypeStruct((s, f), BF16)] * 2 + [jax.ShapeDtypeStruct((taps, f), F32)] * 2
        + [jax.ShapeDtypeStruct((1, f), F32)] * 2,
        compiler_params=_params("parallel"),
        name="conv_act_bwd",
    )(up, up, conv_w, conv_w, conv_b, conv_b, dact)


def _row_tile(rows, row_bytes, budget):
    if rows * row_bytes <= budget or rows % 8:
        return rows
    best = 8
    for t in range(8, rows, 8):
        if rows % t == 0 and t * row_bytes <= budget:
            best = t
    return best


def _adamw(w, g, m, v, name):
    r, c = w.shape
    tr = _row_tile(r, c * 4, ADAM_BLOCK_BYTES)

    def body(w_ref, g_ref, m_ref, v_ref, d_ref, mo_ref, vo_ref):
        gg = g_ref[...]
        m_new = ADAM_B1 * m_ref[...] + (1.0 - ADAM_B1) * gg
        v_new = ADAM_B2 * v_ref[...] + (1.0 - ADAM_B2) * (gg * gg)
        m_hat = m_new / (1.0 - ADAM_B1 ** ADAM_STEP)
        v_hat = v_new / (1.0 - ADAM_B2 ** ADAM_STEP)
        d_ref[...] = -ADAM_LR * (m_hat / (jnp.sqrt(v_hat) + ADAM_EPS) + ADAM_WD * w_ref[...])
        mo_ref[...] = m_new
        vo_ref[...] = v_new

    blk = pl.BlockSpec((tr, c), lambda i: (i, 0))
    return pl.pallas_call(
        body,
        grid=(r // tr,),
        in_specs=[blk] * 4,
        out_specs=[blk] * 3,
        out_shape=[jax.ShapeDtypeStruct((r, c), F32)] * 3,
        compiler_params=_params("parallel"),
        name=name,
    )(w, g, m, v)


def _add_sibling(g, r1, core, name):
    _, _, h, c = g.shape
    th = _row_tile(h, c * 2, ADAM_BLOCK_BYTES)

    def body(core_ref, g_ref, r_ref, o_ref):
        o_ref[...] = (g_ref[...].astype(F32) + r_ref[...].astype(F32)).astype(BF16)

    return pl.pallas_call(
        body,
        grid_spec=pltpu.PrefetchScalarGridSpec(
            num_scalar_prefetch=1,
            grid=(N_CHIPS, h // th),
            in_specs=[pl.BlockSpec((None, None, th, c), lambda j, i, core_ref: (j, core_ref[0], i, 0)),
                      pl.BlockSpec((None, th, c), lambda j, i, core_ref: (j, i, 0))],
            out_specs=pl.BlockSpec((None, th, c), lambda j, i, core_ref: (j, i, 0)),
        ),
        out_shape=jax.ShapeDtypeStruct((N_CHIPS, h, c), BF16),
        compiler_params=_params("parallel", "parallel"),
        name=name,
    )(core, g, r1)


def _add_chips(hsum, r2, chip_core, name):
    _, h, c = hsum.shape
    th = _row_tile(h, c * 4, ADAM_BLOCK_BYTES)

    def body(sel_ref, own_ref, r_ref, o_ref):
        acc = own_ref[...].astype(F32)
        for j in range(N_CHIPS - 1):
            acc = acc + r_ref[j].astype(F32)
        o_ref[...] = acc

    return pl.pallas_call(
        body,
        grid_spec=pltpu.PrefetchScalarGridSpec(
            num_scalar_prefetch=1,
            grid=(h // th,),
            in_specs=[pl.BlockSpec((None, th, c), lambda i, sel_ref: (sel_ref[0], i, 0)),
                      pl.BlockSpec((N_CHIPS - 1, th, c), lambda i, sel_ref: (0, i, 0))],
            out_specs=pl.BlockSpec((None, th, c), lambda i, sel_ref: (sel_ref[1], i, 0)),
        ),
        out_shape=jax.ShapeDtypeStruct((2, h, c), F32),
        compiler_params=_params("parallel"),
        name=name,
    )(chip_core, hsum, r2)


def _sum_devices(parts):
    _, r, c = parts.shape

    def body(p_ref, o_ref):
        acc = p_ref[0]
        for j in range(1, N_DEV):
            acc = acc + p_ref[j]
        o_ref[...] = acc

    return pl.pallas_call(
        body,
        out_shape=jax.ShapeDtypeStruct((r, c), F32),
        compiler_params=pltpu.CompilerParams(vmem_limit_bytes=VMEM_LIMIT_BYTES),
        name="sum_devices",
    )(parts)


def _join_halves(finals):
    n = len(finals)

    def body(*refs):
        outs = refs[n:2 * n]
        send_sems, recv_sems = refs[2 * n:]
        x, y, c, _ = _place()
        sends = [_remote(outs[i].at[c], outs[i].at[c], send_sems.at[i], recv_sems.at[i], (x, y, 1 - c))
                 for i in range(n)]
        for cp in sends:
            cp.start()
        for i in range(n):
            sends[i].wait_send()
            other = outs[i].at[1 - c]
            _remote(other, other, send_sems.at[i], recv_sems.at[i], (x, y, 1 - c)).wait_recv()

    return pl.pallas_call(
        body,
        in_specs=[ANY] * n,
        out_specs=[ANY] * n,
        out_shape=[jax.ShapeDtypeStruct(f.shape, f.dtype) for f in finals],
        input_output_aliases={i: i for i in range(n)},
        scratch_shapes=[pltpu.SemaphoreType.DMA((n,)), pltpu.SemaphoreType.DMA((n,))],
        name="join_halves",
    )(*finals)


def _gather_small(vec):
    k = N_DEV - 1

    def body(v_ref, o_ref, send_sems, recv_sems, local_sem):
        x, y, c, _ = _place()
        me = 4 * x + 2 * y + c
        local = pltpu.make_async_copy(v_ref, o_ref.at[me], local_sem)
        local.start()
        peers = [(x ^ (r >> 2 & 1), y ^ (r >> 1 & 1), c ^ (r & 1)) for r in range(1, N_DEV)]
        sends = [_remote(v_ref, o_ref.at[me], send_sems.at[j], recv_sems.at[j], p) for j, p in enumerate(peers)]
        for cp in sends:
            cp.start()
        for j, (px, py, pc) in enumerate(peers):
            sends[j].wait_send()
            blk = o_ref.at[4 * px + 2 * py + pc]
            _remote(blk, blk, send_sems.at[j], recv_sems.at[j], (px, py, pc)).wait_recv()
        local.wait()

    return pl.pallas_call(
        body,
        in_specs=[ANY],
        out_specs=ANY,
        out_shape=jax.ShapeDtypeStruct((N_DEV,) + vec.shape, vec.dtype),
        scratch_shapes=[pltpu.SemaphoreType.DMA((k,)), pltpu.SemaphoreType.DMA((k,)), pltpu.SemaphoreType.DMA(())],
        name="gather_small",
    )(vec)


W_IN_GATES = ("w_in_g0", "w_in_g1", "w_in_g2")
ROW_SHARDED = ("w_in_a", "w_in_f") + W_IN_GATES + ("w_mem_kv", "w_out", "w_down")
COL_SHARDED = ("w_br_fox", "w_br_sb", "w_br_mem", "w_up")
BIG = ROW_SHARDED + COL_SHARDED


def _whole(name, a):
    if name in ROW_SHARDED:
        return a.reshape(N_CHIPS * a.shape[1], a.shape[2])
    return a.transpose(1, 0, 2).reshape(a.shape[1], N_CHIPS * a.shape[2])


def _by_shard(name, grad):
    if name in ROW_SHARDED:
        a = grad.reshape(N_CHIPS, grad.shape[0] // N_CHIPS, grad.shape[1])
    else:
        a = grad.reshape(grad.shape[0], N_CHIPS, grad.shape[1] // N_CHIPS).transpose(1, 0, 2)
    return a.reshape(N_CHIPS, 2, a.shape[1] // 2, a.shape[2])


def _sibling_sums(names, split, theirs, core):
    return [_add_sibling(a, r, core, "add_sibling_" + name) for name, a, r in zip(names, split, theirs)]


GATHER_FIRST = ("w_in_a", "w_in_f")
GATHER_EARLY = W_IN_GATES[:2] + ("w_mem_kv",)
GATHER_MIX = ("w_out", "w_br_fox", "w_br_sb", "w_br_mem")
REDUCE_FFN = ("w_down", "w_up")
REDUCE_MIX = ("w_out", "w_br_fox", "w_br_sb", "w_br_mem") + W_IN_GATES
REDUCE_IN = ("w_in_a", "w_in_f", "w_mem_kv")


def _local_step(x, mem, target, w, shard, core, chip_core):
    d = x.shape[1]
    nf = shard["w_br_fox"].shape[0] // HEAD_DIM
    nsb = shard["w_br_sb"].shape[0] // HEAD_DIM
    nm = shard["w_br_mem"].shape[0] // HEAD_DIM
    w = dict(w)

    def take(names, gathered):
        for name, a in zip(names, gathered):
            w[name] = _whole(name, a)

    take(GATHER_FIRST, _exchange(_Gather([shard[name] for name in GATHER_FIRST]), "gather_first"))
    fq, fk, fv = 0, nf, 2 * nf
    sq, sk, sv = 3 * nf, 3 * nf + nsb, 3 * nf + 2 * nsb
    mq = 3 * nf + 3 * nsb

    h, rstd1 = _rms_fwd(x, w["g_mix"], "rms_mix_fwd")
    proj, moved = _mm(h, w["w_in_a"], "nn", BF16, "proj_att", carry=_Gather([shard[name] for name in GATHER_EARLY]))
    take(GATHER_EARLY, moved)
    f_logit = _mm(h, w["w_in_f"], "nn", F32, "proj_forget")
    gate0, moved = _mm(h, w["w_in_g0"], "nn", BF16, "proj_gate0", carry=_Gather([shard[W_IN_GATES[2]]]))
    take(W_IN_GATES[2:], moved)
    gates = (gate0, _mm(h, w["w_in_g1"], "nn", BF16, "proj_gate1"), _mm(h, w["w_in_g2"], "nn", BF16, "proj_gate2"))
    c_sum = _forget_fwd(f_logit, w["b_forget"])
    c_t = c_sum[:, :nf].T
    c_col, c_row = c_t[:, :, None], c_t[:, None, :]
    qn = _headnorm_fwd(proj, fq, nf, w["g_q_fox"], "fox_qnorm_fwd")
    kn = _headnorm_fwd(proj, fk, nf, w["g_k_fox"], "fox_knorm_fwd")
    (o_fox, o_fox32, lse), moved = _fox_fwd(qn, kn, proj, fv, c_col, c_row, nf,
                                            carry=_Gather([shard[name] for name in GATHER_MIX]))
    take(GATHER_MIX, moved)
    (o_sb,), moved = _sb_fwd(proj, sq, sk, sv, nsb, carry=_Gather([shard["w_up"]]))
    take(("w_up",), moved)
    memn, rstd_m = _rms_fwd(mem, w["g_mem"], "rms_mem_fwd")
    mkv = _mm(memn, w["w_mem_kv"], "nn", BF16, "mem_kv")
    kmn = _headnorm_fwd(mkv, 0, nm, w["g_k_mem"], "mem_knorm_fwd")
    qmn = _headnorm_fwd(proj, mq, nm, w["g_q_mem"], "mem_qnorm_fwd")
    o_mem = _mem_fwd(qmn, kmn, mkv, nm)
    p0 = _mm(o_fox, w["w_br_fox"], "nn", F32, "branch_fox")
    p1 = _mm(o_sb, w["w_br_sb"], "nn", F32, "branch_sb")
    p2 = _mm(o_mem, w["w_br_mem"], "nn", F32, "branch_mem")
    merged = _merge_fwd(p0, p1, p2, gates, w["b_gate"])
    x1 = _mm(merged, w["w_out"], "nn", F32, "out_proj", residual=x)
    h2, rstd2 = _rms_fwd(x1, w["g_ffn"], "rms_ffn_fwd")
    up, moved = _mm(h2, w["w_up"], "nn", BF16, "ffn_up", carry=_Gather([shard["w_down"]]))
    take(("w_down",), moved)
    act =_conv_act_fwd(up, w["conv_w"], w["conv_b"])
    dy, dyb, lparts = _mm(act, w["w_down"], "nn", F32, "ffn_down_loss", residual=x1, loss_target=target)
    loss = (0.5 / d) * jnp.sum(lparts[::8, ::LANES])

    g = {}
    dact = _mm(dyb, w["w_down"], "nt", BF16, "ffn_down_dx")
    g["w_down"] = _mm(act, dyb, "tn", BF16, "ffn_down_dw")
    dug, duv, dwg, dwv, dbg, dbv = _conv_act_bwd(up, w["conv_w"], w["conv_b"], dact)
    dup = jnp.concatenate([dug, duv], axis=1)
    g["conv_w"] = jnp.concatenate([dwg, dwv], axis=1)
    g["conv_b"] = jnp.concatenate([dbg, dbv], axis=1)
    split_down = [_by_shard("w_down", g["w_down"])]
    dh2, theirs_down = _mm(dup, w["w_up"], "nt", BF16, "ffn_up_dx", carry=_Swap(split_down))
    g["w_up"] = _mm(h2, dup, "tn", BF16, "ffn_up_dw")
    split_up = [_by_shard("w_up", g["w_up"])]
    dx1, dx1b, g["g_ffn"] = _rms_bwd(dh2, x1, rstd2, w["g_ffn"], dy, "rms_ffn_bwd")
    dmerged, theirs_up = _mm(dx1b, w["w_out"], "nt", BF16, "out_proj_dx", carry=_Swap(split_up))
    sums_ffn = _sibling_sums(REDUCE_FFN, split_down + split_up, theirs_down + theirs_up, core)
    g["w_out"] = _mm(merged, dx1b, "tn", BF16, "out_proj_dw")
    dp0, dp1, dp2, dga, dgb, dgc, g["b_gate"] = _merge_bwd(dmerged, p0, p1, p2, gates, w["b_gate"])
    dgates = (dga, dgb, dgc)
    for name, dgate in zip(W_IN_GATES, dgates):
        g[name] = _mm(h, dgate, "tn", BF16, name + "_dw")
    do_fox = _mm(dp0, w["w_br_fox"], "nt", BF16, "branch_fox_dx")
    do_sb = _mm(dp1, w["w_br_sb"], "nt", BF16, "branch_sb_dx")
    do_mem = _mm(dp2, w["w_br_mem"], "nt", BF16, "branch_mem_dx")
    g["w_br_fox"] = _mm(o_fox, dp0, "tn", BF16, "branch_fox_dw")
    g["w_br_sb"] = _mm(o_sb, dp1, "tn", BF16, "branch_sb_dw")
    g["w_br_mem"] = _mm(o_mem, dp2, "tn", BF16, "branch_mem_dw")
    split_mix = [_by_shard(name, g[name]) for name in REDUCE_MIX]

    (dqn, dkn, dfv, drs, dcs), moved = _fox_bwd(qn, kn, proj, fv, c_col, c_row, o_fox32, do_fox, lse, nf,
                                                carry=_Both(_Scatter(sums_ffn[1:]), _Swap(split_mix)))
    others_up, theirs_mix = moved[:1], moved[1:]
    sums_mix = _sibling_sums(REDUCE_MIX, split_mix, theirs_mix, core)
    dfq, g["g_q_fox"] = _headnorm_bwd(dqn, proj, fq, nf, w["g_q_fox"], "fox_qnorm_bwd")
    dfk, g["g_k_fox"] = _headnorm_bwd(dkn, proj, fk, nf, w["g_k_fox"], "fox_knorm_bwd")
    dc = jnp.pad((drs[:, :, 0] - dcs[:, 0, :]).T, ((0, 0), (0, LANES - nf)))
    df, g["b_forget"] = _forget_bwd(dc, f_logit, w["b_forget"])
    (dsq, dsk, dsv), others_mix = _sb_bwd(proj, sq, sk, sv, do_sb, nsb, carry=_Scatter(sums_ffn[:1] + sums_mix))
    others_ffn = others_mix[:1] + others_up
    others_mix = others_mix[1:]
    dqmn, dkmn, dvm = _mem_bwd(qmn, kmn, mkv, do_mem, nm)
    dmq, g["g_q_mem"] = _headnorm_bwd(dqmn, proj, mq, nm, w["g_q_mem"], "mem_qnorm_bwd")
    dkm, g["g_k_mem"] = _headnorm_bwd(dkmn, mkv, 0, nm, w["g_k_mem"], "mem_knorm_bwd")
    dmkv = jnp.concatenate([dkm, dvm.astype(BF16)], axis=1)
    g["w_mem_kv"] = _mm(memn, dmkv, "tn", BF16, "mem_kv_dw")
    dmemn = _mm(dmkv, w["w_mem_kv"], "nt", BF16, "mem_kv_dx")
    _, _, g["g_mem"] = _rms_bwd(dmemn, mem, rstd_m, w["g_mem"], None, "rms_mem_bwd")

    dproj = jnp.concatenate([dfq, dfk, dfv.astype(BF16), dsq.astype(BF16), dsk.astype(BF16), dsv.astype(BF16), dmq],
                            axis=1)
    dfb = df.astype(BF16)
    g["w_in_a"] = _mm(h, dproj, "tn", BF16, "proj_att_dw")
    g["w_in_f"] = _mm(h, dfb, "tn", BF16, "proj_forget_dw")
    split_in = [_by_shard(name, g[name]) for name in REDUCE_IN]
    dh, theirs_in = _mm(dgates[0], w[W_IN_GATES[0]], "nt", F32, W_IN_GATES[0] + "_dx", carry=_Swap(split_in))
    sums_in = _sibling_sums(REDUCE_IN, split_in, theirs_in, core)
    for name, dgate in zip(W_IN_GATES[1:], dgates[1:]):
        dh = _mm(dgate, w[name], "nt", F32, name + "_dx", residual=dh)
    dh, others_in = _mm(dproj, w["w_in_a"], "nt", F32, "proj_att_dx", residual=dh, carry=_Scatter(sums_in))
    dh = _mm(dfb, w["w_in_f"], "nt", F32, "proj_forget_dx", residual=dh)
    grad_x, _, g["g_mix"] = _rms_bwd(dh, x, rstd1, w["g_mix"], dx1, "rms_mix_bwd")

    names = REDUCE_FFN + REDUCE_MIX + REDUCE_IN
    finals = [_add_chips(own, theirs, chip_core, "add_chips_" + name)
              for name, own, theirs in zip(names, sums_ffn + sums_mix + sums_in, others_ffn + others_mix + others_in)]
    summed = {name: a.reshape(2 * a.shape[1], a.shape[2]) for name, a in zip(names, _join_halves(finals))}
    return loss, grad_x, g, summed


SMALL = ("g_mix", "b_forget", "g_q_fox", "g_k_fox", "g_mem", "g_q_mem", "g_k_mem", "b_gate", "g_ffn", "conv_w",
         "conv_b")
SMALL_SHARDED = ("b_gate", "conv_w")
PACK_ROWS = 8


def _pack(arrs):
    flat = jnp.concatenate([a.reshape(-1) for a in arrs])
    unit = PACK_ROWS * LANES
    flat = jnp.pad(flat, (0, -flat.shape[0] % unit))
    return flat.reshape(-1, LANES)


def _unpack(packed, shapes):
    flat = packed.reshape(-1)
    out, at = [], 0
    for s in shapes:
        n = 1
        for dim in s:
            n *= dim
        out.append(flat[at:at + n].reshape(s))
        at += n
    return out


def kernel(x, mem, g_mix, w_in, b_forget, g_q_fox, g_k_fox, g_mem, w_mem_kv, g_q_mem, g_k_mem, w_br_fox, w_br_sb, w_br_mem, b_gate, w_out, g_ffn, w_up, conv_w, conv_b, w_down, loss_target, m_g_mix, m_w_in, m_b_forget, m_g_q_fox, m_g_k_fox, m_g_mem, m_w_mem_kv, m_g_q_mem, m_g_k_mem, m_w_br_fox, m_w_br_sb, m_w_br_mem, m_b_gate, m_w_out, m_g_ffn, m_w_up, m_conv_w, m_conv_b, m_w_down, v_g_mix, v_w_in, v_b_forget, v_g_q_fox, v_g_k_fox, v_g_mem, v_w_mem_kv, v_g_q_mem, v_g_k_mem, v_w_br_fox, v_w_br_sb, v_w_br_mem, v_b_gate, v_w_out, v_g_ffn, v_w_up, v_conv_w, v_conv_b, v_w_down):
    given = dict(g_mix=g_mix, w_in=w_in, b_forget=b_forget, g_q_fox=g_q_fox, g_k_fox=g_k_fox, g_mem=g_mem,
                 w_mem_kv=w_mem_kv, g_q_mem=g_q_mem, g_k_mem=g_k_mem, w_br_fox=w_br_fox, w_br_sb=w_br_sb,
                 w_br_mem=w_br_mem, b_gate=b_gate, w_out=w_out, g_ffn=g_ffn, w_up=w_up, conv_w=conv_w, conv_b=conv_b,
                 w_down=w_down)
    m_in = dict(g_mix=m_g_mix, w_in=m_w_in, b_forget=m_b_forget, g_q_fox=m_g_q_fox, g_k_fox=m_g_k_fox, g_mem=m_g_mem,
                w_mem_kv=m_w_mem_kv, g_q_mem=m_g_q_mem, g_k_mem=m_g_k_mem, w_br_fox=m_w_br_fox, w_br_sb=m_w_br_sb,
                w_br_mem=m_w_br_mem, b_gate=m_b_gate, w_out=m_w_out, g_ffn=m_g_ffn, w_up=m_w_up, conv_w=m_conv_w,
                conv_b=m_conv_b, w_down=m_w_down)
    v_in = dict(g_mix=v_g_mix, w_in=v_w_in, b_forget=v_b_forget, g_q_fox=v_g_q_fox, g_k_fox=v_g_k_fox, g_mem=v_g_mem,
                w_mem_kv=v_w_mem_kv, g_q_mem=v_g_q_mem, g_k_mem=v_g_k_mem, w_br_fox=v_w_br_fox, w_br_sb=v_w_br_sb,
                w_br_mem=v_w_br_mem, b_gate=v_b_gate, w_out=v_w_out, g_ffn=v_g_ffn, w_up=v_w_up, conv_w=v_conv_w,
                conv_b=v_conv_b, w_down=v_w_down)
    layered = {k: a.ndim == 3 for k, a in given.items()}
    drop = lambda a: a[0] if a.ndim == 3 else a
    given = {k: drop(a) for k, a in given.items()}
    m_in = {k: drop(a) for k, a in m_in.items()}
    v_in = {k: drop(a) for k, a in v_in.items()}

    xi, yi, ci = lax.axis_index("x"), lax.axis_index("y"), lax.axis_index("c")
    chip = (2 * xi + yi).astype(jnp.int32)
    core_arr = ci.astype(jnp.int32).reshape(1)
    chip_core = jnp.stack([chip, ci.astype(jnp.int32)])

    nf = given["b_forget"].shape[1]
    cut = 3 * given["w_br_fox"].shape[0]

    d_model = given["w_out"].shape[1]
    gate0 = given["w_in"].shape[1] - len(W_IN_GATES) * d_model
    shard = {
        "w_in_a": jnp.concatenate([given["w_in"][:, :cut], given["w_in"][:, cut + nf:gate0]], axis=1).astype(BF16),
        "w_in_f": jnp.pad(given["w_in"][:, cut:cut + nf], ((0, 0), (0, LANES - nf))).astype(BF16),
    }
    for b, name in enumerate(W_IN_GATES):
        shard[name] = given["w_in"][:, gate0 + b * d_model:gate0 + (b + 1) * d_model].astype(BF16)
    for name in BIG:
        if name not in shard:
            shard[name] = given[name].astype(BF16)
    w = {}
    small_shapes = [given[name].shape for name in SMALL_SHARDED]
    small_parts = _gather_small(_pack([given[name] for name in SMALL_SHARDED]))[0::2]
    per_chip = [_unpack(small_parts[j], small_shapes) for j in range(N_CHIPS)]
    for k, name in enumerate(SMALL_SHARDED):
        w[name] = jnp.concatenate([per_chip[j][k] for j in range(N_CHIPS)], axis=1)
    for name in SMALL:
        if name not in SMALL_SHARDED:
            w[name] = given[name]
    w["b_forget"] = jnp.pad(given["b_forget"], ((0, 0), (0, LANES - nf)))

    loss, grad_x, g, summed = _local_step(x[0], mem[0], loss_target[0], w, shard, core_arr, chip_core)
    loss = lax.psum(loss, ("x", "y", "c"))
    grads = {name: summed[name] for name in BIG if name in given}
    grads["w_in"] = jnp.concatenate([summed["w_in_a"][:, :cut], summed["w_in_f"][:, :nf], summed["w_in_a"][:, cut:]]
                                    + [summed[name] for name in W_IN_GATES], axis=1)

    g["b_forget"] = g["b_forget"][:, :nf]
    small_full_shapes = [g[name].shape for name in SMALL]
    small_sum = _unpack(_sum_devices(_gather_small(_pack([g[name] for name in SMALL]))), small_full_shapes)
    for name, a in zip(SMALL, small_sum):
        if name in SMALL_SHARDED:
            width = given[name].shape[1]
            a = lax.dynamic_slice_in_dim(a, chip * width, width, axis=1)
        grads[name] = a

    delta, new_m, new_v = {}, {}, {}
    for name in WEIGHTS:
        if name not in SMALL:
            delta[name], new_m[name], new_v[name] = _adamw(given[name], grads[name], m_in[name], v_in[name],
                                                           "adamw_" + name)
    shapes = [given[name].shape for name in SMALL]
    packed = [_pack([src[name] for name in SMALL]) for src in (given, grads, m_in, v_in)]
    for dst, res in zip((delta, new_m, new_v), _adamw(*packed, "adamw_small")):
        for name, a in zip(SMALL, _unpack(res, shapes)):
            dst[name] = a

    out = [loss, grad_x[None]]
    for src in (grads, delta, new_m, new_v):
        out.extend(src[name][None] if layered[name] else src[name] for name in WEIGHTS)
    return tuple(out)
```

```python
import functools

import jax
import jax.numpy as jnp
from jax import lax
from jax.experimental import pallas as pl
from jax.experimental.pallas import tpu as pltpu

F32 = jnp.float32
BF16 = jnp.bfloat16

HEAD_DIM = 128
EPS = 1e-6
NEG_BIG = -1e30

ADAM_LR = 0.001
ADAM_B1 = 0.9
ADAM_B2 = 0.999
ADAM_EPS = 1e-08
ADAM_WD = 0.01
ADAM_STEP = 10

LANES = 128
BF16_SUBLANES = 16
VMEM_LIMIT_BYTES = 56 * 1024 * 1024
MM_TILE = 1024
MM_TILE_K = {"nn": 2048, "nt": 2048, "tn": 4096}
ATT_TILE = 256
ROW_TILE = 256
HEADNORM_ROWS = 512
MEM_ROWS = 512
COL_TILE = 512
ADAM_BLOCK_BYTES = 2 << 20

N_CHIPS = 4
N_DEV = 8
MESH = pl.DeviceIdType.MESH

IN_NAMES = ['x', 'mem', 'g_mix', 'w_in', 'b_forget', 'g_q_fox', 'g_k_fox', 'g_mem', 'w_mem_kv', 'g_q_mem', 'g_k_mem',
            'w_br_fox', 'w_br_sb', 'w_br_mem', 'b_gate', 'w_out', 'g_ffn', 'w_up', 'conv_w', 'conv_b', 'w_down']
WEIGHTS = IN_NAMES[2:]


def _tile(n, target):
    if n <= target:
        return n
    for t in range(target - target % LANES, LANES - 1, -LANES):
        if n % t == 0:
            return t
    return n


def _params(*sem):
    return pltpu.CompilerParams(dimension_semantics=sem, vmem_limit_bytes=VMEM_LIMIT_BYTES)


def _log_sigmoid(z):
    return jnp.minimum(z, 0.0) - jnp.log(1.0 + jnp.exp(-jnp.abs(z)))


def _split2(v):
    hi = v.astype(BF16)
    lo = (v - hi.astype(F32)).astype(BF16)
    return hi, lo


def _split3(v):
    hi = v.astype(BF16)
    r = v - hi.astype(F32)
    mid = r.astype(BF16)
    lo = (r - mid.astype(F32)).astype(BF16)
    return hi, mid, lo


def _dot(a, b):
    return lax.dot_general(a, b, (((1,), (0,)), ((), ())), preferred_element_type=F32)


def _dot_nt(a, b):
    return lax.dot_general(a, b, (((1,), (1,)), ((), ())), preferred_element_type=F32)


def _dot_tn(a, b):
    return lax.dot_general(a, b, (((0,), (0,)), ((), ())), preferred_element_type=F32)


ANY = pl.BlockSpec(memory_space=pl.ANY)


def _place():
    x, y, c = lax.axis_index("x"), lax.axis_index("y"), lax.axis_index("c")
    others = [(1 - x, y), (x, 1 - y), (1 - x, 1 - y)]
    return x, y, c, others


def _remote(src, dst, send_sem, recv_sem, to):
    return pltpu.make_async_remote_copy(src_ref=src, dst_ref=dst, send_sem=send_sem, recv_sem=recv_sem,
                                        device_id=to, device_id_type=MESH)


class _Gather:
    PER_SHARD = 7

    def __init__(self, shards):
        self.inputs = list(shards)
        n = len(shards) * self.PER_SHARD
        self.out_shapes = [jax.ShapeDtypeStruct((N_CHIPS,) + s.shape, s.dtype) for s in shards]
        self.scratch = [pltpu.SemaphoreType.DMA((n,)), pltpu.SemaphoreType.DMA((n,))]

    def _first(self, ins, outs, sems):
        send_sems, recv_sems = sems
        x, y, c, others = _place()
        me = 2 * x + y
        k = self.PER_SHARD
        copies = []
        for i in range(len(ins)):
            h = ins[i].shape[0] // 2
            mine = pl.ds(pl.multiple_of(c * h, BF16_SUBLANES), h)
            for j, (ox, oy) in enumerate(others):
                copies.append(_remote(ins[i].at[mine], outs[i].at[me, mine], send_sems.at[k * i + j],
                                      recv_sems.at[k * i + j], (ox, oy, c)))
            copies.append(_remote(ins[i], outs[i].at[me], send_sems.at[k * i + 6], recv_sems.at[k * i + 6],
                                  (x, y, 1 - c)))
        return copies

    def start(self, ins, outs, sems):
        for cp in self._first(ins, outs, sems):
            cp.start()

    def finish(self, ins, outs, sems):
        send_sems, recv_sems = sems
        x, y, c, others = _place()
        me = 2 * x + y
        sibling = (x, y, 1 - c)
        k = self.PER_SHARD
        passed = []
        for i in range(len(ins)):
            h = ins[i].shape[0] // 2
            mine = pl.ds(pl.multiple_of(c * h, BF16_SUBLANES), h)
            for j, (ox, oy) in enumerate(others):
                blk = outs[i].at[2 * ox + oy, mine]
                _remote(blk, blk, send_sems.at[k * i + j], recv_sems.at[k * i + j], (ox, oy, c)).wait_recv()
                cp = _remote(blk, blk, send_sems.at[k * i + 3 + j], recv_sems.at[k * i + 3 + j], sibling)
                cp.start()
                passed.append(cp)
        for i in range(len(ins)):
            h = ins[i].shape[0] // 2
            theirs = pl.ds(pl.multiple_of((1 - c) * h, BF16_SUBLANES), h)
            for j, (ox, oy) in enumerate(others):
                blk = outs[i].at[2 * ox + oy, theirs]
                _remote(blk, blk, send_sems.at[k * i + 3 + j], recv_sems.at[k * i + 3 + j], sibling).wait_recv()
            own = outs[i].at[me]
            _remote(own, own, send_sems.at[k * i + 6], recv_sems.at[k * i + 6], sibling).wait_recv()
        for cp in self._first(ins, outs, sems) + passed:
            cp.wait_send()


class _Scatter:
    def __init__(self, sums):
        self.inputs = list(sums)
        k = N_CHIPS - 1
        self.out_shapes = [jax.ShapeDtypeStruct((k,) + g.shape[1:], g.dtype) for g in sums]
        self.scratch = [pltpu.SemaphoreType.DMA((k * len(sums),)), pltpu.SemaphoreType.DMA((k * len(sums),))]

    def _copies(self, ins, outs, sems):
        send_sems, recv_sems = sems
        _, _, c, others = _place()
        k = N_CHIPS - 1
        return [_remote(ins[i].at[2 * ox + oy], outs[i].at[j], send_sems.at[k * i + j], recv_sems.at[k * i + j],
                        (ox, oy, c))
                for i in range(len(ins)) for j, (ox, oy) in enumerate(others)]

    def start(self, ins, outs, sems):
        for cp in self._copies(ins, outs, sems):
            cp.start()

    def finish(self, ins, outs, sems):
        for cp in self._copies(ins, outs, sems):
            cp.wait()


class _Swap:
    def __init__(self, grads):
        self.inputs = list(grads)
        n = len(grads)
        self.out_shapes = [jax.ShapeDtypeStruct((g.shape[0],) + g.shape[2:], g.dtype) for g in grads]
        self.scratch = [pltpu.SemaphoreType.DMA((n,)), pltpu.SemaphoreType.DMA((n,))]

    def _copies(self, ins, outs, sems):
        send_sems, recv_sems = sems
        x, y, c, _ = _place()
        return [_remote(ins[i].at[:, 1 - c], outs[i], send_sems.at[i], recv_sems.at[i], (x, y, 1 - c))
                for i in range(len(ins))]

    def start(self, ins, outs, sems):
        for cp in self._copies(ins, outs, sems):
            cp.start()

    def finish(self, ins, outs, sems):
        for cp in self._copies(ins, outs, sems):
            cp.wait()


class _Both:
    def __init__(self, first, second):
        self.parts = (first, second)
        self.inputs = first.inputs + second.inputs
        self.out_shapes = first.out_shapes + second.out_shapes
        self.scratch = first.scratch + second.scratch

    def _each(self, ins, outs, sems):
        first = self.parts[0]
        a, b, c = len(first.inputs), len(first.out_shapes), len(first.scratch)
        return ((first, ins[:a], outs[:b], sems[:c]), (self.parts[1], ins[a:], outs[b:], sems[c:]))

    def start(self, ins, outs, sems):
        for part, i, o, s in self._each(ins, outs, sems):
            part.start(i, o, s)

    def finish(self, ins, outs, sems):
        for part, i, o, s in self._each(ins, outs, sems):
            part.finish(i, o, s)


def _call(body, *, grid, in_specs, out_specs, out_shape, scratch_shapes, semantics, name, args, carry=None):
    n_in, n_out, n_scr = len(in_specs), len(out_specs), len(scratch_shapes)
    if carry is None:
        res = pl.pallas_call(body, grid=grid, in_specs=in_specs, out_specs=out_specs, out_shape=out_shape,
                             scratch_shapes=scratch_shapes, compiler_params=_params(*semantics), name=name)(*args)
        return list(res), []
    nci, nco = len(carry.inputs), len(carry.out_shapes)
    a, b = n_in, n_in + nci
    c, d = b + n_out, b + n_out + nco
    e = d + n_scr

    def carried(*refs):
        ids = [pl.program_id(k) for k in range(len(grid))]
        first = functools.reduce(jnp.logical_and, [i == 0 for i in ids])
        last = functools.reduce(jnp.logical_and, [i == n - 1 for i, n in zip(ids, grid)])

        @pl.when(first)
        def _():
            carry.start(refs[a:b], refs[c:d], refs[e:])

        body(*refs[:a], *refs[b:c], *refs[d:e])

        @pl.when(last)
        def _():
            carry.finish(refs[a:b], refs[c:d], refs[e:])

    res = pl.pallas_call(
        carried,
        grid=grid,
        in_specs=list(in_specs) + [ANY] * nci,
        out_specs=list(out_specs) + [ANY] * nco,
        out_shape=list(out_shape) + carry.out_shapes,
        scratch_shapes=list(scratch_shapes) + carry.scratch,
        compiler_params=_params(*(["arbitrary"] * len(grid))),
        name=name,
    )(*args, *carry.inputs)
    return list(res[:n_out]), list(res[n_out:])


def _mm(a, b, mode, out_dtype, name, residual=None, carry=None, loss_target=None):
    if mode == "nn":
        (m, k), (k2, n) = a.shape, b.shape
    elif mode == "nt":
        (m, k), (n, k2) = a.shape, b.shape
    else:
        (k, m), (k2, n) = a.shape, b.shape
    assert k == k2, (a.shape, b.shape, mode)
    has_res = residual is not None
    has_loss = loss_target is not None
    n_in = 2 + has_res + has_loss
    tm, tn, tk = _tile(m, MM_TILE), _tile(n, MM_TILE), _tile(k, MM_TILE_K[mode])
    nk = k // tk
    dot = {"nn": _dot, "nt": _dot_nt, "tn": _dot_tn}[mode]

    def body(*refs):
        a_ref, b_ref = refs[:2]
        r_ref = refs[2] if has_res else None
        t_ref = refs[n_in - 1] if has_loss else None
        o_ref = refs[n_in]

        def finish(acc):
            if has_res:
                acc = acc + r_ref[...]
            if has_loss:
                err = acc - t_ref[...]
                dy = err * (1.0 / n)
                o_ref[...] = dy
                refs[n_in + 1][...] = dy.astype(BF16)
                tot = jnp.sum(jnp.sum(err * err, axis=-1, keepdims=True), axis=0, keepdims=True)
                refs[n_in + 2][...] = jnp.broadcast_to(tot, (8, LANES))
            else:
                o_ref[...] = acc.astype(o_ref.dtype)

        part = dot(a_ref[...], b_ref[...])
        if nk == 1:
            finish(part)
        else:
            acc_ref = refs[-1]
            kk = pl.program_id(2)

            @pl.when(kk == 0)
            def _():
                acc_ref[...] = part

            @pl.when(kk > 0)
            def _():
                acc_ref[...] += part

            @pl.when(kk == nk - 1)
            def _():
                finish(acc_ref[...])

    if mode == "tn":
        a_spec = pl.BlockSpec((tk, tm), lambda j, i, kk: (kk, i))
    else:
        a_spec = pl.BlockSpec((tm, tk), lambda j, i, kk: (i, kk))
    if mode == "nt":
        b_spec = pl.BlockSpec((tn, tk), lambda j, i, kk: (j, kk))
    else:
        b_spec = pl.BlockSpec((tk, tn), lambda j, i, kk: (kk, j))
    o_spec = pl.BlockSpec((tm, tn), lambda j, i, kk: (i, j))
    in_specs = [a_spec, b_spec] + [o_spec] * (has_res + has_loss)
    args = (a, b) + ((residual,) if has_res else ()) + ((loss_target,) if has_loss else ())
    out_specs, out_shape = [o_spec], [jax.ShapeDtypeStruct((m, n), out_dtype)]
    if has_loss:
        out_specs += [o_spec, pl.BlockSpec((8, LANES), lambda j, i, kk: (i, j))]
        out_shape += [jax.ShapeDtypeStruct((m, n), BF16), jax.ShapeDtypeStruct((m // tm * 8, n // tn * LANES), F32)]
    outs, moved = _call(
        body,
        grid=(n // tn, m // tm, nk),
        in_specs=in_specs,
        out_specs=out_specs,
        out_shape=out_shape,
        scratch_shapes=[pltpu.VMEM((tm, tn), F32)] if nk > 1 else [],
        semantics=("parallel", "parallel", "arbitrary"),
        name=name,
        args=args,
        carry=carry,
    )
    out = outs if has_loss else outs[0]
    return out if carry is None else (out, moved)


def _rms_fwd(x, g, name, carry=None):
    s, d = x.shape
    tm = _tile(s, ROW_TILE)

    def body(x_ref, g_ref, h_ref, r_ref):
        xf = x_ref[...]
        r = lax.rsqrt(jnp.mean(xf * xf, axis=-1, keepdims=True) + EPS)
        h_ref[...] = ((xf * r) * g_ref[...]).astype(BF16)
        r_ref[...] = r

    (h, rstd), moved = _call(
        body,
        grid=(s // tm,),
        in_specs=[pl.BlockSpec((tm, d), lambda i: (i, 0)), pl.BlockSpec((1, d), lambda i: (0, 0))],
        out_specs=[pl.BlockSpec((tm, d), lambda i: (i, 0)), pl.BlockSpec((tm, 1), lambda i: (i, 0))],
        out_shape=[jax.ShapeDtypeStruct((s, d), BF16), jax.ShapeDtypeStruct((s, 1), F32)],
        scratch_shapes=[],
        semantics=("parallel",),
        name=name,
        args=(x, g),
        carry=carry,
    )
    return (h, rstd) if carry is None else (h, rstd, moved)


def _rms_bwd(dh, x, rstd, g, res, name):
    s, d = x.shape
    tm = _tile(s, ROW_TILE)
    has_res = res is not None

    def body(*refs):
        if has_res:
            dh_ref, x_ref, r_ref, g_ref, res_ref, dx_ref, dxb_ref, dg_ref = refs
        else:
            dh_ref, x_ref, r_ref, g_ref, dx_ref, dxb_ref, dg_ref = refs
        dhf = dh_ref[...].astype(F32)
        xhat = x_ref[...] * r_ref[...]
        dy = dhf * g_ref[...]
        dx = r_ref[...] * (dy - xhat * jnp.mean(dy * xhat, axis=-1, keepdims=True))
        if has_res:
            dx = dx + res_ref[...]
        dx_ref[...] = dx
        dxb_ref[...] = dx.astype(BF16)
        part = jnp.sum(dhf * xhat, axis=0, keepdims=True)

        @pl.when(pl.program_id(0) == 0)
        def _():
            dg_ref[...] = part

        @pl.when(pl.program_id(0) > 0)
        def _():
            dg_ref[...] += part

    row = pl.BlockSpec((tm, d), lambda i: (i, 0))
    vec = pl.BlockSpec((1, d), lambda i: (0, 0))
    in_specs = [row, row, pl.BlockSpec((tm, 1), lambda i: (i, 0)), vec] + ([row] if has_res else [])
    args = (dh, x, rstd, g) + ((res,) if has_res else ())
    return pl.pallas_call(
        body,
        grid=(s // tm,),
        in_specs=in_specs,
        out_specs=[row, row, vec],
        out_shape=[jax.ShapeDtypeStruct((s, d), F32), jax.ShapeDtypeStruct((s, d), BF16),
                   jax.ShapeDtypeStruct((1, d), F32)],
        compiler_params=_params("arbitrary"),
        name=name,
    )(*args)


def _headnorm_fwd(src, col0, nheads, g, name):
    s = src.shape[0]
    tm = _tile(s, HEADNORM_ROWS)
    w = nheads * HEAD_DIM
    assert col0 % nheads == 0

    def body(x_ref, g_ref, o_ref):
        for hh in range(nheads):
            xf = _head(x_ref, hh).astype(F32)
            r = lax.rsqrt(jnp.mean(xf * xf, axis=-1, keepdims=True) + EPS)
            o_ref[:, hh * HEAD_DIM:(hh + 1) * HEAD_DIM] = ((xf * r) * g_ref[...]).astype(BF16)

    return pl.pallas_call(
        body,
        grid=(s // tm,),
        in_specs=[pl.BlockSpec((tm, w), lambda i: (i, col0 // nheads)),
                  pl.BlockSpec((1, HEAD_DIM), lambda i: (0, 0))],
        out_specs=pl.BlockSpec((tm, w), lambda i: (i, 0)),
        out_shape=jax.ShapeDtypeStruct((s, w), BF16),
        compiler_params=_params("parallel"),
        name=name,
    )(src, g)


def _headnorm_bwd(dxn, src, col0, nheads, g, name):
    s = src.shape[0]
    tm = _tile(s, HEADNORM_ROWS)
    w = nheads * HEAD_DIM
    assert col0 % nheads == 0

    def body(d_ref, x_ref, g_ref, dx_ref, dg_ref):
        part = jnp.zeros((1, HEAD_DIM), F32)
        for hh in range(nheads):
            xf = _head(x_ref, hh).astype(F32)
            r = lax.rsqrt(jnp.mean(xf * xf, axis=-1, keepdims=True) + EPS)
            xhat = xf * r
            dn = _head(d_ref, hh).astype(F32)
            dy = dn * g_ref[...]
            dx = r * (dy - xhat * jnp.mean(dy * xhat, axis=-1, keepdims=True))
            dx_ref[:, hh * HEAD_DIM:(hh + 1) * HEAD_DIM] = dx.astype(BF16)
            part = part + jnp.sum(dn * xhat, axis=0, keepdims=True)

        @pl.when(pl.program_id(0) == 0)
        def _():
            dg_ref[...] = part

        @pl.when(pl.program_id(0) > 0)
        def _():
            dg_ref[...] += part

    return pl.pallas_call(
        body,
        grid=(s // tm,),
        in_specs=[pl.BlockSpec((tm, w), lambda i: (i, 0)),
                  pl.BlockSpec((tm, w), lambda i: (i, col0 // nheads)),
                  pl.BlockSpec((1, HEAD_DIM), lambda i: (0, 0))],
        out_specs=[pl.BlockSpec((tm, w), lambda i: (i, 0)),
                   pl.BlockSpec((1, HEAD_DIM), lambda i: (0, 0))],
        out_shape=[jax.ShapeDtypeStruct((s, w), BF16), jax.ShapeDtypeStruct((1, HEAD_DIM), F32)],
        compiler_params=_params("arbitrary"),
        name=name,
    )(dxn, src, g)


def _tri(t, lower_inclusive):
    r = lax.broadcasted_iota(jnp.int32, (t, t), 0)
    c = lax.broadcasted_iota(jnp.int32, (t, t), 1)
    keep = (c <= r) if lower_inclusive else (c >= r)
    return jnp.where(keep, 1.0, 0.0).astype(BF16)


def _forget_fwd(f_logit, b_pad):
    s = f_logit.shape[0]
    t = _tile(s, ATT_TILE)

    def body(f_ref, b_ref, c_ref, carry):
        @pl.when(pl.program_id(0) == 0)
        def _():
            carry[...] = jnp.zeros_like(carry)

        lf = _log_sigmoid(f_ref[...] + b_ref[...])
        tri = _tri(t, True)
        acc = carry[...]
        for part in _split3(lf):
            acc = acc + _dot(tri, part)
        c_ref[...] = acc
        carry[...] += jnp.sum(lf, axis=0, keepdims=True)

    return pl.pallas_call(
        body,
        grid=(s // t,),
        in_specs=[pl.BlockSpec((t, LANES), lambda i: (i, 0)), pl.BlockSpec((1, LANES), lambda i: (0, 0))],
        out_specs=pl.BlockSpec((t, LANES), lambda i: (i, 0)),
        out_shape=jax.ShapeDtypeStruct((s, LANES), F32),
        scratch_shapes=[pltpu.VMEM((1, LANES), F32)],
        compiler_params=_params("arbitrary"),
        name="forget_fwd",
    )(f_logit, b_pad)


def _forget_bwd(dc, f_logit, b_pad):
    s = f_logit.shape[0]
    t = _tile(s, ATT_TILE)
    nb = s // t

    def body(dc_ref, f_ref, b_ref, df_ref, db_ref, carry):
        @pl.when(pl.program_id(0) == 0)
        def _():
            carry[...] = jnp.zeros_like(carry)
            db_ref[...] = jnp.zeros_like(db_ref)

        d = dc_ref[...]
        tri = _tri(t, False)
        acc = carry[...]
        for part in _split3(d):
            acc = acc + _dot(tri, part)
        z = f_ref[...] + b_ref[...]
        df = acc * jnp.exp(_log_sigmoid(-z))
        df_ref[...] = df
        db_ref[...] += jnp.sum(df, axis=0, keepdims=True)
        carry[...] += jnp.sum(d, axis=0, keepdims=True)

    rev = pl.BlockSpec((t, LANES), lambda i: (nb - 1 - i, 0))
    vec = pl.BlockSpec((1, LANES), lambda i: (0, 0))
    return pl.pallas_call(
        body,
        grid=(nb,),
        in_specs=[rev, rev, vec],
        out_specs=[rev, vec],
        out_shape=[jax.ShapeDtypeStruct((s, LANES), F32), jax.ShapeDtypeStruct((1, LANES), F32)],
        scratch_shapes=[pltpu.VMEM((1, LANES), F32)],
        compiler_params=_params("arbitrary"),
        name="forget_bwd",
    )(dc, f_logit, b_pad)


SB_FWD_GROUP = 6
FOX_GROUP = 6
SB_BWD_GROUP = 3


def _head(ref, hh, rows=slice(None)):
    return ref[rows, hh * HEAD_DIM:(hh + 1) * HEAD_DIM]


def _tri_mask(t, strict):
    r = lax.broadcasted_iota(jnp.int32, (t, t), 0)
    c = lax.broadcasted_iota(jnp.int32, (t, t), 1)
    return (c < r) if strict else (c <= r)


def _fox_fwd(qn, kn, proj, colv, c_col, c_row, nheads, carry=None):
    s = qn.shape[0]
    t = _tile(s, ATT_TILE)
    scale = HEAD_DIM ** -0.5
    hg = FOX_GROUP
    gw = hg * HEAD_DIM
    assert nheads % hg == 0 and colv % hg == 0

    def body(q_ref, k_ref, v_ref, cc_ref, cr_ref, o_ref, of_ref, lse_ref):
        qi = pl.program_id(1)
        causal = _tri_mask(t, False)

        def tile(kj, carry, diagonal):
            off = pl.multiple_of(kj * t, t)
            heads = range(hg)
            rows = pl.ds(off, t)
            qk = [_dot_nt(_head(q_ref, hh), _head(k_ref, hh, rows)) for hh in heads]
            sc = [qk[hh] * scale + (cc_ref[hh] - cr_ref[hh, :, rows]) for hh in heads]
            if diagonal:
                sc = [jnp.where(causal, sc[hh], NEG_BIG) for hh in heads]
            m_new = [jnp.maximum(carry[hh][0], jnp.max(sc[hh], axis=-1, keepdims=True)) for hh in heads]
            p = [jnp.exp(sc[hh] - m_new[hh]) for hh in heads]
            pv = [_dot(p[hh].astype(BF16), _head(v_ref, hh, rows)) for hh in heads]
            out = []
            for hh in heads:
                m, l, acc = carry[hh]
                alpha = jnp.exp(m - m_new[hh])
                out.append((m_new[hh], alpha * l + jnp.sum(p[hh], axis=-1, keepdims=True), alpha * acc + pv[hh]))
            return tuple(out)

        init = tuple((jnp.full((t, 1), NEG_BIG, F32), jnp.zeros((t, 1), F32), jnp.zeros((t, HEAD_DIM), F32))
                     for _ in range(hg))
        carry = lax.fori_loop(0, qi, lambda kj, c: tile(kj, c, False), init)
        carry = tile(qi, carry, True)
        for hh in range(hg):
            m, l, acc = carry[hh]
            o = acc / l
            of_ref[:, hh * HEAD_DIM:(hh + 1) * HEAD_DIM] = o
            o_ref[:, hh * HEAD_DIM:(hh + 1) * HEAD_DIM] = o.astype(BF16)
            lse_ref[hh] = m + jnp.log(l)

    tile_spec = pl.BlockSpec((t, gw), lambda h, i: (i, h))
    w = nheads * HEAD_DIM
    return _call(
        body,
        grid=(nheads // hg, s // t),
        in_specs=[tile_spec,
                  pl.BlockSpec((s, gw), lambda h, i: (0, h), pipeline_mode=pl.Buffered(buffer_count=1)),
                  pl.BlockSpec((s, gw), lambda h, i: (0, colv // hg + h), pipeline_mode=pl.Buffered(buffer_count=1)),
                  pl.BlockSpec((hg, t, 1), lambda h, i: (h, i, 0)),
                  pl.BlockSpec((hg, 1, s), lambda h, i: (h, 0, 0))],
        out_specs=[tile_spec, tile_spec, pl.BlockSpec((hg, t, 1), lambda h, i: (h, i, 0))],
        out_shape=[jax.ShapeDtypeStruct((s, w), BF16), jax.ShapeDtypeStruct((s, w), F32),
                   jax.ShapeDtypeStruct((nheads, s, 1), F32)],
        scratch_shapes=[],
        semantics=("parallel", "parallel"),
        name="fox_fwd",
        args=(qn, kn, proj, c_col, c_row),
        carry=carry,
    )


def _fox_bwd(qn, kn, proj, colv, c_col, c_row, o, do, lse, nheads, carry=None):
    s = qn.shape[0]
    t = _tile(s, ATT_TILE)
    scale = HEAD_DIM ** -0.5
    hg = FOX_GROUP
    gw = hg * HEAD_DIM
    assert nheads % hg == 0 and colv % hg == 0

    def body(q_ref, k_ref, v_ref, cc_ref, cr_ref, o_ref, do_ref, lse_ref,
             dq_ref, dk_ref, dv_ref, drs_ref, dcs_ref):
        qi = pl.program_id(1)

        @pl.when(qi == 0)
        def _():
            dk_ref[...] = jnp.zeros_like(dk_ref)
            dv_ref[...] = jnp.zeros_like(dv_ref)
            dcs_ref[...] = jnp.zeros_like(dcs_ref)

        causal = _tri_mask(t, False)
        delta = [jnp.sum(_head(o_ref, hh) * _head(do_ref, hh).astype(F32), axis=-1, keepdims=True)
                 for hh in range(hg)]

        def tile(kj, carry, diagonal):
            off = pl.multiple_of(kj * t, t)
            heads = range(hg)
            rows = pl.ds(off, t)
            qk = [_dot_nt(_head(q_ref, hh), _head(k_ref, hh, rows)) for hh in heads]
            dp = [_dot_nt(_head(do_ref, hh), _head(v_ref, hh, rows)) for hh in heads]
            p = [jnp.exp(qk[hh] * scale + (cc_ref[hh] - cr_ref[hh, :, rows]) - lse_ref[hh]) for hh in heads]
            if diagonal:
                p = [jnp.where(causal, p[hh], 0.0) for hh in heads]
            ds = [p[hh] * (dp[hh] - delta[hh]) for hh in heads]
            dsb = [ds[hh].astype(BF16) for hh in heads]
            dv = [_dot_tn(p[hh].astype(BF16), _head(do_ref, hh)) for hh in heads]
            dk = [_dot_tn(dsb[hh], _head(q_ref, hh)) * scale for hh in heads]
            dq = [_dot(dsb[hh], _head(k_ref, hh, rows)) * scale for hh in heads]
            for hh in heads:
                cols = slice(hh * HEAD_DIM, (hh + 1) * HEAD_DIM)
                dv_ref[rows, cols] += dv[hh]
                dk_ref[rows, cols] += dk[hh]
                dcs_ref[hh, :, rows] += jnp.sum(ds[hh], axis=0, keepdims=True)
            return tuple((carry[hh][0] + dq[hh], carry[hh][1] + jnp.sum(ds[hh], axis=-1, keepdims=True))
                         for hh in heads)

        init = tuple((jnp.zeros((t, HEAD_DIM), F32), jnp.zeros((t, 1), F32)) for _ in range(hg))
        carry = lax.fori_loop(0, qi, lambda kj, c: tile(kj, c, False), init)
        carry = tile(qi, carry, True)
        for hh in range(hg):
            dq_ref[:, hh * HEAD_DIM:(hh + 1) * HEAD_DIM] = carry[hh][0]
            drs_ref[hh] = carry[hh][1]

    tile_spec = pl.BlockSpec((t, gw), lambda h, i: (i, h))
    once = pl.Buffered(buffer_count=1)
    full = pl.BlockSpec((s, gw), lambda h, i: (0, h), pipeline_mode=once)
    colspec = pl.BlockSpec((hg, t, 1), lambda h, i: (h, i, 0))
    rowspec = pl.BlockSpec((hg, 1, s), lambda h, i: (h, 0, 0))
    w = nheads * HEAD_DIM
    return _call(
        body,
        grid=(nheads // hg, s // t),
        in_specs=[tile_spec, full, pl.BlockSpec((s, gw), lambda h, i: (0, colv // hg + h), pipeline_mode=once),
                  colspec, rowspec,
                  tile_spec, tile_spec, colspec],
        out_specs=[tile_spec, full, full, colspec, rowspec],
        out_shape=[jax.ShapeDtypeStruct((s, w), F32), jax.ShapeDtypeStruct((s, w), F32),
                   jax.ShapeDtypeStruct((s, w), F32), jax.ShapeDtypeStruct((nheads, s, 1), F32),
                   jax.ShapeDtypeStruct((nheads, 1, s), F32)],
        scratch_shapes=[],
        semantics=("arbitrary", "arbitrary"),
        name="fox_bwd",
        args=(qn, kn, proj, c_col, c_row, o, do, lse),
        carry=carry,
    )


def _sb_tile(q, k, scale, later, valid):
    z = _dot_nt(q, k) * scale
    lb = _log_sigmoid(z)
    lm = lb - z
    if valid is not None:
        lm = jnp.where(valid, lm, 0.0)
    suffix = _dot(jnp.concatenate(_split2(lm), axis=1), later)
    return lb, lm, suffix


def _later(t):
    r = lax.broadcasted_iota(jnp.int32, (2 * t, t), 0) % t
    c = lax.broadcasted_iota(jnp.int32, (2 * t, t), 1)
    return jnp.where(r > c, 1.0, 0.0).astype(BF16)


def _sb_fwd(proj, colq, colk, colv, nheads, carry=None):
    s = proj.shape[0]
    t = _tile(s, ATT_TILE)
    scale = HEAD_DIM ** -0.5
    hg = SB_FWD_GROUP
    gw = hg * HEAD_DIM
    assert nheads % hg == 0 and colq % hg == 0 and colk % hg == 0 and colv % hg == 0

    def body(q_ref, k_ref, v_ref, o_ref):
        qi = pl.program_id(1)
        later = _later(t)
        before = _tri_mask(t, True)

        def tile(kj, carry, diagonal):
            off = pl.multiple_of(kj * t, t)
            heads = range(hg)
            z = [_dot_nt(_head(q_ref, hh), _head(k_ref, hh, pl.ds(off, t))) * scale for hh in heads]
            lb = [_log_sigmoid(z[hh]) for hh in heads]
            lm = [lb[hh] - z[hh] for hh in heads]
            if diagonal:
                lm = [jnp.where(before, lm[hh], 0.0) for hh in heads]
            parts = [jnp.concatenate(_split2(lm[hh]), axis=1) for hh in heads]
            suffix = [_dot(parts[hh], later) for hh in heads]
            a = [jnp.exp(lb[hh] + suffix[hh] + carry[hh][0]) for hh in heads]
            if diagonal:
                a = [jnp.where(before, a[hh], 0.0) for hh in heads]
            av = [_dot(a[hh].astype(BF16), _head(v_ref, hh, pl.ds(off, t))) for hh in heads]
            return tuple((carry[hh][0] + jnp.sum(lm[hh], axis=-1, keepdims=True), carry[hh][1] + av[hh])
                         for hh in heads)

        init = tuple((jnp.zeros((t, 1), F32), jnp.zeros((t, HEAD_DIM), F32)) for _ in range(hg))
        carry = tile(qi, init, True)
        carry = lax.fori_loop(1, qi + 1, lambda i, c: tile(qi - i, c, False), carry)
        for hh in range(hg):
            o_ref[:, hh * HEAD_DIM:(hh + 1) * HEAD_DIM] = carry[hh][1].astype(BF16)

    return _call(
        body,
        grid=(nheads // hg, s // t),
        in_specs=[pl.BlockSpec((t, gw), lambda h, i: (i, colq // hg + h)),
                  pl.BlockSpec((s, gw), lambda h, i: (0, colk // hg + h), pipeline_mode=pl.Buffered(buffer_count=1)),
                  pl.BlockSpec((s, gw), lambda h, i: (0, colv // hg + h), pipeline_mode=pl.Buffered(buffer_count=1))],
        out_specs=[pl.BlockSpec((t, gw), lambda h, i: (i, h))],
        out_shape=[jax.ShapeDtypeStruct((s, nheads * HEAD_DIM), BF16)],
        scratch_shapes=[],
        semantics=("parallel", "parallel"),
        name="sb_fwd",
        args=(proj, proj, proj),
        carry=carry,
    )


def _sb_bwd(proj, colq, colk, colv, do, nheads, carry=None):
    s = proj.shape[0]
    t = _tile(s, ATT_TILE)
    scale = HEAD_DIM ** -0.5
    hg = SB_BWD_GROUP
    gw = hg * HEAD_DIM
    assert nheads % hg == 0 and colq % hg == 0 and colk % hg == 0 and colv % hg == 0

    def body(q_ref, k_ref, v_ref, do_ref, dq_ref, dk_ref, dv_ref, g_s, beta_s):
        qi = pl.program_id(1)

        @pl.when(qi == 0)
        def _():
            dk_ref[...] = jnp.zeros_like(dk_ref)
            dv_ref[...] = jnp.zeros_like(dv_ref)

        later = _later(t)
        before = _tri_mask(t, True)

        def back(kj, carry, diagonal):
            off = pl.multiple_of(kj * t, t)
            heads = range(hg)
            rows = pl.ds(off, t)
            z = [_dot_nt(_head(q_ref, hh), _head(k_ref, hh, rows)) * scale for hh in heads]
            da = [_dot_nt(_head(do_ref, hh), _head(v_ref, hh, rows)) for hh in heads]
            lb = [_log_sigmoid(z[hh]) for hh in heads]
            lm = [lb[hh] - z[hh] for hh in heads]
            if diagonal:
                lm = [jnp.where(before, lm[hh], 0.0) for hh in heads]
            parts = [jnp.concatenate(_split2(lm[hh]), axis=1) for hh in heads]
            suffix = [_dot(parts[hh], later) for hh in heads]
            a = [jnp.exp(lb[hh] + suffix[hh] + carry[hh]) for hh in heads]
            if diagonal:
                a = [jnp.where(before, a[hh], 0.0) for hh in heads]
            dv = [_dot_tn(a[hh].astype(BF16), _head(do_ref, hh)) for hh in heads]
            for hh in heads:
                g_s[hh, :, rows] = a[hh] * da[hh]
                beta_s[hh, :, rows] = jnp.exp(lb[hh]).astype(BF16)
            for hh in heads:
                dv_ref[rows, hh * HEAD_DIM:(hh + 1) * HEAD_DIM] += dv[hh]
            return tuple(carry[hh] + jnp.sum(lm[hh], axis=-1, keepdims=True) for hh in heads)

        rc = back(qi, tuple(jnp.zeros((t, 1), F32) for _ in range(hg)), True)
        lax.fori_loop(1, qi + 1, lambda i, c: back(qi - i, c, False), rc)

        earlier = jnp.where(lax.broadcasted_iota(jnp.int32, (2 * t, t), 0) % t
                            < lax.broadcasted_iota(jnp.int32, (2 * t, t), 1), 1.0, 0.0).astype(BF16)

        def fwd(kj, carry, diagonal):
            off = pl.multiple_of(kj * t, t)
            heads = range(hg)
            rows = pl.ds(off, t)
            g = [g_s[hh, :, rows] for hh in heads]
            parts = [jnp.concatenate(_split2(g[hh]), axis=1) for hh in heads]
            gsum = [_dot(parts[hh], earlier) + carry[hh][0] for hh in heads]
            dz = []
            for hh in heads:
                beta = beta_s[hh, :, rows].astype(F32)
                d = g[hh] * (1.0 - beta) - gsum[hh] * beta
                if diagonal:
                    d = jnp.where(before, d, 0.0)
                dz.append(d.astype(BF16))
            dk = [_dot_tn(dz[hh], _head(q_ref, hh)) * scale for hh in heads]
            dq = [_dot(dz[hh], _head(k_ref, hh, rows)) * scale for hh in heads]
            for hh in heads:
                dk_ref[rows, hh * HEAD_DIM:(hh + 1) * HEAD_DIM] += dk[hh]
            return tuple((carry[hh][0] + jnp.sum(g[hh], axis=-1, keepdims=True), carry[hh][1] + dq[hh])
                         for hh in heads)

        init = tuple((jnp.zeros((t, 1), F32), jnp.zeros((t, HEAD_DIM), F32)) for _ in range(hg))
        carry = lax.fori_loop(0, qi, lambda kj, c: fwd(kj, c, False), init)
        carry = fwd(qi, carry, True)
        for hh in range(hg):
            dq_ref[:, hh * HEAD_DIM:(hh + 1) * HEAD_DIM] = carry[hh][1]

    once = pl.Buffered(buffer_count=1)
    tile_spec = pl.BlockSpec((t, gw), lambda h, i: (i, h))
    full = pl.BlockSpec((s, gw), lambda h, i: (0, h), pipeline_mode=once)
    w = nheads * HEAD_DIM
    return _call(
        body,
        grid=(nheads // hg, s // t),
        in_specs=[pl.BlockSpec((t, gw), lambda h, i: (i, colq // hg + h)),
                  pl.BlockSpec((s, gw), lambda h, i: (0, colk // hg + h), pipeline_mode=once),
                  pl.BlockSpec((s, gw), lambda h, i: (0, colv // hg + h), pipeline_mode=once),
                  tile_spec],
        out_specs=[tile_spec, full, full],
        out_shape=[jax.ShapeDtypeStruct((s, w), F32)] * 3,
        scratch_shapes=[pltpu.VMEM((hg, t, s), F32), pltpu.VMEM((hg, t, s), BF16)],
        semantics=("arbitrary", "arbitrary"),
        name="sb_bwd",
        args=(proj, proj, proj, do),
        carry=carry,
    )


def _mem_fwd(qn, kn, mkv, nheads):
    s = qn.shape[0]
    mtok = kn.shape[0]
    t = _tile(s, MEM_ROWS)
    w = nheads * HEAD_DIM
    scale = HEAD_DIM ** -0.5

    def body(q_ref, k_ref, v_ref, o_ref):
        heads = range(nheads)
        sc = [_dot_nt(_head(q_ref, hh), _head(k_ref, hh)) * scale for hh in heads]
        p = [jnp.exp(sc[hh] - jnp.max(sc[hh], axis=-1, keepdims=True)) for hh in heads]
        p = [p[hh] / jnp.sum(p[hh], axis=-1, keepdims=True) for hh in heads]
        o = [_dot(p[hh].astype(BF16), _head(v_ref, hh)) for hh in heads]
        for hh in heads:
            o_ref[:, hh * HEAD_DIM:(hh + 1) * HEAD_DIM] = o[hh].astype(BF16)

    return pl.pallas_call(
        body,
        grid=(s // t,),
        in_specs=[pl.BlockSpec((t, w), lambda i: (i, 0)),
                  pl.BlockSpec((mtok, w), lambda i: (0, 0)),
                  pl.BlockSpec((mtok, w), lambda i: (0, 1))],
        out_specs=pl.BlockSpec((t, w), lambda i: (i, 0)),
        out_shape=jax.ShapeDtypeStruct((s, w), BF16),
        compiler_params=_params("parallel"),
        name="mem_fwd",
    )(qn, kn, mkv)


def _mem_bwd(qn, kn, mkv, do, nheads):
    s = qn.shape[0]
    mtok = kn.shape[0]
    t = _tile(s, MEM_ROWS)
    w = nheads * HEAD_DIM
    scale = HEAD_DIM ** -0.5

    def body(q_ref, k_ref, v_ref, do_ref, dq_ref, dk_ref, dv_ref):
        @pl.when(pl.program_id(0) == 0)
        def _():
            dk_ref[...] = jnp.zeros_like(dk_ref)
            dv_ref[...] = jnp.zeros_like(dv_ref)

        heads = range(nheads)
        sc = [_dot_nt(_head(q_ref, hh), _head(k_ref, hh)) * scale for hh in heads]
        dp = [_dot_nt(_head(do_ref, hh), _head(v_ref, hh)) for hh in heads]
        p = [jnp.exp(sc[hh] - jnp.max(sc[hh], axis=-1, keepdims=True)) for hh in heads]
        p = [p[hh] / jnp.sum(p[hh], axis=-1, keepdims=True) for hh in heads]
        ds = [(p[hh] * (dp[hh] - jnp.sum(p[hh] * dp[hh], axis=-1, keepdims=True))).astype(BF16) for hh in heads]
        dq = [_dot(ds[hh], _head(k_ref, hh)) * scale for hh in heads]
        dk = [_dot_tn(ds[hh], _head(q_ref, hh)) * scale for hh in heads]
        dv = [_dot_tn(p[hh].astype(BF16), _head(do_ref, hh)) for hh in heads]
        for hh in heads:
            cols = slice(hh * HEAD_DIM, (hh + 1) * HEAD_DIM)
            dq_ref[:, cols] = dq[hh]
            dk_ref[:, cols] += dk[hh]
            dv_ref[:, cols] += dv[hh]

    tile = pl.BlockSpec((t, w), lambda i: (i, 0))
    kspec = pl.BlockSpec((mtok, w), lambda i: (0, 0))
    return pl.pallas_call(
        body,
        grid=(s // t,),
        in_specs=[tile, kspec, pl.BlockSpec((mtok, w), lambda i: (0, 1)), tile],
        out_specs=[tile, kspec, kspec],
        out_shape=[jax.ShapeDtypeStruct((s, w), F32), jax.ShapeDtypeStruct((mtok, w), F32),
                   jax.ShapeDtypeStruct((mtok, w), F32)],
        compiler_params=_params("arbitrary"),
        name="mem_bwd",
    )(qn, kn, mkv, do)


def _merge_fwd(p0, p1, p2, gates, b_gate):
    s, d = p0.shape
    tm, tn = _tile(s, ROW_TILE), _tile(d, COL_TILE)
    nj = d // tn

    def body(p0_ref, p1_ref, p2_ref, ga_ref, gb_ref, gc_ref, b_ref, o_ref):
        acc = jnp.zeros((tm, tn), F32)
        for b, (p_ref, g_ref) in enumerate(((p0_ref, ga_ref), (p1_ref, gb_ref), (p2_ref, gc_ref))):
            gate = jax.nn.sigmoid(g_ref[...].astype(F32) + b_ref[b:b + 1, :])
            acc = acc + gate * p_ref[...]
        o_ref[...] = acc.astype(BF16)

    blk = pl.BlockSpec((tm, tn), lambda i, j: (i, j))
    return pl.pallas_call(
        body,
        grid=(s // tm, nj),
        in_specs=[blk] * 6 + [pl.BlockSpec((3, tn), lambda i, j: (0, j))],
        out_specs=blk,
        out_shape=jax.ShapeDtypeStruct((s, d), BF16),
        compiler_params=_params("parallel", "parallel"),
        name="merge_fwd",
    )(p0, p1, p2, *gates, b_gate)


def _merge_bwd(dmerged, p0, p1, p2, gates, b_gate):
    s, d = p0.shape
    tm, tn = _tile(s, ROW_TILE), _tile(d, COL_TILE)
    nj = d // tn

    def body(dm_ref, p0_ref, p1_ref, p2_ref, ga_ref, gb_ref, gc_ref, b_ref,
             d0_ref, d1_ref, d2_ref, dga_ref, dgb_ref, dgc_ref, db_ref):
        dm = dm_ref[...].astype(F32)
        parts = []
        for b, (p_ref, g_ref, dp_ref, dg_ref) in enumerate(((p0_ref, ga_ref, d0_ref, dga_ref),
                                                            (p1_ref, gb_ref, d1_ref, dgb_ref),
                                                            (p2_ref, gc_ref, d2_ref, dgc_ref))):
            gate = jax.nn.sigmoid(g_ref[...].astype(F32) + b_ref[b:b + 1, :])
            dp_ref[...] = (dm * gate).astype(BF16)
            dgate = dm * p_ref[...] * gate * (1.0 - gate)
            dg_ref[...] = dgate.astype(BF16)
            parts.append(jnp.sum(dgate, axis=0, keepdims=True))
        part = jnp.concatenate(parts, axis=0)

        @pl.when(pl.program_id(1) == 0)
        def _():
            db_ref[...] = part

        @pl.when(pl.program_id(1) > 0)
        def _():
            db_ref[...] += part

    blk = pl.BlockSpec((tm, tn), lambda j, i: (i, j))
    bias = pl.BlockSpec((3, tn), lambda j, i: (0, j))
    return pl.pallas_call(
        body,
        grid=(nj, s // tm),
        in_specs=[blk] * 7 + [bias],
        out_specs=[blk] * 6 + [bias],
        out_shape=[jax.ShapeDtypeStruct((s, d), BF16)] * 6 + [jax.ShapeDtypeStruct((3, d), F32)],
        compiler_params=_params("parallel", "arbitrary"),
        name="merge_bwd",
    )(dmerged, p0, p1, p2, *gates, b_gate)


def _shift_down(v, n):
    rows = lax.broadcasted_iota(jnp.int32, v.shape, 0)
    return jnp.where(rows >= n, pltpu.roll(v, n, 0), 0.0)


def _shift_up(v, n):
    s = v.shape[0]
    rows = lax.broadcasted_iota(jnp.int32, v.shape, 0)
    return jnp.where(rows < s - n, pltpu.roll(v, s - n, 0), 0.0)


def _conv(v, w_ref, b_ref):
    taps = w_ref.shape[0]
    out = v * w_ref[taps - 1:taps, :] + b_ref[...]
    for n in range(1, taps):
        out = out + _shift_down(v, n) * w_ref[taps - 1 - n:taps - n, :]
    return out


def _conv_act_fwd(up, conv_w, conv_b):
    s, f2 = up.shape
    f = f2 // 2
    tn = LANES
    nj = f // tn
    taps = conv_w.shape[0]

    def body(ug_ref, uv_ref, wg_ref, wv_ref, bg_ref, bv_ref, o_ref):
        cg = _conv(ug_ref[...].astype(F32), wg_ref, bg_ref)
        cv = _conv(uv_ref[...].astype(F32), wv_ref, bv_ref)
        o_ref[...] = (cg * jax.nn.sigmoid(cg) * cv).astype(BF16)

    return pl.pallas_call(
        body,
        grid=(nj,),
        in_specs=[pl.BlockSpec((s, tn), lambda j: (0, j)), pl.BlockSpec((s, tn), lambda j: (0, nj + j)),
                  pl.BlockSpec((taps, tn), lambda j: (0, j)), pl.BlockSpec((taps, tn), lambda j: (0, nj + j)),
                  pl.BlockSpec((1, tn), lambda j: (0, j)), pl.BlockSpec((1, tn), lambda j: (0, nj + j))],
        out_specs=pl.BlockSpec((s, tn), lambda j: (0, j)),
        out_shape=jax.ShapeDtypeStruct((s, f), BF16),
        compiler_params=_params("parallel"),
        name="conv_act_fwd",
    )(up, up, conv_w, conv_w, conv_b, conv_b)


def _conv_act_bwd(up, conv_w, conv_b, dact):
    s, f2 = up.shape
    f = f2 // 2
    tn = LANES
    nj = f // tn
    taps = conv_w.shape[0]

    def half(v, du, w_ref, dup_ref, dw_ref, db_ref):
        dup = du * w_ref[taps - 1:taps, :]
        rows = [None] * taps
        rows[taps - 1] = jnp.sum(du * v, axis=0, keepdims=True)
        for n in range(1, taps):
            later = _shift_up(du, n)
            dup = dup + later * w_ref[taps - 1 - n:taps - n, :]
            rows[taps - 1 - n] = jnp.sum(later * v, axis=0, keepdims=True)
        dup_ref[...] = dup.astype(BF16)
        dw_ref[...] = jnp.concatenate(rows, axis=0)
        db_ref[...] = jnp.sum(du, axis=0, keepdims=True)

    def body(ug_ref, uv_ref, wg_ref, wv_ref, bg_ref, bv_ref, da_ref,
             dug_ref, duv_ref, dwg_ref, dwv_ref, dbg_ref, dbv_ref):
        ug = ug_ref[...].astype(F32)
        uv = uv_ref[...].astype(F32)
        cg = _conv(ug, wg_ref, bg_ref)
        cv = _conv(uv, wv_ref, bv_ref)
        da = da_ref[...].astype(F32)
        sg = jax.nn.sigmoid(cg)
        dcv = da * cg * sg
        dcg = da * cv * (sg + cg * sg * (1.0 - sg))
        half(ug, dcg, wg_ref, dug_ref, dwg_ref, dbg_ref)
        half(uv, dcv, wv_ref, duv_ref, dwv_ref, dbv_ref)

    lo = lambda rows: pl.BlockSpec((rows, tn), lambda j: (0, j))
    hi = lambda rows: pl.BlockSpec((rows, tn), lambda j: (0, nj + j))
    return pl.pallas_call(
        body,
        grid=(nj,),
        in_specs=[lo(s), hi(s), lo(taps), hi(taps), lo(1), hi(1), lo(s)],
        out_specs=[lo(s), lo(s), lo(taps), lo(taps), lo(1), lo(1)],
        out_shape=[jax.ShapeDtypeStruct((s, f), BF16)] * 2 + [jax.ShapeDtypeStruct((taps, f), F32)] * 2
        + [jax.ShapeDtypeStruct((1, f), F32)] * 2,
        compiler_params=_params("parallel"),
        name="conv_act_bwd",
    )(up, up, conv_w, conv_w, conv_b, conv_b, dact)


def _row_tile(rows, row_bytes, budget):
    if rows * row_bytes <= budget or rows % 8:
        return rows
    best = 8
    for t in range(8, rows, 8):
        if rows % t == 0 and t * row_bytes <= budget:
            best = t
    return best


def _adamw(w, g, m, v, name):
    r, c = w.shape
    tr = _row_tile(r, c * 4, ADAM_BLOCK_BYTES)

    def body(w_ref, g_ref, m_ref, v_ref, d_ref, mo_ref, vo_ref):
        gg = g_ref[...]
        m_new = ADAM_B1 * m_ref[...] + (1.0 - ADAM_B1) * gg
        v_new = ADAM_B2 * v_ref[...] + (1.0 - ADAM_B2) * (gg * gg)
        m_hat = m_new / (1.0 - ADAM_B1 ** ADAM_STEP)
        v_hat = v_new / (1.0 - ADAM_B2 ** ADAM_STEP)
        d_ref[...] = -ADAM_LR * (m_hat / (jnp.sqrt(v_hat) + ADAM_EPS) + ADAM_WD * w_ref[...])
        mo_ref[...] = m_new
        vo_ref[...] = v_new

    blk = pl.BlockSpec((tr, c), lambda i: (i, 0))
    return pl.pallas_call(
        body,
        grid=(r // tr,),
        in_specs=[blk] * 4,
        out_specs=[blk] * 3,
        out_shape=[jax.ShapeDtypeStruct((r, c), F32)] * 3,
        compiler_params=_params("parallel"),
        name=name,
    )(w, g, m, v)


def _add_sibling(g, r1, core, name):
    _, _, h, c = g.shape
    th = _row_tile(h, c * 2, ADAM_BLOCK_BYTES)

    def body(core_ref, g_ref, r_ref, o_ref):
        o_ref[...] = (g_ref[...].astype(F32) + r_ref[...].astype(F32)).astype(BF16)

    return pl.pallas_call(
        body,
        grid_spec=pltpu.PrefetchScalarGridSpec(
            num_scalar_prefetch=1,
            grid=(N_CHIPS, h // th),
            in_specs=[pl.BlockSpec((None, None, th, c), lambda j, i, core_ref: (j, core_ref[0], i, 0)),
                      pl.BlockSpec((None, th, c), lambda j, i, core_ref: (j, i, 0))],
            out_specs=pl.BlockSpec((None, th, c), lambda j, i, core_ref: (j, i, 0)),
        ),
        out_shape=jax.ShapeDtypeStruct((N_CHIPS, h, c), BF16),
        compiler_params=_params("parallel", "parallel"),
        name=name,
    )(core, g, r1)


def _add_chips(hsum, r2, chip_core, name):
    _, h, c = hsum.shape
    th = _row_tile(h, c * 4, ADAM_BLOCK_BYTES)

    def body(sel_ref, own_ref, r_ref, o_ref):
        acc = own_ref[...].astype(F32)
        for j in range(N_CHIPS - 1):
            acc = acc + r_ref[j].astype(F32)
        o_ref[...] = acc

    return pl.pallas_call(
        body,
        grid_spec=pltpu.PrefetchScalarGridSpec(
            num_scalar_prefetch=1,
            grid=(h // th,),
            in_specs=[pl.BlockSpec((None, th, c), lambda i, sel_ref: (sel_ref[0], i, 0)),
                      pl.BlockSpec((N_CHIPS - 1, th, c), lambda i, sel_ref: (0, i, 0))],
            out_specs=pl.BlockSpec((None, th, c), lambda i, sel_ref: (sel_ref[1], i, 0)),
        ),
        out_shape=jax.ShapeDtypeStruct((2, h, c), F32),
        compiler_params=_params("parallel"),
        name=name,
    )(chip_core, hsum, r2)


def _sum_devices(parts):
    _, r, c = parts.shape

    def body(p_ref, o_ref):
        acc = p_ref[0]
        for j in range(1, N_DEV):
            acc = acc + p_ref[j]
        o_ref[...] = acc

    return pl.pallas_call(
        body,
        out_shape=jax.ShapeDtypeStruct((r, c), F32),
        compiler_params=pltpu.CompilerParams(vmem_limit_bytes=VMEM_LIMIT_BYTES),
        name="sum_devices",
    )(parts)


def _join_halves(finals):
    n = len(finals)

    def body(*refs):
        outs = refs[n:2 * n]
        send_sems, recv_sems = refs[2 * n:]
        x, y, c, _ = _place()
        sends = [_remote(outs[i].at[c], outs[i].at[c], send_sems.at[i], recv_sems.at[i], (x, y, 1 - c))
                 for i in range(n)]
        for cp in sends:
            cp.start()
        for i in range(n):
            sends[i].wait_send()
            other = outs[i].at[1 - c]
            _remote(other, other, send_sems.at[i], recv_sems.at[i], (x, y, 1 - c)).wait_recv()

    return pl.pallas_call(
        body,
        in_specs=[ANY] * n,
        out_specs=[ANY] * n,
        out_shape=[jax.ShapeDtypeStruct(f.shape, f.dtype) for f in finals],
        input_output_aliases={i: i for i in range(n)},
        scratch_shapes=[pltpu.SemaphoreType.DMA((n,)), pltpu.SemaphoreType.DMA((n,))],
        name="join_halves",
    )(*finals)


def _gather_small(vec):
    k = N_DEV - 1

    def body(v_ref, o_ref, send_sems, recv_sems, local_sem):
        x, y, c, _ = _place()
        me = 4 * x + 2 * y + c
        local = pltpu.make_async_copy(v_ref, o_ref.at[me], local_sem)
        local.start()
        peers = [(x ^ (r >> 2 & 1), y ^ (r >> 1 & 1), c ^ (r & 1)) for r in range(1, N_DEV)]
        sends = [_remote(v_ref, o_ref.at[me], send_sems.at[j], recv_sems.at[j], p) for j, p in enumerate(peers)]
        for cp in sends:
            cp.start()
        for j, (px, py, pc) in enumerate(peers):
            sends[j].wait_send()
            blk = o_ref.at[4 * px + 2 * py + pc]
            _remote(blk, blk, send_sems.at[j], recv_sems.at[j], (px, py, pc)).wait_recv()
        local.wait()

    return pl.pallas_call(
        body,
        in_specs=[ANY],
        out_specs=ANY,
        out_shape=jax.ShapeDtypeStruct((N_DEV,) + vec.shape, vec.dtype),
        scratch_shapes=[pltpu.SemaphoreType.DMA((k,)), pltpu.SemaphoreType.DMA((k,)), pltpu.SemaphoreType.DMA(())],
        name="gather_small",
    )(vec)


W_IN_GATES = ("w_in_g0", "w_in_g1", "w_in_g2")
ROW_SHARDED = ("w_in_a", "w_in_f") + W_IN_GATES + ("w_mem_kv", "w_out", "w_down")
COL_SHARDED = ("w_br_fox", "w_br_sb", "w_br_mem", "w_up")
BIG = ROW_SHARDED + COL_SHARDED


def _whole(name, a):
    if name in ROW_SHARDED:
        return a.reshape(N_CHIPS * a.shape[1], a.shape[2])
    return a.transpose(1, 0, 2).reshape(a.shape[1], N_CHIPS * a.shape[2])


def _by_shard(name, grad):
    if name in ROW_SHARDED:
        a = grad.reshape(N_CHIPS, grad.shape[0] // N_CHIPS, grad.shape[1])
    else:
        a = grad.reshape(grad.shape[0], N_CHIPS, grad.shape[1] // N_CHIPS).transpose(1, 0, 2)
    return a.reshape(N_CHIPS, 2, a.shape[1] // 2, a.shape[2])


def _sibling_sums(names, split, theirs, core):
    return [_add_sibling(a, r, core, "add_sibling_" + name) for name, a, r in zip(names, split, theirs)]


GATHER_FIRST = ("w_in_a", "w_in_f")
GATHER_EARLY = W_IN_GATES[:2] + ("w_mem_kv",)
GATHER_MIX = ("w_out", "w_br_fox", "w_br_sb", "w_br_mem")
REDUCE_FFN = ("w_down", "w_up")
REDUCE_MIX = ("w_out", "w_br_fox", "w_br_sb", "w_br_mem") + W_IN_GATES
REDUCE_IN = ("w_in_a", "w_in_f", "w_mem_kv")


def _local_step(x, mem, target, w, shard, core, chip_core):
    d = x.shape[1]
    nf = shard["w_br_fox"].shape[0] // HEAD_DIM
    nsb = shard["w_br_sb"].shape[0] // HEAD_DIM
    nm = shard["w_br_mem"].shape[0] // HEAD_DIM
    w = dict(w)

    def take(names, gathered):
        for name, a in zip(names, gathered):
            w[name] = _whole(name, a)

    fq, fk, fv = 0, nf, 2 * nf
    sq, sk, sv = 3 * nf, 3 * nf + nsb, 3 * nf + 2 * nsb
    mq = 3 * nf + 3 * nsb

    h, rstd1, moved = _rms_fwd(x, w["g_mix"], "rms_mix_fwd", carry=_Gather([shard[name] for name in GATHER_FIRST]))
    take(GATHER_FIRST, moved)
    proj, moved = _mm(h, w["w_in_a"], "nn", BF16, "proj_att", carry=_Gather([shard[name] for name in GATHER_EARLY]))
    take(GATHER_EARLY, moved)
    f_logit = _mm(h, w["w_in_f"], "nn", F32, "proj_forget")
    gate0, moved = _mm(h, w["w_in_g0"], "nn", BF16, "proj_gate0", carry=_Gather([shard[W_IN_GATES[2]]]))
    take(W_IN_GATES[2:], moved)
    gates = (gate0, _mm(h, w["w_in_g1"], "nn", BF16, "proj_gate1"), _mm(h, w["w_in_g2"], "nn", BF16, "proj_gate2"))
    c_sum = _forget_fwd(f_logit, w["b_forget"])
    c_t = c_sum[:, :nf].T
    c_col, c_row = c_t[:, :, None], c_t[:, None, :]
    qn = _headnorm_fwd(proj, fq, nf, w["g_q_fox"], "fox_qnorm_fwd")
    kn = _headnorm_fwd(proj, fk, nf, w["g_k_fox"], "fox_knorm_fwd")
    (o_fox, o_fox32, lse), moved = _fox_fwd(qn, kn, proj, fv, c_col, c_row, nf,
                                            carry=_Gather([shard[name] for name in GATHER_MIX]))
    take(GATHER_MIX, moved)
    (o_sb,), moved = _sb_fwd(proj, sq, sk, sv, nsb, carry=_Gather([shard["w_up"]]))
    take(("w_up",), moved)
    memn, rstd_m = _rms_fwd(mem, w["g_mem"], "rms_mem_fwd")
    mkv = _mm(memn, w["w_mem_kv"], "nn", BF16, "mem_kv")
    kmn = _headnorm_fwd(mkv, 0, nm, w["g_k_mem"], "mem_knorm_fwd")
    qmn = _headnorm_fwd(proj, mq, nm, w["g_q_mem"], "mem_qnorm_fwd")
    o_mem = _mem_fwd(qmn, kmn, mkv, nm)
    p0 = _mm(o_fox, w["w_br_fox"], "nn", F32, "branch_fox")
    p1 = _mm(o_sb, w["w_br_sb"], "nn", F32, "branch_sb")
    p2 = _mm(o_mem, w["w_br_mem"], "nn", F32, "branch_mem")
    merged = _merge_fwd(p0, p1, p2, gates, w["b_gate"])
    x1 = _mm(merged, w["w_out"], "nn", F32, "out_proj", residual=x)
    h2, rstd2 = _rms_fwd(x1, w["g_ffn"], "rms_ffn_fwd")
    up, moved = _mm(h2, w["w_up"], "nn", BF16, "ffn_up", carry=_Gather([shard["w_down"]]))
    take(("w_down",), moved)
    act =_conv_act_fwd(up, w["conv_w"], w["conv_b"])
    dy, dyb, lparts = _mm(act, w["w_down"], "nn", F32, "ffn_down_loss", residual=x1, loss_target=target)
    loss = (0.5 / d) * jnp.sum(lparts[::8, ::LANES])

    g = {}
    dact = _mm(dyb, w["w_down"], "nt", BF16, "ffn_down_dx")
    g["w_down"] = _mm(act, dyb, "tn", BF16, "ffn_down_dw")
    dug, duv, dwg, dwv, dbg, dbv = _conv_act_bwd(up, w["conv_w"], w["conv_b"], dact)
    dup = jnp.concatenate([dug, duv], axis=1)
    g["conv_w"] = jnp.concatenate([dwg, dwv], axis=1)
    g["conv_b"] = jnp.concatenate([dbg, dbv], axis=1)
    split_down = [_by_shard("w_down", g["w_down"])]
    dh2, theirs_down = _mm(dup, w["w_up"], "nt", BF16, "ffn_up_dx", carry=_Swap(split_down))
    g["w_up"] = _mm(h2, dup, "tn", BF16, "ffn_up_dw")
    split_up = [_by_shard("w_up", g["w_up"])]
    dx1, dx1b, g["g_ffn"] = _rms_bwd(dh2, x1, rstd2, w["g_ffn"], dy, "rms_ffn_bwd")
    dmerged, theirs_up = _mm(dx1b, w["w_out"], "nt", BF16, "out_proj_dx", carry=_Swap(split_up))
    sums_ffn = _sibling_sums(REDUCE_FFN, split_down + split_up, theirs_down + theirs_up, core)
    g["w_out"] = _mm(merged, dx1b, "tn", BF16, "out_proj_dw")
    dp0, dp1, dp2, dga, dgb, dgc, g["b_gate"] = _merge_bwd(dmerged, p0, p1, p2, gates, w["b_gate"])
    dgates = (dga, dgb, dgc)
    for name, dgate in zip(W_IN_GATES, dgates):
        g[name] = _mm(h, dgate, "tn", BF16, name + "_dw")
    do_fox = _mm(dp0, w["w_br_fox"], "nt", BF16, "branch_fox_dx")
    do_sb = _mm(dp1, w["w_br_sb"], "nt", BF16, "branch_sb_dx")
    do_mem = _mm(dp2, w["w_br_mem"], "nt", BF16, "branch_mem_dx")
    g["w_br_fox"] = _mm(o_fox, dp0, "tn", BF16, "branch_fox_dw")
    g["w_br_sb"] = _mm(o_sb, dp1, "tn", BF16, "branch_sb_dw")
    g["w_br_mem"] = _mm(o_mem, dp2, "tn", BF16, "branch_mem_dw")
    split_mix = [_by_shard(name, g[name]) for name in REDUCE_MIX]

    (dqn, dkn, dfv, drs, dcs), moved = _fox_bwd(qn, kn, proj, fv, c_col, c_row, o_fox32, do_fox, lse, nf,
                                                carry=_Both(_Scatter(sums_ffn[1:]), _Swap(split_mix)))
    others_up, theirs_mix = moved[:1], moved[1:]
    sums_mix = _sibling_sums(REDUCE_MIX, split_mix, theirs_mix, core)
    dfq, g["g_q_fox"] = _headnorm_bwd(dqn, proj, fq, nf, w["g_q_fox"], "fox_qnorm_bwd")
    dfk, g["g_k_fox"] = _headnorm_bwd(dkn, proj, fk, nf, w["g_k_fox"], "fox_knorm_bwd")
    dc = jnp.pad((drs[:, :, 0] - dcs[:, 0, :]).T, ((0, 0), (0, LANES - nf)))
    df, g["b_forget"] = _forget_bwd(dc, f_logit, w["b_forget"])
    (dsq, dsk, dsv), others_mix = _sb_bwd(proj, sq, sk, sv, do_sb, nsb, carry=_Scatter(sums_ffn[:1] + sums_mix))
    others_ffn = others_mix[:1] + others_up
    others_mix = others_mix[1:]
    dqmn, dkmn, dvm = _mem_bwd(qmn, kmn, mkv, do_mem, nm)
    dmq, g["g_q_mem"] = _headnorm_bwd(dqmn, proj, mq, nm, w["g_q_mem"], "mem_qnorm_bwd")
    dkm, g["g_k_mem"] = _headnorm_bwd(dkmn, mkv, 0, nm, w["g_k_mem"], "mem_knorm_bwd")
    dmkv = jnp.concatenate([dkm, dvm.astype(BF16)], axis=1)
    g["w_mem_kv"] = _mm(memn, dmkv, "tn", BF16, "mem_kv_dw")
    dmemn = _mm(dmkv, w["w_mem_kv"], "nt", BF16, "mem_kv_dx")
    _, _, g["g_mem"] = _rms_bwd(dmemn, mem, rstd_m, w["g_mem"], None, "rms_mem_bwd")

    dproj = jnp.concatenate([dfq, dfk, dfv.astype(BF16), dsq.astype(BF16), dsk.astype(BF16), dsv.astype(BF16), dmq],
                            axis=1)
    dfb = df.astype(BF16)
    g["w_in_a"] = _mm(h, dproj, "tn", BF16, "proj_att_dw")
    g["w_in_f"] = _mm(h, dfb, "tn", BF16, "proj_forget_dw")
    split_in = [_by_shard(name, g[name]) for name in REDUCE_IN]
    dh, theirs_in = _mm(dgates[0], w[W_IN_GATES[0]], "nt", F32, W_IN_GATES[0] + "_dx", carry=_Swap(split_in))
    sums_in = _sibling_sums(REDUCE_IN, split_in, theirs_in, core)
    for name, dgate in zip(W_IN_GATES[1:], dgates[1:]):
        dh = _mm(dgate, w[name], "nt", F32, name + "_dx", residual=dh)
    dh, others_in = _mm(dproj, w["w_in_a"], "nt", F32, "proj_att_dx", residual=dh, carry=_Scatter(sums_in))
    dh = _mm(dfb, w["w_in_f"], "nt", F32, "proj_forget_dx", residual=dh)
    grad_x, _, g["g_mix"] = _rms_bwd(dh, x, rstd1, w["g_mix"], dx1, "rms_mix_bwd")

    names = REDUCE_FFN + REDUCE_MIX + REDUCE_IN
    finals = [_add_chips(own, theirs, chip_core, "add_chips_" + name)
              for name, own, theirs in zip(names, sums_ffn + sums_mix + sums_in, others_ffn + others_mix + others_in)]
    summed = {name: a.reshape(2 * a.shape[1], a.shape[2]) for name, a in zip(names, _join_halves(finals))}
    return loss, grad_x, g, summed


SMALL = ("g_mix", "b_forget", "g_q_fox", "g_k_fox", "g_mem", "g_q_mem", "g_k_mem", "b_gate", "g_ffn", "conv_w",
         "conv_b")
SMALL_SHARDED = ("b_gate", "conv_w")
PACK_ROWS = 8


def _pack(arrs):
    flat = jnp.concatenate([a.reshape(-1) for a in arrs])
    unit = PACK_ROWS * LANES
    flat = jnp.pad(flat, (0, -flat.shape[0] % unit))
    return flat.reshape(-1, LANES)


def _unpack(packed, shapes):
    flat = packed.reshape(-1)
    out, at = [], 0
    for s in shapes:
        n = 1
        for dim in s:
            n *= dim
        out.append(flat[at:at + n].reshape(s))
        at += n
    return out


def kernel(x, mem, g_mix, w_in, b_forget, g_q_fox, g_k_fox, g_mem, w_mem_kv, g_q_mem, g_k_mem, w_br_fox, w_br_sb, w_br_mem, b_gate, w_out, g_ffn, w_up, conv_w, conv_b, w_down, loss_target, m_g_mix, m_w_in, m_b_forget, m_g_q_fox, m_g_k_fox, m_g_mem, m_w_mem_kv, m_g_q_mem, m_g_k_mem, m_w_br_fox, m_w_br_sb, m_w_br_mem, m_b_gate, m_w_out, m_g_ffn, m_w_up, m_conv_w, m_conv_b, m_w_down, v_g_mix, v_w_in, v_b_forget, v_g_q_fox, v_g_k_fox, v_g_mem, v_w_mem_kv, v_g_q_mem, v_g_k_mem, v_w_br_fox, v_w_br_sb, v_w_br_mem, v_b_gate, v_w_out, v_g_ffn, v_w_up, v_conv_w, v_conv_b, v_w_down):
    given = dict(g_mix=g_mix, w_in=w_in, b_forget=b_forget, g_q_fox=g_q_fox, g_k_fox=g_k_fox, g_mem=g_mem,
                 w_mem_kv=w_mem_kv, g_q_mem=g_q_mem, g_k_mem=g_k_mem, w_br_fox=w_br_fox, w_br_sb=w_br_sb,
                 w_br_mem=w_br_mem, b_gate=b_gate, w_out=w_out, g_ffn=g_ffn, w_up=w_up, conv_w=conv_w, conv_b=conv_b,
                 w_down=w_down)
    m_in = dict(g_mix=m_g_mix, w_in=m_w_in, b_forget=m_b_forget, g_q_fox=m_g_q_fox, g_k_fox=m_g_k_fox, g_mem=m_g_mem,
                w_mem_kv=m_w_mem_kv, g_q_mem=m_g_q_mem, g_k_mem=m_g_k_mem, w_br_fox=m_w_br_fox, w_br_sb=m_w_br_sb,
                w_br_mem=m_w_br_mem, b_gate=m_b_gate, w_out=m_w_out, g_ffn=m_g_ffn, w_up=m_w_up, conv_w=m_conv_w,
                conv_b=m_conv_b, w_down=m_w_down)
    v_in = dict(g_mix=v_g_mix, w_in=v_w_in, b_forget=v_b_forget, g_q_fox=v_g_q_fox, g_k_fox=v_g_k_fox, g_mem=v_g_mem,
                w_mem_kv=v_w_mem_kv, g_q_mem=v_g_q_mem, g_k_mem=v_g_k_mem, w_br_fox=v_w_br_fox, w_br_sb=v_w_br_sb,
                w_br_mem=v_w_br_mem, b_gate=v_b_gate, w_out=v_w_out, g_ffn=v_g_ffn, w_up=v_w_up, conv_w=v_conv_w,
                conv_b=v_conv_b, w_down=v_w_down)
    layered = {k: a.ndim == 3 for k, a in given.items()}
    drop = lambda a: a[0] if a.ndim == 3 else a
    given = {k: drop(a) for k, a in given.items()}
    m_in = {k: drop(a) for k, a in m_in.items()}
    v_in = {k: drop(a) for k, a in v_in.items()}

    xi, yi, ci = lax.axis_index("x"), lax.axis_index("y"), lax.axis_index("c")
    chip = (2 * xi + yi).astype(jnp.int32)
    core_arr = ci.astype(jnp.int32).reshape(1)
    chip_core = jnp.stack([chip, ci.astype(jnp.int32)])

    nf = given["b_forget"].shape[1]
    cut = 3 * given["w_br_fox"].shape[0]

    d_model = given["w_out"].shape[1]
    gate0 = given["w_in"].shape[1] - len(W_IN_GATES) * d_model
    shard = {
        "w_in_a": jnp.concatenate([given["w_in"][:, :cut], given["w_in"][:, cut + nf:gate0]], axis=1).astype(BF16),
        "w_in_f": jnp.pad(given["w_in"][:, cut:cut + nf], ((0, 0), (0, LANES - nf))).astype(BF16),
    }
    for b, name in enumerate(W_IN_GATES):
        shard[name] = given["w_in"][:, gate0 + b * d_model:gate0 + (b + 1) * d_model].astype(BF16)
    for name in BIG:
        if name not in shard:
            shard[name] = given[name].astype(BF16)
    w = {}
    small_shapes = [given[name].shape for name in SMALL_SHARDED]
    small_parts = _gather_small(_pack([given[name] for name in SMALL_SHARDED]))[0::2]
    per_chip = [_unpack(small_parts[j], small_shapes) for j in range(N_CHIPS)]
    for k, name in enumerate(SMALL_SHARDED):
        w[name] = jnp.concatenate([per_chip[j][k] for j in range(N_CHIPS)], axis=1)
    for name in SMALL:
        if name not in SMALL_SHARDED:
            w[name] = given[name]
    w["b_forget"] = jnp.pad(given["b_forget"], ((0, 0), (0, LANES - nf)))

    loss, grad_x, g, summed = _local_step(x[0], mem[0], loss_target[0], w, shard, core_arr, chip_core)
    loss = lax.psum(loss, ("x", "y", "c"))
    grads = {name: summed[name] for name in BIG if name in given}
    grads["w_in"] = jnp.concatenate([summed["w_in_a"][:, :cut], summed["w_in_f"][:, :nf], summed["w_in_a"][:, cut:]]
                                    + [summed[name] for name in W_IN_GATES], axis=1)

    g["b_forget"] = g["b_forget"][:, :nf]
    small_full_shapes = [g[name].shape for name in SMALL]
    small_sum = _unpack(_sum_devices(_gather_small(_pack([g[name] for name in SMALL]))), small_full_shapes)
    for name, a in zip(SMALL, small_sum):
        if name in SMALL_SHARDED:
            width = given[name].shape[1]
            a = lax.dynamic_slice_in_dim(a, chip * width, width, axis=1)
        grads[name] = a

    delta, new_m, new_v = {}, {}, {}
    for name in WEIGHTS:
        if name not in SMALL:
            delta[name], new_m[name], new_v[name] = _adamw(given[name], grads[name], m_in[name], v_in[name],
                                                           "adamw_" + name)
    shapes = [given[name].shape for name in SMALL]
    packed = [_pack([src[name] for name in SMALL]) for src in (given, grads, m_in, v_in)]
    for dst, res in zip((delta, new_m, new_v), _adamw(*packed, "adamw_small")):
        for name, a in zip(SMALL, _unpack(res, shapes)):
            dst[name] = a

    out = [loss, grad_x[None]]
    for src in (grads, delta, new_m, new_v):
        out.extend(src[name][None] if layered[name] else src[name] for name in WEIGHTS)
    return tuple(out)
```

```python
import functools

import jax
import jax.numpy as jnp
from jax import lax
from jax.experimental import pallas as pl
from jax.experimental.pallas import tpu as pltpu

F32 = jnp.float32
BF16 = jnp.bfloat16

HEAD_DIM = 128
EPS = 1e-6
NEG_BIG = -1e30

ADAM_LR = 0.001
ADAM_B1 = 0.9
ADAM_B2 = 0.999
ADAM_EPS = 1e-08
ADAM_WD = 0.01
ADAM_STEP = 10

LANES = 128
BF16_SUBLANES = 16
VMEM_LIMIT_BYTES = 56 * 1024 * 1024
MM_TILE = 1024
MM_TILE_K = {"nn": 2048, "nt": 2048, "tn": 4096}
ATT_TILE = 256
ROW_TILE = 256
HEADNORM_ROWS = 512
MEM_ROWS = 512
COL_TILE = 512
ADAM_BLOCK_BYTES = 2 << 20

N_CHIPS = 4
N_DEV = 8
MESH = pl.DeviceIdType.MESH

IN_NAMES = ['x', 'mem', 'g_mix', 'w_in', 'b_forget', 'g_q_fox', 'g_k_fox', 'g_mem', 'w_mem_kv', 'g_q_mem', 'g_k_mem',
            'w_br_fox', 'w_br_sb', 'w_br_mem', 'b_gate', 'w_out', 'g_ffn', 'w_up', 'conv_w', 'conv_b', 'w_down']
WEIGHTS = IN_NAMES[2:]


def _tile(n, target):
    if n <= target:
        return n
    for t in range(target - target % LANES, LANES - 1, -LANES):
        if n % t == 0:
            return t
    return n


def _params(*sem):
    return pltpu.CompilerParams(dimension_semantics=sem, vmem_limit_bytes=VMEM_LIMIT_BYTES)


def _log_sigmoid(z):
    return jnp.minimum(z, 0.0) - jnp.log(1.0 + jnp.exp(-jnp.abs(z)))


def _split2(v):
    hi = v.astype(BF16)
    lo = (v - hi.astype(F32)).astype(BF16)
    return hi, lo


def _split3(v):
    hi = v.astype(BF16)
    r = v - hi.astype(F32)
    mid = r.astype(BF16)
    lo = (r - mid.astype(F32)).astype(BF16)
    return hi, mid, lo


def _dot(a, b):
    return lax.dot_general(a, b, (((1,), (0,)), ((), ())), preferred_element_type=F32)


def _dot_nt(a, b):
    return lax.dot_general(a, b, (((1,), (1,)), ((), ())), preferred_element_type=F32)


def _dot_tn(a, b):
    return lax.dot_general(a, b, (((0,), (0,)), ((), ())), preferred_element_type=F32)


ANY = pl.BlockSpec(memory_space=pl.ANY)


def _place():
    x, y, c = lax.axis_index("x"), lax.axis_index("y"), lax.axis_index("c")
    others = [(1 - x, y), (x, 1 - y), (1 - x, 1 - y)]
    return x, y, c, others


def _remote(src, dst, send_sem, recv_sem, to):
    return pltpu.make_async_remote_copy(src_ref=src, dst_ref=dst, send_sem=send_sem, recv_sem=recv_sem,
                                        device_id=to, device_id_type=MESH)


class _Gather:
    PER_SHARD = 7

    def __init__(self, shards):
        self.inputs = list(shards)
        n = len(shards) * self.PER_SHARD
        self.out_shapes = [jax.ShapeDtypeStruct((N_CHIPS,) + s.shape, s.dtype) for s in shards]
        self.scratch = [pltpu.SemaphoreType.DMA((n,)), pltpu.SemaphoreType.DMA((n,))]

    def _first(self, ins, outs, sems):
        send_sems, recv_sems = sems
        x, y, c, others = _place()
        me = 2 * x + y
        k = self.PER_SHARD
        copies = []
        for i in range(len(ins)):
            h = ins[i].shape[0] // 2
            mine = pl.ds(pl.multiple_of(c * h, BF16_SUBLANES), h)
            for j, (ox, oy) in enumerate(others):
                copies.append(_remote(ins[i].at[mine], outs[i].at[me, mine], send_sems.at[k * i + j],
                                      recv_sems.at[k * i + j], (ox, oy, c)))
            copies.append(_remote(ins[i], outs[i].at[me], send_sems.at[k * i + 6], recv_sems.at[k * i + 6],
                                  (x, y, 1 - c)))
        return copies

    def start(self, ins, outs, sems):
        for cp in self._first(ins, outs, sems):
            cp.start()

    def finish(self, ins, outs, sems):
        send_sems, recv_sems = sems
        x, y, c, others = _place()
        me = 2 * x + y
        sibling = (x, y, 1 - c)
        k = self.PER_SHARD
        passed = []
        for i in range(len(ins)):
            h = ins[i].shape[0] // 2
            mine = pl.ds(pl.multiple_of(c * h, BF16_SUBLANES), h)
            for j, (ox, oy) in enumerate(others):
                blk = outs[i].at[2 * ox + oy, mine]
                _remote(blk, blk, send_sems.at[k * i + j], recv_sems.at[k * i + j], (ox, oy, c)).wait_recv()
                cp = _remote(blk, blk, send_sems.at[k * i + 3 + j], recv_sems.at[k * i + 3 + j], sibling)
                cp.start()
                passed.append(cp)
        for i in range(len(ins)):
            h = ins[i].shape[0] // 2
            theirs = pl.ds(pl.multiple_of((1 - c) * h, BF16_SUBLANES), h)
            for j, (ox, oy) in enumerate(others):
                blk = outs[i].at[2 * ox + oy, theirs]
                _remote(blk, blk, send_sems.at[k * i + 3 + j], recv_sems.at[k * i + 3 + j], sibling).wait_recv()
            own = outs[i].at[me]
            _remote(own, own, send_sems.at[k * i + 6], recv_sems.at[k * i + 6], sibling).wait_recv()
        for cp in self._first(ins, outs, sems) + passed:
            cp.wait_send()


class _Scatter:
    def __init__(self, sums):
        self.inputs = list(sums)
        k = N_CHIPS - 1
        self.out_shapes = [jax.ShapeDtypeStruct((k,) + g.shape[1:], g.dtype) for g in sums]
        self.scratch = [pltpu.SemaphoreType.DMA((k * len(sums),)), pltpu.SemaphoreType.DMA((k * len(sums),))]

    def _copies(self, ins, outs, sems):
        send_sems, recv_sems = sems
        _, _, c, others = _place()
        k = N_CHIPS - 1
        return [_remote(ins[i].at[2 * ox + oy], outs[i].at[j], send_sems.at[k * i + j], recv_sems.at[k * i + j],
                        (ox, oy, c))
                for i in range(len(ins)) for j, (ox, oy) in enumerate(others)]

    def start(self, ins, outs, sems):
        for cp in self._copies(ins, outs, sems):
            cp.start()

    def finish(self, ins, outs, sems):
        for cp in self._copies(ins, outs, sems):
            cp.wait()


class _Swap:
    def __init__(self, grads):
        self.inputs = list(grads)
        n = len(grads)
        self.out_shapes = [jax.ShapeDtypeStruct((g.shape[0],) + g.shape[2:], g.dtype) for g in grads]
        self.scratch = [pltpu.SemaphoreType.DMA((n,)), pltpu.SemaphoreType.DMA((n,))]

    def _copies(self, ins, outs, sems):
        send_sems, recv_sems = sems
        x, y, c, _ = _place()
        return [_remote(ins[i].at[:, 1 - c], outs[i], send_sems.at[i], recv_sems.at[i], (x, y, 1 - c))
                for i in range(len(ins))]

    def start(self, ins, outs, sems):
        for cp in self._copies(ins, outs, sems):
            cp.start()

    def finish(self, ins, outs, sems):
        for cp in self._copies(ins, outs, sems):
            cp.wait()


class _Both:
    def __init__(self, first, second):
        self.parts = (first, second)
        self.inputs = first.inputs + second.inputs
        self.out_shapes = first.out_shapes + second.out_shapes
        self.scratch = first.scratch + second.scratch

    def _each(self, ins, outs, sems):
        first = self.parts[0]
        a, b, c = len(first.inputs), len(first.out_shapes), len(first.scratch)
        return ((first, ins[:a], outs[:b], sems[:c]), (self.parts[1], ins[a:], outs[b:], sems[c:]))

    def start(self, ins, outs, sems):
        for part, i, o, s in self._each(ins, outs, sems):
            part.start(i, o, s)

    def finish(self, ins, outs, sems):
        for part, i, o, s in self._each(ins, outs, sems):
            part.finish(i, o, s)


def _call(body, *, grid, in_specs, out_specs, out_shape, scratch_shapes, semantics, name, args, carry=None):
    n_in, n_out, n_scr = len(in_specs), len(out_specs), len(scratch_shapes)
    if carry is None:
        res = pl.pallas_call(body, grid=grid, in_specs=in_specs, out_specs=out_specs, out_shape=out_shape,
                             scratch_shapes=scratch_shapes, compiler_params=_params(*semantics), name=name)(*args)
        return list(res), []
    nci, nco = len(carry.inputs), len(carry.out_shapes)
    a, b = n_in, n_in + nci
    c, d = b + n_out, b + n_out + nco
    e = d + n_scr

    def carried(*refs):
        ids = [pl.program_id(k) for k in range(len(grid))]
        first = functools.reduce(jnp.logical_and, [i == 0 for i in ids])
        last = functools.reduce(jnp.logical_and, [i == n - 1 for i, n in zip(ids, grid)])

        @pl.when(first)
        def _():
            carry.start(refs[a:b], refs[c:d], refs[e:])

        body(*refs[:a], *refs[b:c], *refs[d:e])

        @pl.when(last)
        def _():
            carry.finish(refs[a:b], refs[c:d], refs[e:])

    res = pl.pallas_call(
        carried,
        grid=grid,
        in_specs=list(in_specs) + [ANY] * nci,
        out_specs=list(out_specs) + [ANY] * nco,
        out_shape=list(out_shape) + carry.out_shapes,
        scratch_shapes=list(scratch_shapes) + carry.scratch,
        compiler_params=_params(*(["arbitrary"] * len(grid))),
        name=name,
    )(*args, *carry.inputs)
    return list(res[:n_out]), list(res[n_out:])


def _mm(a, b, mode, out_dtype, name, residual=None, carry=None, loss_target=None):
    if mode == "nn":
        (m, k), (k2, n) = a.shape, b.shape
    elif mode == "nt":
        (m, k), (n, k2) = a.shape, b.shape
    else:
        (k, m), (k2, n) = a.shape, b.shape
    assert k == k2, (a.shape, b.shape, mode)
    has_res = residual is not None
    has_loss = loss_target is not None
    n_in = 2 + has_res + has_loss
    tm, tn, tk = _tile(m, MM_TILE), _tile(n, MM_TILE), _tile(k, MM_TILE_K[mode])
    nk = k // tk
    dot = {"nn": _dot, "nt": _dot_nt, "tn": _dot_tn}[mode]

    def body(*refs):
        a_ref, b_ref = refs[:2]
        r_ref = refs[2] if has_res else None
        t_ref = refs[n_in - 1] if has_loss else None
        o_ref = refs[n_in]

        def finish(acc):
            if has_res:
                acc = acc + r_ref[...]
            if has_loss:
                err = acc - t_ref[...]
                dy = err * (1.0 / n)
                o_ref[...] = dy
                refs[n_in + 1][...] = dy.astype(BF16)
                tot = jnp.sum(jnp.sum(err * err, axis=-1, keepdims=True), axis=0, keepdims=True)
                refs[n_in + 2][...] = jnp.broadcast_to(tot, (8, LANES))
            else:
                o_ref[...] = acc.astype(o_ref.dtype)

        part = dot(a_ref[...], b_ref[...])
        if nk == 1:
            finish(part)
        else:
            acc_ref = refs[-1]
            kk = pl.program_id(2)

            @pl.when(kk == 0)
            def _():
                acc_ref[...] = part

            @pl.when(kk > 0)
            def _():
                acc_ref[...] += part

            @pl.when(kk == nk - 1)
            def _():
                finish(acc_ref[...])

    if mode == "tn":
        a_spec = pl.BlockSpec((tk, tm), lambda j, i, kk: (kk, i))
    else:
        a_spec = pl.BlockSpec((tm, tk), lambda j, i, kk: (i, kk))
    if mode == "nt":
        b_spec = pl.BlockSpec((tn, tk), lambda j, i, kk: (j, kk))
    else:
        b_spec = pl.BlockSpec((tk, tn), lambda j, i, kk: (kk, j))
    o_spec = pl.BlockSpec((tm, tn), lambda j, i, kk: (i, j))
    in_specs = [a_spec, b_spec] + [o_spec] * (has_res + has_loss)
    args = (a, b) + ((residual,) if has_res else ()) + ((loss_target,) if has_loss else ())
    out_specs, out_shape = [o_spec], [jax.ShapeDtypeStruct((m, n), out_dtype)]
    if has_loss:
        out_specs += [o_spec, pl.BlockSpec((8, LANES), lambda j, i, kk: (i, j))]
        out_shape += [jax.ShapeDtypeStruct((m, n), BF16), jax.ShapeDtypeStruct((m // tm * 8, n // tn * LANES), F32)]
    outs, moved = _call(
        body,
        grid=(n // tn, m // tm, nk),
        in_specs=in_specs,
        out_specs=out_specs,
        out_shape=out_shape,
        scratch_shapes=[pltpu.VMEM((tm, tn), F32)] if nk > 1 else [],
        semantics=("parallel", "parallel", "arbitrary"),
        name=name,
        args=args,
        carry=carry,
    )
    out = outs if has_loss else outs[0]
    return out if carry is None else (out, moved)


def _rms_fwd(x, g, name, carry=None):
    s, d = x.shape
    tm = _tile(s, ROW_TILE)

    def body(x_ref, g_ref, h_ref, r_ref):
        xf = x_ref[...]
        r = lax.rsqrt(jnp.mean(xf * xf, axis=-1, keepdims=True) + EPS)
        h_ref[...] = ((xf * r) * g_ref[...]).astype(BF16)
        r_ref[...] = r

    (h, rstd), moved = _call(
        body,
        grid=(s // tm,),
        in_specs=[pl.BlockSpec((tm, d), lambda i: (i, 0)), pl.BlockSpec((1, d), lambda i: (0, 0))],
        out_specs=[pl.BlockSpec((tm, d), lambda i: (i, 0)), pl.BlockSpec((tm, 1), lambda i: (i, 0))],
        out_shape=[jax.ShapeDtypeStruct((s, d), BF16), jax.ShapeDtypeStruct((s, 1), F32)],
        scratch_shapes=[],
        semantics=("parallel",),
        name=name,
        args=(x, g),
        carry=carry,
    )
    return (h, rstd) if carry is None else (h, rstd, moved)


def _rms_bwd(dh, x, rstd, g, res, name):
    s, d = x.shape
    tm = _tile(s, ROW_TILE)
    has_res = res is not None

    def body(*refs):
        if has_res:
            dh_ref, x_ref, r_ref, g_ref, res_ref, dx_ref, dxb_ref, dg_ref = refs
        else:
            dh_ref, x_ref, r_ref, g_ref, dx_ref, dxb_ref, dg_ref = refs
        dhf = dh_ref[...].astype(F32)
        xhat = x_ref[...] * r_ref[...]
        dy = dhf * g_ref[...]
        dx = r_ref[...] * (dy - xhat * jnp.mean(dy * xhat, axis=-1, keepdims=True))
        if has_res:
            dx = dx + res_ref[...]
        dx_ref[...] = dx
        dxb_ref[...] = dx.astype(BF16)
        part = jnp.sum(dhf * xhat, axis=0, keepdims=True)

        @pl.when(pl.program_id(0) == 0)
        def _():
            dg_ref[...] = part

        @pl.when(pl.program_id(0) > 0)
        def _():
            dg_ref[...] += part

    row = pl.BlockSpec((tm, d), lambda i: (i, 0))
    vec = pl.BlockSpec((1, d), lambda i: (0, 0))
    in_specs = [row, row, pl.BlockSpec((tm, 1), lambda i: (i, 0)), vec] + ([row] if has_res else [])
    args = (dh, x, rstd, g) + ((res,) if has_res else ())
    return pl.pallas_call(
        body,
        grid=(s // tm,),
        in_specs=in_specs,
        out_specs=[row, row, vec],
        out_shape=[jax.ShapeDtypeStruct((s, d), F32), jax.ShapeDtypeStruct((s, d), BF16),
                   jax.ShapeDtypeStruct((1, d), F32)],
        compiler_params=_params("arbitrary"),
        name=name,
    )(*args)


def _headnorm_fwd(src, col0, nheads, g, name):
    s = src.shape[0]
    tm = _tile(s, HEADNORM_ROWS)
    w = nheads * HEAD_DIM
    assert col0 % nheads == 0

    def body(x_ref, g_ref, o_ref):
        for hh in range(nheads):
            xf = _head(x_ref, hh).astype(F32)
            r = lax.rsqrt(jnp.mean(xf * xf, axis=-1, keepdims=True) + EPS)
            o_ref[:, hh * HEAD_DIM:(hh + 1) * HEAD_DIM] = ((xf * r) * g_ref[...]).astype(BF16)

    return pl.pallas_call(
        body,
        grid=(s // tm,),
        in_specs=[pl.BlockSpec((tm, w), lambda i: (i, col0 // nheads)),
                  pl.BlockSpec((1, HEAD_DIM), lambda i: (0, 0))],
        out_specs=pl.BlockSpec((tm, w), lambda i: (i, 0)),
        out_shape=jax.ShapeDtypeStruct((s, w), BF16),
        compiler_params=_params("parallel"),
        name=name,
    )(src, g)


def _headnorm_bwd(dxn, src, col0, nheads, g, name):
    s = src.shape[0]
    tm = _tile(s, HEADNORM_ROWS)
    w = nheads * HEAD_DIM
    assert col0 % nheads == 0

    def body(d_ref, x_ref, g_ref, dx_ref, dg_ref):
        part = jnp.zeros((1, HEAD_DIM), F32)
        for hh in range(nheads):
            xf = _head(x_ref, hh).astype(F32)
            r = lax.rsqrt(jnp.mean(xf * xf, axis=-1, keepdims=True) + EPS)
            xhat = xf * r
            dn = _head(d_ref, hh).astype(F32)
            dy = dn * g_ref[...]
            dx = r * (dy - xhat * jnp.mean(dy * xhat, axis=-1, keepdims=True))
            dx_ref[:, hh * HEAD_DIM:(hh + 1) * HEAD_DIM] = dx.astype(BF16)
            part = part + jnp.sum(dn * xhat, axis=0, keepdims=True)

        @pl.when(pl.program_id(0) == 0)
        def _():
            dg_ref[...] = part

        @pl.when(pl.program_id(0) > 0)
        def _():
            dg_ref[...] += part

    return pl.pallas_call(
        body,
        grid=(s // tm,),
        in_specs=[pl.BlockSpec((tm, w), lambda i: (i, 0)),
                  pl.BlockSpec((tm, w), lambda i: (i, col0 // nheads)),
                  pl.BlockSpec((1, HEAD_DIM), lambda i: (0, 0))],
        out_specs=[pl.BlockSpec((tm, w), lambda i: (i, 0)),
                   pl.BlockSpec((1, HEAD_DIM), lambda i: (0, 0))],
        out_shape=[jax.ShapeDtypeStruct((s, w), BF16), jax.ShapeDtypeStruct((1, HEAD_DIM), F32)],
        compiler_params=_params("arbitrary"),
        name=name,
    )(dxn, src, g)


def _tri(t, lower_inclusive):
    r = lax.broadcasted_iota(jnp.int32, (t, t), 0)
    c = lax.broadcasted_iota(jnp.int32, (t, t), 1)
    keep = (c <= r) if lower_inclusive else (c >= r)
    return jnp.where(keep, 1.0, 0.0).astype(BF16)


def _forget_fwd(f_logit, b_pad):
    s = f_logit.shape[0]
    t = _tile(s, ATT_TILE)

    def body(f_ref, b_ref, c_ref, carry):
        @pl.when(pl.program_id(0) == 0)
        def _():
            carry[...] = jnp.zeros_like(carry)

        lf = _log_sigmoid(f_ref[...] + b_ref[...])
        tri = _tri(t, True)
        acc = carry[...]
        for part in _split3(lf):
            acc = acc + _dot(tri, part)
        c_ref[...] = acc
        carry[...] += jnp.sum(lf, axis=0, keepdims=True)

    return pl.pallas_call(
        body,
        grid=(s // t,),
        in_specs=[pl.BlockSpec((t, LANES), lambda i: (i, 0)), pl.BlockSpec((1, LANES), lambda i: (0, 0))],
        out_specs=pl.BlockSpec((t, LANES), lambda i: (i, 0)),
        out_shape=jax.ShapeDtypeStruct((s, LANES), F32),
        scratch_shapes=[pltpu.VMEM((1, LANES), F32)],
        compiler_params=_params("arbitrary"),
        name="forget_fwd",
    )(f_logit, b_pad)


def _forget_bwd(dc, f_logit, b_pad):
    s = f_logit.shape[0]
    t = _tile(s, ATT_TILE)
    nb = s // t

    def body(dc_ref, f_ref, b_ref, df_ref, db_ref, carry):
        @pl.when(pl.program_id(0) == 0)
        def _():
            carry[...] = jnp.zeros_like(carry)
            db_ref[...] = jnp.zeros_like(db_ref)

        d = dc_ref[...]
        tri = _tri(t, False)
        acc = carry[...]
        for part in _split3(d):
            acc = acc + _dot(tri, part)
        z = f_ref[...] + b_ref[...]
        df = acc * jnp.exp(_log_sigmoid(-z))
        df_ref[...] = df
        db_ref[...] += jnp.sum(df, axis=0, keepdims=True)
        carry[...] += jnp.sum(d, axis=0, keepdims=True)

    rev = pl.BlockSpec((t, LANES), lambda i: (nb - 1 - i, 0))
    vec = pl.BlockSpec((1, LANES), lambda i: (0, 0))
    return pl.pallas_call(
        body,
        grid=(nb,),
        in_specs=[rev, rev, vec],
        out_specs=[rev, vec],
        out_shape=[jax.ShapeDtypeStruct((s, LANES), F32), jax.ShapeDtypeStruct((1, LANES), F32)],
        scratch_shapes=[pltpu.VMEM((1, LANES), F32)],
        compiler_params=_params("arbitrary"),
        name="forget_bwd",
    )(dc, f_logit, b_pad)


SB_FWD_GROUP = 6
FOX_GROUP = 6
SB_BWD_GROUP = 3


def _head(ref, hh, rows=slice(None)):
    return ref[rows, hh * HEAD_DIM:(hh + 1) * HEAD_DIM]


def _tri_mask(t, strict):
    r = lax.broadcasted_iota(jnp.int32, (t, t), 0)
    c = lax.broadcasted_iota(jnp.int32, (t, t), 1)
    return (c < r) if strict else (c <= r)


def _fox_fwd(qn, kn, proj, colv, c_col, c_row, nheads, carry=None):
    s = qn.shape[0]
    t = _tile(s, ATT_TILE)
    scale = HEAD_DIM ** -0.5
    hg = FOX_GROUP
    gw = hg * HEAD_DIM
    assert nheads % hg == 0 and colv % hg == 0

    def body(q_ref, k_ref, v_ref, cc_ref, cr_ref, o_ref, of_ref, lse_ref):
        qi = pl.program_id(1)
        causal = _tri_mask(t, False)

        def tile(kj, carry, diagonal):
            off = pl.multiple_of(kj * t, t)
            heads = range(hg)
            rows = pl.ds(off, t)
            qk = [_dot_nt(_head(q_ref, hh), _head(k_ref, hh, rows)) for hh in heads]
            sc = [qk[hh] * scale + (cc_ref[hh] - cr_ref[hh, :, rows]) for hh in heads]
            if diagonal:
                sc = [jnp.where(causal, sc[hh], NEG_BIG) for hh in heads]
            m_new = [jnp.maximum(carry[hh][0], jnp.max(sc[hh], axis=-1, keepdims=True)) for hh in heads]
            p = [jnp.exp(sc[hh] - m_new[hh]) for hh in heads]
            pv = [_dot(p[hh].astype(BF16), _head(v_ref, hh, rows)) for hh in heads]
            out = []
            for hh in heads:
                m, l, acc = carry[hh]
                alpha = jnp.exp(m - m_new[hh])
                out.append((m_new[hh], alpha * l + jnp.sum(p[hh], axis=-1, keepdims=True), alpha * acc + pv[hh]))
            return tuple(out)

        init = tuple((jnp.full((t, 1), NEG_BIG, F32), jnp.zeros((t, 1), F32), jnp.zeros((t, HEAD_DIM), F32))
                     for _ in range(hg))
        carry = lax.fori_loop(0, qi, lambda kj, c: tile(kj, c, False), init)
        carry = tile(qi, carry, True)
        for hh in range(hg):
            m, l, acc = carry[hh]
            o = acc / l
            of_ref[:, hh * HEAD_DIM:(hh + 1) * HEAD_DIM] = o
            o_ref[:, hh * HEAD_DIM:(hh + 1) * HEAD_DIM] = o.astype(BF16)
            lse_ref[hh] = m + jnp.log(l)

    tile_spec = pl.BlockSpec((t, gw), lambda h, i: (i, h))
    w = nheads * HEAD_DIM
    return _call(
        body,
        grid=(nheads // hg, s // t),
        in_specs=[tile_spec,
                  pl.BlockSpec((s, gw), lambda h, i: (0, h), pipeline_mode=pl.Buffered(buffer_count=1)),
                  pl.BlockSpec((s, gw), lambda h, i: (0, colv // hg + h), pipeline_mode=pl.Buffered(buffer_count=1)),
                  pl.BlockSpec((hg, t, 1), lambda h, i: (h, i, 0)),
                  pl.BlockSpec((hg, 1, s), lambda h, i: (h, 0, 0))],
        out_specs=[tile_spec, tile_spec, pl.BlockSpec((hg, t, 1), lambda h, i: (h, i, 0))],
        out_shape=[jax.ShapeDtypeStruct((s, w), BF16), jax.ShapeDtypeStruct((s, w), F32),
                   jax.ShapeDtypeStruct((nheads, s, 1), F32)],
        scratch_shapes=[],
        semantics=("parallel", "parallel"),
        name="fox_fwd",
        args=(qn, kn, proj, c_col, c_row),
        carry=carry,
    )


def _fox_bwd(qn, kn, proj, colv, c_col, c_row, o, do, lse, nheads, carry=None):
    s = qn.shape[0]
    t = _tile(s, ATT_TILE)
    scale = HEAD_DIM ** -0.5
    hg = FOX_GROUP
    gw = hg * HEAD_DIM
    assert nheads % hg == 0 and colv % hg == 0

    def body(q_ref, k_ref, v_ref, cc_ref, cr_ref, o_ref, do_ref, lse_ref,
             dq_ref, dk_ref, dv_ref, drs_ref, dcs_ref):
        qi = pl.program_id(1)

        @pl.when(qi == 0)
        def _():
            dk_ref[...] = jnp.zeros_like(dk_ref)
            dv_ref[...] = jnp.zeros_like(dv_ref)
            dcs_ref[...] = jnp.zeros_like(dcs_ref)

        causal = _tri_mask(t, False)
        delta = [jnp.sum(_head(o_ref, hh) * _head(do_ref, hh).astype(F32), axis=-1, keepdims=True)
                 for hh in range(hg)]

        def tile(kj, carry, diagonal):
            off = pl.multiple_of(kj * t, t)
            heads = range(hg)
            rows = pl.ds(off, t)
            qk = [_dot_nt(_head(q_ref, hh), _head(k_ref, hh, rows)) for hh in heads]
            dp = [_dot_nt(_head(do_ref, hh), _head(v_ref, hh, rows)) for hh in heads]
            p = [jnp.exp(qk[hh] * scale + (cc_ref[hh] - cr_ref[hh, :, rows]) - lse_ref[hh]) for hh in heads]
            if diagonal:
                p = [jnp.where(causal, p[hh], 0.0) for hh in heads]
            ds = [p[hh] * (dp[hh] - delta[hh]) for hh in heads]
            dsb = [ds[hh].astype(BF16) for hh in heads]
            dv = [_dot_tn(p[hh].astype(BF16), _head(do_ref, hh)) for hh in heads]
            dk = [_dot_tn(dsb[hh], _head(q_ref, hh)) * scale for hh in heads]
            dq = [_dot(dsb[hh], _head(k_ref, hh, rows)) * scale for hh in heads]
            for hh in heads:
                cols = slice(hh * HEAD_DIM, (hh + 1) * HEAD_DIM)
                dv_ref[rows, cols] += dv[hh]
                dk_ref[rows, cols] += dk[hh]
                dcs_ref[hh, :, rows] += jnp.sum(ds[hh], axis=0, keepdims=True)
            return tuple((carry[hh][0] + dq[hh], carry[hh][1] + jnp.sum(ds[hh], axis=-1, keepdims=True))
                         for hh in heads)

        init = tuple((jnp.zeros((t, HEAD_DIM), F32), jnp.zeros((t, 1), F32)) for _ in range(hg))
        carry = lax.fori_loop(0, qi, lambda kj, c: tile(kj, c, False), init)
        carry = tile(qi, carry, True)
        for hh in range(hg):
            dq_ref[:, hh * HEAD_DIM:(hh + 1) * HEAD_DIM] = carry[hh][0]
            drs_ref[hh] = carry[hh][1]

    tile_spec = pl.BlockSpec((t, gw), lambda h, i: (i, h))
    once = pl.Buffered(buffer_count=1)
    full = pl.BlockSpec((s, gw), lambda h, i: (0, h), pipeline_mode=once)
    colspec = pl.BlockSpec((hg, t, 1), lambda h, i: (h, i, 0))
    rowspec = pl.BlockSpec((hg, 1, s), lambda h, i: (h, 0, 0))
    w = nheads * HEAD_DIM
    return _call(
        body,
        grid=(nheads // hg, s // t),
        in_specs=[tile_spec, full, pl.BlockSpec((s, gw), lambda h, i: (0, colv // hg + h), pipeline_mode=once),
                  colspec, rowspec,
                  tile_spec, tile_spec, colspec],
        out_specs=[tile_spec, full, full, colspec, rowspec],
        out_shape=[jax.ShapeDtypeStruct((s, w), F32), jax.ShapeDtypeStruct((s, w), F32),
                   jax.ShapeDtypeStruct((s, w), F32), jax.ShapeDtypeStruct((nheads, s, 1), F32),
                   jax.ShapeDtypeStruct((nheads, 1, s), F32)],
        scratch_shapes=[],
        semantics=("arbitrary", "arbitrary"),
        name="fox_bwd",
        args=(qn, kn, proj, c_col, c_row, o, do, lse),
        carry=carry,
    )


def _sb_tile(q, k, scale, later, valid):
    z = _dot_nt(q, k) * scale
    lb = _log_sigmoid(z)
    lm = lb - z
    if valid is not None:
        lm = jnp.where(valid, lm, 0.0)
    suffix = _dot(jnp.concatenate(_split2(lm), axis=1), later)
    return lb, lm, suffix


def _later(t):
    r = lax.broadcasted_iota(jnp.int32, (2 * t, t), 0) % t
    c = lax.broadcasted_iota(jnp.int32, (2 * t, t), 1)
    return jnp.where(r > c, 1.0, 0.0).astype(BF16)


def _sb_fwd(proj, colq, colk, colv, nheads, carry=None):
    s = proj.shape[0]
    t = _tile(s, ATT_TILE)
    scale = HEAD_DIM ** -0.5
    hg = SB_FWD_GROUP
    gw = hg * HEAD_DIM
    assert nheads % hg == 0 and colq % hg == 0 and colk % hg == 0 and colv % hg == 0

    def body(q_ref, k_ref, v_ref, o_ref):
        qi = pl.program_id(1)
        later = _later(t)
        before = _tri_mask(t, True)

        def tile(kj, carry, diagonal):
            off = pl.multiple_of(kj * t, t)
            heads = range(hg)
            z = [_dot_nt(_head(q_ref, hh), _head(k_ref, hh, pl.ds(off, t))) * scale for hh in heads]
            lb = [_log_sigmoid(z[hh]) for hh in heads]
            lm = [lb[hh] - z[hh] for hh in heads]
            if diagonal:
                lm = [jnp.where(before, lm[hh], 0.0) for hh in heads]
            parts = [jnp.concatenate(_split2(lm[hh]), axis=1) for hh in heads]
            suffix = [_dot(parts[hh], later) for hh in heads]
            a = [jnp.exp(lb[hh] + suffix[hh] + carry[hh][0]) for hh in heads]
            if diagonal:
                a = [jnp.where(before, a[hh], 0.0) for hh in heads]
            av = [_dot(a[hh].astype(BF16), _head(v_ref, hh, pl.ds(off, t))) for hh in heads]
            return tuple((carry[hh][0] + jnp.sum(lm[hh], axis=-1, keepdims=True), carry[hh][1] + av[hh])
                         for hh in heads)

        init = tuple((jnp.zeros((t, 1), F32), jnp.zeros((t, HEAD_DIM), F32)) for _ in range(hg))
        carry = tile(qi, init, True)
        carry = lax.fori_loop(1, qi + 1, lambda i, c: tile(qi - i, c, False), carry)
        for hh in range(hg):
            o_ref[:, hh * HEAD_DIM:(hh + 1) * HEAD_DIM] = carry[hh][1].astype(BF16)

    return _call(
        body,
        grid=(nheads // hg, s // t),
        in_specs=[pl.BlockSpec((t, gw), lambda h, i: (i, colq // hg + h)),
                  pl.BlockSpec((s, gw), lambda h, i: (0, colk // hg + h), pipeline_mode=pl.Buffered(buffer_count=1)),
                  pl.BlockSpec((s, gw), lambda h, i: (0, colv // hg + h), pipeline_mode=pl.Buffered(buffer_count=1))],
        out_specs=[pl.BlockSpec((t, gw), lambda h, i: (i, h))],
        out_shape=[jax.ShapeDtypeStruct((s, nheads * HEAD_DIM), BF16)],
        scratch_shapes=[],
        semantics=("parallel", "parallel"),
        name="sb_fwd",
        args=(proj, proj, proj),
        carry=carry,
    )


def _sb_bwd(proj, colq, colk, colv, do, nheads, carry=None):
    s = proj.shape[0]
    t = _tile(s, ATT_TILE)
    scale = HEAD_DIM ** -0.5
    hg = SB_BWD_GROUP
    gw = hg * HEAD_DIM
    assert nheads % hg == 0 and colq % hg == 0 and colk % hg == 0 and colv % hg == 0

    def body(q_ref, k_ref, v_ref, do_ref, dq_ref, dk_ref, dv_ref, g_s, beta_s):
        qi = pl.program_id(1)

        @pl.when(qi == 0)
        def _():
            dk_ref[...] = jnp.zeros_like(dk_ref)
            dv_ref[...] = jnp.zeros_like(dv_ref)

        later = _later(t)
        before = _tri_mask(t, True)

        def back(kj, carry, diagonal):
            off = pl.multiple_of(kj * t, t)
            heads = range(hg)
            rows = pl.ds(off, t)
            z = [_dot_nt(_head(q_ref, hh), _head(k_ref, hh, rows)) * scale for hh in heads]
            da = [_dot_nt(_head(do_ref, hh), _head(v_ref, hh, rows)) for hh in heads]
            lb = [_log_sigmoid(z[hh]) for hh in heads]
            lm = [lb[hh] - z[hh] for hh in heads]
            if diagonal:
                lm = [jnp.where(before, lm[hh], 0.0) for hh in heads]
            parts = [jnp.concatenate(_split2(lm[hh]), axis=1) for hh in heads]
            suffix = [_dot(parts[hh], later) for hh in heads]
            a = [jnp.exp(lb[hh] + suffix[hh] + carry[hh]) for hh in heads]
            if diagonal:
                a = [jnp.where(before, a[hh], 0.0) for hh in heads]
            dv = [_dot_tn(a[hh].astype(BF16), _head(do_ref, hh)) for hh in heads]
            for hh in heads:
                g_s[hh, :, rows] = a[hh] * da[hh]
                beta_s[hh, :, rows] = jnp.exp(lb[hh]).astype(BF16)
            for hh in heads:
                dv_ref[rows, hh * HEAD_DIM:(hh + 1) * HEAD_DIM] += dv[hh]
            return tuple(carry[hh] + jnp.sum(lm[hh], axis=-1, keepdims=True) for hh in heads)

        rc = back(qi, tuple(jnp.zeros((t, 1), F32) for _ in range(hg)), True)
        lax.fori_loop(1, qi + 1, lambda i, c: back(qi - i, c, False), rc)

        earlier = jnp.where(lax.broadcasted_iota(jnp.int32, (2 * t, t), 0) % t
                            < lax.broadcasted_iota(jnp.int32, (2 * t, t), 1), 1.0, 0.0).astype(BF16)

        def fwd(kj, carry, diagonal):
            off = pl.multiple_of(kj * t, t)
            heads = range(hg)
            rows = pl.ds(off, t)
            g = [g_s[hh, :, rows] for hh in heads]
            parts = [jnp.concatenate(_split2(g[hh]), axis=1) for hh in heads]
            gsum = [_dot(parts[hh], earlier) + carry[hh][0] for hh in heads]
            dz = []
            for hh in heads:
                beta = beta_s[hh, :, rows].astype(F32)
                d = g[hh] * (1.0 - beta) - gsum[hh] * beta
                if diagonal:
                    d = jnp.where(before, d, 0.0)
                dz.append(d.astype(BF16))
            dk = [_dot_tn(dz[hh], _head(q_ref, hh)) * scale for hh in heads]
            dq = [_dot(dz[hh], _head(k_ref, hh, rows)) * scale for hh in heads]
            for hh in heads:
                dk_ref[rows, hh * HEAD_DIM:(hh + 1) * HEAD_DIM] += dk[hh]
            return tuple((carry[hh][0] + jnp.sum(g[hh], axis=-1, keepdims=True), carry[hh][1] + dq[hh])
                         for hh in heads)

        init = tuple((jnp.zeros((t, 1), F32), jnp.zeros((t, HEAD_DIM), F32)) for _ in range(hg))
        carry = lax.fori_loop(0, qi, lambda kj, c: fwd(kj, c, False), init)
        carry = fwd(qi, carry, True)
        for hh in range(hg):
            dq_ref[:, hh * HEAD_DIM:(hh + 1) * HEAD_DIM] = carry[hh][1]

    once = pl.Buffered(buffer_count=1)
    tile_spec = pl.BlockSpec((t, gw), lambda h, i: (i, h))
    full = pl.BlockSpec((s, gw), lambda h, i: (0, h), pipeline_mode=once)
    w = nheads * HEAD_DIM
    return _call(
        body,
        grid=(nheads // hg, s // t),
        in_specs=[pl.BlockSpec((t, gw), lambda h, i: (i, colq // hg + h)),
                  pl.BlockSpec((s, gw), lambda h, i: (0, colk // hg + h), pipeline_mode=once),
                  pl.BlockSpec((s, gw), lambda h, i: (0, colv // hg + h), pipeline_mode=once),
                  tile_spec],
        out_specs=[tile_spec, full, full],
        out_shape=[jax.ShapeDtypeStruct((s, w), F32)] * 3,
        scratch_shapes=[pltpu.VMEM((hg, t, s), F32), pltpu.VMEM((hg, t, s), BF16)],
        semantics=("arbitrary", "arbitrary"),
        name="sb_bwd",
        args=(proj, proj, proj, do),
        carry=carry,
    )


def _mem_fwd(qn, kn, mkv, nheads):
    s = qn.shape[0]
    mtok = kn.shape[0]
    t = _tile(s, MEM_ROWS)
    w = nheads * HEAD_DIM
    scale = HEAD_DIM ** -0.5

    def body(q_ref, k_ref, v_ref, o_ref):
        heads = range(nheads)
        sc = [_dot_nt(_head(q_ref, hh), _head(k_ref, hh)) * scale for hh in heads]
        p = [jnp.exp(sc[hh] - jnp.max(sc[hh], axis=-1, keepdims=True)) for hh in heads]
        p = [p[hh] / jnp.sum(p[hh], axis=-1, keepdims=True) for hh in heads]
        o = [_dot(p[hh].astype(BF16), _head(v_ref, hh)) for hh in heads]
        for hh in heads:
            o_ref[:, hh * HEAD_DIM:(hh + 1) * HEAD_DIM] = o[hh].astype(BF16)

    return pl.pallas_call(
        body,
        grid=(s // t,),
        in_specs=[pl.BlockSpec((t, w), lambda i: (i, 0)),
                  pl.BlockSpec((mtok, w), lambda i: (0, 0)),
                  pl.BlockSpec((mtok, w), lambda i: (0, 1))],
        out_specs=pl.BlockSpec((t, w), lambda i: (i, 0)),
        out_shape=jax.ShapeDtypeStruct((s, w), BF16),
        compiler_params=_params("parallel"),
        name="mem_fwd",
    )(qn, kn, mkv)


def _mem_bwd(qn, kn, mkv, do, nheads):
    s = qn.shape[0]
    mtok = kn.shape[0]
    t = _tile(s, MEM_ROWS)
    w = nheads * HEAD_DIM
    scale = HEAD_DIM ** -0.5

    def body(q_ref, k_ref, v_ref, do_ref, dq_ref, dk_ref, dv_ref):
        @pl.when(pl.program_id(0) == 0)
        def _():
            dk_ref[...] = jnp.zeros_like(dk_ref)
            dv_ref[...] = jnp.zeros_like(dv_ref)

        heads = range(nheads)
        sc = [_dot_nt(_head(q_ref, hh), _head(k_ref, hh)) * scale for hh in heads]
        dp = [_dot_nt(_head(do_ref, hh), _head(v_ref, hh)) for hh in heads]
        p = [jnp.exp(sc[hh] - jnp.max(sc[hh], axis=-1, keepdims=True)) for hh in heads]
        p = [p[hh] / jnp.sum(p[hh], axis=-1, keepdims=True) for hh in heads]
        ds = [(p[hh] * (dp[hh] - jnp.sum(p[hh] * dp[hh], axis=-1, keepdims=True))).astype(BF16) for hh in heads]
        dq = [_dot(ds[hh], _head(k_ref, hh)) * scale for hh in heads]
        dk = [_dot_tn(ds[hh], _head(q_ref, hh)) * scale for hh in heads]
        dv = [_dot_tn(p[hh].astype(BF16), _head(do_ref, hh)) for hh in heads]
        for hh in heads:
            cols = slice(hh * HEAD_DIM, (hh + 1) * HEAD_DIM)
            dq_ref[:, cols] = dq[hh]
            dk_ref[:, cols] += dk[hh]
            dv_ref[:, cols] += dv[hh]

    tile = pl.BlockSpec((t, w), lambda i: (i, 0))
    kspec = pl.BlockSpec((mtok, w), lambda i: (0, 0))
    return pl.pallas_call(
        body,
        grid=(s // t,),
        in_specs=[tile, kspec, pl.BlockSpec((mtok, w), lambda i: (0, 1)), tile],
        out_specs=[tile, kspec, kspec],
        out_shape=[jax.ShapeDtypeStruct((s, w), F32), jax.ShapeDtypeStruct((mtok, w), F32),
                   jax.ShapeDtypeStruct((mtok, w), F32)],
        compiler_params=_params("arbitrary"),
        name="mem_bwd",
    )(qn, kn, mkv, do)


def _merge_fwd(p0, p1, p2, gates, b_gate):
    s, d = p0.shape
    tm, tn = _tile(s, ROW_TILE), _tile(d, COL_TILE)
    nj = d // tn

    def body(p0_ref, p1_ref, p2_ref, ga_ref, gb_ref, gc_ref, b_ref, o_ref, sa_ref, sb_ref, sc_ref):
        acc = jnp.zeros((tm, tn), F32)
        for b, (p_ref, g_ref, s_ref) in enumerate(((p0_ref, ga_ref, sa_ref), (p1_ref, gb_ref, sb_ref),
                                                   (p2_ref, gc_ref, sc_ref))):
            gate = jax.nn.sigmoid(g_ref[...].astype(F32) + b_ref[b:b + 1, :])
            s_ref[...] = gate.astype(BF16)
            acc = acc + gate * p_ref[...]
        o_ref[...] = acc.astype(BF16)

    blk = pl.BlockSpec((tm, tn), lambda i, j: (i, j))
    merged, *sig = pl.pallas_call(
        body,
        grid=(s // tm, nj),
        in_specs=[blk] * 6 + [pl.BlockSpec((3, tn), lambda i, j: (0, j))],
        out_specs=[blk] * 4,
        out_shape=[jax.ShapeDtypeStruct((s, d), BF16)] * 4,
        compiler_params=_params("parallel", "parallel"),
        name="merge_fwd",
    )(p0, p1, p2, *gates, b_gate)
    return merged, tuple(sig)


def _merge_bwd(dmerged, p0, p1, p2, sig):
    s, d = p0.shape
    tm, tn = _tile(s, ROW_TILE), _tile(d, COL_TILE)
    nj = d // tn

    def body(dm_ref, p0_ref, p1_ref, p2_ref, ga_ref, gb_ref, gc_ref,
             d0_ref, d1_ref, d2_ref, dga_ref, dgb_ref, dgc_ref, db_ref):
        dm = dm_ref[...].astype(F32)
        parts = []
        for p_ref, g_ref, dp_ref, dg_ref in ((p0_ref, ga_ref, d0_ref, dga_ref), (p1_ref, gb_ref, d1_ref, dgb_ref),
                                             (p2_ref, gc_ref, d2_ref, dgc_ref)):
            gate = g_ref[...].astype(F32)
            dp_ref[...] = (dm * gate).astype(BF16)
            dgate = dm * p_ref[...] * gate * (1.0 - gate)
            dg_ref[...] = dgate.astype(BF16)
            parts.append(jnp.sum(dgate, axis=0, keepdims=True))
        part = jnp.concatenate(parts, axis=0)

        @pl.when(pl.program_id(1) == 0)
        def _():
            db_ref[...] = part

        @pl.when(pl.program_id(1) > 0)
        def _():
            db_ref[...] += part

    blk = pl.BlockSpec((tm, tn), lambda j, i: (i, j))
    bias = pl.BlockSpec((3, tn), lambda j, i: (0, j))
    return pl.pallas_call(
        body,
        grid=(nj, s // tm),
        in_specs=[blk] * 7,
        out_specs=[blk] * 6 + [bias],
        out_shape=[jax.ShapeDtypeStruct((s, d), BF16)] * 6 + [jax.ShapeDtypeStruct((3, d), F32)],
        compiler_params=_params("parallel", "arbitrary"),
        name="merge_bwd",
    )(dmerged, p0, p1, p2, *sig)


def _shift_down(v, n):
    rows = lax.broadcasted_iota(jnp.int32, v.shape, 0)
    return jnp.where(rows >= n, pltpu.roll(v, n, 0), 0.0)


def _shift_up(v, n):
    s = v.shape[0]
    rows = lax.broadcasted_iota(jnp.int32, v.shape, 0)
    return jnp.where(rows < s - n, pltpu.roll(v, s - n, 0), 0.0)


def _conv(v, w_ref, b_ref):
    taps = w_ref.shape[0]
    out = v * w_ref[taps - 1:taps, :] + b_ref[...]
    for n in range(1, taps):
        out = out + _shift_down(v, n) * w_ref[taps - 1 - n:taps - n, :]
    return out


def _conv_act_fwd(up, conv_w, conv_b):
    s, f2 = up.shape
    f = f2 // 2
    tn = LANES
    nj = f // tn
    taps = conv_w.shape[0]

    def body(ug_ref, uv_ref, wg_ref, wv_ref, bg_ref, bv_ref, o_ref, cg_ref, cv_ref):
        cg = _conv(ug_ref[...].astype(F32), wg_ref, bg_ref)
        cv = _conv(uv_ref[...].astype(F32), wv_ref, bv_ref)
        o_ref[...] = (cg * jax.nn.sigmoid(cg) * cv).astype(BF16)
        cg_ref[...] = cg.astype(BF16)
        cv_ref[...] = cv.astype(BF16)

    out = pl.BlockSpec((s, tn), lambda j: (0, j))
    return pl.pallas_call(
        body,
        grid=(nj,),
        in_specs=[pl.BlockSpec((s, tn), lambda j: (0, j)), pl.BlockSpec((s, tn), lambda j: (0, nj + j)),
                  pl.BlockSpec((taps, tn), lambda j: (0, j)), pl.BlockSpec((taps, tn), lambda j: (0, nj + j)),
                  pl.BlockSpec((1, tn), lambda j: (0, j)), pl.BlockSpec((1, tn), lambda j: (0, nj + j))],
        out_specs=[out, out, out],
        out_shape=[jax.ShapeDtypeStruct((s, f), BF16)] * 3,
        compiler_params=_params("parallel"),
        name="conv_act_fwd",
    )(up, up, conv_w, conv_w, conv_b, conv_b)


def _conv_act_bwd(up, conv_w, conv_g, conv_v, dact):
    s, f2 = up.shape
    f = f2 // 2
    tn = LANES
    nj = f // tn
    taps = conv_w.shape[0]

    def half(v, du, w_ref, dup_ref, dw_ref, db_ref):
        dup = du * w_ref[taps - 1:taps, :]
        rows = [None] * taps
        rows[taps - 1] = jnp.sum(du * v, axis=0, keepdims=True)
        for n in range(1, taps):
            later = _shift_up(du, n)
            dup = dup + later * w_ref[taps - 1 - n:taps - n, :]
            rows[taps - 1 - n] = jnp.sum(later * v, axis=0, keepdims=True)
        dup_ref[...] = dup.astype(BF16)
        dw_ref[...] = jnp.concatenate(rows, axis=0)
        db_ref[...] = jnp.sum(du, axis=0, keepdims=True)

    def body(ug_ref, uv_ref, wg_ref, wv_ref, cg_ref, cv_ref, da_ref,
             dug_ref, duv_ref, dwg_ref, dwv_ref, dbg_ref, dbv_ref):
        cg = cg_ref[...].astype(F32)
        cv = cv_ref[...].astype(F32)
        da = da_ref[...].astype(F32)
        sg = jax.nn.sigmoid(cg)
        dcv = da * cg * sg
        dcg = da * cv * (sg + cg * sg * (1.0 - sg))
        half(ug_ref[...].astype(F32), dcg, wg_ref, dug_ref, dwg_ref, dbg_ref)
        half(uv_ref[...].astype(F32), dcv, wv_ref, duv_ref, dwv_ref, dbv_ref)

    lo = lambda rows: pl.BlockSpec((rows, tn), lambda j: (0, j))
    hi = lambda rows: pl.BlockSpec((rows, tn), lambda j: (0, nj + j))
    return pl.pallas_call(
        body,
        grid=(nj,),
        in_specs=[lo(s), hi(s), lo(taps), hi(taps), lo(s), lo(s), lo(s)],
        out_specs=[lo(s), lo(s), lo(taps), lo(taps), lo(1), lo(1)],
        out_shape=[jax.ShapeDtypeStruct((s, f), BF16)] * 2 + [jax.ShapeDtypeStruct((taps, f), F32)] * 2
        + [jax.ShapeDtypeStruct((1, f), F32)] * 2,
        compiler_params=_params("parallel"),
        name="conv_act_bwd",
    )(up, up, conv_w, conv_w, conv_g, conv_v, dact)


def _row_tile(rows, row_bytes, budget):
    if rows * row_bytes <= budget or rows % 8:
        return rows
    best = 8
    for t in range(8, rows, 8):
        if rows % t == 0 and t * row_bytes <= budget:
            best = t
    return best


def _adamw(w, g, m, v, name):
    r, c = w.shape
    tr = _row_tile(r, c * 4, ADAM_BLOCK_BYTES)

    def body(w_ref, g_ref, m_ref, v_ref, d_ref, mo_ref, vo_ref):
        gg = g_ref[...]
        m_new = ADAM_B1 * m_ref[...] + (1.0 - ADAM_B1) * gg
        v_new = ADAM_B2 * v_ref[...] + (1.0 - ADAM_B2) * (gg * gg)
        m_hat = m_new / (1.0 - ADAM_B1 ** ADAM_STEP)
        v_hat = v_new / (1.0 - ADAM_B2 ** ADAM_STEP)
        d_ref[...] = -ADAM_LR * (m_hat / (jnp.sqrt(v_hat) + ADAM_EPS) + ADAM_WD * w_ref[...])
        mo_ref[...] = m_new
        vo_ref[...] = v_new

    blk = pl.BlockSpec((tr, c), lambda i: (i, 0))
    return pl.pallas_call(
        body,
        grid=(r // tr,),
        in_specs=[blk] * 4,
        out_specs=[blk] * 3,
        out_shape=[jax.ShapeDtypeStruct((r, c), F32)] * 3,
        compiler_params=_params("parallel"),
        name=name,
    )(w, g, m, v)


def _add_sibling(g, r1, core, name):
    _, _, h, c = g.shape
    th = _row_tile(h, c * 2, ADAM_BLOCK_BYTES)

    def body(core_ref, g_ref, r_ref, o_ref):
        o_ref[...] = (g_ref[...].astype(F32) + r_ref[...].astype(F32)).astype(BF16)

    return pl.pallas_call(
        body,
        grid_spec=pltpu.PrefetchScalarGridSpec(
            num_scalar_prefetch=1,
            grid=(N_CHIPS, h // th),
            in_specs=[pl.BlockSpec((None, None, th, c), lambda j, i, core_ref: (j, core_ref[0], i, 0)),
                      pl.BlockSpec((None, th, c), lambda j, i, core_ref: (j, i, 0))],
            out_specs=pl.BlockSpec((None, th, c), lambda j, i, core_ref: (j, i, 0)),
        ),
        out_shape=jax.ShapeDtypeStruct((N_CHIPS, h, c), BF16),
        compiler_params=_params("parallel", "parallel"),
        name=name,
    )(core, g, r1)


def _add_chips(hsum, r2, chip_core, name):
    _, h, c = hsum.shape
    th = _row_tile(h, c * 4, ADAM_BLOCK_BYTES)

    def body(sel_ref, own_ref, r_ref, o_ref):
        acc = own_ref[...].astype(F32)
        for j in range(N_CHIPS - 1):
            acc = acc + r_ref[j].astype(F32)
        o_ref[...] = acc

    return pl.pallas_call(
        body,
        grid_spec=pltpu.PrefetchScalarGridSpec(
            num_scalar_prefetch=1,
            grid=(h // th,),
            in_specs=[pl.BlockSpec((None, th, c), lambda i, sel_ref: (sel_ref[0], i, 0)),
                      pl.BlockSpec((N_CHIPS - 1, th, c), lambda i, sel_ref: (0, i, 0))],
            out_specs=pl.BlockSpec((None, th, c), lambda i, sel_ref: (sel_ref[1], i, 0)),
        ),
        out_shape=jax.ShapeDtypeStruct((2, h, c), F32),
        compiler_params=_params("parallel"),
        name=name,
    )(chip_core, hsum, r2)


def _sum_devices(parts):
    _, r, c = parts.shape

    def body(p_ref, o_ref):
        acc = p_ref[0]
        for j in range(1, N_DEV):
            acc = acc + p_ref[j]
        o_ref[...] = acc

    return pl.pallas_call(
        body,
        out_shape=jax.ShapeDtypeStruct((r, c), F32),
        compiler_params=pltpu.CompilerParams(vmem_limit_bytes=VMEM_LIMIT_BYTES),
        name="sum_devices",
    )(parts)


def _join_halves(finals):
    n = len(finals)

    def body(*refs):
        outs = refs[n:2 * n]
        send_sems, recv_sems = refs[2 * n:]
        x, y, c, _ = _place()
        sends = [_remote(outs[i].at[c], outs[i].at[c], send_sems.at[i], recv_sems.at[i], (x, y, 1 - c))
                 for i in range(n)]
        for cp in sends:
            cp.start()
        for i in range(n):
            sends[i].wait_send()
            other = outs[i].at[1 - c]
            _remote(other, other, send_sems.at[i], recv_sems.at[i], (x, y, 1 - c)).wait_recv()

    return pl.pallas_call(
        body,
        in_specs=[ANY] * n,
        out_specs=[ANY] * n,
        out_shape=[jax.ShapeDtypeStruct(f.shape, f.dtype) for f in finals],
        input_output_aliases={i: i for i in range(n)},
        scratch_shapes=[pltpu.SemaphoreType.DMA((n,)), pltpu.SemaphoreType.DMA((n,))],
        name="join_halves",
    )(*finals)


def _gather_small(vec):
    k = N_DEV - 1

    def body(v_ref, o_ref, send_sems, recv_sems, local_sem):
        x, y, c, _ = _place()
        me = 4 * x + 2 * y + c
        local = pltpu.make_async_copy(v_ref, o_ref.at[me], local_sem)
        local.start()
        peers = [(x ^ (r >> 2 & 1), y ^ (r >> 1 & 1), c ^ (r & 1)) for r in range(1, N_DEV)]
        sends = [_remote(v_ref, o_ref.at[me], send_sems.at[j], recv_sems.at[j], p) for j, p in enumerate(peers)]
        for cp in sends:
            cp.start()
        for j, (px, py, pc) in enumerate(peers):
            sends[j].wait_send()
            blk = o_ref.at[4 * px + 2 * py + pc]
            _remote(blk, blk, send_sems.at[j], recv_sems.at[j], (px, py, pc)).wait_recv()
        local.wait()

    return pl.pallas_call(
        body,
        in_specs=[ANY],
        out_specs=ANY,
        out_shape=jax.ShapeDtypeStruct((N_DEV,) + vec.shape, vec.dtype),
        scratch_shapes=[pltpu.SemaphoreType.DMA((k,)), pltpu.SemaphoreType.DMA((k,)), pltpu.SemaphoreType.DMA(())],
        name="gather_small",
    )(vec)


W_IN_GATES = ("w_in_g0", "w_in_g1", "w_in_g2")
ROW_SHARDED = ("w_in_a", "w_in_f") + W_IN_GATES + ("w_mem_kv", "w_out", "w_down")
COL_SHARDED = ("w_br_fox", "w_br_sb", "w_br_mem", "w_up")
BIG = ROW_SHARDED + COL_SHARDED


def _whole(name, a):
    if name in ROW_SHARDED:
        return a.reshape(N_CHIPS * a.shape[1], a.shape[2])
    return a.transpose(1, 0, 2).reshape(a.shape[1], N_CHIPS * a.shape[2])


def _by_shard(name, grad):
    if name in ROW_SHARDED:
        a = grad.reshape(N_CHIPS, grad.shape[0] // N_CHIPS, grad.shape[1])
    else:
        a = grad.reshape(grad.shape[0], N_CHIPS, grad.shape[1] // N_CHIPS).transpose(1, 0, 2)
    return a.reshape(N_CHIPS, 2, a.shape[1] // 2, a.shape[2])


def _sibling_sums(names, split, theirs, core):
    return [_add_sibling(a, r, core, "add_sibling_" + name) for name, a, r in zip(names, split, theirs)]


GATHER_FIRST = ("w_in_a", "w_in_f")
GATHER_EARLY = W_IN_GATES[:2] + ("w_mem_kv",)
GATHER_MIX = ("w_out", "w_br_fox", "w_br_sb", "w_br_mem")
REDUCE_FFN = ("w_down", "w_up")
REDUCE_MIX = ("w_out", "w_br_fox", "w_br_sb", "w_br_mem") + W_IN_GATES
REDUCE_IN = ("w_in_a", "w_in_f", "w_mem_kv")


def _local_step(x, mem, target, w, shard, core, chip_core):
    d = x.shape[1]
    nf = shard["w_br_fox"].shape[0] // HEAD_DIM
    nsb = shard["w_br_sb"].shape[0] // HEAD_DIM
    nm = shard["w_br_mem"].shape[0] // HEAD_DIM
    w = dict(w)

    def take(names, gathered):
        for name, a in zip(names, gathered):
            w[name] = _whole(name, a)

    fq, fk, fv = 0, nf, 2 * nf
    sq, sk, sv = 3 * nf, 3 * nf + nsb, 3 * nf + 2 * nsb
    mq = 3 * nf + 3 * nsb

    h, rstd1, moved = _rms_fwd(x, w["g_mix"], "rms_mix_fwd", carry=_Gather([shard[name] for name in GATHER_FIRST]))
    take(GATHER_FIRST, moved)
    proj, moved = _mm(h, w["w_in_a"], "nn", BF16, "proj_att", carry=_Gather([shard[name] for name in GATHER_EARLY]))
    take(GATHER_EARLY, moved)
    f_logit = _mm(h, w["w_in_f"], "nn", F32, "proj_forget")
    gate0, moved = _mm(h, w["w_in_g0"], "nn", BF16, "proj_gate0", carry=_Gather([shard[W_IN_GATES[2]]]))
    take(W_IN_GATES[2:], moved)
    gates = (gate0, _mm(h, w["w_in_g1"], "nn", BF16, "proj_gate1"), _mm(h, w["w_in_g2"], "nn", BF16, "proj_gate2"))
    c_sum = _forget_fwd(f_logit, w["b_forget"])
    c_t = c_sum[:, :nf].T
    c_col, c_row = c_t[:, :, None], c_t[:, None, :]
    qn = _headnorm_fwd(proj, fq, nf, w["g_q_fox"], "fox_qnorm_fwd")
    kn = _headnorm_fwd(proj, fk, nf, w["g_k_fox"], "fox_knorm_fwd")
    (o_fox, o_fox32, lse), moved = _fox_fwd(qn, kn, proj, fv, c_col, c_row, nf,
                                            carry=_Gather([shard[name] for name in GATHER_MIX]))
    take(GATHER_MIX, moved)
    (o_sb,), moved = _sb_fwd(proj, sq, sk, sv, nsb, carry=_Gather([shard["w_up"]]))
    take(("w_up",), moved)
    memn, rstd_m = _rms_fwd(mem, w["g_mem"], "rms_mem_fwd")
    mkv = _mm(memn, w["w_mem_kv"], "nn", BF16, "mem_kv")
    kmn = _headnorm_fwd(mkv, 0, nm, w["g_k_mem"], "mem_knorm_fwd")
    qmn = _headnorm_fwd(proj, mq, nm, w["g_q_mem"], "mem_qnorm_fwd")
    o_mem = _mem_fwd(qmn, kmn, mkv, nm)
    p0 = _mm(o_fox, w["w_br_fox"], "nn", F32, "branch_fox")
    p1 = _mm(o_sb, w["w_br_sb"], "nn", F32, "branch_sb")
    p2 = _mm(o_mem, w["w_br_mem"], "nn", F32, "branch_mem")
    merged, sig = _merge_fwd(p0, p1, p2, gates, w["b_gate"])
    x1 = _mm(merged, w["w_out"], "nn", F32, "out_proj", residual=x)
    h2, rstd2 = _rms_fwd(x1, w["g_ffn"], "rms_ffn_fwd")
    up, moved = _mm(h2, w["w_up"], "nn", BF16, "ffn_up", carry=_Gather([shard["w_down"]]))
    take(("w_down",), moved)
    act, conv_g, conv_v = _conv_act_fwd(up, w["conv_w"], w["conv_b"])
    dy, dyb, lparts = _mm(act, w["w_down"], "nn", F32, "ffn_down_loss", residual=x1, loss_target=target)
    loss = (0.5 / d) * jnp.sum(lparts[::8, ::LANES])

    g = {}
    dact = _mm(dyb, w["w_down"], "nt", BF16, "ffn_down_dx")
    g["w_down"] = _mm(act, dyb, "tn", BF16, "ffn_down_dw")
    dug, duv, dwg, dwv, dbg, dbv = _conv_act_bwd(up, w["conv_w"], conv_g, conv_v, dact)
    dup = jnp.concatenate([dug, duv], axis=1)
    g["conv_w"] = jnp.concatenate([dwg, dwv], axis=1)
    g["conv_b"] = jnp.concatenate([dbg, dbv], axis=1)
    split_down = [_by_shard("w_down", g["w_down"])]
    dh2, theirs_down = _mm(dup, w["w_up"], "nt", BF16, "ffn_up_dx", carry=_Swap(split_down))
    g["w_up"] = _mm(h2, dup, "tn", BF16, "ffn_up_dw")
    split_up = [_by_shard("w_up", g["w_up"])]
    dx1, dx1b, g["g_ffn"] = _rms_bwd(dh2, x1, rstd2, w["g_ffn"], dy, "rms_ffn_bwd")
    dmerged, theirs_up = _mm(dx1b, w["w_out"], "nt", BF16, "out_proj_dx", carry=_Swap(split_up))
    sums_ffn = _sibling_sums(REDUCE_FFN, split_down + split_up, theirs_down + theirs_up, core)
    g["w_out"] = _mm(merged, dx1b, "tn", BF16, "out_proj_dw")
    dp0, dp1, dp2, dga, dgb, dgc, g["b_gate"] = _merge_bwd(dmerged, p0, p1, p2, sig)
    dgates = (dga, dgb, dgc)
    for name, dgate in zip(W_IN_GATES, dgates):
        g[name] = _mm(h, dgate, "tn", BF16, name + "_dw")
    do_fox = _mm(dp0, w["w_br_fox"], "nt", BF16, "branch_fox_dx")
    do_sb = _mm(dp1, w["w_br_sb"], "nt", BF16, "branch_sb_dx")
    do_mem = _mm(dp2, w["w_br_mem"], "nt", BF16, "branch_mem_dx")
    g["w_br_fox"] = _mm(o_fox, dp0, "tn", BF16, "branch_fox_dw")
    g["w_br_sb"] = _mm(o_sb, dp1, "tn", BF16, "branch_sb_dw")
    g["w_br_mem"] = _mm(o_mem, dp2, "tn", BF16, "branch_mem_dw")
    split_mix = [_by_shard(name, g[name]) for name in REDUCE_MIX]

    (dqn, dkn, dfv, drs, dcs), moved = _fox_bwd(qn, kn, proj, fv, c_col, c_row, o_fox32, do_fox, lse, nf,
                                                carry=_Both(_Scatter(sums_ffn[1:]), _Swap(split_mix)))
    others_up, theirs_mix = moved[:1], moved[1:]
    sums_mix = _sibling_sums(REDUCE_MIX, split_mix, theirs_mix, core)
    dfq, g["g_q_fox"] = _headnorm_bwd(dqn, proj, fq, nf, w["g_q_fox"], "fox_qnorm_bwd")
    dfk, g["g_k_fox"] = _headnorm_bwd(dkn, proj, fk, nf, w["g_k_fox"], "fox_knorm_bwd")
    dc = jnp.pad((drs[:, :, 0] - dcs[:, 0, :]).T, ((0, 0), (0, LANES - nf)))
    df, g["b_forget"] = _forget_bwd(dc, f_logit, w["b_forget"])
    (dsq, dsk, dsv), others_mix = _sb_bwd(proj, sq, sk, sv, do_sb, nsb, carry=_Scatter(sums_ffn[:1] + sums_mix))
    others_ffn = others_mix[:1] + others_up
    others_mix = others_mix[1:]
    dqmn, dkmn, dvm = _mem_bwd(qmn, kmn, mkv, do_mem, nm)
    dmq, g["g_q_mem"] = _headnorm_bwd(dqmn, proj, mq, nm, w["g_q_mem"], "mem_qnorm_bwd")
    dkm, g["g_k_mem"] = _headnorm_bwd(dkmn, mkv, 0, nm, w["g_k_mem"], "mem_knorm_bwd")
    dmkv = jnp.concatenate([dkm, dvm.astype(BF16)], axis=1)
    g["w_mem_kv"] = _mm(memn, dmkv, "tn", BF16, "mem_kv_dw")
    dmemn = _mm(dmkv, w["w_mem_kv"], "nt", BF16, "mem_kv_dx")
    _, _, g["g_mem"] = _rms_bwd(dmemn, mem, rstd_m, w["g_mem"], None, "rms_mem_bwd")

    dproj = jnp.concatenate([dfq, dfk, dfv.astype(BF16), dsq.astype(BF16), dsk.astype(BF16), dsv.astype(BF16), dmq],
                            axis=1)
    dfb = df.astype(BF16)
    g["w_in_a"] = _mm(h, dproj, "tn", BF16, "proj_att_dw")
    g["w_in_f"] = _mm(h, dfb, "tn", BF16, "proj_forget_dw")
    split_in = [_by_shard(name, g[name]) for name in REDUCE_IN]
    dh, theirs_in = _mm(dgates[0], w[W_IN_GATES[0]], "nt", F32, W_IN_GATES[0] + "_dx", carry=_Swap(split_in))
    sums_in = _sibling_sums(REDUCE_IN, split_in, theirs_in, core)
    for name, dgate in zip(W_IN_GATES[1:], dgates[1:]):
        dh = _mm(dgate, w[name], "nt", F32, name + "_dx", residual=dh)
    dh, others_in = _mm(dproj, w["w_in_a"], "nt", F32, "proj_att_dx", residual=dh, carry=_Scatter(sums_in))
    dh = _mm(dfb, w["w_in_f"], "nt", F32, "proj_forget_dx", residual=dh)
    grad_x, _, g["g_mix"] = _rms_bwd(dh, x, rstd1, w["g_mix"], dx1, "rms_mix_bwd")

    names = REDUCE_FFN + REDUCE_MIX + REDUCE_IN
    finals = [_add_chips(own, theirs, chip_core, "add_chips_" + name)
              for name, own, theirs in zip(names, sums_ffn + sums_mix + sums_in, others_ffn + others_mix + others_in)]
    summed = {name: a.reshape(2 * a.shape[1], a.shape[2]) for name, a in zip(names, _join_halves(finals))}
    return loss, grad_x, g, summed


SMALL = ("g_mix", "b_forget", "g_q_fox", "g_k_fox", "g_mem", "g_q_mem", "g_k_mem", "b_gate", "g_ffn", "conv_w",
         "conv_b")
SMALL_SHARDED = ("b_gate", "conv_w")
PACK_ROWS = 8


def _pack(arrs):
    flat = jnp.concatenate([a.reshape(-1) for a in arrs])
    unit = PACK_ROWS * LANES
    flat = jnp.pad(flat, (0, -flat.shape[0] % unit))
    return flat.reshape(-1, LANES)


def _unpack(packed, shapes):
    flat = packed.reshape(-1)
    out, at = [], 0
    for s in shapes:
        n = 1
        for dim in s:
            n *= dim
        out.append(flat[at:at + n].reshape(s))
        at += n
    return out


def kernel(x, mem, g_mix, w_in, b_forget, g_q_fox, g_k_fox, g_mem, w_mem_kv, g_q_mem, g_k_mem, w_br_fox, w_br_sb, w_br_mem, b_gate, w_out, g_ffn, w_up, conv_w, conv_b, w_down, loss_target, m_g_mix, m_w_in, m_b_forget, m_g_q_fox, m_g_k_fox, m_g_mem, m_w_mem_kv, m_g_q_mem, m_g_k_mem, m_w_br_fox, m_w_br_sb, m_w_br_mem, m_b_gate, m_w_out, m_g_ffn, m_w_up, m_conv_w, m_conv_b, m_w_down, v_g_mix, v_w_in, v_b_forget, v_g_q_fox, v_g_k_fox, v_g_mem, v_w_mem_kv, v_g_q_mem, v_g_k_mem, v_w_br_fox, v_w_br_sb, v_w_br_mem, v_b_gate, v_w_out, v_g_ffn, v_w_up, v_conv_w, v_conv_b, v_w_down):
    given = dict(g_mix=g_mix, w_in=w_in, b_forget=b_forget, g_q_fox=g_q_fox, g_k_fox=g_k_fox, g_mem=g_mem,
                 w_mem_kv=w_mem_kv, g_q_mem=g_q_mem, g_k_mem=g_k_mem, w_br_fox=w_br_fox, w_br_sb=w_br_sb,
                 w_br_mem=w_br_mem, b_gate=b_gate, w_out=w_out, g_ffn=g_ffn, w_up=w_up, conv_w=conv_w, conv_b=conv_b,
                 w_down=w_down)
    m_in = dict(g_mix=m_g_mix, w_in=m_w_in, b_forget=m_b_forget, g_q_fox=m_g_q_fox, g_k_fox=m_g_k_fox, g_mem=m_g_mem,
                w_mem_kv=m_w_mem_kv, g_q_mem=m_g_q_mem, g_k_mem=m_g_k_mem, w_br_fox=m_w_br_fox, w_br_sb=m_w_br_sb,
                w_br_mem=m_w_br_mem, b_gate=m_b_gate, w_out=m_w_out, g_ffn=m_g_ffn, w_up=m_w_up, conv_w=m_conv_w,
                conv_b=m_conv_b, w_down=m_w_down)
    v_in = dict(g_mix=v_g_mix, w_in=v_w_in, b_forget=v_b_forget, g_q_fox=v_g_q_fox, g_k_fox=v_g_k_fox, g_mem=v_g_mem,
                w_mem_kv=v_w_mem_kv, g_q_mem=v_g_q_mem, g_k_mem=v_g_k_mem, w_br_fox=v_w_br_fox, w_br_sb=v_w_br_sb,
                w_br_mem=v_w_br_mem, b_gate=v_b_gate, w_out=v_w_out, g_ffn=v_g_ffn, w_up=v_w_up, conv_w=v_conv_w,
                conv_b=v_conv_b, w_down=v_w_down)
    layered = {k: a.ndim == 3 for k, a in given.items()}
    drop = lambda a: a[0] if a.ndim == 3 else a
    given = {k: drop(a) for k, a in given.items()}
    m_in = {k: drop(a) for k, a in m_in.items()}
    v_in = {k: drop(a) for k, a in v_in.items()}

    xi, yi, ci = lax.axis_index("x"), lax.axis_index("y"), lax.axis_index("c")
    chip = (2 * xi + yi).astype(jnp.int32)
    core_arr = ci.astype(jnp.int32).reshape(1)
    chip_core = jnp.stack([chip, ci.astype(jnp.int32)])

    nf = given["b_forget"].shape[1]
    cut = 3 * given["w_br_fox"].shape[0]

    d_model = given["w_out"].shape[1]
    gate0 = given["w_in"].shape[1] - len(W_IN_GATES) * d_model
    shard = {
        "w_in_a": jnp.concatenate([given["w_in"][:, :cut], given["w_in"][:, cut + nf:gate0]], axis=1).astype(BF16),
        "w_in_f": jnp.pad(given["w_in"][:, cut:cut + nf], ((0, 0), (0, LANES - nf))).astype(BF16),
    }
    for b, name in enumerate(W_IN_GATES):
        shard[name] = given["w_in"][:, gate0 + b * d_model:gate0 + (b + 1) * d_model].astype(BF16)
    for name in BIG:
        if name not in shard:
            shard[name] = given[name].astype(BF16)
    w = {}
    small_shapes = [given[name].shape for name in SMALL_SHARDED]
    small_parts = _gather_small(_pack([given[name] for name in SMALL_SHARDED]))[0::2]
    per_chip = [_unpack(small_parts[j], small_shapes) for j in range(N_CHIPS)]
    for k, name in enumerate(SMALL_SHARDED):
        w[name] = jnp.concatenate([per_chip[j][k] for j in range(N_CHIPS)], axis=1)
    for name in SMALL:
        if name not in SMALL_SHARDED:
            w[name] = given[name]
    w["b_forget"] = jnp.pad(given["b_forget"], ((0, 0), (0, LANES - nf)))

    loss, grad_x, g, summed = _local_step(x[0], mem[0], loss_target[0], w, shard, core_arr, chip_core)
    loss = lax.psum(loss, ("x", "y", "c"))
    grads = {name: summed[name] for name in BIG if name in given}
    grads["w_in"] = jnp.concatenate([summed["w_in_a"][:, :cut], summed["w_in_f"][:, :nf], summed["w_in_a"][:, cut:]]
                                    + [summed[name] for name in W_IN_GATES], axis=1)

    g["b_forget"] = g["b_forget"][:, :nf]
    small_full_shapes = [g[name].shape for name in SMALL]
    small_sum = _unpack(_sum_devices(_gather_small(_pack([g[name] for name in SMALL]))), small_full_shapes)
    for name, a in zip(SMALL, small_sum):
        if name in SMALL_SHARDED:
            width = given[name].shape[1]
            a = lax.dynamic_slice_in_dim(a, chip * width, width, axis=1)
        grads[name] = a

    delta, new_m, new_v = {}, {}, {}
    for name in WEIGHTS:
        if name not in SMALL:
            delta[name], new_m[name], new_v[name] = _adamw(given[name], grads[name], m_in[name], v_in[name],
                                                           "adamw_" + name)
    shapes = [given[name].shape for name in SMALL]
    packed = [_pack([src[name] for name in SMALL]) for src in (given, grads, m_in, v_in)]
    for dst, res in zip((delta, new_m, new_v), _adamw(*packed, "adamw_small")):
        for name, a in zip(SMALL, _unpack(res, shapes)):
            dst[name] = a

    out = [loss, grad_x[None]]
    for src in (grads, delta, new_m, new_v):
        out.extend(src[name][None] if layered[name] else src[name] for name in WEIGHTS)
    return tuple(out)
```

```python
import functools

import jax
import jax.numpy as jnp
from jax import lax
from jax.experimental import pallas as pl
from jax.experimental.pallas import tpu as pltpu

F32 = jnp.float32
BF16 = jnp.bfloat16

HEAD_DIM = 128
EPS = 1e-6
NEG_BIG = -1e30

ADAM_LR = 0.001
ADAM_B1 = 0.9
ADAM_B2 = 0.999
ADAM_EPS = 1e-08
ADAM_WD = 0.01
ADAM_STEP = 10

LANES = 128
BF16_SUBLANES = 16
VMEM_LIMIT_BYTES = 56 * 1024 * 1024
MM_TILE = 1024
MM_TILE_K = {"nn": 2048, "nt": 2816, "tn": 4096}
ATT_TILE = 256
ROW_TILE = 256
HEADNORM_ROWS = 512
MEM_ROWS = 512
COL_TILE = 512
ADAM_BLOCK_BYTES = 2 << 20

N_CHIPS = 4
N_DEV = 8
MESH = pl.DeviceIdType.MESH

IN_NAMES = ['x', 'mem', 'g_mix', 'w_in', 'b_forget', 'g_q_fox', 'g_k_fox', 'g_mem', 'w_mem_kv', 'g_q_mem', 'g_k_mem',
            'w_br_fox', 'w_br_sb', 'w_br_mem', 'b_gate', 'w_out', 'g_ffn', 'w_up', 'conv_w', 'conv_b', 'w_down']
WEIGHTS = IN_NAMES[2:]


def _tile(n, target):
    if n <= target:
        return n
    for t in range(target - target % LANES, LANES - 1, -LANES):
        if n % t == 0:
            return t
    return n


def _params(*sem):
    return pltpu.CompilerParams(dimension_semantics=sem, vmem_limit_bytes=VMEM_LIMIT_BYTES)


def _log_sigmoid(z):
    return jnp.minimum(z, 0.0) - jnp.log(1.0 + jnp.exp(-jnp.abs(z)))


def _split2(v):
    hi = v.astype(BF16)
    lo = (v - hi.astype(F32)).astype(BF16)
    return hi, lo


def _split3(v):
    hi = v.astype(BF16)
    r = v - hi.astype(F32)
    mid = r.astype(BF16)
    lo = (r - mid.astype(F32)).astype(BF16)
    return hi, mid, lo


def _dot(a, b):
    return lax.dot_general(a, b, (((1,), (0,)), ((), ())), preferred_element_type=F32)


def _dot_nt(a, b):
    return lax.dot_general(a, b, (((1,), (1,)), ((), ())), preferred_element_type=F32)


def _dot_tn(a, b):
    return lax.dot_general(a, b, (((0,), (0,)), ((), ())), preferred_element_type=F32)


ANY = pl.BlockSpec(memory_space=pl.ANY)


def _place():
    x, y, c = lax.axis_index("x"), lax.axis_index("y"), lax.axis_index("c")
    others = [(1 - x, y), (x, 1 - y), (1 - x, 1 - y)]
    return x, y, c, others


def _remote(src, dst, send_sem, recv_sem, to):
    return pltpu.make_async_remote_copy(src_ref=src, dst_ref=dst, send_sem=send_sem, recv_sem=recv_sem,
                                        device_id=to, device_id_type=MESH)


class _Gather:
    PER_SHARD = 7

    def __init__(self, shards):
        self.inputs = list(shards)
        n = len(shards) * self.PER_SHARD
        self.out_shapes = [jax.ShapeDtypeStruct((N_CHIPS,) + s.shape, s.dtype) for s in shards]
        self.scratch = [pltpu.SemaphoreType.DMA((n,)), pltpu.SemaphoreType.DMA((n,))]

    def _first(self, ins, outs, sems):
        send_sems, recv_sems = sems
        x, y, c, others = _place()
        me = 2 * x + y
        k = self.PER_SHARD
        copies = []
        for i in range(len(ins)):
            h = ins[i].shape[0] // 2
            mine = pl.ds(pl.multiple_of(c * h, BF16_SUBLANES), h)
            for j, (ox, oy) in enumerate(others):
                copies.append(_remote(ins[i].at[mine], outs[i].at[me, mine], send_sems.at[k * i + j],
                                      recv_sems.at[k * i + j], (ox, oy, c)))
            copies.append(_remote(ins[i], outs[i].at[me], send_sems.at[k * i + 6], recv_sems.at[k * i + 6],
                                  (x, y, 1 - c)))
        return copies

    def start(self, ins, outs, sems):
        for cp in self._first(ins, outs, sems):
            cp.start()

    def finish(self, ins, outs, sems):
        send_sems, recv_sems = sems
        x, y, c, others = _place()
        me = 2 * x + y
        sibling = (x, y, 1 - c)
        k = self.PER_SHARD
        passed = []
        for i in range(len(ins)):
            h = ins[i].shape[0] // 2
            mine = pl.ds(pl.multiple_of(c * h, BF16_SUBLANES), h)
            for j, (ox, oy) in enumerate(others):
                blk = outs[i].at[2 * ox + oy, mine]
                _remote(blk, blk, send_sems.at[k * i + j], recv_sems.at[k * i + j], (ox, oy, c)).wait_recv()
                cp = _remote(blk, blk, send_sems.at[k * i + 3 + j], recv_sems.at[k * i + 3 + j], sibling)
                cp.start()
                passed.append(cp)
        for i in range(len(ins)):
            h = ins[i].shape[0] // 2
            theirs = pl.ds(pl.multiple_of((1 - c) * h, BF16_SUBLANES), h)
            for j, (ox, oy) in enumerate(others):
                blk = outs[i].at[2 * ox + oy, theirs]
                _remote(blk, blk, send_sems.at[k * i + 3 + j], recv_sems.at[k * i + 3 + j], sibling).wait_recv()
            own = outs[i].at[me]
            _remote(own, own, send_sems.at[k * i + 6], recv_sems.at[k * i + 6], sibling).wait_recv()
        for cp in self._first(ins, outs, sems) + passed:
            cp.wait_send()


class _Scatter:
    def __init__(self, sums):
        self.inputs = list(sums)
        k = N_CHIPS - 1
        self.out_shapes = [jax.ShapeDtypeStruct((k,) + g.shape[1:], g.dtype) for g in sums]
        self.scratch = [pltpu.SemaphoreType.DMA((k * len(sums),)), pltpu.SemaphoreType.DMA((k * len(sums),))]

    def _copies(self, ins, outs, sems):
        send_sems, recv_sems = sems
        _, _, c, others = _place()
        k = N_CHIPS - 1
        return [_remote(ins[i].at[2 * ox + oy], outs[i].at[j], send_sems.at[k * i + j], recv_sems.at[k * i + j],
                        (ox, oy, c))
                for i in range(len(ins)) for j, (ox, oy) in enumerate(others)]

    def start(self, ins, outs, sems):
        for cp in self._copies(ins, outs, sems):
            cp.start()

    def finish(self, ins, outs, sems):
        for cp in self._copies(ins, outs, sems):
            cp.wait()


class _Swap:
    def __init__(self, grads):
        self.inputs = list(grads)
        n = len(grads)
        self.out_shapes = [jax.ShapeDtypeStruct((g.shape[0],) + g.shape[2:], g.dtype) for g in grads]
        self.scratch = [pltpu.SemaphoreType.DMA((n,)), pltpu.SemaphoreType.DMA((n,))]

    def _copies(self, ins, outs, sems):
        send_sems, recv_sems = sems
        x, y, c, _ = _place()
        return [_remote(ins[i].at[:, 1 - c], outs[i], send_sems.at[i], recv_sems.at[i], (x, y, 1 - c))
                for i in range(len(ins))]

    def start(self, ins, outs, sems):
        for cp in self._copies(ins, outs, sems):
            cp.start()

    def finish(self, ins, outs, sems):
        for cp in self._copies(ins, outs, sems):
            cp.wait()


class _Both:
    def __init__(self, first, second):
        self.parts = (first, second)
        self.inputs = first.inputs + second.inputs
        self.out_shapes = first.out_shapes + second.out_shapes
        self.scratch = first.scratch + second.scratch

    def _each(self, ins, outs, sems):
        first = self.parts[0]
        a, b, c = len(first.inputs), len(first.out_shapes), len(first.scratch)
        return ((first, ins[:a], outs[:b], sems[:c]), (self.parts[1], ins[a:], outs[b:], sems[c:]))

    def start(self, ins, outs, sems):
        for part, i, o, s in self._each(ins, outs, sems):
            part.start(i, o, s)

    def finish(self, ins, outs, sems):
        for part, i, o, s in self._each(ins, outs, sems):
            part.finish(i, o, s)


def _call(body, *, grid, in_specs, out_specs, out_shape, scratch_shapes, semantics, name, args, carry=None):
    n_in, n_out, n_scr = len(in_specs), len(out_specs), len(scratch_shapes)
    if carry is None:
        res = pl.pallas_call(body, grid=grid, in_specs=in_specs, out_specs=out_specs, out_shape=out_shape,
                             scratch_shapes=scratch_shapes, compiler_params=_params(*semantics), name=name)(*args)
        return list(res), []
    nci, nco = len(carry.inputs), len(carry.out_shapes)
    a, b = n_in, n_in + nci
    c, d = b + n_out, b + n_out + nco
    e = d + n_scr

    def carried(*refs):
        ids = [pl.program_id(k) for k in range(len(grid))]
        first = functools.reduce(jnp.logical_and, [i == 0 for i in ids])
        last = functools.reduce(jnp.logical_and, [i == n - 1 for i, n in zip(ids, grid)])

        @pl.when(first)
        def _():
            carry.start(refs[a:b], refs[c:d], refs[e:])

        body(*refs[:a], *refs[b:c], *refs[d:e])

        @pl.when(last)
        def _():
            carry.finish(refs[a:b], refs[c:d], refs[e:])

    res = pl.pallas_call(
        carried,
        grid=grid,
        in_specs=list(in_specs) + [ANY] * nci,
        out_specs=list(out_specs) + [ANY] * nco,
        out_shape=list(out_shape) + carry.out_shapes,
        scratch_shapes=list(scratch_shapes) + carry.scratch,
        compiler_params=_params(*(["arbitrary"] * len(grid))),
        name=name,
    )(*args, *carry.inputs)
    return list(res[:n_out]), list(res[n_out:])


def _mm(a, b, mode, out_dtype, name, residual=None, carry=None, loss_target=None):
    if mode == "nn":
        (m, k), (k2, n) = a.shape, b.shape
    elif mode == "nt":
        (m, k), (n, k2) = a.shape, b.shape
    else:
        (k, m), (k2, n) = a.shape, b.shape
    assert k == k2, (a.shape, b.shape, mode)
    has_res = residual is not None
    has_loss = loss_target is not None
    n_in = 2 + has_res + has_loss
    tm, tn, tk = _tile(m, MM_TILE), _tile(n, MM_TILE), _tile(k, MM_TILE_K[mode])
    nk = k // tk
    dot = {"nn": _dot, "nt": _dot_nt, "tn": _dot_tn}[mode]

    def body(*refs):
        a_ref, b_ref = refs[:2]
        r_ref = refs[2] if has_res else None
        t_ref = refs[n_in - 1] if has_loss else None
        o_ref = refs[n_in]

        def finish(acc):
            if has_res:
                acc = acc + r_ref[...]
            if has_loss:
                err = acc - t_ref[...]
                dy = err * (1.0 / n)
                o_ref[...] = dy
                refs[n_in + 1][...] = dy.astype(BF16)
                tot = jnp.sum(jnp.sum(err * err, axis=-1, keepdims=True), axis=0, keepdims=True)
                refs[n_in + 2][...] = jnp.broadcast_to(tot, (8, LANES))
            else:
                o_ref[...] = acc.astype(o_ref.dtype)

        part = dot(a_ref[...], b_ref[...])
        if nk == 1:
            finish(part)
        else:
            acc_ref = refs[-1]
            kk = pl.program_id(2)

            @pl.when(kk == 0)
            def _():
                acc_ref[...] = part

            @pl.when(kk > 0)
            def _():
                acc_ref[...] += part

            @pl.when(kk == nk - 1)
            def _():
                finish(acc_ref[...])

    if mode == "tn":
        a_spec = pl.BlockSpec((tk, tm), lambda j, i, kk: (kk, i))
    else:
        a_spec = pl.BlockSpec((tm, tk), lambda j, i, kk: (i, kk))
    if mode == "nt":
        b_spec = pl.BlockSpec((tn, tk), lambda j, i, kk: (j, kk))
    else:
        b_spec = pl.BlockSpec((tk, tn), lambda j, i, kk: (kk, j))
    o_spec = pl.BlockSpec((tm, tn), lambda j, i, kk: (i, j))
    in_specs = [a_spec, b_spec] + [o_spec] * (has_res + has_loss)
    args = (a, b) + ((residual,) if has_res else ()) + ((loss_target,) if has_loss else ())
    out_specs, out_shape = [o_spec], [jax.ShapeDtypeStruct((m, n), out_dtype)]
    if has_loss:
        out_specs += [o_spec, pl.BlockSpec((8, LANES), lambda j, i, kk: (i, j))]
        out_shape += [jax.ShapeDtypeStruct((m, n), BF16), jax.ShapeDtypeStruct((m // tm * 8, n // tn * LANES), F32)]
    outs, moved = _call(
        body,
        grid=(n // tn, m // tm, nk),
        in_specs=in_specs,
        out_specs=out_specs,
        out_shape=out_shape,
        scratch_shapes=[pltpu.VMEM((tm, tn), F32)] if nk > 1 else [],
        semantics=("parallel", "parallel", "arbitrary"),
        name=name,
        args=args,
        carry=carry,
    )
    out = outs if has_loss else outs[0]
    return out if carry is None else (out, moved)


def _rms_fwd(x, g, name, carry=None):
    s, d = x.shape
    tm = _tile(s, ROW_TILE)

    def body(x_ref, g_ref, h_ref, r_ref):
        xf = x_ref[...]
        r = lax.rsqrt(jnp.mean(xf * xf, axis=-1, keepdims=True) + EPS)
        h_ref[...] = ((xf * r) * g_ref[...]).astype(BF16)
        r_ref[...] = r

    (h, rstd), moved = _call(
        body,
        grid=(s // tm,),
        in_specs=[pl.BlockSpec((tm, d), lambda i: (i, 0)), pl.BlockSpec((1, d), lambda i: (0, 0))],
        out_specs=[pl.BlockSpec((tm, d), lambda i: (i, 0)), pl.BlockSpec((tm, 1), lambda i: (i, 0))],
        out_shape=[jax.ShapeDtypeStruct((s, d), BF16), jax.ShapeDtypeStruct((s, 1), F32)],
        scratch_shapes=[],
        semantics=("parallel",),
        name=name,
        args=(x, g),
        carry=carry,
    )
    return (h, rstd) if carry is None else (h, rstd, moved)


def _rms_bwd(dh, x, rstd, g, res, name, more=None):
    s, d = x.shape
    tm = _tile(s, ROW_TILE)
    has_res = res is not None
    has_more = more is not None
    n_in = 4 + has_res + 2 * has_more

    def body(*refs):
        dh_ref, x_ref, r_ref, g_ref = refs[:4]
        res_ref = refs[4] if has_res else None
        dx_ref, dxb_ref, dg_ref = refs[n_in:]
        dhf = dh_ref[...].astype(F32)
        if has_more:
            dhf = dhf + _dot_nt(refs[n_in - 2][...], refs[n_in - 1][...])
        xhat = x_ref[...] * r_ref[...]
        dy = dhf * g_ref[...]
        dx = r_ref[...] * (dy - xhat * jnp.mean(dy * xhat, axis=-1, keepdims=True))
        if has_res:
            dx = dx + res_ref[...]
        dx_ref[...] = dx
        dxb_ref[...] = dx.astype(BF16)
        part = jnp.sum(dhf * xhat, axis=0, keepdims=True)

        @pl.when(pl.program_id(0) == 0)
        def _():
            dg_ref[...] = part

        @pl.when(pl.program_id(0) > 0)
        def _():
            dg_ref[...] += part

    row = pl.BlockSpec((tm, d), lambda i: (i, 0))
    vec = pl.BlockSpec((1, d), lambda i: (0, 0))
    in_specs = [row, row, pl.BlockSpec((tm, 1), lambda i: (i, 0)), vec] + ([row] if has_res else [])
    args = (dh, x, rstd, g) + ((res,) if has_res else ())
    if has_more:
        k = more[0].shape[1]
        in_specs += [pl.BlockSpec((tm, k), lambda i: (i, 0)), pl.BlockSpec((d, k), lambda i: (0, 0))]
        args += tuple(more)
    return pl.pallas_call(
        body,
        grid=(s // tm,),
        in_specs=in_specs,
        out_specs=[row, row, vec],
        out_shape=[jax.ShapeDtypeStruct((s, d), F32), jax.ShapeDtypeStruct((s, d), BF16),
                   jax.ShapeDtypeStruct((1, d), F32)],
        compiler_params=_params("arbitrary"),
        name=name,
    )(*args)


def _headnorm_fwd(src, col0, nheads, g, name):
    s = src.shape[0]
    tm = _tile(s, HEADNORM_ROWS)
    w = nheads * HEAD_DIM
    assert col0 % nheads == 0

    def body(x_ref, g_ref, o_ref):
        for hh in range(nheads):
            xf = _head(x_ref, hh).astype(F32)
            r = lax.rsqrt(jnp.mean(xf * xf, axis=-1, keepdims=True) + EPS)
            o_ref[:, hh * HEAD_DIM:(hh + 1) * HEAD_DIM] = ((xf * r) * g_ref[...]).astype(BF16)

    return pl.pallas_call(
        body,
        grid=(s // tm,),
        in_specs=[pl.BlockSpec((tm, w), lambda i: (i, col0 // nheads)),
                  pl.BlockSpec((1, HEAD_DIM), lambda i: (0, 0))],
        out_specs=pl.BlockSpec((tm, w), lambda i: (i, 0)),
        out_shape=jax.ShapeDtypeStruct((s, w), BF16),
        compiler_params=_params("parallel"),
        name=name,
    )(src, g)


def _headnorm_bwd(dxn, src, col0, nheads, g, name):
    s = src.shape[0]
    tm = _tile(s, HEADNORM_ROWS)
    w = nheads * HEAD_DIM
    assert col0 % nheads == 0

    def body(d_ref, x_ref, g_ref, dx_ref, dg_ref):
        part = jnp.zeros((1, HEAD_DIM), F32)
        for hh in range(nheads):
            xf = _head(x_ref, hh).astype(F32)
            r = lax.rsqrt(jnp.mean(xf * xf, axis=-1, keepdims=True) + EPS)
            xhat = xf * r
            dn = _head(d_ref, hh).astype(F32)
            dy = dn * g_ref[...]
            dx = r * (dy - xhat * jnp.mean(dy * xhat, axis=-1, keepdims=True))
            dx_ref[:, hh * HEAD_DIM:(hh + 1) * HEAD_DIM] = dx.astype(BF16)
            part = part + jnp.sum(dn * xhat, axis=0, keepdims=True)

        @pl.when(pl.program_id(0) == 0)
        def _():
            dg_ref[...] = part

        @pl.when(pl.program_id(0) > 0)
        def _():
            dg_ref[...] += part

    return pl.pallas_call(
        body,
        grid=(s // tm,),
        in_specs=[pl.BlockSpec((tm, w), lambda i: (i, 0)),
                  pl.BlockSpec((tm, w), lambda i: (i, col0 // nheads)),
                  pl.BlockSpec((1, HEAD_DIM), lambda i: (0, 0))],
        out_specs=[pl.BlockSpec((tm, w), lambda i: (i, 0)),
                   pl.BlockSpec((1, HEAD_DIM), lambda i: (0, 0))],
        out_shape=[jax.ShapeDtypeStruct((s, w), BF16), jax.ShapeDtypeStruct((1, HEAD_DIM), F32)],
        compiler_params=_params("arbitrary"),
        name=name,
    )(dxn, src, g)


def _tri(t, lower_inclusive):
    r = lax.broadcasted_iota(jnp.int32, (t, t), 0)
    c = lax.broadcasted_iota(jnp.int32, (t, t), 1)
    keep = (c <= r) if lower_inclusive else (c >= r)
    return jnp.where(keep, 1.0, 0.0).astype(BF16)


def _forget_fwd(h, w_f, b_pad):
    s, d = h.shape
    t = _tile(s, ATT_TILE)

    def body(h_ref, w_ref, b_ref, f_ref, c_ref, carry):
        @pl.when(pl.program_id(0) == 0)
        def _():
            carry[...] = jnp.zeros_like(carry)

        f = _dot(h_ref[...], w_ref[...])
        f_ref[...] = f
        lf = _log_sigmoid(f + b_ref[...])
        tri = _tri(t, True)
        acc = carry[...]
        for part in _split3(lf):
            acc = acc + _dot(tri, part)
        c_ref[...] = acc
        carry[...] += jnp.sum(lf, axis=0, keepdims=True)

    blk = pl.BlockSpec((t, LANES), lambda i: (i, 0))
    return pl.pallas_call(
        body,
        grid=(s // t,),
        in_specs=[pl.BlockSpec((t, d), lambda i: (i, 0)), pl.BlockSpec((d, LANES), lambda i: (0, 0)),
                  pl.BlockSpec((1, LANES), lambda i: (0, 0))],
        out_specs=[blk, blk],
        out_shape=[jax.ShapeDtypeStruct((s, LANES), F32)] * 2,
        scratch_shapes=[pltpu.VMEM((1, LANES), F32)],
        compiler_params=_params("arbitrary"),
        name="forget_fwd",
    )(h, w_f, b_pad)


def _forget_bwd(dc, f_logit, b_pad):
    s = f_logit.shape[0]
    t = _tile(s, ATT_TILE)
    nb = s // t

    def body(dc_ref, f_ref, b_ref, df_ref, db_ref, carry):
        @pl.when(pl.program_id(0) == 0)
        def _():
            carry[...] = jnp.zeros_like(carry)
            db_ref[...] = jnp.zeros_like(db_ref)

        d = dc_ref[...]
        tri = _tri(t, False)
        acc = carry[...]
        for part in _split3(d):
            acc = acc + _dot(tri, part)
        z = f_ref[...] + b_ref[...]
        df = acc * jnp.exp(_log_sigmoid(-z))
        df_ref[...] = df
        db_ref[...] += jnp.sum(df, axis=0, keepdims=True)
        carry[...] += jnp.sum(d, axis=0, keepdims=True)

    rev = pl.BlockSpec((t, LANES), lambda i: (nb - 1 - i, 0))
    vec = pl.BlockSpec((1, LANES), lambda i: (0, 0))
    return pl.pallas_call(
        body,
        grid=(nb,),
        in_specs=[rev, rev, vec],
        out_specs=[rev, vec],
        out_shape=[jax.ShapeDtypeStruct((s, LANES), F32), jax.ShapeDtypeStruct((1, LANES), F32)],
        scratch_shapes=[pltpu.VMEM((1, LANES), F32)],
        compiler_params=_params("arbitrary"),
        name="forget_bwd",
    )(dc, f_logit, b_pad)


SB_FWD_GROUP = 6
FOX_GROUP = 6
SB_BWD_GROUP = 3


def _head(ref, hh, rows=slice(None)):
    return ref[rows, hh * HEAD_DIM:(hh + 1) * HEAD_DIM]


def _tri_mask(t, strict):
    r = lax.broadcasted_iota(jnp.int32, (t, t), 0)
    c = lax.broadcasted_iota(jnp.int32, (t, t), 1)
    return (c < r) if strict else (c <= r)


def _fox_fwd(qn, kn, proj, colv, c_col, c_row, nheads, carry=None):
    s = qn.shape[0]
    t = _tile(s, ATT_TILE)
    scale = HEAD_DIM ** -0.5
    hg = FOX_GROUP
    gw = hg * HEAD_DIM
    assert nheads % hg == 0 and colv % hg == 0

    def body(q_ref, k_ref, v_ref, cc_ref, cr_ref, o_ref, of_ref, lse_ref):
        qi = pl.program_id(1)
        causal = _tri_mask(t, False)

        def tile(kj, carry, diagonal):
            off = pl.multiple_of(kj * t, t)
            heads = range(hg)
            rows = pl.ds(off, t)
            qk = [_dot_nt(_head(q_ref, hh), _head(k_ref, hh, rows)) for hh in heads]
            sc = [qk[hh] * scale + (cc_ref[hh] - cr_ref[hh, :, rows]) for hh in heads]
            if diagonal:
                sc = [jnp.where(causal, sc[hh], NEG_BIG) for hh in heads]
            m_new = [jnp.maximum(carry[hh][0], jnp.max(sc[hh], axis=-1, keepdims=True)) for hh in heads]
            p = [jnp.exp(sc[hh] - m_new[hh]) for hh in heads]
            pv = [_dot(p[hh].astype(BF16), _head(v_ref, hh, rows)) for hh in heads]
            out = []
            for hh in heads:
                m, l, acc = carry[hh]
                alpha = jnp.exp(m - m_new[hh])
                out.append((m_new[hh], alpha * l + jnp.sum(p[hh], axis=-1, keepdims=True), alpha * acc + pv[hh]))
            return tuple(out)

        init = tuple((jnp.full((t, 1), NEG_BIG, F32), jnp.zeros((t, 1), F32), jnp.zeros((t, HEAD_DIM), F32))
                     for _ in range(hg))
        carry = lax.fori_loop(0, qi, lambda kj, c: tile(kj, c, False), init)
        carry = tile(qi, carry, True)
        for hh in range(hg):
            m, l, acc = carry[hh]
            o = acc / l
            of_ref[:, hh * HEAD_DIM:(hh + 1) * HEAD_DIM] = o
            o_ref[:, hh * HEAD_DIM:(hh + 1) * HEAD_DIM] = o.astype(BF16)
            lse_ref[hh] = m + jnp.log(l)

    tile_spec = pl.BlockSpec((t, gw), lambda h, i: (i, h))
    w = nheads * HEAD_DIM
    return _call(
        body,
        grid=(nheads // hg, s // t),
        in_specs=[tile_spec,
                  pl.BlockSpec((s, gw), lambda h, i: (0, h), pipeline_mode=pl.Buffered(buffer_count=1)),
                  pl.BlockSpec((s, gw), lambda h, i: (0, colv // hg + h), pipeline_mode=pl.Buffered(buffer_count=1)),
                  pl.BlockSpec((hg, t, 1), lambda h, i: (h, i, 0)),
                  pl.BlockSpec((hg, 1, s), lambda h, i: (h, 0, 0))],
        out_specs=[tile_spec, tile_spec, pl.BlockSpec((hg, t, 1), lambda h, i: (h, i, 0))],
        out_shape=[jax.ShapeDtypeStruct((s, w), BF16), jax.ShapeDtypeStruct((s, w), F32),
                   jax.ShapeDtypeStruct((nheads, s, 1), F32)],
        scratch_shapes=[],
        semantics=("parallel", "parallel"),
        name="fox_fwd",
        args=(qn, kn, proj, c_col, c_row),
        carry=carry,
    )


def _fox_bwd(qn, kn, proj, colv, c_col, c_row, o, do, lse, nheads, carry=None):
    s = qn.shape[0]
    t = _tile(s, ATT_TILE)
    scale = HEAD_DIM ** -0.5
    hg = FOX_GROUP
    gw = hg * HEAD_DIM
    assert nheads % hg == 0 and colv % hg == 0

    def body(q_ref, k_ref, v_ref, cc_ref, cr_ref, o_ref, do_ref, lse_ref,
             dq_ref, dk_ref, dv_ref, drs_ref, dcs_ref):
        qi = pl.program_id(1)

        @pl.when(qi == 0)
        def _():
            dk_ref[...] = jnp.zeros_like(dk_ref)
            dv_ref[...] = jnp.zeros_like(dv_ref)
            dcs_ref[...] = jnp.zeros_like(dcs_ref)

        causal = _tri_mask(t, False)
        delta = [jnp.sum(_head(o_ref, hh) * _head(do_ref, hh).astype(F32), axis=-1, keepdims=True)
                 for hh in range(hg)]

        def tile(kj, carry, diagonal):
            off = pl.multiple_of(kj * t, t)
            heads = range(hg)
            rows = pl.ds(off, t)
            qk = [_dot_nt(_head(q_ref, hh), _head(k_ref, hh, rows)) for hh in heads]
            dp = [_dot_nt(_head(do_ref, hh), _head(v_ref, hh, rows)) for hh in heads]
            p = [jnp.exp(qk[hh] * scale + (cc_ref[hh] - cr_ref[hh, :, rows]) - lse_ref[hh]) for hh in heads]
            if diagonal:
                p = [jnp.where(causal, p[hh], 0.0) for hh in heads]
            ds = [p[hh] * (dp[hh] - delta[hh]) for hh in heads]
            dsb = [ds[hh].astype(BF16) for hh in heads]
            dv = [_dot_tn(p[hh].astype(BF16), _head(do_ref, hh)) for hh in heads]
            dk = [_dot_tn(dsb[hh], _head(q_ref, hh)) * scale for hh in heads]
            dq = [_dot(dsb[hh], _head(k_ref, hh, rows)) * scale for hh in heads]
            for hh in heads:
                cols = slice(hh * HEAD_DIM, (hh + 1) * HEAD_DIM)
                dv_ref[rows, cols] += dv[hh]
                dk_ref[rows, cols] += dk[hh]
                dcs_ref[hh, :, rows] += jnp.sum(ds[hh], axis=0, keepdims=True)
            return tuple((carry[hh][0] + dq[hh], carry[hh][1] + jnp.sum(ds[hh], axis=-1, keepdims=True))
                         for hh in heads)

        init = tuple((jnp.zeros((t, HEAD_DIM), F32), jnp.zeros((t, 1), F32)) for _ in range(hg))
        carry = lax.fori_loop(0, qi, lambda kj, c: tile(kj, c, False), init)
        carry = tile(qi, carry, True)
        for hh in range(hg):
            dq_ref[:, hh * HEAD_DIM:(hh + 1) * HEAD_DIM] = carry[hh][0]
            drs_ref[hh] = carry[hh][1]

    tile_spec = pl.BlockSpec((t, gw), lambda h, i: (i, h))
    once = pl.Buffered(buffer_count=1)
    full = pl.BlockSpec((s, gw), lambda h, i: (0, h), pipeline_mode=once)
    colspec = pl.BlockSpec((hg, t, 1), lambda h, i: (h, i, 0))
    rowspec = pl.BlockSpec((hg, 1, s), lambda h, i: (h, 0, 0))
    w = nheads * HEAD_DIM
    return _call(
        body,
        grid=(nheads // hg, s // t),
        in_specs=[tile_spec, full, pl.BlockSpec((s, gw), lambda h, i: (0, colv // hg + h), pipeline_mode=once),
                  colspec, rowspec,
                  tile_spec, tile_spec, colspec],
        out_specs=[tile_spec, full, full, colspec, rowspec],
        out_shape=[jax.ShapeDtypeStruct((s, w), F32), jax.ShapeDtypeStruct((s, w), F32),
                   jax.ShapeDtypeStruct((s, w), F32), jax.ShapeDtypeStruct((nheads, s, 1), F32),
                   jax.ShapeDtypeStruct((nheads, 1, s), F32)],
        scratch_shapes=[],
        semantics=("arbitrary", "arbitrary"),
        name="fox_bwd",
        args=(qn, kn, proj, c_col, c_row, o, do, lse),
        carry=carry,
    )


def _sb_tile(q, k, scale, later, valid):
    z = _dot_nt(q, k) * scale
    lb = _log_sigmoid(z)
    lm = lb - z
    if valid is not None:
        lm = jnp.where(valid, lm, 0.0)
    suffix = _dot(jnp.concatenate(_split2(lm), axis=1), later)
    return lb, lm, suffix


def _later(t):
    r = lax.broadcasted_iota(jnp.int32, (2 * t, t), 0) % t
    c = lax.broadcasted_iota(jnp.int32, (2 * t, t), 1)
    return jnp.where(r > c, 1.0, 0.0).astype(BF16)


def _sb_fwd(proj, colq, colk, colv, nheads, carry=None):
    s = proj.shape[0]
    t = _tile(s, ATT_TILE)
    scale = HEAD_DIM ** -0.5
    hg = SB_FWD_GROUP
    gw = hg * HEAD_DIM
    assert nheads % hg == 0 and colq % hg == 0 and colk % hg == 0 and colv % hg == 0

    def body(q_ref, k_ref, v_ref, o_ref):
        qi = pl.program_id(1)
        later = _later(t)
        before = _tri_mask(t, True)

        def tile(kj, carry, diagonal):
            off = pl.multiple_of(kj * t, t)
            heads = range(hg)
            z = [_dot_nt(_head(q_ref, hh), _head(k_ref, hh, pl.ds(off, t))) * scale for hh in heads]
            lb = [_log_sigmoid(z[hh]) for hh in heads]
            lm = [lb[hh] - z[hh] for hh in heads]
            if diagonal:
                lm = [jnp.where(before, lm[hh], 0.0) for hh in heads]
            parts = [jnp.concatenate(_split2(lm[hh]), axis=1) for hh in heads]
            suffix = [_dot(parts[hh], later) for hh in heads]
            a = [jnp.exp(lb[hh] + suffix[hh] + carry[hh][0]) for hh in heads]
            if diagonal:
                a = [jnp.where(before, a[hh], 0.0) for hh in heads]
            av = [_dot(a[hh].astype(BF16), _head(v_ref, hh, pl.ds(off, t))) for hh in heads]
            return tuple((carry[hh][0] + jnp.sum(lm[hh], axis=-1, keepdims=True), carry[hh][1] + av[hh])
                         for hh in heads)

        init = tuple((jnp.zeros((t, 1), F32), jnp.zeros((t, HEAD_DIM), F32)) for _ in range(hg))
        carry = tile(qi, init, True)
        carry = lax.fori_loop(1, qi + 1, lambda i, c: tile(qi - i, c, False), carry)
        for hh in range(hg):
            o_ref[:, hh * HEAD_DIM:(hh + 1) * HEAD_DIM] = carry[hh][1].astype(BF16)

    return _call(
        body,
        grid=(nheads // hg, s // t),
        in_specs=[pl.BlockSpec((t, gw), lambda h, i: (i, colq // hg + h)),
                  pl.BlockSpec((s, gw), lambda h, i: (0, colk // hg + h), pipeline_mode=pl.Buffered(buffer_count=1)),
                  pl.BlockSpec((s, gw), lambda h, i: (0, colv // hg + h), pipeline_mode=pl.Buffered(buffer_count=1))],
        out_specs=[pl.BlockSpec((t, gw), lambda h, i: (i, h))],
        out_shape=[jax.ShapeDtypeStruct((s, nheads * HEAD_DIM), BF16)],
        scratch_shapes=[],
        semantics=("parallel", "parallel"),
        name="sb_fwd",
        args=(proj, proj, proj),
        carry=carry,
    )


def _sb_bwd(proj, colq, colk, colv, do, nheads, carry=None):
    s = proj.shape[0]
    t = _tile(s, ATT_TILE)
    scale = HEAD_DIM ** -0.5
    hg = SB_BWD_GROUP
    gw = hg * HEAD_DIM
    assert nheads % hg == 0 and colq % hg == 0 and colk % hg == 0 and colv % hg == 0

    def body(q_ref, k_ref, v_ref, do_ref, dq_ref, dk_ref, dv_ref, g_s, beta_s):
        qi = pl.program_id(1)

        @pl.when(qi == 0)
        def _():
            dk_ref[...] = jnp.zeros_like(dk_ref)
            dv_ref[...] = jnp.zeros_like(dv_ref)

        later = _later(t)
        before = _tri_mask(t, True)

        def back(kj, carry, diagonal):
            off = pl.multiple_of(kj * t, t)
            heads = range(hg)
            rows = pl.ds(off, t)
            z = [_dot_nt(_head(q_ref, hh), _head(k_ref, hh, rows)) * scale for hh in heads]
            da = [_dot_nt(_head(do_ref, hh), _head(v_ref, hh, rows)) for hh in heads]
            lb = [_log_sigmoid(z[hh]) for hh in heads]
            lm = [lb[hh] - z[hh] for hh in heads]
            if diagonal:
                lm = [jnp.where(before, lm[hh], 0.0) for hh in heads]
            parts = [jnp.concatenate(_split2(lm[hh]), axis=1) for hh in heads]
            suffix = [_dot(parts[hh], later) for hh in heads]
            a = [jnp.exp(lb[hh] + suffix[hh] + carry[hh]) for hh in heads]
            if diagonal:
                a = [jnp.where(before, a[hh], 0.0) for hh in heads]
            dv = [_dot_tn(a[hh].astype(BF16), _head(do_ref, hh)) for hh in heads]
            for hh in heads:
                g_s[hh, :, rows] = a[hh] * da[hh]
                beta_s[hh, :, rows] = jnp.exp(lb[hh]).astype(BF16)
            for hh in heads:
                dv_ref[rows, hh * HEAD_DIM:(hh + 1) * HEAD_DIM] += dv[hh]
            return tuple(carry[hh] + jnp.sum(lm[hh], axis=-1, keepdims=True) for hh in heads)

        rc = back(qi, tuple(jnp.zeros((t, 1), F32) for _ in range(hg)), True)
        lax.fori_loop(1, qi + 1, lambda i, c: back(qi - i, c, False), rc)

        earlier = jnp.where(lax.broadcasted_iota(jnp.int32, (2 * t, t), 0) % t
                            < lax.broadcasted_iota(jnp.int32, (2 * t, t), 1), 1.0, 0.0).astype(BF16)

        def fwd(kj, carry, diagonal):
            off = pl.multiple_of(kj * t, t)
            heads = range(hg)
            rows = pl.ds(off, t)
            g = [g_s[hh, :, rows] for hh in heads]
            parts = [jnp.concatenate(_split2(g[hh]), axis=1) for hh in heads]
            gsum = [_dot(parts[hh], earlier) + carry[hh][0] for hh in heads]
            dz = []
            for hh in heads:
                beta = beta_s[hh, :, rows].astype(F32)
                d = g[hh] * (1.0 - beta) - gsum[hh] * beta
                if diagonal:
                    d = jnp.where(before, d, 0.0)
                dz.append(d.astype(BF16))
            dk = [_dot_tn(dz[hh], _head(q_ref, hh)) * scale for hh in heads]
            dq = [_dot(dz[hh], _head(k_ref, hh, rows)) * scale for hh in heads]
            for hh in heads:
                dk_ref[rows, hh * HEAD_DIM:(hh + 1) * HEAD_DIM] += dk[hh]
            return tuple((carry[hh][0] + jnp.sum(g[hh], axis=-1, keepdims=True), carry[hh][1] + dq[hh])
                         for hh in heads)

        init = tuple((jnp.zeros((t, 1), F32), jnp.zeros((t, HEAD_DIM), F32)) for _ in range(hg))
        carry = lax.fori_loop(0, qi, lambda kj, c: fwd(kj, c, False), init)
        carry = fwd(qi, carry, True)
        for hh in range(hg):
            dq_ref[:, hh * HEAD_DIM:(hh + 1) * HEAD_DIM] = carry[hh][1]

    once = pl.Buffered(buffer_count=1)
    tile_spec = pl.BlockSpec((t, gw), lambda h, i: (i, h))
    full = pl.BlockSpec((s, gw), lambda h, i: (0, h), pipeline_mode=once)
    w = nheads * HEAD_DIM
    return _call(
        body,
        grid=(nheads // hg, s // t),
        in_specs=[pl.BlockSpec((t, gw), lambda h, i: (i, colq // hg + h)),
                  pl.BlockSpec((s, gw), lambda h, i: (0, colk // hg + h), pipeline_mode=once),
                  pl.BlockSpec((s, gw), lambda h, i: (0, colv // hg + h), pipeline_mode=once),
                  tile_spec],
        out_specs=[tile_spec, full, full],
        out_shape=[jax.ShapeDtypeStruct((s, w), F32)] * 3,
        scratch_shapes=[pltpu.VMEM((hg, t, s), F32), pltpu.VMEM((hg, t, s), BF16)],
        semantics=("arbitrary", "arbitrary"),
        name="sb_bwd",
        args=(proj, proj, proj, do),
        carry=carry,
    )


def _mem_fwd(qn, kn, mkv, nheads):
    s = qn.shape[0]
    mtok = kn.shape[0]
    t = _tile(s, MEM_ROWS)
    w = nheads * HEAD_DIM
    scale = HEAD_DIM ** -0.5

    def body(q_ref, k_ref, v_ref, o_ref):
        heads = range(nheads)
        sc = [_dot_nt(_head(q_ref, hh), _head(k_ref, hh)) * scale for hh in heads]
        p = [jnp.exp(sc[hh] - jnp.max(sc[hh], axis=-1, keepdims=True)) for hh in heads]
        p = [p[hh] / jnp.sum(p[hh], axis=-1, keepdims=True) for hh in heads]
        o = [_dot(p[hh].astype(BF16), _head(v_ref, hh)) for hh in heads]
        for hh in heads:
            o_ref[:, hh * HEAD_DIM:(hh + 1) * HEAD_DIM] = o[hh].astype(BF16)

    return pl.pallas_call(
        body,
        grid=(s // t,),
        in_specs=[pl.BlockSpec((t, w), lambda i: (i, 0)),
                  pl.BlockSpec((mtok, w), lambda i: (0, 0)),
                  pl.BlockSpec((mtok, w), lambda i: (0, 1))],
        out_specs=pl.BlockSpec((t, w), lambda i: (i, 0)),
        out_shape=jax.ShapeDtypeStruct((s, w), BF16),
        compiler_params=_params("parallel"),
        name="mem_fwd",
    )(qn, kn, mkv)


def _mem_bwd(qn, kn, mkv, do, nheads):
    s = qn.shape[0]
    mtok = kn.shape[0]
    t = _tile(s, MEM_ROWS)
    w = nheads * HEAD_DIM
    scale = HEAD_DIM ** -0.5

    def body(q_ref, k_ref, v_ref, do_ref, dq_ref, dk_ref, dv_ref):
        @pl.when(pl.program_id(0) == 0)
        def _():
            dk_ref[...] = jnp.zeros_like(dk_ref)
            dv_ref[...] = jnp.zeros_like(dv_ref)

        heads = range(nheads)
        sc = [_dot_nt(_head(q_ref, hh), _head(k_ref, hh)) * scale for hh in heads]
        dp = [_dot_nt(_head(do_ref, hh), _head(v_ref, hh)) for hh in heads]
        p = [jnp.exp(sc[hh] - jnp.max(sc[hh], axis=-1, keepdims=True)) for hh in heads]
        p = [p[hh] / jnp.sum(p[hh], axis=-1, keepdims=True) for hh in heads]
        ds = [(p[hh] * (dp[hh] - jnp.sum(p[hh] * dp[hh], axis=-1, keepdims=True))).astype(BF16) for hh in heads]
        dq = [_dot(ds[hh], _head(k_ref, hh)) * scale for hh in heads]
        dk = [_dot_tn(ds[hh], _head(q_ref, hh)) * scale for hh in heads]
        dv = [_dot_tn(p[hh].astype(BF16), _head(do_ref, hh)) for hh in heads]
        for hh in heads:
            cols = slice(hh * HEAD_DIM, (hh + 1) * HEAD_DIM)
            dq_ref[:, cols] = dq[hh]
            dk_ref[:, cols] += dk[hh]
            dv_ref[:, cols] += dv[hh]

    tile = pl.BlockSpec((t, w), lambda i: (i, 0))
    kspec = pl.BlockSpec((mtok, w), lambda i: (0, 0))
    return pl.pallas_call(
        body,
        grid=(s // t,),
        in_specs=[tile, kspec, pl.BlockSpec((mtok, w), lambda i: (0, 1)), tile],
        out_specs=[tile, kspec, kspec],
        out_shape=[jax.ShapeDtypeStruct((s, w), F32), jax.ShapeDtypeStruct((mtok, w), F32),
                   jax.ShapeDtypeStruct((mtok, w), F32)],
        compiler_params=_params("arbitrary"),
        name="mem_bwd",
    )(qn, kn, mkv, do)


def _merge_fwd(p0, p1, p2, gates, b_gate):
    s, d = p0.shape
    tm, tn = _tile(s, ROW_TILE), _tile(d, COL_TILE)
    nj = d // tn

    def body(p0_ref, p1_ref, p2_ref, ga_ref, gb_ref, gc_ref, b_ref, o_ref, sa_ref, sb_ref, sc_ref):
        acc = jnp.zeros((tm, tn), F32)
        for b, (p_ref, g_ref, s_ref) in enumerate(((p0_ref, ga_ref, sa_ref), (p1_ref, gb_ref, sb_ref),
                                                   (p2_ref, gc_ref, sc_ref))):
            gate = jax.nn.sigmoid(g_ref[...].astype(F32) + b_ref[b:b + 1, :])
            s_ref[...] = gate.astype(BF16)
            acc = acc + gate * p_ref[...]
        o_ref[...] = acc.astype(BF16)

    blk = pl.BlockSpec((tm, tn), lambda i, j: (i, j))
    merged, *sig = pl.pallas_call(
        body,
        grid=(s // tm, nj),
        in_specs=[blk] * 6 + [pl.BlockSpec((3, tn), lambda i, j: (0, j))],
        out_specs=[blk] * 4,
        out_shape=[jax.ShapeDtypeStruct((s, d), BF16)] * 4,
        compiler_params=_params("parallel", "parallel"),
        name="merge_fwd",
    )(p0, p1, p2, *gates, b_gate)
    return merged, tuple(sig)


def _merge_bwd(dmerged, p0, p1, p2, sig):
    s, d = p0.shape
    tm, tn = _tile(s, ROW_TILE), _tile(d, COL_TILE)
    nj = d // tn

    def body(dm_ref, p0_ref, p1_ref, p2_ref, ga_ref, gb_ref, gc_ref,
             d0_ref, d1_ref, d2_ref, dga_ref, dgb_ref, dgc_ref, db_ref):
        dm = dm_ref[...].astype(F32)
        parts = []
        for p_ref, g_ref, dp_ref, dg_ref in ((p0_ref, ga_ref, d0_ref, dga_ref), (p1_ref, gb_ref, d1_ref, dgb_ref),
                                             (p2_ref, gc_ref, d2_ref, dgc_ref)):
            gate = g_ref[...].astype(F32)
            dp_ref[...] = (dm * gate).astype(BF16)
            dgate = dm * p_ref[...] * gate * (1.0 - gate)
            dg_ref[...] = dgate.astype(BF16)
            parts.append(jnp.sum(dgate, axis=0, keepdims=True))
        part = jnp.concatenate(parts, axis=0)

        @pl.when(pl.program_id(1) == 0)
        def _():
            db_ref[...] = part

        @pl.when(pl.program_id(1) > 0)
        def _():
            db_ref[...] += part

    blk = pl.BlockSpec((tm, tn), lambda j, i: (i, j))
    bias = pl.BlockSpec((3, tn), lambda j, i: (0, j))
    return pl.pallas_call(
        body,
        grid=(nj, s // tm),
        in_specs=[blk] * 7,
        out_specs=[blk] * 6 + [bias],
        out_shape=[jax.ShapeDtypeStruct((s, d), BF16)] * 6 + [jax.ShapeDtypeStruct((3, d), F32)],
        compiler_params=_params("parallel", "arbitrary"),
        name="merge_bwd",
    )(dmerged, p0, p1, p2, *sig)


def _shift_down(v, n):
    rows = lax.broadcasted_iota(jnp.int32, v.shape, 0)
    return jnp.where(rows >= n, pltpu.roll(v, n, 0), 0.0)


def _shift_up(v, n):
    s = v.shape[0]
    rows = lax.broadcasted_iota(jnp.int32, v.shape, 0)
    return jnp.where(rows < s - n, pltpu.roll(v, s - n, 0), 0.0)


def _conv(v, w_ref, b_ref):
    taps = w_ref.shape[0]
    out = v * w_ref[taps - 1:taps, :] + b_ref[...]
    for n in range(1, taps):
        out = out + _shift_down(v, n) * w_ref[taps - 1 - n:taps - n, :]
    return out


def _conv_act_fwd(up, conv_w, conv_b):
    s, f2 = up.shape
    f = f2 // 2
    tn = LANES
    nj = f // tn
    taps = conv_w.shape[0]

    def body(ug_ref, uv_ref, wg_ref, wv_ref, bg_ref, bv_ref, o_ref, cg_ref, cv_ref):
        cg = _conv(ug_ref[...].astype(F32), wg_ref, bg_ref)
        cv = _conv(uv_ref[...].astype(F32), wv_ref, bv_ref)
        o_ref[...] = (cg * jax.nn.sigmoid(cg) * cv).astype(BF16)
        cg_ref[...] = cg.astype(BF16)
        cv_ref[...] = cv.astype(BF16)

    out = pl.BlockSpec((s, tn), lambda j: (0, j))
    return pl.pallas_call(
        body,
        grid=(nj,),
        in_specs=[pl.BlockSpec((s, tn), lambda j: (0, j)), pl.BlockSpec((s, tn), lambda j: (0, nj + j)),
                  pl.BlockSpec((taps, tn), lambda j: (0, j)), pl.BlockSpec((taps, tn), lambda j: (0, nj + j)),
                  pl.BlockSpec((1, tn), lambda j: (0, j)), pl.BlockSpec((1, tn), lambda j: (0, nj + j))],
        out_specs=[out, out, out],
        out_shape=[jax.ShapeDtypeStruct((s, f), BF16)] * 3,
        compiler_params=_params("parallel"),
        name="conv_act_fwd",
    )(up, up, conv_w, conv_w, conv_b, conv_b)


def _conv_act_bwd(up, conv_w, conv_g, conv_v, dact):
    s, f2 = up.shape
    f = f2 // 2
    tn = LANES
    nj = f // tn
    taps = conv_w.shape[0]

    def half(v, du, w_ref, dup_ref, dw_ref, db_ref):
        dup = du * w_ref[taps - 1:taps, :]
        rows = [None] * taps
        rows[taps - 1] = jnp.sum(du * v, axis=0, keepdims=True)
        for n in range(1, taps):
            later = _shift_up(du, n)
            dup = dup + later * w_ref[taps - 1 - n:taps - n, :]
            rows[taps - 1 - n] = jnp.sum(later * v, axis=0, keepdims=True)
        dup_ref[...] = dup.astype(BF16)
        dw_ref[...] = jnp.concatenate(rows, axis=0)
        db_ref[...] = jnp.sum(du, axis=0, keepdims=True)

    def body(ug_ref, uv_ref, wg_ref, wv_ref, cg_ref, cv_ref, da_ref,
             dug_ref, duv_ref, dwg_ref, dwv_ref, dbg_ref, dbv_ref):
        cg = cg_ref[...].astype(F32)
        cv = cv_ref[...].astype(F32)
        da = da_ref[...].astype(F32)
        sg = jax.nn.sigmoid(cg)
        dcv = da * cg * sg
        dcg = da * cv * (sg + cg * sg * (1.0 - sg))
        half(ug_ref[...].astype(F32), dcg, wg_ref, dug_ref, dwg_ref, dbg_ref)
        half(uv_ref[...].astype(F32), dcv, wv_ref, duv_ref, dwv_ref, dbv_ref)

    lo = lambda rows: pl.BlockSpec((rows, tn), lambda j: (0, j))
    hi = lambda rows: pl.BlockSpec((rows, tn), lambda j: (0, nj + j))
    return pl.pallas_call(
        body,
        grid=(nj,),
        in_specs=[lo(s), hi(s), lo(taps), hi(taps), lo(s), lo(s), lo(s)],
        out_specs=[lo(s), lo(s), lo(taps), lo(taps), lo(1), lo(1)],
        out_shape=[jax.ShapeDtypeStruct((s, f), BF16)] * 2 + [jax.ShapeDtypeStruct((taps, f), F32)] * 2
        + [jax.ShapeDtypeStruct((1, f), F32)] * 2,
        compiler_params=_params("parallel"),
        name="conv_act_bwd",
    )(up, up, conv_w, conv_w, conv_g, conv_v, dact)


def _row_tile(rows, row_bytes, budget):
    if rows * row_bytes <= budget or rows % 8:
        return rows
    best = 8
    for t in range(8, rows, 8):
        if rows % t == 0 and t * row_bytes <= budget:
            best = t
    return best


def _adamw(w, g, m, v, name):
    r, c = w.shape
    tr = _row_tile(r, c * 4, ADAM_BLOCK_BYTES)

    def body(w_ref, g_ref, m_ref, v_ref, d_ref, mo_ref, vo_ref):
        gg = g_ref[...]
        m_new = ADAM_B1 * m_ref[...] + (1.0 - ADAM_B1) * gg
        v_new = ADAM_B2 * v_ref[...] + (1.0 - ADAM_B2) * (gg * gg)
        m_hat = m_new / (1.0 - ADAM_B1 ** ADAM_STEP)
        v_hat = v_new / (1.0 - ADAM_B2 ** ADAM_STEP)
        d_ref[...] = -ADAM_LR * (m_hat / (jnp.sqrt(v_hat) + ADAM_EPS) + ADAM_WD * w_ref[...])
        mo_ref[...] = m_new
        vo_ref[...] = v_new

    blk = pl.BlockSpec((tr, c), lambda i: (i, 0))
    return pl.pallas_call(
        body,
        grid=(r // tr,),
        in_specs=[blk] * 4,
        out_specs=[blk] * 3,
        out_shape=[jax.ShapeDtypeStruct((r, c), F32)] * 3,
        compiler_params=_params("parallel"),
        name=name,
    )(w, g, m, v)


def _add_sibling(g, r1, core, name):
    _, _, h, c = g.shape
    th = _row_tile(h, c * 2, ADAM_BLOCK_BYTES)

    def body(core_ref, g_ref, r_ref, o_ref):
        o_ref[...] = (g_ref[...].astype(F32) + r_ref[...].astype(F32)).astype(BF16)

    return pl.pallas_call(
        body,
        grid_spec=pltpu.PrefetchScalarGridSpec(
            num_scalar_prefetch=1,
            grid=(N_CHIPS, h // th),
            in_specs=[pl.BlockSpec((None, None, th, c), lambda j, i, core_ref: (j, core_ref[0], i, 0)),
                      pl.BlockSpec((None, th, c), lambda j, i, core_ref: (j, i, 0))],
            out_specs=pl.BlockSpec((None, th, c), lambda j, i, core_ref: (j, i, 0)),
        ),
        out_shape=jax.ShapeDtypeStruct((N_CHIPS, h, c), BF16),
        compiler_params=_params("parallel", "parallel"),
        name=name,
    )(core, g, r1)


def _add_chips(hsum, r2, chip_core, name):
    _, h, c = hsum.shape
    th = _row_tile(h, c * 4, ADAM_BLOCK_BYTES)

    def body(sel_ref, own_ref, r_ref, o_ref):
        acc = own_ref[...].astype(F32)
        for j in range(N_CHIPS - 1):
            acc = acc + r_ref[j].astype(F32)
        o_ref[...] = acc

    return pl.pallas_call(
        body,
        grid_spec=pltpu.PrefetchScalarGridSpec(
            num_scalar_prefetch=1,
            grid=(h // th,),
            in_specs=[pl.BlockSpec((None, th, c), lambda i, sel_ref: (sel_ref[0], i, 0)),
                      pl.BlockSpec((N_CHIPS - 1, th, c), lambda i, sel_ref: (0, i, 0))],
            out_specs=pl.BlockSpec((None, th, c), lambda i, sel_ref: (sel_ref[1], i, 0)),
        ),
        out_shape=jax.ShapeDtypeStruct((2, h, c), F32),
        compiler_params=_params("parallel"),
        name=name,
    )(chip_core, hsum, r2)


def _sum_devices(parts):
    _, r, c = parts.shape

    def body(p_ref, o_ref):
        acc = p_ref[0]
        for j in range(1, N_DEV):
            acc = acc + p_ref[j]
        o_ref[...] = acc

    return pl.pallas_call(
        body,
        out_shape=jax.ShapeDtypeStruct((r, c), F32),
        compiler_params=pltpu.CompilerParams(vmem_limit_bytes=VMEM_LIMIT_BYTES),
        name="sum_devices",
    )(parts)


def _join_halves(finals):
    n = len(finals)

    def body(*refs):
        outs = refs[n:2 * n]
        send_sems, recv_sems = refs[2 * n:]
        x, y, c, _ = _place()
        sends = [_remote(outs[i].at[c], outs[i].at[c], send_sems.at[i], recv_sems.at[i], (x, y, 1 - c))
                 for i in range(n)]
        for cp in sends:
            cp.start()
        for i in range(n):
            sends[i].wait_send()
            other = outs[i].at[1 - c]
            _remote(other, other, send_sems.at[i], recv_sems.at[i], (x, y, 1 - c)).wait_recv()

    return pl.pallas_call(
        body,
        in_specs=[ANY] * n,
        out_specs=[ANY] * n,
        out_shape=[jax.ShapeDtypeStruct(f.shape, f.dtype) for f in finals],
        input_output_aliases={i: i for i in range(n)},
        scratch_shapes=[pltpu.SemaphoreType.DMA((n,)), pltpu.SemaphoreType.DMA((n,))],
        name="join_halves",
    )(*finals)


def _gather_small(vec):
    k = N_DEV - 1

    def body(v_ref, o_ref, send_sems, recv_sems, local_sem):
        x, y, c, _ = _place()
        me = 4 * x + 2 * y + c
        local = pltpu.make_async_copy(v_ref, o_ref.at[me], local_sem)
        local.start()
        peers = [(x ^ (r >> 2 & 1), y ^ (r >> 1 & 1), c ^ (r & 1)) for r in range(1, N_DEV)]
        sends = [_remote(v_ref, o_ref.at[me], send_sems.at[j], recv_sems.at[j], p) for j, p in enumerate(peers)]
        for cp in sends:
            cp.start()
        for j, (px, py, pc) in enumerate(peers):
            sends[j].wait_send()
            blk = o_ref.at[4 * px + 2 * py + pc]
            _remote(blk, blk, send_sems.at[j], recv_sems.at[j], (px, py, pc)).wait_recv()
        local.wait()

    return pl.pallas_call(
        body,
        in_specs=[ANY],
        out_specs=ANY,
        out_shape=jax.ShapeDtypeStruct((N_DEV,) + vec.shape, vec.dtype),
        scratch_shapes=[pltpu.SemaphoreType.DMA((k,)), pltpu.SemaphoreType.DMA((k,)), pltpu.SemaphoreType.DMA(())],
        name="gather_small",
    )(vec)


W_IN_GATES = ("w_in_g0", "w_in_g1", "w_in_g2")
ROW_SHARDED = ("w_in_a", "w_in_f") + W_IN_GATES + ("w_mem_kv", "w_out", "w_down")
COL_SHARDED = ("w_br_fox", "w_br_sb", "w_br_mem", "w_up")
BIG = ROW_SHARDED + COL_SHARDED


def _whole(name, a):
    if name in ROW_SHARDED:
        return a.reshape(N_CHIPS * a.shape[1], a.shape[2])
    return a.transpose(1, 0, 2).reshape(a.shape[1], N_CHIPS * a.shape[2])


def _by_shard(name, grad):
    if name in ROW_SHARDED:
        a = grad.reshape(N_CHIPS, grad.shape[0] // N_CHIPS, grad.shape[1])
    else:
        a = grad.reshape(grad.shape[0], N_CHIPS, grad.shape[1] // N_CHIPS).transpose(1, 0, 2)
    return a.reshape(N_CHIPS, 2, a.shape[1] // 2, a.shape[2])


def _sibling_sums(names, split, theirs, core):
    return [_add_sibling(a, r, core, "add_sibling_" + name) for name, a, r in zip(names, split, theirs)]


GATHER_FIRST = ("w_in_a", "w_in_f")
GATHER_EARLY = W_IN_GATES[:2] + ("w_mem_kv",)
GATHER_MIX = ("w_out", "w_br_fox", "w_br_sb", "w_br_mem")
REDUCE_FFN = ("w_down", "w_up")
REDUCE_MIX = ("w_out", "w_br_fox", "w_br_sb", "w_br_mem") + W_IN_GATES
REDUCE_IN = ("w_in_a", "w_in_f", "w_mem_kv")


def _local_step(x, mem, target, w, shard, core, chip_core):
    d = x.shape[1]
    nf = shard["w_br_fox"].shape[0] // HEAD_DIM
    nsb = shard["w_br_sb"].shape[0] // HEAD_DIM
    nm = shard["w_br_mem"].shape[0] // HEAD_DIM
    w = dict(w)

    def take(names, gathered):
        for name, a in zip(names, gathered):
            w[name] = _whole(name, a)

    fq, fk, fv = 0, nf, 2 * nf
    sq, sk, sv = 3 * nf, 3 * nf + nsb, 3 * nf + 2 * nsb
    mq = 3 * nf + 3 * nsb

    h, rstd1, moved = _rms_fwd(x, w["g_mix"], "rms_mix_fwd", carry=_Gather([shard[name] for name in GATHER_FIRST]))
    take(GATHER_FIRST, moved)
    proj, moved = _mm(h, w["w_in_a"], "nn", BF16, "proj_att", carry=_Gather([shard[name] for name in GATHER_EARLY]))
    take(GATHER_EARLY, moved)
    gate0, moved = _mm(h, w["w_in_g0"], "nn", BF16, "proj_gate0", carry=_Gather([shard[W_IN_GATES[2]]]))
    take(W_IN_GATES[2:], moved)
    gates = (gate0, _mm(h, w["w_in_g1"], "nn", BF16, "proj_gate1"), _mm(h, w["w_in_g2"], "nn", BF16, "proj_gate2"))
    f_logit, c_sum = _forget_fwd(h, w["w_in_f"], w["b_forget"])
    c_t = c_sum[:, :nf].T
    c_col, c_row = c_t[:, :, None], c_t[:, None, :]
    qn = _headnorm_fwd(proj, fq, nf, w["g_q_fox"], "fox_qnorm_fwd")
    kn = _headnorm_fwd(proj, fk, nf, w["g_k_fox"], "fox_knorm_fwd")
    (o_fox, o_fox32, lse), moved = _fox_fwd(qn, kn, proj, fv, c_col, c_row, nf,
                                            carry=_Gather([shard[name] for name in GATHER_MIX]))
    take(GATHER_MIX, moved)
    (o_sb,), moved = _sb_fwd(proj, sq, sk, sv, nsb, carry=_Gather([shard["w_up"]]))
    take(("w_up",), moved)
    memn, rstd_m = _rms_fwd(mem, w["g_mem"], "rms_mem_fwd")
    mkv = _mm(memn, w["w_mem_kv"], "nn", BF16, "mem_kv")
    kmn = _headnorm_fwd(mkv, 0, nm, w["g_k_mem"], "mem_knorm_fwd")
    qmn = _headnorm_fwd(proj, mq, nm, w["g_q_mem"], "mem_qnorm_fwd")
    o_mem = _mem_fwd(qmn, kmn, mkv, nm)
    p0 = _mm(o_fox, w["w_br_fox"], "nn", F32, "branch_fox")
    p1 = _mm(o_sb, w["w_br_sb"], "nn", F32, "branch_sb")
    p2 = _mm(o_mem, w["w_br_mem"], "nn", F32, "branch_mem")
    merged, sig = _merge_fwd(p0, p1, p2, gates, w["b_gate"])
    x1 = _mm(merged, w["w_out"], "nn", F32, "out_proj", residual=x)
    h2, rstd2 = _rms_fwd(x1, w["g_ffn"], "rms_ffn_fwd")
    up, moved = _mm(h2, w["w_up"], "nn", BF16, "ffn_up", carry=_Gather([shard["w_down"]]))
    take(("w_down",), moved)
    act, conv_g, conv_v = _conv_act_fwd(up, w["conv_w"], w["conv_b"])
    dy, dyb, lparts = _mm(act, w["w_down"], "nn", F32, "ffn_down_loss", residual=x1, loss_target=target)
    loss = (0.5 / d) * jnp.sum(lparts[::8, ::LANES])

    g = {}
    dact = _mm(dyb, w["w_down"], "nt", BF16, "ffn_down_dx")
    g["w_down"] = _mm(act, dyb, "tn", BF16, "ffn_down_dw")
    dug, duv, dwg, dwv, dbg, dbv = _conv_act_bwd(up, w["conv_w"], conv_g, conv_v, dact)
    dup = jnp.concatenate([dug, duv], axis=1)
    g["conv_w"] = jnp.concatenate([dwg, dwv], axis=1)
    g["conv_b"] = jnp.concatenate([dbg, dbv], axis=1)
    split_down = [_by_shard("w_down", g["w_down"])]
    dh2, theirs_down = _mm(dup, w["w_up"], "nt", BF16, "ffn_up_dx", carry=_Swap(split_down))
    g["w_up"] = _mm(h2, dup, "tn", BF16, "ffn_up_dw")
    split_up = [_by_shard("w_up", g["w_up"])]
    dx1, dx1b, g["g_ffn"] = _rms_bwd(dh2, x1, rstd2, w["g_ffn"], dy, "rms_ffn_bwd")
    dmerged, theirs_up = _mm(dx1b, w["w_out"], "nt", BF16, "out_proj_dx", carry=_Swap(split_up))
    sums_ffn = _sibling_sums(REDUCE_FFN, split_down + split_up, theirs_down + theirs_up, core)
    g["w_out"] = _mm(merged, dx1b, "tn", BF16, "out_proj_dw")
    dp0, dp1, dp2, dga, dgb, dgc, g["b_gate"] = _merge_bwd(dmerged, p0, p1, p2, sig)
    dgates = (dga, dgb, dgc)
    for name, dgate in zip(W_IN_GATES, dgates):
        g[name] = _mm(h, dgate, "tn", BF16, name + "_dw")
    do_fox = _mm(dp0, w["w_br_fox"], "nt", BF16, "branch_fox_dx")
    do_sb = _mm(dp1, w["w_br_sb"], "nt", BF16, "branch_sb_dx")
    do_mem = _mm(dp2, w["w_br_mem"], "nt", BF16, "branch_mem_dx")
    g["w_br_fox"] = _mm(o_fox, dp0, "tn", BF16, "branch_fox_dw")
    g["w_br_sb"] = _mm(o_sb, dp1, "tn", BF16, "branch_sb_dw")
    g["w_br_mem"] = _mm(o_mem, dp2, "tn", BF16, "branch_mem_dw")
    split_mix = [_by_shard(name, g[name]) for name in REDUCE_MIX]

    (dqn, dkn, dfv, drs, dcs), moved = _fox_bwd(qn, kn, proj, fv, c_col, c_row, o_fox32, do_fox, lse, nf,
                                                carry=_Both(_Scatter(sums_ffn[1:]), _Swap(split_mix)))
    others_up, theirs_mix = moved[:1], moved[1:]
    sums_mix = _sibling_sums(REDUCE_MIX, split_mix, theirs_mix, core)
    dfq, g["g_q_fox"] = _headnorm_bwd(dqn, proj, fq, nf, w["g_q_fox"], "fox_qnorm_bwd")
    dfk, g["g_k_fox"] = _headnorm_bwd(dkn, proj, fk, nf, w["g_k_fox"], "fox_knorm_bwd")
    dc = jnp.pad((drs[:, :, 0] - dcs[:, 0, :]).T, ((0, 0), (0, LANES - nf)))
    df, g["b_forget"] = _forget_bwd(dc, f_logit, w["b_forget"])
    (dsq, dsk, dsv), others_mix = _sb_bwd(proj, sq, sk, sv, do_sb, nsb, carry=_Scatter(sums_ffn[:1] + sums_mix))
    others_ffn = others_mix[:1] + others_up
    others_mix = others_mix[1:]
    dqmn, dkmn, dvm = _mem_bwd(qmn, kmn, mkv, do_mem, nm)
    dmq, g["g_q_mem"] = _headnorm_bwd(dqmn, proj, mq, nm, w["g_q_mem"], "mem_qnorm_bwd")
    dkm, g["g_k_mem"] = _headnorm_bwd(dkmn, mkv, 0, nm, w["g_k_mem"], "mem_knorm_bwd")
    dmkv = jnp.concatenate([dkm, dvm.astype(BF16)], axis=1)
    g["w_mem_kv"] = _mm(memn, dmkv, "tn", BF16, "mem_kv_dw")
    dmemn = _mm(dmkv, w["w_mem_kv"], "nt", BF16, "mem_kv_dx")
    _, _, g["g_mem"] = _rms_bwd(dmemn, mem, rstd_m, w["g_mem"], None, "rms_mem_bwd")

    dproj = jnp.concatenate([dfq, dfk, dfv.astype(BF16), dsq.astype(BF16), dsk.astype(BF16), dsv.astype(BF16), dmq],
                            axis=1)
    dfb = df.astype(BF16)
    g["w_in_a"] = _mm(h, dproj, "tn", BF16, "proj_att_dw")
    g["w_in_f"] = _mm(h, dfb, "tn", BF16, "proj_forget_dw")
    split_in = [_by_shard(name, g[name]) for name in REDUCE_IN]
    dh, theirs_in = _mm(dgates[0], w[W_IN_GATES[0]], "nt", F32, W_IN_GATES[0] + "_dx", carry=_Swap(split_in))
    sums_in = _sibling_sums(REDUCE_IN, split_in, theirs_in, core)
    for name, dgate in zip(W_IN_GATES[1:], dgates[1:]):
        dh = _mm(dgate, w[name], "nt", F32, name + "_dx", residual=dh)
    dh, others_in = _mm(dproj, w["w_in_a"], "nt", F32, "proj_att_dx", residual=dh, carry=_Scatter(sums_in))
    grad_x, _, g["g_mix"] = _rms_bwd(dh, x, rstd1, w["g_mix"], dx1, "rms_mix_bwd", more=(dfb, w["w_in_f"]))

    names = REDUCE_FFN + REDUCE_MIX + REDUCE_IN
    finals = [_add_chips(own, theirs, chip_core, "add_chips_" + name)
              for name, own, theirs in zip(names, sums_ffn + sums_mix + sums_in, others_ffn + others_mix + others_in)]
    summed = {name: a.reshape(2 * a.shape[1], a.shape[2]) for name, a in zip(names, _join_halves(finals))}
    return loss, grad_x, g, summed


SMALL = ("g_mix", "b_forget", "g_q_fox", "g_k_fox", "g_mem", "g_q_mem", "g_k_mem", "b_gate", "g_ffn", "conv_w",
         "conv_b")
SMALL_SHARDED = ("b_gate", "conv_w")
PACK_ROWS = 8


def _pack(arrs):
    flat = jnp.concatenate([a.reshape(-1) for a in arrs])
    unit = PACK_ROWS * LANES
    flat = jnp.pad(flat, (0, -flat.shape[0] % unit))
    return flat.reshape(-1, LANES)


def _unpack(packed, shapes):
    flat = packed.reshape(-1)
    out, at = [], 0
    for s in shapes:
        n = 1
        for dim in s:
            n *= dim
        out.append(flat[at:at + n].reshape(s))
        at += n
    return out


def kernel(x, mem, g_mix, w_in, b_forget, g_q_fox, g_k_fox, g_mem, w_mem_kv, g_q_mem, g_k_mem, w_br_fox, w_br_sb, w_br_mem, b_gate, w_out, g_ffn, w_up, conv_w, conv_b, w_down, loss_target, m_g_mix, m_w_in, m_b_forget, m_g_q_fox, m_g_k_fox, m_g_mem, m_w_mem_kv, m_g_q_mem, m_g_k_mem, m_w_br_fox, m_w_br_sb, m_w_br_mem, m_b_gate, m_w_out, m_g_ffn, m_w_up, m_conv_w, m_conv_b, m_w_down, v_g_mix, v_w_in, v_b_forget, v_g_q_fox, v_g_k_fox, v_g_mem, v_w_mem_kv, v_g_q_mem, v_g_k_mem, v_w_br_fox, v_w_br_sb, v_w_br_mem, v_b_gate, v_w_out, v_g_ffn, v_w_up, v_conv_w, v_conv_b, v_w_down):
    given = dict(g_mix=g_mix, w_in=w_in, b_forget=b_forget, g_q_fox=g_q_fox, g_k_fox=g_k_fox, g_mem=g_mem,
                 w_mem_kv=w_mem_kv, g_q_mem=g_q_mem, g_k_mem=g_k_mem, w_br_fox=w_br_fox, w_br_sb=w_br_sb,
                 w_br_mem=w_br_mem, b_gate=b_gate, w_out=w_out, g_ffn=g_ffn, w_up=w_up, conv_w=conv_w, conv_b=conv_b,
                 w_down=w_down)
    m_in = dict(g_mix=m_g_mix, w_in=m_w_in, b_forget=m_b_forget, g_q_fox=m_g_q_fox, g_k_fox=m_g_k_fox, g_mem=m_g_mem,
                w_mem_kv=m_w_mem_kv, g_q_mem=m_g_q_mem, g_k_mem=m_g_k_mem, w_br_fox=m_w_br_fox, w_br_sb=m_w_br_sb,
                w_br_mem=m_w_br_mem, b_gate=m_b_gate, w_out=m_w_out, g_ffn=m_g_ffn, w_up=m_w_up, conv_w=m_conv_w,
                conv_b=m_conv_b, w_down=m_w_down)
    v_in = dict(g_mix=v_g_mix, w_in=v_w_in, b_forget=v_b_forget, g_q_fox=v_g_q_fox, g_k_fox=v_g_k_fox, g_mem=v_g_mem,
                w_mem_kv=v_w_mem_kv, g_q_mem=v_g_q_mem, g_k_mem=v_g_k_mem, w_br_fox=v_w_br_fox, w_br_sb=v_w_br_sb,
                w_br_mem=v_w_br_mem, b_gate=v_b_gate, w_out=v_w_out, g_ffn=v_g_ffn, w_up=v_w_up, conv_w=v_conv_w,
                conv_b=v_conv_b, w_down=v_w_down)
    layered = {k: a.ndim == 3 for k, a in given.items()}
    drop = lambda a: a[0] if a.ndim == 3 else a
    given = {k: drop(a) for k, a in given.items()}
    m_in = {k: drop(a) for k, a in m_in.items()}
    v_in = {k: drop(a) for k, a in v_in.items()}

    xi, yi, ci = lax.axis_index("x"), lax.axis_index("y"), lax.axis_index("c")
    chip = (2 * xi + yi).astype(jnp.int32)
    core_arr = ci.astype(jnp.int32).reshape(1)
    chip_core = jnp.stack([chip, ci.astype(jnp.int32)])

    nf = given["b_forget"].shape[1]
    cut = 3 * given["w_br_fox"].shape[0]

    d_model = given["w_out"].shape[1]
    gate0 = given["w_in"].shape[1] - len(W_IN_GATES) * d_model
    shard = {
        "w_in_a": jnp.concatenate([given["w_in"][:, :cut], given["w_in"][:, cut + nf:gate0]], axis=1).astype(BF16),
        "w_in_f": jnp.pad(given["w_in"][:, cut:cut + nf], ((0, 0), (0, LANES - nf))).astype(BF16),
    }
    for b, name in enumerate(W_IN_GATES):
        shard[name] = given["w_in"][:, gate0 + b * d_model:gate0 + (b + 1) * d_model].astype(BF16)
    for name in BIG:
        if name not in shard:
            shard[name] = given[name].astype(BF16)
    w = {}
    small_shapes = [given[name].shape for name in SMALL_SHARDED]
    small_parts = _gather_small(_pack([given[name] for name in SMALL_SHARDED]))[0::2]
    per_chip = [_unpack(small_parts[j], small_shapes) for j in range(N_CHIPS)]
    for k, name in enumerate(SMALL_SHARDED):
        w[name] = jnp.concatenate([per_chip[j][k] for j in range(N_CHIPS)], axis=1)
    for name in SMALL:
        if name not in SMALL_SHARDED:
            w[name] = given[name]
    w["b_forget"] = jnp.pad(given["b_forget"], ((0, 0), (0, LANES - nf)))

    loss, grad_x, g, summed = _local_step(x[0], mem[0], loss_target[0], w, shard, core_arr, chip_core)
    loss = lax.psum(loss, ("x", "y", "c"))
    grads = {name: summed[name] for name in BIG if name in given}
    grads["w_in"] = jnp.concatenate([summed["w_in_a"][:, :cut], summed["w_in_f"][:, :nf], summed["w_in_a"][:, cut:]]
                                    + [summed[name] for name in W_IN_GATES], axis=1)

    g["b_forget"] = g["b_forget"][:, :nf]
    small_full_shapes = [g[name].shape for name in SMALL]
    small_sum = _unpack(_sum_devices(_gather_small(_pack([g[name] for name in SMALL]))), small_full_shapes)
    for name, a in zip(SMALL, small_sum):
        if name in SMALL_SHARDED:
            width = given[name].shape[1]
            a = lax.dynamic_slice_in_dim(a, chip * width, width, axis=1)
        grads[name] = a

    delta, new_m, new_v = {}, {}, {}
    for name in WEIGHTS:
        if name not in SMALL:
            delta[name], new_m[name], new_v[name] = _adamw(given[name], grads[name], m_in[name], v_in[name],
                                                           "adamw_" + name)
    shapes = [given[name].shape for name in SMALL]
    packed = [_pack([src[name] for name in SMALL]) for src in (given, grads, m_in, v_in)]
    for dst, res in zip((delta, new_m, new_v), _adamw(*packed, "adamw_small")):
        for name, a in zip(SMALL, _unpack(res, shapes)):
            dst[name] = a

    out = [loss, grad_x[None]]
    for src in (grads, delta, new_m, new_v):
        out.extend(src[name][None] if layered[name] else src[name] for name in WEIGHTS)
    return tuple(out)
```

```python
import functools

import jax
import jax.numpy as jnp
from jax import lax
from jax.experimental import pallas as pl
from jax.experimental.pallas import tpu as pltpu

F32 = jnp.float32
BF16 = jnp.bfloat16

HEAD_DIM = 128
EPS = 1e-6
NEG_BIG = -1e30

ADAM_LR = 0.001
ADAM_B1 = 0.9
ADAM_B2 = 0.999
ADAM_EPS = 1e-08
ADAM_WD = 0.01
ADAM_STEP = 10

LANES = 128
BF16_SUBLANES = 16
VMEM_LIMIT_BYTES = 56 * 1024 * 1024
MM_TILE = 1024
MM_TILE_K = {"nn": 2048, "nt": 2816, "tn": 4096}
ATT_TILE = 256
ROW_TILE = 256
HEADNORM_ROWS = 512
MEM_ROWS = 512
COL_TILE = 512
ADAM_BLOCK_BYTES = 2 << 20

N_CHIPS = 4
N_DEV = 8
MESH = pl.DeviceIdType.MESH

IN_NAMES = ['x', 'mem', 'g_mix', 'w_in', 'b_forget', 'g_q_fox', 'g_k_fox', 'g_mem', 'w_mem_kv', 'g_q_mem', 'g_k_mem',
            'w_br_fox', 'w_br_sb', 'w_br_mem', 'b_gate', 'w_out', 'g_ffn', 'w_up', 'conv_w', 'conv_b', 'w_down']
WEIGHTS = IN_NAMES[2:]


def _tile(n, target):
    if n <= target:
        return n
    for t in range(target - target % LANES, LANES - 1, -LANES):
        if n % t == 0:
            return t
    return n


def _params(*sem):
    return pltpu.CompilerParams(dimension_semantics=sem, vmem_limit_bytes=VMEM_LIMIT_BYTES)


def _log_sigmoid(z):
    return jnp.minimum(z, 0.0) - jnp.log(1.0 + jnp.exp(-jnp.abs(z)))


def _split2(v):
    hi = v.astype(BF16)
    lo = (v - hi.astype(F32)).astype(BF16)
    return hi, lo


def _split3(v):
    hi = v.astype(BF16)
    r = v - hi.astype(F32)
    mid = r.astype(BF16)
    lo = (r - mid.astype(F32)).astype(BF16)
    return hi, mid, lo


def _dot(a, b):
    return lax.dot_general(a, b, (((1,), (0,)), ((), ())), preferred_element_type=F32)


def _dot_nt(a, b):
    return lax.dot_general(a, b, (((1,), (1,)), ((), ())), preferred_element_type=F32)


def _dot_tn(a, b):
    return lax.dot_general(a, b, (((0,), (0,)), ((), ())), preferred_element_type=F32)


ANY = pl.BlockSpec(memory_space=pl.ANY)


def _place():
    x, y, c = lax.axis_index("x"), lax.axis_index("y"), lax.axis_index("c")
    others = [(1 - x, y), (x, 1 - y), (1 - x, 1 - y)]
    return x, y, c, others


def _remote(src, dst, send_sem, recv_sem, to):
    return pltpu.make_async_remote_copy(src_ref=src, dst_ref=dst, send_sem=send_sem, recv_sem=recv_sem,
                                        device_id=to, device_id_type=MESH)


class _Gather:
    PER_SHARD = 7

    def __init__(self, shards):
        self.inputs = list(shards)
        n = len(shards) * self.PER_SHARD
        self.out_shapes = [jax.ShapeDtypeStruct((N_CHIPS,) + s.shape, s.dtype) for s in shards]
        self.scratch = [pltpu.SemaphoreType.DMA((n,)), pltpu.SemaphoreType.DMA((n,))]

    def _first(self, ins, outs, sems):
        send_sems, recv_sems = sems
        x, y, c, others = _place()
        me = 2 * x + y
        k = self.PER_SHARD
        copies = []
        for i in range(len(ins)):
            h = ins[i].shape[0] // 2
            mine = pl.ds(pl.multiple_of(c * h, BF16_SUBLANES), h)
            for j, (ox, oy) in enumerate(others):
                copies.append(_remote(ins[i].at[mine], outs[i].at[me, mine], send_sems.at[k * i + j],
                                      recv_sems.at[k * i + j], (ox, oy, c)))
            copies.append(_remote(ins[i], outs[i].at[me], send_sems.at[k * i + 6], recv_sems.at[k * i + 6],
                                  (x, y, 1 - c)))
        return copies

    def start(self, ins, outs, sems):
        for cp in self._first(ins, outs, sems):
            cp.start()

    def finish(self, ins, outs, sems):
        send_sems, recv_sems = sems
        x, y, c, others = _place()
        me = 2 * x + y
        sibling = (x, y, 1 - c)
        k = self.PER_SHARD
        passed = []
        for i in range(len(ins)):
            h = ins[i].shape[0] // 2
            mine = pl.ds(pl.multiple_of(c * h, BF16_SUBLANES), h)
            for j, (ox, oy) in enumerate(others):
                blk = outs[i].at[2 * ox + oy, mine]
                _remote(blk, blk, send_sems.at[k * i + j], recv_sems.at[k * i + j], (ox, oy, c)).wait_recv()
                cp = _remote(blk, blk, send_sems.at[k * i + 3 + j], recv_sems.at[k * i + 3 + j], sibling)
                cp.start()
                passed.append(cp)
        for i in range(len(ins)):
            h = ins[i].shape[0] // 2
            theirs = pl.ds(pl.multiple_of((1 - c) * h, BF16_SUBLANES), h)
            for j, (ox, oy) in enumerate(others):
                blk = outs[i].at[2 * ox + oy, theirs]
                _remote(blk, blk, send_sems.at[k * i + 3 + j], recv_sems.at[k * i + 3 + j], sibling).wait_recv()
            own = outs[i].at[me]
            _remote(own, own, send_sems.at[k * i + 6], recv_sems.at[k * i + 6], sibling).wait_recv()
        for cp in self._first(ins, outs, sems) + passed:
            cp.wait_send()


class _Scatter:
    def __init__(self, sums):
        self.inputs = list(sums)
        k = N_CHIPS - 1
        self.out_shapes = [jax.ShapeDtypeStruct((k,) + g.shape[1:], g.dtype) for g in sums]
        self.scratch = [pltpu.SemaphoreType.DMA((k * len(sums),)), pltpu.SemaphoreType.DMA((k * len(sums),))]

    def _copies(self, ins, outs, sems):
        send_sems, recv_sems = sems
        _, _, c, others = _place()
        k = N_CHIPS - 1
        return [_remote(ins[i].at[2 * ox + oy], outs[i].at[j], send_sems.at[k * i + j], recv_sems.at[k * i + j],
                        (ox, oy, c))
                for i in range(len(ins)) for j, (ox, oy) in enumerate(others)]

    def start(self, ins, outs, sems):
        for cp in self._copies(ins, outs, sems):
            cp.start()

    def finish(self, ins, outs, sems):
        for cp in self._copies(ins, outs, sems):
            cp.wait()


class _Swap:
    def __init__(self, grads):
        self.inputs = list(grads)
        n = len(grads)
        self.out_shapes = [jax.ShapeDtypeStruct((g.shape[0],) + g.shape[2:], g.dtype) for g in grads]
        self.scratch = [pltpu.SemaphoreType.DMA((n,)), pltpu.SemaphoreType.DMA((n,))]

    def _copies(self, ins, outs, sems):
        send_sems, recv_sems = sems
        x, y, c, _ = _place()
        return [_remote(ins[i].at[:, 1 - c], outs[i], send_sems.at[i], recv_sems.at[i], (x, y, 1 - c))
                for i in range(len(ins))]

    def start(self, ins, outs, sems):
        for cp in self._copies(ins, outs, sems):
            cp.start()

    def finish(self, ins, outs, sems):
        for cp in self._copies(ins, outs, sems):
            cp.wait()


class _Both:
    def __init__(self, first, second):
        self.parts = (first, second)
        self.inputs = first.inputs + second.inputs
        self.out_shapes = first.out_shapes + second.out_shapes
        self.scratch = first.scratch + second.scratch

    def _each(self, ins, outs, sems):
        first = self.parts[0]
        a, b, c = len(first.inputs), len(first.out_shapes), len(first.scratch)
        return ((first, ins[:a], outs[:b], sems[:c]), (self.parts[1], ins[a:], outs[b:], sems[c:]))

    def start(self, ins, outs, sems):
        for part, i, o, s in self._each(ins, outs, sems):
            part.start(i, o, s)

    def finish(self, ins, outs, sems):
        for part, i, o, s in self._each(ins, outs, sems):
            part.finish(i, o, s)


def _call(body, *, grid, in_specs, out_specs, out_shape, scratch_shapes, semantics, name, args, carry=None):
    n_in, n_out, n_scr = len(in_specs), len(out_specs), len(scratch_shapes)
    if carry is None:
        res = pl.pallas_call(body, grid=grid, in_specs=in_specs, out_specs=out_specs, out_shape=out_shape,
                             scratch_shapes=scratch_shapes, compiler_params=_params(*semantics), name=name)(*args)
        return list(res), []
    nci, nco = len(carry.inputs), len(carry.out_shapes)
    a, b = n_in, n_in + nci
    c, d = b + n_out, b + n_out + nco
    e = d + n_scr

    def carried(*refs):
        ids = [pl.program_id(k) for k in range(len(grid))]
        first = functools.reduce(jnp.logical_and, [i == 0 for i in ids])
        last = functools.reduce(jnp.logical_and, [i == n - 1 for i, n in zip(ids, grid)])

        @pl.when(first)
        def _():
            carry.start(refs[a:b], refs[c:d], refs[e:])

        body(*refs[:a], *refs[b:c], *refs[d:e])

        @pl.when(last)
        def _():
            carry.finish(refs[a:b], refs[c:d], refs[e:])

    res = pl.pallas_call(
        carried,
        grid=grid,
        in_specs=list(in_specs) + [ANY] * nci,
        out_specs=list(out_specs) + [ANY] * nco,
        out_shape=list(out_shape) + carry.out_shapes,
        scratch_shapes=list(scratch_shapes) + carry.scratch,
        compiler_params=_params(*(["arbitrary"] * len(grid))),
        name=name,
    )(*args, *carry.inputs)
    return list(res[:n_out]), list(res[n_out:])


def _mm(a, b, mode, out_dtype, name, residual=None, carry=None, loss_target=None):
    if mode == "nn":
        (m, k), (k2, n) = a.shape, b.shape
    elif mode == "nt":
        (m, k), (n, k2) = a.shape, b.shape
    else:
        (k, m), (k2, n) = a.shape, b.shape
    assert k == k2, (a.shape, b.shape, mode)
    has_res = residual is not None
    has_loss = loss_target is not None
    n_in = 2 + has_res + has_loss
    tm, tn, tk = _tile(m, MM_TILE), _tile(n, MM_TILE), _tile(k, MM_TILE_K[mode])
    nk = k // tk
    dot = {"nn": _dot, "nt": _dot_nt, "tn": _dot_tn}[mode]

    def body(*refs):
        a_ref, b_ref = refs[:2]
        r_ref = refs[2] if has_res else None
        t_ref = refs[n_in - 1] if has_loss else None
        o_ref = refs[n_in]

        def finish(acc):
            if has_res:
                acc = acc + r_ref[...]
            if has_loss:
                err = acc - t_ref[...]
                dy = err * (1.0 / n)
                o_ref[...] = dy
                refs[n_in + 1][...] = dy.astype(BF16)
                tot = jnp.sum(jnp.sum(err * err, axis=-1, keepdims=True), axis=0, keepdims=True)
                refs[n_in + 2][...] = jnp.broadcast_to(tot, (8, LANES))
            else:
                o_ref[...] = acc.astype(o_ref.dtype)

        part = dot(a_ref[...], b_ref[...])
        if nk == 1:
            finish(part)
        else:
            acc_ref = refs[-1]
            kk = pl.program_id(2)

            @pl.when(kk == 0)
            def _():
                acc_ref[...] = part

            @pl.when(kk > 0)
            def _():
                acc_ref[...] += part

            @pl.when(kk == nk - 1)
            def _():
                finish(acc_ref[...])

    if mode == "tn":
        a_spec = pl.BlockSpec((tk, tm), lambda j, i, kk: (kk, i))
    else:
        a_spec = pl.BlockSpec((tm, tk), lambda j, i, kk: (i, kk))
    if mode == "nt":
        b_spec = pl.BlockSpec((tn, tk), lambda j, i, kk: (j, kk))
    else:
        b_spec = pl.BlockSpec((tk, tn), lambda j, i, kk: (kk, j))
    o_spec = pl.BlockSpec((tm, tn), lambda j, i, kk: (i, j))
    in_specs = [a_spec, b_spec] + [o_spec] * (has_res + has_loss)
    args = (a, b) + ((residual,) if has_res else ()) + ((loss_target,) if has_loss else ())
    out_specs, out_shape = [o_spec], [jax.ShapeDtypeStruct((m, n), out_dtype)]
    if has_loss:
        out_specs += [o_spec, pl.BlockSpec((8, LANES), lambda j, i, kk: (i, j))]
        out_shape += [jax.ShapeDtypeStruct((m, n), BF16), jax.ShapeDtypeStruct((m // tm * 8, n // tn * LANES), F32)]
    outs, moved = _call(
        body,
        grid=(n // tn, m // tm, nk),
        in_specs=in_specs,
        out_specs=out_specs,
        out_shape=out_shape,
        scratch_shapes=[pltpu.VMEM((tm, tn), F32)] if nk > 1 else [],
        semantics=("parallel", "parallel", "arbitrary"),
        name=name,
        args=args,
        carry=carry,
    )
    out = outs if has_loss else outs[0]
    return out if carry is None else (out, moved)


def _rms_fwd(x, g, name, carry=None):
    s, d = x.shape
    tm = _tile(s, ROW_TILE)

    def body(x_ref, g_ref, h_ref, r_ref):
        xf = x_ref[...]
        r = lax.rsqrt(jnp.mean(xf * xf, axis=-1, keepdims=True) + EPS)
        h_ref[...] = ((xf * r) * g_ref[...]).astype(BF16)
        r_ref[...] = r

    (h, rstd), moved = _call(
        body,
        grid=(s // tm,),
        in_specs=[pl.BlockSpec((tm, d), lambda i: (i, 0)), pl.BlockSpec((1, d), lambda i: (0, 0))],
        out_specs=[pl.BlockSpec((tm, d), lambda i: (i, 0)), pl.BlockSpec((tm, 1), lambda i: (i, 0))],
        out_shape=[jax.ShapeDtypeStruct((s, d), BF16), jax.ShapeDtypeStruct((s, 1), F32)],
        scratch_shapes=[],
        semantics=("parallel",),
        name=name,
        args=(x, g),
        carry=carry,
    )
    return (h, rstd) if carry is None else (h, rstd, moved)


def _rms_bwd(dh, x, rstd, g, res, name, more=None):
    s, d = x.shape
    tm = _tile(s, ROW_TILE)
    has_res = res is not None
    has_more = more is not None
    n_in = 4 + has_res + 2 * has_more

    def body(*refs):
        dh_ref, x_ref, r_ref, g_ref = refs[:4]
        res_ref = refs[4] if has_res else None
        dx_ref, dxb_ref, dg_ref = refs[n_in:]
        dhf = dh_ref[...].astype(F32)
        if has_more:
            dhf = dhf + _dot_nt(refs[n_in - 2][...], refs[n_in - 1][...])
        xhat = x_ref[...] * r_ref[...]
        dy = dhf * g_ref[...]
        dx = r_ref[...] * (dy - xhat * jnp.mean(dy * xhat, axis=-1, keepdims=True))
        if has_res:
            dx = dx + res_ref[...]
        dx_ref[...] = dx
        dxb_ref[...] = dx.astype(BF16)
        part = jnp.sum(dhf * xhat, axis=0, keepdims=True)

        @pl.when(pl.program_id(0) == 0)
        def _():
            dg_ref[...] = part

        @pl.when(pl.program_id(0) > 0)
        def _():
            dg_ref[...] += part

    row = pl.BlockSpec((tm, d), lambda i: (i, 0))
    vec = pl.BlockSpec((1, d), lambda i: (0, 0))
    in_specs = [row, row, pl.BlockSpec((tm, 1), lambda i: (i, 0)), vec] + ([row] if has_res else [])
    args = (dh, x, rstd, g) + ((res,) if has_res else ())
    if has_more:
        k = more[0].shape[1]
        in_specs += [pl.BlockSpec((tm, k), lambda i: (i, 0)), pl.BlockSpec((d, k), lambda i: (0, 0))]
        args += tuple(more)
    return pl.pallas_call(
        body,
        grid=(s // tm,),
        in_specs=in_specs,
        out_specs=[row, row, vec],
        out_shape=[jax.ShapeDtypeStruct((s, d), F32), jax.ShapeDtypeStruct((s, d), BF16),
                   jax.ShapeDtypeStruct((1, d), F32)],
        compiler_params=_params("arbitrary"),
        name=name,
    )(*args)


def _headnorm_fwd(src, col0, nheads, g, name):
    s = src.shape[0]
    tm = _tile(s, HEADNORM_ROWS)
    w = nheads * HEAD_DIM
    assert col0 % nheads == 0

    def body(x_ref, g_ref, o_ref):
        for hh in range(nheads):
            xf = _head(x_ref, hh).astype(F32)
            r = lax.rsqrt(jnp.mean(xf * xf, axis=-1, keepdims=True) + EPS)
            o_ref[:, hh * HEAD_DIM:(hh + 1) * HEAD_DIM] = ((xf * r) * g_ref[...]).astype(BF16)

    return pl.pallas_call(
        body,
        grid=(s // tm,),
        in_specs=[pl.BlockSpec((tm, w), lambda i: (i, col0 // nheads)),
                  pl.BlockSpec((1, HEAD_DIM), lambda i: (0, 0))],
        out_specs=pl.BlockSpec((tm, w), lambda i: (i, 0)),
        out_shape=jax.ShapeDtypeStruct((s, w), BF16),
        compiler_params=_params("parallel"),
        name=name,
    )(src, g)


def _headnorm_bwd(dxn, src, col0, nheads, g, name):
    s = src.shape[0]
    tm = _tile(s, HEADNORM_ROWS)
    w = nheads * HEAD_DIM
    assert col0 % nheads == 0

    def body(d_ref, x_ref, g_ref, dx_ref, dg_ref):
        part = jnp.zeros((1, HEAD_DIM), F32)
        for hh in range(nheads):
            xf = _head(x_ref, hh).astype(F32)
            r = lax.rsqrt(jnp.mean(xf * xf, axis=-1, keepdims=True) + EPS)
            xhat = xf * r
            dn = _head(d_ref, hh).astype(F32)
            dy = dn * g_ref[...]
            dx = r * (dy - xhat * jnp.mean(dy * xhat, axis=-1, keepdims=True))
            dx_ref[:, hh * HEAD_DIM:(hh + 1) * HEAD_DIM] = dx.astype(BF16)
            part = part + jnp.sum(dn * xhat, axis=0, keepdims=True)

        @pl.when(pl.program_id(0) == 0)
        def _():
            dg_ref[...] = part

        @pl.when(pl.program_id(0) > 0)
        def _():
            dg_ref[...] += part

    return pl.pallas_call(
        body,
        grid=(s // tm,),
        in_specs=[pl.BlockSpec((tm, w), lambda i: (i, 0)),
                  pl.BlockSpec((tm, w), lambda i: (i, col0 // nheads)),
                  pl.BlockSpec((1, HEAD_DIM), lambda i: (0, 0))],
        out_specs=[pl.BlockSpec((tm, w), lambda i: (i, 0)),
                   pl.BlockSpec((1, HEAD_DIM), lambda i: (0, 0))],
        out_shape=[jax.ShapeDtypeStruct((s, w), BF16), jax.ShapeDtypeStruct((1, HEAD_DIM), F32)],
        compiler_params=_params("arbitrary"),
        name=name,
    )(dxn, src, g)


def _tri(t, lower_inclusive):
    r = lax.broadcasted_iota(jnp.int32, (t, t), 0)
    c = lax.broadcasted_iota(jnp.int32, (t, t), 1)
    keep = (c <= r) if lower_inclusive else (c >= r)
    return jnp.where(keep, 1.0, 0.0).astype(BF16)


def _forget_fwd(h, w_f, b_pad):
    s, d = h.shape
    t = _tile(s, ATT_TILE)

    def body(h_ref, w_ref, b_ref, f_ref, c_ref, carry):
        @pl.when(pl.program_id(0) == 0)
        def _():
            carry[...] = jnp.zeros_like(carry)

        f = _dot(h_ref[...], w_ref[...])
        f_ref[...] = f
        lf = _log_sigmoid(f + b_ref[...])
        tri = _tri(t, True)
        acc = carry[...]
        for part in _split3(lf):
            acc = acc + _dot(tri, part)
        c_ref[...] = acc
        carry[...] += jnp.sum(lf, axis=0, keepdims=True)

    blk = pl.BlockSpec((t, LANES), lambda i: (i, 0))
    return pl.pallas_call(
        body,
        grid=(s // t,),
        in_specs=[pl.BlockSpec((t, d), lambda i: (i, 0)), pl.BlockSpec((d, LANES), lambda i: (0, 0)),
                  pl.BlockSpec((1, LANES), lambda i: (0, 0))],
        out_specs=[blk, blk],
        out_shape=[jax.ShapeDtypeStruct((s, LANES), F32)] * 2,
        scratch_shapes=[pltpu.VMEM((1, LANES), F32)],
        compiler_params=_params("arbitrary"),
        name="forget_fwd",
    )(h, w_f, b_pad)


def _forget_bwd(dc, f_logit, b_pad):
    s = f_logit.shape[0]
    t = _tile(s, ATT_TILE)
    nb = s // t

    def body(dc_ref, f_ref, b_ref, df_ref, db_ref, carry):
        @pl.when(pl.program_id(0) == 0)
        def _():
            carry[...] = jnp.zeros_like(carry)
            db_ref[...] = jnp.zeros_like(db_ref)

        d = dc_ref[...]
        tri = _tri(t, False)
        acc = carry[...]
        for part in _split3(d):
            acc = acc + _dot(tri, part)
        z = f_ref[...] + b_ref[...]
        df = acc * jnp.exp(_log_sigmoid(-z))
        df_ref[...] = df
        db_ref[...] += jnp.sum(df, axis=0, keepdims=True)
        carry[...] += jnp.sum(d, axis=0, keepdims=True)

    rev = pl.BlockSpec((t, LANES), lambda i: (nb - 1 - i, 0))
    vec = pl.BlockSpec((1, LANES), lambda i: (0, 0))
    return pl.pallas_call(
        body,
        grid=(nb,),
        in_specs=[rev, rev, vec],
        out_specs=[rev, vec],
        out_shape=[jax.ShapeDtypeStruct((s, LANES), F32), jax.ShapeDtypeStruct((1, LANES), F32)],
        scratch_shapes=[pltpu.VMEM((1, LANES), F32)],
        compiler_params=_params("arbitrary"),
        name="forget_bwd",
    )(dc, f_logit, b_pad)


SB_FWD_GROUP = 6
FOX_GROUP = 6
SB_BWD_GROUP = 3


def _head(ref, hh, rows=slice(None)):
    return ref[rows, hh * HEAD_DIM:(hh + 1) * HEAD_DIM]


def _tri_mask(t, strict):
    r = lax.broadcasted_iota(jnp.int32, (t, t), 0)
    c = lax.broadcasted_iota(jnp.int32, (t, t), 1)
    return (c < r) if strict else (c <= r)


def _fox_fwd(qn, kn, proj, colv, c_col, c_row, nheads, carry=None):
    s = qn.shape[0]
    t = _tile(s, ATT_TILE)
    scale = HEAD_DIM ** -0.5
    hg = FOX_GROUP
    gw = hg * HEAD_DIM
    assert nheads % hg == 0 and colv % hg == 0

    def body(q_ref, k_ref, v_ref, cc_ref, cr_ref, o_ref, of_ref, lse_ref):
        qi = pl.program_id(1)
        causal = _tri_mask(t, False)

        def tile(kj, carry, diagonal):
            off = pl.multiple_of(kj * t, t)
            heads = range(hg)
            rows = pl.ds(off, t)
            qk = [_dot_nt(_head(q_ref, hh), _head(k_ref, hh, rows)) for hh in heads]
            sc = [qk[hh] * scale + (cc_ref[hh] - cr_ref[hh, :, rows]) for hh in heads]
            if diagonal:
                sc = [jnp.where(causal, sc[hh], NEG_BIG) for hh in heads]
            m_new = [jnp.maximum(carry[hh][0], jnp.max(sc[hh], axis=-1, keepdims=True)) for hh in heads]
            p = [jnp.exp(sc[hh] - m_new[hh]) for hh in heads]
            pv = [_dot(p[hh].astype(BF16), _head(v_ref, hh, rows)) for hh in heads]
            out = []
            for hh in heads:
                m, l, acc = carry[hh]
                alpha = jnp.exp(m - m_new[hh])
                out.append((m_new[hh], alpha * l + jnp.sum(p[hh], axis=-1, keepdims=True), alpha * acc + pv[hh]))
            return tuple(out)

        init = tuple((jnp.full((t, 1), NEG_BIG, F32), jnp.zeros((t, 1), F32), jnp.zeros((t, HEAD_DIM), F32))
                     for _ in range(hg))
        carry = lax.fori_loop(0, qi, lambda kj, c: tile(kj, c, False), init)
        carry = tile(qi, carry, True)
        for hh in range(hg):
            m, l, acc = carry[hh]
            o = acc / l
            of_ref[:, hh * HEAD_DIM:(hh + 1) * HEAD_DIM] = o
            o_ref[:, hh * HEAD_DIM:(hh + 1) * HEAD_DIM] = o.astype(BF16)
            lse_ref[hh] = m + jnp.log(l)

    tile_spec = pl.BlockSpec((t, gw), lambda h, i: (i, h))
    w = nheads * HEAD_DIM
    return _call(
        body,
        grid=(nheads // hg, s // t),
        in_specs=[tile_spec,
                  pl.BlockSpec((s, gw), lambda h, i: (0, h), pipeline_mode=pl.Buffered(buffer_count=1)),
                  pl.BlockSpec((s, gw), lambda h, i: (0, colv // hg + h), pipeline_mode=pl.Buffered(buffer_count=1)),
                  pl.BlockSpec((hg, t, 1), lambda h, i: (h, i, 0)),
                  pl.BlockSpec((hg, 1, s), lambda h, i: (h, 0, 0))],
        out_specs=[tile_spec, tile_spec, pl.BlockSpec((hg, t, 1), lambda h, i: (h, i, 0))],
        out_shape=[jax.ShapeDtypeStruct((s, w), BF16), jax.ShapeDtypeStruct((s, w), F32),
                   jax.ShapeDtypeStruct((nheads, s, 1), F32)],
        scratch_shapes=[],
        semantics=("parallel", "parallel"),
        name="fox_fwd",
        args=(qn, kn, proj, c_col, c_row),
        carry=carry,
    )


def _fox_bwd(qn, kn, proj, colv, c_col, c_row, o, do, lse, nheads, carry=None):
    s = qn.shape[0]
    t = _tile(s, ATT_TILE)
    scale = HEAD_DIM ** -0.5
    hg = FOX_GROUP
    gw = hg * HEAD_DIM
    assert nheads % hg == 0 and colv % hg == 0

    def body(q_ref, k_ref, v_ref, cc_ref, cr_ref, o_ref, do_ref, lse_ref,
             dq_ref, dk_ref, dv_ref, drs_ref, dcs_ref):
        qi = pl.program_id(1)

        @pl.when(qi == 0)
        def _():
            dk_ref[...] = jnp.zeros_like(dk_ref)
            dv_ref[...] = jnp.zeros_like(dv_ref)
            dcs_ref[...] = jnp.zeros_like(dcs_ref)

        causal = _tri_mask(t, False)
        delta = [jnp.sum(_head(o_ref, hh) * _head(do_ref, hh).astype(F32), axis=-1, keepdims=True)
                 for hh in range(hg)]

        def tile(kj, carry, diagonal):
            off = pl.multiple_of(kj * t, t)
            heads = range(hg)
            rows = pl.ds(off, t)
            qk = [_dot_nt(_head(q_ref, hh), _head(k_ref, hh, rows)) for hh in heads]
            dp = [_dot_nt(_head(do_ref, hh), _head(v_ref, hh, rows)) for hh in heads]
            p = [jnp.exp(qk[hh] * scale + (cc_ref[hh] - cr_ref[hh, :, rows]) - lse_ref[hh]) for hh in heads]
            if diagonal:
                p = [jnp.where(causal, p[hh], 0.0) for hh in heads]
            ds = [p[hh] * (dp[hh] - delta[hh]) for hh in heads]
            dsb = [ds[hh].astype(BF16) for hh in heads]
            dv = [_dot_tn(p[hh].astype(BF16), _head(do_ref, hh)) for hh in heads]
            dk = [_dot_tn(dsb[hh], _head(q_ref, hh)) * scale for hh in heads]
            dq = [_dot(dsb[hh], _head(k_ref, hh, rows)) * scale for hh in heads]
            for hh in heads:
                cols = slice(hh * HEAD_DIM, (hh + 1) * HEAD_DIM)
                dv_ref[rows, cols] += dv[hh]
                dk_ref[rows, cols] += dk[hh]
                dcs_ref[hh, :, rows] += jnp.sum(ds[hh], axis=0, keepdims=True)
            return tuple((carry[hh][0] + dq[hh], carry[hh][1] + jnp.sum(ds[hh], axis=-1, keepdims=True))
                         for hh in heads)

        init = tuple((jnp.zeros((t, HEAD_DIM), F32), jnp.zeros((t, 1), F32)) for _ in range(hg))
        carry = lax.fori_loop(0, qi, lambda kj, c: tile(kj, c, False), init)
        carry = tile(qi, carry, True)
        for hh in range(hg):
            dq_ref[:, hh * HEAD_DIM:(hh + 1) * HEAD_DIM] = carry[hh][0]
            drs_ref[hh] = carry[hh][1]

    tile_spec = pl.BlockSpec((t, gw), lambda h, i: (i, h))
    once = pl.Buffered(buffer_count=1)
    full = pl.BlockSpec((s, gw), lambda h, i: (0, h), pipeline_mode=once)
    colspec = pl.BlockSpec((hg, t, 1), lambda h, i: (h, i, 0))
    rowspec = pl.BlockSpec((hg, 1, s), lambda h, i: (h, 0, 0))
    w = nheads * HEAD_DIM
    return _call(
        body,
        grid=(nheads // hg, s // t),
        in_specs=[tile_spec, full, pl.BlockSpec((s, gw), lambda h, i: (0, colv // hg + h), pipeline_mode=once),
                  colspec, rowspec,
                  tile_spec, tile_spec, colspec],
        out_specs=[tile_spec, full, full, colspec, rowspec],
        out_shape=[jax.ShapeDtypeStruct((s, w), F32), jax.ShapeDtypeStruct((s, w), F32),
                   jax.ShapeDtypeStruct((s, w), F32), jax.ShapeDtypeStruct((nheads, s, 1), F32),
                   jax.ShapeDtypeStruct((nheads, 1, s), F32)],
        scratch_shapes=[],
        semantics=("arbitrary", "arbitrary"),
        name="fox_bwd",
        args=(qn, kn, proj, c_col, c_row, o, do, lse),
        carry=carry,
    )


def _sb_tile(q, k, scale, later, valid):
    z = _dot_nt(q, k) * scale
    lb = _log_sigmoid(z)
    lm = lb - z
    if valid is not None:
        lm = jnp.where(valid, lm, 0.0)
    suffix = _dot(jnp.concatenate(_split2(lm), axis=1), later)
    return lb, lm, suffix


def _later(t):
    r = lax.broadcasted_iota(jnp.int32, (2 * t, t), 0) % t
    c = lax.broadcasted_iota(jnp.int32, (2 * t, t), 1)
    return jnp.where(r > c, 1.0, 0.0).astype(BF16)


def _sb_fwd(proj, colq, colk, colv, nheads, carry=None):
    s = proj.shape[0]
    t = _tile(s, ATT_TILE)
    scale = HEAD_DIM ** -0.5
    hg = SB_FWD_GROUP
    gw = hg * HEAD_DIM
    assert nheads % hg == 0 and colq % hg == 0 and colk % hg == 0 and colv % hg == 0

    def body(q_ref, k_ref, v_ref, o_ref):
        qi = pl.program_id(1)
        later = _later(t)
        before = _tri_mask(t, True)

        def tile(kj, carry, diagonal):
            off = pl.multiple_of(kj * t, t)
            heads = range(hg)
            z = [_dot_nt(_head(q_ref, hh), _head(k_ref, hh, pl.ds(off, t))) * scale for hh in heads]
            lb = [_log_sigmoid(z[hh]) for hh in heads]
            lm = [lb[hh] - z[hh] for hh in heads]
            if diagonal:
                lm = [jnp.where(before, lm[hh], 0.0) for hh in heads]
            parts = [jnp.concatenate(_split2(lm[hh]), axis=1) for hh in heads]
            suffix = [_dot(parts[hh], later) for hh in heads]
            a = [jnp.exp(lb[hh] + suffix[hh] + carry[hh][0]) for hh in heads]
            if diagonal:
                a = [jnp.where(before, a[hh], 0.0) for hh in heads]
            av = [_dot(a[hh].astype(BF16), _head(v_ref, hh, pl.ds(off, t))) for hh in heads]
            return tuple((carry[hh][0] + jnp.sum(lm[hh], axis=-1, keepdims=True), carry[hh][1] + av[hh])
                         for hh in heads)

        init = tuple((jnp.zeros((t, 1), F32), jnp.zeros((t, HEAD_DIM), F32)) for _ in range(hg))
        carry = tile(qi, init, True)
        carry = lax.fori_loop(1, qi + 1, lambda i, c: tile(qi - i, c, False), carry)
        for hh in range(hg):
            o_ref[:, hh * HEAD_DIM:(hh + 1) * HEAD_DIM] = carry[hh][1].astype(BF16)

    return _call(
        body,
        grid=(nheads // hg, s // t),
        in_specs=[pl.BlockSpec((t, gw), lambda h, i: (i, colq // hg + h)),
                  pl.BlockSpec((s, gw), lambda h, i: (0, colk // hg + h), pipeline_mode=pl.Buffered(buffer_count=1)),
                  pl.BlockSpec((s, gw), lambda h, i: (0, colv // hg + h), pipeline_mode=pl.Buffered(buffer_count=1))],
        out_specs=[pl.BlockSpec((t, gw), lambda h, i: (i, h))],
        out_shape=[jax.ShapeDtypeStruct((s, nheads * HEAD_DIM), BF16)],
        scratch_shapes=[],
        semantics=("parallel", "parallel"),
        name="sb_fwd",
        args=(proj, proj, proj),
        carry=carry,
    )


def _sb_bwd(proj, colq, colk, colv, do, nheads, carry=None):
    s = proj.shape[0]
    t = _tile(s, ATT_TILE)
    scale = HEAD_DIM ** -0.5
    hg = SB_BWD_GROUP
    gw = hg * HEAD_DIM
    assert nheads % hg == 0 and colq % hg == 0 and colk % hg == 0 and colv % hg == 0

    def body(q_ref, k_ref, v_ref, do_ref, dq_ref, dk_ref, dv_ref, g_s, beta_s):
        qi = pl.program_id(1)

        @pl.when(qi == 0)
        def _():
            dk_ref[...] = jnp.zeros_like(dk_ref)
            dv_ref[...] = jnp.zeros_like(dv_ref)

        later = _later(t)
        before = _tri_mask(t, True)

        def back(kj, carry, diagonal):
            off = pl.multiple_of(kj * t, t)
            heads = range(hg)
            rows = pl.ds(off, t)
            z = [_dot_nt(_head(q_ref, hh), _head(k_ref, hh, rows)) * scale for hh in heads]
            da = [_dot_nt(_head(do_ref, hh), _head(v_ref, hh, rows)) for hh in heads]
            lb = [_log_sigmoid(z[hh]) for hh in heads]
            lm = [lb[hh] - z[hh] for hh in heads]
            if diagonal:
                lm = [jnp.where(before, lm[hh], 0.0) for hh in heads]
            parts = [jnp.concatenate(_split2(lm[hh]), axis=1) for hh in heads]
            suffix = [_dot(parts[hh], later) for hh in heads]
            a = [jnp.exp(lb[hh] + suffix[hh] + carry[hh]) for hh in heads]
            if diagonal:
                a = [jnp.where(before, a[hh], 0.0) for hh in heads]
            dv = [_dot_tn(a[hh].astype(BF16), _head(do_ref, hh)) for hh in heads]
            for hh in heads:
                g_s[hh, :, rows] = a[hh] * da[hh]
                beta_s[hh, :, rows] = jnp.exp(lb[hh]).astype(BF16)
            for hh in heads:
                dv_ref[rows, hh * HEAD_DIM:(hh + 1) * HEAD_DIM] += dv[hh]
            return tuple(carry[hh] + jnp.sum(lm[hh], axis=-1, keepdims=True) for hh in heads)

        rc = back(qi, tuple(jnp.zeros((t, 1), F32) for _ in range(hg)), True)
        lax.fori_loop(1, qi + 1, lambda i, c: back(qi - i, c, False), rc)

        earlier = jnp.where(lax.broadcasted_iota(jnp.int32, (2 * t, t), 0) % t
                            < lax.broadcasted_iota(jnp.int32, (2 * t, t), 1), 1.0, 0.0).astype(BF16)

        def fwd(kj, carry, diagonal):
            off = pl.multiple_of(kj * t, t)
            heads = range(hg)
            rows = pl.ds(off, t)
            g = [g_s[hh, :, rows] for hh in heads]
            parts = [jnp.concatenate(_split2(g[hh]), axis=1) for hh in heads]
            gsum = [_dot(parts[hh], earlier) + carry[hh][0] for hh in heads]
            dz = []
            for hh in heads:
                beta = beta_s[hh, :, rows].astype(F32)
                d = g[hh] * (1.0 - beta) - gsum[hh] * beta
                if diagonal:
                    d = jnp.where(before, d, 0.0)
                dz.append(d.astype(BF16))
            dk = [_dot_tn(dz[hh], _head(q_ref, hh)) * scale for hh in heads]
            dq = [_dot(dz[hh], _head(k_ref, hh, rows)) * scale for hh in heads]
            for hh in heads:
                dk_ref[rows, hh * HEAD_DIM:(hh + 1) * HEAD_DIM] += dk[hh]
            return tuple((carry[hh][0] + jnp.sum(g[hh], axis=-1, keepdims=True), carry[hh][1] + dq[hh])
                         for hh in heads)

        init = tuple((jnp.zeros((t, 1), F32), jnp.zeros((t, HEAD_DIM), F32)) for _ in range(hg))
        carry = lax.fori_loop(0, qi, lambda kj, c: fwd(kj, c, False), init)
        carry = fwd(qi, carry, True)
        for hh in range(hg):
            dq_ref[:, hh * HEAD_DIM:(hh + 1) * HEAD_DIM] = carry[hh][1]

    once = pl.Buffered(buffer_count=1)
    tile_spec = pl.BlockSpec((t, gw), lambda h, i: (i, h))
    full = pl.BlockSpec((s, gw), lambda h, i: (0, h), pipeline_mode=once)
    w = nheads * HEAD_DIM
    return _call(
        body,
        grid=(nheads // hg, s // t),
        in_specs=[pl.BlockSpec((t, gw), lambda h, i: (i, colq // hg + h)),
                  pl.BlockSpec((s, gw), lambda h, i: (0, colk // hg + h), pipeline_mode=once),
                  pl.BlockSpec((s, gw), lambda h, i: (0, colv // hg + h), pipeline_mode=once),
                  tile_spec],
        out_specs=[tile_spec, full, full],
        out_shape=[jax.ShapeDtypeStruct((s, w), F32)] * 3,
        scratch_shapes=[pltpu.VMEM((hg, t, s), F32), pltpu.VMEM((hg, t, s), BF16)],
        semantics=("arbitrary", "arbitrary"),
        name="sb_bwd",
        args=(proj, proj, proj, do),
        carry=carry,
    )


def _mem_fwd(qn, kn, mkv, nheads):
    s = qn.shape[0]
    mtok = kn.shape[0]
    t = _tile(s, MEM_ROWS)
    w = nheads * HEAD_DIM
    scale = HEAD_DIM ** -0.5

    def body(q_ref, k_ref, v_ref, o_ref):
        heads = range(nheads)
        sc = [_dot_nt(_head(q_ref, hh), _head(k_ref, hh)) * scale for hh in heads]
        p = [jnp.exp(sc[hh] - jnp.max(sc[hh], axis=-1, keepdims=True)) for hh in heads]
        p = [p[hh] / jnp.sum(p[hh], axis=-1, keepdims=True) for hh in heads]
        o = [_dot(p[hh].astype(BF16), _head(v_ref, hh)) for hh in heads]
        for hh in heads:
            o_ref[:, hh * HEAD_DIM:(hh + 1) * HEAD_DIM] = o[hh].astype(BF16)

    return pl.pallas_call(
        body,
        grid=(s // t,),
        in_specs=[pl.BlockSpec((t, w), lambda i: (i, 0)),
                  pl.BlockSpec((mtok, w), lambda i: (0, 0)),
                  pl.BlockSpec((mtok, w), lambda i: (0, 1))],
        out_specs=pl.BlockSpec((t, w), lambda i: (i, 0)),
        out_shape=jax.ShapeDtypeStruct((s, w), BF16),
        compiler_params=_params("parallel"),
        name="mem_fwd",
    )(qn, kn, mkv)


def _mem_bwd(qn, kn, mkv, do, nheads):
    s = qn.shape[0]
    mtok = kn.shape[0]
    t = _tile(s, MEM_ROWS)
    w = nheads * HEAD_DIM
    scale = HEAD_DIM ** -0.5

    def body(q_ref, k_ref, v_ref, do_ref, dq_ref, dk_ref, dv_ref):
        @pl.when(pl.program_id(0) == 0)
        def _():
            dk_ref[...] = jnp.zeros_like(dk_ref)
            dv_ref[...] = jnp.zeros_like(dv_ref)

        heads = range(nheads)
        sc = [_dot_nt(_head(q_ref, hh), _head(k_ref, hh)) * scale for hh in heads]
        dp = [_dot_nt(_head(do_ref, hh), _head(v_ref, hh)) for hh in heads]
        p = [jnp.exp(sc[hh] - jnp.max(sc[hh], axis=-1, keepdims=True)) for hh in heads]
        p = [p[hh] / jnp.sum(p[hh], axis=-1, keepdims=True) for hh in heads]
        ds = [(p[hh] * (dp[hh] - jnp.sum(p[hh] * dp[hh], axis=-1, keepdims=True))).astype(BF16) for hh in heads]
        dq = [_dot(ds[hh], _head(k_ref, hh)) * scale for hh in heads]
        dk = [_dot_tn(ds[hh], _head(q_ref, hh)) * scale for hh in heads]
        dv = [_dot_tn(p[hh].astype(BF16), _head(do_ref, hh)) for hh in heads]
        for hh in heads:
            cols = slice(hh * HEAD_DIM, (hh + 1) * HEAD_DIM)
            dq_ref[:, cols] = dq[hh]
            dk_ref[:, cols] += dk[hh]
            dv_ref[:, cols] += dv[hh]

    tile = pl.BlockSpec((t, w), lambda i: (i, 0))
    kspec = pl.BlockSpec((mtok, w), lambda i: (0, 0))
    return pl.pallas_call(
        body,
        grid=(s // t,),
        in_specs=[tile, kspec, pl.BlockSpec((mtok, w), lambda i: (0, 1)), tile],
        out_specs=[tile, kspec, kspec],
        out_shape=[jax.ShapeDtypeStruct((s, w), F32), jax.ShapeDtypeStruct((mtok, w), F32),
                   jax.ShapeDtypeStruct((mtok, w), F32)],
        compiler_params=_params("arbitrary"),
        name="mem_bwd",
    )(qn, kn, mkv, do)


def _merge_fwd(p0, p1, p2, gates, b_gate):
    s, d = p0.shape
    tm, tn = _tile(s, ROW_TILE), _tile(d, COL_TILE)
    nj = d // tn

    def body(p0_ref, p1_ref, p2_ref, ga_ref, gb_ref, gc_ref, b_ref, o_ref, sa_ref, sb_ref, sc_ref):
        acc = jnp.zeros((tm, tn), F32)
        for b, (p_ref, g_ref, s_ref) in enumerate(((p0_ref, ga_ref, sa_ref), (p1_ref, gb_ref, sb_ref),
                                                   (p2_ref, gc_ref, sc_ref))):
            gate = jax.nn.sigmoid(g_ref[...].astype(F32) + b_ref[b:b + 1, :])
            s_ref[...] = gate.astype(BF16)
            acc = acc + gate * p_ref[...]
        o_ref[...] = acc.astype(BF16)

    blk = pl.BlockSpec((tm, tn), lambda i, j: (i, j))
    merged, *sig = pl.pallas_call(
        body,
        grid=(s // tm, nj),
        in_specs=[blk] * 6 + [pl.BlockSpec((3, tn), lambda i, j: (0, j))],
        out_specs=[blk] * 4,
        out_shape=[jax.ShapeDtypeStruct((s, d), BF16)] * 4,
        compiler_params=_params("parallel", "parallel"),
        name="merge_fwd",
    )(p0, p1, p2, *gates, b_gate)
    return merged, tuple(sig)


def _merge_bwd(dmerged, p0, p1, p2, sig):
    s, d = p0.shape
    tm, tn = _tile(s, ROW_TILE), _tile(d, COL_TILE)
    nj = d // tn

    def body(dm_ref, p0_ref, p1_ref, p2_ref, ga_ref, gb_ref, gc_ref,
             d0_ref, d1_ref, d2_ref, dga_ref, dgb_ref, dgc_ref, db_ref):
        dm = dm_ref[...].astype(F32)
        parts = []
        for p_ref, g_ref, dp_ref, dg_ref in ((p0_ref, ga_ref, d0_ref, dga_ref), (p1_ref, gb_ref, d1_ref, dgb_ref),
                                             (p2_ref, gc_ref, d2_ref, dgc_ref)):
            gate = g_ref[...].astype(F32)
            dp_ref[...] = (dm * gate).astype(BF16)
            dgate = dm * p_ref[...] * gate * (1.0 - gate)
            dg_ref[...] = dgate.astype(BF16)
            parts.append(jnp.sum(dgate, axis=0, keepdims=True))
        part = jnp.concatenate(parts, axis=0)

        @pl.when(pl.program_id(1) == 0)
        def _():
            db_ref[...] = part

        @pl.when(pl.program_id(1) > 0)
        def _():
            db_ref[...] += part

    blk = pl.BlockSpec((tm, tn), lambda j, i: (i, j))
    bias = pl.BlockSpec((3, tn), lambda j, i: (0, j))
    return pl.pallas_call(
        body,
        grid=(nj, s // tm),
        in_specs=[blk] * 7,
        out_specs=[blk] * 6 + [bias],
        out_shape=[jax.ShapeDtypeStruct((s, d), BF16)] * 6 + [jax.ShapeDtypeStruct((3, d), F32)],
        compiler_params=_params("parallel", "arbitrary"),
        name="merge_bwd",
    )(dmerged, p0, p1, p2, *sig)


def _shift_down(v, n):
    rows = lax.broadcasted_iota(jnp.int32, v.shape, 0)
    return jnp.where(rows >= n, pltpu.roll(v, n, 0), 0.0)


def _shift_up(v, n):
    s = v.shape[0]
    rows = lax.broadcasted_iota(jnp.int32, v.shape, 0)
    return jnp.where(rows < s - n, pltpu.roll(v, s - n, 0), 0.0)


def _conv(v, w_ref, b_ref):
    taps = w_ref.shape[0]
    out = v * w_ref[taps - 1:taps, :] + b_ref[...]
    for n in range(1, taps):
        out = out + _shift_down(v, n) * w_ref[taps - 1 - n:taps - n, :]
    return out


def _conv_act_fwd(up, conv_w, conv_b):
    s, f2 = up.shape
    f = f2 // 2
    tn = LANES
    nj = f // tn
    taps = conv_w.shape[0]

    def body(ug_ref, uv_ref, wg_ref, wv_ref, bg_ref, bv_ref, o_ref, cg_ref, cv_ref):
        cg = _conv(ug_ref[...].astype(F32), wg_ref, bg_ref)
        cv = _conv(uv_ref[...].astype(F32), wv_ref, bv_ref)
        o_ref[...] = (cg * jax.nn.sigmoid(cg) * cv).astype(BF16)
        cg_ref[...] = cg.astype(BF16)
        cv_ref[...] = cv.astype(BF16)

    out = pl.BlockSpec((s, tn), lambda j: (0, j))
    return pl.pallas_call(
        body,
        grid=(nj,),
        in_specs=[pl.BlockSpec((s, tn), lambda j: (0, j)), pl.BlockSpec((s, tn), lambda j: (0, nj + j)),
                  pl.BlockSpec((taps, tn), lambda j: (0, j)), pl.BlockSpec((taps, tn), lambda j: (0, nj + j)),
                  pl.BlockSpec((1, tn), lambda j: (0, j)), pl.BlockSpec((1, tn), lambda j: (0, nj + j))],
        out_specs=[out, out, out],
        out_shape=[jax.ShapeDtypeStruct((s, f), BF16)] * 3,
        compiler_params=_params("parallel"),
        name="conv_act_fwd",
    )(up, up, conv_w, conv_w, conv_b, conv_b)


def _conv_act_bwd(up, conv_w, conv_g, conv_v, dact):
    s, f2 = up.shape
    f = f2 // 2
    tn = LANES
    nj = f // tn
    taps = conv_w.shape[0]

    def half(v, du, w_ref, dup_ref, dw_ref, db_ref):
        dup = du * w_ref[taps - 1:taps, :]
        rows = [None] * taps
        rows[taps - 1] = jnp.sum(du * v, axis=0, keepdims=True)
        for n in range(1, taps):
            later = _shift_up(du, n)
            dup = dup + later * w_ref[taps - 1 - n:taps - n, :]
            rows[taps - 1 - n] = jnp.sum(later * v, axis=0, keepdims=True)
        dup_ref[...] = dup.astype(BF16)
        dw_ref[...] = jnp.concatenate(rows, axis=0)
        db_ref[...] = jnp.sum(du, axis=0, keepdims=True)

    def body(ug_ref, uv_ref, wg_ref, wv_ref, cg_ref, cv_ref, da_ref,
             dug_ref, duv_ref, dwg_ref, dwv_ref, dbg_ref, dbv_ref):
        cg = cg_ref[...].astype(F32)
        cv = cv_ref[...].astype(F32)
        da = da_ref[...].astype(F32)
        sg = jax.nn.sigmoid(cg)
        dcv = da * cg * sg
        dcg = da * cv * (sg + cg * sg * (1.0 - sg))
        half(ug_ref[...].astype(F32), dcg, wg_ref, dug_ref, dwg_ref, dbg_ref)
        half(uv_ref[...].astype(F32), dcv, wv_ref, duv_ref, dwv_ref, dbv_ref)

    lo = lambda rows: pl.BlockSpec((rows, tn), lambda j: (0, j))
    hi = lambda rows: pl.BlockSpec((rows, tn), lambda j: (0, nj + j))
    return pl.pallas_call(
        body,
        grid=(nj,),
        in_specs=[lo(s), hi(s), lo(taps), hi(taps), lo(s), lo(s), lo(s)],
        out_specs=[lo(s), lo(s), lo(taps), lo(taps), lo(1), lo(1)],
        out_shape=[jax.ShapeDtypeStruct((s, f), BF16)] * 2 + [jax.ShapeDtypeStruct((taps, f), F32)] * 2
        + [jax.ShapeDtypeStruct((1, f), F32)] * 2,
        compiler_params=_params("parallel"),
        name="conv_act_bwd",
    )(up, up, conv_w, conv_w, conv_g, conv_v, dact)


def _row_tile(rows, row_bytes, budget):
    if rows * row_bytes <= budget or rows % 8:
        return rows
    best = 8
    for t in range(8, rows, 8):
        if rows % t == 0 and t * row_bytes <= budget:
            best = t
    return best


def _adamw(w, g, m, v, name):
    r, c = w.shape
    tr = _row_tile(r, c * 4, ADAM_BLOCK_BYTES)

    def body(w_ref, g_ref, m_ref, v_ref, d_ref, mo_ref, vo_ref):
        gg = g_ref[...]
        m_new = ADAM_B1 * m_ref[...] + (1.0 - ADAM_B1) * gg
        v_new = ADAM_B2 * v_ref[...] + (1.0 - ADAM_B2) * (gg * gg)
        m_hat = m_new / (1.0 - ADAM_B1 ** ADAM_STEP)
        v_hat = v_new / (1.0 - ADAM_B2 ** ADAM_STEP)
        d_ref[...] = -ADAM_LR * (m_hat / (jnp.sqrt(v_hat) + ADAM_EPS) + ADAM_WD * w_ref[...])
        mo_ref[...] = m_new
        vo_ref[...] = v_new

    blk = pl.BlockSpec((tr, c), lambda i: (i, 0))
    return pl.pallas_call(
        body,
        grid=(r // tr,),
        in_specs=[blk] * 4,
        out_specs=[blk] * 3,
        out_shape=[jax.ShapeDtypeStruct((r, c), F32)] * 3,
        compiler_params=_params("parallel"),
        name=name,
    )(w, g, m, v)


def _add_sibling(g, r1, core, name):
    _, _, h, c = g.shape
    th = _row_tile(h, c * 2, ADAM_BLOCK_BYTES)

    def body(core_ref, g_ref, r_ref, o_ref):
        o_ref[...] = (g_ref[...].astype(F32) + r_ref[...].astype(F32)).astype(BF16)

    return pl.pallas_call(
        body,
        grid_spec=pltpu.PrefetchScalarGridSpec(
            num_scalar_prefetch=1,
            grid=(N_CHIPS, h // th),
            in_specs=[pl.BlockSpec((None, None, th, c), lambda j, i, core_ref: (j, core_ref[0], i, 0)),
                      pl.BlockSpec((None, th, c), lambda j, i, core_ref: (j, i, 0))],
            out_specs=pl.BlockSpec((None, th, c), lambda j, i, core_ref: (j, i, 0)),
        ),
        out_shape=jax.ShapeDtypeStruct((N_CHIPS, h, c), BF16),
        compiler_params=_params("parallel", "parallel"),
        name=name,
    )(core, g, r1)


def _add_chips(hsum, r2, chip_core, name):
    _, h, c = hsum.shape
    th = _row_tile(h, c * 4, ADAM_BLOCK_BYTES)

    def body(sel_ref, own_ref, r_ref, o_ref):
        acc = own_ref[...].astype(F32)
        for j in range(N_CHIPS - 1):
            acc = acc + r_ref[j].astype(F32)
        o_ref[...] = acc

    return pl.pallas_call(
        body,
        grid_spec=pltpu.PrefetchScalarGridSpec(
            num_scalar_prefetch=1,
            grid=(h // th,),
            in_specs=[pl.BlockSpec((None, th, c), lambda i, sel_ref: (sel_ref[0], i, 0)),
                      pl.BlockSpec((N_CHIPS - 1, th, c), lambda i, sel_ref: (0, i, 0))],
            out_specs=pl.BlockSpec((None, th, c), lambda i, sel_ref: (sel_ref[1], i, 0)),
        ),
        out_shape=jax.ShapeDtypeStruct((2, h, c), F32),
        compiler_params=_params("parallel"),
        name=name,
    )(chip_core, hsum, r2)


def _sum_devices(parts):
    _, r, c = parts.shape

    def body(p_ref, o_ref):
        acc = p_ref[0]
        for j in range(1, N_DEV):
            acc = acc + p_ref[j]
        o_ref[...] = acc

    return pl.pallas_call(
        body,
        out_shape=jax.ShapeDtypeStruct((r, c), F32),
        compiler_params=pltpu.CompilerParams(vmem_limit_bytes=VMEM_LIMIT_BYTES),
        name="sum_devices",
    )(parts)


def _join_halves(finals):
    n = len(finals)

    def body(*refs):
        outs = refs[n:2 * n]
        send_sems, recv_sems = refs[2 * n:]
        x, y, c, _ = _place()
        sends = [_remote(outs[i].at[c], outs[i].at[c], send_sems.at[i], recv_sems.at[i], (x, y, 1 - c))
                 for i in range(n)]
        for cp in sends:
            cp.start()
        for i in range(n):
            sends[i].wait_send()
            other = outs[i].at[1 - c]
            _remote(other, other, send_sems.at[i], recv_sems.at[i], (x, y, 1 - c)).wait_recv()

    return pl.pallas_call(
        body,
        in_specs=[ANY] * n,
        out_specs=[ANY] * n,
        out_shape=[jax.ShapeDtypeStruct(f.shape, f.dtype) for f in finals],
        input_output_aliases={i: i for i in range(n)},
        scratch_shapes=[pltpu.SemaphoreType.DMA((n,)), pltpu.SemaphoreType.DMA((n,))],
        name="join_halves",
    )(*finals)


def _gather_small(vec):
    k = N_DEV - 1

    def body(v_ref, o_ref, send_sems, recv_sems, local_sem):
        x, y, c, _ = _place()
        me = 4 * x + 2 * y + c
        local = pltpu.make_async_copy(v_ref, o_ref.at[me], local_sem)
        local.start()
        peers = [(x ^ (r >> 2 & 1), y ^ (r >> 1 & 1), c ^ (r & 1)) for r in range(1, N_DEV)]
        sends = [_remote(v_ref, o_ref.at[me], send_sems.at[j], recv_sems.at[j], p) for j, p in enumerate(peers)]
        for cp in sends:
            cp.start()
        for j, (px, py, pc) in enumerate(peers):
            sends[j].wait_send()
            blk = o_ref.at[4 * px + 2 * py + pc]
            _remote(blk, blk, send_sems.at[j], recv_sems.at[j], (px, py, pc)).wait_recv()
        local.wait()

    return pl.pallas_call(
        body,
        in_specs=[ANY],
        out_specs=ANY,
        out_shape=jax.ShapeDtypeStruct((N_DEV,) + vec.shape, vec.dtype),
        scratch_shapes=[pltpu.SemaphoreType.DMA((k,)), pltpu.SemaphoreType.DMA((k,)), pltpu.SemaphoreType.DMA(())],
        name="gather_small",
    )(vec)


W_IN_GATES = ("w_in_g0", "w_in_g1", "w_in_g2")
ROW_SHARDED = ("w_in_a", "w_in_f") + W_IN_GATES + ("w_mem_kv", "w_out", "w_down")
COL_SHARDED = ("w_br_fox", "w_br_sb", "w_br_mem", "w_up")
BIG = ROW_SHARDED + COL_SHARDED


def _whole(name, a):
    if name in ROW_SHARDED:
        return a.reshape(N_CHIPS * a.shape[1], a.shape[2])
    return a.transpose(1, 0, 2).reshape(a.shape[1], N_CHIPS * a.shape[2])


def _by_shard(name, grad):
    if name in ROW_SHARDED:
        a = grad.reshape(N_CHIPS, grad.shape[0] // N_CHIPS, grad.shape[1])
    else:
        a = grad.reshape(grad.shape[0], N_CHIPS, grad.shape[1] // N_CHIPS).transpose(1, 0, 2)
    return a.reshape(N_CHIPS, 2, a.shape[1] // 2, a.shape[2])


def _sibling_sums(names, split, theirs, core):
    return [_add_sibling(a, r, core, "add_sibling_" + name) for name, a, r in zip(names, split, theirs)]


GATHER_FIRST = ("w_in_a", "w_in_f")
GATHER_EARLY = W_IN_GATES[:2]
GATHER_MIX = ("w_out", "w_br_fox", "w_br_sb", "w_br_mem") + W_IN_GATES[2:]
REDUCE_FFN = ("w_down", "w_up")
REDUCE_MIX = ("w_out", "w_br_fox", "w_br_sb", "w_br_mem", "w_mem_kv") + W_IN_GATES
REDUCE_IN = ("w_in_a", "w_in_f")


def _local_step(x, mem, target, w, shard, core, chip_core):
    d = x.shape[1]
    nf = shard["w_br_fox"].shape[0] // HEAD_DIM
    nsb = shard["w_br_sb"].shape[0] // HEAD_DIM
    nm = shard["w_br_mem"].shape[0] // HEAD_DIM
    w = dict(w)

    def take(names, gathered):
        for name, a in zip(names, gathered):
            w[name] = _whole(name, a)

    fq, fk, fv = 0, nf, 2 * nf
    sq, sk, sv = 3 * nf, 3 * nf + nsb, 3 * nf + 2 * nsb
    mq = 3 * nf + 3 * nsb

    h, rstd1, moved = _rms_fwd(x, w["g_mix"], "rms_mix_fwd", carry=_Gather([shard[name] for name in GATHER_FIRST]))
    take(GATHER_FIRST, moved)
    proj, moved = _mm(h, w["w_in_a"], "nn", BF16, "proj_att", carry=_Gather([shard[name] for name in GATHER_EARLY]))
    take(GATHER_EARLY, moved)
    gate0, moved = _mm(h, w["w_in_g0"], "nn", BF16, "proj_gate0", carry=_Gather([shard["w_mem_kv"]]))
    take(("w_mem_kv",), moved)
    gate1 = _mm(h, w["w_in_g1"], "nn", BF16, "proj_gate1")
    f_logit, c_sum = _forget_fwd(h, w["w_in_f"], w["b_forget"])
    c_t = c_sum[:, :nf].T
    c_col, c_row = c_t[:, :, None], c_t[:, None, :]
    qn = _headnorm_fwd(proj, fq, nf, w["g_q_fox"], "fox_qnorm_fwd")
    kn = _headnorm_fwd(proj, fk, nf, w["g_k_fox"], "fox_knorm_fwd")
    (o_fox, o_fox32, lse), moved = _fox_fwd(qn, kn, proj, fv, c_col, c_row, nf,
                                            carry=_Gather([shard[name] for name in GATHER_MIX]))
    take(GATHER_MIX, moved)
    gates = (gate0, gate1, _mm(h, w["w_in_g2"], "nn", BF16, "proj_gate2"))
    (o_sb,), moved = _sb_fwd(proj, sq, sk, sv, nsb, carry=_Gather([shard["w_up"]]))
    take(("w_up",), moved)
    memn, rstd_m = _rms_fwd(mem, w["g_mem"], "rms_mem_fwd")
    mkv = _mm(memn, w["w_mem_kv"], "nn", BF16, "mem_kv")
    kmn = _headnorm_fwd(mkv, 0, nm, w["g_k_mem"], "mem_knorm_fwd")
    qmn = _headnorm_fwd(proj, mq, nm, w["g_q_mem"], "mem_qnorm_fwd")
    o_mem = _mem_fwd(qmn, kmn, mkv, nm)
    p0 = _mm(o_fox, w["w_br_fox"], "nn", F32, "branch_fox")
    p1 = _mm(o_sb, w["w_br_sb"], "nn", F32, "branch_sb")
    p2 = _mm(o_mem, w["w_br_mem"], "nn", F32, "branch_mem")
    merged, sig = _merge_fwd(p0, p1, p2, gates, w["b_gate"])
    x1 = _mm(merged, w["w_out"], "nn", F32, "out_proj", residual=x)
    h2, rstd2 = _rms_fwd(x1, w["g_ffn"], "rms_ffn_fwd")
    up, moved = _mm(h2, w["w_up"], "nn", BF16, "ffn_up", carry=_Gather([shard["w_down"]]))
    take(("w_down",), moved)
    act, conv_g, conv_v = _conv_act_fwd(up, w["conv_w"], w["conv_b"])
    dy, dyb, lparts = _mm(act, w["w_down"], "nn", F32, "ffn_down_loss", residual=x1, loss_target=target)
    loss = (0.5 / d) * jnp.sum(lparts[::8, ::LANES])

    g = {}
    dact = _mm(dyb, w["w_down"], "nt", BF16, "ffn_down_dx")
    g["w_down"] = _mm(act, dyb, "tn", BF16, "ffn_down_dw")
    dug, duv, dwg, dwv, dbg, dbv = _conv_act_bwd(up, w["conv_w"], conv_g, conv_v, dact)
    dup = jnp.concatenate([dug, duv], axis=1)
    g["conv_w"] = jnp.concatenate([dwg, dwv], axis=1)
    g["conv_b"] = jnp.concatenate([dbg, dbv], axis=1)
    split_down = [_by_shard("w_down", g["w_down"])]
    dh2, theirs_down = _mm(dup, w["w_up"], "nt", BF16, "ffn_up_dx", carry=_Swap(split_down))
    g["w_up"] = _mm(h2, dup, "tn", BF16, "ffn_up_dw")
    split_up = [_by_shard("w_up", g["w_up"])]
    dx1, dx1b, g["g_ffn"] = _rms_bwd(dh2, x1, rstd2, w["g_ffn"], dy, "rms_ffn_bwd")
    dmerged, theirs_up = _mm(dx1b, w["w_out"], "nt", BF16, "out_proj_dx", carry=_Swap(split_up))
    sums_ffn = _sibling_sums(REDUCE_FFN, split_down + split_up, theirs_down + theirs_up, core)
    g["w_out"] = _mm(merged, dx1b, "tn", BF16, "out_proj_dw")
    dp0, dp1, dp2, dga, dgb, dgc, g["b_gate"] = _merge_bwd(dmerged, p0, p1, p2, sig)
    dgates = (dga, dgb, dgc)
    for name, dgate in zip(W_IN_GATES, dgates):
        g[name] = _mm(h, dgate, "tn", BF16, name + "_dw")
    do_fox = _mm(dp0, w["w_br_fox"], "nt", BF16, "branch_fox_dx")
    do_sb = _mm(dp1, w["w_br_sb"], "nt", BF16, "branch_sb_dx")
    do_mem = _mm(dp2, w["w_br_mem"], "nt", BF16, "branch_mem_dx")
    g["w_br_fox"] = _mm(o_fox, dp0, "tn", BF16, "branch_fox_dw")
    g["w_br_sb"] = _mm(o_sb, dp1, "tn", BF16, "branch_sb_dw")
    g["w_br_mem"] = _mm(o_mem, dp2, "tn", BF16, "branch_mem_dw")
    dqmn, dkmn, dvm = _mem_bwd(qmn, kmn, mkv, do_mem, nm)
    dmq, g["g_q_mem"] = _headnorm_bwd(dqmn, proj, mq, nm, w["g_q_mem"], "mem_qnorm_bwd")
    dkm, g["g_k_mem"] = _headnorm_bwd(dkmn, mkv, 0, nm, w["g_k_mem"], "mem_knorm_bwd")
    dmkv = jnp.concatenate([dkm, dvm.astype(BF16)], axis=1)
    g["w_mem_kv"] = _mm(memn, dmkv, "tn", BF16, "mem_kv_dw")
    dmemn = _mm(dmkv, w["w_mem_kv"], "nt", BF16, "mem_kv_dx")
    _, _, g["g_mem"] = _rms_bwd(dmemn, mem, rstd_m, w["g_mem"], None, "rms_mem_bwd")
    split_mix =[_by_shard(name, g[name]) for name in REDUCE_MIX]

    (dqn, dkn, dfv, drs, dcs), moved = _fox_bwd(qn, kn, proj, fv, c_col, c_row, o_fox32, do_fox, lse, nf,
                                                carry=_Both(_Scatter(sums_ffn[1:]), _Swap(split_mix)))
    others_up, theirs_mix = moved[:1], moved[1:]
    sums_mix = _sibling_sums(REDUCE_MIX, split_mix, theirs_mix, core)
    dfq, g["g_q_fox"] = _headnorm_bwd(dqn, proj, fq, nf, w["g_q_fox"], "fox_qnorm_bwd")
    dfk, g["g_k_fox"] = _headnorm_bwd(dkn, proj, fk, nf, w["g_k_fox"], "fox_knorm_bwd")
    dc = jnp.pad((drs[:, :, 0] - dcs[:, 0, :]).T, ((0, 0), (0, LANES - nf)))
    df, g["b_forget"] = _forget_bwd(dc, f_logit, w["b_forget"])
    (dsq, dsk, dsv), others_mix = _sb_bwd(proj, sq, sk, sv, do_sb, nsb, carry=_Scatter(sums_ffn[:1] + sums_mix))
    others_ffn = others_mix[:1] + others_up
    others_mix = others_mix[1:]

    dproj = jnp.concatenate([dfq, dfk, dfv.astype(BF16), dsq.astype(BF16), dsk.astype(BF16), dsv.astype(BF16), dmq],
                            axis=1)
    dfb = df.astype(BF16)
    g["w_in_a"] = _mm(h, dproj, "tn", BF16, "proj_att_dw")
    g["w_in_f"] = _mm(h, dfb, "tn", BF16, "proj_forget_dw")
    split_in = [_by_shard(name, g[name]) for name in REDUCE_IN]
    dh, theirs_in = _mm(dgates[0], w[W_IN_GATES[0]], "nt", F32, W_IN_GATES[0] + "_dx", carry=_Swap(split_in))
    sums_in = _sibling_sums(REDUCE_IN, split_in, theirs_in, core)
    for name, dgate in zip(W_IN_GATES[1:], dgates[1:]):
        dh = _mm(dgate, w[name], "nt", F32, name + "_dx", residual=dh)
    dh, others_in = _mm(dproj, w["w_in_a"], "nt", F32, "proj_att_dx", residual=dh, carry=_Scatter(sums_in))
    grad_x, _, g["g_mix"] = _rms_bwd(dh, x, rstd1, w["g_mix"], dx1, "rms_mix_bwd", more=(dfb, w["w_in_f"]))

    names = REDUCE_FFN + REDUCE_MIX + REDUCE_IN
    finals = [_add_chips(own, theirs, chip_core, "add_chips_" + name)
              for name, own, theirs in zip(names, sums_ffn + sums_mix + sums_in, others_ffn + others_mix + others_in)]
    summed = {name: a.reshape(2 * a.shape[1], a.shape[2]) for name, a in zip(names, _join_halves(finals))}
    return loss, grad_x, g, summed


SMALL = ("g_mix", "b_forget", "g_q_fox", "g_k_fox", "g_mem", "g_q_mem", "g_k_mem", "b_gate", "g_ffn", "conv_w",
         "conv_b")
SMALL_SHARDED = ("b_gate", "conv_w")
PACK_ROWS = 8


def _pack(arrs):
    flat = jnp.concatenate([a.reshape(-1) for a in arrs])
    unit = PACK_ROWS * LANES
    flat = jnp.pad(flat, (0, -flat.shape[0] % unit))
    return flat.reshape(-1, LANES)


def _unpack(packed, shapes):
    flat = packed.reshape(-1)
    out, at = [], 0
    for s in shapes:
        n = 1
        for dim in s:
            n *= dim
        out.append(flat[at:at + n].reshape(s))
        at += n
    return out


def kernel(x, mem, g_mix, w_in, b_forget, g_q_fox, g_k_fox, g_mem, w_mem_kv, g_q_mem, g_k_mem, w_br_fox, w_br_sb, w_br_mem, b_gate, w_out, g_ffn, w_up, conv_w, conv_b, w_down, loss_target, m_g_mix, m_w_in, m_b_forget, m_g_q_fox, m_g_k_fox, m_g_mem, m_w_mem_kv, m_g_q_mem, m_g_k_mem, m_w_br_fox, m_w_br_sb, m_w_br_mem, m_b_gate, m_w_out, m_g_ffn, m_w_up, m_conv_w, m_conv_b, m_w_down, v_g_mix, v_w_in, v_b_forget, v_g_q_fox, v_g_k_fox, v_g_mem, v_w_mem_kv, v_g_q_mem, v_g_k_mem, v_w_br_fox, v_w_br_sb, v_w_br_mem, v_b_gate, v_w_out, v_g_ffn, v_w_up, v_conv_w, v_conv_b, v_w_down):
    given = dict(g_mix=g_mix, w_in=w_in, b_forget=b_forget, g_q_fox=g_q_fox, g_k_fox=g_k_fox, g_mem=g_mem,
                 w_mem_kv=w_mem_kv, g_q_mem=g_q_mem, g_k_mem=g_k_mem, w_br_fox=w_br_fox, w_br_sb=w_br_sb,
                 w_br_mem=w_br_mem, b_gate=b_gate, w_out=w_out, g_ffn=g_ffn, w_up=w_up, conv_w=conv_w, conv_b=conv_b,
                 w_down=w_down)
    m_in = dict(g_mix=m_g_mix, w_in=m_w_in, b_forget=m_b_forget, g_q_fox=m_g_q_fox, g_k_fox=m_g_k_fox, g_mem=m_g_mem,
                w_mem_kv=m_w_mem_kv, g_q_mem=m_g_q_mem, g_k_mem=m_g_k_mem, w_br_fox=m_w_br_fox, w_br_sb=m_w_br_sb,
                w_br_mem=m_w_br_mem, b_gate=m_b_gate, w_out=m_w_out, g_ffn=m_g_ffn, w_up=m_w_up, conv_w=m_conv_w,
                conv_b=m_conv_b, w_down=m_w_down)
    v_in = dict(g_mix=v_g_mix, w_in=v_w_in, b_forget=v_b_forget, g_q_fox=v_g_q_fox, g_k_fox=v_g_k_fox, g_mem=v_g_mem,
                w_mem_kv=v_w_mem_kv, g_q_mem=v_g_q_mem, g_k_mem=v_g_k_mem, w_br_fox=v_w_br_fox, w_br_sb=v_w_br_sb,
                w_br_mem=v_w_br_mem, b_gate=v_b_gate, w_out=v_w_out, g_ffn=v_g_ffn, w_up=v_w_up, conv_w=v_conv_w,
                conv_b=v_conv_b, w_down=v_w_down)
    layered = {k: a.ndim == 3 for k, a in given.items()}
    drop = lambda a: a[0] if a.ndim == 3 else a
    given = {k: drop(a) for k, a in given.items()}
    m_in = {k: drop(a) for k, a in m_in.items()}
    v_in = {k: drop(a) for k, a in v_in.items()}

    xi, yi, ci = lax.axis_index("x"), lax.axis_index("y"), lax.axis_index("c")
    chip = (2 * xi + yi).astype(jnp.int32)
    core_arr = ci.astype(jnp.int32).reshape(1)
    chip_core = jnp.stack([chip, ci.astype(jnp.int32)])

    nf = given["b_forget"].shape[1]
    cut = 3 * given["w_br_fox"].shape[0]

    d_model = given["w_out"].shape[1]
    gate0 = given["w_in"].shape[1] - len(W_IN_GATES) * d_model
    shard = {
        "w_in_a": jnp.concatenate([given["w_in"][:, :cut], given["w_in"][:, cut + nf:gate0]], axis=1).astype(BF16),
        "w_in_f": jnp.pad(given["w_in"][:, cut:cut + nf], ((0, 0), (0, LANES - nf))).astype(BF16),
    }
    for b, name in enumerate(W_IN_GATES):
        shard[name] = given["w_in"][:, gate0 + b * d_model:gate0 + (b + 1) * d_model].astype(BF16)
    for name in BIG:
        if name not in shard:
            shard[name] = given[name].astype(BF16)
    w = {}
    small_shapes = [given[name].shape for name in SMALL_SHARDED]
    small_parts = _gather_small(_pack([given[name] for name in SMALL_SHARDED]))[0::2]
    per_chip = [_unpack(small_parts[j], small_shapes) for j in range(N_CHIPS)]
    for k, name in enumerate(SMALL_SHARDED):
        w[name] = jnp.concatenate([per_chip[j][k] for j in range(N_CHIPS)], axis=1)
    for name in SMALL:
        if name not in SMALL_SHARDED:
            w[name] = given[name]
    w["b_forget"] = jnp.pad(given["b_forget"], ((0, 0), (0, LANES - nf)))

    loss, grad_x, g, summed = _local_step(x[0], mem[0], loss_target[0], w, shard, core_arr, chip_core)
    loss = lax.psum(loss, ("x", "y", "c"))
    grads = {name: summed[name] for name in BIG if name in given}
    grads["w_in"] = jnp.concatenate([summed["w_in_a"][:, :cut], summed["w_in_f"][:, :nf], summed["w_in_a"][:, cut:]]
                                    + [summed[name] for name in W_IN_GATES], axis=1)

    g["b_forget"] = g["b_forget"][:, :nf]
    small_full_shapes = [g[name].shape for name in SMALL]
    small_sum = _unpack(_sum_devices(_gather_small(_pack([g[name] for name in SMALL]))), small_full_shapes)
    for name, a in zip(SMALL, small_sum):
        if name in SMALL_SHARDED:
            width = given[name].shape[1]
            a = lax.dynamic_slice_in_dim(a, chip * width, width, axis=1)
        grads[name] = a

    delta, new_m, new_v = {}, {}, {}
    for name in WEIGHTS:
        if name not in SMALL:
            delta[name], new_m[name], new_v[name] = _adamw(given[name], grads[name], m_in[name], v_in[name],
                                                           "adamw_" + name)
    shapes = [given[name].shape for name in SMALL]
    packed = [_pack([src[name] for name in SMALL]) for src in (given, grads, m_in, v_in)]
    for dst, res in zip((delta, new_m, new_v), _adamw(*packed, "adamw_small")):
        for name, a in zip(SMALL, _unpack(res, shapes)):
            dst[name] = a

    out = [loss, grad_x[None]]
    for src in (grads, delta, new_m, new_v):
        out.extend(src[name][None] if layered[name] else src[name] for name in WEIGHTS)
    return tuple(out)
```

```python
import functools

import jax
import jax.numpy as jnp
from jax import lax
from jax.experimental import pallas as pl
from jax.experimental.pallas import tpu as pltpu

F32 = jnp.float32
BF16 = jnp.bfloat16

HEAD_DIM = 128
EPS = 1e-6
NEG_BIG = -1e30

ADAM_LR = 0.001
ADAM_B1 = 0.9
ADAM_B2 = 0.999
ADAM_EPS = 1e-08
ADAM_WD = 0.01
ADAM_STEP = 10

LANES = 128
BF16_SUBLANES = 16
VMEM_LIMIT_BYTES = 56 * 1024 * 1024
MM_TILE = 1024
MM_TILE_K = {"nn": 2048, "nt": 2816, "tn": 4096}
ATT_TILE = 256
ROW_TILE = 256
HEADNORM_ROWS = 512
MEM_ROWS = 512
COL_TILE = 512
ADAM_BLOCK_BYTES = 2 << 20

N_CHIPS = 4
N_DEV = 8
MESH = pl.DeviceIdType.MESH

IN_NAMES = ['x', 'mem', 'g_mix', 'w_in', 'b_forget', 'g_q_fox', 'g_k_fox', 'g_mem', 'w_mem_kv', 'g_q_mem', 'g_k_mem',
            'w_br_fox', 'w_br_sb', 'w_br_mem', 'b_gate', 'w_out', 'g_ffn', 'w_up', 'conv_w', 'conv_b', 'w_down']
WEIGHTS = IN_NAMES[2:]


def _tile(n, target):
    if n <= target:
        return n
    for t in range(target - target % LANES, LANES - 1, -LANES):
        if n % t == 0:
            return t
    return n


def _params(*sem):
    return pltpu.CompilerParams(dimension_semantics=sem, vmem_limit_bytes=VMEM_LIMIT_BYTES)


def _log_sigmoid(z):
    return jnp.minimum(z, 0.0) - jnp.log(1.0 + jnp.exp(-jnp.abs(z)))


def _split2(v):
    hi = v.astype(BF16)
    lo = (v - hi.astype(F32)).astype(BF16)
    return hi, lo


def _split3(v):
    hi = v.astype(BF16)
    r = v - hi.astype(F32)
    mid = r.astype(BF16)
    lo = (r - mid.astype(F32)).astype(BF16)
    return hi, mid, lo


def _dot(a, b):
    return lax.dot_general(a, b, (((1,), (0,)), ((), ())), preferred_element_type=F32)


def _dot_nt(a, b):
    return lax.dot_general(a, b, (((1,), (1,)), ((), ())), preferred_element_type=F32)


def _dot_tn(a, b):
    return lax.dot_general(a, b, (((0,), (0,)), ((), ())), preferred_element_type=F32)


ANY = pl.BlockSpec(memory_space=pl.ANY)


def _place():
    x, y, c = lax.axis_index("x"), lax.axis_index("y"), lax.axis_index("c")
    others = [(1 - x, y), (x, 1 - y), (1 - x, 1 - y)]
    return x, y, c, others


def _remote(src, dst, send_sem, recv_sem, to):
    return pltpu.make_async_remote_copy(src_ref=src, dst_ref=dst, send_sem=send_sem, recv_sem=recv_sem,
                                        device_id=to, device_id_type=MESH)


class _Gather:
    PER_SHARD = 7

    def __init__(self, shards):
        self.inputs = list(shards)
        n = len(shards) * self.PER_SHARD
        self.out_shapes = [jax.ShapeDtypeStruct((N_CHIPS,) + s.shape, s.dtype) for s in shards]
        self.scratch = [pltpu.SemaphoreType.DMA((n,)), pltpu.SemaphoreType.DMA((n,))]

    def _first(self, ins, outs, sems):
        send_sems, recv_sems = sems
        x, y, c, others = _place()
        me = 2 * x + y
        k = self.PER_SHARD
        copies = []
        for i in range(len(ins)):
            h = ins[i].shape[0] // 2
            mine = pl.ds(pl.multiple_of(c * h, BF16_SUBLANES), h)
            for j, (ox, oy) in enumerate(others):
                copies.append(_remote(ins[i].at[mine], outs[i].at[me, mine], send_sems.at[k * i + j],
                                      recv_sems.at[k * i + j], (ox, oy, c)))
            copies.append(_remote(ins[i], outs[i].at[me], send_sems.at[k * i + 6], recv_sems.at[k * i + 6],
                                  (x, y, 1 - c)))
        return copies

    def start(self, ins, outs, sems):
        for cp in self._first(ins, outs, sems):
            cp.start()

    def finish(self, ins, outs, sems):
        send_sems, recv_sems = sems
        x, y, c, others = _place()
        me = 2 * x + y
        sibling = (x, y, 1 - c)
        k = self.PER_SHARD
        passed = []
        for i in range(len(ins)):
            h = ins[i].shape[0] // 2
            mine = pl.ds(pl.multiple_of(c * h, BF16_SUBLANES), h)
            for j, (ox, oy) in enumerate(others):
                blk = outs[i].at[2 * ox + oy, mine]
                _remote(blk, blk, send_sems.at[k * i + j], recv_sems.at[k * i + j], (ox, oy, c)).wait_recv()
                cp = _remote(blk, blk, send_sems.at[k * i + 3 + j], recv_sems.at[k * i + 3 + j], sibling)
                cp.start()
                passed.append(cp)
        for i in range(len(ins)):
            h = ins[i].shape[0] // 2
            theirs = pl.ds(pl.multiple_of((1 - c) * h, BF16_SUBLANES), h)
            for j, (ox, oy) in enumerate(others):
                blk = outs[i].at[2 * ox + oy, theirs]
                _remote(blk, blk, send_sems.at[k * i + 3 + j], recv_sems.at[k * i + 3 + j], sibling).wait_recv()
            own = outs[i].at[me]
            _remote(own, own, send_sems.at[k * i + 6], recv_sems.at[k * i + 6], sibling).wait_recv()
        for cp in self._first(ins, outs, sems) + passed:
            cp.wait_send()


class _Scatter:
    def __init__(self, sums):
        self.inputs = list(sums)
        k = N_CHIPS - 1
        self.out_shapes = [jax.ShapeDtypeStruct((k,) + g.shape[1:], g.dtype) for g in sums]
        self.scratch = [pltpu.SemaphoreType.DMA((k * len(sums),)), pltpu.SemaphoreType.DMA((k * len(sums),))]

    def _copies(self, ins, outs, sems):
        send_sems, recv_sems = sems
        _, _, c, others = _place()
        k = N_CHIPS - 1
        return [_remote(ins[i].at[2 * ox + oy], outs[i].at[j], send_sems.at[k * i + j], recv_sems.at[k * i + j],
                        (ox, oy, c))
                for i in range(len(ins)) for j, (ox, oy) in enumerate(others)]

    def start(self, ins, outs, sems):
        for cp in self._copies(ins, outs, sems):
            cp.start()

    def finish(self, ins, outs, sems):
        for cp in self._copies(ins, outs, sems):
            cp.wait()


class _Swap:
    def __init__(self, grads):
        self.inputs = list(grads)
        n = len(grads)
        self.out_shapes = [jax.ShapeDtypeStruct((g.shape[0],) + g.shape[2:], g.dtype) for g in grads]
        self.scratch = [pltpu.SemaphoreType.DMA((n,)), pltpu.SemaphoreType.DMA((n,))]

    def _copies(self, ins, outs, sems):
        send_sems, recv_sems = sems
        x, y, c, _ = _place()
        return [_remote(ins[i].at[:, 1 - c], outs[i], send_sems.at[i], recv_sems.at[i], (x, y, 1 - c))
                for i in range(len(ins))]

    def start(self, ins, outs, sems):
        for cp in self._copies(ins, outs, sems):
            cp.start()

    def finish(self, ins, outs, sems):
        for cp in self._copies(ins, outs, sems):
            cp.wait()


class _Join:
    def __init__(self, finals):
        self.inputs = list(finals)
        n = len(finals)
        self.out_shapes = [jax.ShapeDtypeStruct(f.shape, f.dtype) for f in finals]
        self.scratch = [pltpu.SemaphoreType.DMA((n,)), pltpu.SemaphoreType.DMA((n,))]
        self.aliases = {i: i for i in range(n)}

    def _sends(self, outs, sems):
        send_sems, recv_sems = sems
        x, y, c, _ = _place()
        return [_remote(outs[i].at[c], outs[i].at[c], send_sems.at[i], recv_sems.at[i], (x, y, 1 - c))
                for i in range(len(outs))]

    def start(self, ins, outs, sems):
        for cp in self._sends(outs, sems):
            cp.start()

    def finish(self, ins, outs, sems):
        send_sems, recv_sems = sems
        x, y, c, _ = _place()
        for i, cp in enumerate(self._sends(outs, sems)):
            cp.wait_send()
            other = outs[i].at[1 - c]
            _remote(other, other, send_sems.at[i], recv_sems.at[i], (x, y, 1 - c)).wait_recv()


class _Both:
    def __init__(self, first, second):
        self.parts = (first, second)
        self.inputs = first.inputs + second.inputs
        self.out_shapes = first.out_shapes + second.out_shapes
        self.scratch = first.scratch + second.scratch

    def _each(self, ins, outs, sems):
        first = self.parts[0]
        a, b, c = len(first.inputs), len(first.out_shapes), len(first.scratch)
        return ((first, ins[:a], outs[:b], sems[:c]), (self.parts[1], ins[a:], outs[b:], sems[c:]))

    def start(self, ins, outs, sems):
        for part, i, o, s in self._each(ins, outs, sems):
            part.start(i, o, s)

    def finish(self, ins, outs, sems):
        for part, i, o, s in self._each(ins, outs, sems):
            part.finish(i, o, s)


def _call(body, *, grid, in_specs, out_specs, out_shape, scratch_shapes, semantics, name, args, carry=None):
    n_in, n_out, n_scr = len(in_specs), len(out_specs), len(scratch_shapes)
    if carry is None:
        res = pl.pallas_call(body, grid=grid, in_specs=in_specs, out_specs=out_specs, out_shape=out_shape,
                             scratch_shapes=scratch_shapes, compiler_params=_params(*semantics), name=name)(*args)
        return list(res), []
    nci, nco = len(carry.inputs), len(carry.out_shapes)
    a, b = n_in, n_in + nci
    c, d = b + n_out, b + n_out + nco
    e = d + n_scr

    def carried(*refs):
        ids = [pl.program_id(k) for k in range(len(grid))]
        first = functools.reduce(jnp.logical_and, [i == 0 for i in ids])
        last = functools.reduce(jnp.logical_and, [i == n - 1 for i, n in zip(ids, grid)])

        @pl.when(first)
        def _():
            carry.start(refs[a:b], refs[c:d], refs[e:])

        body(*refs[:a], *refs[b:c], *refs[d:e])

        @pl.when(last)
        def _():
            carry.finish(refs[a:b], refs[c:d], refs[e:])

    res = pl.pallas_call(
        carried,
        grid=grid,
        in_specs=list(in_specs) + [ANY] * nci,
        out_specs=list(out_specs) + [ANY] * nco,
        out_shape=list(out_shape) + carry.out_shapes,
        scratch_shapes=list(scratch_shapes) + carry.scratch,
        input_output_aliases={n_in + i: n_out + o for i, o in getattr(carry, "aliases", {}).items()},
        compiler_params=_params(*(["arbitrary"] * len(grid))),
        name=name,
    )(*args, *carry.inputs)
    return list(res[:n_out]), list(res[n_out:])


def _mm(a, b, mode, out_dtype, name, residual=None, carry=None, loss_target=None):
    if mode == "nn":
        (m, k), (k2, n) = a.shape, b.shape
    elif mode == "nt":
        (m, k), (n, k2) = a.shape, b.shape
    else:
        (k, m), (k2, n) = a.shape, b.shape
    assert k == k2, (a.shape, b.shape, mode)
    has_res = residual is not None
    has_loss = loss_target is not None
    n_in = 2 + has_res + has_loss
    tm, tn, tk = _tile(m, MM_TILE), _tile(n, MM_TILE), _tile(k, MM_TILE_K[mode])
    nk = k // tk
    dot = {"nn": _dot, "nt": _dot_nt, "tn": _dot_tn}[mode]

    def body(*refs):
        a_ref, b_ref = refs[:2]
        r_ref = refs[2] if has_res else None
        t_ref = refs[n_in - 1] if has_loss else None
        o_ref = refs[n_in]

        def finish(acc):
            if has_res:
                acc = acc + r_ref[...]
            if has_loss:
                err = acc - t_ref[...]
                dy = err * (1.0 / n)
                o_ref[...] = dy
                refs[n_in + 1][...] = dy.astype(BF16)
                tot = jnp.sum(jnp.sum(err * err, axis=-1, keepdims=True), axis=0, keepdims=True)
                refs[n_in + 2][...] = jnp.broadcast_to(tot, (8, LANES))
            else:
                o_ref[...] = acc.astype(o_ref.dtype)

        part = dot(a_ref[...], b_ref[...])
        if nk == 1:
            finish(part)
        else:
            acc_ref = refs[-1]
            kk = pl.program_id(2)

            @pl.when(kk == 0)
            def _():
                acc_ref[...] = part

            @pl.when(kk > 0)
            def _():
                acc_ref[...] += part

            @pl.when(kk == nk - 1)
            def _():
                finish(acc_ref[...])

    if mode == "tn":
        a_spec = pl.BlockSpec((tk, tm), lambda j, i, kk: (kk, i))
    else:
        a_spec = pl.BlockSpec((tm, tk), lambda j, i, kk: (i, kk))
    if mode == "nt":
        b_spec = pl.BlockSpec((tn, tk), lambda j, i, kk: (j, kk))
    else:
        b_spec = pl.BlockSpec((tk, tn), lambda j, i, kk: (kk, j))
    o_spec = pl.BlockSpec((tm, tn), lambda j, i, kk: (i, j))
    in_specs = [a_spec, b_spec] + [o_spec] * (has_res + has_loss)
    args = (a, b) + ((residual,) if has_res else ()) + ((loss_target,) if has_loss else ())
    out_specs, out_shape = [o_spec], [jax.ShapeDtypeStruct((m, n), out_dtype)]
    if has_loss:
        out_specs += [o_spec, pl.BlockSpec((8, LANES), lambda j, i, kk: (i, j))]
        out_shape += [jax.ShapeDtypeStruct((m, n), BF16), jax.ShapeDtypeStruct((m // tm * 8, n // tn * LANES), F32)]
    outs, moved = _call(
        body,
        grid=(n // tn, m // tm, nk),
        in_specs=in_specs,
        out_specs=out_specs,
        out_shape=out_shape,
        scratch_shapes=[pltpu.VMEM((tm, tn), F32)] if nk > 1 else [],
        semantics=("parallel", "parallel", "arbitrary"),
        name=name,
        args=args,
        carry=carry,
    )
    out = outs if has_loss else outs[0]
    return out if carry is None else (out, moved)


def _rms_fwd(x, g, name, carry=None):
    s, d = x.shape
    tm = _tile(s, ROW_TILE)

    def body(x_ref, g_ref, h_ref, r_ref):
        xf = x_ref[...]
        r = lax.rsqrt(jnp.mean(xf * xf, axis=-1, keepdims=True) + EPS)
        h_ref[...] = ((xf * r) * g_ref[...]).astype(BF16)
        r_ref[...] = r

    (h, rstd), moved = _call(
        body,
        grid=(s // tm,),
        in_specs=[pl.BlockSpec((tm, d), lambda i: (i, 0)), pl.BlockSpec((1, d), lambda i: (0, 0))],
        out_specs=[pl.BlockSpec((tm, d), lambda i: (i, 0)), pl.BlockSpec((tm, 1), lambda i: (i, 0))],
        out_shape=[jax.ShapeDtypeStruct((s, d), BF16), jax.ShapeDtypeStruct((s, 1), F32)],
        scratch_shapes=[],
        semantics=("parallel",),
        name=name,
        args=(x, g),
        carry=carry,
    )
    return (h, rstd) if carry is None else (h, rstd, moved)


def _rms_bwd(dh, x, rstd, g, res, name, more=None, carry=None):
    s, d = x.shape
    tm = _tile(s, ROW_TILE)
    has_res = res is not None
    has_more = more is not None
    n_in = 4 + has_res + 2 * has_more

    def body(*refs):
        dh_ref, x_ref, r_ref, g_ref = refs[:4]
        res_ref = refs[4] if has_res else None
        dx_ref, dxb_ref, dg_ref = refs[n_in:]
        dhf = dh_ref[...].astype(F32)
        if has_more:
            dhf = dhf + _dot_nt(refs[n_in - 2][...], refs[n_in - 1][...])
        xhat = x_ref[...] * r_ref[...]
        dy = dhf * g_ref[...]
        dx = r_ref[...] * (dy - xhat * jnp.mean(dy * xhat, axis=-1, keepdims=True))
        if has_res:
            dx = dx + res_ref[...]
        dx_ref[...] = dx
        dxb_ref[...] = dx.astype(BF16)
        part = jnp.sum(dhf * xhat, axis=0, keepdims=True)

        @pl.when(pl.program_id(0) == 0)
        def _():
            dg_ref[...] = part

        @pl.when(pl.program_id(0) > 0)
        def _():
            dg_ref[...] += part

    row = pl.BlockSpec((tm, d), lambda i: (i, 0))
    vec = pl.BlockSpec((1, d), lambda i: (0, 0))
    in_specs = [row, row, pl.BlockSpec((tm, 1), lambda i: (i, 0)), vec] + ([row] if has_res else [])
    args = (dh, x, rstd, g) + ((res,) if has_res else ())
    if has_more:
        k = more[0].shape[1]
        in_specs += [pl.BlockSpec((tm, k), lambda i: (i, 0)), pl.BlockSpec((d, k), lambda i: (0, 0))]
        args += tuple(more)
    (dx, dxb, dg), moved = _call(
        body,
        grid=(s // tm,),
        in_specs=in_specs,
        out_specs=[row, row, vec],
        out_shape=[jax.ShapeDtypeStruct((s, d), F32), jax.ShapeDtypeStruct((s, d), BF16),
                   jax.ShapeDtypeStruct((1, d), F32)],
        scratch_shapes=[],
        semantics=("arbitrary",),
        name=name,
        args=args,
        carry=carry,
    )
    return (dx, dxb, dg) if carry is None else (dx, dxb, dg, moved)


def _headnorm_fwd(src, col0, nheads, g, name):
    s = src.shape[0]
    tm = _tile(s, HEADNORM_ROWS)
    w = nheads * HEAD_DIM
    assert col0 % nheads == 0

    def body(x_ref, g_ref, o_ref):
        for hh in range(nheads):
            xf = _head(x_ref, hh).astype(F32)
            r = lax.rsqrt(jnp.mean(xf * xf, axis=-1, keepdims=True) + EPS)
            o_ref[:, hh * HEAD_DIM:(hh + 1) * HEAD_DIM] = ((xf * r) * g_ref[...]).astype(BF16)

    return pl.pallas_call(
        body,
        grid=(s // tm,),
        in_specs=[pl.BlockSpec((tm, w), lambda i: (i, col0 // nheads)),
                  pl.BlockSpec((1, HEAD_DIM), lambda i: (0, 0))],
        out_specs=pl.BlockSpec((tm, w), lambda i: (i, 0)),
        out_shape=jax.ShapeDtypeStruct((s, w), BF16),
        compiler_params=_params("parallel"),
        name=name,
    )(src, g)


def _headnorm_bwd(dxn, src, col0, nheads, g, name):
    s = src.shape[0]
    tm = _tile(s, HEADNORM_ROWS)
    w = nheads * HEAD_DIM
    assert col0 % nheads == 0

    def body(d_ref, x_ref, g_ref, dx_ref, dg_ref):
        part = jnp.zeros((1, HEAD_DIM), F32)
        for hh in range(nheads):
            xf = _head(x_ref, hh).astype(F32)
            r = lax.rsqrt(jnp.mean(xf * xf, axis=-1, keepdims=True) + EPS)
            xhat = xf * r
            dn = _head(d_ref, hh).astype(F32)
            dy = dn * g_ref[...]
            dx = r * (dy - xhat * jnp.mean(dy * xhat, axis=-1, keepdims=True))
            dx_ref[:, hh * HEAD_DIM:(hh + 1) * HEAD_DIM] = dx.astype(BF16)
            part = part + jnp.sum(dn * xhat, axis=0, keepdims=True)

        @pl.when(pl.program_id(0) == 0)
        def _():
            dg_ref[...] = part

        @pl.when(pl.program_id(0) > 0)
        def _():
            dg_ref[...] += part

    return pl.pallas_call(
        body,
        grid=(s // tm,),
        in_specs=[pl.BlockSpec((tm, w), lambda i: (i, 0)),
                  pl.BlockSpec((tm, w), lambda i: (i, col0 // nheads)),
                  pl.BlockSpec((1, HEAD_DIM), lambda i: (0, 0))],
        out_specs=[pl.BlockSpec((tm, w), lambda i: (i, 0)),
                   pl.BlockSpec((1, HEAD_DIM), lambda i: (0, 0))],
        out_shape=[jax.ShapeDtypeStruct((s, w), BF16), jax.ShapeDtypeStruct((1, HEAD_DIM), F32)],
        compiler_params=_params("arbitrary"),
        name=name,
    )(dxn, src, g)


def _tri(t, lower_inclusive):
    r = lax.broadcasted_iota(jnp.int32, (t, t), 0)
    c = lax.broadcasted_iota(jnp.int32, (t, t), 1)
    keep = (c <= r) if lower_inclusive else (c >= r)
    return jnp.where(keep, 1.0, 0.0).astype(BF16)


def _forget_fwd(h, w_f, b_pad):
    s, d = h.shape
    t = _tile(s, ATT_TILE)

    def body(h_ref, w_ref, b_ref, f_ref, c_ref, carry):
        @pl.when(pl.program_id(0) == 0)
        def _():
            carry[...] = jnp.zeros_like(carry)

        f = _dot(h_ref[...], w_ref[...])
        f_ref[...] = f
        lf = _log_sigmoid(f + b_ref[...])
        tri = _tri(t, True)
        acc = carry[...]
        for part in _split3(lf):
            acc = acc + _dot(tri, part)
        c_ref[...] = acc
        carry[...] += jnp.sum(lf, axis=0, keepdims=True)

    blk = pl.BlockSpec((t, LANES), lambda i: (i, 0))
    return pl.pallas_call(
        body,
        grid=(s // t,),
        in_specs=[pl.BlockSpec((t, d), lambda i: (i, 0)), pl.BlockSpec((d, LANES), lambda i: (0, 0)),
                  pl.BlockSpec((1, LANES), lambda i: (0, 0))],
        out_specs=[blk, blk],
        out_shape=[jax.ShapeDtypeStruct((s, LANES), F32)] * 2,
        scratch_shapes=[pltpu.VMEM((1, LANES), F32)],
        compiler_params=_params("arbitrary"),
        name="forget_fwd",
    )(h, w_f, b_pad)


def _forget_bwd(dc, f_logit, b_pad):
    s = f_logit.shape[0]
    t = _tile(s, ATT_TILE)
    nb = s // t

    def body(dc_ref, f_ref, b_ref, df_ref, db_ref, carry):
        @pl.when(pl.program_id(0) == 0)
        def _():
            carry[...] = jnp.zeros_like(carry)
            db_ref[...] = jnp.zeros_like(db_ref)

        d = dc_ref[...]
        tri = _tri(t, False)
        acc = carry[...]
        for part in _split3(d):
            acc = acc + _dot(tri, part)
        z = f_ref[...] + b_ref[...]
        df = acc * jnp.exp(_log_sigmoid(-z))
        df_ref[...] = df
        db_ref[...] += jnp.sum(df, axis=0, keepdims=True)
        carry[...] += jnp.sum(d, axis=0, keepdims=True)

    rev = pl.BlockSpec((t, LANES), lambda i: (nb - 1 - i, 0))
    vec = pl.BlockSpec((1, LANES), lambda i: (0, 0))
    return pl.pallas_call(
        body,
        grid=(nb,),
        in_specs=[rev, rev, vec],
        out_specs=[rev, vec],
        out_shape=[jax.ShapeDtypeStruct((s, LANES), F32), jax.ShapeDtypeStruct((1, LANES), F32)],
        scratch_shapes=[pltpu.VMEM((1, LANES), F32)],
        compiler_params=_params("arbitrary"),
        name="forget_bwd",
    )(dc, f_logit, b_pad)


SB_FWD_GROUP = 6
FOX_GROUP = 6
SB_BWD_GROUP = 3


def _head(ref, hh, rows=slice(None)):
    return ref[rows, hh * HEAD_DIM:(hh + 1) * HEAD_DIM]


def _tri_mask(t, strict):
    r = lax.broadcasted_iota(jnp.int32, (t, t), 0)
    c = lax.broadcasted_iota(jnp.int32, (t, t), 1)
    return (c < r) if strict else (c <= r)


def _fox_fwd(qn, kn, proj, colv, c_col, c_row, nheads, carry=None):
    s = qn.shape[0]
    t = _tile(s, ATT_TILE)
    scale = HEAD_DIM ** -0.5
    hg = FOX_GROUP
    gw = hg * HEAD_DIM
    assert nheads % hg == 0 and colv % hg == 0

    def body(q_ref, k_ref, v_ref, cc_ref, cr_ref, o_ref, of_ref, lse_ref):
        qi = pl.program_id(1)
        causal = _tri_mask(t, False)

        def tile(kj, carry, diagonal):
            off = pl.multiple_of(kj * t, t)
            heads = range(hg)
            rows = pl.ds(off, t)
            qk = [_dot_nt(_head(q_ref, hh), _head(k_ref, hh, rows)) for hh in heads]
            sc = [qk[hh] * scale + (cc_ref[hh] - cr_ref[hh, :, rows]) for hh in heads]
            if diagonal:
                sc = [jnp.where(causal, sc[hh], NEG_BIG) for hh in heads]
            m_new = [jnp.maximum(carry[hh][0], jnp.max(sc[hh], axis=-1, keepdims=True)) for hh in heads]
            p = [jnp.exp(sc[hh] - m_new[hh]) for hh in heads]
            pv = [_dot(p[hh].astype(BF16), _head(v_ref, hh, rows)) for hh in heads]
            out = []
            for hh in heads:
                m, l, acc = carry[hh]
                alpha = jnp.exp(m - m_new[hh])
                out.append((m_new[hh], alpha * l + jnp.sum(p[hh], axis=-1, keepdims=True), alpha * acc + pv[hh]))
            return tuple(out)

        init = tuple((jnp.full((t, 1), NEG_BIG, F32), jnp.zeros((t, 1), F32), jnp.zeros((t, HEAD_DIM), F32))
                     for _ in range(hg))
        carry = lax.fori_loop(0, qi, lambda kj, c: tile(kj, c, False), init)
        carry = tile(qi, carry, True)
        for hh in range(hg):
            m, l, acc = carry[hh]
            o = acc / l
            of_ref[:, hh * HEAD_DIM:(hh + 1) * HEAD_DIM] = o
            o_ref[:, hh * HEAD_DIM:(hh + 1) * HEAD_DIM] = o.astype(BF16)
            lse_ref[hh] = m + jnp.log(l)

    tile_spec = pl.BlockSpec((t, gw), lambda h, i: (i, h))
    w = nheads * HEAD_DIM
    return _call(
        body,
        grid=(nheads // hg, s // t),
        in_specs=[tile_spec,
                  pl.BlockSpec((s, gw), lambda h, i: (0, h), pipeline_mode=pl.Buffered(buffer_count=1)),
                  pl.BlockSpec((s, gw), lambda h, i: (0, colv // hg + h), pipeline_mode=pl.Buffered(buffer_count=1)),
                  pl.BlockSpec((hg, t, 1), lambda h, i: (h, i, 0)),
                  pl.BlockSpec((hg, 1, s), lambda h, i: (h, 0, 0))],
        out_specs=[tile_spec, tile_spec, pl.BlockSpec((hg, t, 1), lambda h, i: (h, i, 0))],
        out_shape=[jax.ShapeDtypeStruct((s, w), BF16), jax.ShapeDtypeStruct((s, w), F32),
                   jax.ShapeDtypeStruct((nheads, s, 1), F32)],
        scratch_shapes=[],
        semantics=("parallel", "parallel"),
        name="fox_fwd",
        args=(qn, kn, proj, c_col, c_row),
        carry=carry,
    )


def _fox_bwd(qn, kn, proj, colv, c_col, c_row, o, do, lse, nheads, carry=None):
    s = qn.shape[0]
    t = _tile(s, ATT_TILE)
    scale = HEAD_DIM ** -0.5
    hg = FOX_GROUP
    gw = hg * HEAD_DIM
    assert nheads % hg == 0 and colv % hg == 0

    def body(q_ref, k_ref, v_ref, cc_ref, cr_ref, o_ref, do_ref, lse_ref,
             dq_ref, dk_ref, dv_ref, drs_ref, dcs_ref):
        qi = pl.program_id(1)

        @pl.when(qi == 0)
        def _():
            dk_ref[...] = jnp.zeros_like(dk_ref)
            dv_ref[...] = jnp.zeros_like(dv_ref)
            dcs_ref[...] = jnp.zeros_like(dcs_ref)

        causal = _tri_mask(t, False)
        delta = [jnp.sum(_head(o_ref, hh) * _head(do_ref, hh).astype(F32), axis=-1, keepdims=True)
                 for hh in range(hg)]

        def tile(kj, carry, diagonal):
            off = pl.multiple_of(kj * t, t)
            heads = range(hg)
            rows = pl.ds(off, t)
            qk = [_dot_nt(_head(q_ref, hh), _head(k_ref, hh, rows)) for hh in heads]
            dp = [_dot_nt(_head(do_ref, hh), _head(v_ref, hh, rows)) for hh in heads]
            p = [jnp.exp(qk[hh] * scale + (cc_ref[hh] - cr_ref[hh, :, rows]) - lse_ref[hh]) for hh in heads]
            if diagonal:
                p = [jnp.where(causal, p[hh], 0.0) for hh in heads]
            ds = [p[hh] * (dp[hh] - delta[hh]) for hh in heads]
            dsb = [ds[hh].astype(BF16) for hh in heads]
            dv = [_dot_tn(p[hh].astype(BF16), _head(do_ref, hh)) for hh in heads]
            dk = [_dot_tn(dsb[hh], _head(q_ref, hh)) * scale for hh in heads]
            dq = [_dot(dsb[hh], _head(k_ref, hh, rows)) * scale for hh in heads]
            for hh in heads:
                cols = slice(hh * HEAD_DIM, (hh + 1) * HEAD_DIM)
                dv_ref[rows, cols] += dv[hh]
                dk_ref[rows, cols] += dk[hh]
                dcs_ref[hh, :, rows] += jnp.sum(ds[hh], axis=0, keepdims=True)
            return tuple((carry[hh][0] + dq[hh], carry[hh][1] + jnp.sum(ds[hh], axis=-1, keepdims=True))
                         for hh in heads)

        init = tuple((jnp.zeros((t, HEAD_DIM), F32), jnp.zeros((t, 1), F32)) for _ in range(hg))
        carry = lax.fori_loop(0, qi, lambda kj, c: tile(kj, c, False), init)
        carry = tile(qi, carry, True)
        for hh in range(hg):
            dq_ref[:, hh * HEAD_DIM:(hh + 1) * HEAD_DIM] = carry[hh][0]
            drs_ref[hh] = carry[hh][1]

    tile_spec = pl.BlockSpec((t, gw), lambda h, i: (i, h))
    once = pl.Buffered(buffer_count=1)
    full = pl.BlockSpec((s, gw), lambda h, i: (0, h), pipeline_mode=once)
    colspec = pl.BlockSpec((hg, t, 1), lambda h, i: (h, i, 0))
    rowspec = pl.BlockSpec((hg, 1, s), lambda h, i: (h, 0, 0))
    w = nheads * HEAD_DIM
    return _call(
        body,
        grid=(nheads // hg, s // t),
        in_specs=[tile_spec, full, pl.BlockSpec((s, gw), lambda h, i: (0, colv // hg + h), pipeline_mode=once),
                  colspec, rowspec,
                  tile_spec, tile_spec, colspec],
        out_specs=[tile_spec, full, full, colspec, rowspec],
        out_shape=[jax.ShapeDtypeStruct((s, w), F32), jax.ShapeDtypeStruct((s, w), F32),
                   jax.ShapeDtypeStruct((s, w), F32), jax.ShapeDtypeStruct((nheads, s, 1), F32),
                   jax.ShapeDtypeStruct((nheads, 1, s), F32)],
        scratch_shapes=[],
        semantics=("arbitrary", "arbitrary"),
        name="fox_bwd",
        args=(qn, kn, proj, c_col, c_row, o, do, lse),
        carry=carry,
    )


def _sb_tile(q, k, scale, later, valid):
    z = _dot_nt(q, k) * scale
    lb = _log_sigmoid(z)
    lm = lb - z
    if valid is not None:
        lm = jnp.where(valid, lm, 0.0)
    suffix = _dot(jnp.concatenate(_split2(lm), axis=1), later)
    return lb, lm, suffix


def _later(t):
    r = lax.broadcasted_iota(jnp.int32, (2 * t, t), 0) % t
    c = lax.broadcasted_iota(jnp.int32, (2 * t, t), 1)
    return jnp.where(r > c, 1.0, 0.0).astype(BF16)


def _sb_fwd(proj, colq, colk, colv, nheads, carry=None):
    s = proj.shape[0]
    t = _tile(s, ATT_TILE)
    scale = HEAD_DIM ** -0.5
    hg = SB_FWD_GROUP
    gw = hg * HEAD_DIM
    assert nheads % hg == 0 and colq % hg == 0 and colk % hg == 0 and colv % hg == 0

    def body(q_ref, k_ref, v_ref, o_ref):
        qi = pl.program_id(1)
        later = _later(t)
        before = _tri_mask(t, True)

        def tile(kj, carry, diagonal):
            off = pl.multiple_of(kj * t, t)
            heads = range(hg)
            z = [_dot_nt(_head(q_ref, hh), _head(k_ref, hh, pl.ds(off, t))) * scale for hh in heads]
            lb = [_log_sigmoid(z[hh]) for hh in heads]
            lm = [lb[hh] - z[hh] for hh in heads]
            if diagonal:
                lm = [jnp.where(before, lm[hh], 0.0) for hh in heads]
            parts = [jnp.concatenate(_split2(lm[hh]), axis=1) for hh in heads]
            suffix = [_dot(parts[hh], later) for hh in heads]
            a = [jnp.exp(lb[hh] + suffix[hh] + carry[hh][0]) for hh in heads]
            if diagonal:
                a = [jnp.where(before, a[hh], 0.0) for hh in heads]
            av = [_dot(a[hh].astype(BF16), _head(v_ref, hh, pl.ds(off, t))) for hh in heads]
            return tuple((carry[hh][0] + jnp.sum(lm[hh], axis=-1, keepdims=True), carry[hh][1] + av[hh])
                         for hh in heads)

        init = tuple((jnp.zeros((t, 1), F32), jnp.zeros((t, HEAD_DIM), F32)) for _ in range(hg))
        carry = tile(qi, init, True)
        carry = lax.fori_loop(1, qi + 1, lambda i, c: tile(qi - i, c, False), carry)
        for hh in range(hg):
            o_ref[:, hh * HEAD_DIM:(hh + 1) * HEAD_DIM] = carry[hh][1].astype(BF16)

    return _call(
        body,
        grid=(nheads // hg, s // t),
        in_specs=[pl.BlockSpec((t, gw), lambda h, i: (i, colq // hg + h)),
                  pl.BlockSpec((s, gw), lambda h, i: (0, colk // hg + h), pipeline_mode=pl.Buffered(buffer_count=1)),
                  pl.BlockSpec((s, gw), lambda h, i: (0, colv // hg + h), pipeline_mode=pl.Buffered(buffer_count=1))],
        out_specs=[pl.BlockSpec((t, gw), lambda h, i: (i, h))],
        out_shape=[jax.ShapeDtypeStruct((s, nheads * HEAD_DIM), BF16)],
        scratch_shapes=[],
        semantics=("parallel", "parallel"),
        name="sb_fwd",
        args=(proj, proj, proj),
        carry=carry,
    )


def _sb_bwd(proj, colq, colk, colv, do, nheads, carry=None):
    s = proj.shape[0]
    t = _tile(s, ATT_TILE)
    scale = HEAD_DIM ** -0.5
    hg = SB_BWD_GROUP
    gw = hg * HEAD_DIM
    assert nheads % hg == 0 and colq % hg == 0 and colk % hg == 0 and colv % hg == 0

    def body(q_ref, k_ref, v_ref, do_ref, dq_ref, dk_ref, dv_ref, g_s, beta_s):
        qi = pl.program_id(1)

        @pl.when(qi == 0)
        def _():
            dk_ref[...] = jnp.zeros_like(dk_ref)
            dv_ref[...] = jnp.zeros_like(dv_ref)

        later = _later(t)
        before = _tri_mask(t, True)

        def back(kj, carry, diagonal):
            off = pl.multiple_of(kj * t, t)
            heads = range(hg)
            rows = pl.ds(off, t)
            z = [_dot_nt(_head(q_ref, hh), _head(k_ref, hh, rows)) * scale for hh in heads]
            da = [_dot_nt(_head(do_ref, hh), _head(v_ref, hh, rows)) for hh in heads]
            lb = [_log_sigmoid(z[hh]) for hh in heads]
            lm = [lb[hh] - z[hh] for hh in heads]
            if diagonal:
                lm = [jnp.where(before, lm[hh], 0.0) for hh in heads]
            parts = [jnp.concatenate(_split2(lm[hh]), axis=1) for hh in heads]
            suffix = [_dot(parts[hh], later) for hh in heads]
            a = [jnp.exp(lb[hh] + suffix[hh] + carry[hh]) for hh in heads]
            if diagonal:
                a = [jnp.where(before, a[hh], 0.0) for hh in heads]
            dv = [_dot_tn(a[hh].astype(BF16), _head(do_ref, hh)) for hh in heads]
            for hh in heads:
                g_s[hh, :, rows] = a[hh] * da[hh]
                beta_s[hh, :, rows] = jnp.exp(lb[hh]).astype(BF16)
            for hh in heads:
                dv_ref[rows, hh * HEAD_DIM:(hh + 1) * HEAD_DIM] += dv[hh]
            return tuple(carry[hh] + jnp.sum(lm[hh], axis=-1, keepdims=True) for hh in heads)

        rc = back(qi, tuple(jnp.zeros((t, 1), F32) for _ in range(hg)), True)
        lax.fori_loop(1, qi + 1, lambda i, c: back(qi - i, c, False), rc)

        earlier = jnp.where(lax.broadcasted_iota(jnp.int32, (2 * t, t), 0) % t
                            < lax.broadcasted_iota(jnp.int32, (2 * t, t), 1), 1.0, 0.0).astype(BF16)

        def fwd(kj, carry, diagonal):
            off = pl.multiple_of(kj * t, t)
            heads = range(hg)
            rows = pl.ds(off, t)
            g = [g_s[hh, :, rows] for hh in heads]
            parts = [jnp.concatenate(_split2(g[hh]), axis=1) for hh in heads]
            gsum = [_dot(parts[hh], earlier) + carry[hh][0] for hh in heads]
            dz = []
            for hh in heads:
                beta = beta_s[hh, :, rows].astype(F32)
                d = g[hh] * (1.0 - beta) - gsum[hh] * beta
                if diagonal:
                    d = jnp.where(before, d, 0.0)
                dz.append(d.astype(BF16))
            dk = [_dot_tn(dz[hh], _head(q_ref, hh)) * scale for hh in heads]
            dq = [_dot(dz[hh], _head(k_ref, hh, rows)) * scale for hh in heads]
            for hh in heads:
                dk_ref[rows, hh * HEAD_DIM:(hh + 1) * HEAD_DIM] += dk[hh]
            return tuple((carry[hh][0] + jnp.sum(g[hh], axis=-1, keepdims=True), carry[hh][1] + dq[hh])
                         for hh in heads)

        init = tuple((jnp.zeros((t, 1), F32), jnp.zeros((t, HEAD_DIM), F32)) for _ in range(hg))
        carry = lax.fori_loop(0, qi, lambda kj, c: fwd(kj, c, False), init)
        carry = fwd(qi, carry, True)
        for hh in range(hg):
            dq_ref[:, hh * HEAD_DIM:(hh + 1) * HEAD_DIM] = carry[hh][1]

    once = pl.Buffered(buffer_count=1)
    tile_spec = pl.BlockSpec((t, gw), lambda h, i: (i, h))
    full = pl.BlockSpec((s, gw), lambda h, i: (0, h), pipeline_mode=once)
    w = nheads * HEAD_DIM
    return _call(
        body,
        grid=(nheads // hg, s // t),
        in_specs=[pl.BlockSpec((t, gw), lambda h, i: (i, colq // hg + h)),
                  pl.BlockSpec((s, gw), lambda h, i: (0, colk // hg + h), pipeline_mode=once),
                  pl.BlockSpec((s, gw), lambda h, i: (0, colv // hg + h), pipeline_mode=once),
                  tile_spec],
        out_specs=[tile_spec, full, full],
        out_shape=[jax.ShapeDtypeStruct((s, w), F32)] * 3,
        scratch_shapes=[pltpu.VMEM((hg, t, s), F32), pltpu.VMEM((hg, t, s), BF16)],
        semantics=("arbitrary", "arbitrary"),
        name="sb_bwd",
        args=(proj, proj, proj, do),
        carry=carry,
    )


def _mem_fwd(qn, kn, mkv, nheads):
    s = qn.shape[0]
    mtok = kn.shape[0]
    t = _tile(s, MEM_ROWS)
    w = nheads * HEAD_DIM
    scale = HEAD_DIM ** -0.5

    def body(q_ref, k_ref, v_ref, o_ref):
        heads = range(nheads)
        sc = [_dot_nt(_head(q_ref, hh), _head(k_ref, hh)) * scale for hh in heads]
        p = [jnp.exp(sc[hh] - jnp.max(sc[hh], axis=-1, keepdims=True)) for hh in heads]
        p = [p[hh] / jnp.sum(p[hh], axis=-1, keepdims=True) for hh in heads]
        o = [_dot(p[hh].astype(BF16), _head(v_ref, hh)) for hh in heads]
        for hh in heads:
            o_ref[:, hh * HEAD_DIM:(hh + 1) * HEAD_DIM] = o[hh].astype(BF16)

    return pl.pallas_call(
        body,
        grid=(s // t,),
        in_specs=[pl.BlockSpec((t, w), lambda i: (i, 0)),
                  pl.BlockSpec((mtok, w), lambda i: (0, 0)),
                  pl.BlockSpec((mtok, w), lambda i: (0, 1))],
        out_specs=pl.BlockSpec((t, w), lambda i: (i, 0)),
        out_shape=jax.ShapeDtypeStruct((s, w), BF16),
        compiler_params=_params("parallel"),
        name="mem_fwd",
    )(qn, kn, mkv)


def _mem_bwd(qn, kn, mkv, do, nheads):
    s = qn.shape[0]
    mtok = kn.shape[0]
    t = _tile(s, MEM_ROWS)
    w = nheads * HEAD_DIM
    scale = HEAD_DIM ** -0.5

    def body(q_ref, k_ref, v_ref, do_ref, dq_ref, dk_ref, dv_ref):
        @pl.when(pl.program_id(0) == 0)
        def _():
            dk_ref[...] = jnp.zeros_like(dk_ref)
            dv_ref[...] = jnp.zeros_like(dv_ref)

        heads = range(nheads)
        sc = [_dot_nt(_head(q_ref, hh), _head(k_ref, hh)) * scale for hh in heads]
        dp = [_dot_nt(_head(do_ref, hh), _head(v_ref, hh)) for hh in heads]
        p = [jnp.exp(sc[hh] - jnp.max(sc[hh], axis=-1, keepdims=True)) for hh in heads]
        p = [p[hh] / jnp.sum(p[hh], axis=-1, keepdims=True) for hh in heads]
        ds = [(p[hh] * (dp[hh] - jnp.sum(p[hh] * dp[hh], axis=-1, keepdims=True))).astype(BF16) for hh in heads]
        dq = [_dot(ds[hh], _head(k_ref, hh)) * scale for hh in heads]
        dk = [_dot_tn(ds[hh], _head(q_ref, hh)) * scale for hh in heads]
        dv = [_dot_tn(p[hh].astype(BF16), _head(do_ref, hh)) for hh in heads]
        for hh in heads:
            cols = slice(hh * HEAD_DIM, (hh + 1) * HEAD_DIM)
            dq_ref[:, cols] = dq[hh]
            dk_ref[:, cols] += dk[hh]
            dv_ref[:, cols] += dv[hh]

    tile = pl.BlockSpec((t, w), lambda i: (i, 0))
    kspec = pl.BlockSpec((mtok, w), lambda i: (0, 0))
    return pl.pallas_call(
        body,
        grid=(s // t,),
        in_specs=[tile, kspec, pl.BlockSpec((mtok, w), lambda i: (0, 1)), tile],
        out_specs=[tile, kspec, kspec],
        out_shape=[jax.ShapeDtypeStruct((s, w), F32), jax.ShapeDtypeStruct((mtok, w), F32),
                   jax.ShapeDtypeStruct((mtok, w), F32)],
        compiler_params=_params("arbitrary"),
        name="mem_bwd",
    )(qn, kn, mkv, do)


def _merge_fwd(p0, p1, p2, gates, b_gate):
    s, d = p0.shape
    tm, tn = _tile(s, ROW_TILE), _tile(d, COL_TILE)
    nj = d // tn

    def body(p0_ref, p1_ref, p2_ref, ga_ref, gb_ref, gc_ref, b_ref, o_ref, sa_ref, sb_ref, sc_ref):
        acc = jnp.zeros((tm, tn), F32)
        for b, (p_ref, g_ref, s_ref) in enumerate(((p0_ref, ga_ref, sa_ref), (p1_ref, gb_ref, sb_ref),
                                                   (p2_ref, gc_ref, sc_ref))):
            gate = jax.nn.sigmoid(g_ref[...].astype(F32) + b_ref[b:b + 1, :])
            s_ref[...] = gate.astype(BF16)
            acc = acc + gate * p_ref[...]
        o_ref[...] = acc.astype(BF16)

    blk = pl.BlockSpec((tm, tn), lambda i, j: (i, j))
    merged, *sig = pl.pallas_call(
        body,
        grid=(s // tm, nj),
        in_specs=[blk] * 6 + [pl.BlockSpec((3, tn), lambda i, j: (0, j))],
        out_specs=[blk] * 4,
        out_shape=[jax.ShapeDtypeStruct((s, d), BF16)] * 4,
        compiler_params=_params("parallel", "parallel"),
        name="merge_fwd",
    )(p0, p1, p2, *gates, b_gate)
    return merged, tuple(sig)


def _merge_bwd(dmerged, p0, p1, p2, sig):
    s, d = p0.shape
    tm, tn = _tile(s, ROW_TILE), _tile(d, COL_TILE)
    nj = d // tn

    def body(dm_ref, p0_ref, p1_ref, p2_ref, ga_ref, gb_ref, gc_ref,
             d0_ref, d1_ref, d2_ref, dga_ref, dgb_ref, dgc_ref, db_ref):
        dm = dm_ref[...].astype(F32)
        parts = []
        for p_ref, g_ref, dp_ref, dg_ref in ((p0_ref, ga_ref, d0_ref, dga_ref), (p1_ref, gb_ref, d1_ref, dgb_ref),
                                             (p2_ref, gc_ref, d2_ref, dgc_ref)):
            gate = g_ref[...].astype(F32)
            dp_ref[...] = (dm * gate).astype(BF16)
            dgate = dm * p_ref[...] * gate * (1.0 - gate)
            dg_ref[...] = dgate.astype(BF16)
            parts.append(jnp.sum(dgate, axis=0, keepdims=True))
        part = jnp.concatenate(parts, axis=0)

        @pl.when(pl.program_id(1) == 0)
        def _():
            db_ref[...] = part

        @pl.when(pl.program_id(1) > 0)
        def _():
            db_ref[...] += part

    blk = pl.BlockSpec((tm, tn), lambda j, i: (i, j))
    bias = pl.BlockSpec((3, tn), lambda j, i: (0, j))
    return pl.pallas_call(
        body,
        grid=(nj, s // tm),
        in_specs=[blk] * 7,
        out_specs=[blk] * 6 + [bias],
        out_shape=[jax.ShapeDtypeStruct((s, d), BF16)] * 6 + [jax.ShapeDtypeStruct((3, d), F32)],
        compiler_params=_params("parallel", "arbitrary"),
        name="merge_bwd",
    )(dmerged, p0, p1, p2, *sig)


def _shift_down(v, n):
    rows = lax.broadcasted_iota(jnp.int32, v.shape, 0)
    return jnp.where(rows >= n, pltpu.roll(v, n, 0), 0.0)


def _shift_up(v, n):
    s = v.shape[0]
    rows = lax.broadcasted_iota(jnp.int32, v.shape, 0)
    return jnp.where(rows < s - n, pltpu.roll(v, s - n, 0), 0.0)


def _conv(v, w_ref, b_ref):
    taps = w_ref.shape[0]
    out = v * w_ref[taps - 1:taps, :] + b_ref[...]
    for n in range(1, taps):
        out = out + _shift_down(v, n) * w_ref[taps - 1 - n:taps - n, :]
    return out


def _conv_act_fwd(up, conv_w, conv_b):
    s, f2 = up.shape
    f = f2 // 2
    tn = LANES
    nj = f // tn
    taps = conv_w.shape[0]

    def body(ug_ref, uv_ref, wg_ref, wv_ref, bg_ref, bv_ref, o_ref, cg_ref, cv_ref):
        cg = _conv(ug_ref[...].astype(F32), wg_ref, bg_ref)
        cv = _conv(uv_ref[...].astype(F32), wv_ref, bv_ref)
        o_ref[...] = (cg * jax.nn.sigmoid(cg) * cv).astype(BF16)
        cg_ref[...] = cg.astype(BF16)
        cv_ref[...] = cv.astype(BF16)

    out = pl.BlockSpec((s, tn), lambda j: (0, j))
    return pl.pallas_call(
        body,
        grid=(nj,),
        in_specs=[pl.BlockSpec((s, tn), lambda j: (0, j)), pl.BlockSpec((s, tn), lambda j: (0, nj + j)),
                  pl.BlockSpec((taps, tn), lambda j: (0, j)), pl.BlockSpec((taps, tn), lambda j: (0, nj + j)),
                  pl.BlockSpec((1, tn), lambda j: (0, j)), pl.BlockSpec((1, tn), lambda j: (0, nj + j))],
        out_specs=[out, out, out],
        out_shape=[jax.ShapeDtypeStruct((s, f), BF16)] * 3,
        compiler_params=_params("parallel"),
        name="conv_act_fwd",
    )(up, up, conv_w, conv_w, conv_b, conv_b)


def _conv_act_bwd(up, conv_w, conv_g, conv_v, dact):
    s, f2 = up.shape
    f = f2 // 2
    tn = LANES
    nj = f // tn
    taps = conv_w.shape[0]

    def half(v, du, w_ref, dup_ref, dw_ref, db_ref):
        dup = du * w_ref[taps - 1:taps, :]
        rows = [None] * taps
        rows[taps - 1] = jnp.sum(du * v, axis=0, keepdims=True)
        for n in range(1, taps):
            later = _shift_up(du, n)
            dup = dup + later * w_ref[taps - 1 - n:taps - n, :]
            rows[taps - 1 - n] = jnp.sum(later * v, axis=0, keepdims=True)
        dup_ref[...] = dup.astype(BF16)
        dw_ref[...] = jnp.concatenate(rows, axis=0)
        db_ref[...] = jnp.sum(du, axis=0, keepdims=True)

    def body(ug_ref, uv_ref, wg_ref, wv_ref, cg_ref, cv_ref, da_ref,
             dug_ref, duv_ref, dwg_ref, dwv_ref, dbg_ref, dbv_ref):
        cg = cg_ref[...].astype(F32)
        cv = cv_ref[...].astype(F32)
        da = da_ref[...].astype(F32)
        sg = jax.nn.sigmoid(cg)
        dcv = da * cg * sg
        dcg = da * cv * (sg + cg * sg * (1.0 - sg))
        half(ug_ref[...].astype(F32), dcg, wg_ref, dug_ref, dwg_ref, dbg_ref)
        half(uv_ref[...].astype(F32), dcv, wv_ref, duv_ref, dwv_ref, dbv_ref)

    lo = lambda rows: pl.BlockSpec((rows, tn), lambda j: (0, j))
    hi = lambda rows: pl.BlockSpec((rows, tn), lambda j: (0, nj + j))
    return pl.pallas_call(
        body,
        grid=(nj,),
        in_specs=[lo(s), hi(s), lo(taps), hi(taps), lo(s), lo(s), lo(s)],
        out_specs=[lo(s), lo(s), lo(taps), lo(taps), lo(1), lo(1)],
        out_shape=[jax.ShapeDtypeStruct((s, f), BF16)] * 2 + [jax.ShapeDtypeStruct((taps, f), F32)] * 2
        + [jax.ShapeDtypeStruct((1, f), F32)] * 2,
        compiler_params=_params("parallel"),
        name="conv_act_bwd",
    )(up, up, conv_w, conv_w, conv_g, conv_v, dact)


def _row_tile(rows, row_bytes, budget):
    if rows * row_bytes <= budget or rows % 8:
        return rows
    best = 8
    for t in range(8, rows, 8):
        if rows % t == 0 and t * row_bytes <= budget:
            best = t
    return best


def _adamw(w, g, m, v, name):
    r, c = w.shape
    tr = _row_tile(r, c * 4, ADAM_BLOCK_BYTES)

    def body(w_ref, g_ref, m_ref, v_ref, d_ref, mo_ref, vo_ref):
        gg = g_ref[...]
        m_new = ADAM_B1 * m_ref[...] + (1.0 - ADAM_B1) * gg
        v_new = ADAM_B2 * v_ref[...] + (1.0 - ADAM_B2) * (gg * gg)
        m_hat = m_new / (1.0 - ADAM_B1 ** ADAM_STEP)
        v_hat = v_new / (1.0 - ADAM_B2 ** ADAM_STEP)
        d_ref[...] = -ADAM_LR * (m_hat / (jnp.sqrt(v_hat) + ADAM_EPS) + ADAM_WD * w_ref[...])
        mo_ref[...] = m_new
        vo_ref[...] = v_new

    blk = pl.BlockSpec((tr, c), lambda i: (i, 0))
    return pl.pallas_call(
        body,
        grid=(r // tr,),
        in_specs=[blk] * 4,
        out_specs=[blk] * 3,
        out_shape=[jax.ShapeDtypeStruct((r, c), F32)] * 3,
        compiler_params=_params("parallel"),
        name=name,
    )(w, g, m, v)


def _add_sibling(g, r1, core, name):
    _, _, h, c = g.shape
    th = _row_tile(h, c * 2, ADAM_BLOCK_BYTES)

    def body(core_ref, g_ref, r_ref, o_ref):
        o_ref[...] = (g_ref[...].astype(F32) + r_ref[...].astype(F32)).astype(BF16)

    return pl.pallas_call(
        body,
        grid_spec=pltpu.PrefetchScalarGridSpec(
            num_scalar_prefetch=1,
            grid=(N_CHIPS, h // th),
            in_specs=[pl.BlockSpec((None, None, th, c), lambda j, i, core_ref: (j, core_ref[0], i, 0)),
                      pl.BlockSpec((None, th, c), lambda j, i, core_ref: (j, i, 0))],
            out_specs=pl.BlockSpec((None, th, c), lambda j, i, core_ref: (j, i, 0)),
        ),
        out_shape=jax.ShapeDtypeStruct((N_CHIPS, h, c), BF16),
        compiler_params=_params("parallel", "parallel"),
        name=name,
    )(core, g, r1)


def _add_chips(hsum, r2, chip_core, name):
    _, h, c = hsum.shape
    th = _row_tile(h, c * 4, ADAM_BLOCK_BYTES)

    def body(sel_ref, own_ref, r_ref, o_ref):
        acc = own_ref[...].astype(F32)
        for j in range(N_CHIPS - 1):
            acc = acc + r_ref[j].astype(F32)
        o_ref[...] = acc

    return pl.pallas_call(
        body,
        grid_spec=pltpu.PrefetchScalarGridSpec(
            num_scalar_prefetch=1,
            grid=(h // th,),
            in_specs=[pl.BlockSpec((None, th, c), lambda i, sel_ref: (sel_ref[0], i, 0)),
                      pl.BlockSpec((N_CHIPS - 1, th, c), lambda i, sel_ref: (0, i, 0))],
            out_specs=pl.BlockSpec((None, th, c), lambda i, sel_ref: (sel_ref[1], i, 0)),
        ),
        out_shape=jax.ShapeDtypeStruct((2, h, c), F32),
        compiler_params=_params("parallel"),
        name=name,
    )(chip_core, hsum, r2)


def _sum_devices(parts):
    _, r, c = parts.shape

    def body(p_ref, o_ref):
        acc = p_ref[0]
        for j in range(1, N_DEV):
            acc = acc + p_ref[j]
        o_ref[...] = acc

    return pl.pallas_call(
        body,
        out_shape=jax.ShapeDtypeStruct((r, c), F32),
        compiler_params=pltpu.CompilerParams(vmem_limit_bytes=VMEM_LIMIT_BYTES),
        name="sum_devices",
    )(parts)


def _gather_small(vec):
    k = N_DEV - 1

    def body(v_ref, o_ref, send_sems, recv_sems, local_sem):
        x, y, c, _ = _place()
        me = 4 * x + 2 * y + c
        local = pltpu.make_async_copy(v_ref, o_ref.at[me], local_sem)
        local.start()
        peers = [(x ^ (r >> 2 & 1), y ^ (r >> 1 & 1), c ^ (r & 1)) for r in range(1, N_DEV)]
        sends = [_remote(v_ref, o_ref.at[me], send_sems.at[j], recv_sems.at[j], p) for j, p in enumerate(peers)]
        for cp in sends:
            cp.start()
        for j, (px, py, pc) in enumerate(peers):
            sends[j].wait_send()
            blk = o_ref.at[4 * px + 2 * py + pc]
            _remote(blk, blk, send_sems.at[j], recv_sems.at[j], (px, py, pc)).wait_recv()
        local.wait()

    return pl.pallas_call(
        body,
        in_specs=[ANY],
        out_specs=ANY,
        out_shape=jax.ShapeDtypeStruct((N_DEV,) + vec.shape, vec.dtype),
        scratch_shapes=[pltpu.SemaphoreType.DMA((k,)), pltpu.SemaphoreType.DMA((k,)), pltpu.SemaphoreType.DMA(())],
        name="gather_small",
    )(vec)


W_IN_GATES = ("w_in_g0", "w_in_g1", "w_in_g2")
ROW_SHARDED = ("w_in_a", "w_in_f") + W_IN_GATES + ("w_mem_kv", "w_out", "w_down")
COL_SHARDED = ("w_br_fox", "w_br_sb", "w_br_mem", "w_up")
BIG = ROW_SHARDED + COL_SHARDED


def _whole(name, a):
    if name in ROW_SHARDED:
        return a.reshape(N_CHIPS * a.shape[1], a.shape[2])
    return a.transpose(1, 0, 2).reshape(a.shape[1], N_CHIPS * a.shape[2])


def _by_shard(name, grad):
    if name in ROW_SHARDED:
        a = grad.reshape(N_CHIPS, grad.shape[0] // N_CHIPS, grad.shape[1])
    else:
        a = grad.reshape(grad.shape[0], N_CHIPS, grad.shape[1] // N_CHIPS).transpose(1, 0, 2)
    return a.reshape(N_CHIPS, 2, a.shape[1] // 2, a.shape[2])


def _sibling_sums(names, split, theirs, core):
    return [_add_sibling(a, r, core, "add_sibling_" + name) for name, a, r in zip(names, split, theirs)]


GATHER_FIRST = ("w_in_a", "w_in_f")
GATHER_EARLY = W_IN_GATES[:2]
GATHER_MIX = ("w_out", "w_br_fox", "w_br_sb", "w_br_mem") + W_IN_GATES[2:]
REDUCE_FFN = ("w_down", "w_up")
REDUCE_MIX = ("w_out", "w_br_fox", "w_br_sb", "w_br_mem", "w_mem_kv") + W_IN_GATES
REDUCE_IN = ("w_in_a", "w_in_f")


def _local_step(x, mem, target, w, shard, core, chip_core):
    d = x.shape[1]
    nf = shard["w_br_fox"].shape[0] // HEAD_DIM
    nsb = shard["w_br_sb"].shape[0] // HEAD_DIM
    nm = shard["w_br_mem"].shape[0] // HEAD_DIM
    w = dict(w)

    def take(names, gathered):
        for name, a in zip(names, gathered):
            w[name] = _whole(name, a)

    fq, fk, fv = 0, nf, 2 * nf
    sq, sk, sv = 3 * nf, 3 * nf + nsb, 3 * nf + 2 * nsb
    mq = 3 * nf + 3 * nsb

    h, rstd1, moved = _rms_fwd(x, w["g_mix"], "rms_mix_fwd", carry=_Gather([shard[name] for name in GATHER_FIRST]))
    take(GATHER_FIRST, moved)
    proj, moved = _mm(h, w["w_in_a"], "nn", BF16, "proj_att", carry=_Gather([shard[name] for name in GATHER_EARLY]))
    take(GATHER_EARLY, moved)
    gate0, moved = _mm(h, w["w_in_g0"], "nn", BF16, "proj_gate0", carry=_Gather([shard["w_mem_kv"]]))
    take(("w_mem_kv",), moved)
    gate1 = _mm(h, w["w_in_g1"], "nn", BF16, "proj_gate1")
    f_logit, c_sum = _forget_fwd(h, w["w_in_f"], w["b_forget"])
    c_t = c_sum[:, :nf].T
    c_col, c_row = c_t[:, :, None], c_t[:, None, :]
    qn = _headnorm_fwd(proj, fq, nf, w["g_q_fox"], "fox_qnorm_fwd")
    kn = _headnorm_fwd(proj, fk, nf, w["g_k_fox"], "fox_knorm_fwd")
    (o_fox, o_fox32, lse), moved = _fox_fwd(qn, kn, proj, fv, c_col, c_row, nf,
                                            carry=_Gather([shard[name] for name in GATHER_MIX]))
    take(GATHER_MIX, moved)
    gates = (gate0, gate1, _mm(h, w["w_in_g2"], "nn", BF16, "proj_gate2"))
    (o_sb,), moved = _sb_fwd(proj, sq, sk, sv, nsb, carry=_Gather([shard["w_up"]]))
    take(("w_up",), moved)
    memn, rstd_m = _rms_fwd(mem, w["g_mem"], "rms_mem_fwd")
    mkv = _mm(memn, w["w_mem_kv"], "nn", BF16, "mem_kv")
    kmn = _headnorm_fwd(mkv, 0, nm, w["g_k_mem"], "mem_knorm_fwd")
    qmn = _headnorm_fwd(proj, mq, nm, w["g_q_mem"], "mem_qnorm_fwd")
    o_mem = _mem_fwd(qmn, kmn, mkv, nm)
    p0 = _mm(o_fox, w["w_br_fox"], "nn", F32, "branch_fox")
    p1 = _mm(o_sb, w["w_br_sb"], "nn", F32, "branch_sb")
    p2 = _mm(o_mem, w["w_br_mem"], "nn", F32, "branch_mem")
    merged, sig = _merge_fwd(p0, p1, p2, gates, w["b_gate"])
    x1 = _mm(merged, w["w_out"], "nn", F32, "out_proj", residual=x)
    h2, rstd2 = _rms_fwd(x1, w["g_ffn"], "rms_ffn_fwd")
    up, moved = _mm(h2, w["w_up"], "nn", BF16, "ffn_up", carry=_Gather([shard["w_down"]]))
    take(("w_down",), moved)
    act, conv_g, conv_v = _conv_act_fwd(up, w["conv_w"], w["conv_b"])
    dy, dyb, lparts = _mm(act, w["w_down"], "nn", F32, "ffn_down_loss", residual=x1, loss_target=target)
    loss = (0.5 / d) * jnp.sum(lparts[::8, ::LANES])

    g = {}
    dact = _mm(dyb, w["w_down"], "nt", BF16, "ffn_down_dx")
    g["w_down"] = _mm(act, dyb, "tn", BF16, "ffn_down_dw")
    dug, duv, dwg, dwv, dbg, dbv = _conv_act_bwd(up, w["conv_w"], conv_g, conv_v, dact)
    dup = jnp.concatenate([dug, duv], axis=1)
    g["conv_w"] = jnp.concatenate([dwg, dwv], axis=1)
    g["conv_b"] = jnp.concatenate([dbg, dbv], axis=1)
    split_down = [_by_shard("w_down", g["w_down"])]
    dh2, theirs_down = _mm(dup, w["w_up"], "nt", BF16, "ffn_up_dx", carry=_Swap(split_down))
    g["w_up"] = _mm(h2, dup, "tn", BF16, "ffn_up_dw")
    split_up = [_by_shard("w_up", g["w_up"])]
    dx1, dx1b, g["g_ffn"] = _rms_bwd(dh2, x1, rstd2, w["g_ffn"], dy, "rms_ffn_bwd")
    dmerged, theirs_up = _mm(dx1b, w["w_out"], "nt", BF16, "out_proj_dx", carry=_Swap(split_up))
    sums_ffn = _sibling_sums(REDUCE_FFN, split_down + split_up, theirs_down + theirs_up, core)
    g["w_out"] = _mm(merged, dx1b, "tn", BF16, "out_proj_dw")
    dp0, dp1, dp2, dga, dgb, dgc, g["b_gate"] = _merge_bwd(dmerged, p0, p1, p2, sig)
    dgates = (dga, dgb, dgc)
    for name, dgate in zip(W_IN_GATES, dgates):
        g[name] = _mm(h, dgate, "tn", BF16, name + "_dw")
    do_fox = _mm(dp0, w["w_br_fox"], "nt", BF16, "branch_fox_dx")
    do_sb = _mm(dp1, w["w_br_sb"], "nt", BF16, "branch_sb_dx")
    do_mem = _mm(dp2, w["w_br_mem"], "nt", BF16, "branch_mem_dx")
    g["w_br_fox"] = _mm(o_fox, dp0, "tn", BF16, "branch_fox_dw")
    g["w_br_sb"] = _mm(o_sb, dp1, "tn", BF16, "branch_sb_dw")
    g["w_br_mem"] = _mm(o_mem, dp2, "tn", BF16, "branch_mem_dw")
    dqmn, dkmn, dvm = _mem_bwd(qmn, kmn, mkv, do_mem, nm)
    dmq, g["g_q_mem"] = _headnorm_bwd(dqmn, proj, mq, nm, w["g_q_mem"], "mem_qnorm_bwd")
    dkm, g["g_k_mem"] = _headnorm_bwd(dkmn, mkv, 0, nm, w["g_k_mem"], "mem_knorm_bwd")
    dmkv = jnp.concatenate([dkm, dvm.astype(BF16)], axis=1)
    g["w_mem_kv"] = _mm(memn, dmkv, "tn", BF16, "mem_kv_dw")
    dmemn = _mm(dmkv, w["w_mem_kv"], "nt", BF16, "mem_kv_dx")
    _, _, g["g_mem"] = _rms_bwd(dmemn, mem, rstd_m, w["g_mem"], None, "rms_mem_bwd")
    split_mix =[_by_shard(name, g[name]) for name in REDUCE_MIX]

    (dqn, dkn, dfv, drs, dcs), moved = _fox_bwd(qn, kn, proj, fv, c_col, c_row, o_fox32, do_fox, lse, nf,
                                                carry=_Both(_Scatter(sums_ffn[1:]), _Swap(split_mix)))
    others_up, theirs_mix = moved[:1], moved[1:]
    sums_mix = _sibling_sums(REDUCE_MIX, split_mix, theirs_mix, core)
    dfq, g["g_q_fox"] = _headnorm_bwd(dqn, proj, fq, nf, w["g_q_fox"], "fox_qnorm_bwd")
    dfk, g["g_k_fox"] = _headnorm_bwd(dkn, proj, fk, nf, w["g_k_fox"], "fox_knorm_bwd")
    dc = jnp.pad((drs[:, :, 0] - dcs[:, 0, :]).T, ((0, 0), (0, LANES - nf)))
    df, g["b_forget"] = _forget_bwd(dc, f_logit, w["b_forget"])
    (dsq, dsk, dsv), others_mix = _sb_bwd(proj, sq, sk, sv, do_sb, nsb, carry=_Scatter(sums_ffn[:1] + sums_mix))
    others_ffn = others_mix[:1] + others_up
    others_mix = others_mix[1:]

    dproj = jnp.concatenate([dfq, dfk, dfv.astype(BF16), dsq.astype(BF16), dsk.astype(BF16), dsv.astype(BF16), dmq],
                            axis=1)
    dfb = df.astype(BF16)
    g["w_in_a"] = _mm(h, dproj, "tn", BF16, "proj_att_dw")
    g["w_in_f"] = _mm(h, dfb, "tn", BF16, "proj_forget_dw")
    split_in = [_by_shard(name, g[name]) for name in REDUCE_IN]
    dh, theirs_in = _mm(dgates[0], w[W_IN_GATES[0]], "nt", F32, W_IN_GATES[0] + "_dx", carry=_Swap(split_in))
    sums_in = _sibling_sums(REDUCE_IN, split_in, theirs_in, core)
    for name, dgate in zip(W_IN_GATES[1:], dgates[1:]):
        dh = _mm(dgate, w[name], "nt", F32, name + "_dx", residual=dh)
    dh, others_in = _mm(dproj, w["w_in_a"], "nt", F32, "proj_att_dx", residual=dh, carry=_Scatter(sums_in))
    names = REDUCE_FFN + REDUCE_MIX + REDUCE_IN
    finals = [_add_chips(own, theirs, chip_core, "add_chips_" + name)
              for name, own, theirs in zip(names, sums_ffn + sums_mix + sums_in, others_ffn + others_mix + others_in)]
    grad_x, _, g["g_mix"], joined = _rms_bwd(dh, x, rstd1, w["g_mix"], dx1, "rms_mix_bwd", more=(dfb, w["w_in_f"]),
                                             carry=_Join(finals))
    summed = {name: a.reshape(2 * a.shape[1], a.shape[2]) for name, a in zip(names, joined)}
    return loss, grad_x, g, summed


SMALL = ("g_mix", "b_forget", "g_q_fox", "g_k_fox", "g_mem", "g_q_mem", "g_k_mem", "b_gate", "g_ffn", "conv_w",
         "conv_b")
SMALL_SHARDED = ("b_gate", "conv_w")
PACK_ROWS = 8


def _pack(arrs):
    flat = jnp.concatenate([a.reshape(-1) for a in arrs])
    unit = PACK_ROWS * LANES
    flat = jnp.pad(flat, (0, -flat.shape[0] % unit))
    return flat.reshape(-1, LANES)


def _unpack(packed, shapes):
    flat = packed.reshape(-1)
    out, at = [], 0
    for s in shapes:
        n = 1
        for dim in s:
            n *= dim
        out.append(flat[at:at + n].reshape(s))
        at += n
    return out


def kernel(x, mem, g_mix, w_in, b_forget, g_q_fox, g_k_fox, g_mem, w_mem_kv, g_q_mem, g_k_mem, w_br_fox, w_br_sb, w_br_mem, b_gate, w_out, g_ffn, w_up, conv_w, conv_b, w_down, loss_target, m_g_mix, m_w_in, m_b_forget, m_g_q_fox, m_g_k_fox, m_g_mem, m_w_mem_kv, m_g_q_mem, m_g_k_mem, m_w_br_fox, m_w_br_sb, m_w_br_mem, m_b_gate, m_w_out, m_g_ffn, m_w_up, m_conv_w, m_conv_b, m_w_down, v_g_mix, v_w_in, v_b_forget, v_g_q_fox, v_g_k_fox, v_g_mem, v_w_mem_kv, v_g_q_mem, v_g_k_mem, v_w_br_fox, v_w_br_sb, v_w_br_mem, v_b_gate, v_w_out, v_g_ffn, v_w_up, v_conv_w, v_conv_b, v_w_down):
    given = dict(g_mix=g_mix, w_in=w_in, b_forget=b_forget, g_q_fox=g_q_fox, g_k_fox=g_k_fox, g_mem=g_mem,
                 w_mem_kv=w_mem_kv, g_q_mem=g_q_mem, g_k_mem=g_k_mem, w_br_fox=w_br_fox, w_br_sb=w_br_sb,
                 w_br_mem=w_br_mem, b_gate=b_gate, w_out=w_out, g_ffn=g_ffn, w_up=w_up, conv_w=conv_w, conv_b=conv_b,
                 w_down=w_down)
    m_in = dict(g_mix=m_g_mix, w_in=m_w_in, b_forget=m_b_forget, g_q_fox=m_g_q_fox, g_k_fox=m_g_k_fox, g_mem=m_g_mem,
                w_mem_kv=m_w_mem_kv, g_q_mem=m_g_q_mem, g_k_mem=m_g_k_mem, w_br_fox=m_w_br_fox, w_br_sb=m_w_br_sb,
                w_br_mem=m_w_br_mem, b_gate=m_b_gate, w_out=m_w_out, g_ffn=m_g_ffn, w_up=m_w_up, conv_w=m_conv_w,
                conv_b=m_conv_b, w_down=m_w_down)
    v_in = dict(g_mix=v_g_mix, w_in=v_w_in, b_forget=v_b_forget, g_q_fox=v_g_q_fox, g_k_fox=v_g_k_fox, g_mem=v_g_mem,
                w_mem_kv=v_w_mem_kv, g_q_mem=v_g_q_mem, g_k_mem=v_g_k_mem, w_br_fox=v_w_br_fox, w_br_sb=v_w_br_sb,
                w_br_mem=v_w_br_mem, b_gate=v_b_gate, w_out=v_w_out, g_ffn=v_g_ffn, w_up=v_w_up, conv_w=v_conv_w,
                conv_b=v_conv_b, w_down=v_w_down)
    layered = {k: a.ndim == 3 for k, a in given.items()}
    drop = lambda a: a[0] if a.ndim == 3 else a
    given = {k: drop(a) for k, a in given.items()}
    m_in = {k: drop(a) for k, a in m_in.items()}
    v_in = {k: drop(a) for k, a in v_in.items()}

    xi, yi, ci = lax.axis_index("x"), lax.axis_index("y"), lax.axis_index("c")
    chip = (2 * xi + yi).astype(jnp.int32)
    core_arr = ci.astype(jnp.int32).reshape(1)
    chip_core = jnp.stack([chip, ci.astype(jnp.int32)])

    nf = given["b_forget"].shape[1]
    cut = 3 * given["w_br_fox"].shape[0]

    d_model = given["w_out"].shape[1]
    gate0 = given["w_in"].shape[1] - len(W_IN_GATES) * d_model
    shard = {
        "w_in_a": jnp.concatenate([given["w_in"][:, :cut], given["w_in"][:, cut + nf:gate0]], axis=1).astype(BF16),
        "w_in_f": jnp.pad(given["w_in"][:, cut:cut + nf], ((0, 0), (0, LANES - nf))).astype(BF16),
    }
    for b, name in enumerate(W_IN_GATES):
        shard[name] = given["w_in"][:, gate0 + b * d_model:gate0 + (b + 1) * d_model].astype(BF16)
    for name in BIG:
        if name not in shard:
            shard[name] = given[name].astype(BF16)
    w = {}
    small_shapes = [given[name].shape for name in SMALL_SHARDED]
    small_parts = _gather_small(_pack([given[name] for name in SMALL_SHARDED]))[0::2]
    per_chip = [_unpack(small_parts[j], small_shapes) for j in range(N_CHIPS)]
    for k, name in enumerate(SMALL_SHARDED):
        w[name] = jnp.concatenate([per_chip[j][k] for j in range(N_CHIPS)], axis=1)
    for name in SMALL:
        if name not in SMALL_SHARDED:
            w[name] = given[name]
    w["b_forget"] = jnp.pad(given["b_forget"], ((0, 0), (0, LANES - nf)))

    loss, grad_x, g, summed = _local_step(x[0], mem[0], loss_target[0], w, shard, core_arr, chip_core)
    loss = lax.psum(loss, ("x", "y", "c"))
    grads = {name: summed[name] for name in BIG if name in given}
    grads["w_in"] = jnp.concatenate([summed["w_in_a"][:, :cut], summed["w_in_f"][:, :nf], summed["w_in_a"][:, cut:]]
                                    + [summed[name] for name in W_IN_GATES], axis=1)

    g["b_forget"] = g["b_forget"][:, :nf]
    small_full_shapes = [g[name].shape for name in SMALL]
    small_sum = _unpack(_sum_devices(_gather_small(_pack([g[name] for name in SMALL]))), small_full_shapes)
    for name, a in zip(SMALL, small_sum):
        if name in SMALL_SHARDED:
            width = given[name].shape[1]
            a = lax.dynamic_slice_in_dim(a, chip * width, width, axis=1)
        grads[name] = a

    delta, new_m, new_v = {}, {}, {}
    for name in WEIGHTS:
        if name not in SMALL:
            delta[name], new_m[name], new_v[name] = _adamw(given[name], grads[name], m_in[name], v_in[name],
                                                           "adamw_" + name)
    shapes = [given[name].shape for name in SMALL]
    packed = [_pack([src[name] for name in SMALL]) for src in (given, grads, m_in, v_in)]
    for dst, res in zip((delta, new_m, new_v), _adamw(*packed, "adamw_small")):
        for name, a in zip(SMALL, _unpack(res, shapes)):
            dst[name] = a

    out = [loss, grad_x[None]]
    for src in (grads, delta, new_m, new_v):
        out.extend(src[name][None] if layered[name] else src[name] for name in WEIGHTS)
    return tuple(out)
```

```python
import functools

import jax
import jax.numpy as jnp
from jax import lax
from jax.experimental import pallas as pl
from jax.experimental.pallas import tpu as pltpu

F32 = jnp.float32
BF16 = jnp.bfloat16

HEAD_DIM = 128
EPS = 1e-6
NEG_BIG = -1e30

ADAM_LR = 0.001
ADAM_B1 = 0.9
ADAM_B2 = 0.999
ADAM_EPS = 1e-08
ADAM_WD = 0.01
ADAM_STEP = 10

LANES = 128
BF16_SUBLANES = 16
VMEM_LIMIT_BYTES = 56 * 1024 * 1024
MM_TILE = 1024
MM_TILE_K = {"nn": 2048, "nt": 2816, "tn": 4096}
ATT_TILE = 256
ROW_TILE = 256
HEADNORM_ROWS = 1024
MEM_ROWS = 1024
FORGET_ROWS = 512
COL_TILE = 1024
ADAM_BLOCK_BYTES = 2 << 20

N_CHIPS = 4
N_DEV = 8
MESH = pl.DeviceIdType.MESH

IN_NAMES = ['x', 'mem', 'g_mix', 'w_in', 'b_forget', 'g_q_fox', 'g_k_fox', 'g_mem', 'w_mem_kv', 'g_q_mem', 'g_k_mem',
            'w_br_fox', 'w_br_sb', 'w_br_mem', 'b_gate', 'w_out', 'g_ffn', 'w_up', 'conv_w', 'conv_b', 'w_down']
WEIGHTS = IN_NAMES[2:]


def _tile(n, target):
    if n <= target:
        return n
    for t in range(target - target % LANES, LANES - 1, -LANES):
        if n % t == 0:
            return t
    return n


def _params(*sem):
    return pltpu.CompilerParams(dimension_semantics=sem, vmem_limit_bytes=VMEM_LIMIT_BYTES)


def _log_sigmoid(z):
    return jnp.minimum(z, 0.0) - jnp.log(1.0 + jnp.exp(-jnp.abs(z)))


def _split2(v):
    hi = v.astype(BF16)
    lo = (v - hi.astype(F32)).astype(BF16)
    return hi, lo


def _split3(v):
    hi = v.astype(BF16)
    r = v - hi.astype(F32)
    mid = r.astype(BF16)
    lo = (r - mid.astype(F32)).astype(BF16)
    return hi, mid, lo


def _dot(a, b):
    return lax.dot_general(a, b, (((1,), (0,)), ((), ())), preferred_element_type=F32)


def _dot_nt(a, b):
    return lax.dot_general(a, b, (((1,), (1,)), ((), ())), preferred_element_type=F32)


def _dot_tn(a, b):
    return lax.dot_general(a, b, (((0,), (0,)), ((), ())), preferred_element_type=F32)


ANY = pl.BlockSpec(memory_space=pl.ANY)


def _place():
    x, y, c = lax.axis_index("x"), lax.axis_index("y"), lax.axis_index("c")
    others = [(1 - x, y), (x, 1 - y), (1 - x, 1 - y)]
    return x, y, c, others


def _remote(src, dst, send_sem, recv_sem, to):
    return pltpu.make_async_remote_copy(src_ref=src, dst_ref=dst, send_sem=send_sem, recv_sem=recv_sem,
                                        device_id=to, device_id_type=MESH)


class _Gather:
    PER_SHARD = 7

    def __init__(self, shards):
        self.inputs = list(shards)
        n = len(shards) * self.PER_SHARD
        self.out_shapes = [jax.ShapeDtypeStruct((N_CHIPS,) + s.shape, s.dtype) for s in shards]
        self.scratch = [pltpu.SemaphoreType.DMA((n,)), pltpu.SemaphoreType.DMA((n,))]

    def _first(self, ins, outs, sems):
        send_sems, recv_sems = sems
        x, y, c, others = _place()
        me = 2 * x + y
        k = self.PER_SHARD
        copies = []
        for i in range(len(ins)):
            h = ins[i].shape[0] // 2
            mine = pl.ds(pl.multiple_of(c * h, BF16_SUBLANES), h)
            for j, (ox, oy) in enumerate(others):
                copies.append(_remote(ins[i].at[mine], outs[i].at[me, mine], send_sems.at[k * i + j],
                                      recv_sems.at[k * i + j], (ox, oy, c)))
            copies.append(_remote(ins[i], outs[i].at[me], send_sems.at[k * i + 6], recv_sems.at[k * i + 6],
                                  (x, y, 1 - c)))
        return copies

    def start(self, ins, outs, sems):
        for cp in self._first(ins, outs, sems):
            cp.start()

    def finish(self, ins, outs, sems):
        send_sems, recv_sems = sems
        x, y, c, others = _place()
        me = 2 * x + y
        sibling = (x, y, 1 - c)
        k = self.PER_SHARD
        passed = []
        for i in range(len(ins)):
            h = ins[i].shape[0] // 2
            mine = pl.ds(pl.multiple_of(c * h, BF16_SUBLANES), h)
            for j, (ox, oy) in enumerate(others):
                blk = outs[i].at[2 * ox + oy, mine]
                _remote(blk, blk, send_sems.at[k * i + j], recv_sems.at[k * i + j], (ox, oy, c)).wait_recv()
                cp = _remote(blk, blk, send_sems.at[k * i + 3 + j], recv_sems.at[k * i + 3 + j], sibling)
                cp.start()
                passed.append(cp)
        for i in range(len(ins)):
            h = ins[i].shape[0] // 2
            theirs = pl.ds(pl.multiple_of((1 - c) * h, BF16_SUBLANES), h)
            for j, (ox, oy) in enumerate(others):
                blk = outs[i].at[2 * ox + oy, theirs]
                _remote(blk, blk, send_sems.at[k * i + 3 + j], recv_sems.at[k * i + 3 + j], sibling).wait_recv()
            own = outs[i].at[me]
            _remote(own, own, send_sems.at[k * i + 6], recv_sems.at[k * i + 6], sibling).wait_recv()
        for cp in self._first(ins, outs, sems) + passed:
            cp.wait_send()


class _Scatter:
    def __init__(self, sums):
        self.inputs = list(sums)
        k = N_CHIPS - 1
        self.out_shapes = [jax.ShapeDtypeStruct((k,) + g.shape[1:], g.dtype) for g in sums]
        self.scratch = [pltpu.SemaphoreType.DMA((k * len(sums),)), pltpu.SemaphoreType.DMA((k * len(sums),))]

    def _copies(self, ins, outs, sems):
        send_sems, recv_sems = sems
        _, _, c, others = _place()
        k = N_CHIPS - 1
        return [_remote(ins[i].at[2 * ox + oy], outs[i].at[j], send_sems.at[k * i + j], recv_sems.at[k * i + j],
                        (ox, oy, c))
                for i in range(len(ins)) for j, (ox, oy) in enumerate(others)]

    def start(self, ins, outs, sems):
        for cp in self._copies(ins, outs, sems):
            cp.start()

    def finish(self, ins, outs, sems):
        for cp in self._copies(ins, outs, sems):
            cp.wait()


class _Swap:
    def __init__(self, grads):
        self.inputs = list(grads)
        n = len(grads)
        self.out_shapes = [jax.ShapeDtypeStruct((g.shape[0],) + g.shape[2:], g.dtype) for g in grads]
        self.scratch = [pltpu.SemaphoreType.DMA((n,)), pltpu.SemaphoreType.DMA((n,))]

    def _copies(self, ins, outs, sems):
        send_sems, recv_sems = sems
        x, y, c, _ = _place()
        return [_remote(ins[i].at[:, 1 - c], outs[i], send_sems.at[i], recv_sems.at[i], (x, y, 1 - c))
                for i in range(len(ins))]

    def start(self, ins, outs, sems):
        for cp in self._copies(ins, outs, sems):
            cp.start()

    def finish(self, ins, outs, sems):
        for cp in self._copies(ins, outs, sems):
            cp.wait()


class _Join:
    def __init__(self, finals):
        self.inputs = list(finals)
        n = len(finals)
        self.out_shapes = [jax.ShapeDtypeStruct(f.shape, f.dtype) for f in finals]
        self.scratch = [pltpu.SemaphoreType.DMA((n,)), pltpu.SemaphoreType.DMA((n,))]
        self.aliases = {i: i for i in range(n)}

    def _sends(self, outs, sems):
        send_sems, recv_sems = sems
        x, y, c, _ = _place()
        return [_remote(outs[i].at[c], outs[i].at[c], send_sems.at[i], recv_sems.at[i], (x, y, 1 - c))
                for i in range(len(outs))]

    def start(self, ins, outs, sems):
        for cp in self._sends(outs, sems):
            cp.start()

    def finish(self, ins, outs, sems):
        send_sems, recv_sems = sems
        x, y, c, _ = _place()
        for i, cp in enumerate(self._sends(outs, sems)):
            cp.wait_send()
            other = outs[i].at[1 - c]
            _remote(other, other, send_sems.at[i], recv_sems.at[i], (x, y, 1 - c)).wait_recv()


class _Both:
    def __init__(self, first, second):
        self.parts = (first, second)
        self.inputs = first.inputs + second.inputs
        self.out_shapes = first.out_shapes + second.out_shapes
        self.scratch = first.scratch + second.scratch

    def _each(self, ins, outs, sems):
        first = self.parts[0]
        a, b, c = len(first.inputs), len(first.out_shapes), len(first.scratch)
        return ((first, ins[:a], outs[:b], sems[:c]), (self.parts[1], ins[a:], outs[b:], sems[c:]))

    def start(self, ins, outs, sems):
        for part, i, o, s in self._each(ins, outs, sems):
            part.start(i, o, s)

    def finish(self, ins, outs, sems):
        for part, i, o, s in self._each(ins, outs, sems):
            part.finish(i, o, s)


def _call(body, *, grid, in_specs, out_specs, out_shape, scratch_shapes, semantics, name, args, carry=None):
    n_in, n_out, n_scr = len(in_specs), len(out_specs), len(scratch_shapes)
    if carry is None:
        res = pl.pallas_call(body, grid=grid, in_specs=in_specs, out_specs=out_specs, out_shape=out_shape,
                             scratch_shapes=scratch_shapes, compiler_params=_params(*semantics), name=name)(*args)
        return list(res), []
    nci, nco = len(carry.inputs), len(carry.out_shapes)
    a, b = n_in, n_in + nci
    c, d = b + n_out, b + n_out + nco
    e = d + n_scr

    def carried(*refs):
        ids = [pl.program_id(k) for k in range(len(grid))]
        first = functools.reduce(jnp.logical_and, [i == 0 for i in ids])
        last = functools.reduce(jnp.logical_and, [i == n - 1 for i, n in zip(ids, grid)])

        @pl.when(first)
        def _():
            carry.start(refs[a:b], refs[c:d], refs[e:])

        body(*refs[:a], *refs[b:c], *refs[d:e])

        @pl.when(last)
        def _():
            carry.finish(refs[a:b], refs[c:d], refs[e:])

    res = pl.pallas_call(
        carried,
        grid=grid,
        in_specs=list(in_specs) + [ANY] * nci,
        out_specs=list(out_specs) + [ANY] * nco,
        out_shape=list(out_shape) + carry.out_shapes,
        scratch_shapes=list(scratch_shapes) + carry.scratch,
        input_output_aliases={n_in + i: n_out + o for i, o in getattr(carry, "aliases", {}).items()},
        compiler_params=_params(*(["arbitrary"] * len(grid))),
        name=name,
    )(*args, *carry.inputs)
    return list(res[:n_out]), list(res[n_out:])


def _mm(a, b, mode, out_dtype, name, residual=None, carry=None, loss_target=None):
    if mode == "nn":
        (m, k), (k2, n) = a.shape, b.shape
    elif mode == "nt":
        (m, k), (n, k2) = a.shape, b.shape
    else:
        (k, m), (k2, n) = a.shape, b.shape
    assert k == k2, (a.shape, b.shape, mode)
    has_res = residual is not None
    has_loss = loss_target is not None
    n_in = 2 + has_res + has_loss
    tm, tn, tk = _tile(m, MM_TILE), _tile(n, MM_TILE), _tile(k, MM_TILE_K[mode])
    nk = k // tk
    dot = {"nn": _dot, "nt": _dot_nt, "tn": _dot_tn}[mode]

    def body(*refs):
        a_ref, b_ref = refs[:2]
        r_ref = refs[2] if has_res else None
        t_ref = refs[n_in - 1] if has_loss else None
        o_ref = refs[n_in]

        def finish(acc):
            if has_res:
                acc = acc + r_ref[...]
            if has_loss:
                err = acc - t_ref[...]
                dy = err * (1.0 / n)
                o_ref[...] = dy
                refs[n_in + 1][...] = dy.astype(BF16)
                tot = jnp.sum(jnp.sum(err * err, axis=-1, keepdims=True), axis=0, keepdims=True)
                refs[n_in + 2][...] = jnp.broadcast_to(tot, (8, LANES))
            else:
                o_ref[...] = acc.astype(o_ref.dtype)

        part = dot(a_ref[...], b_ref[...])
        if nk == 1:
            finish(part)
        else:
            acc_ref = refs[-1]
            kk = pl.program_id(2)

            @pl.when(kk == 0)
            def _():
                acc_ref[...] = part

            @pl.when(kk > 0)
            def _():
                acc_ref[...] += part

            @pl.when(kk == nk - 1)
            def _():
                finish(acc_ref[...])

    if mode == "tn":
        a_spec = pl.BlockSpec((tk, tm), lambda j, i, kk: (kk, i))
    else:
        a_spec = pl.BlockSpec((tm, tk), lambda j, i, kk: (i, kk))
    if mode == "nt":
        b_spec = pl.BlockSpec((tn, tk), lambda j, i, kk: (j, kk))
    else:
        b_spec = pl.BlockSpec((tk, tn), lambda j, i, kk: (kk, j))
    o_spec = pl.BlockSpec((tm, tn), lambda j, i, kk: (i, j))
    in_specs = [a_spec, b_spec] + [o_spec] * (has_res + has_loss)
    args = (a, b) + ((residual,) if has_res else ()) + ((loss_target,) if has_loss else ())
    out_specs, out_shape = [o_spec], [jax.ShapeDtypeStruct((m, n), out_dtype)]
    if has_loss:
        out_specs += [o_spec, pl.BlockSpec((8, LANES), lambda j, i, kk: (i, j))]
        out_shape += [jax.ShapeDtypeStruct((m, n), BF16), jax.ShapeDtypeStruct((m // tm * 8, n // tn * LANES), F32)]
    outs, moved = _call(
        body,
        grid=(n // tn, m // tm, nk),
        in_specs=in_specs,
        out_specs=out_specs,
        out_shape=out_shape,
        scratch_shapes=[pltpu.VMEM((tm, tn), F32)] if nk > 1 else [],
        semantics=("parallel", "parallel", "arbitrary"),
        name=name,
        args=args,
        carry=carry,
    )
    out = outs if has_loss else outs[0]
    return out if carry is None else (out, moved)


def _rms_fwd(x, g, name, carry=None):
    s, d = x.shape
    tm = _tile(s, ROW_TILE)

    def body(x_ref, g_ref, h_ref, r_ref):
        xf = x_ref[...]
        r = lax.rsqrt(jnp.mean(xf * xf, axis=-1, keepdims=True) + EPS)
        h_ref[...] = ((xf * r) * g_ref[...]).astype(BF16)
        r_ref[...] = r

    (h, rstd), moved = _call(
        body,
        grid=(s // tm,),
        in_specs=[pl.BlockSpec((tm, d), lambda i: (i, 0)), pl.BlockSpec((1, d), lambda i: (0, 0))],
        out_specs=[pl.BlockSpec((tm, d), lambda i: (i, 0)), pl.BlockSpec((tm, 1), lambda i: (i, 0))],
        out_shape=[jax.ShapeDtypeStruct((s, d), BF16), jax.ShapeDtypeStruct((s, 1), F32)],
        scratch_shapes=[],
        semantics=("parallel",),
        name=name,
        args=(x, g),
        carry=carry,
    )
    return (h, rstd) if carry is None else (h, rstd, moved)


def _rms_bwd(dh, x, rstd, g, res, name, more=None, carry=None):
    s, d = x.shape
    tm = _tile(s, ROW_TILE)
    has_res = res is not None
    has_more = more is not None
    n_in = 4 + has_res + 2 * has_more

    def body(*refs):
        dh_ref, x_ref, r_ref, g_ref = refs[:4]
        res_ref = refs[4] if has_res else None
        dx_ref, dxb_ref, dg_ref = refs[n_in:]
        dhf = dh_ref[...].astype(F32)
        if has_more:
            dhf = dhf + _dot_nt(refs[n_in - 2][...], refs[n_in - 1][...])
        xhat = x_ref[...] * r_ref[...]
        dy = dhf * g_ref[...]
        dx = r_ref[...] * (dy - xhat * jnp.mean(dy * xhat, axis=-1, keepdims=True))
        if has_res:
            dx = dx + res_ref[...]
        dx_ref[...] = dx
        dxb_ref[...] = dx.astype(BF16)
        part = jnp.sum(dhf * xhat, axis=0, keepdims=True)

        @pl.when(pl.program_id(0) == 0)
        def _():
            dg_ref[...] = part

        @pl.when(pl.program_id(0) > 0)
        def _():
            dg_ref[...] += part

    row = pl.BlockSpec((tm, d), lambda i: (i, 0))
    vec = pl.BlockSpec((1, d), lambda i: (0, 0))
    in_specs = [row, row, pl.BlockSpec((tm, 1), lambda i: (i, 0)), vec] + ([row] if has_res else [])
    args = (dh, x, rstd, g) + ((res,) if has_res else ())
    if has_more:
        k = more[0].shape[1]
        in_specs += [pl.BlockSpec((tm, k), lambda i: (i, 0)), pl.BlockSpec((d, k), lambda i: (0, 0))]
        args += tuple(more)
    (dx, dxb, dg), moved = _call(
        body,
        grid=(s // tm,),
        in_specs=in_specs,
        out_specs=[row, row, vec],
        out_shape=[jax.ShapeDtypeStruct((s, d), F32), jax.ShapeDtypeStruct((s, d), BF16),
                   jax.ShapeDtypeStruct((1, d), F32)],
        scratch_shapes=[],
        semantics=("arbitrary",),
        name=name,
        args=args,
        carry=carry,
    )
    return (dx, dxb, dg) if carry is None else (dx, dxb, dg, moved)


def _headnorm_fwd(src, col0, nheads, g, name):
    s = src.shape[0]
    tm = _tile(s, HEADNORM_ROWS)
    w = nheads * HEAD_DIM
    assert col0 % nheads == 0

    def body(x_ref, g_ref, o_ref):
        for hh in range(nheads):
            xf = _head(x_ref, hh).astype(F32)
            r = lax.rsqrt(jnp.mean(xf * xf, axis=-1, keepdims=True) + EPS)
            o_ref[:, hh * HEAD_DIM:(hh + 1) * HEAD_DIM] = ((xf * r) * g_ref[...]).astype(BF16)

    return pl.pallas_call(
        body,
        grid=(s // tm,),
        in_specs=[pl.BlockSpec((tm, w), lambda i: (i, col0 // nheads)),
                  pl.BlockSpec((1, HEAD_DIM), lambda i: (0, 0))],
        out_specs=pl.BlockSpec((tm, w), lambda i: (i, 0)),
        out_shape=jax.ShapeDtypeStruct((s, w), BF16),
        compiler_params=_params("parallel"),
        name=name,
    )(src, g)


def _headnorm_bwd(dxn, src, col0, nheads, g, name):
    s = src.shape[0]
    tm = _tile(s, HEADNORM_ROWS)
    w = nheads * HEAD_DIM
    assert col0 % nheads == 0

    def body(d_ref, x_ref, g_ref, dx_ref, dg_ref):
        part = jnp.zeros((1, HEAD_DIM), F32)
        for hh in range(nheads):
            xf = _head(x_ref, hh).astype(F32)
            r = lax.rsqrt(jnp.mean(xf * xf, axis=-1, keepdims=True) + EPS)
            xhat = xf * r
            dn = _head(d_ref, hh).astype(F32)
            dy = dn * g_ref[...]
            dx = r * (dy - xhat * jnp.mean(dy * xhat, axis=-1, keepdims=True))
            dx_ref[:, hh * HEAD_DIM:(hh + 1) * HEAD_DIM] = dx.astype(BF16)
            part = part + jnp.sum(dn * xhat, axis=0, keepdims=True)

        @pl.when(pl.program_id(0) == 0)
        def _():
            dg_ref[...] = part

        @pl.when(pl.program_id(0) > 0)
        def _():
            dg_ref[...] += part

    return pl.pallas_call(
        body,
        grid=(s // tm,),
        in_specs=[pl.BlockSpec((tm, w), lambda i: (i, 0)),
                  pl.BlockSpec((tm, w), lambda i: (i, col0 // nheads)),
                  pl.BlockSpec((1, HEAD_DIM), lambda i: (0, 0))],
        out_specs=[pl.BlockSpec((tm, w), lambda i: (i, 0)),
                   pl.BlockSpec((1, HEAD_DIM), lambda i: (0, 0))],
        out_shape=[jax.ShapeDtypeStruct((s, w), BF16), jax.ShapeDtypeStruct((1, HEAD_DIM), F32)],
        compiler_params=_params("arbitrary"),
        name=name,
    )(dxn, src, g)


def _tri(t, lower_inclusive):
    r = lax.broadcasted_iota(jnp.int32, (t, t), 0)
    c = lax.broadcasted_iota(jnp.int32, (t, t), 1)
    keep = (c <= r) if lower_inclusive else (c >= r)
    return jnp.where(keep, 1.0, 0.0).astype(BF16)


def _forget_fwd(h, w_f, b_pad):
    s, d = h.shape
    t = _tile(s, FORGET_ROWS)

    def body(h_ref, w_ref, b_ref, f_ref, c_ref, carry):
        @pl.when(pl.program_id(0) == 0)
        def _():
            carry[...] = jnp.zeros_like(carry)

        f = _dot(h_ref[...], w_ref[...])
        f_ref[...] = f
        lf = _log_sigmoid(f + b_ref[...])
        tri = _tri(t, True)
        acc = carry[...]
        for part in _split3(lf):
            acc = acc + _dot(tri, part)
        c_ref[...] = acc
        carry[...] += jnp.sum(lf, axis=0, keepdims=True)

    blk = pl.BlockSpec((t, LANES), lambda i: (i, 0))
    return pl.pallas_call(
        body,
        grid=(s // t,),
        in_specs=[pl.BlockSpec((t, d), lambda i: (i, 0)), pl.BlockSpec((d, LANES), lambda i: (0, 0)),
                  pl.BlockSpec((1, LANES), lambda i: (0, 0))],
        out_specs=[blk, blk],
        out_shape=[jax.ShapeDtypeStruct((s, LANES), F32)] * 2,
        scratch_shapes=[pltpu.VMEM((1, LANES), F32)],
        compiler_params=_params("arbitrary"),
        name="forget_fwd",
    )(h, w_f, b_pad)


def _forget_bwd(dc, f_logit, b_pad):
    s = f_logit.shape[0]
    t = _tile(s, FORGET_ROWS)
    nb = s // t

    def body(dc_ref, f_ref, b_ref, df_ref, db_ref, carry):
        @pl.when(pl.program_id(0) == 0)
        def _():
            carry[...] = jnp.zeros_like(carry)
            db_ref[...] = jnp.zeros_like(db_ref)

        d = dc_ref[...]
        tri = _tri(t, False)
        acc = carry[...]
        for part in _split3(d):
            acc = acc + _dot(tri, part)
        z = f_ref[...] + b_ref[...]
        df = acc * jnp.exp(_log_sigmoid(-z))
        df_ref[...] = df
        db_ref[...] += jnp.sum(df, axis=0, keepdims=True)
        carry[...] += jnp.sum(d, axis=0, keepdims=True)

    rev = pl.BlockSpec((t, LANES), lambda i: (nb - 1 - i, 0))
    vec = pl.BlockSpec((1, LANES), lambda i: (0, 0))
    return pl.pallas_call(
        body,
        grid=(nb,),
        in_specs=[rev, rev, vec],
        out_specs=[rev, vec],
        out_shape=[jax.ShapeDtypeStruct((s, LANES), F32), jax.ShapeDtypeStruct((1, LANES), F32)],
        scratch_shapes=[pltpu.VMEM((1, LANES), F32)],
        compiler_params=_params("arbitrary"),
        name="forget_bwd",
    )(dc, f_logit, b_pad)


SB_FWD_GROUP = 6
FOX_GROUP = 6
SB_BWD_GROUP = 3


def _head(ref, hh, rows=slice(None)):
    return ref[rows, hh * HEAD_DIM:(hh + 1) * HEAD_DIM]


def _tri_mask(t, strict):
    r = lax.broadcasted_iota(jnp.int32, (t, t), 0)
    c = lax.broadcasted_iota(jnp.int32, (t, t), 1)
    return (c < r) if strict else (c <= r)


def _fox_fwd(qn, kn, proj, colv, c_col, c_row, nheads, carry=None):
    s = qn.shape[0]
    t = _tile(s, ATT_TILE)
    scale = HEAD_DIM ** -0.5
    hg = FOX_GROUP
    gw = hg * HEAD_DIM
    assert nheads % hg == 0 and colv % hg == 0

    def body(q_ref, k_ref, v_ref, cc_ref, cr_ref, o_ref, of_ref, lse_ref):
        qi = pl.program_id(1)
        causal = _tri_mask(t, False)

        def tile(kj, carry, diagonal):
            off = pl.multiple_of(kj * t, t)
            heads = range(hg)
            rows = pl.ds(off, t)
            qk = [_dot_nt(_head(q_ref, hh), _head(k_ref, hh, rows)) for hh in heads]
            sc = [qk[hh] * scale + (cc_ref[hh] - cr_ref[hh, :, rows]) for hh in heads]
            if diagonal:
                sc = [jnp.where(causal, sc[hh], NEG_BIG) for hh in heads]
            m_new = [jnp.maximum(carry[hh][0], jnp.max(sc[hh], axis=-1, keepdims=True)) for hh in heads]
            p = [jnp.exp(sc[hh] - m_new[hh]) for hh in heads]
            pv = [_dot(p[hh].astype(BF16), _head(v_ref, hh, rows)) for hh in heads]
            out = []
            for hh in heads:
                m, l, acc = carry[hh]
                alpha = jnp.exp(m - m_new[hh])
                out.append((m_new[hh], alpha * l + jnp.sum(p[hh], axis=-1, keepdims=True), alpha * acc + pv[hh]))
            return tuple(out)

        init = tuple((jnp.full((t, 1), NEG_BIG, F32), jnp.zeros((t, 1), F32), jnp.zeros((t, HEAD_DIM), F32))
                     for _ in range(hg))
        carry = lax.fori_loop(0, qi, lambda kj, c: tile(kj, c, False), init)
        carry = tile(qi, carry, True)
        for hh in range(hg):
            m, l, acc = carry[hh]
            o = acc / l
            of_ref[:, hh * HEAD_DIM:(hh + 1) * HEAD_DIM] = o
            o_ref[:, hh * HEAD_DIM:(hh + 1) * HEAD_DIM] = o.astype(BF16)
            lse_ref[hh] = m + jnp.log(l)

    tile_spec = pl.BlockSpec((t, gw), lambda h, i: (i, h))
    w = nheads * HEAD_DIM
    return _call(
        body,
        grid=(nheads // hg, s // t),
        in_specs=[tile_spec,
                  pl.BlockSpec((s, gw), lambda h, i: (0, h), pipeline_mode=pl.Buffered(buffer_count=1)),
                  pl.BlockSpec((s, gw), lambda h, i: (0, colv // hg + h), pipeline_mode=pl.Buffered(buffer_count=1)),
                  pl.BlockSpec((hg, t, 1), lambda h, i: (h, i, 0)),
                  pl.BlockSpec((hg, 1, s), lambda h, i: (h, 0, 0))],
        out_specs=[tile_spec, tile_spec, pl.BlockSpec((hg, t, 1), lambda h, i: (h, i, 0))],
        out_shape=[jax.ShapeDtypeStruct((s, w), BF16), jax.ShapeDtypeStruct((s, w), F32),
                   jax.ShapeDtypeStruct((nheads, s, 1), F32)],
        scratch_shapes=[],
        semantics=("parallel", "parallel"),
        name="fox_fwd",
        args=(qn, kn, proj, c_col, c_row),
        carry=carry,
    )


def _fox_bwd(qn, kn, proj, colv, c_col, c_row, o, do, lse, nheads, carry=None):
    s = qn.shape[0]
    t = _tile(s, ATT_TILE)
    scale = HEAD_DIM ** -0.5
    hg = FOX_GROUP
    gw = hg * HEAD_DIM
    assert nheads % hg == 0 and colv % hg == 0

    def body(q_ref, k_ref, v_ref, cc_ref, cr_ref, o_ref, do_ref, lse_ref,
             dq_ref, dk_ref, dv_ref, drs_ref, dcs_ref):
        qi = pl.program_id(1)

        @pl.when(qi == 0)
        def _():
            dk_ref[...] = jnp.zeros_like(dk_ref)
            dv_ref[...] = jnp.zeros_like(dv_ref)
            dcs_ref[...] = jnp.zeros_like(dcs_ref)

        causal = _tri_mask(t, False)
        delta = [jnp.sum(_head(o_ref, hh) * _head(do_ref, hh).astype(F32), axis=-1, keepdims=True)
                 for hh in range(hg)]

        def tile(kj, carry, diagonal):
            off = pl.multiple_of(kj * t, t)
            heads = range(hg)
            rows = pl.ds(off, t)
            qk = [_dot_nt(_head(q_ref, hh), _head(k_ref, hh, rows)) for hh in heads]
            dp = [_dot_nt(_head(do_ref, hh), _head(v_ref, hh, rows)) for hh in heads]
            p = [jnp.exp(qk[hh] * scale + (cc_ref[hh] - cr_ref[hh, :, rows]) - lse_ref[hh]) for hh in heads]
            if diagonal:
                p = [jnp.where(causal, p[hh], 0.0) for hh in heads]
            ds = [p[hh] * (dp[hh] - delta[hh]) for hh in heads]
            dsb = [ds[hh].astype(BF16) for hh in heads]
            dv = [_dot_tn(p[hh].astype(BF16), _head(do_ref, hh)) for hh in heads]
            dk = [_dot_tn(dsb[hh], _head(q_ref, hh)) * scale for hh in heads]
            dq = [_dot(dsb[hh], _head(k_ref, hh, rows)) * scale for hh in heads]
            for hh in heads:
                cols = slice(hh * HEAD_DIM, (hh + 1) * HEAD_DIM)
                dv_ref[rows, cols] += dv[hh]
                dk_ref[rows, cols] += dk[hh]
                dcs_ref[hh, :, rows] += jnp.sum(ds[hh], axis=0, keepdims=True)
            return tuple((carry[hh][0] + dq[hh], carry[hh][1] + jnp.sum(ds[hh], axis=-1, keepdims=True))
                         for hh in heads)

        init = tuple((jnp.zeros((t, HEAD_DIM), F32), jnp.zeros((t, 1), F32)) for _ in range(hg))
        carry = lax.fori_loop(0, qi, lambda kj, c: tile(kj, c, False), init)
        carry = tile(qi, carry, True)
        for hh in range(hg):
            dq_ref[:, hh * HEAD_DIM:(hh + 1) * HEAD_DIM] = carry[hh][0]
            drs_ref[hh] = carry[hh][1]

    tile_spec = pl.BlockSpec((t, gw), lambda h, i: (i, h))
    once = pl.Buffered(buffer_count=1)
    full = pl.BlockSpec((s, gw), lambda h, i: (0, h), pipeline_mode=once)
    colspec = pl.BlockSpec((hg, t, 1), lambda h, i: (h, i, 0))
    rowspec = pl.BlockSpec((hg, 1, s), lambda h, i: (h, 0, 0))
    w = nheads * HEAD_DIM
    return _call(
        body,
        grid=(nheads // hg, s // t),
        in_specs=[tile_spec, full, pl.BlockSpec((s, gw), lambda h, i: (0, colv // hg + h), pipeline_mode=once),
                  colspec, rowspec,
                  tile_spec, tile_spec, colspec],
        out_specs=[tile_spec, full, full, colspec, rowspec],
        out_shape=[jax.ShapeDtypeStruct((s, w), F32), jax.ShapeDtypeStruct((s, w), F32),
                   jax.ShapeDtypeStruct((s, w), F32), jax.ShapeDtypeStruct((nheads, s, 1), F32),
                   jax.ShapeDtypeStruct((nheads, 1, s), F32)],
        scratch_shapes=[],
        semantics=("arbitrary", "arbitrary"),
        name="fox_bwd",
        args=(qn, kn, proj, c_col, c_row, o, do, lse),
        carry=carry,
    )


def _sb_tile(q, k, scale, later, valid):
    z = _dot_nt(q, k) * scale
    lb = _log_sigmoid(z)
    lm = lb - z
    if valid is not None:
        lm = jnp.where(valid, lm, 0.0)
    suffix = _dot(jnp.concatenate(_split2(lm), axis=1), later)
    return lb, lm, suffix


def _later(t):
    r = lax.broadcasted_iota(jnp.int32, (2 * t, t), 0) % t
    c = lax.broadcasted_iota(jnp.int32, (2 * t, t), 1)
    return jnp.where(r > c, 1.0, 0.0).astype(BF16)


def _sb_fwd(proj, colq, colk, colv, nheads, carry=None):
    s = proj.shape[0]
    t = _tile(s, ATT_TILE)
    scale = HEAD_DIM ** -0.5
    hg = SB_FWD_GROUP
    gw = hg * HEAD_DIM
    assert nheads % hg == 0 and colq % hg == 0 and colk % hg == 0 and colv % hg == 0

    def body(q_ref, k_ref, v_ref, o_ref):
        qi = pl.program_id(1)
        later = _later(t)
        before = _tri_mask(t, True)

        def tile(kj, carry, diagonal):
            off = pl.multiple_of(kj * t, t)
            heads = range(hg)
            z = [_dot_nt(_head(q_ref, hh), _head(k_ref, hh, pl.ds(off, t))) * scale for hh in heads]
            lb = [_log_sigmoid(z[hh]) for hh in heads]
            lm = [lb[hh] - z[hh] for hh in heads]
            if diagonal:
                lm = [jnp.where(before, lm[hh], 0.0) for hh in heads]
            parts = [jnp.concatenate(_split2(lm[hh]), axis=1) for hh in heads]
            suffix = [_dot(parts[hh], later) for hh in heads]
            a = [jnp.exp(lb[hh] + suffix[hh] + carry[hh][0]) for hh in heads]
            if diagonal:
                a = [jnp.where(before, a[hh], 0.0) for hh in heads]
            av = [_dot(a[hh].astype(BF16), _head(v_ref, hh, pl.ds(off, t))) for hh in heads]
            return tuple((carry[hh][0] + jnp.sum(lm[hh], axis=-1, keepdims=True), carry[hh][1] + av[hh])
                         for hh in heads)

        init = tuple((jnp.zeros((t, 1), F32), jnp.zeros((t, HEAD_DIM), F32)) for _ in range(hg))
        carry = tile(qi, init, True)
        carry = lax.fori_loop(1, qi + 1, lambda i, c: tile(qi - i, c, False), carry)
        for hh in range(hg):
            o_ref[:, hh * HEAD_DIM:(hh + 1) * HEAD_DIM] = carry[hh][1].astype(BF16)

    return _call(
        body,
        grid=(nheads // hg, s // t),
        in_specs=[pl.BlockSpec((t, gw), lambda h, i: (i, colq // hg + h)),
                  pl.BlockSpec((s, gw), lambda h, i: (0, colk // hg + h), pipeline_mode=pl.Buffered(buffer_count=1)),
                  pl.BlockSpec((s, gw), lambda h, i: (0, colv // hg + h), pipeline_mode=pl.Buffered(buffer_count=1))],
        out_specs=[pl.BlockSpec((t, gw), lambda h, i: (i, h))],
        out_shape=[jax.ShapeDtypeStruct((s, nheads * HEAD_DIM), BF16)],
        scratch_shapes=[],
        semantics=("parallel", "parallel"),
        name="sb_fwd",
        args=(proj, proj, proj),
        carry=carry,
    )


def _sb_bwd(proj, colq, colk, colv, do, nheads, carry=None):
    s = proj.shape[0]
    t = _tile(s, ATT_TILE)
    scale = HEAD_DIM ** -0.5
    hg = SB_BWD_GROUP
    gw = hg * HEAD_DIM
    assert nheads % hg == 0 and colq % hg == 0 and colk % hg == 0 and colv % hg == 0

    def body(q_ref, k_ref, v_ref, do_ref, dq_ref, dk_ref, dv_ref, g_s, beta_s):
        qi = pl.program_id(1)

        @pl.when(qi == 0)
        def _():
            dk_ref[...] = jnp.zeros_like(dk_ref)
            dv_ref[...] = jnp.zeros_like(dv_ref)

        later = _later(t)
        before = _tri_mask(t, True)

        def back(kj, carry, diagonal):
            off = pl.multiple_of(kj * t, t)
            heads = range(hg)
            rows = pl.ds(off, t)
            z = [_dot_nt(_head(q_ref, hh), _head(k_ref, hh, rows)) * scale for hh in heads]
            da = [_dot_nt(_head(do_ref, hh), _head(v_ref, hh, rows)) for hh in heads]
            lb = [_log_sigmoid(z[hh]) for hh in heads]
            lm = [lb[hh] - z[hh] for hh in heads]
            if diagonal:
                lm = [jnp.where(before, lm[hh], 0.0) for hh in heads]
            parts = [jnp.concatenate(_split2(lm[hh]), axis=1) for hh in heads]
            suffix = [_dot(parts[hh], later) for hh in heads]
            a = [jnp.exp(lb[hh] + suffix[hh] + carry[hh]) for hh in heads]
            if diagonal:
                a = [jnp.where(before, a[hh], 0.0) for hh in heads]
            dv = [_dot_tn(a[hh].astype(BF16), _head(do_ref, hh)) for hh in heads]
            for hh in heads:
                g_s[hh, :, rows] = a[hh] * da[hh]
                beta_s[hh, :, rows] = jnp.exp(lb[hh]).astype(BF16)
            for hh in heads:
                dv_ref[rows, hh * HEAD_DIM:(hh + 1) * HEAD_DIM] += dv[hh]
            return tuple(carry[hh] + jnp.sum(lm[hh], axis=-1, keepdims=True) for hh in heads)

        rc = back(qi, tuple(jnp.zeros((t, 1), F32) for _ in range(hg)), True)
        lax.fori_loop(1, qi + 1, lambda i, c: back(qi - i, c, False), rc)

        earlier = jnp.where(lax.broadcasted_iota(jnp.int32, (2 * t, t), 0) % t
                            < lax.broadcasted_iota(jnp.int32, (2 * t, t), 1), 1.0, 0.0).astype(BF16)

        def fwd(kj, carry, diagonal):
            off = pl.multiple_of(kj * t, t)
            heads = range(hg)
            rows = pl.ds(off, t)
            g = [g_s[hh, :, rows] for hh in heads]
            parts = [jnp.concatenate(_split2(g[hh]), axis=1) for hh in heads]
            gsum = [_dot(parts[hh], earlier) + carry[hh][0] for hh in heads]
            dz = []
            for hh in heads:
                beta = beta_s[hh, :, rows].astype(F32)
                d = g[hh] * (1.0 - beta) - gsum[hh] * beta
                if diagonal:
                    d = jnp.where(before, d, 0.0)
                dz.append(d.astype(BF16))
            dk = [_dot_tn(dz[hh], _head(q_ref, hh)) * scale for hh in heads]
            dq = [_dot(dz[hh], _head(k_ref, hh, rows)) * scale for hh in heads]
            for hh in heads:
                dk_ref[rows, hh * HEAD_DIM:(hh + 1) * HEAD_DIM] += dk[hh]
            return tuple((carry[hh][0] + jnp.sum(g[hh], axis=-1, keepdims=True), carry[hh][1] + dq[hh])
                         for hh in heads)

        init = tuple((jnp.zeros((t, 1), F32), jnp.zeros((t, HEAD_DIM), F32)) for _ in range(hg))
        carry = lax.fori_loop(0, qi, lambda kj, c: fwd(kj, c, False), init)
        carry = fwd(qi, carry, True)
        for hh in range(hg):
            dq_ref[:, hh * HEAD_DIM:(hh + 1) * HEAD_DIM] = carry[hh][1]

    once = pl.Buffered(buffer_count=1)
    tile_spec = pl.BlockSpec((t, gw), lambda h, i: (i, h))
    full = pl.BlockSpec((s, gw), lambda h, i: (0, h), pipeline_mode=once)
    w = nheads * HEAD_DIM
    return _call(
        body,
        grid=(nheads // hg, s // t),
        in_specs=[pl.BlockSpec((t, gw), lambda h, i: (i, colq // hg + h)),
                  pl.BlockSpec((s, gw), lambda h, i: (0, colk // hg + h), pipeline_mode=once),
                  pl.BlockSpec((s, gw), lambda h, i: (0, colv // hg + h), pipeline_mode=once),
                  tile_spec],
        out_specs=[tile_spec, full, full],
        out_shape=[jax.ShapeDtypeStruct((s, w), F32)] * 3,
        scratch_shapes=[pltpu.VMEM((hg, t, s), F32), pltpu.VMEM((hg, t, s), BF16)],
        semantics=("arbitrary", "arbitrary"),
        name="sb_bwd",
        args=(proj, proj, proj, do),
        carry=carry,
    )


def _mem_fwd(qn, kn, mkv, nheads):
    s = qn.shape[0]
    mtok = kn.shape[0]
    t = _tile(s, MEM_ROWS)
    w = nheads * HEAD_DIM
    scale = HEAD_DIM ** -0.5

    def body(q_ref, k_ref, v_ref, o_ref):
        heads = range(nheads)
        sc = [_dot_nt(_head(q_ref, hh), _head(k_ref, hh)) * scale for hh in heads]
        p = [jnp.exp(sc[hh] - jnp.max(sc[hh], axis=-1, keepdims=True)) for hh in heads]
        p = [p[hh] / jnp.sum(p[hh], axis=-1, keepdims=True) for hh in heads]
        o = [_dot(p[hh].astype(BF16), _head(v_ref, hh)) for hh in heads]
        for hh in heads:
            o_ref[:, hh * HEAD_DIM:(hh + 1) * HEAD_DIM] = o[hh].astype(BF16)

    return pl.pallas_call(
        body,
        grid=(s // t,),
        in_specs=[pl.BlockSpec((t, w), lambda i: (i, 0)),
                  pl.BlockSpec((mtok, w), lambda i: (0, 0)),
                  pl.BlockSpec((mtok, w), lambda i: (0, 1))],
        out_specs=pl.BlockSpec((t, w), lambda i: (i, 0)),
        out_shape=jax.ShapeDtypeStruct((s, w), BF16),
        compiler_params=_params("parallel"),
        name="mem_fwd",
    )(qn, kn, mkv)


def _mem_bwd(qn, kn, mkv, do, nheads):
    s = qn.shape[0]
    mtok = kn.shape[0]
    t = _tile(s, MEM_ROWS)
    w = nheads * HEAD_DIM
    scale = HEAD_DIM ** -0.5

    def body(q_ref, k_ref, v_ref, do_ref, dq_ref, dk_ref, dv_ref):
        @pl.when(pl.program_id(0) == 0)
        def _():
            dk_ref[...] = jnp.zeros_like(dk_ref)
            dv_ref[...] = jnp.zeros_like(dv_ref)

        heads = range(nheads)
        sc = [_dot_nt(_head(q_ref, hh), _head(k_ref, hh)) * scale for hh in heads]
        dp = [_dot_nt(_head(do_ref, hh), _head(v_ref, hh)) for hh in heads]
        p = [jnp.exp(sc[hh] - jnp.max(sc[hh], axis=-1, keepdims=True)) for hh in heads]
        p = [p[hh] / jnp.sum(p[hh], axis=-1, keepdims=True) for hh in heads]
        ds = [(p[hh] * (dp[hh] - jnp.sum(p[hh] * dp[hh], axis=-1, keepdims=True))).astype(BF16) for hh in heads]
        dq = [_dot(ds[hh], _head(k_ref, hh)) * scale for hh in heads]
        dk = [_dot_tn(ds[hh], _head(q_ref, hh)) * scale for hh in heads]
        dv = [_dot_tn(p[hh].astype(BF16), _head(do_ref, hh)) for hh in heads]
        for hh in heads:
            cols = slice(hh * HEAD_DIM, (hh + 1) * HEAD_DIM)
            dq_ref[:, cols] = dq[hh]
            dk_ref[:, cols] += dk[hh]
            dv_ref[:, cols] += dv[hh]

    tile = pl.BlockSpec((t, w), lambda i: (i, 0))
    kspec = pl.BlockSpec((mtok, w), lambda i: (0, 0))
    return pl.pallas_call(
        body,
        grid=(s // t,),
        in_specs=[tile, kspec, pl.BlockSpec((mtok, w), lambda i: (0, 1)), tile],
        out_specs=[tile, kspec, kspec],
        out_shape=[jax.ShapeDtypeStruct((s, w), F32), jax.ShapeDtypeStruct((mtok, w), F32),
                   jax.ShapeDtypeStruct((mtok, w), F32)],
        compiler_params=_params("arbitrary"),
        name="mem_bwd",
    )(qn, kn, mkv, do)


def _merge_fwd(p0, p1, p2, gates, b_gate):
    s, d = p0.shape
    tm, tn = _tile(s, ROW_TILE), _tile(d, COL_TILE)
    nj = d // tn

    def body(p0_ref, p1_ref, p2_ref, ga_ref, gb_ref, gc_ref, b_ref, o_ref, sa_ref, sb_ref, sc_ref):
        acc = jnp.zeros((tm, tn), F32)
        for b, (p_ref, g_ref, s_ref) in enumerate(((p0_ref, ga_ref, sa_ref), (p1_ref, gb_ref, sb_ref),
                                                   (p2_ref, gc_ref, sc_ref))):
            gate = jax.nn.sigmoid(g_ref[...].astype(F32) + b_ref[b:b + 1, :])
            s_ref[...] = gate.astype(BF16)
            acc = acc + gate * p_ref[...]
        o_ref[...] = acc.astype(BF16)

    blk = pl.BlockSpec((tm, tn), lambda i, j: (i, j))
    merged, *sig = pl.pallas_call(
        body,
        grid=(s // tm, nj),
        in_specs=[blk] * 6 + [pl.BlockSpec((3, tn), lambda i, j: (0, j))],
        out_specs=[blk] * 4,
        out_shape=[jax.ShapeDtypeStruct((s, d), BF16)] * 4,
        compiler_params=_params("parallel", "parallel"),
        name="merge_fwd",
    )(p0, p1, p2, *gates, b_gate)
    return merged, tuple(sig)


def _merge_bwd(dmerged, p0, p1, p2, sig):
    s, d = p0.shape
    tm, tn = _tile(s, ROW_TILE), _tile(d, COL_TILE)
    nj = d // tn

    def body(dm_ref, p0_ref, p1_ref, p2_ref, ga_ref, gb_ref, gc_ref,
             d0_ref, d1_ref, d2_ref, dga_ref, dgb_ref, dgc_ref, db_ref):
        dm = dm_ref[...].astype(F32)
        parts = []
        for p_ref, g_ref, dp_ref, dg_ref in ((p0_ref, ga_ref, d0_ref, dga_ref), (p1_ref, gb_ref, d1_ref, dgb_ref),
                                             (p2_ref, gc_ref, d2_ref, dgc_ref)):
            gate = g_ref[...].astype(F32)
            dp_ref[...] = (dm * gate).astype(BF16)
            dgate = dm * p_ref[...] * gate * (1.0 - gate)
            dg_ref[...] = dgate.astype(BF16)
            parts.append(jnp.sum(dgate, axis=0, keepdims=True))
        part = jnp.concatenate(parts, axis=0)

        @pl.when(pl.program_id(1) == 0)
        def _():
            db_ref[...] = part

        @pl.when(pl.program_id(1) > 0)
        def _():
            db_ref[...] += part

    blk = pl.BlockSpec((tm, tn), lambda j, i: (i, j))
    bias = pl.BlockSpec((3, tn), lambda j, i: (0, j))
    return pl.pallas_call(
        body,
        grid=(nj, s // tm),
        in_specs=[blk] * 7,
        out_specs=[blk] * 6 + [bias],
        out_shape=[jax.ShapeDtypeStruct((s, d), BF16)] * 6 + [jax.ShapeDtypeStruct((3, d), F32)],
        compiler_params=_params("parallel", "arbitrary"),
        name="merge_bwd",
    )(dmerged, p0, p1, p2, *sig)


def _shift_down(v, n):
    rows = lax.broadcasted_iota(jnp.int32, v.shape, 0)
    return jnp.where(rows >= n, pltpu.roll(v, n, 0), 0.0)


def _shift_up(v, n):
    s = v.shape[0]
    rows = lax.broadcasted_iota(jnp.int32, v.shape, 0)
    return jnp.where(rows < s - n, pltpu.roll(v, s - n, 0), 0.0)


def _conv(v, w_ref, b_ref):
    taps = w_ref.shape[0]
    out = v * w_ref[taps - 1:taps, :] + b_ref[...]
    for n in range(1, taps):
        out = out + _shift_down(v, n) * w_ref[taps - 1 - n:taps - n, :]
    return out


def _conv_act_fwd(up, conv_w, conv_b):
    s, f2 = up.shape
    f = f2 // 2
    tn = LANES
    nj = f // tn
    taps = conv_w.shape[0]

    def body(ug_ref, uv_ref, wg_ref, wv_ref, bg_ref, bv_ref, o_ref, cg_ref, cv_ref):
        cg = _conv(ug_ref[...].astype(F32), wg_ref, bg_ref)
        cv = _conv(uv_ref[...].astype(F32), wv_ref, bv_ref)
        o_ref[...] = (cg * jax.nn.sigmoid(cg) * cv).astype(BF16)
        cg_ref[...] = cg.astype(BF16)
        cv_ref[...] = cv.astype(BF16)

    out = pl.BlockSpec((s, tn), lambda j: (0, j))
    return pl.pallas_call(
        body,
        grid=(nj,),
        in_specs=[pl.BlockSpec((s, tn), lambda j: (0, j)), pl.BlockSpec((s, tn), lambda j: (0, nj + j)),
                  pl.BlockSpec((taps, tn), lambda j: (0, j)), pl.BlockSpec((taps, tn), lambda j: (0, nj + j)),
                  pl.BlockSpec((1, tn), lambda j: (0, j)), pl.BlockSpec((1, tn), lambda j: (0, nj + j))],
        out_specs=[out, out, out],
        out_shape=[jax.ShapeDtypeStruct((s, f), BF16)] * 3,
        compiler_params=_params("parallel"),
        name="conv_act_fwd",
    )(up, up, conv_w, conv_w, conv_b, conv_b)


def _conv_act_bwd(up, conv_w, conv_g, conv_v, dact):
    s, f2 = up.shape
    f = f2 // 2
    tn = LANES
    nj = f // tn
    taps = conv_w.shape[0]

    def half(v, du, w_ref, dup_ref, dw_ref, db_ref):
        dup = du * w_ref[taps - 1:taps, :]
        rows = [None] * taps
        rows[taps - 1] = jnp.sum(du * v, axis=0, keepdims=True)
        for n in range(1, taps):
            later = _shift_up(du, n)
            dup = dup + later * w_ref[taps - 1 - n:taps - n, :]
            rows[taps - 1 - n] = jnp.sum(later * v, axis=0, keepdims=True)
        dup_ref[...] = dup.astype(BF16)
        dw_ref[...] = jnp.concatenate(rows, axis=0)
        db_ref[...] = jnp.sum(du, axis=0, keepdims=True)

    def body(ug_ref, uv_ref, wg_ref, wv_ref, cg_ref, cv_ref, da_ref,
             dug_ref, duv_ref, dwg_ref, dwv_ref, dbg_ref, dbv_ref):
        cg = cg_ref[...].astype(F32)
        cv = cv_ref[...].astype(F32)
        da = da_ref[...].astype(F32)
        sg = jax.nn.sigmoid(cg)
        dcv = da * cg * sg
        dcg = da * cv * (sg + cg * sg * (1.0 - sg))
        half(ug_ref[...].astype(F32), dcg, wg_ref, dug_ref, dwg_ref, dbg_ref)
        half(uv_ref[...].astype(F32), dcv, wv_ref, duv_ref, dwv_ref, dbv_ref)

    lo = lambda rows: pl.BlockSpec((rows, tn), lambda j: (0, j))
    hi = lambda rows: pl.BlockSpec((rows, tn), lambda j: (0, nj + j))
    return pl.pallas_call(
        body,
        grid=(nj,),
        in_specs=[lo(s), hi(s), lo(taps), hi(taps), lo(s), lo(s), lo(s)],
        out_specs=[lo(s), lo(s), lo(taps), lo(taps), lo(1), lo(1)],
        out_shape=[jax.ShapeDtypeStruct((s, f), BF16)] * 2 + [jax.ShapeDtypeStruct((taps, f), F32)] * 2
        + [jax.ShapeDtypeStruct((1, f), F32)] * 2,
        compiler_params=_params("parallel"),
        name="conv_act_bwd",
    )(up, up, conv_w, conv_w, conv_g, conv_v, dact)


def _row_tile(rows, row_bytes, budget):
    if rows * row_bytes <= budget or rows % 8:
        return rows
    best = 8
    for t in range(8, rows, 8):
        if rows % t == 0 and t * row_bytes <= budget:
            best = t
    return best


def _adamw(w, g, m, v, name):
    r, c = w.shape
    tr = _row_tile(r, c * 4, ADAM_BLOCK_BYTES)

    def body(w_ref, g_ref, m_ref, v_ref, d_ref, mo_ref, vo_ref):
        gg = g_ref[...]
        m_new = ADAM_B1 * m_ref[...] + (1.0 - ADAM_B1) * gg
        v_new = ADAM_B2 * v_ref[...] + (1.0 - ADAM_B2) * (gg * gg)
        m_hat = m_new / (1.0 - ADAM_B1 ** ADAM_STEP)
        v_hat = v_new / (1.0 - ADAM_B2 ** ADAM_STEP)
        d_ref[...] = -ADAM_LR * (m_hat / (jnp.sqrt(v_hat) + ADAM_EPS) + ADAM_WD * w_ref[...])
        mo_ref[...] = m_new
        vo_ref[...] = v_new

    blk = pl.BlockSpec((tr, c), lambda i: (i, 0))
    return pl.pallas_call(
        body,
        grid=(r // tr,),
        in_specs=[blk] * 4,
        out_specs=[blk] * 3,
        out_shape=[jax.ShapeDtypeStruct((r, c), F32)] * 3,
        compiler_params=_params("parallel"),
        name=name,
    )(w, g, m, v)


def _add_sibling(g, r1, core, name):
    _, _, h, c = g.shape
    th = _row_tile(h, c * 2, ADAM_BLOCK_BYTES)

    def body(core_ref, g_ref, r_ref, o_ref):
        o_ref[...] = (g_ref[...].astype(F32) + r_ref[...].astype(F32)).astype(BF16)

    return pl.pallas_call(
        body,
        grid_spec=pltpu.PrefetchScalarGridSpec(
            num_scalar_prefetch=1,
            grid=(N_CHIPS, h // th),
            in_specs=[pl.BlockSpec((None, None, th, c), lambda j, i, core_ref: (j, core_ref[0], i, 0)),
                      pl.BlockSpec((None, th, c), lambda j, i, core_ref: (j, i, 0))],
            out_specs=pl.BlockSpec((None, th, c), lambda j, i, core_ref: (j, i, 0)),
        ),
        out_shape=jax.ShapeDtypeStruct((N_CHIPS, h, c), BF16),
        compiler_params=_params("parallel", "parallel"),
        name=name,
    )(core, g, r1)


def _add_chips(hsum, r2, chip_core, name):
    _, h, c = hsum.shape
    th = _row_tile(h, c * 4, ADAM_BLOCK_BYTES)

    def body(sel_ref, own_ref, r_ref, o_ref):
        acc = own_ref[...].astype(F32)
        for j in range(N_CHIPS - 1):
            acc = acc + r_ref[j].astype(F32)
        o_ref[...] = acc

    return pl.pallas_call(
        body,
        grid_spec=pltpu.PrefetchScalarGridSpec(
            num_scalar_prefetch=1,
            grid=(h // th,),
            in_specs=[pl.BlockSpec((None, th, c), lambda i, sel_ref: (sel_ref[0], i, 0)),
                      pl.BlockSpec((N_CHIPS - 1, th, c), lambda i, sel_ref: (0, i, 0))],
            out_specs=pl.BlockSpec((None, th, c), lambda i, sel_ref: (sel_ref[1], i, 0)),
        ),
        out_shape=jax.ShapeDtypeStruct((2, h, c), F32),
        compiler_params=_params("parallel"),
        name=name,
    )(chip_core, hsum, r2)


def _sum_devices(parts):
    _, r, c = parts.shape

    def body(p_ref, o_ref):
        acc = p_ref[0]
        for j in range(1, N_DEV):
            acc = acc + p_ref[j]
        o_ref[...] = acc

    return pl.pallas_call(
        body,
        out_shape=jax.ShapeDtypeStruct((r, c), F32),
        compiler_params=pltpu.CompilerParams(vmem_limit_bytes=VMEM_LIMIT_BYTES),
        name="sum_devices",
    )(parts)


def _gather_small(vec):
    k = N_DEV - 1

    def body(v_ref, o_ref, send_sems, recv_sems, local_sem):
        x, y, c, _ = _place()
        me = 4 * x + 2 * y + c
        local = pltpu.make_async_copy(v_ref, o_ref.at[me], local_sem)
        local.start()
        peers = [(x ^ (r >> 2 & 1), y ^ (r >> 1 & 1), c ^ (r & 1)) for r in range(1, N_DEV)]
        sends = [_remote(v_ref, o_ref.at[me], send_sems.at[j], recv_sems.at[j], p) for j, p in enumerate(peers)]
        for cp in sends:
            cp.start()
        for j, (px, py, pc) in enumerate(peers):
            sends[j].wait_send()
            blk = o_ref.at[4 * px + 2 * py + pc]
            _remote(blk, blk, send_sems.at[j], recv_sems.at[j], (px, py, pc)).wait_recv()
        local.wait()

    return pl.pallas_call(
        body,
        in_specs=[ANY],
        out_specs=ANY,
        out_shape=jax.ShapeDtypeStruct((N_DEV,) + vec.shape, vec.dtype),
        scratch_shapes=[pltpu.SemaphoreType.DMA((k,)), pltpu.SemaphoreType.DMA((k,)), pltpu.SemaphoreType.DMA(())],
        name="gather_small",
    )(vec)


W_IN_GATES = ("w_in_g0", "w_in_g1", "w_in_g2")
ROW_SHARDED = ("w_in_a", "w_in_f") + W_IN_GATES + ("w_mem_kv", "w_out", "w_down")
COL_SHARDED = ("w_br_fox", "w_br_sb", "w_br_mem", "w_up")
BIG = ROW_SHARDED + COL_SHARDED


def _whole(name, a):
    if name in ROW_SHARDED:
        return a.reshape(N_CHIPS * a.shape[1], a.shape[2])
    return a.transpose(1, 0, 2).reshape(a.shape[1], N_CHIPS * a.shape[2])


def _by_shard(name, grad):
    if name in ROW_SHARDED:
        a = grad.reshape(N_CHIPS, grad.shape[0] // N_CHIPS, grad.shape[1])
    else:
        a = grad.reshape(grad.shape[0], N_CHIPS, grad.shape[1] // N_CHIPS).transpose(1, 0, 2)
    return a.reshape(N_CHIPS, 2, a.shape[1] // 2, a.shape[2])


def _sibling_sums(names, split, theirs, core):
    return [_add_sibling(a, r, core, "add_sibling_" + name) for name, a, r in zip(names, split, theirs)]


GATHER_FIRST = ("w_in_a", "w_in_f")
GATHER_EARLY = W_IN_GATES[:2]
GATHER_MIX = ("w_out", "w_br_fox", "w_br_sb", "w_br_mem") + W_IN_GATES[2:]
REDUCE_FFN = ("w_down", "w_up")
REDUCE_MIX = ("w_out", "w_br_fox", "w_br_sb", "w_br_mem", "w_mem_kv") + W_IN_GATES
REDUCE_IN = ("w_in_a", "w_in_f")


def _local_step(x, mem, target, w, shard, core, chip_core):
    d = x.shape[1]
    nf = shard["w_br_fox"].shape[0] // HEAD_DIM
    nsb = shard["w_br_sb"].shape[0] // HEAD_DIM
    nm = shard["w_br_mem"].shape[0] // HEAD_DIM
    w = dict(w)

    def take(names, gathered):
        for name, a in zip(names, gathered):
            w[name] = _whole(name, a)

    fq, fk, fv = 0, nf, 2 * nf
    sq, sk, sv = 3 * nf, 3 * nf + nsb, 3 * nf + 2 * nsb
    mq = 3 * nf + 3 * nsb

    h, rstd1, moved = _rms_fwd(x, w["g_mix"], "rms_mix_fwd", carry=_Gather([shard[name] for name in GATHER_FIRST]))
    take(GATHER_FIRST, moved)
    proj, moved = _mm(h, w["w_in_a"], "nn", BF16, "proj_att", carry=_Gather([shard[name] for name in GATHER_EARLY]))
    take(GATHER_EARLY, moved)
    gate0, moved = _mm(h, w["w_in_g0"], "nn", BF16, "proj_gate0", carry=_Gather([shard["w_mem_kv"]]))
    take(("w_mem_kv",), moved)
    gate1 = _mm(h, w["w_in_g1"], "nn", BF16, "proj_gate1")
    f_logit, c_sum = _forget_fwd(h, w["w_in_f"], w["b_forget"])
    c_t = c_sum[:, :nf].T
    c_col, c_row = c_t[:, :, None], c_t[:, None, :]
    qn = _headnorm_fwd(proj, fq, nf, w["g_q_fox"], "fox_qnorm_fwd")
    kn = _headnorm_fwd(proj, fk, nf, w["g_k_fox"], "fox_knorm_fwd")
    (o_fox, o_fox32, lse), moved = _fox_fwd(qn, kn, proj, fv, c_col, c_row, nf,
                                            carry=_Gather([shard[name] for name in GATHER_MIX]))
    take(GATHER_MIX, moved)
    gates = (gate0, gate1, _mm(h, w["w_in_g2"], "nn", BF16, "proj_gate2"))
    (o_sb,), moved = _sb_fwd(proj, sq, sk, sv, nsb, carry=_Gather([shard["w_up"]]))
    take(("w_up",), moved)
    memn, rstd_m = _rms_fwd(mem, w["g_mem"], "rms_mem_fwd")
    mkv = _mm(memn, w["w_mem_kv"], "nn", BF16, "mem_kv")
    kmn = _headnorm_fwd(mkv, 0, nm, w["g_k_mem"], "mem_knorm_fwd")
    qmn = _headnorm_fwd(proj, mq, nm, w["g_q_mem"], "mem_qnorm_fwd")
    o_mem = _mem_fwd(qmn, kmn, mkv, nm)
    p0 = _mm(o_fox, w["w_br_fox"], "nn", F32, "branch_fox")
    p1 = _mm(o_sb, w["w_br_sb"], "nn", F32, "branch_sb")
    p2 = _mm(o_mem, w["w_br_mem"], "nn", F32, "branch_mem")
    merged, sig = _merge_fwd(p0, p1, p2, gates, w["b_gate"])
    x1 = _mm(merged, w["w_out"], "nn", F32, "out_proj", residual=x)
    h2, rstd2 = _rms_fwd(x1, w["g_ffn"], "rms_ffn_fwd")
    up, moved = _mm(h2, w["w_up"], "nn", BF16, "ffn_up", carry=_Gather([shard["w_down"]]))
    take(("w_down",), moved)
    act, conv_g, conv_v = _conv_act_fwd(up, w["conv_w"], w["conv_b"])
    dy, dyb, lparts = _mm(act, w["w_down"], "nn", F32, "ffn_down_loss", residual=x1, loss_target=target)
    loss = (0.5 / d) * jnp.sum(lparts[::8, ::LANES])

    g = {}
    dact = _mm(dyb, w["w_down"], "nt", BF16, "ffn_down_dx")
    g["w_down"] = _mm(act, dyb, "tn", BF16, "ffn_down_dw")
    dug, duv, dwg, dwv, dbg, dbv = _conv_act_bwd(up, w["conv_w"], conv_g, conv_v, dact)
    dup = jnp.concatenate([dug, duv], axis=1)
    g["conv_w"] = jnp.concatenate([dwg, dwv], axis=1)
    g["conv_b"] = jnp.concatenate([dbg, dbv], axis=1)
    split_down = [_by_shard("w_down", g["w_down"])]
    dh2, theirs_down = _mm(dup, w["w_up"], "nt", BF16, "ffn_up_dx", carry=_Swap(split_down))
    g["w_up"] = _mm(h2, dup, "tn", BF16, "ffn_up_dw")
    split_up = [_by_shard("w_up", g["w_up"])]
    dx1, dx1b, g["g_ffn"] = _rms_bwd(dh2, x1, rstd2, w["g_ffn"], dy, "rms_ffn_bwd")
    dmerged, theirs_up = _mm(dx1b, w["w_out"], "nt", BF16, "out_proj_dx", carry=_Swap(split_up))
    sums_ffn = _sibling_sums(REDUCE_FFN, split_down + split_up, theirs_down + theirs_up, core)
    g["w_out"] = _mm(merged, dx1b, "tn", BF16, "out_proj_dw")
    dp0, dp1, dp2, dga, dgb, dgc, g["b_gate"] = _merge_bwd(dmerged, p0, p1, p2, sig)
    dgates = (dga, dgb, dgc)
    for name, dgate in zip(W_IN_GATES, dgates):
        g[name] = _mm(h, dgate, "tn", BF16, name + "_dw")
    do_fox = _mm(dp0, w["w_br_fox"], "nt", BF16, "branch_fox_dx")
    do_sb = _mm(dp1, w["w_br_sb"], "nt", BF16, "branch_sb_dx")
    do_mem = _mm(dp2, w["w_br_mem"], "nt", BF16, "branch_mem_dx")
    g["w_br_fox"] = _mm(o_fox, dp0, "tn", BF16, "branch_fox_dw")
    g["w_br_sb"] = _mm(o_sb, dp1, "tn", BF16, "branch_sb_dw")
    g["w_br_mem"] = _mm(o_mem, dp2, "tn", BF16, "branch_mem_dw")
    dqmn, dkmn, dvm = _mem_bwd(qmn, kmn, mkv, do_mem, nm)
    dmq, g["g_q_mem"] = _headnorm_bwd(dqmn, proj, mq, nm, w["g_q_mem"], "mem_qnorm_bwd")
    dkm, g["g_k_mem"] = _headnorm_bwd(dkmn, mkv, 0, nm, w["g_k_mem"], "mem_knorm_bwd")
    dmkv = jnp.concatenate([dkm, dvm.astype(BF16)], axis=1)
    g["w_mem_kv"] = _mm(memn, dmkv, "tn", BF16, "mem_kv_dw")
    dmemn = _mm(dmkv, w["w_mem_kv"], "nt", BF16, "mem_kv_dx")
    _, _, g["g_mem"] = _rms_bwd(dmemn, mem, rstd_m, w["g_mem"], None, "rms_mem_bwd")
    split_mix =[_by_shard(name, g[name]) for name in REDUCE_MIX]

    (dqn, dkn, dfv, drs, dcs), moved = _fox_bwd(qn, kn, proj, fv, c_col, c_row, o_fox32, do_fox, lse, nf,
                                                carry=_Both(_Scatter(sums_ffn[1:]), _Swap(split_mix)))
    others_up, theirs_mix = moved[:1], moved[1:]
    sums_mix = _sibling_sums(REDUCE_MIX, split_mix, theirs_mix, core)
    dfq, g["g_q_fox"] = _headnorm_bwd(dqn, proj, fq, nf, w["g_q_fox"], "fox_qnorm_bwd")
    dfk, g["g_k_fox"] = _headnorm_bwd(dkn, proj, fk, nf, w["g_k_fox"], "fox_knorm_bwd")
    dc = jnp.pad((drs[:, :, 0] - dcs[:, 0, :]).T, ((0, 0), (0, LANES - nf)))
    df, g["b_forget"] = _forget_bwd(dc, f_logit, w["b_forget"])
    (dsq, dsk, dsv), others_mix = _sb_bwd(proj, sq, sk, sv, do_sb, nsb, carry=_Scatter(sums_ffn[:1] + sums_mix))
    others_ffn = others_mix[:1] + others_up
    others_mix = others_mix[1:]

    dproj = jnp.concatenate([dfq, dfk, dfv.astype(BF16), dsq.astype(BF16), dsk.astype(BF16), dsv.astype(BF16), dmq],
                            axis=1)
    dfb = df.astype(BF16)
    g["w_in_a"] = _mm(h, dproj, "tn", BF16, "proj_att_dw")
    g["w_in_f"] = _mm(h, dfb, "tn", BF16, "proj_forget_dw")
    split_in = [_by_shard(name, g[name]) for name in REDUCE_IN]
    dh, theirs_in = _mm(dgates[0], w[W_IN_GATES[0]], "nt", F32, W_IN_GATES[0] + "_dx", carry=_Swap(split_in))
    sums_in = _sibling_sums(REDUCE_IN, split_in, theirs_in, core)
    for name, dgate in zip(W_IN_GATES[1:], dgates[1:]):
        dh = _mm(dgate, w[name], "nt", F32, name + "_dx", residual=dh)
    dh, others_in = _mm(dproj, w["w_in_a"], "nt", F32, "proj_att_dx", residual=dh, carry=_Scatter(sums_in))
    names = REDUCE_FFN + REDUCE_MIX + REDUCE_IN
    finals = [_add_chips(own, theirs, chip_core, "add_chips_" + name)
              for name, own, theirs in zip(names, sums_ffn + sums_mix + sums_in, others_ffn + others_mix + others_in)]
    grad_x, _, g["g_mix"], joined = _rms_bwd(dh, x, rstd1, w["g_mix"], dx1, "rms_mix_bwd", more=(dfb, w["w_in_f"]),
                                             carry=_Join(finals))
    summed = {name: a.reshape(2 * a.shape[1], a.shape[2]) for name, a in zip(names, joined)}
    return loss, grad_x, g, summed


SMALL = ("g_mix", "b_forget", "g_q_fox", "g_k_fox", "g_mem", "g_q_mem", "g_k_mem", "b_gate", "g_ffn", "conv_w",
         "conv_b")
SMALL_SHARDED = ("b_gate", "conv_w")
PACK_ROWS = 8


def _pack(arrs):
    flat = jnp.concatenate([a.reshape(-1) for a in arrs])
    unit = PACK_ROWS * LANES
    flat = jnp.pad(flat, (0, -flat.shape[0] % unit))
    return flat.reshape(-1, LANES)


def _unpack(packed, shapes):
    flat = packed.reshape(-1)
    out, at = [], 0
    for s in shapes:
        n = 1
        for dim in s:
            n *= dim
        out.append(flat[at:at + n].reshape(s))
        at += n
    return out


def kernel(x, mem, g_mix, w_in, b_forget, g_q_fox, g_k_fox, g_mem, w_mem_kv, g_q_mem, g_k_mem, w_br_fox, w_br_sb, w_br_mem, b_gate, w_out, g_ffn, w_up, conv_w, conv_b, w_down, loss_target, m_g_mix, m_w_in, m_b_forget, m_g_q_fox, m_g_k_fox, m_g_mem, m_w_mem_kv, m_g_q_mem, m_g_k_mem, m_w_br_fox, m_w_br_sb, m_w_br_mem, m_b_gate, m_w_out, m_g_ffn, m_w_up, m_conv_w, m_conv_b, m_w_down, v_g_mix, v_w_in, v_b_forget, v_g_q_fox, v_g_k_fox, v_g_mem, v_w_mem_kv, v_g_q_mem, v_g_k_mem, v_w_br_fox, v_w_br_sb, v_w_br_mem, v_b_gate, v_w_out, v_g_ffn, v_w_up, v_conv_w, v_conv_b, v_w_down):
    given = dict(g_mix=g_mix, w_in=w_in, b_forget=b_forget, g_q_fox=g_q_fox, g_k_fox=g_k_fox, g_mem=g_mem,
                 w_mem_kv=w_mem_kv, g_q_mem=g_q_mem, g_k_mem=g_k_mem, w_br_fox=w_br_fox, w_br_sb=w_br_sb,
                 w_br_mem=w_br_mem, b_gate=b_gate, w_out=w_out, g_ffn=g_ffn, w_up=w_up, conv_w=conv_w, conv_b=conv_b,
                 w_down=w_down)
    m_in = dict(g_mix=m_g_mix, w_in=m_w_in, b_forget=m_b_forget, g_q_fox=m_g_q_fox, g_k_fox=m_g_k_fox, g_mem=m_g_mem,
                w_mem_kv=m_w_mem_kv, g_q_mem=m_g_q_mem, g_k_mem=m_g_k_mem, w_br_fox=m_w_br_fox, w_br_sb=m_w_br_sb,
                w_br_mem=m_w_br_mem, b_gate=m_b_gate, w_out=m_w_out, g_ffn=m_g_ffn, w_up=m_w_up, conv_w=m_conv_w,
                conv_b=m_conv_b, w_down=m_w_down)
    v_in = dict(g_mix=v_g_mix, w_in=v_w_in, b_forget=v_b_forget, g_q_fox=v_g_q_fox, g_k_fox=v_g_k_fox, g_mem=v_g_mem,
                w_mem_kv=v_w_mem_kv, g_q_mem=v_g_q_mem, g_k_mem=v_g_k_mem, w_br_fox=v_w_br_fox, w_br_sb=v_w_br_sb,
                w_br_mem=v_w_br_mem, b_gate=v_b_gate, w_out=v_w_out, g_ffn=v_g_ffn, w_up=v_w_up, conv_w=v_conv_w,
                conv_b=v_conv_b, w_down=v_w_down)
    layered = {k: a.ndim == 3 for k, a in given.items()}
    drop = lambda a: a[0] if a.ndim == 3 else a
    given = {k: drop(a) for k, a in given.items()}
    m_in = {k: drop(a) for k, a in m_in.items()}
    v_in = {k: drop(a) for k, a in v_in.items()}

    xi, yi, ci = lax.axis_index("x"), lax.axis_index("y"), lax.axis_index("c")
    chip = (2 * xi + yi).astype(jnp.int32)
    core_arr = ci.astype(jnp.int32).reshape(1)
    chip_core = jnp.stack([chip, ci.astype(jnp.int32)])

    nf = given["b_forget"].shape[1]
    cut = 3 * given["w_br_fox"].shape[0]

    d_model = given["w_out"].shape[1]
    gate0 = given["w_in"].shape[1] - len(W_IN_GATES) * d_model
    shard = {
        "w_in_a": jnp.concatenate([given["w_in"][:, :cut], given["w_in"][:, cut + nf:gate0]], axis=1).astype(BF16),
        "w_in_f": jnp.pad(given["w_in"][:, cut:cut + nf], ((0, 0), (0, LANES - nf))).astype(BF16),
    }
    for b, name in enumerate(W_IN_GATES):
        shard[name] = given["w_in"][:, gate0 + b * d_model:gate0 + (b + 1) * d_model].astype(BF16)
    for name in BIG:
        if name not in shard:
            shard[name] = given[name].astype(BF16)
    w = {}
    small_shapes = [given[name].shape for name in SMALL_SHARDED]
    small_parts = _gather_small(_pack([given[name] for name in SMALL_SHARDED]))[0::2]
    per_chip = [_unpack(small_parts[j], small_shapes) for j in range(N_CHIPS)]
    for k, name in enumerate(SMALL_SHARDED):
        w[name] = jnp.concatenate([per_chip[j][k] for j in range(N_CHIPS)], axis=1)
    for name in SMALL:
        if name not in SMALL_SHARDED:
            w[name] = given[name]
    w["b_forget"] = jnp.pad(given["b_forget"], ((0, 0), (0, LANES - nf)))

    loss, grad_x, g, summed = _local_step(x[0], mem[0], loss_target[0], w, shard, core_arr, chip_core)
    loss = lax.psum(loss, ("x", "y", "c"))
    grads = {name: summed[name] for name in BIG if name in given}
    grads["w_in"] = jnp.concatenate([summed["w_in_a"][:, :cut], summed["w_in_f"][:, :nf], summed["w_in_a"][:, cut:]]
                                    + [summed[name] for name in W_IN_GATES], axis=1)

    g["b_forget"] = g["b_forget"][:, :nf]
    small_full_shapes = [g[name].shape for name in SMALL]
    small_sum = _unpack(_sum_devices(_gather_small(_pack([g[name] for name in SMALL]))), small_full_shapes)
    for name, a in zip(SMALL, small_sum):
        if name in SMALL_SHARDED:
            width = given[name].shape[1]
            a = lax.dynamic_slice_in_dim(a, chip * width, width, axis=1)
        grads[name] = a

    delta, new_m, new_v = {}, {}, {}
    for name in WEIGHTS:
        if name not in SMALL:
            delta[name], new_m[name], new_v[name] = _adamw(given[name], grads[name], m_in[name], v_in[name],
                                                           "adamw_" + name)
    shapes = [given[name].shape for name in SMALL]
    packed = [_pack([src[name] for name in SMALL]) for src in (given, grads, m_in, v_in)]
    for dst, res in zip((delta, new_m, new_v), _adamw(*packed, "adamw_small")):
        for name, a in zip(SMALL, _unpack(res, shapes)):
            dst[name] = a

    out = [loss, grad_x[None]]
    for src in (grads, delta, new_m, new_v):
        out.extend(src[name][None] if layered[name] else src[name] for name in WEIGHTS)
    return tuple(out)
```

```python
import functools

import jax
import jax.numpy as jnp
from jax import lax
from jax.experimental import pallas as pl
from jax.experimental.pallas import tpu as pltpu

F32 = jnp.float32
BF16 = jnp.bfloat16

HEAD_DIM = 128
EPS = 1e-6
NEG_BIG = -1e30

ADAM_LR = 0.001
ADAM_B1 = 0.9
ADAM_B2 = 0.999
ADAM_EPS = 1e-08
ADAM_WD = 0.01
ADAM_STEP = 10

LANES = 128
BF16_SUBLANES = 16
VMEM_LIMIT_BYTES = 56 * 1024 * 1024
MM_TILE = 1024
MM_TILE_K = {"nn": 2048, "nt": 2816, "tn": 4096}
ATT_TILE = 256
ROW_TILE = 256
HEADNORM_ROWS = 1024
MEM_ROWS = 1024
FORGET_ROWS = 512
COL_TILE = 1024
ADAM_BLOCK_BYTES = 2 << 20

N_CHIPS = 4
N_DEV = 8
MESH = pl.DeviceIdType.MESH

IN_NAMES = ['x', 'mem', 'g_mix', 'w_in', 'b_forget', 'g_q_fox', 'g_k_fox', 'g_mem', 'w_mem_kv', 'g_q_mem', 'g_k_mem',
            'w_br_fox', 'w_br_sb', 'w_br_mem', 'b_gate', 'w_out', 'g_ffn', 'w_up', 'conv_w', 'conv_b', 'w_down']
WEIGHTS = IN_NAMES[2:]


def _tile(n, target):
    if n <= target:
        return n
    for t in range(target - target % LANES, LANES - 1, -LANES):
        if n % t == 0:
            return t
    return n


def _params(*sem):
    return pltpu.CompilerParams(dimension_semantics=sem, vmem_limit_bytes=VMEM_LIMIT_BYTES)


def _log_sigmoid(z):
    return jnp.minimum(z, 0.0) - jnp.log(1.0 + jnp.exp(-jnp.abs(z)))


def _split2(v):
    hi = v.astype(BF16)
    lo = (v - hi.astype(F32)).astype(BF16)
    return hi, lo


def _split3(v):
    hi = v.astype(BF16)
    r = v - hi.astype(F32)
    mid = r.astype(BF16)
    lo = (r - mid.astype(F32)).astype(BF16)
    return hi, mid, lo


def _dot(a, b):
    return lax.dot_general(a, b, (((1,), (0,)), ((), ())), preferred_element_type=F32)


def _dot_nt(a, b):
    return lax.dot_general(a, b, (((1,), (1,)), ((), ())), preferred_element_type=F32)


def _dot_tn(a, b):
    return lax.dot_general(a, b, (((0,), (0,)), ((), ())), preferred_element_type=F32)


ANY = pl.BlockSpec(memory_space=pl.ANY)


def _place():
    x, y, c = lax.axis_index("x"), lax.axis_index("y"), lax.axis_index("c")
    others = [(1 - x, y), (x, 1 - y), (1 - x, 1 - y)]
    return x, y, c, others


def _remote(src, dst, send_sem, recv_sem, to):
    return pltpu.make_async_remote_copy(src_ref=src, dst_ref=dst, send_sem=send_sem, recv_sem=recv_sem,
                                        device_id=to, device_id_type=MESH)


class _Gather:
    PER_SHARD = 7

    def __init__(self, shards):
        self.inputs = list(shards)
        n = len(shards) * self.PER_SHARD
        self.out_shapes = [jax.ShapeDtypeStruct((N_CHIPS,) + s.shape, s.dtype) for s in shards]
        self.scratch = [pltpu.SemaphoreType.DMA((n,)), pltpu.SemaphoreType.DMA((n,))]

    def _first(self, ins, outs, sems):
        send_sems, recv_sems = sems
        x, y, c, others = _place()
        me = 2 * x + y
        k = self.PER_SHARD
        copies = []
        for i in range(len(ins)):
            h = ins[i].shape[0] // 2
            mine = pl.ds(pl.multiple_of(c * h, BF16_SUBLANES), h)
            for j, (ox, oy) in enumerate(others):
                copies.append(_remote(ins[i].at[mine], outs[i].at[me, mine], send_sems.at[k * i + j],
                                      recv_sems.at[k * i + j], (ox, oy, c)))
            copies.append(_remote(ins[i], outs[i].at[me], send_sems.at[k * i + 6], recv_sems.at[k * i + 6],
                                  (x, y, 1 - c)))
        return copies

    def start(self, ins, outs, sems):
        for cp in self._first(ins, outs, sems):
            cp.start()

    def finish(self, ins, outs, sems):
        send_sems, recv_sems = sems
        x, y, c, others = _place()
        me = 2 * x + y
        sibling = (x, y, 1 - c)
        k = self.PER_SHARD
        passed = []
        for i in range(len(ins)):
            h = ins[i].shape[0] // 2
            mine = pl.ds(pl.multiple_of(c * h, BF16_SUBLANES), h)
            for j, (ox, oy) in enumerate(others):
                blk = outs[i].at[2 * ox + oy, mine]
                _remote(blk, blk, send_sems.at[k * i + j], recv_sems.at[k * i + j], (ox, oy, c)).wait_recv()
                cp = _remote(blk, blk, send_sems.at[k * i + 3 + j], recv_sems.at[k * i + 3 + j], sibling)
                cp.start()
                passed.append(cp)
        for i in range(len(ins)):
            h = ins[i].shape[0] // 2
            theirs = pl.ds(pl.multiple_of((1 - c) * h, BF16_SUBLANES), h)
            for j, (ox, oy) in enumerate(others):
                blk = outs[i].at[2 * ox + oy, theirs]
                _remote(blk, blk, send_sems.at[k * i + 3 + j], recv_sems.at[k * i + 3 + j], sibling).wait_recv()
            own = outs[i].at[me]
            _remote(own, own, send_sems.at[k * i + 6], recv_sems.at[k * i + 6], sibling).wait_recv()
        for cp in self._first(ins, outs, sems) + passed:
            cp.wait_send()


class _Scatter:
    def __init__(self, sums):
        self.inputs = list(sums)
        k = N_CHIPS - 1
        self.out_shapes = [jax.ShapeDtypeStruct((k,) + g.shape[1:], g.dtype) for g in sums]
        self.scratch = [pltpu.SemaphoreType.DMA((k * len(sums),)), pltpu.SemaphoreType.DMA((k * len(sums),))]

    def _copies(self, ins, outs, sems):
        send_sems, recv_sems = sems
        _, _, c, others = _place()
        k = N_CHIPS - 1
        return [_remote(ins[i].at[2 * ox + oy], outs[i].at[j], send_sems.at[k * i + j], recv_sems.at[k * i + j],
                        (ox, oy, c))
                for i in range(len(ins)) for j, (ox, oy) in enumerate(others)]

    def start(self, ins, outs, sems):
        for cp in self._copies(ins, outs, sems):
            cp.start()

    def finish(self, ins, outs, sems):
        for cp in self._copies(ins, outs, sems):
            cp.wait()


class _Swap:
    def __init__(self, grads):
        self.inputs = list(grads)
        n = len(grads)
        self.out_shapes = [jax.ShapeDtypeStruct((g.shape[0],) + g.shape[2:], g.dtype) for g in grads]
        self.scratch = [pltpu.SemaphoreType.DMA((n,)), pltpu.SemaphoreType.DMA((n,))]

    def _copies(self, ins, outs, sems):
        send_sems, recv_sems = sems
        x, y, c, _ = _place()
        return [_remote(ins[i].at[:, 1 - c], outs[i], send_sems.at[i], recv_sems.at[i], (x, y, 1 - c))
                for i in range(len(ins))]

    def start(self, ins, outs, sems):
        for cp in self._copies(ins, outs, sems):
            cp.start()

    def finish(self, ins, outs, sems):
        for cp in self._copies(ins, outs, sems):
            cp.wait()


class _Join:
    def __init__(self, finals):
        self.inputs = list(finals)
        n = len(finals)
        self.out_shapes = [jax.ShapeDtypeStruct(f.shape, f.dtype) for f in finals]
        self.scratch = [pltpu.SemaphoreType.DMA((n,)), pltpu.SemaphoreType.DMA((n,))]
        self.aliases = {i: i for i in range(n)}

    def _sends(self, outs, sems):
        send_sems, recv_sems = sems
        x, y, c, _ = _place()
        return [_remote(outs[i].at[c], outs[i].at[c], send_sems.at[i], recv_sems.at[i], (x, y, 1 - c))
                for i in range(len(outs))]

    def start(self, ins, outs, sems):
        for cp in self._sends(outs, sems):
            cp.start()

    def finish(self, ins, outs, sems):
        send_sems, recv_sems = sems
        x, y, c, _ = _place()
        for i, cp in enumerate(self._sends(outs, sems)):
            cp.wait_send()
            other = outs[i].at[1 - c]
            _remote(other, other, send_sems.at[i], recv_sems.at[i], (x, y, 1 - c)).wait_recv()


class _Both:
    def __init__(self, first, second):
        self.parts = (first, second)
        self.inputs = first.inputs + second.inputs
        self.out_shapes = first.out_shapes + second.out_shapes
        self.scratch = first.scratch + second.scratch

    def _each(self, ins, outs, sems):
        first = self.parts[0]
        a, b, c = len(first.inputs), len(first.out_shapes), len(first.scratch)
        return ((first, ins[:a], outs[:b], sems[:c]), (self.parts[1], ins[a:], outs[b:], sems[c:]))

    def start(self, ins, outs, sems):
        for part, i, o, s in self._each(ins, outs, sems):
            part.start(i, o, s)

    def finish(self, ins, outs, sems):
        for part, i, o, s in self._each(ins, outs, sems):
            part.finish(i, o, s)


def _call(body, *, grid, in_specs, out_specs, out_shape, scratch_shapes, semantics, name, args, carry=None):
    n_in, n_out, n_scr = len(in_specs), len(out_specs), len(scratch_shapes)
    if carry is None:
        res = pl.pallas_call(body, grid=grid, in_specs=in_specs, out_specs=out_specs, out_shape=out_shape,
                             scratch_shapes=scratch_shapes, compiler_params=_params(*semantics), name=name)(*args)
        return list(res), []
    nci, nco = len(carry.inputs), len(carry.out_shapes)
    a, b = n_in, n_in + nci
    c, d = b + n_out, b + n_out + nco
    e = d + n_scr

    def carried(*refs):
        ids = [pl.program_id(k) for k in range(len(grid))]
        first = functools.reduce(jnp.logical_and, [i == 0 for i in ids])
        last = functools.reduce(jnp.logical_and, [i == n - 1 for i, n in zip(ids, grid)])

        @pl.when(first)
        def _():
            carry.start(refs[a:b], refs[c:d], refs[e:])

        body(*refs[:a], *refs[b:c], *refs[d:e])

        @pl.when(last)
        def _():
            carry.finish(refs[a:b], refs[c:d], refs[e:])

    res = pl.pallas_call(
        carried,
        grid=grid,
        in_specs=list(in_specs) + [ANY] * nci,
        out_specs=list(out_specs) + [ANY] * nco,
        out_shape=list(out_shape) + carry.out_shapes,
        scratch_shapes=list(scratch_shapes) + carry.scratch,
        input_output_aliases={n_in + i: n_out + o for i, o in getattr(carry, "aliases", {}).items()},
        compiler_params=_params(*(["arbitrary"] * len(grid))),
        name=name,
    )(*args, *carry.inputs)
    return list(res[:n_out]), list(res[n_out:])


def _mm(a, b, mode, out_dtype, name, residual=None, carry=None, loss_target=None):
    if mode == "nn":
        (m, k), (k2, n) = a.shape, b.shape
    elif mode == "nt":
        (m, k), (n, k2) = a.shape, b.shape
    else:
        (k, m), (k2, n) = a.shape, b.shape
    assert k == k2, (a.shape, b.shape, mode)
    has_res = residual is not None
    has_loss = loss_target is not None
    n_in = 2 + has_res + has_loss
    tm, tn, tk = _tile(m, MM_TILE), _tile(n, MM_TILE), _tile(k, MM_TILE_K[mode])
    nk = k // tk
    dot = {"nn": _dot, "nt": _dot_nt, "tn": _dot_tn}[mode]

    def body(*refs):
        a_ref, b_ref = refs[:2]
        r_ref = refs[2] if has_res else None
        t_ref = refs[n_in - 1] if has_loss else None
        o_ref = refs[n_in]

        def finish(acc):
            if has_res:
                acc = acc + r_ref[...]
            if has_loss:
                err = acc - t_ref[...]
                dy = err * (1.0 / n)
                o_ref[...] = dy
                refs[n_in + 1][...] = dy.astype(BF16)
                tot = jnp.sum(jnp.sum(err * err, axis=-1, keepdims=True), axis=0, keepdims=True)
                refs[n_in + 2][...] = jnp.broadcast_to(tot, (8, LANES))
            else:
                o_ref[...] = acc.astype(o_ref.dtype)

        part = dot(a_ref[...], b_ref[...])
        if nk == 1:
            finish(part)
        else:
            acc_ref = refs[-1]
            kk = pl.program_id(2)

            @pl.when(kk == 0)
            def _():
                acc_ref[...] = part

            @pl.when(kk > 0)
            def _():
                acc_ref[...] += part

            @pl.when(kk == nk - 1)
            def _():
                finish(acc_ref[...])

    if mode == "tn":
        a_spec = pl.BlockSpec((tk, tm), lambda j, i, kk: (kk, i))
    else:
        a_spec = pl.BlockSpec((tm, tk), lambda j, i, kk: (i, kk))
    if mode == "nt":
        b_spec = pl.BlockSpec((tn, tk), lambda j, i, kk: (j, kk))
    else:
        b_spec = pl.BlockSpec((tk, tn), lambda j, i, kk: (kk, j))
    o_spec = pl.BlockSpec((tm, tn), lambda j, i, kk: (i, j))
    in_specs = [a_spec, b_spec] + [o_spec] * (has_res + has_loss)
    args = (a, b) + ((residual,) if has_res else ()) + ((loss_target,) if has_loss else ())
    out_specs, out_shape = [o_spec], [jax.ShapeDtypeStruct((m, n), out_dtype)]
    if has_loss:
        out_specs += [o_spec, pl.BlockSpec((8, LANES), lambda j, i, kk: (i, j))]
        out_shape += [jax.ShapeDtypeStruct((m, n), BF16), jax.ShapeDtypeStruct((m // tm * 8, n // tn * LANES), F32)]
    outs, moved = _call(
        body,
        grid=(n // tn, m // tm, nk),
        in_specs=in_specs,
        out_specs=out_specs,
        out_shape=out_shape,
        scratch_shapes=[pltpu.VMEM((tm, tn), F32)] if nk > 1 else [],
        semantics=("parallel", "parallel", "arbitrary"),
        name=name,
        args=args,
        carry=carry,
    )
    out = outs if has_loss else outs[0]
    return out if carry is None else (out, moved)


def _rms_fwd(x, g, name, carry=None):
    s, d = x.shape
    tm = _tile(s, ROW_TILE)

    def body(x_ref, g_ref, h_ref, r_ref):
        xf = x_ref[...]
        r = lax.rsqrt(jnp.mean(xf * xf, axis=-1, keepdims=True) + EPS)
        h_ref[...] = ((xf * r) * g_ref[...]).astype(BF16)
        r_ref[...] = r

    (h, rstd), moved = _call(
        body,
        grid=(s // tm,),
        in_specs=[pl.BlockSpec((tm, d), lambda i: (i, 0)), pl.BlockSpec((1, d), lambda i: (0, 0))],
        out_specs=[pl.BlockSpec((tm, d), lambda i: (i, 0)), pl.BlockSpec((tm, 1), lambda i: (i, 0))],
        out_shape=[jax.ShapeDtypeStruct((s, d), BF16), jax.ShapeDtypeStruct((s, 1), F32)],
        scratch_shapes=[],
        semantics=("parallel",),
        name=name,
        args=(x, g),
        carry=carry,
    )
    return (h, rstd) if carry is None else (h, rstd, moved)


def _rms_bwd(dh, x, rstd, g, res, name, more=None, carry=None):
    s, d = x.shape
    tm = _tile(s, ROW_TILE)
    has_res = res is not None
    has_more = more is not None
    n_in = 4 + has_res + 2 * has_more

    def body(*refs):
        dh_ref, x_ref, r_ref, g_ref = refs[:4]
        res_ref = refs[4] if has_res else None
        dx_ref, dxb_ref, dg_ref = refs[n_in:]
        dhf = dh_ref[...].astype(F32)
        if has_more:
            dhf = dhf + _dot_nt(refs[n_in - 2][...], refs[n_in - 1][...])
        xhat = x_ref[...] * r_ref[...]
        dy = dhf * g_ref[...]
        dx = r_ref[...] * (dy - xhat * jnp.mean(dy * xhat, axis=-1, keepdims=True))
        if has_res:
            dx = dx + res_ref[...]
        dx_ref[...] = dx
        dxb_ref[...] = dx.astype(BF16)
        part = jnp.sum(dhf * xhat, axis=0, keepdims=True)

        @pl.when(pl.program_id(0) == 0)
        def _():
            dg_ref[...] = part

        @pl.when(pl.program_id(0) > 0)
        def _():
            dg_ref[...] += part

    row = pl.BlockSpec((tm, d), lambda i: (i, 0))
    vec = pl.BlockSpec((1, d), lambda i: (0, 0))
    in_specs = [row, row, pl.BlockSpec((tm, 1), lambda i: (i, 0)), vec] + ([row] if has_res else [])
    args = (dh, x, rstd, g) + ((res,) if has_res else ())
    if has_more:
        k = more[0].shape[1]
        in_specs += [pl.BlockSpec((tm, k), lambda i: (i, 0)), pl.BlockSpec((d, k), lambda i: (0, 0))]
        args += tuple(more)
    (dx, dxb, dg), moved = _call(
        body,
        grid=(s // tm,),
        in_specs=in_specs,
        out_specs=[row, row, vec],
        out_shape=[jax.ShapeDtypeStruct((s, d), F32), jax.ShapeDtypeStruct((s, d), BF16),
                   jax.ShapeDtypeStruct((1, d), F32)],
        scratch_shapes=[],
        semantics=("arbitrary",),
        name=name,
        args=args,
        carry=carry,
    )
    return (dx, dxb, dg) if carry is None else (dx, dxb, dg, moved)


def _headnorm_fwd(src, col0, nheads, g, name):
    s = src.shape[0]
    tm = _tile(s, HEADNORM_ROWS)
    w = nheads * HEAD_DIM
    assert col0 % nheads == 0

    def body(x_ref, g_ref, o_ref):
        for hh in range(nheads):
            xf = _head(x_ref, hh).astype(F32)
            r = lax.rsqrt(jnp.mean(xf * xf, axis=-1, keepdims=True) + EPS)
            o_ref[:, hh * HEAD_DIM:(hh + 1) * HEAD_DIM] = ((xf * r) * g_ref[...]).astype(BF16)

    return pl.pallas_call(
        body,
        grid=(s // tm,),
        in_specs=[pl.BlockSpec((tm, w), lambda i: (i, col0 // nheads)),
                  pl.BlockSpec((1, HEAD_DIM), lambda i: (0, 0))],
        out_specs=pl.BlockSpec((tm, w), lambda i: (i, 0)),
        out_shape=jax.ShapeDtypeStruct((s, w), BF16),
        compiler_params=_params("parallel"),
        name=name,
    )(src, g)


def _headnorm_bwd(dxn, src, col0, nheads, g, name):
    s = src.shape[0]
    tm = _tile(s, HEADNORM_ROWS)
    w = nheads * HEAD_DIM
    assert col0 % nheads == 0

    def body(d_ref, x_ref, g_ref, dx_ref, dg_ref):
        part = jnp.zeros((1, HEAD_DIM), F32)
        for hh in range(nheads):
            xf = _head(x_ref, hh).astype(F32)
            r = lax.rsqrt(jnp.mean(xf * xf, axis=-1, keepdims=True) + EPS)
            xhat = xf * r
            dn = _head(d_ref, hh).astype(F32)
            dy = dn * g_ref[...]
            dx = r * (dy - xhat * jnp.mean(dy * xhat, axis=-1, keepdims=True))
            dx_ref[:, hh * HEAD_DIM:(hh + 1) * HEAD_DIM] = dx.astype(BF16)
            part = part + jnp.sum(dn * xhat, axis=0, keepdims=True)

        @pl.when(pl.program_id(0) == 0)
        def _():
            dg_ref[...] = part

        @pl.when(pl.program_id(0) > 0)
        def _():
            dg_ref[...] += part

    return pl.pallas_call(
        body,
        grid=(s // tm,),
        in_specs=[pl.BlockSpec((tm, w), lambda i: (i, 0)),
                  pl.BlockSpec((tm, w), lambda i: (i, col0 // nheads)),
                  pl.BlockSpec((1, HEAD_DIM), lambda i: (0, 0))],
        out_specs=[pl.BlockSpec((tm, w), lambda i: (i, 0)),
                   pl.BlockSpec((1, HEAD_DIM), lambda i: (0, 0))],
        out_shape=[jax.ShapeDtypeStruct((s, w), BF16), jax.ShapeDtypeStruct((1, HEAD_DIM), F32)],
        compiler_params=_params("arbitrary"),
        name=name,
    )(dxn, src, g)


def _tri(t, lower_inclusive):
    r = lax.broadcasted_iota(jnp.int32, (t, t), 0)
    c = lax.broadcasted_iota(jnp.int32, (t, t), 1)
    keep = (c <= r) if lower_inclusive else (c >= r)
    return jnp.where(keep, 1.0, 0.0).astype(BF16)


def _forget_fwd(h, w_f, b_pad):
    s, d = h.shape
    t = _tile(s, FORGET_ROWS)

    def body(h_ref, w_ref, b_ref, f_ref, c_ref, carry):
        @pl.when(pl.program_id(0) == 0)
        def _():
            carry[...] = jnp.zeros_like(carry)

        f = _dot(h_ref[...], w_ref[...])
        f_ref[...] = f
        lf = _log_sigmoid(f + b_ref[...])
        tri = _tri(t, True)
        acc = carry[...]
        for part in _split3(lf):
            acc = acc + _dot(tri, part)
        c_ref[...] = acc
        carry[...] += jnp.sum(lf, axis=0, keepdims=True)

    blk = pl.BlockSpec((t, LANES), lambda i: (i, 0))
    return pl.pallas_call(
        body,
        grid=(s // t,),
        in_specs=[pl.BlockSpec((t, d), lambda i: (i, 0)), pl.BlockSpec((d, LANES), lambda i: (0, 0)),
                  pl.BlockSpec((1, LANES), lambda i: (0, 0))],
        out_specs=[blk, blk],
        out_shape=[jax.ShapeDtypeStruct((s, LANES), F32)] * 2,
        scratch_shapes=[pltpu.VMEM((1, LANES), F32)],
        compiler_params=_params("arbitrary"),
        name="forget_fwd",
    )(h, w_f, b_pad)


def _forget_bwd(dc, f_logit, b_pad):
    s = f_logit.shape[0]
    t = _tile(s, FORGET_ROWS)
    nb = s // t

    def body(dc_ref, f_ref, b_ref, df_ref, db_ref, carry):
        @pl.when(pl.program_id(0) == 0)
        def _():
            carry[...] = jnp.zeros_like(carry)
            db_ref[...] = jnp.zeros_like(db_ref)

        d = dc_ref[...]
        tri = _tri(t, False)
        acc = carry[...]
        for part in _split3(d):
            acc = acc + _dot(tri, part)
        z = f_ref[...] + b_ref[...]
        df = acc * jnp.exp(_log_sigmoid(-z))
        df_ref[...] = df
        db_ref[...] += jnp.sum(df, axis=0, keepdims=True)
        carry[...] += jnp.sum(d, axis=0, keepdims=True)

    rev = pl.BlockSpec((t, LANES), lambda i: (nb - 1 - i, 0))
    vec = pl.BlockSpec((1, LANES), lambda i: (0, 0))
    return pl.pallas_call(
        body,
        grid=(nb,),
        in_specs=[rev, rev, vec],
        out_specs=[rev, vec],
        out_shape=[jax.ShapeDtypeStruct((s, LANES), F32), jax.ShapeDtypeStruct((1, LANES), F32)],
        scratch_shapes=[pltpu.VMEM((1, LANES), F32)],
        compiler_params=_params("arbitrary"),
        name="forget_bwd",
    )(dc, f_logit, b_pad)


SB_FWD_GROUP = 6
FOX_GROUP = 6
SB_BWD_GROUP = 3


def _head(ref, hh, rows=slice(None)):
    return ref[rows, hh * HEAD_DIM:(hh + 1) * HEAD_DIM]


def _tri_mask(t, strict):
    r = lax.broadcasted_iota(jnp.int32, (t, t), 0)
    c = lax.broadcasted_iota(jnp.int32, (t, t), 1)
    return (c < r) if strict else (c <= r)


def _fox_fwd(qn, kn, proj, colv, c_col, c_row, nheads, carry=None):
    s = qn.shape[0]
    t = _tile(s, ATT_TILE)
    scale = HEAD_DIM ** -0.5
    hg = FOX_GROUP
    gw = hg * HEAD_DIM
    assert nheads % hg == 0 and colv % hg == 0

    def body(q_ref, k_ref, v_ref, cc_ref, cr_ref, o_ref, of_ref, lse_ref):
        qi = pl.program_id(1)
        causal = _tri_mask(t, False)

        def tile(kj, carry, diagonal):
            off = pl.multiple_of(kj * t, t)
            heads = range(hg)
            rows = pl.ds(off, t)
            qk = [_dot_nt(_head(q_ref, hh), _head(k_ref, hh, rows)) for hh in heads]
            sc = [qk[hh] * scale + (cc_ref[hh] - cr_ref[hh, :, rows]) for hh in heads]
            if diagonal:
                sc = [jnp.where(causal, sc[hh], NEG_BIG) for hh in heads]
            m_new = [jnp.maximum(carry[hh][0], jnp.max(sc[hh], axis=-1, keepdims=True)) for hh in heads]
            p = [jnp.exp(sc[hh] - m_new[hh]) for hh in heads]
            pv = [_dot(p[hh].astype(BF16), _head(v_ref, hh, rows)) for hh in heads]
            out = []
            for hh in heads:
                m, l, acc = carry[hh]
                alpha = jnp.exp(m - m_new[hh])
                out.append((m_new[hh], alpha * l + jnp.sum(p[hh], axis=-1, keepdims=True), alpha * acc + pv[hh]))
            return tuple(out)

        init = tuple((jnp.full((t, 1), NEG_BIG, F32), jnp.zeros((t, 1), F32), jnp.zeros((t, HEAD_DIM), F32))
                     for _ in range(hg))
        carry = lax.fori_loop(0, qi, lambda kj, c: tile(kj, c, False), init)
        carry = tile(qi, carry, True)
        for hh in range(hg):
            m, l, acc = carry[hh]
            o = acc / l
            of_ref[:, hh * HEAD_DIM:(hh + 1) * HEAD_DIM] = o
            o_ref[:, hh * HEAD_DIM:(hh + 1) * HEAD_DIM] = o.astype(BF16)
            lse_ref[hh] = m + jnp.log(l)

    tile_spec = pl.BlockSpec((t, gw), lambda h, i: (i, h))
    w = nheads * HEAD_DIM
    return _call(
        body,
        grid=(nheads // hg, s // t),
        in_specs=[tile_spec,
                  pl.BlockSpec((s, gw), lambda h, i: (0, h), pipeline_mode=pl.Buffered(buffer_count=1)),
                  pl.BlockSpec((s, gw), lambda h, i: (0, colv // hg + h), pipeline_mode=pl.Buffered(buffer_count=1)),
                  pl.BlockSpec((hg, t, 1), lambda h, i: (h, i, 0)),
                  pl.BlockSpec((hg, 1, s), lambda h, i: (h, 0, 0))],
        out_specs=[tile_spec, tile_spec, pl.BlockSpec((hg, t, 1), lambda h, i: (h, i, 0))],
        out_shape=[jax.ShapeDtypeStruct((s, w), BF16), jax.ShapeDtypeStruct((s, w), F32),
                   jax.ShapeDtypeStruct((nheads, s, 1), F32)],
        scratch_shapes=[],
        semantics=("parallel", "parallel"),
        name="fox_fwd",
        args=(qn, kn, proj, c_col, c_row),
        carry=carry,
    )


def _fox_bwd(qn, kn, proj, colv, c_col, c_row, o, do, lse, nheads, carry=None):
    s = qn.shape[0]
    t = _tile(s, ATT_TILE)
    scale = HEAD_DIM ** -0.5
    hg = FOX_GROUP
    gw = hg * HEAD_DIM
    assert nheads % hg == 0 and colv % hg == 0

    def body(q_ref, k_ref, v_ref, cc_ref, cr_ref, o_ref, do_ref, lse_ref,
             dq_ref, dk_ref, dv_ref, drs_ref, dcs_ref):
        qi = pl.program_id(1)

        @pl.when(qi == 0)
        def _():
            dk_ref[...] = jnp.zeros_like(dk_ref)
            dv_ref[...] = jnp.zeros_like(dv_ref)
            dcs_ref[...] = jnp.zeros_like(dcs_ref)

        causal = _tri_mask(t, False)
        delta = [jnp.sum(_head(o_ref, hh) * _head(do_ref, hh).astype(F32), axis=-1, keepdims=True)
                 for hh in range(hg)]

        def tile(kj, carry, diagonal):
            off = pl.multiple_of(kj * t, t)
            heads = range(hg)
            rows = pl.ds(off, t)
            qk = [_dot_nt(_head(q_ref, hh), _head(k_ref, hh, rows)) for hh in heads]
            dp = [_dot_nt(_head(do_ref, hh), _head(v_ref, hh, rows)) for hh in heads]
            p = [jnp.exp(qk[hh] * scale + (cc_ref[hh] - cr_ref[hh, :, rows]) - lse_ref[hh]) for hh in heads]
            if diagonal:
                p = [jnp.where(causal, p[hh], 0.0) for hh in heads]
            ds = [p[hh] * (dp[hh] - delta[hh]) for hh in heads]
            dsb = [ds[hh].astype(BF16) for hh in heads]
            dv = [_dot_tn(p[hh].astype(BF16), _head(do_ref, hh)) for hh in heads]
            dk = [_dot_tn(dsb[hh], _head(q_ref, hh)) * scale for hh in heads]
            dq = [_dot(dsb[hh], _head(k_ref, hh, rows)) * scale for hh in heads]
            for hh in heads:
                cols = slice(hh * HEAD_DIM, (hh + 1) * HEAD_DIM)
                dv_ref[rows, cols] += dv[hh]
                dk_ref[rows, cols] += dk[hh]
                dcs_ref[hh, :, rows] += jnp.sum(ds[hh], axis=0, keepdims=True)
            return tuple((carry[hh][0] + dq[hh], carry[hh][1] + jnp.sum(ds[hh], axis=-1, keepdims=True))
                         for hh in heads)

        init = tuple((jnp.zeros((t, HEAD_DIM), F32), jnp.zeros((t, 1), F32)) for _ in range(hg))
        carry = lax.fori_loop(0, qi, lambda kj, c: tile(kj, c, False), init)
        carry = tile(qi, carry, True)
        for hh in range(hg):
            dq_ref[:, hh * HEAD_DIM:(hh + 1) * HEAD_DIM] = carry[hh][0]
            drs_ref[hh] = carry[hh][1]

    tile_spec = pl.BlockSpec((t, gw), lambda h, i: (i, h))
    once = pl.Buffered(buffer_count=1)
    full = pl.BlockSpec((s, gw), lambda h, i: (0, h), pipeline_mode=once)
    colspec = pl.BlockSpec((hg, t, 1), lambda h, i: (h, i, 0))
    rowspec = pl.BlockSpec((hg, 1, s), lambda h, i: (h, 0, 0))
    w = nheads * HEAD_DIM
    return _call(
        body,
        grid=(nheads // hg, s // t),
        in_specs=[tile_spec, full, pl.BlockSpec((s, gw), lambda h, i: (0, colv // hg + h), pipeline_mode=once),
                  colspec, rowspec,
                  tile_spec, tile_spec, colspec],
        out_specs=[tile_spec, full, full, colspec, rowspec],
        out_shape=[jax.ShapeDtypeStruct((s, w), F32), jax.ShapeDtypeStruct((s, w), F32),
                   jax.ShapeDtypeStruct((s, w), F32), jax.ShapeDtypeStruct((nheads, s, 1), F32),
                   jax.ShapeDtypeStruct((nheads, 1, s), F32)],
        scratch_shapes=[],
        semantics=("arbitrary", "arbitrary"),
        name="fox_bwd",
        args=(qn, kn, proj, c_col, c_row, o, do, lse),
        carry=carry,
    )


def _sb_tile(q, k, scale, later, valid):
    z = _dot_nt(q, k) * scale
    lb = _log_sigmoid(z)
    lm = lb - z
    if valid is not None:
        lm = jnp.where(valid, lm, 0.0)
    suffix = _dot(jnp.concatenate(_split2(lm), axis=1), later)
    return lb, lm, suffix


def _later(t):
    r = lax.broadcasted_iota(jnp.int32, (2 * t, t), 0) % t
    c = lax.broadcasted_iota(jnp.int32, (2 * t, t), 1)
    return jnp.where(r > c, 1.0, 0.0).astype(BF16)


def _sb_fwd(proj, colq, colk, colv, nheads, carry=None):
    s = proj.shape[0]
    t = _tile(s, ATT_TILE)
    scale = HEAD_DIM ** -0.5
    hg = SB_FWD_GROUP
    gw = hg * HEAD_DIM
    assert nheads % hg == 0 and colq % hg == 0 and colk % hg == 0 and colv % hg == 0

    def body(q_ref, k_ref, v_ref, o_ref):
        qi = pl.program_id(1)
        later = _later(t)
        before = _tri_mask(t, True)

        def tile(kj, carry, diagonal):
            off = pl.multiple_of(kj * t, t)
            heads = range(hg)
            z = [_dot_nt(_head(q_ref, hh), _head(k_ref, hh, pl.ds(off, t))) * scale for hh in heads]
            lb = [_log_sigmoid(z[hh]) for hh in heads]
            lm = [lb[hh] - z[hh] for hh in heads]
            if diagonal:
                lm = [jnp.where(before, lm[hh], 0.0) for hh in heads]
            parts = [jnp.concatenate(_split2(lm[hh]), axis=1) for hh in heads]
            suffix = [_dot(parts[hh], later) for hh in heads]
            a = [jnp.exp(lb[hh] + suffix[hh] + carry[hh][0]) for hh in heads]
            if diagonal:
                a = [jnp.where(before, a[hh], 0.0) for hh in heads]
            av = [_dot(a[hh].astype(BF16), _head(v_ref, hh, pl.ds(off, t))) for hh in heads]
            return tuple((carry[hh][0] + jnp.sum(lm[hh], axis=-1, keepdims=True), carry[hh][1] + av[hh])
                         for hh in heads)

        init = tuple((jnp.zeros((t, 1), F32), jnp.zeros((t, HEAD_DIM), F32)) for _ in range(hg))
        carry = tile(qi, init, True)
        carry = lax.fori_loop(1, qi + 1, lambda i, c: tile(qi - i, c, False), carry)
        for hh in range(hg):
            o_ref[:, hh * HEAD_DIM:(hh + 1) * HEAD_DIM] = carry[hh][1].astype(BF16)

    return _call(
        body,
        grid=(nheads // hg, s // t),
        in_specs=[pl.BlockSpec((t, gw), lambda h, i: (i, colq // hg + h)),
                  pl.BlockSpec((s, gw), lambda h, i: (0, colk // hg + h), pipeline_mode=pl.Buffered(buffer_count=1)),
                  pl.BlockSpec((s, gw), lambda h, i: (0, colv // hg + h), pipeline_mode=pl.Buffered(buffer_count=1))],
        out_specs=[pl.BlockSpec((t, gw), lambda h, i: (i, h))],
        out_shape=[jax.ShapeDtypeStruct((s, nheads * HEAD_DIM), BF16)],
        scratch_shapes=[],
        semantics=("parallel", "parallel"),
        name="sb_fwd",
        args=(proj, proj, proj),
        carry=carry,
    )


def _sb_bwd(proj, colq, colk, colv, do, nheads, carry=None):
    s = proj.shape[0]
    t = _tile(s, ATT_TILE)
    scale = HEAD_DIM ** -0.5
    hg = SB_BWD_GROUP
    gw = hg * HEAD_DIM
    assert nheads % hg == 0 and colq % hg == 0 and colk % hg == 0 and colv % hg == 0

    def body(q_ref, k_ref, v_ref, do_ref, dq_ref, dk_ref, dv_ref, g_s, beta_s):
        qi = pl.program_id(1)

        @pl.when(qi == 0)
        def _():
            dk_ref[...] = jnp.zeros_like(dk_ref)
            dv_ref[...] = jnp.zeros_like(dv_ref)

        later = _later(t)
        before = _tri_mask(t, True)

        def back(kj, carry, diagonal):
            off = pl.multiple_of(kj * t, t)
            heads = range(hg)
            rows = pl.ds(off, t)
            z = [_dot_nt(_head(q_ref, hh), _head(k_ref, hh, rows)) * scale for hh in heads]
            da = [_dot_nt(_head(do_ref, hh), _head(v_ref, hh, rows)) for hh in heads]
            lb = [_log_sigmoid(z[hh]) for hh in heads]
            lm = [lb[hh] - z[hh] for hh in heads]
            if diagonal:
                lm = [jnp.where(before, lm[hh], 0.0) for hh in heads]
            parts = [jnp.concatenate(_split2(lm[hh]), axis=1) for hh in heads]
            suffix = [_dot(parts[hh], later) for hh in heads]
            a = [jnp.exp(lb[hh] + suffix[hh] + carry[hh]) for hh in heads]
            if diagonal:
                a = [jnp.where(before, a[hh], 0.0) for hh in heads]
            dv = [_dot_tn(a[hh].astype(BF16), _head(do_ref, hh)) for hh in heads]
            for hh in heads:
                g_s[hh, :, rows] = a[hh] * da[hh]
                beta_s[hh, :, rows] = jnp.exp(lb[hh]).astype(BF16)
            for hh in heads:
                dv_ref[rows, hh * HEAD_DIM:(hh + 1) * HEAD_DIM] += dv[hh]
            return tuple(carry[hh] + jnp.sum(lm[hh], axis=-1, keepdims=True) for hh in heads)

        rc = back(qi, tuple(jnp.zeros((t, 1), F32) for _ in range(hg)), True)
        lax.fori_loop(1, qi + 1, lambda i, c: back(qi - i, c, False), rc)

        earlier = jnp.where(lax.broadcasted_iota(jnp.int32, (2 * t, t), 0) % t
                            < lax.broadcasted_iota(jnp.int32, (2 * t, t), 1), 1.0, 0.0).astype(BF16)

        def fwd(kj, carry, diagonal):
            off = pl.multiple_of(kj * t, t)
            heads = range(hg)
            rows = pl.ds(off, t)
            g = [g_s[hh, :, rows] for hh in heads]
            parts = [jnp.concatenate(_split2(g[hh]), axis=1) for hh in heads]
            gsum = [_dot(parts[hh], earlier) + carry[hh][0] for hh in heads]
            dz = []
            for hh in heads:
                beta = beta_s[hh, :, rows].astype(F32)
                d = g[hh] * (1.0 - beta) - gsum[hh] * beta
                if diagonal:
                    d = jnp.where(before, d, 0.0)
                dz.append(d.astype(BF16))
            dk = [_dot_tn(dz[hh], _head(q_ref, hh)) * scale for hh in heads]
            dq = [_dot(dz[hh], _head(k_ref, hh, rows)) * scale for hh in heads]
            for hh in heads:
                dk_ref[rows, hh * HEAD_DIM:(hh + 1) * HEAD_DIM] += dk[hh]
            return tuple((carry[hh][0] + jnp.sum(g[hh], axis=-1, keepdims=True), carry[hh][1] + dq[hh])
                         for hh in heads)

        init = tuple((jnp.zeros((t, 1), F32), jnp.zeros((t, HEAD_DIM), F32)) for _ in range(hg))
        carry = lax.fori_loop(0, qi, lambda kj, c: fwd(kj, c, False), init)
        carry = fwd(qi, carry, True)
        for hh in range(hg):
            dq_ref[:, hh * HEAD_DIM:(hh + 1) * HEAD_DIM] = carry[hh][1]

    once = pl.Buffered(buffer_count=1)
    tile_spec = pl.BlockSpec((t, gw), lambda h, i: (i, h))
    full = pl.BlockSpec((s, gw), lambda h, i: (0, h), pipeline_mode=once)
    w = nheads * HEAD_DIM
    return _call(
        body,
        grid=(nheads // hg, s // t),
        in_specs=[pl.BlockSpec((t, gw), lambda h, i: (i, colq // hg + h)),
                  pl.BlockSpec((s, gw), lambda h, i: (0, colk // hg + h), pipeline_mode=once),
                  pl.BlockSpec((s, gw), lambda h, i: (0, colv // hg + h), pipeline_mode=once),
                  tile_spec],
        out_specs=[tile_spec, full, full],
        out_shape=[jax.ShapeDtypeStruct((s, w), F32)] * 3,
        scratch_shapes=[pltpu.VMEM((hg, t, s), F32), pltpu.VMEM((hg, t, s), BF16)],
        semantics=("arbitrary", "arbitrary"),
        name="sb_bwd",
        args=(proj, proj, proj, do),
        carry=carry,
    )


def _mem_fwd(qn, kn, mkv, nheads):
    s = qn.shape[0]
    mtok = kn.shape[0]
    t = _tile(s, MEM_ROWS)
    w = nheads * HEAD_DIM
    scale = HEAD_DIM ** -0.5

    def body(q_ref, k_ref, v_ref, o_ref):
        heads = range(nheads)
        sc = [_dot_nt(_head(q_ref, hh), _head(k_ref, hh)) * scale for hh in heads]
        p = [jnp.exp(sc[hh] - jnp.max(sc[hh], axis=-1, keepdims=True)) for hh in heads]
        p = [p[hh] / jnp.sum(p[hh], axis=-1, keepdims=True) for hh in heads]
        o = [_dot(p[hh].astype(BF16), _head(v_ref, hh)) for hh in heads]
        for hh in heads:
            o_ref[:, hh * HEAD_DIM:(hh + 1) * HEAD_DIM] = o[hh].astype(BF16)

    return pl.pallas_call(
        body,
        grid=(s // t,),
        in_specs=[pl.BlockSpec((t, w), lambda i: (i, 0)),
                  pl.BlockSpec((mtok, w), lambda i: (0, 0)),
                  pl.BlockSpec((mtok, w), lambda i: (0, 1))],
        out_specs=pl.BlockSpec((t, w), lambda i: (i, 0)),
        out_shape=jax.ShapeDtypeStruct((s, w), BF16),
        compiler_params=_params("parallel"),
        name="mem_fwd",
    )(qn, kn, mkv)


def _mem_bwd(qn, kn, mkv, do, nheads):
    s = qn.shape[0]
    mtok = kn.shape[0]
    t = _tile(s, MEM_ROWS)
    w = nheads * HEAD_DIM
    scale = HEAD_DIM ** -0.5

    def body(q_ref, k_ref, v_ref, do_ref, dq_ref, dk_ref, dv_ref):
        @pl.when(pl.program_id(0) == 0)
        def _():
            dk_ref[...] = jnp.zeros_like(dk_ref)
            dv_ref[...] = jnp.zeros_like(dv_ref)

        heads = range(nheads)
        sc = [_dot_nt(_head(q_ref, hh), _head(k_ref, hh)) * scale for hh in heads]
        dp = [_dot_nt(_head(do_ref, hh), _head(v_ref, hh)) for hh in heads]
        p = [jnp.exp(sc[hh] - jnp.max(sc[hh], axis=-1, keepdims=True)) for hh in heads]
        p = [p[hh] / jnp.sum(p[hh], axis=-1, keepdims=True) for hh in heads]
        ds = [(p[hh] * (dp[hh] - jnp.sum(p[hh] * dp[hh], axis=-1, keepdims=True))).astype(BF16) for hh in heads]
        dq = [_dot(ds[hh], _head(k_ref, hh)) * scale for hh in heads]
        dk = [_dot_tn(ds[hh], _head(q_ref, hh)) * scale for hh in heads]
        dv = [_dot_tn(p[hh].astype(BF16), _head(do_ref, hh)) for hh in heads]
        for hh in heads:
            cols = slice(hh * HEAD_DIM, (hh + 1) * HEAD_DIM)
            dq_ref[:, cols] = dq[hh]
            dk_ref[:, cols] += dk[hh]
            dv_ref[:, cols] += dv[hh]

    tile = pl.BlockSpec((t, w), lambda i: (i, 0))
    kspec = pl.BlockSpec((mtok, w), lambda i: (0, 0))
    return pl.pallas_call(
        body,
        grid=(s // t,),
        in_specs=[tile, kspec, pl.BlockSpec((mtok, w), lambda i: (0, 1)), tile],
        out_specs=[tile, kspec, kspec],
        out_shape=[jax.ShapeDtypeStruct((s, w), F32), jax.ShapeDtypeStruct((mtok, w), F32),
                   jax.ShapeDtypeStruct((mtok, w), F32)],
        compiler_params=_params("arbitrary"),
        name="mem_bwd",
    )(qn, kn, mkv, do)


def _merge_fwd(p0, p1, p2, gates, b_gate):
    s, d = p0.shape
    tm, tn = _tile(s, ROW_TILE), _tile(d, COL_TILE)
    nj = d // tn

    def body(p0_ref, p1_ref, p2_ref, ga_ref, gb_ref, gc_ref, b_ref, o_ref, sa_ref, sb_ref, sc_ref):
        acc = jnp.zeros((tm, tn), F32)
        for b, (p_ref, g_ref, s_ref) in enumerate(((p0_ref, ga_ref, sa_ref), (p1_ref, gb_ref, sb_ref),
                                                   (p2_ref, gc_ref, sc_ref))):
            gate = jax.nn.sigmoid(g_ref[...].astype(F32) + b_ref[b:b + 1, :])
            s_ref[...] = gate.astype(BF16)
            acc = acc + gate * p_ref[...].astype(F32)
        o_ref[...] = acc.astype(BF16)

    blk = pl.BlockSpec((tm, tn), lambda i, j: (i, j))
    merged, *sig = pl.pallas_call(
        body,
        grid=(s // tm, nj),
        in_specs=[blk] * 6 + [pl.BlockSpec((3, tn), lambda i, j: (0, j))],
        out_specs=[blk] * 4,
        out_shape=[jax.ShapeDtypeStruct((s, d), BF16)] * 4,
        compiler_params=_params("parallel", "parallel"),
        name="merge_fwd",
    )(p0, p1, p2, *gates, b_gate)
    return merged, tuple(sig)


def _merge_bwd(dmerged, p0, p1, p2, sig):
    s, d = p0.shape
    tm, tn = _tile(s, ROW_TILE), _tile(d, COL_TILE)
    nj = d // tn

    def body(dm_ref, p0_ref, p1_ref, p2_ref, ga_ref, gb_ref, gc_ref,
             d0_ref, d1_ref, d2_ref, dga_ref, dgb_ref, dgc_ref, db_ref):
        dm = dm_ref[...].astype(F32)
        parts = []
        for p_ref, g_ref, dp_ref, dg_ref in ((p0_ref, ga_ref, d0_ref, dga_ref), (p1_ref, gb_ref, d1_ref, dgb_ref),
                                             (p2_ref, gc_ref, d2_ref, dgc_ref)):
            gate = g_ref[...].astype(F32)
            dp_ref[...] = (dm * gate).astype(BF16)
            dgate = dm * p_ref[...].astype(F32) * gate * (1.0 - gate)
            dg_ref[...] = dgate.astype(BF16)
            parts.append(jnp.sum(dgate, axis=0, keepdims=True))
        part = jnp.concatenate(parts, axis=0)

        @pl.when(pl.program_id(1) == 0)
        def _():
            db_ref[...] = part

        @pl.when(pl.program_id(1) > 0)
        def _():
            db_ref[...] += part

    blk = pl.BlockSpec((tm, tn), lambda j, i: (i, j))
    bias = pl.BlockSpec((3, tn), lambda j, i: (0, j))
    return pl.pallas_call(
        body,
        grid=(nj, s // tm),
        in_specs=[blk] * 7,
        out_specs=[blk] * 6 + [bias],
        out_shape=[jax.ShapeDtypeStruct((s, d), BF16)] * 6 + [jax.ShapeDtypeStruct((3, d), F32)],
        compiler_params=_params("parallel", "arbitrary"),
        name="merge_bwd",
    )(dmerged, p0, p1, p2, *sig)


def _shift_down(v, n):
    rows = lax.broadcasted_iota(jnp.int32, v.shape, 0)
    return jnp.where(rows >= n, pltpu.roll(v, n, 0), 0.0)


def _shift_up(v, n):
    s = v.shape[0]
    rows = lax.broadcasted_iota(jnp.int32, v.shape, 0)
    return jnp.where(rows < s - n, pltpu.roll(v, s - n, 0), 0.0)


def _conv(v, w_ref, b_ref):
    taps = w_ref.shape[0]
    out = v * w_ref[taps - 1:taps, :] + b_ref[...]
    for n in range(1, taps):
        out = out + _shift_down(v, n) * w_ref[taps - 1 - n:taps - n, :]
    return out


def _conv_act_fwd(up, conv_w, conv_b):
    s, f2 = up.shape
    f = f2 // 2
    tn = LANES
    nj = f // tn
    taps = conv_w.shape[0]

    def body(ug_ref, uv_ref, wg_ref, wv_ref, bg_ref, bv_ref, o_ref, cg_ref, cv_ref):
        cg = _conv(ug_ref[...].astype(F32), wg_ref, bg_ref)
        cv = _conv(uv_ref[...].astype(F32), wv_ref, bv_ref)
        o_ref[...] = (cg * jax.nn.sigmoid(cg) * cv).astype(BF16)
        cg_ref[...] = cg.astype(BF16)
        cv_ref[...] = cv.astype(BF16)

    out = pl.BlockSpec((s, tn), lambda j: (0, j))
    return pl.pallas_call(
        body,
        grid=(nj,),
        in_specs=[pl.BlockSpec((s, tn), lambda j: (0, j)), pl.BlockSpec((s, tn), lambda j: (0, nj + j)),
                  pl.BlockSpec((taps, tn), lambda j: (0, j)), pl.BlockSpec((taps, tn), lambda j: (0, nj + j)),
                  pl.BlockSpec((1, tn), lambda j: (0, j)), pl.BlockSpec((1, tn), lambda j: (0, nj + j))],
        out_specs=[out, out, out],
        out_shape=[jax.ShapeDtypeStruct((s, f), BF16)] * 3,
        compiler_params=_params("parallel"),
        name="conv_act_fwd",
    )(up, up, conv_w, conv_w, conv_b, conv_b)


def _conv_act_bwd(up, conv_w, conv_g, conv_v, dact):
    s, f2 = up.shape
    f = f2 // 2
    tn = LANES
    nj = f // tn
    taps = conv_w.shape[0]

    def half(v, du, w_ref, dup_ref, dw_ref, db_ref):
        dup = du * w_ref[taps - 1:taps, :]
        rows = [None] * taps
        rows[taps - 1] = jnp.sum(du * v, axis=0, keepdims=True)
        for n in range(1, taps):
            later = _shift_up(du, n)
            dup = dup + later * w_ref[taps - 1 - n:taps - n, :]
            rows[taps - 1 - n] = jnp.sum(later * v, axis=0, keepdims=True)
        dup_ref[...] = dup.astype(BF16)
        dw_ref[...] = jnp.concatenate(rows, axis=0)
        db_ref[...] = jnp.sum(du, axis=0, keepdims=True)

    def body(ug_ref, uv_ref, wg_ref, wv_ref, cg_ref, cv_ref, da_ref,
             dug_ref, duv_ref, dwg_ref, dwv_ref, dbg_ref, dbv_ref):
        cg = cg_ref[...].astype(F32)
        cv = cv_ref[...].astype(F32)
        da = da_ref[...].astype(F32)
        sg = jax.nn.sigmoid(cg)
        dcv = da * cg * sg
        dcg = da * cv * (sg + cg * sg * (1.0 - sg))
        half(ug_ref[...].astype(F32), dcg, wg_ref, dug_ref, dwg_ref, dbg_ref)
        half(uv_ref[...].astype(F32), dcv, wv_ref, duv_ref, dwv_ref, dbv_ref)

    lo = lambda rows: pl.BlockSpec((rows, tn), lambda j: (0, j))
    hi = lambda rows: pl.BlockSpec((rows, tn), lambda j: (0, nj + j))
    return pl.pallas_call(
        body,
        grid=(nj,),
        in_specs=[lo(s), hi(s), lo(taps), hi(taps), lo(s), lo(s), lo(s)],
        out_specs=[lo(s), lo(s), lo(taps), lo(taps), lo(1), lo(1)],
        out_shape=[jax.ShapeDtypeStruct((s, f), BF16)] * 2 + [jax.ShapeDtypeStruct((taps, f), F32)] * 2
        + [jax.ShapeDtypeStruct((1, f), F32)] * 2,
        compiler_params=_params("parallel"),
        name="conv_act_bwd",
    )(up, up, conv_w, conv_w, conv_g, conv_v, dact)


def _row_tile(rows, row_bytes, budget):
    if rows * row_bytes <= budget or rows % 8:
        return rows
    best = 8
    for t in range(8, rows, 8):
        if rows % t == 0 and t * row_bytes <= budget:
            best = t
    return best


def _adamw(w, g, m, v, name):
    r, c = w.shape
    tr = _row_tile(r, c * 4, ADAM_BLOCK_BYTES)

    def body(w_ref, g_ref, m_ref, v_ref, d_ref, mo_ref, vo_ref):
        gg = g_ref[...]
        m_new = ADAM_B1 * m_ref[...] + (1.0 - ADAM_B1) * gg
        v_new = ADAM_B2 * v_ref[...] + (1.0 - ADAM_B2) * (gg * gg)
        m_hat = m_new / (1.0 - ADAM_B1 ** ADAM_STEP)
        v_hat = v_new / (1.0 - ADAM_B2 ** ADAM_STEP)
        d_ref[...] = -ADAM_LR * (m_hat / (jnp.sqrt(v_hat) + ADAM_EPS) + ADAM_WD * w_ref[...])
        mo_ref[...] = m_new
        vo_ref[...] = v_new

    blk = pl.BlockSpec((tr, c), lambda i: (i, 0))
    return pl.pallas_call(
        body,
        grid=(r // tr,),
        in_specs=[blk] * 4,
        out_specs=[blk] * 3,
        out_shape=[jax.ShapeDtypeStruct((r, c), F32)] * 3,
        compiler_params=_params("parallel"),
        name=name,
    )(w, g, m, v)


def _add_sibling(g, r1, core, name):
    _, _, h, c = g.shape
    th = _row_tile(h, c * 2, ADAM_BLOCK_BYTES)

    def body(core_ref, g_ref, r_ref, o_ref):
        o_ref[...] = (g_ref[...].astype(F32) + r_ref[...].astype(F32)).astype(BF16)

    return pl.pallas_call(
        body,
        grid_spec=pltpu.PrefetchScalarGridSpec(
            num_scalar_prefetch=1,
            grid=(N_CHIPS, h // th),
            in_specs=[pl.BlockSpec((None, None, th, c), lambda j, i, core_ref: (j, core_ref[0], i, 0)),
                      pl.BlockSpec((None, th, c), lambda j, i, core_ref: (j, i, 0))],
            out_specs=pl.BlockSpec((None, th, c), lambda j, i, core_ref: (j, i, 0)),
        ),
        out_shape=jax.ShapeDtypeStruct((N_CHIPS, h, c), BF16),
        compiler_params=_params("parallel", "parallel"),
        name=name,
    )(core, g, r1)


def _add_chips(hsum, r2, chip_core, name):
    _, h, c = hsum.shape
    th = _row_tile(h, c * 4, ADAM_BLOCK_BYTES)

    def body(sel_ref, own_ref, r_ref, o_ref):
        acc = own_ref[...].astype(F32)
        for j in range(N_CHIPS - 1):
            acc = acc + r_ref[j].astype(F32)
        o_ref[...] = acc

    return pl.pallas_call(
        body,
        grid_spec=pltpu.PrefetchScalarGridSpec(
            num_scalar_prefetch=1,
            grid=(h // th,),
            in_specs=[pl.BlockSpec((None, th, c), lambda i, sel_ref: (sel_ref[0], i, 0)),
                      pl.BlockSpec((N_CHIPS - 1, th, c), lambda i, sel_ref: (0, i, 0))],
            out_specs=pl.BlockSpec((None, th, c), lambda i, sel_ref: (sel_ref[1], i, 0)),
        ),
        out_shape=jax.ShapeDtypeStruct((2, h, c), F32),
        compiler_params=_params("parallel"),
        name=name,
    )(chip_core, hsum, r2)


def _sum_devices(parts):
    _, r, c = parts.shape

    def body(p_ref, o_ref):
        acc = p_ref[0]
        for j in range(1, N_DEV):
            acc = acc + p_ref[j]
        o_ref[...] = acc

    return pl.pallas_call(
        body,
        out_shape=jax.ShapeDtypeStruct((r, c), F32),
        compiler_params=pltpu.CompilerParams(vmem_limit_bytes=VMEM_LIMIT_BYTES),
        name="sum_devices",
    )(parts)


def _gather_small(vec):
    k = N_DEV - 1

    def body(v_ref, o_ref, send_sems, recv_sems, local_sem):
        x, y, c, _ = _place()
        me = 4 * x + 2 * y + c
        local = pltpu.make_async_copy(v_ref, o_ref.at[me], local_sem)
        local.start()
        peers = [(x ^ (r >> 2 & 1), y ^ (r >> 1 & 1), c ^ (r & 1)) for r in range(1, N_DEV)]
        sends = [_remote(v_ref, o_ref.at[me], send_sems.at[j], recv_sems.at[j], p) for j, p in enumerate(peers)]
        for cp in sends:
            cp.start()
        for j, (px, py, pc) in enumerate(peers):
            sends[j].wait_send()
            blk = o_ref.at[4 * px + 2 * py + pc]
            _remote(blk, blk, send_sems.at[j], recv_sems.at[j], (px, py, pc)).wait_recv()
        local.wait()

    return pl.pallas_call(
        body,
        in_specs=[ANY],
        out_specs=ANY,
        out_shape=jax.ShapeDtypeStruct((N_DEV,) + vec.shape, vec.dtype),
        scratch_shapes=[pltpu.SemaphoreType.DMA((k,)), pltpu.SemaphoreType.DMA((k,)), pltpu.SemaphoreType.DMA(())],
        name="gather_small",
    )(vec)


W_IN_GATES = ("w_in_g0", "w_in_g1", "w_in_g2")
ROW_SHARDED = ("w_in_a", "w_in_f") + W_IN_GATES + ("w_mem_kv", "w_out", "w_down")
COL_SHARDED = ("w_br_fox", "w_br_sb", "w_br_mem", "w_up")
BIG = ROW_SHARDED + COL_SHARDED


def _whole(name, a):
    if name in ROW_SHARDED:
        return a.reshape(N_CHIPS * a.shape[1], a.shape[2])
    return a.transpose(1, 0, 2).reshape(a.shape[1], N_CHIPS * a.shape[2])


def _by_shard(name, grad):
    if name in ROW_SHARDED:
        a = grad.reshape(N_CHIPS, grad.shape[0] // N_CHIPS, grad.shape[1])
    else:
        a = grad.reshape(grad.shape[0], N_CHIPS, grad.shape[1] // N_CHIPS).transpose(1, 0, 2)
    return a.reshape(N_CHIPS, 2, a.shape[1] // 2, a.shape[2])


def _sibling_sums(names, split, theirs, core):
    return [_add_sibling(a, r, core, "add_sibling_" + name) for name, a, r in zip(names, split, theirs)]


GATHER_FIRST = ("w_in_a", "w_in_f")
GATHER_EARLY = W_IN_GATES[:2]
GATHER_MIX = ("w_out", "w_br_fox", "w_br_sb", "w_br_mem") + W_IN_GATES[2:]
REDUCE_FFN = ("w_down", "w_up")
REDUCE_MIX = ("w_out", "w_br_fox", "w_br_sb", "w_br_mem", "w_mem_kv") + W_IN_GATES
REDUCE_IN = ("w_in_a", "w_in_f")


def _local_step(x, mem, target, w, shard, core, chip_core):
    d = x.shape[1]
    nf = shard["w_br_fox"].shape[0] // HEAD_DIM
    nsb = shard["w_br_sb"].shape[0] // HEAD_DIM
    nm = shard["w_br_mem"].shape[0] // HEAD_DIM
    w = dict(w)

    def take(names, gathered):
        for name, a in zip(names, gathered):
            w[name] = _whole(name, a)

    fq, fk, fv = 0, nf, 2 * nf
    sq, sk, sv = 3 * nf, 3 * nf + nsb, 3 * nf + 2 * nsb
    mq = 3 * nf + 3 * nsb

    h, rstd1, moved = _rms_fwd(x, w["g_mix"], "rms_mix_fwd", carry=_Gather([shard[name] for name in GATHER_FIRST]))
    take(GATHER_FIRST, moved)
    proj, moved = _mm(h, w["w_in_a"], "nn", BF16, "proj_att", carry=_Gather([shard[name] for name in GATHER_EARLY]))
    take(GATHER_EARLY, moved)
    gate0, moved = _mm(h, w["w_in_g0"], "nn", BF16, "proj_gate0", carry=_Gather([shard["w_mem_kv"]]))
    take(("w_mem_kv",), moved)
    gate1 = _mm(h, w["w_in_g1"], "nn", BF16, "proj_gate1")
    f_logit, c_sum = _forget_fwd(h, w["w_in_f"], w["b_forget"])
    c_t = c_sum[:, :nf].T
    c_col, c_row = c_t[:, :, None], c_t[:, None, :]
    qn = _headnorm_fwd(proj, fq, nf, w["g_q_fox"], "fox_qnorm_fwd")
    kn = _headnorm_fwd(proj, fk, nf, w["g_k_fox"], "fox_knorm_fwd")
    (o_fox, o_fox32, lse), moved = _fox_fwd(qn, kn, proj, fv, c_col, c_row, nf,
                                            carry=_Gather([shard[name] for name in GATHER_MIX]))
    take(GATHER_MIX, moved)
    gates = (gate0, gate1, _mm(h, w["w_in_g2"], "nn", BF16, "proj_gate2"))
    (o_sb,), moved = _sb_fwd(proj, sq, sk, sv, nsb, carry=_Gather([shard["w_up"]]))
    take(("w_up",), moved)
    memn, rstd_m = _rms_fwd(mem, w["g_mem"], "rms_mem_fwd")
    mkv = _mm(memn, w["w_mem_kv"], "nn", BF16, "mem_kv")
    kmn = _headnorm_fwd(mkv, 0, nm, w["g_k_mem"], "mem_knorm_fwd")
    qmn = _headnorm_fwd(proj, mq, nm, w["g_q_mem"], "mem_qnorm_fwd")
    o_mem = _mem_fwd(qmn, kmn, mkv, nm)
    p0 = _mm(o_fox, w["w_br_fox"], "nn", BF16, "branch_fox")
    p1 = _mm(o_sb, w["w_br_sb"], "nn", BF16, "branch_sb")
    p2 = _mm(o_mem, w["w_br_mem"], "nn", BF16, "branch_mem")
    merged, sig = _merge_fwd(p0, p1, p2, gates, w["b_gate"])
    x1 = _mm(merged, w["w_out"], "nn", F32, "out_proj", residual=x)
    h2, rstd2 = _rms_fwd(x1, w["g_ffn"], "rms_ffn_fwd")
    up, moved = _mm(h2, w["w_up"], "nn", BF16, "ffn_up", carry=_Gather([shard["w_down"]]))
    take(("w_down",), moved)
    act, conv_g, conv_v = _conv_act_fwd(up, w["conv_w"], w["conv_b"])
    dy, dyb, lparts = _mm(act, w["w_down"], "nn", F32, "ffn_down_loss", residual=x1, loss_target=target)
    loss = (0.5 / d) * jnp.sum(lparts[::8, ::LANES])

    g = {}
    dact = _mm(dyb, w["w_down"], "nt", BF16, "ffn_down_dx")
    g["w_down"] = _mm(act, dyb, "tn", BF16, "ffn_down_dw")
    dug, duv, dwg, dwv, dbg, dbv = _conv_act_bwd(up, w["conv_w"], conv_g, conv_v, dact)
    dup = jnp.concatenate([dug, duv], axis=1)
    g["conv_w"] = jnp.concatenate([dwg, dwv], axis=1)
    g["conv_b"] = jnp.concatenate([dbg, dbv], axis=1)
    split_down = [_by_shard("w_down", g["w_down"])]
    dh2, theirs_down = _mm(dup, w["w_up"], "nt", BF16, "ffn_up_dx", carry=_Swap(split_down))
    g["w_up"] = _mm(h2, dup, "tn", BF16, "ffn_up_dw")
    split_up = [_by_shard("w_up", g["w_up"])]
    dx1, dx1b, g["g_ffn"] = _rms_bwd(dh2, x1, rstd2, w["g_ffn"], dy, "rms_ffn_bwd")
    dmerged, theirs_up = _mm(dx1b, w["w_out"], "nt", BF16, "out_proj_dx", carry=_Swap(split_up))
    sums_ffn = _sibling_sums(REDUCE_FFN, split_down + split_up, theirs_down + theirs_up, core)
    g["w_out"] = _mm(merged, dx1b, "tn", BF16, "out_proj_dw")
    dp0, dp1, dp2, dga, dgb, dgc, g["b_gate"] = _merge_bwd(dmerged, p0, p1, p2, sig)
    dgates = (dga, dgb, dgc)
    for name, dgate in zip(W_IN_GATES, dgates):
        g[name] = _mm(h, dgate, "tn", BF16, name + "_dw")
    do_fox = _mm(dp0, w["w_br_fox"], "nt", BF16, "branch_fox_dx")
    do_sb = _mm(dp1, w["w_br_sb"], "nt", BF16, "branch_sb_dx")
    do_mem = _mm(dp2, w["w_br_mem"], "nt", BF16, "branch_mem_dx")
    g["w_br_fox"] = _mm(o_fox, dp0, "tn", BF16, "branch_fox_dw")
    g["w_br_sb"] = _mm(o_sb, dp1, "tn", BF16, "branch_sb_dw")
    g["w_br_mem"] = _mm(o_mem, dp2, "tn", BF16, "branch_mem_dw")
    dqmn, dkmn, dvm = _mem_bwd(qmn, kmn, mkv, do_mem, nm)
    dmq, g["g_q_mem"] = _headnorm_bwd(dqmn, proj, mq, nm, w["g_q_mem"], "mem_qnorm_bwd")
    dkm, g["g_k_mem"] = _headnorm_bwd(dkmn, mkv, 0, nm, w["g_k_mem"], "mem_knorm_bwd")
    dmkv = jnp.concatenate([dkm, dvm.astype(BF16)], axis=1)
    g["w_mem_kv"] = _mm(memn, dmkv, "tn", BF16, "mem_kv_dw")
    dmemn = _mm(dmkv, w["w_mem_kv"], "nt", BF16, "mem_kv_dx")
    _, _, g["g_mem"] = _rms_bwd(dmemn, mem, rstd_m, w["g_mem"], None, "rms_mem_bwd")
    split_mix =[_by_shard(name, g[name]) for name in REDUCE_MIX]

    (dqn, dkn, dfv, drs, dcs), moved = _fox_bwd(qn, kn, proj, fv, c_col, c_row, o_fox32, do_fox, lse, nf,
                                                carry=_Both(_Scatter(sums_ffn[1:]), _Swap(split_mix)))
    others_up, theirs_mix = moved[:1], moved[1:]
    sums_mix = _sibling_sums(REDUCE_MIX, split_mix, theirs_mix, core)
    dfq, g["g_q_fox"] = _headnorm_bwd(dqn, proj, fq, nf, w["g_q_fox"], "fox_qnorm_bwd")
    dfk, g["g_k_fox"] = _headnorm_bwd(dkn, proj, fk, nf, w["g_k_fox"], "fox_knorm_bwd")
    dc = jnp.pad((drs[:, :, 0] - dcs[:, 0, :]).T, ((0, 0), (0, LANES - nf)))
    df, g["b_forget"] = _forget_bwd(dc, f_logit, w["b_forget"])
    (dsq, dsk, dsv), others_mix = _sb_bwd(proj, sq, sk, sv, do_sb, nsb, carry=_Scatter(sums_ffn[:1] + sums_mix))
    others_ffn = others_mix[:1] + others_up
    others_mix = others_mix[1:]

    dproj = jnp.concatenate([dfq, dfk, dfv.astype(BF16), dsq.astype(BF16), dsk.astype(BF16), dsv.astype(BF16), dmq],
                            axis=1)
    dfb = df.astype(BF16)
    g["w_in_a"] = _mm(h, dproj, "tn", BF16, "proj_att_dw")
    g["w_in_f"] = _mm(h, dfb, "tn", BF16, "proj_forget_dw")
    split_in = [_by_shard(name, g[name]) for name in REDUCE_IN]
    dh, theirs_in = _mm(dgates[0], w[W_IN_GATES[0]], "nt", F32, W_IN_GATES[0] + "_dx", carry=_Swap(split_in))
    sums_in = _sibling_sums(REDUCE_IN, split_in, theirs_in, core)
    for name, dgate in zip(W_IN_GATES[1:], dgates[1:]):
        dh = _mm(dgate, w[name], "nt", F32, name + "_dx", residual=dh)
    dh, others_in = _mm(dproj, w["w_in_a"], "nt", F32, "proj_att_dx", residual=dh, carry=_Scatter(sums_in))
    names = REDUCE_FFN + REDUCE_MIX + REDUCE_IN
    finals = [_add_chips(own, theirs, chip_core, "add_chips_" + name)
              for name, own, theirs in zip(names, sums_ffn + sums_mix + sums_in, others_ffn + others_mix + others_in)]
    grad_x, _, g["g_mix"], joined = _rms_bwd(dh, x, rstd1, w["g_mix"], dx1, "rms_mix_bwd", more=(dfb, w["w_in_f"]),
                                             carry=_Join(finals))
    summed = {name: a.reshape(2 * a.shape[1], a.shape[2]) for name, a in zip(names, joined)}
    return loss, grad_x, g, summed


SMALL = ("g_mix", "b_forget", "g_q_fox", "g_k_fox", "g_mem", "g_q_mem", "g_k_mem", "b_gate", "g_ffn", "conv_w",
         "conv_b")
SMALL_SHARDED = ("b_gate", "conv_w")
PACK_ROWS = 8


def _pack(arrs):
    flat = jnp.concatenate([a.reshape(-1) for a in arrs])
    unit = PACK_ROWS * LANES
    flat = jnp.pad(flat, (0, -flat.shape[0] % unit))
    return flat.reshape(-1, LANES)


def _unpack(packed, shapes):
    flat = packed.reshape(-1)
    out, at = [], 0
    for s in shapes:
        n = 1
        for dim in s:
            n *= dim
        out.append(flat[at:at + n].reshape(s))
        at += n
    return out


def kernel(x, mem, g_mix, w_in, b_forget, g_q_fox, g_k_fox, g_mem, w_mem_kv, g_q_mem, g_k_mem, w_br_fox, w_br_sb, w_br_mem, b_gate, w_out, g_ffn, w_up, conv_w, conv_b, w_down, loss_target, m_g_mix, m_w_in, m_b_forget, m_g_q_fox, m_g_k_fox, m_g_mem, m_w_mem_kv, m_g_q_mem, m_g_k_mem, m_w_br_fox, m_w_br_sb, m_w_br_mem, m_b_gate, m_w_out, m_g_ffn, m_w_up, m_conv_w, m_conv_b, m_w_down, v_g_mix, v_w_in, v_b_forget, v_g_q_fox, v_g_k_fox, v_g_mem, v_w_mem_kv, v_g_q_mem, v_g_k_mem, v_w_br_fox, v_w_br_sb, v_w_br_mem, v_b_gate, v_w_out, v_g_ffn, v_w_up, v_conv_w, v_conv_b, v_w_down):
    given = dict(g_mix=g_mix, w_in=w_in, b_forget=b_forget, g_q_fox=g_q_fox, g_k_fox=g_k_fox, g_mem=g_mem,
                 w_mem_kv=w_mem_kv, g_q_mem=g_q_mem, g_k_mem=g_k_mem, w_br_fox=w_br_fox, w_br_sb=w_br_sb,
                 w_br_mem=w_br_mem, b_gate=b_gate, w_out=w_out, g_ffn=g_ffn, w_up=w_up, conv_w=conv_w, conv_b=conv_b,
                 w_down=w_down)
    m_in = dict(g_mix=m_g_mix, w_in=m_w_in, b_forget=m_b_forget, g_q_fox=m_g_q_fox, g_k_fox=m_g_k_fox, g_mem=m_g_mem,
                w_mem_kv=m_w_mem_kv, g_q_mem=m_g_q_mem, g_k_mem=m_g_k_mem, w_br_fox=m_w_br_fox, w_br_sb=m_w_br_sb,
                w_br_mem=m_w_br_mem, b_gate=m_b_gate, w_out=m_w_out, g_ffn=m_g_ffn, w_up=m_w_up, conv_w=m_conv_w,
                conv_b=m_conv_b, w_down=m_w_down)
    v_in = dict(g_mix=v_g_mix, w_in=v_w_in, b_forget=v_b_forget, g_q_fox=v_g_q_fox, g_k_fox=v_g_k_fox, g_mem=v_g_mem,
                w_mem_kv=v_w_mem_kv, g_q_mem=v_g_q_mem, g_k_mem=v_g_k_mem, w_br_fox=v_w_br_fox, w_br_sb=v_w_br_sb,
                w_br_mem=v_w_br_mem, b_gate=v_b_gate, w_out=v_w_out, g_ffn=v_g_ffn, w_up=v_w_up, conv_w=v_conv_w,
                conv_b=v_conv_b, w_down=v_w_down)
    layered = {k: a.ndim == 3 for k, a in given.items()}
    drop = lambda a: a[0] if a.ndim == 3 else a
    given = {k: drop(a) for k, a in given.items()}
    m_in = {k: drop(a) for k, a in m_in.items()}
    v_in = {k: drop(a) for k, a in v_in.items()}

    xi, yi, ci = lax.axis_index("x"), lax.axis_index("y"), lax.axis_index("c")
    chip = (2 * xi + yi).astype(jnp.int32)
    core_arr = ci.astype(jnp.int32).reshape(1)
    chip_core = jnp.stack([chip, ci.astype(jnp.int32)])

    nf = given["b_forget"].shape[1]
    cut = 3 * given["w_br_fox"].shape[0]

    d_model = given["w_out"].shape[1]
    gate0 = given["w_in"].shape[1] - len(W_IN_GATES) * d_model
    shard = {
        "w_in_a": jnp.concatenate([given["w_in"][:, :cut], given["w_in"][:, cut + nf:gate0]], axis=1).astype(BF16),
        "w_in_f": jnp.pad(given["w_in"][:, cut:cut + nf], ((0, 0), (0, LANES - nf))).astype(BF16),
    }
    for b, name in enumerate(W_IN_GATES):
        shard[name] = given["w_in"][:, gate0 + b * d_model:gate0 + (b + 1) * d_model].astype(BF16)
    for name in BIG:
        if name not in shard:
            shard[name] = given[name].astype(BF16)
    w = {}
    small_shapes = [given[name].shape for name in SMALL_SHARDED]
    small_parts = _gather_small(_pack([given[name] for name in SMALL_SHARDED]))[0::2]
    per_chip = [_unpack(small_parts[j], small_shapes) for j in range(N_CHIPS)]
    for k, name in enumerate(SMALL_SHARDED):
        w[name] = jnp.concatenate([per_chip[j][k] for j in range(N_CHIPS)], axis=1)
    for name in SMALL:
        if name not in SMALL_SHARDED:
            w[name] = given[name]
    w["b_forget"] = jnp.pad(given["b_forget"], ((0, 0), (0, LANES - nf)))

    loss, grad_x, g, summed = _local_step(x[0], mem[0], loss_target[0], w, shard, core_arr, chip_core)
    loss = lax.psum(loss, ("x", "y", "c"))
    grads = {name: summed[name] for name in BIG if name in given}
    grads["w_in"] = jnp.concatenate([summed["w_in_a"][:, :cut], summed["w_in_f"][:, :nf], summed["w_in_a"][:, cut:]]
                                    + [summed[name] for name in W_IN_GATES], axis=1)

    g["b_forget"] = g["b_forget"][:, :nf]
    small_full_shapes = [g[name].shape for name in SMALL]
    small_sum = _unpack(_sum_devices(_gather_small(_pack([g[name] for name in SMALL]))), small_full_shapes)
    for name, a in zip(SMALL, small_sum):
        if name in SMALL_SHARDED:
            width = given[name].shape[1]
            a = lax.dynamic_slice_in_dim(a, chip * width, width, axis=1)
        grads[name] = a

    delta, new_m, new_v = {}, {}, {}
    for name in WEIGHTS:
        if name not in SMALL:
            delta[name], new_m[name], new_v[name] = _adamw(given[name], grads[name], m_in[name], v_in[name],
                                                           "adamw_" + name)
    shapes = [given[name].shape for name in SMALL]
    packed = [_pack([src[name] for name in SMALL]) for src in (given, grads, m_in, v_in)]
    for dst, res in zip((delta, new_m, new_v), _adamw(*packed, "adamw_small")):
        for name, a in zip(SMALL, _unpack(res, shapes)):
            dst[name] = a

    out = [loss, grad_x[None]]
    for src in (grads, delta, new_m, new_v):
        out.extend(src[name][None] if layered[name] else src[name] for name in WEIGHTS)
    return tuple(out)
```

```python
import functools

import jax
import jax.numpy as jnp
from jax import lax
from jax.experimental import pallas as pl
from jax.experimental.pallas import tpu as pltpu

F32 = jnp.float32
BF16 = jnp.bfloat16

HEAD_DIM = 128
EPS = 1e-6
NEG_BIG = -1e30

ADAM_LR = 0.001
ADAM_B1 = 0.9
ADAM_B2 = 0.999
ADAM_EPS = 1e-08
ADAM_WD = 0.01
ADAM_STEP = 10

LANES = 128
BF16_SUBLANES = 16
VMEM_LIMIT_BYTES = 60 * 1024 * 1024
MM_TILE = 1024
MM_TILE_K = {"nn": 2048, "nt": 2816, "tn": 4096}
ATT_TILE = 256
ROW_TILE = 256
HEADNORM_ROWS = 1024
MEM_ROWS = 1024
FORGET_ROWS = 512
COL_TILE = 1024
ADAM_BLOCK_BYTES = 2 << 20

N_CHIPS = 4
N_DEV = 8
MESH = pl.DeviceIdType.MESH

IN_NAMES = ['x', 'mem', 'g_mix', 'w_in', 'b_forget', 'g_q_fox', 'g_k_fox', 'g_mem', 'w_mem_kv', 'g_q_mem', 'g_k_mem',
            'w_br_fox', 'w_br_sb', 'w_br_mem', 'b_gate', 'w_out', 'g_ffn', 'w_up', 'conv_w', 'conv_b', 'w_down']
WEIGHTS = IN_NAMES[2:]


def _tile(n, target):
    if n <= target:
        return n
    for t in range(target - target % LANES, LANES - 1, -LANES):
        if n % t == 0:
            return t
    return n


def _params(*sem):
    return pltpu.CompilerParams(dimension_semantics=sem, vmem_limit_bytes=VMEM_LIMIT_BYTES)


def _log_sigmoid(z):
    return jnp.minimum(z, 0.0) - jnp.log(1.0 + jnp.exp(-jnp.abs(z)))


def _split2(v):
    hi = v.astype(BF16)
    lo = (v - hi.astype(F32)).astype(BF16)
    return hi, lo


def _split3(v):
    hi = v.astype(BF16)
    r = v - hi.astype(F32)
    mid = r.astype(BF16)
    lo = (r - mid.astype(F32)).astype(BF16)
    return hi, mid, lo


def _dot(a, b):
    return lax.dot_general(a, b, (((1,), (0,)), ((), ())), preferred_element_type=F32)


def _dot_nt(a, b):
    return lax.dot_general(a, b, (((1,), (1,)), ((), ())), preferred_element_type=F32)


def _dot_tn(a, b):
    return lax.dot_general(a, b, (((0,), (0,)), ((), ())), preferred_element_type=F32)


ANY = pl.BlockSpec(memory_space=pl.ANY)


def _place():
    x, y, c = lax.axis_index("x"), lax.axis_index("y"), lax.axis_index("c")
    others = [(1 - x, y), (x, 1 - y), (1 - x, 1 - y)]
    return x, y, c, others


def _remote(src, dst, send_sem, recv_sem, to):
    return pltpu.make_async_remote_copy(src_ref=src, dst_ref=dst, send_sem=send_sem, recv_sem=recv_sem,
                                        device_id=to, device_id_type=MESH)


class _Gather:
    PER_SHARD = 7

    def __init__(self, shards):
        self.inputs = list(shards)
        n = len(shards) * self.PER_SHARD
        self.out_shapes = [jax.ShapeDtypeStruct((N_CHIPS,) + s.shape, s.dtype) for s in shards]
        self.scratch = [pltpu.SemaphoreType.DMA((n,)), pltpu.SemaphoreType.DMA((n,))]

    def _first(self, ins, outs, sems):
        send_sems, recv_sems = sems
        x, y, c, others = _place()
        me = 2 * x + y
        k = self.PER_SHARD
        copies = []
        for i in range(len(ins)):
            h = ins[i].shape[0] // 2
            mine = pl.ds(pl.multiple_of(c * h, BF16_SUBLANES), h)
            for j, (ox, oy) in enumerate(others):
                copies.append(_remote(ins[i].at[mine], outs[i].at[me, mine], send_sems.at[k * i + j],
                                      recv_sems.at[k * i + j], (ox, oy, c)))
            copies.append(_remote(ins[i], outs[i].at[me], send_sems.at[k * i + 6], recv_sems.at[k * i + 6],
                                  (x, y, 1 - c)))
        return copies

    def start(self, ins, outs, sems):
        for cp in self._first(ins, outs, sems):
            cp.start()

    def finish(self, ins, outs, sems):
        send_sems, recv_sems = sems
        x, y, c, others = _place()
        me = 2 * x + y
        sibling = (x, y, 1 - c)
        k = self.PER_SHARD
        passed = []
        for i in range(len(ins)):
            h = ins[i].shape[0] // 2
            mine = pl.ds(pl.multiple_of(c * h, BF16_SUBLANES), h)
            for j, (ox, oy) in enumerate(others):
                blk = outs[i].at[2 * ox + oy, mine]
                _remote(blk, blk, send_sems.at[k * i + j], recv_sems.at[k * i + j], (ox, oy, c)).wait_recv()
                cp = _remote(blk, blk, send_sems.at[k * i + 3 + j], recv_sems.at[k * i + 3 + j], sibling)
                cp.start()
                passed.append(cp)
        for i in range(len(ins)):
            h = ins[i].shape[0] // 2
            theirs = pl.ds(pl.multiple_of((1 - c) * h, BF16_SUBLANES), h)
            for j, (ox, oy) in enumerate(others):
                blk = outs[i].at[2 * ox + oy, theirs]
                _remote(blk, blk, send_sems.at[k * i + 3 + j], recv_sems.at[k * i + 3 + j], sibling).wait_recv()
            own = outs[i].at[me]
            _remote(own, own, send_sems.at[k * i + 6], recv_sems.at[k * i + 6], sibling).wait_recv()
        for cp in self._first(ins, outs, sems) + passed:
            cp.wait_send()


class _Scatter:
    def __init__(self, sums):
        self.inputs = list(sums)
        k = N_CHIPS - 1
        self.out_shapes = [jax.ShapeDtypeStruct((k,) + g.shape[1:], g.dtype) for g in sums]
        self.scratch = [pltpu.SemaphoreType.DMA((k * len(sums),)), pltpu.SemaphoreType.DMA((k * len(sums),))]

    def _copies(self, ins, outs, sems):
        send_sems, recv_sems = sems
        _, _, c, others = _place()
        k = N_CHIPS - 1
        return [_remote(ins[i].at[2 * ox + oy], outs[i].at[j], send_sems.at[k * i + j], recv_sems.at[k * i + j],
                        (ox, oy, c))
                for i in range(len(ins)) for j, (ox, oy) in enumerate(others)]

    def start(self, ins, outs, sems):
        for cp in self._copies(ins, outs, sems):
            cp.start()

    def finish(self, ins, outs, sems):
        for cp in self._copies(ins, outs, sems):
            cp.wait()


class _Swap:
    def __init__(self, grads):
        self.inputs = list(grads)
        n = len(grads)
        self.out_shapes = [jax.ShapeDtypeStruct((g.shape[0],) + g.shape[2:], g.dtype) for g in grads]
        self.scratch = [pltpu.SemaphoreType.DMA((n,)), pltpu.SemaphoreType.DMA((n,))]

    def _copies(self, ins, outs, sems):
        send_sems, recv_sems = sems
        x, y, c, _ = _place()
        return [_remote(ins[i].at[:, 1 - c], outs[i], send_sems.at[i], recv_sems.at[i], (x, y, 1 - c))
                for i in range(len(ins))]

    def start(self, ins, outs, sems):
        for cp in self._copies(ins, outs, sems):
            cp.start()

    def finish(self, ins, outs, sems):
        for cp in self._copies(ins, outs, sems):
            cp.wait()


class _Join:
    def __init__(self, finals):
        self.inputs = list(finals)
        n = len(finals)
        self.out_shapes = [jax.ShapeDtypeStruct(f.shape, f.dtype) for f in finals]
        self.scratch = [pltpu.SemaphoreType.DMA((n,)), pltpu.SemaphoreType.DMA((n,))]
        self.aliases = {i: i for i in range(n)}

    def _sends(self, outs, sems):
        send_sems, recv_sems = sems
        x, y, c, _ = _place()
        return [_remote(outs[i].at[c], outs[i].at[c], send_sems.at[i], recv_sems.at[i], (x, y, 1 - c))
                for i in range(len(outs))]

    def start(self, ins, outs, sems):
        for cp in self._sends(outs, sems):
            cp.start()

    def finish(self, ins, outs, sems):
        send_sems, recv_sems = sems
        x, y, c, _ = _place()
        for i, cp in enumerate(self._sends(outs, sems)):
            cp.wait_send()
            other = outs[i].at[1 - c]
            _remote(other, other, send_sems.at[i], recv_sems.at[i], (x, y, 1 - c)).wait_recv()


class _Both:
    def __init__(self, first, second):
        self.parts = (first, second)
        self.inputs = first.inputs + second.inputs
        self.out_shapes = first.out_shapes + second.out_shapes
        self.scratch = first.scratch + second.scratch

    def _each(self, ins, outs, sems):
        first = self.parts[0]
        a, b, c = len(first.inputs), len(first.out_shapes), len(first.scratch)
        return ((first, ins[:a], outs[:b], sems[:c]), (self.parts[1], ins[a:], outs[b:], sems[c:]))

    def start(self, ins, outs, sems):
        for part, i, o, s in self._each(ins, outs, sems):
            part.start(i, o, s)

    def finish(self, ins, outs, sems):
        for part, i, o, s in self._each(ins, outs, sems):
            part.finish(i, o, s)


def _call(body, *, grid, in_specs, out_specs, out_shape, scratch_shapes, semantics, name, args, carry=None):
    n_in, n_out, n_scr = len(in_specs), len(out_specs), len(scratch_shapes)
    if carry is None:
        res = pl.pallas_call(body, grid=grid, in_specs=in_specs, out_specs=out_specs, out_shape=out_shape,
                             scratch_shapes=scratch_shapes, compiler_params=_params(*semantics), name=name)(*args)
        return list(res), []
    nci, nco = len(carry.inputs), len(carry.out_shapes)
    a, b = n_in, n_in + nci
    c, d = b + n_out, b + n_out + nco
    e = d + n_scr

    def carried(*refs):
        ids = [pl.program_id(k) for k in range(len(grid))]
        first = functools.reduce(jnp.logical_and, [i == 0 for i in ids])
        last = functools.reduce(jnp.logical_and, [i == n - 1 for i, n in zip(ids, grid)])

        @pl.when(first)
        def _():
            carry.start(refs[a:b], refs[c:d], refs[e:])

        body(*refs[:a], *refs[b:c], *refs[d:e])

        @pl.when(last)
        def _():
            carry.finish(refs[a:b], refs[c:d], refs[e:])

    res = pl.pallas_call(
        carried,
        grid=grid,
        in_specs=list(in_specs) + [ANY] * nci,
        out_specs=list(out_specs) + [ANY] * nco,
        out_shape=list(out_shape) + carry.out_shapes,
        scratch_shapes=list(scratch_shapes) + carry.scratch,
        input_output_aliases={n_in + i: n_out + o for i, o in getattr(carry, "aliases", {}).items()},
        compiler_params=_params(*(["arbitrary"] * len(grid))),
        name=name,
    )(*args, *carry.inputs)
    return list(res[:n_out]), list(res[n_out:])


def _mm(a, b, mode, out_dtype, name, residual=None, carry=None, loss_target=None):
    if mode == "nn":
        (m, k), (k2, n) = a.shape, b.shape
    elif mode == "nt":
        (m, k), (n, k2) = a.shape, b.shape
    else:
        (k, m), (k2, n) = a.shape, b.shape
    assert k == k2, (a.shape, b.shape, mode)
    has_res = residual is not None
    has_loss = loss_target is not None
    n_in = 2 + has_res + has_loss
    tm, tn, tk = _tile(m, MM_TILE), _tile(n, MM_TILE), _tile(k, MM_TILE_K[mode])
    nk = k // tk
    dot = {"nn": _dot, "nt": _dot_nt, "tn": _dot_tn}[mode]

    def body(*refs):
        a_ref, b_ref = refs[:2]
        r_ref = refs[2] if has_res else None
        t_ref = refs[n_in - 1] if has_loss else None
        o_ref = refs[n_in]

        def finish(acc):
            if has_res:
                acc = acc + r_ref[...]
            if has_loss:
                err = acc - t_ref[...]
                dy = err * (1.0 / n)
                o_ref[...] = dy
                refs[n_in + 1][...] = dy.astype(BF16)
                tot = jnp.sum(jnp.sum(err * err, axis=-1, keepdims=True), axis=0, keepdims=True)
                refs[n_in + 2][...] = jnp.broadcast_to(tot, (8, LANES))
            else:
                o_ref[...] = acc.astype(o_ref.dtype)

        part = dot(a_ref[...], b_ref[...])
        if nk == 1:
            finish(part)
        else:
            acc_ref = refs[-1]
            kk = pl.program_id(2)

            @pl.when(kk == 0)
            def _():
                acc_ref[...] = part

            @pl.when(kk > 0)
            def _():
                acc_ref[...] += part

            @pl.when(kk == nk - 1)
            def _():
                finish(acc_ref[...])

    if mode == "tn":
        a_spec = pl.BlockSpec((tk, tm), lambda j, i, kk: (kk, i))
    else:
        a_spec = pl.BlockSpec((tm, tk), lambda j, i, kk: (i, kk))
    if mode == "nt":
        b_spec = pl.BlockSpec((tn, tk), lambda j, i, kk: (j, kk))
    else:
        b_spec = pl.BlockSpec((tk, tn), lambda j, i, kk: (kk, j))
    o_spec = pl.BlockSpec((tm, tn), lambda j, i, kk: (i, j))
    in_specs = [a_spec, b_spec] + [o_spec] * (has_res + has_loss)
    args = (a, b) + ((residual,) if has_res else ()) + ((loss_target,) if has_loss else ())
    out_specs, out_shape = [o_spec], [jax.ShapeDtypeStruct((m, n), out_dtype)]
    if has_loss:
        out_specs += [o_spec, pl.BlockSpec((8, LANES), lambda j, i, kk: (i, j))]
        out_shape += [jax.ShapeDtypeStruct((m, n), BF16), jax.ShapeDtypeStruct((m // tm * 8, n // tn * LANES), F32)]
    outs, moved = _call(
        body,
        grid=(n // tn, m // tm, nk),
        in_specs=in_specs,
        out_specs=out_specs,
        out_shape=out_shape,
        scratch_shapes=[pltpu.VMEM((tm, tn), F32)] if nk > 1 else [],
        semantics=("parallel", "parallel", "arbitrary"),
        name=name,
        args=args,
        carry=carry,
    )
    out = outs if has_loss else outs[0]
    return out if carry is None else (out, moved)


def _rms_fwd(x, g, name, carry=None):
    s, d = x.shape
    tm = _tile(s, ROW_TILE)

    def body(x_ref, g_ref, h_ref, r_ref):
        xf = x_ref[...]
        r = lax.rsqrt(jnp.mean(xf * xf, axis=-1, keepdims=True) + EPS)
        h_ref[...] = ((xf * r) * g_ref[...]).astype(BF16)
        r_ref[...] = r

    (h, rstd), moved = _call(
        body,
        grid=(s // tm,),
        in_specs=[pl.BlockSpec((tm, d), lambda i: (i, 0)), pl.BlockSpec((1, d), lambda i: (0, 0))],
        out_specs=[pl.BlockSpec((tm, d), lambda i: (i, 0)), pl.BlockSpec((tm, 1), lambda i: (i, 0))],
        out_shape=[jax.ShapeDtypeStruct((s, d), BF16), jax.ShapeDtypeStruct((s, 1), F32)],
        scratch_shapes=[],
        semantics=("parallel",),
        name=name,
        args=(x, g),
        carry=carry,
    )
    return (h, rstd) if carry is None else (h, rstd, moved)


def _rms_bwd(dh, x, rstd, g, res, name, more=None, carry=None):
    s, d = x.shape
    tm = _tile(s, ROW_TILE)
    has_res = res is not None
    has_more = more is not None
    n_in = 4 + has_res + 2 * has_more

    def body(*refs):
        dh_ref, x_ref, r_ref, g_ref = refs[:4]
        res_ref = refs[4] if has_res else None
        dx_ref, dxb_ref, dg_ref = refs[n_in:]
        dhf = dh_ref[...].astype(F32)
        if has_more:
            dhf = dhf + _dot_nt(refs[n_in - 2][...], refs[n_in - 1][...])
        xhat = x_ref[...] * r_ref[...]
        dy = dhf * g_ref[...]
        dx = r_ref[...] * (dy - xhat * jnp.mean(dy * xhat, axis=-1, keepdims=True))
        if has_res:
            dx = dx + res_ref[...]
        dx_ref[...] = dx
        dxb_ref[...] = dx.astype(BF16)
        part = jnp.sum(dhf * xhat, axis=0, keepdims=True)

        @pl.when(pl.program_id(0) == 0)
        def _():
            dg_ref[...] = part

        @pl.when(pl.program_id(0) > 0)
        def _():
            dg_ref[...] += part

    row = pl.BlockSpec((tm, d), lambda i: (i, 0))
    vec = pl.BlockSpec((1, d), lambda i: (0, 0))
    in_specs = [row, row, pl.BlockSpec((tm, 1), lambda i: (i, 0)), vec] + ([row] if has_res else [])
    args = (dh, x, rstd, g) + ((res,) if has_res else ())
    if has_more:
        k = more[0].shape[1]
        in_specs += [pl.BlockSpec((tm, k), lambda i: (i, 0)), pl.BlockSpec((d, k), lambda i: (0, 0))]
        args += tuple(more)
    (dx, dxb, dg), moved = _call(
        body,
        grid=(s // tm,),
        in_specs=in_specs,
        out_specs=[row, row, vec],
        out_shape=[jax.ShapeDtypeStruct((s, d), F32), jax.ShapeDtypeStruct((s, d), BF16),
                   jax.ShapeDtypeStruct((1, d), F32)],
        scratch_shapes=[],
        semantics=("arbitrary",),
        name=name,
        args=args,
        carry=carry,
    )
    return (dx, dxb, dg) if carry is None else (dx, dxb, dg, moved)


def _headnorm_fwd(src, col0, nheads, g, name):
    s = src.shape[0]
    tm = _tile(s, HEADNORM_ROWS)
    w = nheads * HEAD_DIM
    assert col0 % nheads == 0

    def body(x_ref, g_ref, o_ref):
        for hh in range(nheads):
            xf = _head(x_ref, hh).astype(F32)
            r = lax.rsqrt(jnp.mean(xf * xf, axis=-1, keepdims=True) + EPS)
            o_ref[:, hh * HEAD_DIM:(hh + 1) * HEAD_DIM] = ((xf * r) * g_ref[...]).astype(BF16)

    return pl.pallas_call(
        body,
        grid=(s // tm,),
        in_specs=[pl.BlockSpec((tm, w), lambda i: (i, col0 // nheads)),
                  pl.BlockSpec((1, HEAD_DIM), lambda i: (0, 0))],
        out_specs=pl.BlockSpec((tm, w), lambda i: (i, 0)),
        out_shape=jax.ShapeDtypeStruct((s, w), BF16),
        compiler_params=_params("parallel"),
        name=name,
    )(src, g)


def _headnorm_bwd(dxn, src, col0, nheads, g, name):
    s = src.shape[0]
    tm = _tile(s, HEADNORM_ROWS)
    w = nheads * HEAD_DIM
    assert col0 % nheads == 0

    def body(d_ref, x_ref, g_ref, dx_ref, dg_ref):
        part = jnp.zeros((1, HEAD_DIM), F32)
        for hh in range(nheads):
            xf = _head(x_ref, hh).astype(F32)
            r = lax.rsqrt(jnp.mean(xf * xf, axis=-1, keepdims=True) + EPS)
            xhat = xf * r
            dn = _head(d_ref, hh).astype(F32)
            dy = dn * g_ref[...]
            dx = r * (dy - xhat * jnp.mean(dy * xhat, axis=-1, keepdims=True))
            dx_ref[:, hh * HEAD_DIM:(hh + 1) * HEAD_DIM] = dx.astype(BF16)
            part = part + jnp.sum(dn * xhat, axis=0, keepdims=True)

        @pl.when(pl.program_id(0) == 0)
        def _():
            dg_ref[...] = part

        @pl.when(pl.program_id(0) > 0)
        def _():
            dg_ref[...] += part

    return pl.pallas_call(
        body,
        grid=(s // tm,),
        in_specs=[pl.BlockSpec((tm, w), lambda i: (i, 0)),
                  pl.BlockSpec((tm, w), lambda i: (i, col0 // nheads)),
                  pl.BlockSpec((1, HEAD_DIM), lambda i: (0, 0))],
        out_specs=[pl.BlockSpec((tm, w), lambda i: (i, 0)),
                   pl.BlockSpec((1, HEAD_DIM), lambda i: (0, 0))],
        out_shape=[jax.ShapeDtypeStruct((s, w), BF16), jax.ShapeDtypeStruct((1, HEAD_DIM), F32)],
        compiler_params=_params("arbitrary"),
        name=name,
    )(dxn, src, g)


def _tri(t, lower_inclusive):
    r = lax.broadcasted_iota(jnp.int32, (t, t), 0)
    c = lax.broadcasted_iota(jnp.int32, (t, t), 1)
    keep = (c <= r) if lower_inclusive else (c >= r)
    return jnp.where(keep, 1.0, 0.0).astype(BF16)


def _forget_fwd(h, w_f, b_pad):
    s, d = h.shape
    t = _tile(s, FORGET_ROWS)

    def body(h_ref, w_ref, b_ref, f_ref, c_ref, carry):
        @pl.when(pl.program_id(0) == 0)
        def _():
            carry[...] = jnp.zeros_like(carry)

        f = _dot(h_ref[...], w_ref[...])
        f_ref[...] = f
        lf = _log_sigmoid(f + b_ref[...])
        tri = _tri(t, True)
        acc = carry[...]
        for part in _split3(lf):
            acc = acc + _dot(tri, part)
        c_ref[...] = acc
        carry[...] += jnp.sum(lf, axis=0, keepdims=True)

    blk = pl.BlockSpec((t, LANES), lambda i: (i, 0))
    return pl.pallas_call(
        body,
        grid=(s // t,),
        in_specs=[pl.BlockSpec((t, d), lambda i: (i, 0)), pl.BlockSpec((d, LANES), lambda i: (0, 0)),
                  pl.BlockSpec((1, LANES), lambda i: (0, 0))],
        out_specs=[blk, blk],
        out_shape=[jax.ShapeDtypeStruct((s, LANES), F32)] * 2,
        scratch_shapes=[pltpu.VMEM((1, LANES), F32)],
        compiler_params=_params("arbitrary"),
        name="forget_fwd",
    )(h, w_f, b_pad)


def _forget_bwd(dc, f_logit, b_pad):
    s = f_logit.shape[0]
    t = _tile(s, FORGET_ROWS)
    nb = s // t

    def body(dc_ref, f_ref, b_ref, df_ref, db_ref, carry):
        @pl.when(pl.program_id(0) == 0)
        def _():
            carry[...] = jnp.zeros_like(carry)
            db_ref[...] = jnp.zeros_like(db_ref)

        d = dc_ref[...]
        tri = _tri(t, False)
        acc = carry[...]
        for part in _split3(d):
            acc = acc + _dot(tri, part)
        z = f_ref[...] + b_ref[...]
        df = acc * jnp.exp(_log_sigmoid(-z))
        df_ref[...] = df
        db_ref[...] += jnp.sum(df, axis=0, keepdims=True)
        carry[...] += jnp.sum(d, axis=0, keepdims=True)

    rev = pl.BlockSpec((t, LANES), lambda i: (nb - 1 - i, 0))
    vec = pl.BlockSpec((1, LANES), lambda i: (0, 0))
    return pl.pallas_call(
        body,
        grid=(nb,),
        in_specs=[rev, rev, vec],
        out_specs=[rev, vec],
        out_shape=[jax.ShapeDtypeStruct((s, LANES), F32), jax.ShapeDtypeStruct((1, LANES), F32)],
        scratch_shapes=[pltpu.VMEM((1, LANES), F32)],
        compiler_params=_params("arbitrary"),
        name="forget_bwd",
    )(dc, f_logit, b_pad)


SB_FWD_GROUP = 6
FOX_GROUP = 6
SB_BWD_GROUP = 6


def _head(ref, hh, rows=slice(None)):
    return ref[rows, hh * HEAD_DIM:(hh + 1) * HEAD_DIM]


def _tri_mask(t, strict):
    r = lax.broadcasted_iota(jnp.int32, (t, t), 0)
    c = lax.broadcasted_iota(jnp.int32, (t, t), 1)
    return (c < r) if strict else (c <= r)


def _fox_fwd(qn, kn, proj, colv, c_col, c_row, nheads, carry=None):
    s = qn.shape[0]
    t = _tile(s, ATT_TILE)
    scale = HEAD_DIM ** -0.5
    hg = FOX_GROUP
    gw = hg * HEAD_DIM
    assert nheads % hg == 0 and colv % hg == 0

    def body(q_ref, k_ref, v_ref, cc_ref, cr_ref, o_ref, of_ref, lse_ref):
        qi = pl.program_id(1)
        causal = _tri_mask(t, False)

        def tile(kj, carry, diagonal):
            off = pl.multiple_of(kj * t, t)
            heads = range(hg)
            rows = pl.ds(off, t)
            qk = [_dot_nt(_head(q_ref, hh), _head(k_ref, hh, rows)) for hh in heads]
            sc = [qk[hh] * scale + (cc_ref[hh] - cr_ref[hh, :, rows]) for hh in heads]
            if diagonal:
                sc = [jnp.where(causal, sc[hh], NEG_BIG) for hh in heads]
            m_new = [jnp.maximum(carry[hh][0], jnp.max(sc[hh], axis=-1, keepdims=True)) for hh in heads]
            p = [jnp.exp(sc[hh] - m_new[hh]) for hh in heads]
            pv = [_dot(p[hh].astype(BF16), _head(v_ref, hh, rows)) for hh in heads]
            out = []
            for hh in heads:
                m, l, acc = carry[hh]
                alpha = jnp.exp(m - m_new[hh])
                out.append((m_new[hh], alpha * l + jnp.sum(p[hh], axis=-1, keepdims=True), alpha * acc + pv[hh]))
            return tuple(out)

        init = tuple((jnp.full((t, 1), NEG_BIG, F32), jnp.zeros((t, 1), F32), jnp.zeros((t, HEAD_DIM), F32))
                     for _ in range(hg))
        carry = lax.fori_loop(0, qi, lambda kj, c: tile(kj, c, False), init)
        carry = tile(qi, carry, True)
        for hh in range(hg):
            m, l, acc = carry[hh]
            o = acc / l
            of_ref[:, hh * HEAD_DIM:(hh + 1) * HEAD_DIM] = o
            o_ref[:, hh * HEAD_DIM:(hh + 1) * HEAD_DIM] = o.astype(BF16)
            lse_ref[hh] = m + jnp.log(l)

    tile_spec = pl.BlockSpec((t, gw), lambda h, i: (i, h))
    w = nheads * HEAD_DIM
    return _call(
        body,
        grid=(nheads // hg, s // t),
        in_specs=[tile_spec,
                  pl.BlockSpec((s, gw), lambda h, i: (0, h), pipeline_mode=pl.Buffered(buffer_count=1)),
                  pl.BlockSpec((s, gw), lambda h, i: (0, colv // hg + h), pipeline_mode=pl.Buffered(buffer_count=1)),
                  pl.BlockSpec((hg, t, 1), lambda h, i: (h, i, 0)),
                  pl.BlockSpec((hg, 1, s), lambda h, i: (h, 0, 0))],
        out_specs=[tile_spec, tile_spec, pl.BlockSpec((hg, t, 1), lambda h, i: (h, i, 0))],
        out_shape=[jax.ShapeDtypeStruct((s, w), BF16), jax.ShapeDtypeStruct((s, w), F32),
                   jax.ShapeDtypeStruct((nheads, s, 1), F32)],
        scratch_shapes=[],
        semantics=("parallel", "parallel"),
        name="fox_fwd",
        args=(qn, kn, proj, c_col, c_row),
        carry=carry,
    )


def _fox_bwd(qn, kn, proj, colv, c_col, c_row, o, do, lse, nheads, carry=None):
    s = qn.shape[0]
    t = _tile(s, ATT_TILE)
    scale = HEAD_DIM ** -0.5
    hg = FOX_GROUP
    gw = hg * HEAD_DIM
    assert nheads % hg == 0 and colv % hg == 0

    def body(q_ref, k_ref, v_ref, cc_ref, cr_ref, o_ref, do_ref, lse_ref,
             dq_ref, dk_ref, dv_ref, drs_ref, dcs_ref):
        qi = pl.program_id(1)

        @pl.when(qi == 0)
        def _():
            dk_ref[...] = jnp.zeros_like(dk_ref)
            dv_ref[...] = jnp.zeros_like(dv_ref)
            dcs_ref[...] = jnp.zeros_like(dcs_ref)

        causal = _tri_mask(t, False)
        delta = [jnp.sum(_head(o_ref, hh) * _head(do_ref, hh).astype(F32), axis=-1, keepdims=True)
                 for hh in range(hg)]

        def tile(kj, carry, diagonal):
            off = pl.multiple_of(kj * t, t)
            heads = range(hg)
            rows = pl.ds(off, t)
            qk = [_dot_nt(_head(q_ref, hh), _head(k_ref, hh, rows)) for hh in heads]
            dp = [_dot_nt(_head(do_ref, hh), _head(v_ref, hh, rows)) for hh in heads]
            p = [jnp.exp(qk[hh] * scale + (cc_ref[hh] - cr_ref[hh, :, rows]) - lse_ref[hh]) for hh in heads]
            if diagonal:
                p = [jnp.where(causal, p[hh], 0.0) for hh in heads]
            ds = [p[hh] * (dp[hh] - delta[hh]) for hh in heads]
            dsb = [ds[hh].astype(BF16) for hh in heads]
            dv = [_dot_tn(p[hh].astype(BF16), _head(do_ref, hh)) for hh in heads]
            dk = [_dot_tn(dsb[hh], _head(q_ref, hh)) * scale for hh in heads]
            dq = [_dot(dsb[hh], _head(k_ref, hh, rows)) * scale for hh in heads]
            for hh in heads:
                cols = slice(hh * HEAD_DIM, (hh + 1) * HEAD_DIM)
                dv_ref[rows, cols] += dv[hh]
                dk_ref[rows, cols] += dk[hh]
                dcs_ref[hh, :, rows] += jnp.sum(ds[hh], axis=0, keepdims=True)
            return tuple((carry[hh][0] + dq[hh], carry[hh][1] + jnp.sum(ds[hh], axis=-1, keepdims=True))
                         for hh in heads)

        init = tuple((jnp.zeros((t, HEAD_DIM), F32), jnp.zeros((t, 1), F32)) for _ in range(hg))
        carry = lax.fori_loop(0, qi, lambda kj, c: tile(kj, c, False), init)
        carry = tile(qi, carry, True)
        for hh in range(hg):
            dq_ref[:, hh * HEAD_DIM:(hh + 1) * HEAD_DIM] = carry[hh][0]
            drs_ref[hh] = carry[hh][1]

    tile_spec = pl.BlockSpec((t, gw), lambda h, i: (i, h))
    once = pl.Buffered(buffer_count=1)
    full = pl.BlockSpec((s, gw), lambda h, i: (0, h), pipeline_mode=once)
    colspec = pl.BlockSpec((hg, t, 1), lambda h, i: (h, i, 0))
    rowspec = pl.BlockSpec((hg, 1, s), lambda h, i: (h, 0, 0))
    w = nheads * HEAD_DIM
    return _call(
        body,
        grid=(nheads // hg, s // t),
        in_specs=[tile_spec, full, pl.BlockSpec((s, gw), lambda h, i: (0, colv // hg + h), pipeline_mode=once),
                  colspec, rowspec,
                  tile_spec, tile_spec, colspec],
        out_specs=[tile_spec, full, full, colspec, rowspec],
        out_shape=[jax.ShapeDtypeStruct((s, w), F32), jax.ShapeDtypeStruct((s, w), F32),
                   jax.ShapeDtypeStruct((s, w), F32), jax.ShapeDtypeStruct((nheads, s, 1), F32),
                   jax.ShapeDtypeStruct((nheads, 1, s), F32)],
        scratch_shapes=[],
        semantics=("arbitrary", "arbitrary"),
        name="fox_bwd",
        args=(qn, kn, proj, c_col, c_row, o, do, lse),
        carry=carry,
    )


def _sb_tile(q, k, scale, later, valid):
    z = _dot_nt(q, k) * scale
    lb = _log_sigmoid(z)
    lm = lb - z
    if valid is not None:
        lm = jnp.where(valid, lm, 0.0)
    suffix = _dot(jnp.concatenate(_split2(lm), axis=1), later)
    return lb, lm, suffix


def _later(t):
    r = lax.broadcasted_iota(jnp.int32, (2 * t, t), 0) % t
    c = lax.broadcasted_iota(jnp.int32, (2 * t, t), 1)
    return jnp.where(r > c, 1.0, 0.0).astype(BF16)


def _sb_fwd(proj, colq, colk, colv, nheads, carry=None):
    s = proj.shape[0]
    t = _tile(s, ATT_TILE)
    scale = HEAD_DIM ** -0.5
    hg = SB_FWD_GROUP
    gw = hg * HEAD_DIM
    assert nheads % hg == 0 and colq % hg == 0 and colk % hg == 0 and colv % hg == 0

    def body(q_ref, k_ref, v_ref, o_ref, tot_ref):
        qi = pl.program_id(1)
        later = _later(t)
        before = _tri_mask(t, True)

        def tile(kj, carry, diagonal):
            off = pl.multiple_of(kj * t, t)
            heads = range(hg)
            z = [_dot_nt(_head(q_ref, hh), _head(k_ref, hh, pl.ds(off, t))) * scale for hh in heads]
            lb = [_log_sigmoid(z[hh]) for hh in heads]
            lm = [lb[hh] - z[hh] for hh in heads]
            if diagonal:
                lm = [jnp.where(before, lm[hh], 0.0) for hh in heads]
            parts = [jnp.concatenate(_split2(lm[hh]), axis=1) for hh in heads]
            suffix = [_dot(parts[hh], later) for hh in heads]
            a = [jnp.exp(lb[hh] + suffix[hh] + carry[hh][0]) for hh in heads]
            if diagonal:
                a = [jnp.where(before, a[hh], 0.0) for hh in heads]
            av = [_dot(a[hh].astype(BF16), _head(v_ref, hh, pl.ds(off, t))) for hh in heads]
            return tuple((carry[hh][0] + jnp.sum(lm[hh], axis=-1, keepdims=True), carry[hh][1] + av[hh])
                         for hh in heads)

        init = tuple((jnp.zeros((t, 1), F32), jnp.zeros((t, HEAD_DIM), F32)) for _ in range(hg))
        carry = tile(qi, init, True)
        carry = lax.fori_loop(1, qi + 1, lambda i, c: tile(qi - i, c, False), carry)
        for hh in range(hg):
            o_ref[:, hh * HEAD_DIM:(hh + 1) * HEAD_DIM] = carry[hh][1].astype(BF16)
            tot_ref[hh] = carry[hh][0]

    return _call(
        body,
        grid=(nheads // hg, s // t),
        in_specs=[pl.BlockSpec((t, gw), lambda h, i: (i, colq // hg + h)),
                  pl.BlockSpec((s, gw), lambda h, i: (0, colk // hg + h), pipeline_mode=pl.Buffered(buffer_count=1)),
                  pl.BlockSpec((s, gw), lambda h, i: (0, colv // hg + h), pipeline_mode=pl.Buffered(buffer_count=1))],
        out_specs=[pl.BlockSpec((t, gw), lambda h, i: (i, h)), pl.BlockSpec((hg, t, 1), lambda h, i: (h, i, 0))],
        out_shape=[jax.ShapeDtypeStruct((s, nheads * HEAD_DIM), BF16), jax.ShapeDtypeStruct((nheads, s, 1), F32)],
        scratch_shapes=[],
        semantics=("parallel", "parallel"),
        name="sb_fwd",
        args=(proj, proj, proj),
        carry=carry,
    )


def _sb_bwd(proj, colq, colk, colv, do, tot, nheads, carry=None):
    s = proj.shape[0]
    t = _tile(s, ATT_TILE)
    scale = HEAD_DIM ** -0.5
    hg = SB_BWD_GROUP
    gw = hg * HEAD_DIM
    assert nheads % hg == 0 and colq % hg == 0 and colk % hg == 0 and colv % hg == 0

    def body(q_ref, k_ref, v_ref, do_ref, tot_ref, dq_ref, dk_ref, dv_ref):
        qi = pl.program_id(1)

        @pl.when(qi == 0)
        def _():
            dk_ref[...] = jnp.zeros_like(dk_ref)
            dv_ref[...] = jnp.zeros_like(dv_ref)

        key = lax.broadcasted_iota(jnp.int32, (2 * t, t), 0) % t
        col = lax.broadcasted_iota(jnp.int32, (2 * t, t), 1)
        upto = jnp.where(key <= col, 1.0, 0.0).astype(BF16)
        earlier = jnp.where(key < col, 1.0, 0.0).astype(BF16)
        before = _tri_mask(t, True)

        def tile(kj, carry, diagonal):
            off = pl.multiple_of(kj * t, t)
            heads = range(hg)
            rows = pl.ds(off, t)
            z = [_dot_nt(_head(q_ref, hh), _head(k_ref, hh, rows)) * scale for hh in heads]
            da = [_dot_nt(_head(do_ref, hh), _head(v_ref, hh, rows)) for hh in heads]
            lb = [_log_sigmoid(z[hh]) for hh in heads]
            lm = [lb[hh] - z[hh] for hh in heads]
            if diagonal:
                lm = [jnp.where(before, lm[hh], 0.0) for hh in heads]
            parts = [jnp.concatenate(_split2(lm[hh]), axis=1) for hh in heads]
            seen = [_dot(parts[hh], upto) + carry[hh][0] for hh in heads]
            a = [jnp.exp(lb[hh] + (tot_ref[hh] - seen[hh])) for hh in heads]
            if diagonal:
                a = [jnp.where(before, a[hh], 0.0) for hh in heads]
            g = [a[hh] * da[hh] for hh in heads]
            gparts = [jnp.concatenate(_split2(g[hh]), axis=1) for hh in heads]
            gsum = [_dot(gparts[hh], earlier) + carry[hh][1] for hh in heads]
            dz = []
            for hh in heads:
                beta = jnp.exp(lb[hh])
                d = g[hh] * (1.0 - beta) - gsum[hh] * beta
                if diagonal:
                    d = jnp.where(before, d, 0.0)
                dz.append(d.astype(BF16))
            dv = [_dot_tn(a[hh].astype(BF16), _head(do_ref, hh)) for hh in heads]
            dk = [_dot_tn(dz[hh], _head(q_ref, hh)) * scale for hh in heads]
            dq = [_dot(dz[hh], _head(k_ref, hh, rows)) * scale for hh in heads]
            for hh in heads:
                cols = slice(hh * HEAD_DIM, (hh + 1) * HEAD_DIM)
                dv_ref[rows, cols] += dv[hh]
                dk_ref[rows, cols] += dk[hh]
            return tuple((carry[hh][0] + jnp.sum(lm[hh], axis=-1, keepdims=True),
                          carry[hh][1] + jnp.sum(g[hh], axis=-1, keepdims=True), carry[hh][2] + dq[hh])
                         for hh in heads)

        init = tuple((jnp.zeros((t, 1), F32), jnp.zeros((t, 1), F32), jnp.zeros((t, HEAD_DIM), F32))
                     for _ in range(hg))
        carry = lax.fori_loop(0, qi, lambda kj, c: tile(kj, c, False), init)
        carry = tile(qi, carry, True)
        for hh in range(hg):
            dq_ref[:, hh * HEAD_DIM:(hh + 1) * HEAD_DIM] = carry[hh][2]

    once = pl.Buffered(buffer_count=1)
    tile_spec = pl.BlockSpec((t, gw), lambda h, i: (i, h))
    full = pl.BlockSpec((s, gw), lambda h, i: (0, h), pipeline_mode=once)
    w = nheads * HEAD_DIM
    return _call(
        body,
        grid=(nheads // hg, s // t),
        in_specs=[pl.BlockSpec((t, gw), lambda h, i: (i, colq // hg + h)),
                  pl.BlockSpec((s, gw), lambda h, i: (0, colk // hg + h), pipeline_mode=once),
                  pl.BlockSpec((s, gw), lambda h, i: (0, colv // hg + h), pipeline_mode=once),
                  tile_spec, pl.BlockSpec((hg, t, 1), lambda h, i: (h, i, 0))],
        out_specs=[tile_spec, full, full],
        out_shape=[jax.ShapeDtypeStruct((s, w), F32)] * 3,
        scratch_shapes=[],
        semantics=("arbitrary", "arbitrary"),
        name="sb_bwd",
        args=(proj, proj, proj, do, tot),
        carry=carry,
    )


def _mem_fwd(qn, kn, mkv, nheads):
    s = qn.shape[0]
    mtok = kn.shape[0]
    t = _tile(s, MEM_ROWS)
    w = nheads * HEAD_DIM
    scale = HEAD_DIM ** -0.5

    def body(q_ref, k_ref, v_ref, o_ref):
        heads = range(nheads)
        sc = [_dot_nt(_head(q_ref, hh), _head(k_ref, hh)) * scale for hh in heads]
        p = [jnp.exp(sc[hh] - jnp.max(sc[hh], axis=-1, keepdims=True)) for hh in heads]
        p = [p[hh] / jnp.sum(p[hh], axis=-1, keepdims=True) for hh in heads]
        o = [_dot(p[hh].astype(BF16), _head(v_ref, hh)) for hh in heads]
        for hh in heads:
            o_ref[:, hh * HEAD_DIM:(hh + 1) * HEAD_DIM] = o[hh].astype(BF16)

    return pl.pallas_call(
        body,
        grid=(s // t,),
        in_specs=[pl.BlockSpec((t, w), lambda i: (i, 0)),
                  pl.BlockSpec((mtok, w), lambda i: (0, 0)),
                  pl.BlockSpec((mtok, w), lambda i: (0, 1))],
        out_specs=pl.BlockSpec((t, w), lambda i: (i, 0)),
        out_shape=jax.ShapeDtypeStruct((s, w), BF16),
        compiler_params=_params("parallel"),
        name="mem_fwd",
    )(qn, kn, mkv)


def _mem_bwd(qn, kn, mkv, do, nheads):
    s = qn.shape[0]
    mtok = kn.shape[0]
    t = _tile(s, MEM_ROWS)
    w = nheads * HEAD_DIM
    scale = HEAD_DIM ** -0.5

    def body(q_ref, k_ref, v_ref, do_ref, dq_ref, dk_ref, dv_ref):
        @pl.when(pl.program_id(0) == 0)
        def _():
            dk_ref[...] = jnp.zeros_like(dk_ref)
            dv_ref[...] = jnp.zeros_like(dv_ref)

        heads = range(nheads)
        sc = [_dot_nt(_head(q_ref, hh), _head(k_ref, hh)) * scale for hh in heads]
        dp = [_dot_nt(_head(do_ref, hh), _head(v_ref, hh)) for hh in heads]
        p = [jnp.exp(sc[hh] - jnp.max(sc[hh], axis=-1, keepdims=True)) for hh in heads]
        p = [p[hh] / jnp.sum(p[hh], axis=-1, keepdims=True) for hh in heads]
        ds = [(p[hh] * (dp[hh] - jnp.sum(p[hh] * dp[hh], axis=-1, keepdims=True))).astype(BF16) for hh in heads]
        dq = [_dot(ds[hh], _head(k_ref, hh)) * scale for hh in heads]
        dk = [_dot_tn(ds[hh], _head(q_ref, hh)) * scale for hh in heads]
        dv = [_dot_tn(p[hh].astype(BF16), _head(do_ref, hh)) for hh in heads]
        for hh in heads:
            cols = slice(hh * HEAD_DIM, (hh + 1) * HEAD_DIM)
            dq_ref[:, cols] = dq[hh]
            dk_ref[:, cols] += dk[hh]
            dv_ref[:, cols] += dv[hh]

    tile = pl.BlockSpec((t, w), lambda i: (i, 0))
    kspec = pl.BlockSpec((mtok, w), lambda i: (0, 0))
    return pl.pallas_call(
        body,
        grid=(s // t,),
        in_specs=[tile, kspec, pl.BlockSpec((mtok, w), lambda i: (0, 1)), tile],
        out_specs=[tile, kspec, kspec],
        out_shape=[jax.ShapeDtypeStruct((s, w), F32), jax.ShapeDtypeStruct((mtok, w), F32),
                   jax.ShapeDtypeStruct((mtok, w), F32)],
        compiler_params=_params("arbitrary"),
        name="mem_bwd",
    )(qn, kn, mkv, do)


def _merge_fwd(p0, p1, p2, gates, b_gate):
    s, d = p0.shape
    tm, tn = _tile(s, ROW_TILE), _tile(d, COL_TILE)
    nj = d // tn

    def body(p0_ref, p1_ref, p2_ref, ga_ref, gb_ref, gc_ref, b_ref, o_ref, sa_ref, sb_ref, sc_ref):
        acc = jnp.zeros((tm, tn), F32)
        for b, (p_ref, g_ref, s_ref) in enumerate(((p0_ref, ga_ref, sa_ref), (p1_ref, gb_ref, sb_ref),
                                                   (p2_ref, gc_ref, sc_ref))):
            gate = jax.nn.sigmoid(g_ref[...].astype(F32) + b_ref[b:b + 1, :])
            s_ref[...] = gate.astype(BF16)
            acc = acc + gate * p_ref[...].astype(F32)
        o_ref[...] = acc.astype(BF16)

    blk = pl.BlockSpec((tm, tn), lambda i, j: (i, j))
    merged, *sig = pl.pallas_call(
        body,
        grid=(s // tm, nj),
        in_specs=[blk] * 6 + [pl.BlockSpec((3, tn), lambda i, j: (0, j))],
        out_specs=[blk] * 4,
        out_shape=[jax.ShapeDtypeStruct((s, d), BF16)] * 4,
        compiler_params=_params("parallel", "parallel"),
        name="merge_fwd",
    )(p0, p1, p2, *gates, b_gate)
    return merged, tuple(sig)


def _merge_bwd(dmerged, p0, p1, p2, sig):
    s, d = p0.shape
    tm, tn = _tile(s, ROW_TILE), _tile(d, COL_TILE)
    nj = d // tn

    def body(dm_ref, p0_ref, p1_ref, p2_ref, ga_ref, gb_ref, gc_ref,
             d0_ref, d1_ref, d2_ref, dga_ref, dgb_ref, dgc_ref, db_ref):
        dm = dm_ref[...].astype(F32)
        parts = []
        for p_ref, g_ref, dp_ref, dg_ref in ((p0_ref, ga_ref, d0_ref, dga_ref), (p1_ref, gb_ref, d1_ref, dgb_ref),
                                             (p2_ref, gc_ref, d2_ref, dgc_ref)):
            gate = g_ref[...].astype(F32)
            dp_ref[...] = (dm * gate).astype(BF16)
            dgate = dm * p_ref[...].astype(F32) * gate * (1.0 - gate)
            dg_ref[...] = dgate.astype(BF16)
            parts.append(jnp.sum(dgate, axis=0, keepdims=True))
        part = jnp.concatenate(parts, axis=0)

        @pl.when(pl.program_id(1) == 0)
        def _():
            db_ref[...] = part

        @pl.when(pl.program_id(1) > 0)
        def _():
            db_ref[...] += part

    blk = pl.BlockSpec((tm, tn), lambda j, i: (i, j))
    bias = pl.BlockSpec((3, tn), lambda j, i: (0, j))
    return pl.pallas_call(
        body,
        grid=(nj, s // tm),
        in_specs=[blk] * 7,
        out_specs=[blk] * 6 + [bias],
        out_shape=[jax.ShapeDtypeStruct((s, d), BF16)] * 6 + [jax.ShapeDtypeStruct((3, d), F32)],
        compiler_params=_params("parallel", "arbitrary"),
        name="merge_bwd",
    )(dmerged, p0, p1, p2, *sig)


def _shift_down(v, n):
    rows = lax.broadcasted_iota(jnp.int32, v.shape, 0)
    return jnp.where(rows >= n, pltpu.roll(v, n, 0), 0.0)


def _shift_up(v, n):
    s = v.shape[0]
    rows = lax.broadcasted_iota(jnp.int32, v.shape, 0)
    return jnp.where(rows < s - n, pltpu.roll(v, s - n, 0), 0.0)


def _conv(v, w_ref, b_ref):
    taps = w_ref.shape[0]
    out = v * w_ref[taps - 1:taps, :] + b_ref[...]
    for n in range(1, taps):
        out = out + _shift_down(v, n) * w_ref[taps - 1 - n:taps - n, :]
    return out


def _conv_act_fwd(up, conv_w, conv_b):
    s, f2 = up.shape
    f = f2 // 2
    tn = LANES
    nj = f // tn
    taps = conv_w.shape[0]

    def body(ug_ref, uv_ref, wg_ref, wv_ref, bg_ref, bv_ref, o_ref, cg_ref, cv_ref):
        cg = _conv(ug_ref[...].astype(F32), wg_ref, bg_ref)
        cv = _conv(uv_ref[...].astype(F32), wv_ref, bv_ref)
        o_ref[...] = (cg * jax.nn.sigmoid(cg) * cv).astype(BF16)
        cg_ref[...] = cg.astype(BF16)
        cv_ref[...] = cv.astype(BF16)

    out = pl.BlockSpec((s, tn), lambda j: (0, j))
    return pl.pallas_call(
        body,
        grid=(nj,),
        in_specs=[pl.BlockSpec((s, tn), lambda j: (0, j)), pl.BlockSpec((s, tn), lambda j: (0, nj + j)),
                  pl.BlockSpec((taps, tn), lambda j: (0, j)), pl.BlockSpec((taps, tn), lambda j: (0, nj + j)),
                  pl.BlockSpec((1, tn), lambda j: (0, j)), pl.BlockSpec((1, tn), lambda j: (0, nj + j))],
        out_specs=[out, out, out],
        out_shape=[jax.ShapeDtypeStruct((s, f), BF16)] * 3,
        compiler_params=_params("parallel"),
        name="conv_act_fwd",
    )(up, up, conv_w, conv_w, conv_b, conv_b)


def _conv_act_bwd(up, conv_w, conv_g, conv_v, dact):
    s, f2 = up.shape
    f = f2 // 2
    tn = LANES
    nj = f // tn
    taps = conv_w.shape[0]

    def half(v, du, w_ref, dup_ref, dw_ref, db_ref):
        dup = du * w_ref[taps - 1:taps, :]
        rows = [None] * taps
        rows[taps - 1] = jnp.sum(du * v, axis=0, keepdims=True)
        for n in range(1, taps):
            later = _shift_up(du, n)
            dup = dup + later * w_ref[taps - 1 - n:taps - n, :]
            rows[taps - 1 - n] = jnp.sum(later * v, axis=0, keepdims=True)
        dup_ref[...] = dup.astype(BF16)
        dw_ref[...] = jnp.concatenate(rows, axis=0)
        db_ref[...] = jnp.sum(du, axis=0, keepdims=True)

    def body(ug_ref, uv_ref, wg_ref, wv_ref, cg_ref, cv_ref, da_ref,
             dug_ref, duv_ref, dwg_ref, dwv_ref, dbg_ref, dbv_ref):
        cg = cg_ref[...].astype(F32)
        cv = cv_ref[...].astype(F32)
        da = da_ref[...].astype(F32)
        sg = jax.nn.sigmoid(cg)
        dcv = da * cg * sg
        dcg = da * cv * (sg + cg * sg * (1.0 - sg))
        half(ug_ref[...].astype(F32), dcg, wg_ref, dug_ref, dwg_ref, dbg_ref)
        half(uv_ref[...].astype(F32), dcv, wv_ref, duv_ref, dwv_ref, dbv_ref)

    lo = lambda rows: pl.BlockSpec((rows, tn), lambda j: (0, j))
    hi = lambda rows: pl.BlockSpec((rows, tn), lambda j: (0, nj + j))
    return pl.pallas_call(
        body,
        grid=(nj,),
        in_specs=[lo(s), hi(s), lo(taps), hi(taps), lo(s), lo(s), lo(s)],
        out_specs=[lo(s), lo(s), lo(taps), lo(taps), lo(1), lo(1)],
        out_shape=[jax.ShapeDtypeStruct((s, f), BF16)] * 2 + [jax.ShapeDtypeStruct((taps, f), F32)] * 2
        + [jax.ShapeDtypeStruct((1, f), F32)] * 2,
        compiler_params=_params("parallel"),
        name="conv_act_bwd",
    )(up, up, conv_w, conv_w, conv_g, conv_v, dact)


def _row_tile(rows, row_bytes, budget):
    if rows * row_bytes <= budget or rows % 8:
        return rows
    best = 8
    for t in range(8, rows, 8):
        if rows % t == 0 and t * row_bytes <= budget:
            best = t
    return best


def _adamw(w, g, m, v, name):
    r, c = w.shape
    tr = _row_tile(r, c * 4, ADAM_BLOCK_BYTES)

    def body(w_ref, g_ref, m_ref, v_ref, d_ref, mo_ref, vo_ref):
        gg = g_ref[...]
        m_new = ADAM_B1 * m_ref[...] + (1.0 - ADAM_B1) * gg
        v_new = ADAM_B2 * v_ref[...] + (1.0 - ADAM_B2) * (gg * gg)
        m_hat = m_new / (1.0 - ADAM_B1 ** ADAM_STEP)
        v_hat = v_new / (1.0 - ADAM_B2 ** ADAM_STEP)
        d_ref[...] = -ADAM_LR * (m_hat / (jnp.sqrt(v_hat) + ADAM_EPS) + ADAM_WD * w_ref[...])
        mo_ref[...] = m_new
        vo_ref[...] = v_new

    blk = pl.BlockSpec((tr, c), lambda i: (i, 0))
    return pl.pallas_call(
        body,
        grid=(r // tr,),
        in_specs=[blk] * 4,
        out_specs=[blk] * 3,
        out_shape=[jax.ShapeDtypeStruct((r, c), F32)] * 3,
        compiler_params=_params("parallel"),
        name=name,
    )(w, g, m, v)


def _add_sibling(g, r1, core, name):
    _, _, h, c = g.shape
    th = _row_tile(h, c * 2, ADAM_BLOCK_BYTES)

    def body(core_ref, g_ref, r_ref, o_ref):
        o_ref[...] = (g_ref[...].astype(F32) + r_ref[...].astype(F32)).astype(BF16)

    return pl.pallas_call(
        body,
        grid_spec=pltpu.PrefetchScalarGridSpec(
            num_scalar_prefetch=1,
            grid=(N_CHIPS, h // th),
            in_specs=[pl.BlockSpec((None, None, th, c), lambda j, i, core_ref: (j, core_ref[0], i, 0)),
                      pl.BlockSpec((None, th, c), lambda j, i, core_ref: (j, i, 0))],
            out_specs=pl.BlockSpec((None, th, c), lambda j, i, core_ref: (j, i, 0)),
        ),
        out_shape=jax.ShapeDtypeStruct((N_CHIPS, h, c), BF16),
        compiler_params=_params("parallel", "parallel"),
        name=name,
    )(core, g, r1)


def _add_chips(hsum, r2, chip_core, name):
    _, h, c = hsum.shape
    th = _row_tile(h, c * 4, ADAM_BLOCK_BYTES)

    def body(sel_ref, own_ref, r_ref, o_ref):
        acc = own_ref[...].astype(F32)
        for j in range(N_CHIPS - 1):
            acc = acc + r_ref[j].astype(F32)
        o_ref[...] = acc

    return pl.pallas_call(
        body,
        grid_spec=pltpu.PrefetchScalarGridSpec(
            num_scalar_prefetch=1,
            grid=(h // th,),
            in_specs=[pl.BlockSpec((None, th, c), lambda i, sel_ref: (sel_ref[0], i, 0)),
                      pl.BlockSpec((N_CHIPS - 1, th, c), lambda i, sel_ref: (0, i, 0))],
            out_specs=pl.BlockSpec((None, th, c), lambda i, sel_ref: (sel_ref[1], i, 0)),
        ),
        out_shape=jax.ShapeDtypeStruct((2, h, c), F32),
        compiler_params=_params("parallel"),
        name=name,
    )(chip_core, hsum, r2)


def _sum_devices(parts):
    _, r, c = parts.shape

    def body(p_ref, o_ref):
        acc = p_ref[0]
        for j in range(1, N_DEV):
            acc = acc + p_ref[j]
        o_ref[...] = acc

    return pl.pallas_call(
        body,
        out_shape=jax.ShapeDtypeStruct((r, c), F32),
        compiler_params=pltpu.CompilerParams(vmem_limit_bytes=VMEM_LIMIT_BYTES),
        name="sum_devices",
    )(parts)


def _gather_small(vec):
    k = N_DEV - 1

    def body(v_ref, o_ref, send_sems, recv_sems, local_sem):
        x, y, c, _ = _place()
        me = 4 * x + 2 * y + c
        local = pltpu.make_async_copy(v_ref, o_ref.at[me], local_sem)
        local.start()
        peers = [(x ^ (r >> 2 & 1), y ^ (r >> 1 & 1), c ^ (r & 1)) for r in range(1, N_DEV)]
        sends = [_remote(v_ref, o_ref.at[me], send_sems.at[j], recv_sems.at[j], p) for j, p in enumerate(peers)]
        for cp in sends:
            cp.start()
        for j, (px, py, pc) in enumerate(peers):
            sends[j].wait_send()
            blk = o_ref.at[4 * px + 2 * py + pc]
            _remote(blk, blk, send_sems.at[j], recv_sems.at[j], (px, py, pc)).wait_recv()
        local.wait()

    return pl.pallas_call(
        body,
        in_specs=[ANY],
        out_specs=ANY,
        out_shape=jax.ShapeDtypeStruct((N_DEV,) + vec.shape, vec.dtype),
        scratch_shapes=[pltpu.SemaphoreType.DMA((k,)), pltpu.SemaphoreType.DMA((k,)), pltpu.SemaphoreType.DMA(())],
        name="gather_small",
    )(vec)


W_IN_GATES = ("w_in_g0", "w_in_g1", "w_in_g2")
ROW_SHARDED = ("w_in_a", "w_in_f") + W_IN_GATES + ("w_mem_kv", "w_out", "w_down")
COL_SHARDED = ("w_br_fox", "w_br_sb", "w_br_mem", "w_up")
BIG = ROW_SHARDED + COL_SHARDED


def _whole(name, a):
    if name in ROW_SHARDED:
        return a.reshape(N_CHIPS * a.shape[1], a.shape[2])
    return a.transpose(1, 0, 2).reshape(a.shape[1], N_CHIPS * a.shape[2])


def _by_shard(name, grad):
    if name in ROW_SHARDED:
        a = grad.reshape(N_CHIPS, grad.shape[0] // N_CHIPS, grad.shape[1])
    else:
        a = grad.reshape(grad.shape[0], N_CHIPS, grad.shape[1] // N_CHIPS).transpose(1, 0, 2)
    return a.reshape(N_CHIPS, 2, a.shape[1] // 2, a.shape[2])


def _sibling_sums(names, split, theirs, core):
    return [_add_sibling(a, r, core, "add_sibling_" + name) for name, a, r in zip(names, split, theirs)]


GATHER_FIRST = ("w_in_a", "w_in_f")
GATHER_EARLY = W_IN_GATES[:2]
GATHER_MIX = ("w_out", "w_br_fox", "w_br_sb", "w_br_mem") + W_IN_GATES[2:]
REDUCE_FFN = ("w_down", "w_up")
REDUCE_MIX = ("w_out", "w_br_fox", "w_br_sb", "w_br_mem", "w_mem_kv") + W_IN_GATES
REDUCE_IN = ("w_in_a", "w_in_f")


def _local_step(x, mem, target, w, shard, core, chip_core):
    d = x.shape[1]
    nf = shard["w_br_fox"].shape[0] // HEAD_DIM
    nsb = shard["w_br_sb"].shape[0] // HEAD_DIM
    nm = shard["w_br_mem"].shape[0] // HEAD_DIM
    w = dict(w)

    def take(names, gathered):
        for name, a in zip(names, gathered):
            w[name] = _whole(name, a)

    fq, fk, fv = 0, nf, 2 * nf
    sq, sk, sv = 3 * nf, 3 * nf + nsb, 3 * nf + 2 * nsb
    mq = 3 * nf + 3 * nsb

    h, rstd1, moved = _rms_fwd(x, w["g_mix"], "rms_mix_fwd", carry=_Gather([shard[name] for name in GATHER_FIRST]))
    take(GATHER_FIRST, moved)
    proj, moved = _mm(h, w["w_in_a"], "nn", BF16, "proj_att", carry=_Gather([shard[name] for name in GATHER_EARLY]))
    take(GATHER_EARLY, moved)
    gate0, moved = _mm(h, w["w_in_g0"], "nn", BF16, "proj_gate0", carry=_Gather([shard["w_mem_kv"]]))
    take(("w_mem_kv",), moved)
    gate1 = _mm(h, w["w_in_g1"], "nn", BF16, "proj_gate1")
    f_logit, c_sum = _forget_fwd(h, w["w_in_f"], w["b_forget"])
    c_t = c_sum[:, :nf].T
    c_col, c_row = c_t[:, :, None], c_t[:, None, :]
    qn = _headnorm_fwd(proj, fq, nf, w["g_q_fox"], "fox_qnorm_fwd")
    kn = _headnorm_fwd(proj, fk, nf, w["g_k_fox"], "fox_knorm_fwd")
    (o_fox, o_fox32, lse), moved = _fox_fwd(qn, kn, proj, fv, c_col, c_row, nf,
                                            carry=_Gather([shard[name] for name in GATHER_MIX]))
    take(GATHER_MIX, moved)
    gates = (gate0, gate1, _mm(h, w["w_in_g2"], "nn", BF16, "proj_gate2"))
    (o_sb, sb_tot), moved = _sb_fwd(proj, sq, sk, sv, nsb, carry=_Gather([shard["w_up"]]))
    take(("w_up",), moved)
    memn, rstd_m = _rms_fwd(mem, w["g_mem"], "rms_mem_fwd")
    mkv = _mm(memn, w["w_mem_kv"], "nn", BF16, "mem_kv")
    kmn = _headnorm_fwd(mkv, 0, nm, w["g_k_mem"], "mem_knorm_fwd")
    qmn = _headnorm_fwd(proj, mq, nm, w["g_q_mem"], "mem_qnorm_fwd")
    o_mem = _mem_fwd(qmn, kmn, mkv, nm)
    p0 = _mm(o_fox, w["w_br_fox"], "nn", BF16, "branch_fox")
    p1 = _mm(o_sb, w["w_br_sb"], "nn", BF16, "branch_sb")
    p2 = _mm(o_mem, w["w_br_mem"], "nn", BF16, "branch_mem")
    merged, sig = _merge_fwd(p0, p1, p2, gates, w["b_gate"])
    x1 = _mm(merged, w["w_out"], "nn", F32, "out_proj", residual=x)
    h2, rstd2 = _rms_fwd(x1, w["g_ffn"], "rms_ffn_fwd")
    up, moved = _mm(h2, w["w_up"], "nn", BF16, "ffn_up", carry=_Gather([shard["w_down"]]))
    take(("w_down",), moved)
    act, conv_g, conv_v = _conv_act_fwd(up, w["conv_w"], w["conv_b"])
    dy, dyb, lparts = _mm(act, w["w_down"], "nn", F32, "ffn_down_loss", residual=x1, loss_target=target)
    loss = (0.5 / d) * jnp.sum(lparts[::8, ::LANES])

    g = {}
    dact = _mm(dyb, w["w_down"], "nt", BF16, "ffn_down_dx")
    g["w_down"] = _mm(act, dyb, "tn", BF16, "ffn_down_dw")
    dug, duv, dwg, dwv, dbg, dbv = _conv_act_bwd(up, w["conv_w"], conv_g, conv_v, dact)
    dup = jnp.concatenate([dug, duv], axis=1)
    g["conv_w"] = jnp.concatenate([dwg, dwv], axis=1)
    g["conv_b"] = jnp.concatenate([dbg, dbv], axis=1)
    split_down = [_by_shard("w_down", g["w_down"])]
    dh2, theirs_down = _mm(dup, w["w_up"], "nt", BF16, "ffn_up_dx", carry=_Swap(split_down))
    g["w_up"] = _mm(h2, dup, "tn", BF16, "ffn_up_dw")
    split_up = [_by_shard("w_up", g["w_up"])]
    dx1, dx1b, g["g_ffn"] = _rms_bwd(dh2, x1, rstd2, w["g_ffn"], dy, "rms_ffn_bwd")
    dmerged, theirs_up = _mm(dx1b, w["w_out"], "nt", BF16, "out_proj_dx", carry=_Swap(split_up))
    sums_ffn = _sibling_sums(REDUCE_FFN, split_down + split_up, theirs_down + theirs_up, core)
    g["w_out"] = _mm(merged, dx1b, "tn", BF16, "out_proj_dw")
    dp0, dp1, dp2, dga, dgb, dgc, g["b_gate"] = _merge_bwd(dmerged, p0, p1, p2, sig)
    dgates = (dga, dgb, dgc)
    for name, dgate in zip(W_IN_GATES, dgates):
        g[name] = _mm(h, dgate, "tn", BF16, name + "_dw")
    do_fox = _mm(dp0, w["w_br_fox"], "nt", BF16, "branch_fox_dx")
    do_sb = _mm(dp1, w["w_br_sb"], "nt", BF16, "branch_sb_dx")
    do_mem = _mm(dp2, w["w_br_mem"], "nt", BF16, "branch_mem_dx")
    g["w_br_fox"] = _mm(o_fox, dp0, "tn", BF16, "branch_fox_dw")
    g["w_br_sb"] = _mm(o_sb, dp1, "tn", BF16, "branch_sb_dw")
    g["w_br_mem"] = _mm(o_mem, dp2, "tn", BF16, "branch_mem_dw")
    dqmn, dkmn, dvm = _mem_bwd(qmn, kmn, mkv, do_mem, nm)
    dmq, g["g_q_mem"] = _headnorm_bwd(dqmn, proj, mq, nm, w["g_q_mem"], "mem_qnorm_bwd")
    dkm, g["g_k_mem"] = _headnorm_bwd(dkmn, mkv, 0, nm, w["g_k_mem"], "mem_knorm_bwd")
    dmkv = jnp.concatenate([dkm, dvm.astype(BF16)], axis=1)
    g["w_mem_kv"] = _mm(memn, dmkv, "tn", BF16, "mem_kv_dw")
    dmemn = _mm(dmkv, w["w_mem_kv"], "nt", BF16, "mem_kv_dx")
    _, _, g["g_mem"] = _rms_bwd(dmemn, mem, rstd_m, w["g_mem"], None, "rms_mem_bwd")
    split_mix =[_by_shard(name, g[name]) for name in REDUCE_MIX]

    (dqn, dkn, dfv, drs, dcs), moved = _fox_bwd(qn, kn, proj, fv, c_col, c_row, o_fox32, do_fox, lse, nf,
                                                carry=_Both(_Scatter(sums_ffn[1:]), _Swap(split_mix)))
    others_up, theirs_mix = moved[:1], moved[1:]
    sums_mix = _sibling_sums(REDUCE_MIX, split_mix, theirs_mix, core)
    dfq, g["g_q_fox"] = _headnorm_bwd(dqn, proj, fq, nf, w["g_q_fox"], "fox_qnorm_bwd")
    dfk, g["g_k_fox"] = _headnorm_bwd(dkn, proj, fk, nf, w["g_k_fox"], "fox_knorm_bwd")
    dc = jnp.pad((drs[:, :, 0] - dcs[:, 0, :]).T, ((0, 0), (0, LANES - nf)))
    df, g["b_forget"] = _forget_bwd(dc, f_logit, w["b_forget"])
    (dsq, dsk, dsv), others_mix = _sb_bwd(proj, sq, sk, sv, do_sb, sb_tot, nsb, carry=_Scatter(sums_ffn[:1] + sums_mix))
    others_ffn = others_mix[:1] + others_up
    others_mix = others_mix[1:]

    dproj = jnp.concatenate([dfq, dfk, dfv.astype(BF16), dsq.astype(BF16), dsk.astype(BF16), dsv.astype(BF16), dmq],
                            axis=1)
    dfb = df.astype(BF16)
    g["w_in_a"] = _mm(h, dproj, "tn", BF16, "proj_att_dw")
    g["w_in_f"] = _mm(h, dfb, "tn", BF16, "proj_forget_dw")
    split_in = [_by_shard(name, g[name]) for name in REDUCE_IN]
    dh, theirs_in = _mm(dgates[0], w[W_IN_GATES[0]], "nt", F32, W_IN_GATES[0] + "_dx", carry=_Swap(split_in))
    sums_in = _sibling_sums(REDUCE_IN, split_in, theirs_in, core)
    for name, dgate in zip(W_IN_GATES[1:], dgates[1:]):
        dh = _mm(dgate, w[name], "nt", F32, name + "_dx", residual=dh)
    dh, others_in = _mm(dproj, w["w_in_a"], "nt", F32, "proj_att_dx", residual=dh, carry=_Scatter(sums_in))
    names = REDUCE_FFN + REDUCE_MIX + REDUCE_IN
    finals = [_add_chips(own, theirs, chip_core, "add_chips_" + name)
              for name, own, theirs in zip(names, sums_ffn + sums_mix + sums_in, others_ffn + others_mix + others_in)]
    grad_x, _, g["g_mix"], joined = _rms_bwd(dh, x, rstd1, w["g_mix"], dx1, "rms_mix_bwd", more=(dfb, w["w_in_f"]),
                                             carry=_Join(finals))
    summed = {name: a.reshape(2 * a.shape[1], a.shape[2]) for name, a in zip(names, joined)}
    return loss, grad_x, g, summed


SMALL = ("g_mix", "b_forget", "g_q_fox", "g_k_fox", "g_mem", "g_q_mem", "g_k_mem", "b_gate", "g_ffn", "conv_w",
         "conv_b")
SMALL_SHARDED = ("b_gate", "conv_w")
PACK_ROWS = 8


def _pack(arrs):
    flat = jnp.concatenate([a.reshape(-1) for a in arrs])
    unit = PACK_ROWS * LANES
    flat = jnp.pad(flat, (0, -flat.shape[0] % unit))
    return flat.reshape(-1, LANES)


def _unpack(packed, shapes):
    flat = packed.reshape(-1)
    out, at = [], 0
    for s in shapes:
        n = 1
        for dim in s:
            n *= dim
        out.append(flat[at:at + n].reshape(s))
        at += n
    return out


def kernel(x, mem, g_mix, w_in, b_forget, g_q_fox, g_k_fox, g_mem, w_mem_kv, g_q_mem, g_k_mem, w_br_fox, w_br_sb, w_br_mem, b_gate, w_out, g_ffn, w_up, conv_w, conv_b, w_down, loss_target, m_g_mix, m_w_in, m_b_forget, m_g_q_fox, m_g_k_fox, m_g_mem, m_w_mem_kv, m_g_q_mem, m_g_k_mem, m_w_br_fox, m_w_br_sb, m_w_br_mem, m_b_gate, m_w_out, m_g_ffn, m_w_up, m_conv_w, m_conv_b, m_w_down, v_g_mix, v_w_in, v_b_forget, v_g_q_fox, v_g_k_fox, v_g_mem, v_w_mem_kv, v_g_q_mem, v_g_k_mem, v_w_br_fox, v_w_br_sb, v_w_br_mem, v_b_gate, v_w_out, v_g_ffn, v_w_up, v_conv_w, v_conv_b, v_w_down):
    given = dict(g_mix=g_mix, w_in=w_in, b_forget=b_forget, g_q_fox=g_q_fox, g_k_fox=g_k_fox, g_mem=g_mem,
                 w_mem_kv=w_mem_kv, g_q_mem=g_q_mem, g_k_mem=g_k_mem, w_br_fox=w_br_fox, w_br_sb=w_br_sb,
                 w_br_mem=w_br_mem, b_gate=b_gate, w_out=w_out, g_ffn=g_ffn, w_up=w_up, conv_w=conv_w, conv_b=conv_b,
                 w_down=w_down)
    m_in = dict(g_mix=m_g_mix, w_in=m_w_in, b_forget=m_b_forget, g_q_fox=m_g_q_fox, g_k_fox=m_g_k_fox, g_mem=m_g_mem,
                w_mem_kv=m_w_mem_kv, g_q_mem=m_g_q_mem, g_k_mem=m_g_k_mem, w_br_fox=m_w_br_fox, w_br_sb=m_w_br_sb,
                w_br_mem=m_w_br_mem, b_gate=m_b_gate, w_out=m_w_out, g_ffn=m_g_ffn, w_up=m_w_up, conv_w=m_conv_w,
                conv_b=m_conv_b, w_down=m_w_down)
    v_in = dict(g_mix=v_g_mix, w_in=v_w_in, b_forget=v_b_forget, g_q_fox=v_g_q_fox, g_k_fox=v_g_k_fox, g_mem=v_g_mem,
                w_mem_kv=v_w_mem_kv, g_q_mem=v_g_q_mem, g_k_mem=v_g_k_mem, w_br_fox=v_w_br_fox, w_br_sb=v_w_br_sb,
                w_br_mem=v_w_br_mem, b_gate=v_b_gate, w_out=v_w_out, g_ffn=v_g_ffn, w_up=v_w_up, conv_w=v_conv_w,
                conv_b=v_conv_b, w_down=v_w_down)
    layered = {k: a.ndim == 3 for k, a in given.items()}
    drop = lambda a: a[0] if a.ndim == 3 else a
    given = {k: drop(a) for k, a in given.items()}
    m_in = {k: drop(a) for k, a in m_in.items()}
    v_in = {k: drop(a) for k, a in v_in.items()}

    xi, yi, ci = lax.axis_index("x"), lax.axis_index("y"), lax.axis_index("c")
    chip = (2 * xi + yi).astype(jnp.int32)
    core_arr = ci.astype(jnp.int32).reshape(1)
    chip_core = jnp.stack([chip, ci.astype(jnp.int32)])

    nf = given["b_forget"].shape[1]
    cut = 3 * given["w_br_fox"].shape[0]

    d_model = given["w_out"].shape[1]
    gate0 = given["w_in"].shape[1] - len(W_IN_GATES) * d_model
    shard = {
        "w_in_a": jnp.concatenate([given["w_in"][:, :cut], given["w_in"][:, cut + nf:gate0]], axis=1).astype(BF16),
        "w_in_f": jnp.pad(given["w_in"][:, cut:cut + nf], ((0, 0), (0, LANES - nf))).astype(BF16),
    }
    for b, name in enumerate(W_IN_GATES):
        shard[name] = given["w_in"][:, gate0 + b * d_model:gate0 + (b + 1) * d_model].astype(BF16)
    for name in BIG:
        if name not in shard:
            shard[name] = given[name].astype(BF16)
    w = {}
    small_shapes = [given[name].shape for name in SMALL_SHARDED]
    small_parts = _gather_small(_pack([given[name] for name in SMALL_SHARDED]))[0::2]
    per_chip = [_unpack(small_parts[j], small_shapes) for j in range(N_CHIPS)]
    for k, name in enumerate(SMALL_SHARDED):
        w[name] = jnp.concatenate([per_chip[j][k] for j in range(N_CHIPS)], axis=1)
    for name in SMALL:
        if name not in SMALL_SHARDED:
            w[name] = given[name]
    w["b_forget"] = jnp.pad(given["b_forget"], ((0, 0), (0, LANES - nf)))

    loss, grad_x, g, summed = _local_step(x[0], mem[0], loss_target[0], w, shard, core_arr, chip_core)
    loss = lax.psum(loss, ("x", "y", "c"))
    grads = {name: summed[name] for name in BIG if name in given}
    grads["w_in"] = jnp.concatenate([summed["w_in_a"][:, :cut], summed["w_in_f"][:, :nf], summed["w_in_a"][:, cut:]]
                                    + [summed[name] for name in W_IN_GATES], axis=1)

    g["b_forget"] = g["b_forget"][:, :nf]
    small_full_shapes = [g[name].shape for name in SMALL]
    small_sum = _unpack(_sum_devices(_gather_small(_pack([g[name] for name in SMALL]))), small_full_shapes)
    for name, a in zip(SMALL, small_sum):
        if name in SMALL_SHARDED:
            width = given[name].shape[1]
            a = lax.dynamic_slice_in_dim(a, chip * width, width, axis=1)
        grads[name] = a

    delta, new_m, new_v = {}, {}, {}
    for name in WEIGHTS:
        if name not in SMALL:
            delta[name], new_m[name], new_v[name] = _adamw(given[name], grads[name], m_in[name], v_in[name],
                                                           "adamw_" + name)
    shapes = [given[name].shape for name in SMALL]
    packed = [_pack([src[name] for name in SMALL]) for src in (given, grads, m_in, v_in)]
    for dst, res in zip((delta, new_m, new_v), _adamw(*packed, "adamw_small")):
        for name, a in zip(SMALL, _unpack(res, shapes)):
            dst[name] = a

    out = [loss, grad_x[None]]
    for src in (grads, delta, new_m, new_v):
        out.extend(src[name][None] if layered[name] else src[name] for name in WEIGHTS)
    return tuple(out)
```

```python
import functools

import jax
import jax.numpy as jnp
from jax import lax
from jax.experimental import pallas as pl
from jax.experimental.pallas import tpu as pltpu

F32 = jnp.float32
BF16 = jnp.bfloat16

HEAD_DIM = 128
EPS = 1e-6
NEG_BIG = -1e30

ADAM_LR = 0.001
ADAM_B1 = 0.9
ADAM_B2 = 0.999
ADAM_EPS = 1e-08
ADAM_WD = 0.01
ADAM_STEP = 10

LANES = 128
BF16_SUBLANES = 16
VMEM_LIMIT_BYTES = 60 * 1024 * 1024
MM_TILE = 1024
MM_TILE_K = {"nn": 2048, "nt": 2816, "tn": 4096}
ATT_TILE = 256
ROW_TILE = 256
HEADNORM_ROWS = 1024
MEM_ROWS = 1024
FORGET_ROWS = 512
COL_TILE = 1024
ADAM_BLOCK_BYTES = 2 << 20

N_CHIPS = 4
N_DEV = 8
MESH = pl.DeviceIdType.MESH

IN_NAMES = ['x', 'mem', 'g_mix', 'w_in', 'b_forget', 'g_q_fox', 'g_k_fox', 'g_mem', 'w_mem_kv', 'g_q_mem', 'g_k_mem',
            'w_br_fox', 'w_br_sb', 'w_br_mem', 'b_gate', 'w_out', 'g_ffn', 'w_up', 'conv_w', 'conv_b', 'w_down']
WEIGHTS = IN_NAMES[2:]


def _tile(n, target):
    if n <= target:
        return n
    for t in range(target - target % LANES, LANES - 1, -LANES):
        if n % t == 0:
            return t
    return n


def _params(*sem):
    return pltpu.CompilerParams(dimension_semantics=sem, vmem_limit_bytes=VMEM_LIMIT_BYTES)


def _log_sigmoid(z):
    return jnp.minimum(z, 0.0) - jnp.log(1.0 + jnp.exp(-jnp.abs(z)))


def _split2(v):
    hi = v.astype(BF16)
    lo = (v - hi.astype(F32)).astype(BF16)
    return hi, lo


def _split3(v):
    hi = v.astype(BF16)
    r = v - hi.astype(F32)
    mid = r.astype(BF16)
    lo = (r - mid.astype(F32)).astype(BF16)
    return hi, mid, lo


def _dot(a, b):
    return lax.dot_general(a, b, (((1,), (0,)), ((), ())), preferred_element_type=F32)


def _dot_nt(a, b):
    return lax.dot_general(a, b, (((1,), (1,)), ((), ())), preferred_element_type=F32)


def _dot_tn(a, b):
    return lax.dot_general(a, b, (((0,), (0,)), ((), ())), preferred_element_type=F32)


ANY = pl.BlockSpec(memory_space=pl.ANY)


def _place():
    x, y, c = lax.axis_index("x"), lax.axis_index("y"), lax.axis_index("c")
    others = [(1 - x, y), (x, 1 - y), (1 - x, 1 - y)]
    return x, y, c, others


def _remote(src, dst, send_sem, recv_sem, to):
    return pltpu.make_async_remote_copy(src_ref=src, dst_ref=dst, send_sem=send_sem, recv_sem=recv_sem,
                                        device_id=to, device_id_type=MESH)


class _Gather:
    PER_SHARD = 7

    def __init__(self, shards):
        self.inputs = list(shards)
        n = len(shards) * self.PER_SHARD
        self.out_shapes = [jax.ShapeDtypeStruct((N_CHIPS,) + s.shape, s.dtype) for s in shards]
        self.scratch = [pltpu.SemaphoreType.DMA((n,)), pltpu.SemaphoreType.DMA((n,))]

    def _first(self, ins, outs, sems):
        send_sems, recv_sems = sems
        x, y, c, others = _place()
        me = 2 * x + y
        k = self.PER_SHARD
        copies = []
        for i in range(len(ins)):
            h = ins[i].shape[0] // 2
            mine = pl.ds(pl.multiple_of(c * h, BF16_SUBLANES), h)
            for j, (ox, oy) in enumerate(others):
                copies.append(_remote(ins[i].at[mine], outs[i].at[me, mine], send_sems.at[k * i + j],
                                      recv_sems.at[k * i + j], (ox, oy, c)))
            copies.append(_remote(ins[i], outs[i].at[me], send_sems.at[k * i + 6], recv_sems.at[k * i + 6],
                                  (x, y, 1 - c)))
        return copies

    def start(self, ins, outs, sems):
        for cp in self._first(ins, outs, sems):
            cp.start()

    def finish(self, ins, outs, sems):
        send_sems, recv_sems = sems
        x, y, c, others = _place()
        me = 2 * x + y
        sibling = (x, y, 1 - c)
        k = self.PER_SHARD
        passed = []
        for i in range(len(ins)):
            h = ins[i].shape[0] // 2
            mine = pl.ds(pl.multiple_of(c * h, BF16_SUBLANES), h)
            for j, (ox, oy) in enumerate(others):
                blk = outs[i].at[2 * ox + oy, mine]
                _remote(blk, blk, send_sems.at[k * i + j], recv_sems.at[k * i + j], (ox, oy, c)).wait_recv()
                cp = _remote(blk, blk, send_sems.at[k * i + 3 + j], recv_sems.at[k * i + 3 + j], sibling)
                cp.start()
                passed.append(cp)
        for i in range(len(ins)):
            h = ins[i].shape[0] // 2
            theirs = pl.ds(pl.multiple_of((1 - c) * h, BF16_SUBLANES), h)
            for j, (ox, oy) in enumerate(others):
                blk = outs[i].at[2 * ox + oy, theirs]
                _remote(blk, blk, send_sems.at[k * i + 3 + j], recv_sems.at[k * i + 3 + j], sibling).wait_recv()
            own = outs[i].at[me]
            _remote(own, own, send_sems.at[k * i + 6], recv_sems.at[k * i + 6], sibling).wait_recv()
        for cp in self._first(ins, outs, sems) + passed:
            cp.wait_send()


class _Scatter:
    def __init__(self, sums):
        self.inputs = list(sums)
        k = N_CHIPS - 1
        self.out_shapes = [jax.ShapeDtypeStruct((k,) + g.shape[1:], g.dtype) for g in sums]
        self.scratch = [pltpu.SemaphoreType.DMA((k * len(sums),)), pltpu.SemaphoreType.DMA((k * len(sums),))]

    def _copies(self, ins, outs, sems):
        send_sems, recv_sems = sems
        _, _, c, others = _place()
        k = N_CHIPS - 1
        return [_remote(ins[i].at[2 * ox + oy], outs[i].at[j], send_sems.at[k * i + j], recv_sems.at[k * i + j],
                        (ox, oy, c))
                for i in range(len(ins)) for j, (ox, oy) in enumerate(others)]

    def start(self, ins, outs, sems):
        for cp in self._copies(ins, outs, sems):
            cp.start()

    def finish(self, ins, outs, sems):
        for cp in self._copies(ins, outs, sems):
            cp.wait()


class _Swap:
    def __init__(self, grads):
        self.inputs = list(grads)
        n = len(grads)
        self.out_shapes = [jax.ShapeDtypeStruct((g.shape[0],) + g.shape[2:], g.dtype) for g in grads]
        self.scratch = [pltpu.SemaphoreType.DMA((n,)), pltpu.SemaphoreType.DMA((n,))]

    def _copies(self, ins, outs, sems):
        send_sems, recv_sems = sems
        x, y, c, _ = _place()
        return [_remote(ins[i].at[:, 1 - c], outs[i], send_sems.at[i], recv_sems.at[i], (x, y, 1 - c))
                for i in range(len(ins))]

    def start(self, ins, outs, sems):
        for cp in self._copies(ins, outs, sems):
            cp.start()

    def finish(self, ins, outs, sems):
        for cp in self._copies(ins, outs, sems):
            cp.wait()


class _Join:
    def __init__(self, finals):
        self.inputs = list(finals)
        n = len(finals)
        self.out_shapes = [jax.ShapeDtypeStruct(f.shape, f.dtype) for f in finals]
        self.scratch = [pltpu.SemaphoreType.DMA((n,)), pltpu.SemaphoreType.DMA((n,))]
        self.aliases = {i: i for i in range(n)}

    def _sends(self, outs, sems):
        send_sems, recv_sems = sems
        x, y, c, _ = _place()
        return [_remote(outs[i].at[c], outs[i].at[c], send_sems.at[i], recv_sems.at[i], (x, y, 1 - c))
                for i in range(len(outs))]

    def start(self, ins, outs, sems):
        for cp in self._sends(outs, sems):
            cp.start()

    def finish(self, ins, outs, sems):
        send_sems, recv_sems = sems
        x, y, c, _ = _place()
        for i, cp in enumerate(self._sends(outs, sems)):
            cp.wait_send()
            other = outs[i].at[1 - c]
            _remote(other, other, send_sems.at[i], recv_sems.at[i], (x, y, 1 - c)).wait_recv()


class _Both:
    def __init__(self, first, second):
        self.parts = (first, second)
        self.inputs = first.inputs + second.inputs
        self.out_shapes = first.out_shapes + second.out_shapes
        self.scratch = first.scratch + second.scratch

    def _each(self, ins, outs, sems):
        first = self.parts[0]
        a, b, c = len(first.inputs), len(first.out_shapes), len(first.scratch)
        return ((first, ins[:a], outs[:b], sems[:c]), (self.parts[1], ins[a:], outs[b:], sems[c:]))

    def start(self, ins, outs, sems):
        for part, i, o, s in self._each(ins, outs, sems):
            part.start(i, o, s)

    def finish(self, ins, outs, sems):
        for part, i, o, s in self._each(ins, outs, sems):
            part.finish(i, o, s)


def _call(body, *, grid, in_specs, out_specs, out_shape, scratch_shapes, semantics, name, args, carry=None):
    n_in, n_out, n_scr = len(in_specs), len(out_specs), len(scratch_shapes)
    if carry is None:
        res = pl.pallas_call(body, grid=grid, in_specs=in_specs, out_specs=out_specs, out_shape=out_shape,
                             scratch_shapes=scratch_shapes, compiler_params=_params(*semantics), name=name)(*args)
        return list(res), []
    nci, nco = len(carry.inputs), len(carry.out_shapes)
    a, b = n_in, n_in + nci
    c, d = b + n_out, b + n_out + nco
    e = d + n_scr

    def carried(*refs):
        ids = [pl.program_id(k) for k in range(len(grid))]
        first = functools.reduce(jnp.logical_and, [i == 0 for i in ids])
        last = functools.reduce(jnp.logical_and, [i == n - 1 for i, n in zip(ids, grid)])

        @pl.when(first)
        def _():
            carry.start(refs[a:b], refs[c:d], refs[e:])

        body(*refs[:a], *refs[b:c], *refs[d:e])

        @pl.when(last)
        def _():
            carry.finish(refs[a:b], refs[c:d], refs[e:])

    res = pl.pallas_call(
        carried,
        grid=grid,
        in_specs=list(in_specs) + [ANY] * nci,
        out_specs=list(out_specs) + [ANY] * nco,
        out_shape=list(out_shape) + carry.out_shapes,
        scratch_shapes=list(scratch_shapes) + carry.scratch,
        input_output_aliases={n_in + i: n_out + o for i, o in getattr(carry, "aliases", {}).items()},
        compiler_params=_params(*(["arbitrary"] * len(grid))),
        name=name,
    )(*args, *carry.inputs)
    return list(res[:n_out]), list(res[n_out:])


def _mm(a, b, mode, out_dtype, name, residual=None, carry=None, loss_target=None):
    if mode == "nn":
        (m, k), (k2, n) = a.shape, b.shape
    elif mode == "nt":
        (m, k), (n, k2) = a.shape, b.shape
    else:
        (k, m), (k2, n) = a.shape, b.shape
    assert k == k2, (a.shape, b.shape, mode)
    has_res = residual is not None
    has_loss = loss_target is not None
    n_in = 2 + has_res + has_loss
    tm, tn, tk = _tile(m, MM_TILE), _tile(n, MM_TILE), _tile(k, MM_TILE_K[mode])
    nk = k // tk
    dot = {"nn": _dot, "nt": _dot_nt, "tn": _dot_tn}[mode]

    def body(*refs):
        a_ref, b_ref = refs[:2]
        r_ref = refs[2] if has_res else None
        t_ref = refs[n_in - 1] if has_loss else None
        o_ref = refs[n_in]

        def finish(acc):
            if has_res:
                acc = acc + r_ref[...]
            if has_loss:
                err = acc - t_ref[...]
                dy = err * (1.0 / n)
                o_ref[...] = dy
                refs[n_in + 1][...] = dy.astype(BF16)
                tot = jnp.sum(jnp.sum(err * err, axis=-1, keepdims=True), axis=0, keepdims=True)
                refs[n_in + 2][...] = jnp.broadcast_to(tot, (8, LANES))
            else:
                o_ref[...] = acc.astype(o_ref.dtype)

        part = dot(a_ref[...], b_ref[...])
        if nk == 1:
            finish(part)
        else:
            acc_ref = refs[-1]
            kk = pl.program_id(2)

            @pl.when(kk == 0)
            def _():
                acc_ref[...] = part

            @pl.when(kk > 0)
            def _():
                acc_ref[...] += part

            @pl.when(kk == nk - 1)
            def _():
                finish(acc_ref[...])

    if mode == "tn":
        a_spec = pl.BlockSpec((tk, tm), lambda j, i, kk: (kk, i))
    else:
        a_spec = pl.BlockSpec((tm, tk), lambda j, i, kk: (i, kk))
    if mode == "nt":
        b_spec = pl.BlockSpec((tn, tk), lambda j, i, kk: (j, kk))
    else:
        b_spec = pl.BlockSpec((tk, tn), lambda j, i, kk: (kk, j))
    o_spec = pl.BlockSpec((tm, tn), lambda j, i, kk: (i, j))
    in_specs = [a_spec, b_spec] + [o_spec] * (has_res + has_loss)
    args = (a, b) + ((residual,) if has_res else ()) + ((loss_target,) if has_loss else ())
    out_specs, out_shape = [o_spec], [jax.ShapeDtypeStruct((m, n), out_dtype)]
    if has_loss:
        out_specs += [o_spec, pl.BlockSpec((8, LANES), lambda j, i, kk: (i, j))]
        out_shape += [jax.ShapeDtypeStruct((m, n), BF16), jax.ShapeDtypeStruct((m // tm * 8, n // tn * LANES), F32)]
    outs, moved = _call(
        body,
        grid=(n // tn, m // tm, nk),
        in_specs=in_specs,
        out_specs=out_specs,
        out_shape=out_shape,
        scratch_shapes=[pltpu.VMEM((tm, tn), F32)] if nk > 1 else [],
        semantics=("parallel", "parallel", "arbitrary"),
        name=name,
        args=args,
        carry=carry,
    )
    out = outs if has_loss else outs[0]
    return out if carry is None else (out, moved)


def _rms_fwd(x, g, name, carry=None):
    s, d = x.shape
    tm = _tile(s, ROW_TILE)

    def body(x_ref, g_ref, h_ref, r_ref):
        xf = x_ref[...]
        r = lax.rsqrt(jnp.mean(xf * xf, axis=-1, keepdims=True) + EPS)
        h_ref[...] = ((xf * r) * g_ref[...]).astype(BF16)
        r_ref[...] = r

    (h, rstd), moved = _call(
        body,
        grid=(s // tm,),
        in_specs=[pl.BlockSpec((tm, d), lambda i: (i, 0)), pl.BlockSpec((1, d), lambda i: (0, 0))],
        out_specs=[pl.BlockSpec((tm, d), lambda i: (i, 0)), pl.BlockSpec((tm, 1), lambda i: (i, 0))],
        out_shape=[jax.ShapeDtypeStruct((s, d), BF16), jax.ShapeDtypeStruct((s, 1), F32)],
        scratch_shapes=[],
        semantics=("parallel",),
        name=name,
        args=(x, g),
        carry=carry,
    )
    return (h, rstd) if carry is None else (h, rstd, moved)


def _rms_bwd(dh, x, rstd, g, res, name, more=None, carry=None):
    s, d = x.shape
    tm = _tile(s, ROW_TILE)
    has_res = res is not None
    has_more = more is not None
    n_in = 4 + has_res + 2 * has_more

    def body(*refs):
        dh_ref, x_ref, r_ref, g_ref = refs[:4]
        res_ref = refs[4] if has_res else None
        dx_ref, dxb_ref, dg_ref = refs[n_in:]
        dhf = dh_ref[...].astype(F32)
        if has_more:
            dhf = dhf + _dot_nt(refs[n_in - 2][...], refs[n_in - 1][...])
        xhat = x_ref[...] * r_ref[...]
        dy = dhf * g_ref[...]
        dx = r_ref[...] * (dy - xhat * jnp.mean(dy * xhat, axis=-1, keepdims=True))
        if has_res:
            dx = dx + res_ref[...]
        dx_ref[...] = dx
        dxb_ref[...] = dx.astype(BF16)
        part = jnp.sum(dhf * xhat, axis=0, keepdims=True)

        @pl.when(pl.program_id(0) == 0)
        def _():
            dg_ref[...] = part

        @pl.when(pl.program_id(0) > 0)
        def _():
            dg_ref[...] += part

    row = pl.BlockSpec((tm, d), lambda i: (i, 0))
    vec = pl.BlockSpec((1, d), lambda i: (0, 0))
    in_specs = [row, row, pl.BlockSpec((tm, 1), lambda i: (i, 0)), vec] + ([row] if has_res else [])
    args = (dh, x, rstd, g) + ((res,) if has_res else ())
    if has_more:
        k = more[0].shape[1]
        in_specs += [pl.BlockSpec((tm, k), lambda i: (i, 0)), pl.BlockSpec((d, k), lambda i: (0, 0))]
        args += tuple(more)
    (dx, dxb, dg), moved = _call(
        body,
        grid=(s // tm,),
        in_specs=in_specs,
        out_specs=[row, row, vec],
        out_shape=[jax.ShapeDtypeStruct((s, d), F32), jax.ShapeDtypeStruct((s, d), BF16),
                   jax.ShapeDtypeStruct((1, d), F32)],
        scratch_shapes=[],
        semantics=("arbitrary",),
        name=name,
        args=args,
        carry=carry,
    )
    return (dx, dxb, dg) if carry is None else (dx, dxb, dg, moved)


def _headnorm_fwd(src, cols, nheads, gains, name):
    s = src.shape[0]
    tm = _tile(s, HEADNORM_ROWS)
    w = nheads * HEAD_DIM
    n = len(cols)
    assert all(col0 % nheads == 0 for col0 in cols)

    def body(*refs):
        for x_ref, g_ref, o_ref in zip(refs[:n], refs[n:2 * n], refs[2 * n:]):
            for hh in range(nheads):
                xf = _head(x_ref, hh).astype(F32)
                r = lax.rsqrt(jnp.mean(xf * xf, axis=-1, keepdims=True) + EPS)
                o_ref[:, hh * HEAD_DIM:(hh + 1) * HEAD_DIM] = ((xf * r) * g_ref[...]).astype(BF16)

    windows = [pl.BlockSpec((tm, w), functools.partial(lambda i, blk: (i, blk), blk=col0 // nheads)) for col0 in cols]
    return pl.pallas_call(
        body,
        grid=(s // tm,),
        in_specs=windows + [pl.BlockSpec((1, HEAD_DIM), lambda i: (0, 0))] * n,
        out_specs=[pl.BlockSpec((tm, w), lambda i: (i, 0))] * n,
        out_shape=[jax.ShapeDtypeStruct((s, w), BF16)] * n,
        compiler_params=_params("parallel"),
        name=name,
    )(*([src] * n), *gains)


def _headnorm_bwd(dxn, src, col0, nheads, g, name):
    s = src.shape[0]
    tm = _tile(s, HEADNORM_ROWS)
    w = nheads * HEAD_DIM
    assert col0 % nheads == 0

    def body(d_ref, x_ref, g_ref, dx_ref, dg_ref):
        part = jnp.zeros((1, HEAD_DIM), F32)
        for hh in range(nheads):
            xf = _head(x_ref, hh).astype(F32)
            r = lax.rsqrt(jnp.mean(xf * xf, axis=-1, keepdims=True) + EPS)
            xhat = xf * r
            dn = _head(d_ref, hh).astype(F32)
            dy = dn * g_ref[...]
            dx = r * (dy - xhat * jnp.mean(dy * xhat, axis=-1, keepdims=True))
            dx_ref[:, hh * HEAD_DIM:(hh + 1) * HEAD_DIM] = dx.astype(BF16)
            part = part + jnp.sum(dn * xhat, axis=0, keepdims=True)

        @pl.when(pl.program_id(0) == 0)
        def _():
            dg_ref[...] = part

        @pl.when(pl.program_id(0) > 0)
        def _():
            dg_ref[...] += part

    return pl.pallas_call(
        body,
        grid=(s // tm,),
        in_specs=[pl.BlockSpec((tm, w), lambda i: (i, 0)),
                  pl.BlockSpec((tm, w), lambda i: (i, col0 // nheads)),
                  pl.BlockSpec((1, HEAD_DIM), lambda i: (0, 0))],
        out_specs=[pl.BlockSpec((tm, w), lambda i: (i, 0)),
                   pl.BlockSpec((1, HEAD_DIM), lambda i: (0, 0))],
        out_shape=[jax.ShapeDtypeStruct((s, w), BF16), jax.ShapeDtypeStruct((1, HEAD_DIM), F32)],
        compiler_params=_params("arbitrary"),
        name=name,
    )(dxn, src, g)


def _tri(t, lower_inclusive):
    r = lax.broadcasted_iota(jnp.int32, (t, t), 0)
    c = lax.broadcasted_iota(jnp.int32, (t, t), 1)
    keep = (c <= r) if lower_inclusive else (c >= r)
    return jnp.where(keep, 1.0, 0.0).astype(BF16)


def _forget_fwd(h, w_f, b_pad):
    s, d = h.shape
    t = _tile(s, FORGET_ROWS)

    def body(h_ref, w_ref, b_ref, f_ref, c_ref, carry):
        @pl.when(pl.program_id(0) == 0)
        def _():
            carry[...] = jnp.zeros_like(carry)

        f = _dot(h_ref[...], w_ref[...])
        f_ref[...] = f
        lf = _log_sigmoid(f + b_ref[...])
        tri = _tri(t, True)
        acc = carry[...]
        for part in _split3(lf):
            acc = acc + _dot(tri, part)
        c_ref[...] = acc
        carry[...] += jnp.sum(lf, axis=0, keepdims=True)

    blk = pl.BlockSpec((t, LANES), lambda i: (i, 0))
    return pl.pallas_call(
        body,
        grid=(s // t,),
        in_specs=[pl.BlockSpec((t, d), lambda i: (i, 0)), pl.BlockSpec((d, LANES), lambda i: (0, 0)),
                  pl.BlockSpec((1, LANES), lambda i: (0, 0))],
        out_specs=[blk, blk],
        out_shape=[jax.ShapeDtypeStruct((s, LANES), F32)] * 2,
        scratch_shapes=[pltpu.VMEM((1, LANES), F32)],
        compiler_params=_params("arbitrary"),
        name="forget_fwd",
    )(h, w_f, b_pad)


def _forget_bwd(dc, f_logit, b_pad):
    s = f_logit.shape[0]
    t = _tile(s, FORGET_ROWS)
    nb = s // t

    def body(dc_ref, f_ref, b_ref, df_ref, db_ref, carry):
        @pl.when(pl.program_id(0) == 0)
        def _():
            carry[...] = jnp.zeros_like(carry)
            db_ref[...] = jnp.zeros_like(db_ref)

        d = dc_ref[...]
        tri = _tri(t, False)
        acc = carry[...]
        for part in _split3(d):
            acc = acc + _dot(tri, part)
        z = f_ref[...] + b_ref[...]
        df = acc * jnp.exp(_log_sigmoid(-z))
        df_ref[...] = df
        db_ref[...] += jnp.sum(df, axis=0, keepdims=True)
        carry[...] += jnp.sum(d, axis=0, keepdims=True)

    rev = pl.BlockSpec((t, LANES), lambda i: (nb - 1 - i, 0))
    vec = pl.BlockSpec((1, LANES), lambda i: (0, 0))
    return pl.pallas_call(
        body,
        grid=(nb,),
        in_specs=[rev, rev, vec],
        out_specs=[rev, vec],
        out_shape=[jax.ShapeDtypeStruct((s, LANES), F32), jax.ShapeDtypeStruct((1, LANES), F32)],
        scratch_shapes=[pltpu.VMEM((1, LANES), F32)],
        compiler_params=_params("arbitrary"),
        name="forget_bwd",
    )(dc, f_logit, b_pad)


SB_FWD_GROUP = 6
FOX_GROUP = 6
SB_BWD_GROUP = 6


def _head(ref, hh, rows=slice(None)):
    return ref[rows, hh * HEAD_DIM:(hh + 1) * HEAD_DIM]


def _tri_mask(t, strict):
    r = lax.broadcasted_iota(jnp.int32, (t, t), 0)
    c = lax.broadcasted_iota(jnp.int32, (t, t), 1)
    return (c < r) if strict else (c <= r)


def _fox_fwd(qn, kn, proj, colv, c_col, c_row, nheads, carry=None):
    s = qn.shape[0]
    t = _tile(s, ATT_TILE)
    scale = HEAD_DIM ** -0.5
    hg = FOX_GROUP
    gw = hg * HEAD_DIM
    assert nheads % hg == 0 and colv % hg == 0

    def body(q_ref, k_ref, v_ref, cc_ref, cr_ref, o_ref, of_ref, lse_ref):
        qi = pl.program_id(1)
        causal = _tri_mask(t, False)

        def tile(kj, carry, diagonal):
            off = pl.multiple_of(kj * t, t)
            heads = range(hg)
            rows = pl.ds(off, t)
            qk = [_dot_nt(_head(q_ref, hh), _head(k_ref, hh, rows)) for hh in heads]
            sc = [qk[hh] * scale + (cc_ref[hh] - cr_ref[hh, :, rows]) for hh in heads]
            if diagonal:
                sc = [jnp.where(causal, sc[hh], NEG_BIG) for hh in heads]
            m_new = [jnp.maximum(carry[hh][0], jnp.max(sc[hh], axis=-1, keepdims=True)) for hh in heads]
            p = [jnp.exp(sc[hh] - m_new[hh]) for hh in heads]
            pv = [_dot(p[hh].astype(BF16), _head(v_ref, hh, rows)) for hh in heads]
            out = []
            for hh in heads:
                m, l, acc = carry[hh]
                alpha = jnp.exp(m - m_new[hh])
                out.append((m_new[hh], alpha * l + jnp.sum(p[hh], axis=-1, keepdims=True), alpha * acc + pv[hh]))
            return tuple(out)

        init = tuple((jnp.full((t, 1), NEG_BIG, F32), jnp.zeros((t, 1), F32), jnp.zeros((t, HEAD_DIM), F32))
                     for _ in range(hg))
        carry = lax.fori_loop(0, qi, lambda kj, c: tile(kj, c, False), init)
        carry = tile(qi, carry, True)
        for hh in range(hg):
            m, l, acc = carry[hh]
            o = acc / l
            of_ref[:, hh * HEAD_DIM:(hh + 1) * HEAD_DIM] = o
            o_ref[:, hh * HEAD_DIM:(hh + 1) * HEAD_DIM] = o.astype(BF16)
            lse_ref[hh] = m + jnp.log(l)

    tile_spec = pl.BlockSpec((t, gw), lambda h, i: (i, h))
    w = nheads * HEAD_DIM
    return _call(
        body,
        grid=(nheads // hg, s // t),
        in_specs=[tile_spec,
                  pl.BlockSpec((s, gw), lambda h, i: (0, h), pipeline_mode=pl.Buffered(buffer_count=1)),
                  pl.BlockSpec((s, gw), lambda h, i: (0, colv // hg + h), pipeline_mode=pl.Buffered(buffer_count=1)),
                  pl.BlockSpec((hg, t, 1), lambda h, i: (h, i, 0)),
                  pl.BlockSpec((hg, 1, s), lambda h, i: (h, 0, 0))],
        out_specs=[tile_spec, tile_spec, pl.BlockSpec((hg, t, 1), lambda h, i: (h, i, 0))],
        out_shape=[jax.ShapeDtypeStruct((s, w), BF16), jax.ShapeDtypeStruct((s, w), F32),
                   jax.ShapeDtypeStruct((nheads, s, 1), F32)],
        scratch_shapes=[],
        semantics=("parallel", "parallel"),
        name="fox_fwd",
        args=(qn, kn, proj, c_col, c_row),
        carry=carry,
    )


def _fox_bwd(qn, kn, proj, colv, c_col, c_row, o, do, lse, nheads, carry=None):
    s = qn.shape[0]
    t = _tile(s, ATT_TILE)
    scale = HEAD_DIM ** -0.5
    hg = FOX_GROUP
    gw = hg * HEAD_DIM
    assert nheads % hg == 0 and colv % hg == 0

    def body(q_ref, k_ref, v_ref, cc_ref, cr_ref, o_ref, do_ref, lse_ref,
             dq_ref, dk_ref, dv_ref, drs_ref, dcs_ref):
        qi = pl.program_id(1)

        @pl.when(qi == 0)
        def _():
            dk_ref[...] = jnp.zeros_like(dk_ref)
            dv_ref[...] = jnp.zeros_like(dv_ref)
            dcs_ref[...] = jnp.zeros_like(dcs_ref)

        causal = _tri_mask(t, False)
        delta = [jnp.sum(_head(o_ref, hh) * _head(do_ref, hh).astype(F32), axis=-1, keepdims=True)
                 for hh in range(hg)]

        def tile(kj, carry, diagonal):
            off = pl.multiple_of(kj * t, t)
            heads = range(hg)
            rows = pl.ds(off, t)
            qk = [_dot_nt(_head(q_ref, hh), _head(k_ref, hh, rows)) for hh in heads]
            dp = [_dot_nt(_head(do_ref, hh), _head(v_ref, hh, rows)) for hh in heads]
            p = [jnp.exp(qk[hh] * scale + (cc_ref[hh] - cr_ref[hh, :, rows]) - lse_ref[hh]) for hh in heads]
            if diagonal:
                p = [jnp.where(causal, p[hh], 0.0) for hh in heads]
            ds = [p[hh] * (dp[hh] - delta[hh]) for hh in heads]
            dsb = [ds[hh].astype(BF16) for hh in heads]
            dv = [_dot_tn(p[hh].astype(BF16), _head(do_ref, hh)) for hh in heads]
            dk = [_dot_tn(dsb[hh], _head(q_ref, hh)) * scale for hh in heads]
            dq = [_dot(dsb[hh], _head(k_ref, hh, rows)) * scale for hh in heads]
            for hh in heads:
                cols = slice(hh * HEAD_DIM, (hh + 1) * HEAD_DIM)
                dv_ref[rows, cols] += dv[hh]
                dk_ref[rows, cols] += dk[hh]
                dcs_ref[hh, :, rows] += jnp.sum(ds[hh], axis=0, keepdims=True)
            return tuple((carry[hh][0] + dq[hh], carry[hh][1] + jnp.sum(ds[hh], axis=-1, keepdims=True))
                         for hh in heads)

        init = tuple((jnp.zeros((t, HEAD_DIM), F32), jnp.zeros((t, 1), F32)) for _ in range(hg))
        carry = lax.fori_loop(0, qi, lambda kj, c: tile(kj, c, False), init)
        carry = tile(qi, carry, True)
        for hh in range(hg):
            dq_ref[:, hh * HEAD_DIM:(hh + 1) * HEAD_DIM] = carry[hh][0]
            drs_ref[hh] = carry[hh][1]

    tile_spec = pl.BlockSpec((t, gw), lambda h, i: (i, h))
    once = pl.Buffered(buffer_count=1)
    full = pl.BlockSpec((s, gw), lambda h, i: (0, h), pipeline_mode=once)
    colspec = pl.BlockSpec((hg, t, 1), lambda h, i: (h, i, 0))
    rowspec = pl.BlockSpec((hg, 1, s), lambda h, i: (h, 0, 0))
    w = nheads * HEAD_DIM
    return _call(
        body,
        grid=(nheads // hg, s // t),
        in_specs=[tile_spec, full, pl.BlockSpec((s, gw), lambda h, i: (0, colv // hg + h), pipeline_mode=once),
                  colspec, rowspec,
                  tile_spec, tile_spec, colspec],
        out_specs=[tile_spec, full, full, colspec, rowspec],
        out_shape=[jax.ShapeDtypeStruct((s, w), F32), jax.ShapeDtypeStruct((s, w), F32),
                   jax.ShapeDtypeStruct((s, w), F32), jax.ShapeDtypeStruct((nheads, s, 1), F32),
                   jax.ShapeDtypeStruct((nheads, 1, s), F32)],
        scratch_shapes=[],
        semantics=("arbitrary", "arbitrary"),
        name="fox_bwd",
        args=(qn, kn, proj, c_col, c_row, o, do, lse),
        carry=carry,
    )


def _sb_tile(q, k, scale, later, valid):
    z = _dot_nt(q, k) * scale
    lb = _log_sigmoid(z)
    lm = lb - z
    if valid is not None:
        lm = jnp.where(valid, lm, 0.0)
    suffix = _dot(jnp.concatenate(_split2(lm), axis=1), later)
    return lb, lm, suffix


def _later(t):
    r = lax.broadcasted_iota(jnp.int32, (2 * t, t), 0) % t
    c = lax.broadcasted_iota(jnp.int32, (2 * t, t), 1)
    return jnp.where(r > c, 1.0, 0.0).astype(BF16)


def _sb_fwd(proj, colq, colk, colv, nheads, carry=None):
    s = proj.shape[0]
    t = _tile(s, ATT_TILE)
    scale = HEAD_DIM ** -0.5
    hg = SB_FWD_GROUP
    gw = hg * HEAD_DIM
    assert nheads % hg == 0 and colq % hg == 0 and colk % hg == 0 and colv % hg == 0

    def body(q_ref, k_ref, v_ref, o_ref, tot_ref):
        qi = pl.program_id(1)
        later = _later(t)
        before = _tri_mask(t, True)

        def tile(kj, carry, diagonal):
            off = pl.multiple_of(kj * t, t)
            heads = range(hg)
            z = [_dot_nt(_head(q_ref, hh), _head(k_ref, hh, pl.ds(off, t))) * scale for hh in heads]
            lb = [_log_sigmoid(z[hh]) for hh in heads]
            lm = [lb[hh] - z[hh] for hh in heads]
            if diagonal:
                lm = [jnp.where(before, lm[hh], 0.0) for hh in heads]
            parts = [jnp.concatenate(_split2(lm[hh]), axis=1) for hh in heads]
            suffix = [_dot(parts[hh], later) for hh in heads]
            a = [jnp.exp(lb[hh] + suffix[hh] + carry[hh][0]) for hh in heads]
            if diagonal:
                a = [jnp.where(before, a[hh], 0.0) for hh in heads]
            av = [_dot(a[hh].astype(BF16), _head(v_ref, hh, pl.ds(off, t))) for hh in heads]
            return tuple((carry[hh][0] + jnp.sum(lm[hh], axis=-1, keepdims=True), carry[hh][1] + av[hh])
                         for hh in heads)

        init = tuple((jnp.zeros((t, 1), F32), jnp.zeros((t, HEAD_DIM), F32)) for _ in range(hg))
        carry = tile(qi, init, True)
        carry = lax.fori_loop(1, qi + 1, lambda i, c: tile(qi - i, c, False), carry)
        for hh in range(hg):
            o_ref[:, hh * HEAD_DIM:(hh + 1) * HEAD_DIM] = carry[hh][1].astype(BF16)
            tot_ref[hh] = carry[hh][0]

    return _call(
        body,
        grid=(nheads // hg, s // t),
        in_specs=[pl.BlockSpec((t, gw), lambda h, i: (i, colq // hg + h)),
                  pl.BlockSpec((s, gw), lambda h, i: (0, colk // hg + h), pipeline_mode=pl.Buffered(buffer_count=1)),
                  pl.BlockSpec((s, gw), lambda h, i: (0, colv // hg + h), pipeline_mode=pl.Buffered(buffer_count=1))],
        out_specs=[pl.BlockSpec((t, gw), lambda h, i: (i, h)), pl.BlockSpec((hg, t, 1), lambda h, i: (h, i, 0))],
        out_shape=[jax.ShapeDtypeStruct((s, nheads * HEAD_DIM), BF16), jax.ShapeDtypeStruct((nheads, s, 1), F32)],
        scratch_shapes=[],
        semantics=("parallel", "parallel"),
        name="sb_fwd",
        args=(proj, proj, proj),
        carry=carry,
    )


def _sb_bwd(proj, colq, colk, colv, do, tot, nheads, carry=None):
    s = proj.shape[0]
    t = _tile(s, ATT_TILE)
    scale = HEAD_DIM ** -0.5
    hg = SB_BWD_GROUP
    gw = hg * HEAD_DIM
    assert nheads % hg == 0 and colq % hg == 0 and colk % hg == 0 and colv % hg == 0

    def body(q_ref, k_ref, v_ref, do_ref, tot_ref, dq_ref, dk_ref, dv_ref):
        qi = pl.program_id(1)

        @pl.when(qi == 0)
        def _():
            dk_ref[...] = jnp.zeros_like(dk_ref)
            dv_ref[...] = jnp.zeros_like(dv_ref)

        key = lax.broadcasted_iota(jnp.int32, (2 * t, t), 0) % t
        col = lax.broadcasted_iota(jnp.int32, (2 * t, t), 1)
        upto = jnp.where(key <= col, 1.0, 0.0).astype(BF16)
        earlier = jnp.where(key < col, 1.0, 0.0).astype(BF16)
        before = _tri_mask(t, True)

        def tile(kj, carry, diagonal):
            off = pl.multiple_of(kj * t, t)
            heads = range(hg)
            rows = pl.ds(off, t)
            z = [_dot_nt(_head(q_ref, hh), _head(k_ref, hh, rows)) * scale for hh in heads]
            da = [_dot_nt(_head(do_ref, hh), _head(v_ref, hh, rows)) for hh in heads]
            lb = [_log_sigmoid(z[hh]) for hh in heads]
            lm = [lb[hh] - z[hh] for hh in heads]
            if diagonal:
                lm = [jnp.where(before, lm[hh], 0.0) for hh in heads]
            parts = [jnp.concatenate(_split2(lm[hh]), axis=1) for hh in heads]
            seen = [_dot(parts[hh], upto) + carry[hh][0] for hh in heads]
            a = [jnp.exp(lb[hh] + (tot_ref[hh] - seen[hh])) for hh in heads]
            if diagonal:
                a = [jnp.where(before, a[hh], 0.0) for hh in heads]
            g = [a[hh] * da[hh] for hh in heads]
            gparts = [jnp.concatenate(_split2(g[hh]), axis=1) for hh in heads]
            gsum = [_dot(gparts[hh], earlier) + carry[hh][1] for hh in heads]
            dz = []
            for hh in heads:
                beta = jnp.exp(lb[hh])
                d = g[hh] * (1.0 - beta) - gsum[hh] * beta
                if diagonal:
                    d = jnp.where(before, d, 0.0)
                dz.append(d.astype(BF16))
            dv = [_dot_tn(a[hh].astype(BF16), _head(do_ref, hh)) for hh in heads]
            dk = [_dot_tn(dz[hh], _head(q_ref, hh)) * scale for hh in heads]
            dq = [_dot(dz[hh], _head(k_ref, hh, rows)) * scale for hh in heads]
            for hh in heads:
                cols = slice(hh * HEAD_DIM, (hh + 1) * HEAD_DIM)
                dv_ref[rows, cols] += dv[hh]
                dk_ref[rows, cols] += dk[hh]
            return tuple((carry[hh][0] + jnp.sum(lm[hh], axis=-1, keepdims=True),
                          carry[hh][1] + jnp.sum(g[hh], axis=-1, keepdims=True), carry[hh][2] + dq[hh])
                         for hh in heads)

        init = tuple((jnp.zeros((t, 1), F32), jnp.zeros((t, 1), F32), jnp.zeros((t, HEAD_DIM), F32))
                     for _ in range(hg))
        carry = lax.fori_loop(0, qi, lambda kj, c: tile(kj, c, False), init)
        carry = tile(qi, carry, True)
        for hh in range(hg):
            dq_ref[:, hh * HEAD_DIM:(hh + 1) * HEAD_DIM] = carry[hh][2]

    once = pl.Buffered(buffer_count=1)
    tile_spec = pl.BlockSpec((t, gw), lambda h, i: (i, h))
    full = pl.BlockSpec((s, gw), lambda h, i: (0, h), pipeline_mode=once)
    w = nheads * HEAD_DIM
    return _call(
        body,
        grid=(nheads // hg, s // t),
        in_specs=[pl.BlockSpec((t, gw), lambda h, i: (i, colq // hg + h)),
                  pl.BlockSpec((s, gw), lambda h, i: (0, colk // hg + h), pipeline_mode=once),
                  pl.BlockSpec((s, gw), lambda h, i: (0, colv // hg + h), pipeline_mode=once),
                  tile_spec, pl.BlockSpec((hg, t, 1), lambda h, i: (h, i, 0))],
        out_specs=[tile_spec, full, full],
        out_shape=[jax.ShapeDtypeStruct((s, w), F32)] * 3,
        scratch_shapes=[],
        semantics=("arbitrary", "arbitrary"),
        name="sb_bwd",
        args=(proj, proj, proj, do, tot),
        carry=carry,
    )


def _mem_fwd(qn, kn, mkv, nheads):
    s = qn.shape[0]
    mtok = kn.shape[0]
    t = _tile(s, MEM_ROWS)
    w = nheads * HEAD_DIM
    scale = HEAD_DIM ** -0.5

    def body(q_ref, k_ref, v_ref, o_ref):
        heads = range(nheads)
        sc = [_dot_nt(_head(q_ref, hh), _head(k_ref, hh)) * scale for hh in heads]
        p = [jnp.exp(sc[hh] - jnp.max(sc[hh], axis=-1, keepdims=True)) for hh in heads]
        p = [p[hh] / jnp.sum(p[hh], axis=-1, keepdims=True) for hh in heads]
        o = [_dot(p[hh].astype(BF16), _head(v_ref, hh)) for hh in heads]
        for hh in heads:
            o_ref[:, hh * HEAD_DIM:(hh + 1) * HEAD_DIM] = o[hh].astype(BF16)

    return pl.pallas_call(
        body,
        grid=(s // t,),
        in_specs=[pl.BlockSpec((t, w), lambda i: (i, 0)),
                  pl.BlockSpec((mtok, w), lambda i: (0, 0)),
                  pl.BlockSpec((mtok, w), lambda i: (0, 1))],
        out_specs=pl.BlockSpec((t, w), lambda i: (i, 0)),
        out_shape=jax.ShapeDtypeStruct((s, w), BF16),
        compiler_params=_params("parallel"),
        name="mem_fwd",
    )(qn, kn, mkv)


def _mem_bwd(qn, kn, mkv, do, nheads):
    s = qn.shape[0]
    mtok = kn.shape[0]
    t = _tile(s, MEM_ROWS)
    w = nheads * HEAD_DIM
    scale = HEAD_DIM ** -0.5

    def body(q_ref, k_ref, v_ref, do_ref, dq_ref, dk_ref, dv_ref):
        @pl.when(pl.program_id(0) == 0)
        def _():
            dk_ref[...] = jnp.zeros_like(dk_ref)
            dv_ref[...] = jnp.zeros_like(dv_ref)

        heads = range(nheads)
        sc = [_dot_nt(_head(q_ref, hh), _head(k_ref, hh)) * scale for hh in heads]
        dp = [_dot_nt(_head(do_ref, hh), _head(v_ref, hh)) for hh in heads]
        p = [jnp.exp(sc[hh] - jnp.max(sc[hh], axis=-1, keepdims=True)) for hh in heads]
        p = [p[hh] / jnp.sum(p[hh], axis=-1, keepdims=True) for hh in heads]
        ds = [(p[hh] * (dp[hh] - jnp.sum(p[hh] * dp[hh], axis=-1, keepdims=True))).astype(BF16) for hh in heads]
        dq = [_dot(ds[hh], _head(k_ref, hh)) * scale for hh in heads]
        dk = [_dot_tn(ds[hh], _head(q_ref, hh)) * scale for hh in heads]
        dv = [_dot_tn(p[hh].astype(BF16), _head(do_ref, hh)) for hh in heads]
        for hh in heads:
            cols = slice(hh * HEAD_DIM, (hh + 1) * HEAD_DIM)
            dq_ref[:, cols] = dq[hh]
            dk_ref[:, cols] += dk[hh]
            dv_ref[:, cols] += dv[hh]

    tile = pl.BlockSpec((t, w), lambda i: (i, 0))
    kspec = pl.BlockSpec((mtok, w), lambda i: (0, 0))
    return pl.pallas_call(
        body,
        grid=(s // t,),
        in_specs=[tile, kspec, pl.BlockSpec((mtok, w), lambda i: (0, 1)), tile],
        out_specs=[tile, kspec, kspec],
        out_shape=[jax.ShapeDtypeStruct((s, w), F32), jax.ShapeDtypeStruct((mtok, w), F32),
                   jax.ShapeDtypeStruct((mtok, w), F32)],
        compiler_params=_params("arbitrary"),
        name="mem_bwd",
    )(qn, kn, mkv, do)


def _merge_fwd(p0, p1, p2, gates, b_gate):
    s, d = p0.shape
    tm, tn = _tile(s, ROW_TILE), _tile(d, COL_TILE)
    nj = d // tn

    def body(p0_ref, p1_ref, p2_ref, ga_ref, gb_ref, gc_ref, b_ref, o_ref, sa_ref, sb_ref, sc_ref):
        acc = jnp.zeros((tm, tn), F32)
        for b, (p_ref, g_ref, s_ref) in enumerate(((p0_ref, ga_ref, sa_ref), (p1_ref, gb_ref, sb_ref),
                                                   (p2_ref, gc_ref, sc_ref))):
            gate = jax.nn.sigmoid(g_ref[...].astype(F32) + b_ref[b:b + 1, :])
            s_ref[...] = gate.astype(BF16)
            acc = acc + gate * p_ref[...].astype(F32)
        o_ref[...] = acc.astype(BF16)

    blk = pl.BlockSpec((tm, tn), lambda i, j: (i, j))
    merged, *sig = pl.pallas_call(
        body,
        grid=(s // tm, nj),
        in_specs=[blk] * 6 + [pl.BlockSpec((3, tn), lambda i, j: (0, j))],
        out_specs=[blk] * 4,
        out_shape=[jax.ShapeDtypeStruct((s, d), BF16)] * 4,
        compiler_params=_params("parallel", "parallel"),
        name="merge_fwd",
    )(p0, p1, p2, *gates, b_gate)
    return merged, tuple(sig)


def _merge_bwd(dmerged, p0, p1, p2, sig):
    s, d = p0.shape
    tm, tn = _tile(s, ROW_TILE), _tile(d, COL_TILE)
    nj = d // tn

    def body(dm_ref, p0_ref, p1_ref, p2_ref, ga_ref, gb_ref, gc_ref,
             d0_ref, d1_ref, d2_ref, dga_ref, dgb_ref, dgc_ref, db_ref):
        dm = dm_ref[...].astype(F32)
        parts = []
        for p_ref, g_ref, dp_ref, dg_ref in ((p0_ref, ga_ref, d0_ref, dga_ref), (p1_ref, gb_ref, d1_ref, dgb_ref),
                                             (p2_ref, gc_ref, d2_ref, dgc_ref)):
            gate = g_ref[...].astype(F32)
            dp_ref[...] = (dm * gate).astype(BF16)
            dgate = dm * p_ref[...].astype(F32) * gate * (1.0 - gate)
            dg_ref[...] = dgate.astype(BF16)
            parts.append(jnp.sum(dgate, axis=0, keepdims=True))
        part = jnp.concatenate(parts, axis=0)

        @pl.when(pl.program_id(1) == 0)
        def _():
            db_ref[...] = part

        @pl.when(pl.program_id(1) > 0)
        def _():
            db_ref[...] += part

    blk = pl.BlockSpec((tm, tn), lambda j, i: (i, j))
    bias = pl.BlockSpec((3, tn), lambda j, i: (0, j))
    return pl.pallas_call(
        body,
        grid=(nj, s // tm),
        in_specs=[blk] * 7,
        out_specs=[blk] * 6 + [bias],
        out_shape=[jax.ShapeDtypeStruct((s, d), BF16)] * 6 + [jax.ShapeDtypeStruct((3, d), F32)],
        compiler_params=_params("parallel", "arbitrary"),
        name="merge_bwd",
    )(dmerged, p0, p1, p2, *sig)


def _shift_down(v, n):
    rows = lax.broadcasted_iota(jnp.int32, v.shape, 0)
    return jnp.where(rows >= n, pltpu.roll(v, n, 0), 0.0)


def _shift_up(v, n):
    s = v.shape[0]
    rows = lax.broadcasted_iota(jnp.int32, v.shape, 0)
    return jnp.where(rows < s - n, pltpu.roll(v, s - n, 0), 0.0)


def _conv(v, w_ref, b_ref):
    taps = w_ref.shape[0]
    out = v * w_ref[taps - 1:taps, :] + b_ref[...]
    for n in range(1, taps):
        out = out + _shift_down(v, n) * w_ref[taps - 1 - n:taps - n, :]
    return out


def _conv_act_fwd(up, conv_w, conv_b):
    s, f2 = up.shape
    f = f2 // 2
    tn = LANES
    nj = f // tn
    taps = conv_w.shape[0]

    def body(ug_ref, uv_ref, wg_ref, wv_ref, bg_ref, bv_ref, o_ref, cg_ref, cv_ref):
        cg = _conv(ug_ref[...].astype(F32), wg_ref, bg_ref)
        cv = _conv(uv_ref[...].astype(F32), wv_ref, bv_ref)
        o_ref[...] = (cg * jax.nn.sigmoid(cg) * cv).astype(BF16)
        cg_ref[...] = cg.astype(BF16)
        cv_ref[...] = cv.astype(BF16)

    out = pl.BlockSpec((s, tn), lambda j: (0, j))
    return pl.pallas_call(
        body,
        grid=(nj,),
        in_specs=[pl.BlockSpec((s, tn), lambda j: (0, j)), pl.BlockSpec((s, tn), lambda j: (0, nj + j)),
                  pl.BlockSpec((taps, tn), lambda j: (0, j)), pl.BlockSpec((taps, tn), lambda j: (0, nj + j)),
                  pl.BlockSpec((1, tn), lambda j: (0, j)), pl.BlockSpec((1, tn), lambda j: (0, nj + j))],
        out_specs=[out, out, out],
        out_shape=[jax.ShapeDtypeStruct((s, f), BF16)] * 3,
        compiler_params=_params("parallel"),
        name="conv_act_fwd",
    )(up, up, conv_w, conv_w, conv_b, conv_b)


def _conv_act_bwd(up, conv_w, conv_g, conv_v, dact):
    s, f2 = up.shape
    f = f2 // 2
    tn = LANES
    nj = f // tn
    taps = conv_w.shape[0]

    def half(v, du, w_ref, dup_ref, dw_ref, db_ref):
        dup = du * w_ref[taps - 1:taps, :]
        rows = [None] * taps
        rows[taps - 1] = jnp.sum(du * v, axis=0, keepdims=True)
        for n in range(1, taps):
            later = _shift_up(du, n)
            dup = dup + later * w_ref[taps - 1 - n:taps - n, :]
            rows[taps - 1 - n] = jnp.sum(later * v, axis=0, keepdims=True)
        dup_ref[...] = dup.astype(BF16)
        dw_ref[...] = jnp.concatenate(rows, axis=0)
        db_ref[...] = jnp.sum(du, axis=0, keepdims=True)

    def body(ug_ref, uv_ref, wg_ref, wv_ref, cg_ref, cv_ref, da_ref,
             dug_ref, duv_ref, dwg_ref, dwv_ref, dbg_ref, dbv_ref):
        cg = cg_ref[...].astype(F32)
        cv = cv_ref[...].astype(F32)
        da = da_ref[...].astype(F32)
        sg = jax.nn.sigmoid(cg)
        dcv = da * cg * sg
        dcg = da * cv * (sg + cg * sg * (1.0 - sg))
        half(ug_ref[...].astype(F32), dcg, wg_ref, dug_ref, dwg_ref, dbg_ref)
        half(uv_ref[...].astype(F32), dcv, wv_ref, duv_ref, dwv_ref, dbv_ref)

    lo = lambda rows: pl.BlockSpec((rows, tn), lambda j: (0, j))
    hi = lambda rows: pl.BlockSpec((rows, tn), lambda j: (0, nj + j))
    return pl.pallas_call(
        body,
        grid=(nj,),
        in_specs=[lo(s), hi(s), lo(taps), hi(taps), lo(s), lo(s), lo(s)],
        out_specs=[lo(s), lo(s), lo(taps), lo(taps), lo(1), lo(1)],
        out_shape=[jax.ShapeDtypeStruct((s, f), BF16)] * 2 + [jax.ShapeDtypeStruct((taps, f), F32)] * 2
        + [jax.ShapeDtypeStruct((1, f), F32)] * 2,
        compiler_params=_params("parallel"),
        name="conv_act_bwd",
    )(up, up, conv_w, conv_w, conv_g, conv_v, dact)


def _row_tile(rows, row_bytes, budget):
    if rows * row_bytes <= budget or rows % 8:
        return rows
    best = 8
    for t in range(8, rows, 8):
        if rows % t == 0 and t * row_bytes <= budget:
            best = t
    return best


def _adamw(w, g, m, v, name):
    r, c = w.shape
    tr = _row_tile(r, c * 4, ADAM_BLOCK_BYTES)

    def body(w_ref, g_ref, m_ref, v_ref, d_ref, mo_ref, vo_ref):
        gg = g_ref[...]
        m_new = ADAM_B1 * m_ref[...] + (1.0 - ADAM_B1) * gg
        v_new = ADAM_B2 * v_ref[...] + (1.0 - ADAM_B2) * (gg * gg)
        m_hat = m_new / (1.0 - ADAM_B1 ** ADAM_STEP)
        v_hat = v_new / (1.0 - ADAM_B2 ** ADAM_STEP)
        d_ref[...] = -ADAM_LR * (m_hat / (jnp.sqrt(v_hat) + ADAM_EPS) + ADAM_WD * w_ref[...])
        mo_ref[...] = m_new
        vo_ref[...] = v_new

    blk = pl.BlockSpec((tr, c), lambda i: (i, 0))
    return pl.pallas_call(
        body,
        grid=(r // tr,),
        in_specs=[blk] * 4,
        out_specs=[blk] * 3,
        out_shape=[jax.ShapeDtypeStruct((r, c), F32)] * 3,
        compiler_params=_params("parallel"),
        name=name,
    )(w, g, m, v)


def _add_sibling(g, r1, core, name):
    _, _, h, c = g.shape
    th = _row_tile(h, c * 2, ADAM_BLOCK_BYTES)

    def body(core_ref, g_ref, r_ref, o_ref):
        o_ref[...] = (g_ref[...].astype(F32) + r_ref[...].astype(F32)).astype(BF16)

    return pl.pallas_call(
        body,
        grid_spec=pltpu.PrefetchScalarGridSpec(
            num_scalar_prefetch=1,
            grid=(N_CHIPS, h // th),
            in_specs=[pl.BlockSpec((None, None, th, c), lambda j, i, core_ref: (j, core_ref[0], i, 0)),
                      pl.BlockSpec((None, th, c), lambda j, i, core_ref: (j, i, 0))],
            out_specs=pl.BlockSpec((None, th, c), lambda j, i, core_ref: (j, i, 0)),
        ),
        out_shape=jax.ShapeDtypeStruct((N_CHIPS, h, c), BF16),
        compiler_params=_params("parallel", "parallel"),
        name=name,
    )(core, g, r1)


def _add_chips(hsum, r2, chip_core, name):
    _, h, c = hsum.shape
    th = _row_tile(h, c * 4, ADAM_BLOCK_BYTES)

    def body(sel_ref, own_ref, r_ref, o_ref):
        acc = own_ref[...].astype(F32)
        for j in range(N_CHIPS - 1):
            acc = acc + r_ref[j].astype(F32)
        o_ref[...] = acc

    return pl.pallas_call(
        body,
        grid_spec=pltpu.PrefetchScalarGridSpec(
            num_scalar_prefetch=1,
            grid=(h // th,),
            in_specs=[pl.BlockSpec((None, th, c), lambda i, sel_ref: (sel_ref[0], i, 0)),
                      pl.BlockSpec((N_CHIPS - 1, th, c), lambda i, sel_ref: (0, i, 0))],
            out_specs=pl.BlockSpec((None, th, c), lambda i, sel_ref: (sel_ref[1], i, 0)),
        ),
        out_shape=jax.ShapeDtypeStruct((2, h, c), F32),
        compiler_params=_params("parallel"),
        name=name,
    )(chip_core, hsum, r2)


def _sum_devices(parts):
    _, r, c = parts.shape

    def body(p_ref, o_ref):
        acc = p_ref[0]
        for j in range(1, N_DEV):
            acc = acc + p_ref[j]
        o_ref[...] = acc

    return pl.pallas_call(
        body,
        out_shape=jax.ShapeDtypeStruct((r, c), F32),
        compiler_params=pltpu.CompilerParams(vmem_limit_bytes=VMEM_LIMIT_BYTES),
        name="sum_devices",
    )(parts)


def _gather_small(vec):
    k = N_DEV - 1

    def body(v_ref, o_ref, send_sems, recv_sems, local_sem):
        x, y, c, _ = _place()
        me = 4 * x + 2 * y + c
        local = pltpu.make_async_copy(v_ref, o_ref.at[me], local_sem)
        local.start()
        peers = [(x ^ (r >> 2 & 1), y ^ (r >> 1 & 1), c ^ (r & 1)) for r in range(1, N_DEV)]
        sends = [_remote(v_ref, o_ref.at[me], send_sems.at[j], recv_sems.at[j], p) for j, p in enumerate(peers)]
        for cp in sends:
            cp.start()
        for j, (px, py, pc) in enumerate(peers):
            sends[j].wait_send()
            blk = o_ref.at[4 * px + 2 * py + pc]
            _remote(blk, blk, send_sems.at[j], recv_sems.at[j], (px, py, pc)).wait_recv()
        local.wait()

    return pl.pallas_call(
        body,
        in_specs=[ANY],
        out_specs=ANY,
        out_shape=jax.ShapeDtypeStruct((N_DEV,) + vec.shape, vec.dtype),
        scratch_shapes=[pltpu.SemaphoreType.DMA((k,)), pltpu.SemaphoreType.DMA((k,)), pltpu.SemaphoreType.DMA(())],
        name="gather_small",
    )(vec)


W_IN_GATES = ("w_in_g0", "w_in_g1", "w_in_g2")
ROW_SHARDED = ("w_in_a", "w_in_f") + W_IN_GATES + ("w_mem_kv", "w_out", "w_down")
COL_SHARDED = ("w_br_fox", "w_br_sb", "w_br_mem", "w_up")
BIG = ROW_SHARDED + COL_SHARDED


def _whole(name, a):
    if name in ROW_SHARDED:
        return a.reshape(N_CHIPS * a.shape[1], a.shape[2])
    return a.transpose(1, 0, 2).reshape(a.shape[1], N_CHIPS * a.shape[2])


def _by_shard(name, grad):
    if name in ROW_SHARDED:
        a = grad.reshape(N_CHIPS, grad.shape[0] // N_CHIPS, grad.shape[1])
    else:
        a = grad.reshape(grad.shape[0], N_CHIPS, grad.shape[1] // N_CHIPS).transpose(1, 0, 2)
    return a.reshape(N_CHIPS, 2, a.shape[1] // 2, a.shape[2])


def _sibling_sums(names, split, theirs, core):
    return [_add_sibling(a, r, core, "add_sibling_" + name) for name, a, r in zip(names, split, theirs)]


GATHER_FIRST = ("w_in_a", "w_in_f")
GATHER_EARLY = W_IN_GATES[:2]
GATHER_MIX = ("w_out", "w_br_fox", "w_br_sb", "w_br_mem") + W_IN_GATES[2:]
REDUCE_FFN = ("w_down", "w_up")
REDUCE_MIX = ("w_out", "w_br_fox", "w_br_sb", "w_br_mem", "w_mem_kv") + W_IN_GATES
REDUCE_IN = ("w_in_a", "w_in_f")


def _local_step(x, mem, target, w, shard, core, chip_core):
    d = x.shape[1]
    nf = shard["w_br_fox"].shape[0] // HEAD_DIM
    nsb = shard["w_br_sb"].shape[0] // HEAD_DIM
    nm = shard["w_br_mem"].shape[0] // HEAD_DIM
    w = dict(w)

    def take(names, gathered):
        for name, a in zip(names, gathered):
            w[name] = _whole(name, a)

    fq, fk, fv = 0, nf, 2 * nf
    sq, sk, sv = 3 * nf, 3 * nf + nsb, 3 * nf + 2 * nsb
    mq = 3 * nf + 3 * nsb

    h, rstd1, moved = _rms_fwd(x, w["g_mix"], "rms_mix_fwd", carry=_Gather([shard[name] for name in GATHER_FIRST]))
    take(GATHER_FIRST, moved)
    proj, moved = _mm(h, w["w_in_a"], "nn", BF16, "proj_att", carry=_Gather([shard[name] for name in GATHER_EARLY]))
    take(GATHER_EARLY, moved)
    gate0, moved = _mm(h, w["w_in_g0"], "nn", BF16, "proj_gate0", carry=_Gather([shard["w_mem_kv"]]))
    take(("w_mem_kv",), moved)
    gate1 = _mm(h, w["w_in_g1"], "nn", BF16, "proj_gate1")
    f_logit, c_sum = _forget_fwd(h, w["w_in_f"], w["b_forget"])
    c_t = c_sum[:, :nf].T
    c_col, c_row = c_t[:, :, None], c_t[:, None, :]
    qn, kn = _headnorm_fwd(proj, (fq, fk), nf, (w["g_q_fox"], w["g_k_fox"]), "fox_qknorm_fwd")
    (o_fox, o_fox32, lse), moved = _fox_fwd(qn, kn, proj, fv, c_col, c_row, nf,
                                            carry=_Gather([shard[name] for name in GATHER_MIX]))
    take(GATHER_MIX, moved)
    gates = (gate0, gate1, _mm(h, w["w_in_g2"], "nn", BF16, "proj_gate2"))
    (o_sb, sb_tot), moved = _sb_fwd(proj, sq, sk, sv, nsb, carry=_Gather([shard["w_up"]]))
    take(("w_up",), moved)
    memn, rstd_m = _rms_fwd(mem, w["g_mem"], "rms_mem_fwd")
    mkv = _mm(memn, w["w_mem_kv"], "nn", BF16, "mem_kv")
    (kmn,) = _headnorm_fwd(mkv, (0,), nm, (w["g_k_mem"],), "mem_knorm_fwd")
    (qmn,) = _headnorm_fwd(proj, (mq,), nm, (w["g_q_mem"],), "mem_qnorm_fwd")
    o_mem = _mem_fwd(qmn, kmn, mkv, nm)
    p0 = _mm(o_fox, w["w_br_fox"], "nn", BF16, "branch_fox")
    p1 = _mm(o_sb, w["w_br_sb"], "nn", BF16, "branch_sb")
    p2 = _mm(o_mem, w["w_br_mem"], "nn", BF16, "branch_mem")
    merged, sig = _merge_fwd(p0, p1, p2, gates, w["b_gate"])
    x1 = _mm(merged, w["w_out"], "nn", F32, "out_proj", residual=x)
    h2, rstd2 = _rms_fwd(x1, w["g_ffn"], "rms_ffn_fwd")
    up, moved = _mm(h2, w["w_up"], "nn", BF16, "ffn_up", carry=_Gather([shard["w_down"]]))
    take(("w_down",), moved)
    act, conv_g, conv_v = _conv_act_fwd(up, w["conv_w"], w["conv_b"])
    dy, dyb, lparts = _mm(act, w["w_down"], "nn", F32, "ffn_down_loss", residual=x1, loss_target=target)
    loss = (0.5 / d) * jnp.sum(lparts[::8, ::LANES])

    g = {}
    dact = _mm(dyb, w["w_down"], "nt", BF16, "ffn_down_dx")
    g["w_down"] = _mm(act, dyb, "tn", BF16, "ffn_down_dw")
    dug, duv, dwg, dwv, dbg, dbv = _conv_act_bwd(up, w["conv_w"], conv_g, conv_v, dact)
    dup = jnp.concatenate([dug, duv], axis=1)
    g["conv_w"] = jnp.concatenate([dwg, dwv], axis=1)
    g["conv_b"] = jnp.concatenate([dbg, dbv], axis=1)
    split_down = [_by_shard("w_down", g["w_down"])]
    dh2, theirs_down = _mm(dup, w["w_up"], "nt", BF16, "ffn_up_dx", carry=_Swap(split_down))
    g["w_up"] = _mm(h2, dup, "tn", BF16, "ffn_up_dw")
    split_up = [_by_shard("w_up", g["w_up"])]
    dx1, dx1b, g["g_ffn"] = _rms_bwd(dh2, x1, rstd2, w["g_ffn"], dy, "rms_ffn_bwd")
    dmerged, theirs_up = _mm(dx1b, w["w_out"], "nt", BF16, "out_proj_dx", carry=_Swap(split_up))
    sums_ffn = _sibling_sums(REDUCE_FFN, split_down + split_up, theirs_down + theirs_up, core)
    g["w_out"] = _mm(merged, dx1b, "tn", BF16, "out_proj_dw")
    dp0, dp1, dp2, dga, dgb, dgc, g["b_gate"] = _merge_bwd(dmerged, p0, p1, p2, sig)
    dgates = (dga, dgb, dgc)
    for name, dgate in zip(W_IN_GATES, dgates):
        g[name] = _mm(h, dgate, "tn", BF16, name + "_dw")
    do_fox = _mm(dp0, w["w_br_fox"], "nt", BF16, "branch_fox_dx")
    do_sb = _mm(dp1, w["w_br_sb"], "nt", BF16, "branch_sb_dx")
    do_mem = _mm(dp2, w["w_br_mem"], "nt", BF16, "branch_mem_dx")
    g["w_br_fox"] = _mm(o_fox, dp0, "tn", BF16, "branch_fox_dw")
    g["w_br_sb"] = _mm(o_sb, dp1, "tn", BF16, "branch_sb_dw")
    g["w_br_mem"] = _mm(o_mem, dp2, "tn", BF16, "branch_mem_dw")
    dqmn, dkmn, dvm = _mem_bwd(qmn, kmn, mkv, do_mem, nm)
    dmq, g["g_q_mem"] = _headnorm_bwd(dqmn, proj, mq, nm, w["g_q_mem"], "mem_qnorm_bwd")
    dkm, g["g_k_mem"] = _headnorm_bwd(dkmn, mkv, 0, nm, w["g_k_mem"], "mem_knorm_bwd")
    dmkv = jnp.concatenate([dkm, dvm.astype(BF16)], axis=1)
    g["w_mem_kv"] = _mm(memn, dmkv, "tn", BF16, "mem_kv_dw")
    dmemn = _mm(dmkv, w["w_mem_kv"], "nt", BF16, "mem_kv_dx")
    _, _, g["g_mem"] = _rms_bwd(dmemn, mem, rstd_m, w["g_mem"], None, "rms_mem_bwd")
    split_mix =[_by_shard(name, g[name]) for name in REDUCE_MIX]

    (dqn, dkn, dfv, drs, dcs), moved = _fox_bwd(qn, kn, proj, fv, c_col, c_row, o_fox32, do_fox, lse, nf,
                                                carry=_Both(_Scatter(sums_ffn[1:]), _Swap(split_mix)))
    others_up, theirs_mix = moved[:1], moved[1:]
    sums_mix = _sibling_sums(REDUCE_MIX, split_mix, theirs_mix, core)
    dfq, g["g_q_fox"] = _headnorm_bwd(dqn, proj, fq, nf, w["g_q_fox"], "fox_qnorm_bwd")
    dfk, g["g_k_fox"] = _headnorm_bwd(dkn, proj, fk, nf, w["g_k_fox"], "fox_knorm_bwd")
    dc = jnp.pad((drs[:, :, 0] - dcs[:, 0, :]).T, ((0, 0), (0, LANES - nf)))
    df, g["b_forget"] = _forget_bwd(dc, f_logit, w["b_forget"])
    (dsq, dsk, dsv), others_mix = _sb_bwd(proj, sq, sk, sv, do_sb, sb_tot, nsb, carry=_Scatter(sums_ffn[:1] + sums_mix))
    others_ffn = others_mix[:1] + others_up
    others_mix = others_mix[1:]

    dproj = jnp.concatenate([dfq, dfk, dfv.astype(BF16), dsq.astype(BF16), dsk.astype(BF16), dsv.astype(BF16), dmq],
                            axis=1)
    dfb = df.astype(BF16)
    g["w_in_a"] = _mm(h, dproj, "tn", BF16, "proj_att_dw")
    g["w_in_f"] = _mm(h, dfb, "tn", BF16, "proj_forget_dw")
    split_in = [_by_shard(name, g[name]) for name in REDUCE_IN]
    dh, theirs_in = _mm(dgates[0], w[W_IN_GATES[0]], "nt", F32, W_IN_GATES[0] + "_dx", carry=_Swap(split_in))
    sums_in = _sibling_sums(REDUCE_IN, split_in, theirs_in, core)
    for name, dgate in zip(W_IN_GATES[1:], dgates[1:]):
        dh = _mm(dgate, w[name], "nt", F32, name + "_dx", residual=dh)
    dh, others_in = _mm(dproj, w["w_in_a"], "nt", F32, "proj_att_dx", residual=dh, carry=_Scatter(sums_in))
    names = REDUCE_FFN + REDUCE_MIX + REDUCE_IN
    finals = [_add_chips(own, theirs, chip_core, "add_chips_" + name)
              for name, own, theirs in zip(names, sums_ffn + sums_mix + sums_in, others_ffn + others_mix + others_in)]
    grad_x, _, g["g_mix"], joined = _rms_bwd(dh, x, rstd1, w["g_mix"], dx1, "rms_mix_bwd", more=(dfb, w["w_in_f"]),
                                             carry=_Join(finals))
    summed = {name: a.reshape(2 * a.shape[1], a.shape[2]) for name, a in zip(names, joined)}
    return loss, grad_x, g, summed


SMALL = ("g_mix", "b_forget", "g_q_fox", "g_k_fox", "g_mem", "g_q_mem", "g_k_mem", "b_gate", "g_ffn", "conv_w",
         "conv_b")
SMALL_SHARDED = ("b_gate", "conv_w")
PACK_ROWS = 8


def _pack(arrs):
    flat = jnp.concatenate([a.reshape(-1) for a in arrs])
    unit = PACK_ROWS * LANES
    flat = jnp.pad(flat, (0, -flat.shape[0] % unit))
    return flat.reshape(-1, LANES)


def _unpack(packed, shapes):
    flat = packed.reshape(-1)
    out, at = [], 0
    for s in shapes:
        n = 1
        for dim in s:
            n *= dim
        out.append(flat[at:at + n].reshape(s))
        at += n
    return out


def kernel(x, mem, g_mix, w_in, b_forget, g_q_fox, g_k_fox, g_mem, w_mem_kv, g_q_mem, g_k_mem, w_br_fox, w_br_sb, w_br_mem, b_gate, w_out, g_ffn, w_up, conv_w, conv_b, w_down, loss_target, m_g_mix, m_w_in, m_b_forget, m_g_q_fox, m_g_k_fox, m_g_mem, m_w_mem_kv, m_g_q_mem, m_g_k_mem, m_w_br_fox, m_w_br_sb, m_w_br_mem, m_b_gate, m_w_out, m_g_ffn, m_w_up, m_conv_w, m_conv_b, m_w_down, v_g_mix, v_w_in, v_b_forget, v_g_q_fox, v_g_k_fox, v_g_mem, v_w_mem_kv, v_g_q_mem, v_g_k_mem, v_w_br_fox, v_w_br_sb, v_w_br_mem, v_b_gate, v_w_out, v_g_ffn, v_w_up, v_conv_w, v_conv_b, v_w_down):
    given = dict(g_mix=g_mix, w_in=w_in, b_forget=b_forget, g_q_fox=g_q_fox, g_k_fox=g_k_fox, g_mem=g_mem,
                 w_mem_kv=w_mem_kv, g_q_mem=g_q_mem, g_k_mem=g_k_mem, w_br_fox=w_br_fox, w_br_sb=w_br_sb,
                 w_br_mem=w_br_mem, b_gate=b_gate, w_out=w_out, g_ffn=g_ffn, w_up=w_up, conv_w=conv_w, conv_b=conv_b,
                 w_down=w_down)
    m_in = dict(g_mix=m_g_mix, w_in=m_w_in, b_forget=m_b_forget, g_q_fox=m_g_q_fox, g_k_fox=m_g_k_fox, g_mem=m_g_mem,
                w_mem_kv=m_w_mem_kv, g_q_mem=m_g_q_mem, g_k_mem=m_g_k_mem, w_br_fox=m_w_br_fox, w_br_sb=m_w_br_sb,
                w_br_mem=m_w_br_mem, b_gate=m_b_gate, w_out=m_w_out, g_ffn=m_g_ffn, w_up=m_w_up, conv_w=m_conv_w,
                conv_b=m_conv_b, w_down=m_w_down)
    v_in = dict(g_mix=v_g_mix, w_in=v_w_in, b_forget=v_b_forget, g_q_fox=v_g_q_fox, g_k_fox=v_g_k_fox, g_mem=v_g_mem,
                w_mem_kv=v_w_mem_kv, g_q_mem=v_g_q_mem, g_k_mem=v_g_k_mem, w_br_fox=v_w_br_fox, w_br_sb=v_w_br_sb,
                w_br_mem=v_w_br_mem, b_gate=v_b_gate, w_out=v_w_out, g_ffn=v_g_ffn, w_up=v_w_up, conv_w=v_conv_w,
                conv_b=v_conv_b, w_down=v_w_down)
    layered = {k: a.ndim == 3 for k, a in given.items()}
    drop = lambda a: a[0] if a.ndim == 3 else a
    given = {k: drop(a) for k, a in given.items()}
    m_in = {k: drop(a) for k, a in m_in.items()}
    v_in = {k: drop(a) for k, a in v_in.items()}

    xi, yi, ci = lax.axis_index("x"), lax.axis_index("y"), lax.axis_index("c")
    chip = (2 * xi + yi).astype(jnp.int32)
    core_arr = ci.astype(jnp.int32).reshape(1)
    chip_core = jnp.stack([chip, ci.astype(jnp.int32)])

    nf = given["b_forget"].shape[1]
    cut = 3 * given["w_br_fox"].shape[0]

    d_model = given["w_out"].shape[1]
    gate0 = given["w_in"].shape[1] - len(W_IN_GATES) * d_model
    shard = {
        "w_in_a": jnp.concatenate([given["w_in"][:, :cut], given["w_in"][:, cut + nf:gate0]], axis=1).astype(BF16),
        "w_in_f": jnp.pad(given["w_in"][:, cut:cut + nf], ((0, 0), (0, LANES - nf))).astype(BF16),
    }
    for b, name in enumerate(W_IN_GATES):
        shard[name] = given["w_in"][:, gate0 + b * d_model:gate0 + (b + 1) * d_model].astype(BF16)
    for name in BIG:
        if name not in shard:
            shard[name] = given[name].astype(BF16)
    w = {}
    small_shapes = [given[name].shape for name in SMALL_SHARDED]
    small_parts = _gather_small(_pack([given[name] for name in SMALL_SHARDED]))[0::2]
    per_chip = [_unpack(small_parts[j], small_shapes) for j in range(N_CHIPS)]
    for k, name in enumerate(SMALL_SHARDED):
        w[name] = jnp.concatenate([per_chip[j][k] for j in range(N_CHIPS)], axis=1)
    for name in SMALL:
        if name not in SMALL_SHARDED:
            w[name] = given[name]
    w["b_forget"] = jnp.pad(given["b_forget"], ((0, 0), (0, LANES - nf)))

    loss, grad_x, g, summed = _local_step(x[0], mem[0], loss_target[0], w, shard, core_arr, chip_core)
    loss = lax.psum(loss, ("x", "y", "c"))
    grads = {name: summed[name] for name in BIG if name in given}
    grads["w_in"] = jnp.concatenate([summed["w_in_a"][:, :cut], summed["w_in_f"][:, :nf], summed["w_in_a"][:, cut:]]
                                    + [summed[name] for name in W_IN_GATES], axis=1)

    g["b_forget"] = g["b_forget"][:, :nf]
    small_full_shapes = [g[name].shape for name in SMALL]
    small_sum = _unpack(_sum_devices(_gather_small(_pack([g[name] for name in SMALL]))), small_full_shapes)
    for name, a in zip(SMALL, small_sum):
        if name in SMALL_SHARDED:
            width = given[name].shape[1]
            a = lax.dynamic_slice_in_dim(a, chip * width, width, axis=1)
        grads[name] = a

    delta, new_m, new_v = {}, {}, {}
    for name in WEIGHTS:
        if name not in SMALL:
            delta[name], new_m[name], new_v[name] = _adamw(given[name], grads[name], m_in[name], v_in[name],
                                                           "adamw_" + name)
    shapes = [given[name].shape for name in SMALL]
    packed = [_pack([src[name] for name in SMALL]) for src in (given, grads, m_in, v_in)]
    for dst, res in zip((delta, new_m, new_v), _adamw(*packed, "adamw_small")):
        for name, a in zip(SMALL, _unpack(res, shapes)):
            dst[name] = a

    out = [loss, grad_x[None]]
    for src in (grads, delta, new_m, new_v):
        out.extend(src[name][None] if layered[name] else src[name] for name in WEIGHTS)
    return tuple(out)
```
